```python
import math
import jax, jax.numpy as jnp
from jax import lax
import numpy as np

D_MODEL = 1024
BATCH = 2
SEQ = 8192
DEPTH = 1

D_MIX = D_MODEL
D_POOL = D_MIX // 2
POOL_WINDOWS = (2, 4, 8, 16)
N_POOL_GROUPS = len(POOL_WINDOWS)
POOL_GROUP = D_POOL // N_POOL_GROUPS
D_SSM = D_MIX - D_POOL
SSM_GROUP = 16
N_SSM_GROUPS = D_SSM // SSM_GROUP
SSM_STATE = 64
N_DIR = 2
DT_MIN = 0.001
DT_MAX = 0.1
N_EXPERT_GROUPS = 4
EXPERTS_PER_GROUP = 8
TOP_K_IN_GROUP = 2
D_EXPERT = D_MODEL // 4
D_PLE = 256
RMS_EPS = 1e-6

kernel_name = 'hymba_pool_s5_hmoe_block'

F32 = jnp.float32


def rmsnorm(x, g):
    xf = x.astype(F32)
    y = xf * lax.rsqrt(jnp.mean(xf * xf, axis=-1, keepdims=True) + RMS_EPS)
    return (y * g.astype(F32)).astype(x.dtype)


def centred_window_mean(x, w):
    s = x.shape[1]
    xf = x.astype(F32)
    cs = jnp.concatenate([jnp.zeros_like(xf[:, :1]), jnp.cumsum(xf, axis=1)], axis=1)
    t = jnp.arange(s)
    lo = jnp.clip(t - w // 2, 0, s)
    hi = jnp.clip(t - w // 2 + w, 0, s)
    total = jnp.take(cs, hi, axis=1) - jnp.take(cs, lo, axis=1)
    return total / (hi - lo).astype(F32)[None, :, None]


def pool_mixer(u, pool_w, pool_scale):
    outs = []
    for gi, w in enumerate(POOL_WINDOWS):
        ug = u[..., gi * POOL_GROUP:(gi + 1) * POOL_GROUP]
        diff = (centred_window_mean(ug, w) - ug.astype(F32)).astype(u.dtype)
        outs.append(diff @ pool_w[gi])
    return jnp.concatenate(outs, axis=-1) * pool_scale


def ssm_scan_direction(u_c, lam_bar, b_bar, c):
    bu = jnp.einsum('bsgc,gnc->bsgn', u_c, b_bar)
    a = jnp.broadcast_to(lam_bar, bu.shape)

    def combine(left, right):
        a_l, b_l = left
        a_r, b_r = right
        return a_l * a_r, a_r * b_l + b_r

    _, states = lax.associative_scan(combine, (a, bu), axis=1)
    return jnp.einsum('bsgn,gcn->bsgc', states, c).real


def ssm_mixer(u, a_re, a_im, log_dt, b_re, b_im, c_re, c_im, d, glu_w, glu_b):
    bsz, s, _ = u.shape
    uf = u.astype(F32)
    u_c = uf.reshape(bsz, s, N_SSM_GROUPS, SSM_GROUP).astype(jnp.complex64)
    y = uf * d.astype(F32)
    for di in range(N_DIR):
        lam = lax.complex(a_re[di].astype(F32), a_im[di].astype(F32))
        dt = jnp.exp(log_dt[di].astype(F32))[:, None]
        lam_bar = jnp.exp(lam * dt)
        b_bar = ((lam_bar - 1.0) / lam)[..., None] * lax.complex(b_re[di].astype(F32), b_im[di].astype(F32))
        c = lax.complex(c_re[di].astype(F32), c_im[di].astype(F32))
        if di == 0:
            yd = ssm_scan_direction(u_c, lam_bar, b_bar, c)
        else:
            yd = jnp.flip(ssm_scan_direction(jnp.flip(u_c, axis=1), lam_bar, b_bar, c), axis=1)
        y = y + yd.reshape(bsz, s, D_SSM)
    z = jax.nn.gelu(y)
    gate = jax.nn.sigmoid(z @ glu_w.astype(F32) + glu_b.astype(F32))
    return (z * gate).astype(u.dtype)


def hierarchical_moe(v, grp_w, grp_b, exp_w, exp_b, w_gate, w_up, w_down):
    bsz, s, dm = v.shape
    t = v.reshape(-1, dm)
    tf = t.astype(F32)
    grp_prob = jax.nn.softmax(tf @ grp_w.astype(F32) + grp_b.astype(F32), axis=-1)
    grp_p, grp_idx = lax.top_k(grp_prob, 1)
    exp_logits = jnp.einsum('td,gde->tge', tf, exp_w.astype(F32)) + exp_b.astype(F32)
    sel_logits = jnp.take_along_axis(exp_logits, grp_idx[:, :, None], axis=1)[:, 0]
    exp_prob = jax.nn.softmax(sel_logits, axis=-1)
    top_w, top_idx = lax.top_k(exp_prob, TOP_K_IN_GROUP)
    top_w = top_w / jnp.sum(top_w, axis=-1, keepdims=True)
    exp_comb = jnp.sum(jax.nn.one_hot(top_idx, EXPERTS_PER_GROUP, dtype=F32) * top_w[..., None], axis=1)
    combine = jax.nn.one_hot(grp_idx[:, 0], N_EXPERT_GROUPS, dtype=F32)[:, :, None] * (grp_p * exp_comb)[:, None, :]
    out = jnp.zeros((t.shape[0], dm), F32)
    for g in range(N_EXPERT_GROUPS):
        hg = jax.nn.silu(jnp.einsum('td,edf->tef', t, w_gate[g])) * jnp.einsum('td,edf->tef', t, w_up[g])
        hg = hg * combine[:, g, :, None].astype(hg.dtype)
        out = out + jnp.einsum('tef,efd->td', hg, w_down[g]).astype(F32)
    return out.astype(v.dtype).reshape(bsz, s, dm)


def setup_inputs(seed: int = 0) -> dict:
    key = jax.random.key(seed)
    ks = jax.random.split(key, 32)
    L = DEPTH
    G, N, C = N_SSM_GROUPS, SSM_STATE, SSM_GROUP
    GE, E, F = N_EXPERT_GROUPS, EXPERTS_PER_GROUP, D_EXPERT
    nrm = lambda k, shape, scale: jax.random.normal(k, shape, F32) * scale
    a_im_base = jnp.pi * jnp.arange(N, dtype=F32)
    return {
        'x': jax.random.normal(ks[0], (BATCH, SEQ, D_MODEL), F32),
        'p': jax.random.normal(ks[1], (DEPTH, BATCH, SEQ, D_PLE), F32),
        'g_mix': 1.0 + nrm(ks[2], (L, D_MODEL), 0.05),
        'w_in': nrm(ks[3], (L, D_MODEL, D_MIX), D_MODEL ** -0.5),
        'pool_w': nrm(ks[4], (L, N_POOL_GROUPS, POOL_GROUP, POOL_GROUP), POOL_GROUP ** -0.5),
        'pool_scale': 1.0 + nrm(ks[5], (L, D_POOL), 0.05),
        'ssm_a_re': -0.5 + nrm(ks[6], (L, N_DIR, G, N), 0.01),
        'ssm_a_im': a_im_base + nrm(ks[7], (L, N_DIR, G, N), 0.01),
        'ssm_log_dt': jax.random.uniform(ks[8], (L, N_DIR, G), F32, math.log(DT_MIN), math.log(DT_MAX)),
        'ssm_b_re': nrm(ks[9], (L, N_DIR, G, N, C), (2 * C) ** -0.5),
        'ssm_b_im': nrm(ks[10], (L, N_DIR, G, N, C), (2 * C) ** -0.5),
        'ssm_c_re': nrm(ks[11], (L, N_DIR, G, C, N), 0.5),
        'ssm_c_im': nrm(ks[12], (L, N_DIR, G, C, N), 0.5),
        'ssm_d': nrm(ks[13], (L, D_SSM), 1.0),
        'glu_w': nrm(ks[14], (L, D_SSM, D_SSM), D_SSM ** -0.5),
        'glu_b': nrm(ks[15], (L, D_SSM), 0.02),
        'w_out': nrm(ks[16], (L, D_MIX, D_MODEL), D_MIX ** -0.5),
        'g_ffn': 1.0 + nrm(ks[17], (L, D_MODEL), 0.05),
        'router_grp_w': nrm(ks[18], (L, D_MODEL, GE), D_MODEL ** -0.5),
        'router_grp_b': nrm(ks[19], (L, GE), 0.01),
        'router_exp_w': nrm(ks[20], (L, GE, D_MODEL, E), D_MODEL ** -0.5),
        'router_exp_b': nrm(ks[21], (L, GE, E), 0.01),
        'exp_w_gate': nrm(ks[22], (L, GE, E, D_MODEL, F), D_MODEL ** -0.5),
        'exp_w_up': nrm(ks[23], (L, GE, E, D_MODEL, F), D_MODEL ** -0.5),
        'exp_w_down': nrm(ks[24], (L, GE, E, F, D_MODEL), F ** -0.5),
        'g_ple': 1.0 + nrm(ks[25], (L, D_MODEL), 0.05),
        'ple_gate_w': nrm(ks[26], (L, D_MODEL, D_MODEL), D_MODEL ** -0.5),
        'ple_gate_b': nrm(ks[27], (L, D_MODEL), 0.02),
        'ple_proj_w': nrm(ks[28], (L, D_PLE, D_MODEL), D_PLE ** -0.5),
        'g_final': 1.0 + nrm(ks[29], (D_MODEL,), 0.05),
    }


def reference(x, p, g_mix, w_in, pool_w, pool_scale, ssm_a_re, ssm_a_im, ssm_log_dt,
              ssm_b_re, ssm_b_im, ssm_c_re, ssm_c_im, ssm_d, glu_w, glu_b, w_out,
              g_ffn, router_grp_w, router_grp_b, router_exp_w, router_exp_b,
              exp_w_gate, exp_w_up, exp_w_down, g_ple, ple_gate_w, ple_gate_b,
              ple_proj_w, g_final):
    h = x
    for i in range(DEPTH):
        u = rmsnorm(h, g_mix[i])
        z = u @ w_in[i]
        a = pool_mixer(z[..., :D_POOL], pool_w[i], pool_scale[i])
        s = ssm_mixer(z[..., D_POOL:], ssm_a_re[i], ssm_a_im[i], ssm_log_dt[i],
                      ssm_b_re[i], ssm_b_im[i], ssm_c_re[i], ssm_c_im[i], ssm_d[i],
                      glu_w[i], glu_b[i])
        h = h + jnp.concatenate([a, s], axis=-1) @ w_out[i]
        h = h + hierarchical_moe(rmsnorm(h, g_ffn[i]), router_grp_w[i], router_grp_b[i],
                                 router_exp_w[i], router_exp_b[i], exp_w_gate[i],
                                 exp_w_up[i], exp_w_down[i])
        gate = jax.nn.sigmoid(rmsnorm(h, g_ple[i]) @ ple_gate_w[i] + ple_gate_b[i])
        h = h + gate * (p[i] @ ple_proj_w[i])
    return rmsnorm(h, g_final)
```

```python
import functools
import math

import numpy as np
import jax
import jax.numpy as jnp
from jax import lax
from jax.experimental import pallas as pl
from jax.experimental.pallas import tpu as pltpu

F32 = jnp.float32
BF16 = jnp.bfloat16

D_MODEL = 1024
D_POOL = 512
D_SSM = 512
POOL_WINDOWS = (2, 4, 8, 16)
POOL_GROUP = 128
SSM_GROUP = 16
N_SSM_GROUPS = 32
SSM_STATE = 64
N_EXPERT_GROUPS = 4
EXPERTS_PER_GROUP = 8
N_EXPERTS = N_EXPERT_GROUPS * EXPERTS_PER_GROUP
D_EXPERT = 256
D_PLE = 256
RMS_EPS = 1e-6

LANES = 128
CHUNK = 32
CHUNK_W = CHUNK * SSM_GROUP
T_SUB = 8
ROUTER_W = LANES
VMEM_LIMIT = 56 * 1024 * 1024


def _dot(a, b):
    return jnp.dot(a, b, preferred_element_type=F32)


def _dot_nt(a, b):
    return lax.dot_general(a, b, (((1,), (1,)), ((), ())), preferred_element_type=F32)


def _rms(x, g):
    return x * lax.rsqrt(jnp.mean(x * x, axis=-1, keepdims=True) + RMS_EPS) * g


def _sigmoid(x):
    return 1.0 / (1.0 + jnp.exp(-x))


def _params(sem):
    return pltpu.CompilerParams(dimension_semantics=sem, vmem_limit_bytes=VMEM_LIMIT)


def _in_proj_kernel(x_ref, g_ref, w_ref, zp_ref, zs_ref):
    u = _rms(x_ref[...], g_ref[...]).astype(BF16)
    z = _dot(u, w_ref[...])
    zp_ref[...] = z[:, :D_POOL]
    zs_ref[...] = z[:, D_POOL:].astype(BF16)


def _in_proj(x2, g_mix, w_in, tm):
    t = x2.shape[0]
    return pl.pallas_call(
        _in_proj_kernel,
        grid=(t // tm,),
        in_specs=[
            pl.BlockSpec((tm, D_MODEL), lambda i: (i, 0)),
            pl.BlockSpec((1, D_MODEL), lambda i: (0, 0)),
            pl.BlockSpec((D_MODEL, D_MODEL), lambda i: (0, 0)),
        ],
        out_specs=[
            pl.BlockSpec((tm, D_POOL), lambda i: (i, 0)),
            pl.BlockSpec((tm, D_SSM), lambda i: (i, 0)),
        ],
        out_shape=[
            jax.ShapeDtypeStruct((t, D_POOL), F32),
            jax.ShapeDtypeStruct((t, D_SSM), BF16),
        ],
        compiler_params=_params(("parallel",)),
        name="in_proj",
    )(x2, g_mix, w_in)


def _shift_rows(x, d, row, n):
    if d == 0:
        return x
    r = pltpu.roll(x, d % n, 0)
    if d > 0:
        return jnp.where(row >= d, r, 0.0)
    return jnp.where(row < n + d, r, 0.0)


def _pool_kernel(z_ref, w_ref, sc_ref, o_ref):
    n = z_ref.shape[0]
    gi = pl.program_id(1)
    row = lax.broadcasted_iota(jnp.int32, (n, 1), 0)

    for k, w in enumerate(POOL_WINDOWS):
        @pl.when(gi == k)
        def _(w=w):
            x = z_ref[...]
            half = w // 2
            pd, pu, span = x, x, 1
            while span < half:
                pd = pd + _shift_rows(pd, span, row, n)
                pu = pu + _shift_rows(pu, -span, row, n)
                span *= 2
            total = _shift_rows(pd, 1, row, n) + pu
            lo = jnp.maximum(row - half, 0)
            hi = jnp.minimum(row + half, n)
            cnt = (hi - lo).astype(F32)
            diff = (total / cnt - x).astype(BF16)
            o_ref[...] = (_dot(diff, w_ref[...].astype(BF16)) * sc_ref[...]).astype(BF16)


def _pool(zp3, pool_w, pool_scale):
    b, s, _ = zp3.shape
    return pl.pallas_call(
        _pool_kernel,
        grid=(b, len(POOL_WINDOWS)),
        in_specs=[
            pl.BlockSpec((None, s, POOL_GROUP), lambda i, g: (i, 0, g)),
            pl.BlockSpec((None, POOL_GROUP, POOL_GROUP), lambda i, g: (g, 0, 0)),
            pl.BlockSpec((1, POOL_GROUP), lambda i, g: (0, g)),
        ],
        out_specs=pl.BlockSpec((None, s, POOL_GROUP), lambda i, g: (i, 0, g)),
        out_shape=jax.ShapeDtypeStruct((b, s, D_POOL), BF16),
        compiler_params=_params(("parallel", "parallel")),
        name="pool",
    )(zp3, pool_w, pool_scale)


def _expand_consts():
    time = np.arange(CHUNK_W) // SSM_GROUP
    def onehot(e):
        m = np.zeros((CHUNK_W, LANES), np.float32)
        m[np.arange(CHUNK_W), e] = 1.0
        return m
    return np.stack([
        onehot(CHUNK - 1 - time),
        onehot(time),
        onehot(time + 1),
        onehot(CHUNK - time),
    ])


def _cmul_packed(x, p, q):
    return x * p + pltpu.roll(x, LANES // 2, 1) * q


def _ssm_kernel(u_ref, are_ref, aim_ref, ldt_ref, bt_re_ref, bt_im_ref, ccr_ref, ct_re_ref,
                ct_im_ref, d_ref, exp_ref, y_ref, *, seq_chunks):
    nch = u_ref.shape[0]
    half = LANES // 2
    lane = lax.broadcasted_iota(jnp.int32, (1, LANES), 1)
    lo_half = lane < half

    def direction(di):
        a_re = are_ref[di]
        a_im = aim_ref[di]
        dt = jnp.exp(ldt_ref[di])
        mag = jnp.exp(a_re * dt)
        ang = a_im * dt
        lam = jnp.where(lo_half, mag * jnp.cos(ang), mag * jnp.sin(ang))
        lb_re = mag * jnp.cos(ang)
        lb_im = mag * jnp.sin(ang)
        den = a_re * a_re + a_im * a_im
        f_re = ((lb_re - 1.0) * a_re + lb_im * a_im) / den
        f_im = (lb_im * a_re - (lb_re - 1.0) * a_im) / den
        return lam, f_re, f_im

    def power_table(lam):
        e = lax.broadcasted_iota(jnp.int32, (LANES, 1), 0)
        tab = jnp.where(lo_half, 1.0, 0.0) * jnp.ones((LANES, 1), F32)
        sq = lam
        for k in range(7):
            p = jnp.where(lo_half, sq, pltpu.roll(sq, half, 1))
            q = jnp.where(lo_half, -pltpu.roll(sq, half, 1), sq)
            tab = jnp.where(((e >> k) & 1) == 1, _cmul_packed(tab, p, q), tab)
            sq = _cmul_packed(sq, p, q)
        return tab

    def tile_rows(x16):
        return jnp.broadcast_to(x16[None], (CHUNK, SSM_GROUP, LANES)).reshape(CHUNK_W, LANES)

    def expanded(tab, which, v_re, v_im, conj_sign):
        lexp = _dot(exp_ref[which], tab.astype(BF16))
        if conj_sign > 0:
            p = jnp.where(lo_half, v_re, v_re)
            q = jnp.where(lo_half, -v_im, v_im)
        else:
            p = jnp.where(lo_half, v_re, -v_re)
            q = jnp.where(lo_half, -v_im, -v_im)
        return lexp * tile_rows(p) + pltpu.roll(lexp, half, 1) * tile_rows(q)

    lam_f, ff_re, ff_im = direction(0)
    lam_b, fb_re, fb_im = direction(1)
    tab_f = power_table(lam_f)
    tab_b = power_table(lam_b)

    def bbar(bt_re, bt_im, f_re, f_im):
        return bt_re * f_re - bt_im * f_im, bt_re * f_im + bt_im * f_re

    bf_re, bf_im = bbar(bt_re_ref[0], bt_im_ref[0], ff_re, ff_im)
    bb_re, bb_im = bbar(bt_re_ref[1], bt_im_ref[1], fb_re, fb_im)

    pb1 = expanded(tab_f, 0, bf_re, bf_im, 1)
    pb2 = expanded(tab_b, 1, bb_re, bb_im, 1)
    pb3 = expanded(tab_b, 2, bb_re, bb_im, 1)
    ft_f = expanded(tab_f, 2, ct_re_ref[0], ct_im_ref[0], -1)
    ft_b = expanded(tab_b, 3, ct_re_ref[1], ct_im_ref[1], -1)

    row_w = lax.broadcasted_iota(jnp.int32, (CHUNK_W, 1), 0)
    last_blk = row_w >= CHUNK_W - SSM_GROUP
    pb2_lag0 = jnp.where(last_blk, pltpu.roll(pb2, CHUNK_W - SSM_GROUP, 0), 0.0)
    ccr_f = ccr_ref[0].astype(BF16)
    ccr_b = ccr_ref[1].astype(BF16)
    r_lo = _dot_nt(ccr_f, pb1.astype(BF16)) + _dot_nt(ccr_b, pb2_lag0.astype(BF16))
    co = lax.broadcasted_iota(jnp.int32, (SSM_GROUP, CHUNK_W), 0)
    col = lax.broadcasted_iota(jnp.int32, (SSM_GROUP, CHUNK_W), 1)
    r_lo = r_lo + jnp.where(col == CHUNK_W - SSM_GROUP + co, d_ref[...], 0.0)
    r_hi = _dot_nt(ccr_b, pb3.astype(BF16))
    r_t = jnp.concatenate([r_lo, r_hi], axis=1)
    g_t = jnp.concatenate(
        [pltpu.roll(r_t, SSM_GROUP * (tl + 1), 1) for tl in range(T_SUB)], axis=0
    ).astype(BF16)

    u = u_ref[...]
    e_mat = jnp.concatenate([pb1, pb2], axis=1).astype(BF16)
    xend = _dot(u, e_mat)
    rowc = lax.broadcasted_iota(jnp.int32, (nch, 1), 0) % seq_chunks

    def scan(a, tab, forward):
        lam_l = tab[CHUNK:CHUNK + 1, :]
        n_steps = int(math.log2(seq_chunks))
        for k in range(n_steps):
            d = 1 << k
            p = jnp.where(lo_half, lam_l, pltpu.roll(lam_l, half, 1))
            q = jnp.where(lo_half, -pltpu.roll(lam_l, half, 1), lam_l)
            if forward:
                sh = jnp.where(rowc >= d, pltpu.roll(a, d, 0), 0.0)
            else:
                sh = jnp.where(rowc < seq_chunks - d, pltpu.roll(a, nch - d, 0), 0.0)
            a = a + _cmul_packed(sh, p, q)
            lam_l = _cmul_packed(lam_l, p, q)
        if forward:
            return jnp.where(rowc >= 1, pltpu.roll(a, 1, 0), 0.0)
        return jnp.where(rowc < seq_chunks - 1, pltpu.roll(a, nch - 1, 0), 0.0)

    xin_f = scan(xend[:, :LANES], tab_f, True)
    xin_b = scan(xend[:, LANES:], tab_b, False)
    xin = jnp.concatenate([xin_f, xin_b], axis=1).astype(BF16)
    f_t = jnp.concatenate([ft_f, ft_b], axis=1).astype(BF16)

    y_state = _dot_nt(xin, f_t)
    for th in range(CHUNK // T_SUB):
        start = CHUNK_W - LANES * th
        y_loc = _dot_nt(u, g_t[:, start:start + CHUNK_W])
        y_ref[:, th * LANES:(th + 1) * LANES] = (
            y_loc + y_state[:, th * LANES:(th + 1) * LANES]).astype(y_ref.dtype)


def _ssm(ug, a_re, a_im, log_dt, b_re, b_im, c_re, c_im, d, seq_chunks):
    g, nch, _ = ug.shape
    n = SSM_STATE

    def lane_vec(a):
        a = jnp.transpose(a, (1, 0, 2))
        return jnp.concatenate([a, a], axis=-1)[:, :, None, :]

    are = lane_vec(a_re)
    aim = lane_vec(a_im)
    ldt = lane_vec(jnp.broadcast_to(log_dt[..., None], (2, g, n)))

    def bt(b):
        b = jnp.transpose(b, (1, 0, 3, 2))
        return jnp.concatenate([b, b], axis=-1)

    def ct(c):
        c = jnp.transpose(c, (1, 0, 2, 3))
        return jnp.concatenate([c, c], axis=-1)

    ccr = jnp.concatenate([jnp.transpose(c_re, (1, 0, 2, 3)),
                           -jnp.transpose(c_im, (1, 0, 2, 3))], axis=-1)
    d_col = d.reshape(g, SSM_GROUP, 1)
    exp_c = jnp.asarray(_expand_consts(), BF16)

    vec_spec = pl.BlockSpec((None, 2, 1, LANES), lambda i: (i, 0, 0, 0))
    mat_spec = pl.BlockSpec((None, 2, SSM_GROUP, LANES), lambda i: (i, 0, 0, 0))
    return pl.pallas_call(
        functools.partial(_ssm_kernel, seq_chunks=seq_chunks),
        grid=(g,),
        in_specs=[
            pl.BlockSpec((None, nch, CHUNK_W), lambda i: (i, 0, 0)),
            vec_spec, vec_spec, vec_spec,
            mat_spec, mat_spec, mat_spec, mat_spec, mat_spec,
            pl.BlockSpec((None, SSM_GROUP, 1), lambda i: (i, 0, 0)),
            pl.BlockSpec((4, CHUNK_W, LANES), lambda i: (0, 0, 0)),
        ],
        out_specs=pl.BlockSpec((None, nch, CHUNK_W), lambda i: (i, 0, 0)),
        out_shape=jax.ShapeDtypeStruct((g, nch, CHUNK_W), BF16),
        compiler_params=_params(("parallel",)),
        name="ssm",
    )(ug, are, aim, ldt, bt(b_re), bt(b_im), ccr, ct(c_re), ct(c_im), d_col, exp_c)


def _mix_out_kernel(x_ref, a_ref, y_ref, gw_ref, gb_ref, wo_ref, h_ref):
    y = y_ref[...].astype(F32)
    z = 0.5 * y * (1.0 + jnp.tanh(math.sqrt(2.0 / math.pi) * (y + 0.044715 * (y * y * y))))
    gate = _sigmoid(_dot(z.astype(BF16), gw_ref[...]) + gb_ref[...])
    s = (z * gate).astype(BF16)
    h = x_ref[...] + _dot(a_ref[...], wo_ref[:D_POOL, :]) + _dot(s, wo_ref[D_POOL:, :])
    h_ref[...] = h


def _mix_out(x2, a2, y2, glu_w, glu_b, w_out, tm):
    t = x2.shape[0]
    return pl.pallas_call(
        _mix_out_kernel,
        grid=(t // tm,),
        in_specs=[
            pl.BlockSpec((tm, D_MODEL), lambda i: (i, 0)),
            pl.BlockSpec((tm, D_POOL), lambda i: (i, 0)),
            pl.BlockSpec((tm, D_SSM), lambda i: (i, 0)),
            pl.BlockSpec((D_SSM, D_SSM), lambda i: (0, 0)),
            pl.BlockSpec((1, D_SSM), lambda i: (0, 0)),
            pl.BlockSpec((D_MODEL, D_MODEL), lambda i: (0, 0)),
        ],
        out_specs=pl.BlockSpec((tm, D_MODEL), lambda i: (i, 0)),
        out_shape=jax.ShapeDtypeStruct((t, D_MODEL), F32),
        compiler_params=_params(("parallel",)),
        name="mix_out",
    )(x2, a2, y2, glu_w, glu_b, w_out)


def _route(v32, wr_ref, br_ref):
    logits = jnp.dot(v32, wr_ref[...], preferred_element_type=F32,
                     precision=lax.Precision.HIGHEST) + br_ref[...]
    lane = lax.broadcasted_iota(jnp.int32, logits.shape, 1)
    neg = -jnp.inf
    lg = jnp.where(lane < N_EXPERT_GROUPS, logits, neg)
    mg = jnp.max(lg, axis=1, keepdims=True)
    grp_p = 1.0 / jnp.sum(jnp.exp(lg - mg), axis=1, keepdims=True)
    grp_idx = jnp.min(jnp.where(lg == mg, lane, ROUTER_W), axis=1, keepdims=True)
    first = N_EXPERT_GROUPS + grp_idx * EXPERTS_PER_GROUP
    sel = (lane >= first) & (lane < first + EXPERTS_PER_GROUP)
    le = jnp.where(sel, logits, neg)
    m1 = jnp.max(le, axis=1, keepdims=True)
    i1 = jnp.min(jnp.where(le == m1, lane, ROUTER_W), axis=1, keepdims=True)
    z = jnp.sum(jnp.exp(le - m1), axis=1, keepdims=True)
    le2 = jnp.where(lane == i1, neg, le)
    m2 = jnp.max(le2, axis=1, keepdims=True)
    i2 = jnp.min(jnp.where(le2 == m2, lane, ROUTER_W), axis=1, keepdims=True)
    p1 = 1.0 / z
    p2 = jnp.exp(m2 - m1) / z
    tot = p1 + p2
    return jnp.where(lane == i1, grp_p * (p1 / tot),
                     jnp.where(lane == i2, grp_p * (p2 / tot), 0.0))


def _moe_kernel(h_ref, g_ref, wr_ref, br_ref, wg_ref, wu_ref, wd_ref, o_ref, v_ref, comb_ref):
    e = pl.program_id(1)

    @pl.when(e == 0)
    def _():
        v32 = _rms(h_ref[...], g_ref[...])
        v_ref[...] = v32.astype(BF16)
        comb_ref[...] = _route(v32, wr_ref, br_ref)
        o_ref[...] = h_ref[...]

    v = v_ref[...]
    comb = comb_ref[...]
    lane = lax.broadcasted_iota(jnp.int32, comb.shape, 1)
    w_e = jnp.sum(jnp.where(lane == N_EXPERT_GROUPS + e, comb, 0.0), axis=1, keepdims=True)
    hg = _dot(v, wg_ref[...].astype(BF16))
    hu = _dot(v, wu_ref[...].astype(BF16))
    hid = (hg * _sigmoid(hg)) * hu * w_e
    o_ref[...] += _dot(hid.astype(BF16), wd_ref[...].astype(BF16))


def _moe(h1, g_ffn, w_router, b_router, w_gate, w_up, w_down, tm):
    t = h1.shape[0]
    return pl.pallas_call(
        _moe_kernel,
        grid=(t // tm, N_EXPERTS),
        in_specs=[
            pl.BlockSpec((tm, D_MODEL), lambda i, e: (i, 0)),
            pl.BlockSpec((1, D_MODEL), lambda i, e: (0, 0)),
            pl.BlockSpec((D_MODEL, ROUTER_W), lambda i, e: (0, 0)),
            pl.BlockSpec((1, ROUTER_W), lambda i, e: (0, 0)),
            pl.BlockSpec((None, D_MODEL, D_EXPERT), lambda i, e: (e, 0, 0)),
            pl.BlockSpec((None, D_MODEL, D_EXPERT), lambda i, e: (e, 0, 0)),
            pl.BlockSpec((None, D_EXPERT, D_MODEL), lambda i, e: (e, 0, 0)),
        ],
        out_specs=pl.BlockSpec((tm, D_MODEL), lambda i, e: (i, 0)),
        out_shape=jax.ShapeDtypeStruct((t, D_MODEL), F32),
        scratch_shapes=[
            pltpu.VMEM((tm, D_MODEL), BF16),
            pltpu.VMEM((tm, ROUTER_W), F32),
        ],
        compiler_params=_params(("parallel", "arbitrary")),
        name="moe",
    )(h1, g_ffn, w_router, b_router, w_gate, w_up, w_down)


def _ple_kernel(h_ref, p_ref, gp_ref, wg_ref, bg_ref, wp_ref, gf_ref, o_ref, *, final_norm):
    h = h_ref[...]
    gate = _sigmoid(_dot(_rms(h, gp_ref[...]).astype(BF16), wg_ref[...]) + bg_ref[...])
    h = h + gate * _dot(p_ref[...].astype(BF16), wp_ref[...])
    o_ref[...] = _rms(h, gf_ref[...]) if final_norm else h


def _ple(h2, p2, g_ple, w_gate, b_gate, w_proj, g_final, final_norm, tm):
    t = h2.shape[0]
    return pl.pallas_call(
        functools.partial(_ple_kernel, final_norm=final_norm),
        grid=(t // tm,),
        in_specs=[
            pl.BlockSpec((tm, D_MODEL), lambda i: (i, 0)),
            pl.BlockSpec((tm, D_PLE), lambda i: (i, 0)),
            pl.BlockSpec((1, D_MODEL), lambda i: (0, 0)),
            pl.BlockSpec((D_MODEL, D_MODEL), lambda i: (0, 0)),
            pl.BlockSpec((1, D_MODEL), lambda i: (0, 0)),
            pl.BlockSpec((D_PLE, D_MODEL), lambda i: (0, 0)),
            pl.BlockSpec((1, D_MODEL), lambda i: (0, 0)),
        ],
        out_specs=pl.BlockSpec((tm, D_MODEL), lambda i: (i, 0)),
        out_shape=jax.ShapeDtypeStruct((t, D_MODEL), F32),
        compiler_params=_params(("parallel",)),
        name="ple",
    )(h2, p2, g_ple, w_gate, b_gate, w_proj, g_final)


def kernel(x, p, g_mix, w_in, pool_w, pool_scale, ssm_a_re, ssm_a_im, ssm_log_dt, ssm_b_re,
           ssm_b_im, ssm_c_re, ssm_c_im, ssm_d, glu_w, glu_b, w_out, g_ffn, router_grp_w,
           router_grp_b, router_exp_w, router_exp_b, exp_w_gate, exp_w_up, exp_w_down, g_ple,
           ple_gate_w, ple_gate_b, ple_proj_w, g_final):
    bsz, seq, dm = x.shape
    depth = g_mix.shape[0]
    t = bsz * seq
    seq_chunks = seq // CHUNK
    nch = t // CHUNK
    tm = 512

    h = x.reshape(t, dm)
    for i in range(depth):
        zp, zs = _in_proj(h, g_mix[i][None], w_in[i].astype(BF16), tm)
        a = _pool(zp.reshape(bsz, seq, D_POOL), pool_w[i], pool_scale[i][None])
        ug = zs.reshape(nch, CHUNK, N_SSM_GROUPS, SSM_GROUP)
        ug = jnp.transpose(ug, (2, 0, 1, 3)).reshape(N_SSM_GROUPS, nch, CHUNK_W)
        yg = _ssm(ug, ssm_a_re[i], ssm_a_im[i], ssm_log_dt[i], ssm_b_re[i], ssm_b_im[i],
                  ssm_c_re[i], ssm_c_im[i], ssm_d[i], seq_chunks)
        y = yg.reshape(N_SSM_GROUPS, nch, CHUNK, SSM_GROUP)
        y = jnp.transpose(y, (1, 2, 0, 3)).reshape(t, D_SSM)
        h = _mix_out(h, a.reshape(t, D_POOL), y, glu_w[i].astype(BF16), glu_b[i][None],
                     w_out[i].astype(BF16), tm)

        w_router = jnp.concatenate(
            [router_grp_w[i],
             jnp.transpose(router_exp_w[i], (1, 0, 2)).reshape(dm, N_EXPERTS),
             jnp.zeros((dm, ROUTER_W - N_EXPERT_GROUPS - N_EXPERTS), F32)], axis=1)
        b_router = jnp.concatenate(
            [router_grp_b[i], router_exp_b[i].reshape(N_EXPERTS),
             jnp.zeros((ROUTER_W - N_EXPERT_GROUPS - N_EXPERTS,), F32)])[None]
        h = _moe(h, g_ffn[i][None], w_router, b_router,
                 exp_w_gate[i].reshape(N_EXPERTS, dm, D_EXPERT),
                 exp_w_up[i].reshape(N_EXPERTS, dm, D_EXPERT),
                 exp_w_down[i].reshape(N_EXPERTS, D_EXPERT, dm), 1024)
        h = _ple(h, p[i].reshape(t, D_PLE), g_ple[i][None], ple_gate_w[i].astype(BF16),
                 ple_gate_b[i][None], ple_proj_w[i].astype(BF16), g_final[None],
                 i == depth - 1, tm)
    return h.reshape(bsz, seq, dm)
```

```python
import functools
import math

import numpy as np
import jax
import jax.numpy as jnp
from jax import lax
from jax.experimental import pallas as pl
from jax.experimental.pallas import tpu as pltpu
from jax.experimental.pallas import tpu_sc as plsc

F32 = jnp.float32
BF16 = jnp.bfloat16
U32 = jnp.uint32

D_MODEL = 1024
D_POOL = 512
D_SSM = 512
POOL_WINDOWS = (2, 4, 8, 16)
POOL_GROUP = 128
SSM_GROUP = 16
N_SSM_GROUPS = 32
SSM_STATE = 64
N_EXPERT_GROUPS = 4
EXPERTS_PER_GROUP = 8
N_EXPERTS = N_EXPERT_GROUPS * EXPERTS_PER_GROUP
D_EXPERT = 256
D_PLE = 256
RMS_EPS = 1e-6

LANES = 128
CHUNK = 32
CHUNK_W = CHUNK * SSM_GROUP
T_SUB = 8
ROUTER_W = LANES
HALF = D_MODEL // 2
ROW_TILE = 256
SC_WINDOW = 128
SC_ROW = HALF // 2
VMEM_LIMIT = 56 * 1024 * 1024


def _dot(a, b):
    return jnp.dot(a, b, preferred_element_type=F32)


def _dot_nt(a, b):
    return lax.dot_general(a, b, (((1,), (1,)), ((), ())), preferred_element_type=F32)


def _rms(x, g):
    return x * lax.rsqrt(jnp.mean(x * x, axis=-1, keepdims=True) + RMS_EPS) * g


def _sigmoid(x):
    return 1.0 / (1.0 + jnp.exp(-x))


def _params(sem):
    return pltpu.CompilerParams(dimension_semantics=sem, vmem_limit_bytes=VMEM_LIMIT)


def _in_proj_kernel(x_ref, g_ref, w_ref, zp_ref, zs_ref):
    u = _rms(x_ref[...], g_ref[...]).astype(BF16)
    z = _dot(u, w_ref[...])
    zp_ref[...] = z[:, :D_POOL]
    zs_ref[...] = z[:, D_POOL:].astype(BF16)


def _in_proj(x2, g_mix, w_in, tm):
    t = x2.shape[0]
    return pl.pallas_call(
        _in_proj_kernel,
        grid=(t // tm,),
        in_specs=[
            pl.BlockSpec((tm, D_MODEL), lambda i: (i, 0)),
            pl.BlockSpec((1, D_MODEL), lambda i: (0, 0)),
            pl.BlockSpec((D_MODEL, D_MODEL), lambda i: (0, 0)),
        ],
        out_specs=[
            pl.BlockSpec((tm, D_POOL), lambda i: (i, 0)),
            pl.BlockSpec((tm, D_SSM), lambda i: (i, 0)),
        ],
        out_shape=[
            jax.ShapeDtypeStruct((t, D_POOL), F32),
            jax.ShapeDtypeStruct((t, D_SSM), BF16),
        ],
        compiler_params=_params(("parallel",)),
        name="in_proj",
    )(x2, g_mix, w_in)


def _shift_rows(x, d, row, n):
    if d == 0:
        return x
    r = pltpu.roll(x, d % n, 0)
    if d > 0:
        return jnp.where(row >= d, r, 0.0)
    return jnp.where(row < n + d, r, 0.0)


def _pool_kernel(z_ref, w_ref, sc_ref, o_ref):
    n = z_ref.shape[0]
    gi = pl.program_id(1)
    row = lax.broadcasted_iota(jnp.int32, (n, 1), 0)

    for k, w in enumerate(POOL_WINDOWS):
        @pl.when(gi == k)
        def _(w=w):
            x = z_ref[...]
            half = w // 2
            pd, pu, span = x, x, 1
            while span < half:
                pd = pd + _shift_rows(pd, span, row, n)
                pu = pu + _shift_rows(pu, -span, row, n)
                span *= 2
            total = _shift_rows(pd, 1, row, n) + pu
            lo = jnp.maximum(row - half, 0)
            hi = jnp.minimum(row + half, n)
            cnt = (hi - lo).astype(F32)
            diff = (total / cnt - x).astype(BF16)
            o_ref[...] = (_dot(diff, w_ref[...].astype(BF16)) * sc_ref[...]).astype(BF16)


def _pool(zp3, pool_w, pool_scale):
    b, s, _ = zp3.shape
    return pl.pallas_call(
        _pool_kernel,
        grid=(b, len(POOL_WINDOWS)),
        in_specs=[
            pl.BlockSpec((None, s, POOL_GROUP), lambda i, g: (i, 0, g)),
            pl.BlockSpec((None, POOL_GROUP, POOL_GROUP), lambda i, g: (g, 0, 0)),
            pl.BlockSpec((1, POOL_GROUP), lambda i, g: (0, g)),
        ],
        out_specs=pl.BlockSpec((None, s, POOL_GROUP), lambda i, g: (i, 0, g)),
        out_shape=jax.ShapeDtypeStruct((b, s, D_POOL), BF16),
        compiler_params=_params(("parallel", "parallel")),
        name="pool",
    )(zp3, pool_w, pool_scale)


def _expand_consts():
    time = np.arange(CHUNK_W) // SSM_GROUP
    def onehot(e):
        m = np.zeros((CHUNK_W, LANES), np.float32)
        m[np.arange(CHUNK_W), e] = 1.0
        return m
    return np.stack([
        onehot(CHUNK - 1 - time),
        onehot(time),
        onehot(time + 1),
        onehot(CHUNK - time),
    ])


def _cmul_packed(x, p, q):
    return x * p + pltpu.roll(x, LANES // 2, 1) * q


def _ssm_kernel(u_ref, are_ref, aim_ref, ldt_ref, bt_re_ref, bt_im_ref, ccr_ref, ct_re_ref,
                ct_im_ref, d_ref, exp_ref, y_ref, *, seq_chunks):
    nch = u_ref.shape[0]
    half = LANES // 2
    lane = lax.broadcasted_iota(jnp.int32, (1, LANES), 1)
    lo_half = lane < half

    def direction(di):
        a_re = are_ref[di]
        a_im = aim_ref[di]
        dt = jnp.exp(ldt_ref[di])
        mag = jnp.exp(a_re * dt)
        ang = a_im * dt
        lam = jnp.where(lo_half, mag * jnp.cos(ang), mag * jnp.sin(ang))
        lb_re = mag * jnp.cos(ang)
        lb_im = mag * jnp.sin(ang)
        den = a_re * a_re + a_im * a_im
        f_re = ((lb_re - 1.0) * a_re + lb_im * a_im) / den
        f_im = (lb_im * a_re - (lb_re - 1.0) * a_im) / den
        return lam, f_re, f_im

    def power_table(lam):
        e = lax.broadcasted_iota(jnp.int32, (LANES, 1), 0)
        tab = jnp.where(lo_half, 1.0, 0.0) * jnp.ones((LANES, 1), F32)
        sq = lam
        for k in range(7):
            p = jnp.where(lo_half, sq, pltpu.roll(sq, half, 1))
            q = jnp.where(lo_half, -pltpu.roll(sq, half, 1), sq)
            tab = jnp.where(((e >> k) & 1) == 1, _cmul_packed(tab, p, q), tab)
            sq = _cmul_packed(sq, p, q)
        return tab

    def tile_rows(x16):
        return jnp.broadcast_to(x16[None], (CHUNK, SSM_GROUP, LANES)).reshape(CHUNK_W, LANES)

    def expanded(tab, which, v_re, v_im, conj_sign):
        lexp = _dot(exp_ref[which], tab.astype(BF16))
        if conj_sign > 0:
            p = jnp.where(lo_half, v_re, v_re)
            q = jnp.where(lo_half, -v_im, v_im)
        else:
            p = jnp.where(lo_half, v_re, -v_re)
            q = jnp.where(lo_half, -v_im, -v_im)
        return lexp * tile_rows(p) + pltpu.roll(lexp, half, 1) * tile_rows(q)

    lam_f, ff_re, ff_im = direction(0)
    lam_b, fb_re, fb_im = direction(1)
    tab_f = power_table(lam_f)
    tab_b = power_table(lam_b)

    def bbar(bt_re, bt_im, f_re, f_im):
        return bt_re * f_re - bt_im * f_im, bt_re * f_im + bt_im * f_re

    bf_re, bf_im = bbar(bt_re_ref[0], bt_im_ref[0], ff_re, ff_im)
    bb_re, bb_im = bbar(bt_re_ref[1], bt_im_ref[1], fb_re, fb_im)

    pb1 = expanded(tab_f, 0, bf_re, bf_im, 1)
    pb2 = expanded(tab_b, 1, bb_re, bb_im, 1)
    pb3 = expanded(tab_b, 2, bb_re, bb_im, 1)
    ft_f = expanded(tab_f, 2, ct_re_ref[0], ct_im_ref[0], -1)
    ft_b = expanded(tab_b, 3, ct_re_ref[1], ct_im_ref[1], -1)

    row_w = lax.broadcasted_iota(jnp.int32, (CHUNK_W, 1), 0)
    last_blk = row_w >= CHUNK_W - SSM_GROUP
    pb2_lag0 = jnp.where(last_blk, pltpu.roll(pb2, CHUNK_W - SSM_GROUP, 0), 0.0)
    ccr_f = ccr_ref[0].astype(BF16)
    ccr_b = ccr_ref[1].astype(BF16)
    r_lo = _dot_nt(ccr_f, pb1.astype(BF16)) + _dot_nt(ccr_b, pb2_lag0.astype(BF16))
    co = lax.broadcasted_iota(jnp.int32, (SSM_GROUP, CHUNK_W), 0)
    col = lax.broadcasted_iota(jnp.int32, (SSM_GROUP, CHUNK_W), 1)
    r_lo = r_lo + jnp.where(col == CHUNK_W - SSM_GROUP + co, d_ref[...], 0.0)
    r_hi = _dot_nt(ccr_b, pb3.astype(BF16))
    r_t = jnp.concatenate([r_lo, r_hi], axis=1)
    g_t = jnp.concatenate(
        [pltpu.roll(r_t, SSM_GROUP * (tl + 1), 1) for tl in range(T_SUB)], axis=0
    ).astype(BF16)

    u = u_ref[...]
    e_mat = jnp.concatenate([pb1, pb2], axis=1).astype(BF16)
    xend = _dot(u, e_mat)
    rowc = lax.broadcasted_iota(jnp.int32, (nch, 1), 0) % seq_chunks

    def scan(a, tab, forward):
        lam_l = tab[CHUNK:CHUNK + 1, :]
        n_steps = int(math.log2(seq_chunks))
        for k in range(n_steps):
            d = 1 << k
            p = jnp.where(lo_half, lam_l, pltpu.roll(lam_l, half, 1))
            q = jnp.where(lo_half, -pltpu.roll(lam_l, half, 1), lam_l)
            if forward:
                sh = jnp.where(rowc >= d, pltpu.roll(a, d, 0), 0.0)
            else:
                sh = jnp.where(rowc < seq_chunks - d, pltpu.roll(a, nch - d, 0), 0.0)
            a = a + _cmul_packed(sh, p, q)
            lam_l = _cmul_packed(lam_l, p, q)
        if forward:
            return jnp.where(rowc >= 1, pltpu.roll(a, 1, 0), 0.0)
        return jnp.where(rowc < seq_chunks - 1, pltpu.roll(a, nch - 1, 0), 0.0)

    xin_f = scan(xend[:, :LANES], tab_f, True)
    xin_b = scan(xend[:, LANES:], tab_b, False)
    xin = jnp.concatenate([xin_f, xin_b], axis=1).astype(BF16)
    f_t = jnp.concatenate([ft_f, ft_b], axis=1).astype(BF16)

    y_state = _dot_nt(xin, f_t)
    for th in range(CHUNK // T_SUB):
        start = CHUNK_W - LANES * th
        y_loc = _dot_nt(u, g_t[:, start:start + CHUNK_W])
        y_ref[:, th * LANES:(th + 1) * LANES] = (
            y_loc + y_state[:, th * LANES:(th + 1) * LANES]).astype(y_ref.dtype)


def _ssm(ug, a_re, a_im, log_dt, b_re, b_im, c_re, c_im, d, seq_chunks):
    g, nch, _ = ug.shape
    n = SSM_STATE

    def lane_vec(a):
        a = jnp.transpose(a, (1, 0, 2))
        return jnp.concatenate([a, a], axis=-1)[:, :, None, :]

    are = lane_vec(a_re)
    aim = lane_vec(a_im)
    ldt = lane_vec(jnp.broadcast_to(log_dt[..., None], (2, g, n)))

    def bt(b):
        b = jnp.transpose(b, (1, 0, 3, 2))
        return jnp.concatenate([b, b], axis=-1)

    def ct(c):
        c = jnp.transpose(c, (1, 0, 2, 3))
        return jnp.concatenate([c, c], axis=-1)

    ccr = jnp.concatenate([jnp.transpose(c_re, (1, 0, 2, 3)),
                           -jnp.transpose(c_im, (1, 0, 2, 3))], axis=-1)
    d_col = d.reshape(g, SSM_GROUP, 1)
    exp_c = jnp.asarray(_expand_consts(), BF16)

    vec_spec = pl.BlockSpec((None, 2, 1, LANES), lambda i: (i, 0, 0, 0))
    mat_spec = pl.BlockSpec((None, 2, SSM_GROUP, LANES), lambda i: (i, 0, 0, 0))
    return pl.pallas_call(
        functools.partial(_ssm_kernel, seq_chunks=seq_chunks),
        grid=(g,),
        in_specs=[
            pl.BlockSpec((None, nch, CHUNK_W), lambda i: (i, 0, 0)),
            vec_spec, vec_spec, vec_spec,
            mat_spec, mat_spec, mat_spec, mat_spec, mat_spec,
            pl.BlockSpec((None, SSM_GROUP, 1), lambda i: (i, 0, 0)),
            pl.BlockSpec((4, CHUNK_W, LANES), lambda i: (0, 0, 0)),
        ],
        out_specs=pl.BlockSpec((None, nch, CHUNK_W), lambda i: (i, 0, 0)),
        out_shape=jax.ShapeDtypeStruct((g, nch, CHUNK_W), BF16),
        compiler_params=_params(("parallel",)),
        name="ssm",
    )(ug, are, aim, ldt, bt(b_re), bt(b_im), ccr, ct(c_re), ct(c_im), d_col, exp_c)


def _mix_out_kernel(x_ref, a_ref, y_ref, gw_ref, gb_ref, wo_ref, h_ref):
    y = y_ref[...].astype(F32)
    z = 0.5 * y * (1.0 + jnp.tanh(math.sqrt(2.0 / math.pi) * (y + 0.044715 * (y * y * y))))
    gate = _sigmoid(_dot(z.astype(BF16), gw_ref[...]) + gb_ref[...])
    s = (z * gate).astype(BF16)
    h = x_ref[...] + _dot(a_ref[...], wo_ref[:D_POOL, :]) + _dot(s, wo_ref[D_POOL:, :])
    h_ref[...] = h


def _mix_out(x2, a2, y2, glu_w, glu_b, w_out, tm):
    t = x2.shape[0]
    return pl.pallas_call(
        _mix_out_kernel,
        grid=(t // tm,),
        in_specs=[
            pl.BlockSpec((tm, D_MODEL), lambda i: (i, 0)),
            pl.BlockSpec((tm, D_POOL), lambda i: (i, 0)),
            pl.BlockSpec((tm, D_SSM), lambda i: (i, 0)),
            pl.BlockSpec((D_SSM, D_SSM), lambda i: (0, 0)),
            pl.BlockSpec((1, D_SSM), lambda i: (0, 0)),
            pl.BlockSpec((D_MODEL, D_MODEL), lambda i: (0, 0)),
        ],
        out_specs=pl.BlockSpec((tm, D_MODEL), lambda i: (i, 0)),
        out_shape=jax.ShapeDtypeStruct((t, D_MODEL), F32),
        compiler_params=_params(("parallel",)),
        name="mix_out",
    )(x2, a2, y2, glu_w, glu_b, w_out)


def _pack_rows(x):
    b = lax.bitcast_convert_type(x.astype(BF16).astype(F32), U32)
    return (b[:, :HALF] & jnp.uint32(0xFFFF0000)) | (b[:, HALF:] >> 16)


def _unpack_rows(w):
    lo = lax.bitcast_convert_type(w & jnp.uint32(0xFFFF0000), F32)
    hi = lax.bitcast_convert_type(w << 16, F32)
    return lo, hi


def _route(v32, wr_ref, br_ref):
    logits = jnp.dot(v32, wr_ref[...], preferred_element_type=F32,
                     precision=lax.Precision.HIGHEST) + br_ref[...]
    lane = lax.broadcasted_iota(jnp.int32, logits.shape, 1)
    neg = -jnp.inf
    lg = jnp.where(lane < N_EXPERT_GROUPS, logits, neg)
    mg = jnp.max(lg, axis=1, keepdims=True)
    grp_p = 1.0 / jnp.sum(jnp.exp(lg - mg), axis=1, keepdims=True)
    grp_idx = jnp.min(jnp.where(lg == mg, lane, ROUTER_W), axis=1, keepdims=True)
    first = N_EXPERT_GROUPS + grp_idx * EXPERTS_PER_GROUP
    sel = (lane >= first) & (lane < first + EXPERTS_PER_GROUP)
    le = jnp.where(sel, logits, neg)
    m1 = jnp.max(le, axis=1, keepdims=True)
    i1 = jnp.min(jnp.where(le == m1, lane, ROUTER_W), axis=1, keepdims=True)
    z = jnp.sum(jnp.exp(le - m1), axis=1, keepdims=True)
    le2 = jnp.where(lane == i1, neg, le)
    m2 = jnp.max(le2, axis=1, keepdims=True)
    i2 = jnp.min(jnp.where(le2 == m2, lane, ROUTER_W), axis=1, keepdims=True)
    p1 = 1.0 / z
    p2 = jnp.exp(m2 - m1) / z
    tot = p1 + p2
    return i1, i2, grp_p * (p1 / tot), grp_p * (p2 / tot)


def _router_kernel(h_ref, g_ref, wr_ref, br_ref, vp_ref, meta_ref, cnt_ref, carry_ref):
    @pl.when(pl.program_id(0) == 0)
    def _():
        carry_ref[...] = jnp.zeros_like(carry_ref)

    v32 = _rms(h_ref[...], g_ref[...])
    vp_ref[...] = _pack_rows(v32)
    i1, i2, w1, w2 = _route(v32, wr_ref, br_ref)
    tm = v32.shape[0]
    lane = lax.broadcasted_iota(jnp.int32, (tm, ROUTER_W), 1)
    onehot = jnp.where(lane == i1, 1.0, jnp.where(lane == i2, 1.0, 0.0))
    r = lax.broadcasted_iota(jnp.int32, (tm, tm), 0)
    c = lax.broadcasted_iota(jnp.int32, (tm, tm), 1)
    below = jnp.where(c < r, 1.0, 0.0).astype(BF16)
    before = _dot(below, onehot.astype(BF16)) + carry_ref[...]
    rank1 = jnp.sum(jnp.where(lane == i1, before, 0.0), axis=1, keepdims=True)
    rank2 = jnp.sum(jnp.where(lane == i2, before, 0.0), axis=1, keepdims=True)
    carry = carry_ref[...] + jnp.sum(onehot, axis=0, keepdims=True)
    carry_ref[...] = carry
    cnt_ref[...] = carry
    e1 = (i1 - N_EXPERT_GROUPS).astype(F32)
    e2 = (i2 - N_EXPERT_GROUPS).astype(F32)
    meta = jnp.where(lane == 0, e1, jnp.where(lane == 1, e2, jnp.where(
        lane == 2, rank1, jnp.where(lane == 3, rank2, jnp.where(
            lane == 4, w1, jnp.where(lane == 5, w2, 0.0))))))
    meta_ref[...] = meta


def _router(h1, g_ffn, w_router, b_router, tm):
    t = h1.shape[0]
    return pl.pallas_call(
        _router_kernel,
        grid=(t // tm,),
        in_specs=[
            pl.BlockSpec((tm, D_MODEL), lambda i: (i, 0)),
            pl.BlockSpec((1, D_MODEL), lambda i: (0, 0)),
            pl.BlockSpec((D_MODEL, ROUTER_W), lambda i: (0, 0)),
            pl.BlockSpec((1, ROUTER_W), lambda i: (0, 0)),
        ],
        out_specs=[
            pl.BlockSpec((tm, HALF), lambda i: (i, 0)),
            pl.BlockSpec((tm, ROUTER_W), lambda i: (i, 0)),
            pl.BlockSpec((1, ROUTER_W), lambda i: (0, 0)),
        ],
        out_shape=[
            jax.ShapeDtypeStruct((t, HALF), U32),
            jax.ShapeDtypeStruct((t, ROUTER_W), F32),
            jax.ShapeDtypeStruct((1, ROUTER_W), F32),
        ],
        scratch_shapes=[pltpu.VMEM((1, ROUTER_W), F32)],
        compiler_params=_params(("arbitrary",)),
        name="router",
    )(h1, g_ffn, w_router, b_router)


def _plan(meta, counts, n_tiles):
    e1 = meta[:, 0].astype(jnp.int32)
    e2 = meta[:, 1].astype(jnp.int32)
    rank1 = meta[:, 2].astype(jnp.int32)
    rank2 = meta[:, 3].astype(jnp.int32)
    cnt = counts[0, N_EXPERT_GROUPS:N_EXPERT_GROUPS + N_EXPERTS].astype(jnp.int32)
    padded = ((cnt + ROW_TILE - 1) // ROW_TILE) * ROW_TILE
    ends = jnp.cumsum(padded)
    starts = ends - padded
    experts = jnp.arange(N_EXPERTS, dtype=jnp.int32)
    pos1 = rank1 + jnp.sum(jnp.where(e1[:, None] == experts, starts, 0), axis=1)
    pos2 = rank2 + jnp.sum(jnp.where(e2[:, None] == experts, starts, 0), axis=1)
    tile_start = jnp.arange(n_tiles, dtype=jnp.int32) * ROW_TILE
    tile_expert = jnp.sum((tile_start[:, None] >= ends[None, :]).astype(jnp.int32), axis=1)
    tile_expert = jnp.minimum(tile_expert, N_EXPERTS - 1)
    rows_left = jnp.sum(jnp.where(tile_expert[:, None] == experts, cnt + starts, 0), axis=1) - tile_start
    n_valid = jnp.clip(rows_left, 0, ROW_TILE).astype(jnp.int32)
    changed = jnp.concatenate([jnp.ones((1,), jnp.int32),
                               (tile_expert[1:] != tile_expert[:-1]).astype(jnp.int32)])
    pos = jnp.concatenate([pos1, pos2])
    half_rows = jnp.stack([2 * pos, 2 * pos + 1], axis=1).reshape(1, -1)
    return half_rows, tile_expert, n_valid, changed


def _sc_mesh():
    return plsc.VectorSubcoreMesh(core_axis_name="c", subcore_axis_name="s")


def _sc_scatter_rows(rows, idx, n_out):
    t, width = rows.shape
    steps = t // SC_WINDOW

    @pl.kernel(out_type=jax.ShapeDtypeStruct((n_out, width), rows.dtype), mesh=_sc_mesh(),
               scratch_types=[], name="moe_scatter")
    def scatter(rows_hbm, idx_hbm, out_hbm):
        def body(rows_vmem, idx_vmem):
            pltpu.sync_copy(rows_vmem, out_hbm.at[idx_vmem.at[0]])

        pltpu.emit_pipeline(
            body,
            grid=(2, steps),
            in_specs=[pl.BlockSpec((SC_WINDOW, width), lambda k, j: (j, 0)),
                      pl.BlockSpec((1, SC_WINDOW), lambda k, j: (0, k * steps + j))],
            out_specs=[],
            core_axis_name=("c", "s"),
            dimension_semantics=(pltpu.PARALLEL, pltpu.PARALLEL),
        )(rows_hbm, idx_hbm)

    return scatter(rows, idx)


def _sc_gather_rows(table, idx):
    m = idx.shape[1]
    width = table.shape[1]
    steps = m // (2 * SC_WINDOW)

    @pl.kernel(out_type=jax.ShapeDtypeStruct((m, width), table.dtype), mesh=_sc_mesh(),
               scratch_types=[], name="moe_gather")
    def gather(table_hbm, idx_hbm, out_hbm):
        def body(idx_vmem, out_vmem):
            pltpu.sync_copy(table_hbm.at[idx_vmem.at[0]], out_vmem)

        pltpu.emit_pipeline(
            body,
            grid=(2, steps),
            in_specs=[pl.BlockSpec((1, SC_WINDOW), lambda k, j: (0, k * steps + j))],
            out_specs=[pl.BlockSpec((SC_WINDOW, width), lambda k, j: (k * steps + j, 0))],
            core_axis_name=("c", "s"),
            dimension_semantics=(pltpu.PARALLEL, pltpu.PARALLEL),
        )(idx_hbm, out_hbm)

    return gather(table, idx)


def _experts_kernel(te_ref, nv_ref, new_ref, xs_ref, wg_ref, wu_ref, wd_ref, ys_ref,
                    wg_s, wu_s, wd_s):
    r = pl.program_id(0)
    n_valid = nv_ref[r]

    @pl.when(new_ref[r] == 1)
    def _():
        wg_s[...] = wg_ref[...].astype(BF16)
        wu_s[...] = wu_ref[...].astype(BF16)
        wd_s[...] = wd_ref[...].astype(BF16)

    @pl.when(n_valid > 0)
    def _():
        lo, hi = _unpack_rows(xs_ref[...])
        lo = lo.astype(BF16)
        hi = hi.astype(BF16)
        hg = _dot(lo, wg_s[:HALF, :]) + _dot(hi, wg_s[HALF:, :])
        hu = _dot(lo, wu_s[:HALF, :]) + _dot(hi, wu_s[HALF:, :])
        row = lax.broadcasted_iota(jnp.int32, (ROW_TILE, 1), 0)
        hid = jnp.where(row < n_valid, hg * _sigmoid(hg) * hu, 0.0).astype(BF16)
        ys_ref[...] = _pack_rows(_dot(hid, wd_s[...]))

    @pl.when(n_valid == 0)
    def _():
        ys_ref[...] = jnp.zeros_like(ys_ref)


def _experts(xs, tile_expert, n_valid, changed, w_gate, w_up, w_down, layer):
    n_rows = xs.shape[0]
    base = layer * N_EXPERTS
    w_spec = pl.BlockSpec((None, D_MODEL, D_EXPERT), lambda r, te, nv, new: (base + te[r], 0, 0))
    grid_spec = pltpu.PrefetchScalarGridSpec(
        num_scalar_prefetch=3,
        grid=(n_rows // ROW_TILE,),
        in_specs=[
            pl.BlockSpec((ROW_TILE, HALF), lambda r, te, nv, new: (r, 0)),
            w_spec, w_spec,
            pl.BlockSpec((None, D_EXPERT, D_MODEL), lambda r, te, nv, new: (base + te[r], 0, 0)),
        ],
        out_specs=pl.BlockSpec((ROW_TILE, HALF), lambda r, te, nv, new: (r, 0)),
        scratch_shapes=[
            pltpu.VMEM((D_MODEL, D_EXPERT), BF16),
            pltpu.VMEM((D_MODEL, D_EXPERT), BF16),
            pltpu.VMEM((D_EXPERT, D_MODEL), BF16),
        ],
    )
    return pl.pallas_call(
        _experts_kernel,
        grid_spec=grid_spec,
        out_shape=jax.ShapeDtypeStruct((n_rows, HALF), U32),
        compiler_params=_params(("arbitrary",)),
        name="experts",
    )(tile_expert, n_valid, changed, xs, w_gate, w_up, w_down)


def _ple_kernel(h_ref, y1_ref, y2_ref, meta_ref, p_ref, gp_ref, wg_ref, bg_ref, wp_ref, gf_ref,
                o_ref, *, final_norm):
    meta = meta_ref[...]
    lo1, hi1 = _unpack_rows(y1_ref[...])
    lo2, hi2 = _unpack_rows(y2_ref[...])
    w1 = meta[:, 4:5]
    w2 = meta[:, 5:6]
    moe = jnp.concatenate([w1 * lo1 + w2 * lo2, w1 * hi1 + w2 * hi2], axis=1)
    h = h_ref[...] + moe
    gate = _sigmoid(_dot(_rms(h, gp_ref[...]).astype(BF16), wg_ref[...]) + bg_ref[...])
    h = h + gate * _dot(p_ref[...].astype(BF16), wp_ref[...])
    o_ref[...] = _rms(h, gf_ref[...]) if final_norm else h


def _ple(h1, yg, meta, p2, g_ple, w_gate, b_gate, w_proj, g_final, final_norm, tm):
    t = h1.shape[0]
    steps = t // tm
    return pl.pallas_call(
        functools.partial(_ple_kernel, final_norm=final_norm),
        grid=(steps,),
        in_specs=[
            pl.BlockSpec((tm, D_MODEL), lambda i: (i, 0)),
            pl.BlockSpec((tm, HALF), lambda i: (i, 0)),
            pl.BlockSpec((tm, HALF), lambda i: (i + steps, 0)),
            pl.BlockSpec((tm, ROUTER_W), lambda i: (i, 0)),
            pl.BlockSpec((tm, D_PLE), lambda i: (i, 0)),
            pl.BlockSpec((1, D_MODEL), lambda i: (0, 0)),
            pl.BlockSpec((D_MODEL, D_MODEL), lambda i: (0, 0)),
            pl.BlockSpec((1, D_MODEL), lambda i: (0, 0)),
            pl.BlockSpec((D_PLE, D_MODEL), lambda i: (0, 0)),
            pl.BlockSpec((1, D_MODEL), lambda i: (0, 0)),
        ],
        out_specs=pl.BlockSpec((tm, D_MODEL), lambda i: (i, 0)),
        out_shape=jax.ShapeDtypeStruct((t, D_MODEL), F32),
        compiler_params=_params(("parallel",)),
        name="ple",
    )(h1, yg, yg, meta, p2, g_ple, w_gate, b_gate, w_proj, g_final)


def kernel(x, p, g_mix, w_in, pool_w, pool_scale, ssm_a_re, ssm_a_im, ssm_log_dt, ssm_b_re,
           ssm_b_im, ssm_c_re, ssm_c_im, ssm_d, glu_w, glu_b, w_out, g_ffn, router_grp_w,
           router_grp_b, router_exp_w, router_exp_b, exp_w_gate, exp_w_up, exp_w_down, g_ple,
           ple_gate_w, ple_gate_b, ple_proj_w, g_final):
    bsz, seq, dm = x.shape
    depth = g_mix.shape[0]
    t = bsz * seq
    seq_chunks = seq // CHUNK
    nch = t // CHUNK
    tm = 512
    n_sorted = 2 * t + N_EXPERTS * ROW_TILE
    w_gate_all = exp_w_gate.reshape(depth * N_EXPERTS, dm, D_EXPERT)
    w_up_all = exp_w_up.reshape(depth * N_EXPERTS, dm, D_EXPERT)
    w_down_all = exp_w_down.reshape(depth * N_EXPERTS, D_EXPERT, dm)

    h = x.reshape(t, dm)
    for i in range(depth):
        zp, zs = _in_proj(h, g_mix[i][None], w_in[i].astype(BF16), tm)
        a = _pool(zp.reshape(bsz, seq, D_POOL), pool_w[i], pool_scale[i][None])
        ug = zs.reshape(nch, CHUNK, N_SSM_GROUPS, SSM_GROUP)
        ug = jnp.transpose(ug, (2, 0, 1, 3)).reshape(N_SSM_GROUPS, nch, CHUNK_W)
        yg = _ssm(ug, ssm_a_re[i], ssm_a_im[i], ssm_log_dt[i], ssm_b_re[i], ssm_b_im[i],
                  ssm_c_re[i], ssm_c_im[i], ssm_d[i], seq_chunks)
        y = yg.reshape(N_SSM_GROUPS, nch, CHUNK, SSM_GROUP)
        y = jnp.transpose(y, (1, 2, 0, 3)).reshape(t, D_SSM)
        h = _mix_out(h, a.reshape(t, D_POOL), y, glu_w[i].astype(BF16), glu_b[i][None],
                     w_out[i].astype(BF16), tm)

        w_router = jnp.concatenate(
            [router_grp_w[i],
             jnp.transpose(router_exp_w[i], (1, 0, 2)).reshape(dm, N_EXPERTS),
             jnp.zeros((dm, ROUTER_W - N_EXPERT_GROUPS - N_EXPERTS), F32)], axis=1)
        b_router = jnp.concatenate(
            [router_grp_b[i], router_exp_b[i].reshape(N_EXPERTS),
             jnp.zeros((ROUTER_W - N_EXPERT_GROUPS - N_EXPERTS,), F32)])[None]
        vp, meta, counts = _router(h, g_ffn[i][None], w_router, b_router, tm)
        idx, tile_expert, n_valid, changed = _plan(meta, counts, n_sorted // ROW_TILE)
        xs = _sc_scatter_rows(vp.reshape(2 * t, SC_ROW), idx, 2 * n_sorted).reshape(n_sorted, HALF)
        ys = _experts(xs, tile_expert, n_valid, changed, w_gate_all, w_up_all, w_down_all, i)
        yg2 = _sc_gather_rows(ys.reshape(2 * n_sorted, SC_ROW), idx).reshape(2 * t, HALF)
        h = _ple(h, yg2, meta, p[i].reshape(t, D_PLE), g_ple[i][None],
                 ple_gate_w[i].astype(BF16), ple_gate_b[i][None], ple_proj_w[i].astype(BF16),
                 g_final[None], i == depth - 1, tm)
    return h.reshape(bsz, seq, dm)
```

```python
import functools
import math

import numpy as np
import jax
import jax.numpy as jnp
from jax import lax
from jax.experimental import pallas as pl
from jax.experimental.pallas import tpu as pltpu
from jax.experimental.pallas import tpu_sc as plsc

F32 = jnp.float32
BF16 = jnp.bfloat16
U32 = jnp.uint32

D_MODEL = 1024
D_POOL = 512
D_SSM = 512
POOL_WINDOWS = (2, 4, 8, 16)
POOL_GROUP = 128
SSM_GROUP = 16
N_SSM_GROUPS = 32
SSM_STATE = 64
N_EXPERT_GROUPS = 4
EXPERTS_PER_GROUP = 8
N_EXPERTS = N_EXPERT_GROUPS * EXPERTS_PER_GROUP
D_EXPERT = 256
D_PLE = 256
RMS_EPS = 1e-6

LANES = 128
CHUNK = 32
CHUNK_W = CHUNK * SSM_GROUP
T_SUB = 8
S_TILE = 8
C_TILE = 128
ROUTER_W = LANES
HALF = D_MODEL // 2
ROW_TILE = 256
SC_WINDOW = 128
SC_ROW = HALF // 2
META_ROWS = 8
VMEM_LIMIT = 56 * 1024 * 1024


def _dot(a, b):
    return jnp.dot(a, b, preferred_element_type=F32)


def _dot_nt(a, b):
    return lax.dot_general(a, b, (((1,), (1,)), ((), ())), preferred_element_type=F32)


def _dot_tn(a, b):
    return lax.dot_general(a, b, (((0,), (0,)), ((), ())), preferred_element_type=F32)


def _rms(x, g):
    return x * lax.rsqrt(jnp.mean(x * x, axis=-1, keepdims=True) + RMS_EPS) * g


def _sigmoid(x):
    return 1.0 / (1.0 + jnp.exp(-x))


def _params(sem):
    return pltpu.CompilerParams(dimension_semantics=sem, vmem_limit_bytes=VMEM_LIMIT)


def _rows_by_time(ref):
    return jnp.concatenate([ref[:, j, :] for j in range(S_TILE)], axis=0)


def _in_proj_kernel(x_ref, g_ref, wp_ref, wst_ref, zp_ref, ut_ref):
    nc = x_ref.shape[0]
    u = _rms(_rows_by_time(x_ref), g_ref[...]).astype(BF16)
    zp = _dot(u, wp_ref[...])
    zt = _dot_nt(wst_ref[...], u).astype(BF16)
    for j in range(S_TILE):
        zp_ref[:, j, :] = zp[j * nc:(j + 1) * nc, :]
        ut_ref[:, j, :, :] = zt[:, j * nc:(j + 1) * nc].reshape(N_SSM_GROUPS, SSM_GROUP, nc)


def _in_proj(x3, g_mix, w_pool, w_ssm_t):
    nch = x3.shape[0]
    return pl.pallas_call(
        _in_proj_kernel,
        grid=(nch // C_TILE, CHUNK // S_TILE),
        in_specs=[
            pl.BlockSpec((C_TILE, S_TILE, D_MODEL), lambda c, s: (c, s, 0)),
            pl.BlockSpec((1, D_MODEL), lambda c, s: (0, 0)),
            pl.BlockSpec((D_MODEL, D_POOL), lambda c, s: (0, 0)),
            pl.BlockSpec((D_SSM, D_MODEL), lambda c, s: (0, 0)),
        ],
        out_specs=[
            pl.BlockSpec((C_TILE, S_TILE, D_POOL), lambda c, s: (c, s, 0)),
            pl.BlockSpec((N_SSM_GROUPS, S_TILE, SSM_GROUP, C_TILE), lambda c, s: (0, s, 0, c)),
        ],
        out_shape=[
            jax.ShapeDtypeStruct((nch, CHUNK, D_POOL), F32),
            jax.ShapeDtypeStruct((N_SSM_GROUPS, CHUNK, SSM_GROUP, nch), BF16),
        ],
        compiler_params=_params(("parallel", "parallel")),
        name="in_proj",
    )(x3, g_mix, w_pool, w_ssm_t)


def _shift_rows(x, d, row, n):
    if d == 0:
        return x
    r = pltpu.roll(x, d % n, 0)
    if d > 0:
        return jnp.where(row >= d, r, 0.0)
    return jnp.where(row < n + d, r, 0.0)


def _pool_kernel(z_ref, w_ref, sc_ref, o_ref):
    n = z_ref.shape[0]
    gi = pl.program_id(1)
    row = lax.broadcasted_iota(jnp.int32, (n, 1), 0)

    for k, w in enumerate(POOL_WINDOWS):
        @pl.when(gi == k)
        def _(w=w):
            x = z_ref[...]
            half = w // 2
            pd, pu, span = x, x, 1
            while span < half:
                pd = pd + _shift_rows(pd, span, row, n)
                pu = pu + _shift_rows(pu, -span, row, n)
                span *= 2
            total = _shift_rows(pd, 1, row, n) + pu
            lo = jnp.maximum(row - half, 0)
            hi = jnp.minimum(row + half, n)
            cnt = (hi - lo).astype(F32)
            diff = (total / cnt - x).astype(BF16)
            o_ref[...] = _dot(diff, w_ref[...].astype(BF16)) * sc_ref[...]


def _pool(zp3, pool_w, pool_scale):
    b, s, _ = zp3.shape
    return pl.pallas_call(
        _pool_kernel,
        grid=(b, len(POOL_WINDOWS)),
        in_specs=[
            pl.BlockSpec((None, s, POOL_GROUP), lambda i, g: (i, 0, g)),
            pl.BlockSpec((None, POOL_GROUP, POOL_GROUP), lambda i, g: (g, 0, 0)),
            pl.BlockSpec((1, POOL_GROUP), lambda i, g: (0, g)),
        ],
        out_specs=pl.BlockSpec((None, s, POOL_GROUP), lambda i, g: (i, 0, g)),
        out_shape=jax.ShapeDtypeStruct((b, s, D_POOL), F32),
        compiler_params=_params(("parallel", "parallel")),
        name="pool",
    )(zp3, pool_w, pool_scale)


def _expand_consts():
    time = np.arange(CHUNK_W) // SSM_GROUP
    def onehot(e):
        m = np.zeros((CHUNK_W, LANES), np.float32)
        m[np.arange(CHUNK_W), e] = 1.0
        return m
    return np.stack([
        onehot(CHUNK - 1 - time),
        onehot(time),
        onehot(time + 1),
        onehot(CHUNK - time),
    ])


def _cmul_packed(x, p, q):
    return x * p + pltpu.roll(x, LANES // 2, 1) * q


def _ssm_kernel(u_ref, are_ref, aim_ref, ldt_ref, bt_re_ref, bt_im_ref, ccr_ref, ct_re_ref,
                ct_im_ref, d_ref, exp_ref, y_ref, *, seq_chunks):
    nch = u_ref.shape[-1]
    half = LANES // 2
    lane = lax.broadcasted_iota(jnp.int32, (1, LANES), 1)
    lo_half = lane < half

    def direction(di):
        a_re = are_ref[di]
        a_im = aim_ref[di]
        dt = jnp.exp(ldt_ref[di])
        mag = jnp.exp(a_re * dt)
        ang = a_im * dt
        lam = jnp.where(lo_half, mag * jnp.cos(ang), mag * jnp.sin(ang))
        lb_re = mag * jnp.cos(ang)
        lb_im = mag * jnp.sin(ang)
        den = a_re * a_re + a_im * a_im
        f_re = ((lb_re - 1.0) * a_re + lb_im * a_im) / den
        f_im = (lb_im * a_re - (lb_re - 1.0) * a_im) / den
        return lam, f_re, f_im

    def power_table(lam):
        e = lax.broadcasted_iota(jnp.int32, (LANES, 1), 0)
        tab = jnp.where(lo_half, 1.0, 0.0) * jnp.ones((LANES, 1), F32)
        sq = lam
        for k in range(7):
            p = jnp.where(lo_half, sq, pltpu.roll(sq, half, 1))
            q = jnp.where(lo_half, -pltpu.roll(sq, half, 1), sq)
            tab = jnp.where(((e >> k) & 1) == 1, _cmul_packed(tab, p, q), tab)
            sq = _cmul_packed(sq, p, q)
        return tab

    def tile_rows(x16):
        return jnp.broadcast_to(x16[None], (CHUNK, SSM_GROUP, LANES)).reshape(CHUNK_W, LANES)

    def expanded(tab, which, v_re, v_im, conj_sign):
        lexp = _dot(exp_ref[which], tab.astype(BF16))
        if conj_sign > 0:
            p = jnp.where(lo_half, v_re, v_re)
            q = jnp.where(lo_half, -v_im, v_im)
        else:
            p = jnp.where(lo_half, v_re, -v_re)
            q = jnp.where(lo_half, -v_im, -v_im)
        return lexp * tile_rows(p) + pltpu.roll(lexp, half, 1) * tile_rows(q)

    lam_f, ff_re, ff_im = direction(0)
    lam_b, fb_re, fb_im = direction(1)
    tab_f = power_table(lam_f)
    tab_b = power_table(lam_b)

    def bbar(bt_re, bt_im, f_re, f_im):
        return bt_re * f_re - bt_im * f_im, bt_re * f_im + bt_im * f_re

    bf_re, bf_im = bbar(bt_re_ref[0], bt_im_ref[0], ff_re, ff_im)
    bb_re, bb_im = bbar(bt_re_ref[1], bt_im_ref[1], fb_re, fb_im)

    pb1 = expanded(tab_f, 0, bf_re, bf_im, 1)
    pb2 = expanded(tab_b, 1, bb_re, bb_im, 1)
    pb3 = expanded(tab_b, 2, bb_re, bb_im, 1)
    ft_f = expanded(tab_f, 2, ct_re_ref[0], ct_im_ref[0], -1)
    ft_b = expanded(tab_b, 3, ct_re_ref[1], ct_im_ref[1], -1)

    row_w = lax.broadcasted_iota(jnp.int32, (CHUNK_W, 1), 0)
    last_blk = row_w >= CHUNK_W - SSM_GROUP
    pb2_lag0 = jnp.where(last_blk, pltpu.roll(pb2, CHUNK_W - SSM_GROUP, 0), 0.0)
    ccr_f = ccr_ref[0].astype(BF16)
    ccr_b = ccr_ref[1].astype(BF16)
    r_lo = _dot_nt(ccr_f, pb1.astype(BF16)) + _dot_nt(ccr_b, pb2_lag0.astype(BF16))
    co = lax.broadcasted_iota(jnp.int32, (SSM_GROUP, CHUNK_W), 0)
    col = lax.broadcasted_iota(jnp.int32, (SSM_GROUP, CHUNK_W), 1)
    r_lo = r_lo + jnp.where(col == CHUNK_W - SSM_GROUP + co, d_ref[...], 0.0)
    r_hi = _dot_nt(ccr_b, pb3.astype(BF16))
    r_t = jnp.concatenate([r_lo, r_hi], axis=1)
    g_t = jnp.concatenate(
        [pltpu.roll(r_t, SSM_GROUP * (tl + 1), 1) for tl in range(T_SUB)], axis=0
    ).astype(BF16)

    u = u_ref[...].reshape(CHUNK_W, nch)
    e_mat = jnp.concatenate([pb1, pb2], axis=1).astype(BF16)
    xend = _dot_tn(e_mat, u)
    lanec = lax.broadcasted_iota(jnp.int32, (1, nch), 1) % seq_chunks
    ns = SSM_STATE

    def scan(re, im, tab, forward):
        lam_col = jnp.transpose(tab[CHUNK:CHUNK + 8, :])[:, 0:1]
        a, b = lam_col[:ns], lam_col[ns:]
        n_steps = int(math.log2(seq_chunks))

        def shifted(v, d):
            if forward:
                return jnp.where(lanec >= d, pltpu.roll(v, d, 1), 0.0)
            return jnp.where(lanec < seq_chunks - d, pltpu.roll(v, nch - d, 1), 0.0)

        for k in range(n_steps):
            sr, si = shifted(re, 1 << k), shifted(im, 1 << k)
            re, im = re + (sr * a - si * b), im + (sr * b + si * a)
            a, b = a * a - b * b, 2.0 * a * b
        return shifted(re, 1), shifted(im, 1)

    f_re, f_im = scan(xend[:ns], xend[ns:2 * ns], tab_f, True)
    b_re, b_im = scan(xend[2 * ns:3 * ns], xend[3 * ns:], tab_b, False)
    xin = jnp.concatenate([f_re, f_im, b_re, b_im], axis=0).astype(BF16)
    f_t = jnp.concatenate([ft_f, ft_b], axis=1).astype(BF16)

    toeplitz = jnp.concatenate(
        [g_t[:, CHUNK_W - LANES * th:2 * CHUNK_W - LANES * th] for th in range(CHUNK // T_SUB)],
        axis=0)
    y_t = _dot(toeplitz, u) + _dot(f_t, xin)
    y_ref[...] = y_t.reshape(CHUNK, SSM_GROUP, nch).astype(y_ref.dtype)


def _ssm(ut, a_re, a_im, log_dt, b_re, b_im, c_re, c_im, d, seq_chunks):
    g, _, _, nch = ut.shape
    n = SSM_STATE

    def lane_vec(a):
        a = jnp.transpose(a, (1, 0, 2))
        return jnp.concatenate([a, a], axis=-1)[:, :, None, :]

    are = lane_vec(a_re)
    aim = lane_vec(a_im)
    ldt = lane_vec(jnp.broadcast_to(log_dt[..., None], (2, g, n)))

    def bt(b):
        b = jnp.transpose(b, (1, 0, 3, 2))
        return jnp.concatenate([b, b], axis=-1)

    def ct(c):
        c = jnp.transpose(c, (1, 0, 2, 3))
        return jnp.concatenate([c, c], axis=-1)

    ccr = jnp.concatenate([jnp.transpose(c_re, (1, 0, 2, 3)),
                           -jnp.transpose(c_im, (1, 0, 2, 3))], axis=-1)
    d_col = d.reshape(g, SSM_GROUP, 1)
    exp_c = jnp.asarray(_expand_consts(), BF16)

    vec_spec = pl.BlockSpec((None, 2, 1, LANES), lambda i: (i, 0, 0, 0))
    mat_spec = pl.BlockSpec((None, 2, SSM_GROUP, LANES), lambda i: (i, 0, 0, 0))
    return pl.pallas_call(
        functools.partial(_ssm_kernel, seq_chunks=seq_chunks),
        grid=(g,),
        in_specs=[
            pl.BlockSpec((None, CHUNK, SSM_GROUP, nch), lambda i: (i, 0, 0, 0)),
            vec_spec, vec_spec, vec_spec,
            mat_spec, mat_spec, mat_spec, mat_spec, mat_spec,
            pl.BlockSpec((None, SSM_GROUP, 1), lambda i: (i, 0, 0)),
            pl.BlockSpec((4, CHUNK_W, LANES), lambda i: (0, 0, 0)),
        ],
        out_specs=pl.BlockSpec((None, CHUNK, SSM_GROUP, nch), lambda i: (i, 0, 0, 0)),
        out_shape=jax.ShapeDtypeStruct((g, CHUNK, SSM_GROUP, nch), BF16),
        compiler_params=_params(("parallel",)),
        name="ssm",
    )(ut, are, aim, ldt, bt(b_re), bt(b_im), ccr, ct(c_re), ct(c_im), d_col, exp_c)


def _mix_out_kernel(x_ref, a_ref, yt_ref, gwt_ref, gb_ref, wo_ref, h_ref):
    nc = x_ref.shape[0]
    y = jnp.concatenate([yt_ref[:, j, :, :].reshape(D_SSM, nc) for j in range(S_TILE)],
                        axis=1).astype(F32)
    z = 0.5 * y * (1.0 + jnp.tanh(math.sqrt(2.0 / math.pi) * (y + 0.044715 * (y * y * y))))
    gate = _sigmoid(_dot(gwt_ref[...], z.astype(BF16)) + gb_ref[...])
    s = (z * gate).astype(BF16)
    h = (_rows_by_time(x_ref) + _dot(_rows_by_time(a_ref).astype(BF16), wo_ref[:D_POOL, :])
         + _dot_tn(s, wo_ref[D_POOL:, :]))
    for j in range(S_TILE):
        h_ref[:, j, :] = h[j * nc:(j + 1) * nc, :]


def _mix_out(x3, a3, yt, glu_w_t, glu_b_col, w_out):
    nch = x3.shape[0]
    return pl.pallas_call(
        _mix_out_kernel,
        grid=(nch // C_TILE, CHUNK // S_TILE),
        in_specs=[
            pl.BlockSpec((C_TILE, S_TILE, D_MODEL), lambda c, t: (c, t, 0)),
            pl.BlockSpec((C_TILE, S_TILE, D_POOL), lambda c, t: (c, t, 0)),
            pl.BlockSpec((N_SSM_GROUPS, S_TILE, SSM_GROUP, C_TILE), lambda c, t: (0, t, 0, c)),
            pl.BlockSpec((D_SSM, D_SSM), lambda c, t: (0, 0)),
            pl.BlockSpec((D_SSM, 1), lambda c, t: (0, 0)),
            pl.BlockSpec((D_MODEL, D_MODEL), lambda c, t: (0, 0)),
        ],
        out_specs=pl.BlockSpec((C_TILE, S_TILE, D_MODEL), lambda c, t: (c, t, 0)),
        out_shape=jax.ShapeDtypeStruct((nch, CHUNK, D_MODEL), F32),
        compiler_params=_params(("parallel", "parallel")),
        name="mix_out",
    )(x3, a3, yt, glu_w_t, glu_b_col, w_out)


def _pack_rows(x):
    b = lax.bitcast_convert_type(x.astype(BF16).astype(F32), U32)
    return (b[:, :HALF] & jnp.uint32(0xFFFF0000)) | (b[:, HALF:] >> 16)


def _unpack_rows(w):
    lo = lax.bitcast_convert_type(w & jnp.uint32(0xFFFF0000), F32)
    hi = lax.bitcast_convert_type(w << 16, F32)
    return lo, hi


def _route(v32, wr_ref, br_ref):
    logits = jnp.dot(v32, wr_ref[...], preferred_element_type=F32,
                     precision=lax.Precision.HIGHEST) + br_ref[...]
    lane = lax.broadcasted_iota(jnp.int32, logits.shape, 1)
    neg = -jnp.inf
    lg = jnp.where(lane < N_EXPERT_GROUPS, logits, neg)
    mg = jnp.max(lg, axis=1, keepdims=True)
    grp_p = 1.0 / jnp.sum(jnp.exp(lg - mg), axis=1, keepdims=True)
    grp_idx = jnp.min(jnp.where(lg == mg, lane, ROUTER_W), axis=1, keepdims=True)
    first = N_EXPERT_GROUPS + grp_idx * EXPERTS_PER_GROUP
    sel = (lane >= first) & (lane < first + EXPERTS_PER_GROUP)
    le = jnp.where(sel, logits, neg)
    m1 = jnp.max(le, axis=1, keepdims=True)
    i1 = jnp.min(jnp.where(le == m1, lane, ROUTER_W), axis=1, keepdims=True)
    z = jnp.sum(jnp.exp(le - m1), axis=1, keepdims=True)
    le2 = jnp.where(lane == i1, neg, le)
    m2 = jnp.max(le2, axis=1, keepdims=True)
    i2 = jnp.min(jnp.where(le2 == m2, lane, ROUTER_W), axis=1, keepdims=True)
    p1 = 1.0 / z
    p2 = jnp.exp(m2 - m1) / z
    tot = p1 + p2
    return i1, i2, grp_p * (p1 / tot), grp_p * (p2 / tot)


def _split_planes(packed, ref):
    ref[0] = packed[:, :SC_ROW]
    ref[1] = packed[:, SC_ROW:]


def _router_kernel(h_ref, g_ref, wr_ref, br_ref, vp_ref, meta_ref, meta_t_ref, cnt_ref, carry_ref):
    @pl.when(pl.program_id(0) == 0)
    def _():
        carry_ref[...] = jnp.zeros_like(carry_ref)

    v32 = _rms(h_ref[...], g_ref[...])
    _split_planes(_pack_rows(v32), vp_ref)
    i1, i2, w1, w2 = _route(v32, wr_ref, br_ref)
    tm = v32.shape[0]
    lane = lax.broadcasted_iota(jnp.int32, (tm, ROUTER_W), 1)
    onehot = jnp.where(lane == i1, 1.0, jnp.where(lane == i2, 1.0, 0.0))
    r = lax.broadcasted_iota(jnp.int32, (tm, tm), 0)
    c = lax.broadcasted_iota(jnp.int32, (tm, tm), 1)
    below = jnp.where(c < r, 1.0, 0.0).astype(BF16)
    before = _dot(below, onehot.astype(BF16)) + carry_ref[...]
    rank1 = jnp.sum(jnp.where(lane == i1, before, 0.0), axis=1, keepdims=True)
    rank2 = jnp.sum(jnp.where(lane == i2, before, 0.0), axis=1, keepdims=True)
    carry = carry_ref[...] + jnp.sum(onehot, axis=0, keepdims=True)
    carry_ref[...] = carry
    cnt_ref[...] = carry
    e1 = (i1 - N_EXPERT_GROUPS).astype(F32)
    e2 = (i2 - N_EXPERT_GROUPS).astype(F32)
    meta = jnp.where(lane == 0, e1, jnp.where(lane == 1, e2, jnp.where(
        lane == 2, rank1, jnp.where(lane == 3, rank2, jnp.where(
            lane == 4, w1, jnp.where(lane == 5, w2, 0.0))))))
    meta_ref[...] = meta
    meta_t_ref[...] = jnp.transpose(meta)[:META_ROWS, :]


def _router(h1, g_ffn, w_router, b_router, tm):
    t = h1.shape[0]
    return pl.pallas_call(
        _router_kernel,
        grid=(t // tm,),
        in_specs=[
            pl.BlockSpec((tm, D_MODEL), lambda i: (i, 0)),
            pl.BlockSpec((1, D_MODEL), lambda i: (0, 0)),
            pl.BlockSpec((D_MODEL, ROUTER_W), lambda i: (0, 0)),
            pl.BlockSpec((1, ROUTER_W), lambda i: (0, 0)),
        ],
        out_specs=[
            pl.BlockSpec((2, tm, SC_ROW), lambda i: (0, i, 0)),
            pl.BlockSpec((tm, ROUTER_W), lambda i: (i, 0)),
            pl.BlockSpec((META_ROWS, tm), lambda i: (0, i)),
            pl.BlockSpec((1, ROUTER_W), lambda i: (0, 0)),
        ],
        out_shape=[
            jax.ShapeDtypeStruct((2, t, SC_ROW), U32),
            jax.ShapeDtypeStruct((t, ROUTER_W), F32),
            jax.ShapeDtypeStruct((META_ROWS, t), F32),
            jax.ShapeDtypeStruct((1, ROUTER_W), F32),
        ],
        scratch_shapes=[pltpu.VMEM((1, ROUTER_W), F32)],
        compiler_params=_params(("arbitrary",)),
        name="router",
    )(h1, g_ffn, w_router, b_router)


def _plan(meta_t, counts, n_tiles):
    e1 = meta_t[0].astype(jnp.int32)
    e2 = meta_t[1].astype(jnp.int32)
    rank1 = meta_t[2].astype(jnp.int32)
    rank2 = meta_t[3].astype(jnp.int32)
    cnt = counts[0, N_EXPERT_GROUPS:N_EXPERT_GROUPS + N_EXPERTS].astype(jnp.int32)
    padded = ((cnt + ROW_TILE - 1) // ROW_TILE) * ROW_TILE
    ends = jnp.cumsum(padded)
    starts = ends - padded
    experts = jnp.arange(N_EXPERTS, dtype=jnp.int32)
    pos1 = rank1 + jnp.sum(jnp.where(e1[None, :] == experts[:, None], starts[:, None], 0), axis=0)
    pos2 = rank2 + jnp.sum(jnp.where(e2[None, :] == experts[:, None], starts[:, None], 0), axis=0)
    tile_start = jnp.arange(n_tiles, dtype=jnp.int32) * ROW_TILE
    tile_expert = jnp.sum((tile_start[:, None] >= ends[None, :]).astype(jnp.int32), axis=1)
    tile_expert = jnp.minimum(tile_expert, N_EXPERTS - 1)
    rows_left = jnp.sum(jnp.where(tile_expert[:, None] == experts, cnt + starts, 0), axis=1) - tile_start
    n_valid = jnp.clip(rows_left, 0, ROW_TILE).astype(jnp.int32)
    changed = jnp.concatenate([jnp.ones((1,), jnp.int32),
                               (tile_expert[1:] != tile_expert[:-1]).astype(jnp.int32)])
    plane = n_tiles * ROW_TILE
    half_rows = jnp.concatenate([pos1, pos1 + plane, pos2, pos2 + plane])[None]
    return half_rows, tile_expert, n_valid, changed


def _sc_mesh():
    return plsc.VectorSubcoreMesh(core_axis_name="c", subcore_axis_name="s")


def _sc_scatter_rows(rows, idx, n_out):
    t, width = rows.shape
    steps = t // SC_WINDOW

    @pl.kernel(out_type=jax.ShapeDtypeStruct((n_out, width), rows.dtype), mesh=_sc_mesh(),
               scratch_types=[], name="moe_scatter")
    def scatter(rows_hbm, idx_hbm, out_hbm):
        def body(rows_vmem, idx_vmem):
            pltpu.sync_copy(rows_vmem, out_hbm.at[idx_vmem.at[0]])

        pltpu.emit_pipeline(
            body,
            grid=(2, steps),
            in_specs=[pl.BlockSpec((SC_WINDOW, width), lambda k, j: (j, 0)),
                      pl.BlockSpec((1, SC_WINDOW), lambda k, j: (0, k * steps + j))],
            out_specs=[],
            core_axis_name=("c", "s"),
            dimension_semantics=(pltpu.PARALLEL, pltpu.PARALLEL),
        )(rows_hbm, idx_hbm)

    return scatter(rows, idx)


def _sc_gather_rows(table, idx):
    m = idx.shape[1]
    width = table.shape[1]
    steps = m // (2 * SC_WINDOW)

    @pl.kernel(out_type=jax.ShapeDtypeStruct((m, width), table.dtype), mesh=_sc_mesh(),
               scratch_types=[], name="moe_gather")
    def gather(table_hbm, idx_hbm, out_hbm):
        def body(idx_vmem, out_vmem):
            pltpu.sync_copy(table_hbm.at[idx_vmem.at[0]], out_vmem)

        pltpu.emit_pipeline(
            body,
            grid=(2, steps),
            in_specs=[pl.BlockSpec((1, SC_WINDOW), lambda k, j: (0, k * steps + j))],
            out_specs=[pl.BlockSpec((SC_WINDOW, width), lambda k, j: (k * steps + j, 0))],
            core_axis_name=("c", "s"),
            dimension_semantics=(pltpu.PARALLEL, pltpu.PARALLEL),
        )(idx_hbm, out_hbm)

    return gather(table, idx)


def _experts_kernel(te_ref, nv_ref, new_ref, xs_ref, wg_ref, wu_ref, wd_ref, ys_ref,
                    wg_s, wu_s, wd_s):
    r = pl.program_id(0)
    n_valid = nv_ref[r]

    @pl.when(new_ref[r] == 1)
    def _():
        wg_s[...] = wg_ref[...].astype(BF16)
        wu_s[...] = wu_ref[...].astype(BF16)
        wd_s[...] = wd_ref[...].astype(BF16)

    @pl.when(n_valid > 0)
    def _():
        parts = [p.astype(BF16) for p in _unpack_rows(xs_ref[0]) + _unpack_rows(xs_ref[1])]
        cols = (0, 2 * SC_ROW, SC_ROW, 3 * SC_ROW)
        hg = sum(_dot(p, wg_s[c:c + SC_ROW, :]) for p, c in zip(parts, cols))
        hu = sum(_dot(p, wu_s[c:c + SC_ROW, :]) for p, c in zip(parts, cols))
        row = lax.broadcasted_iota(jnp.int32, (ROW_TILE, 1), 0)
        hid = jnp.where(row < n_valid, hg * _sigmoid(hg) * hu, 0.0).astype(BF16)
        _split_planes(_pack_rows(_dot(hid, wd_s[...])), ys_ref)

    @pl.when(n_valid == 0)
    def _():
        ys_ref[...] = jnp.zeros_like(ys_ref)


def _experts(xs, tile_expert, n_valid, changed, w_gate, w_up, w_down, layer):
    n_tiles = xs.shape[1] // ROW_TILE
    base = layer * N_EXPERTS
    w_spec = pl.BlockSpec((None, D_MODEL, D_EXPERT), lambda r, te, nv, new: (base + te[r], 0, 0))
    grid_spec = pltpu.PrefetchScalarGridSpec(
        num_scalar_prefetch=3,
        grid=(n_tiles,),
        in_specs=[
            pl.BlockSpec((2, ROW_TILE, SC_ROW), lambda r, te, nv, new: (0, r, 0)),
            w_spec, w_spec,
            pl.BlockSpec((None, D_EXPERT, D_MODEL), lambda r, te, nv, new: (base + te[r], 0, 0)),
        ],
        out_specs=pl.BlockSpec((2, ROW_TILE, SC_ROW), lambda r, te, nv, new: (0, r, 0)),
        scratch_shapes=[
            pltpu.VMEM((D_MODEL, D_EXPERT), BF16),
            pltpu.VMEM((D_MODEL, D_EXPERT), BF16),
            pltpu.VMEM((D_EXPERT, D_MODEL), BF16),
        ],
    )
    return pl.pallas_call(
        _experts_kernel,
        grid_spec=grid_spec,
        out_shape=jax.ShapeDtypeStruct((2, n_tiles * ROW_TILE, SC_ROW), U32),
        compiler_params=_params(("arbitrary",)),
        name="experts",
    )(tile_expert, n_valid, changed, xs, w_gate, w_up, w_down)


def _ple_kernel(h_ref, yg_ref, meta_ref, p_ref, gp_ref, wg_ref, bg_ref, wp_ref, gf_ref, o_ref, *,
                final_norm):
    meta = meta_ref[...]
    w1 = meta[:, 4:5]
    w2 = meta[:, 5:6]
    q0, q2 = (w1 * u + w2 * v for u, v in zip(_unpack_rows(yg_ref[0]), _unpack_rows(yg_ref[2])))
    q1, q3 = (w1 * u + w2 * v for u, v in zip(_unpack_rows(yg_ref[1]), _unpack_rows(yg_ref[3])))
    moe = jnp.concatenate([q0, q1, q2, q3], axis=1)
    h = h_ref[...] + moe
    gate = _sigmoid(_dot(_rms(h, gp_ref[...]).astype(BF16), wg_ref[...]) + bg_ref[...])
    h = h + gate * _dot(p_ref[...].astype(BF16), wp_ref[...])
    o_ref[...] = _rms(h, gf_ref[...]) if final_norm else h


def _ple(h1, yg, meta, p2, g_ple, w_gate, b_gate, w_proj, g_final, final_norm, tm):
    t = h1.shape[0]
    return pl.pallas_call(
        functools.partial(_ple_kernel, final_norm=final_norm),
        grid=(t // tm,),
        in_specs=[
            pl.BlockSpec((tm, D_MODEL), lambda i: (i, 0)),
            pl.BlockSpec((4, tm, SC_ROW), lambda i: (0, i, 0)),
            pl.BlockSpec((tm, ROUTER_W), lambda i: (i, 0)),
            pl.BlockSpec((tm, D_PLE), lambda i: (i, 0)),
            pl.BlockSpec((1, D_MODEL), lambda i: (0, 0)),
            pl.BlockSpec((D_MODEL, D_MODEL), lambda i: (0, 0)),
            pl.BlockSpec((1, D_MODEL), lambda i: (0, 0)),
            pl.BlockSpec((D_PLE, D_MODEL), lambda i: (0, 0)),
            pl.BlockSpec((1, D_MODEL), lambda i: (0, 0)),
        ],
        out_specs=pl.BlockSpec((tm, D_MODEL), lambda i: (i, 0)),
        out_shape=jax.ShapeDtypeStruct((t, D_MODEL), F32),
        compiler_params=_params(("parallel",)),
        name="ple",
    )(h1, yg, meta, p2, g_ple, w_gate, b_gate, w_proj, g_final)


def kernel(x, p, g_mix, w_in, pool_w, pool_scale, ssm_a_re, ssm_a_im, ssm_log_dt, ssm_b_re,
           ssm_b_im, ssm_c_re, ssm_c_im, ssm_d, glu_w, glu_b, w_out, g_ffn, router_grp_w,
           router_grp_b, router_exp_w, router_exp_b, exp_w_gate, exp_w_up, exp_w_down, g_ple,
           ple_gate_w, ple_gate_b, ple_proj_w, g_final):
    bsz, seq, dm = x.shape
    depth = g_mix.shape[0]
    t = bsz * seq
    seq_chunks = seq // CHUNK
    nch = t // CHUNK
    tm = 512
    n_sorted = 2 * t + N_EXPERTS * ROW_TILE
    w_gate_all = exp_w_gate.reshape(depth * N_EXPERTS, dm, D_EXPERT)
    w_up_all = exp_w_up.reshape(depth * N_EXPERTS, dm, D_EXPERT)
    w_down_all = exp_w_down.reshape(depth * N_EXPERTS, D_EXPERT, dm)

    h = x.reshape(t, dm)
    for i in range(depth):
        w_in_b = w_in[i].astype(BF16)
        zp, ut = _in_proj(h.reshape(nch, CHUNK, dm), g_mix[i][None], w_in_b[:, :D_POOL],
                          jnp.transpose(w_in_b[:, D_POOL:]))
        a = _pool(zp.reshape(bsz, seq, D_POOL), pool_w[i], pool_scale[i][None])
        yt = _ssm(ut, ssm_a_re[i], ssm_a_im[i], ssm_log_dt[i], ssm_b_re[i], ssm_b_im[i],
                  ssm_c_re[i], ssm_c_im[i], ssm_d[i], seq_chunks)
        h = _mix_out(h.reshape(nch, CHUNK, dm), a.reshape(nch, CHUNK, D_POOL), yt,
                     jnp.transpose(glu_w[i]).astype(BF16), glu_b[i][:, None],
                     w_out[i].astype(BF16)).reshape(t, dm)

        w_router = jnp.concatenate(
            [router_grp_w[i],
             jnp.transpose(router_exp_w[i], (1, 0, 2)).reshape(dm, N_EXPERTS),
             jnp.zeros((dm, ROUTER_W - N_EXPERT_GROUPS - N_EXPERTS), F32)], axis=1)
        b_router = jnp.concatenate(
            [router_grp_b[i], router_exp_b[i].reshape(N_EXPERTS),
             jnp.zeros((ROUTER_W - N_EXPERT_GROUPS - N_EXPERTS,), F32)])[None]
        vp, meta, meta_t, counts = _router(h, g_ffn[i][None], w_router, b_router, tm)
        idx, tile_expert, n_valid, changed = _plan(meta_t, counts, n_sorted // ROW_TILE)
        xs = _sc_scatter_rows(vp.reshape(2 * t, SC_ROW), idx, 2 * n_sorted)
        ys = _experts(xs.reshape(2, n_sorted, SC_ROW), tile_expert, n_valid, changed,
                      w_gate_all, w_up_all, w_down_all, i)
        yg2 = _sc_gather_rows(ys.reshape(2 * n_sorted, SC_ROW), idx).reshape(4, t, SC_ROW)
        h = _ple(h, yg2, meta, p[i].reshape(t, D_PLE), g_ple[i][None],
                 ple_gate_w[i].astype(BF16), ple_gate_b[i][None], ple_proj_w[i].astype(BF16),
                 g_final[None], i == depth - 1, tm)
    return h.reshape(bsz, seq, dm)
```

```python
import functools
import math

import numpy as np
import jax
import jax.numpy as jnp
from jax import lax
from jax.experimental import pallas as pl
from jax.experimental.pallas import tpu as pltpu
from jax.experimental.pallas import tpu_sc as plsc

F32 = jnp.float32
BF16 = jnp.bfloat16
U32 = jnp.uint32

D_MODEL = 1024
D_POOL = 512
D_SSM = 512
POOL_WINDOWS = (2, 4, 8, 16)
POOL_GROUP = 128
SSM_GROUP = 16
N_SSM_GROUPS = 32
SSM_STATE = 64
N_EXPERT_GROUPS = 4
EXPERTS_PER_GROUP = 8
N_EXPERTS = N_EXPERT_GROUPS * EXPERTS_PER_GROUP
D_EXPERT = 256
D_PLE = 256
RMS_EPS = 1e-6

LANES = 128
CHUNK = 32
CHUNK_W = CHUNK * SSM_GROUP
T_SUB = 8
S_TILE = 8
C_TILE = 128
ROUTER_W = LANES
HALF = D_MODEL // 2
ROW_TILE = 512
SC_WINDOW = 128
SC_ROW = HALF // 2
META_ROWS = 8
VMEM_LIMIT = 56 * 1024 * 1024


def _dot(a, b):
    return jnp.dot(a, b, preferred_element_type=F32)


def _dot_nt(a, b):
    return lax.dot_general(a, b, (((1,), (1,)), ((), ())), preferred_element_type=F32)


def _dot_tn(a, b):
    return lax.dot_general(a, b, (((0,), (0,)), ((), ())), preferred_element_type=F32)


def _rms(x, g=None):
    y = x * lax.rsqrt(jnp.mean(x * x, axis=-1, keepdims=True) + RMS_EPS)
    return y if g is None else y * g


def _sigmoid(x):
    return 1.0 / (1.0 + jnp.exp(-x))


def _params(sem):
    return pltpu.CompilerParams(dimension_semantics=sem, vmem_limit_bytes=VMEM_LIMIT)


def _rows_by_time(ref):
    return jnp.concatenate([ref[:, j, :] for j in range(S_TILE)], axis=0)


def _in_proj_kernel(x_ref, wp_ref, wst_ref, zp_ref, ut_ref):
    nc = x_ref.shape[0]
    u = _rms(_rows_by_time(x_ref)).astype(BF16)
    zp = _dot(u, wp_ref[...])
    zt = _dot_nt(wst_ref[...], u).astype(BF16)
    for j in range(S_TILE):
        zp_ref[:, j, :] = zp[j * nc:(j + 1) * nc, :]
        ut_ref[:, j, :, :] = zt[:, j * nc:(j + 1) * nc].reshape(N_SSM_GROUPS, SSM_GROUP, nc)


def _in_proj(x3, w_pool, w_ssm_t):
    nch = x3.shape[0]
    return pl.pallas_call(
        _in_proj_kernel,
        grid=(nch // C_TILE, CHUNK // S_TILE),
        in_specs=[
            pl.BlockSpec((C_TILE, S_TILE, D_MODEL), lambda c, s: (c, s, 0)),
            pl.BlockSpec((D_MODEL, D_POOL), lambda c, s: (0, 0)),
            pl.BlockSpec((D_SSM, D_MODEL), lambda c, s: (0, 0)),
        ],
        out_specs=[
            pl.BlockSpec((C_TILE, S_TILE, D_POOL), lambda c, s: (c, s, 0)),
            pl.BlockSpec((N_SSM_GROUPS, None, S_TILE, SSM_GROUP, C_TILE), lambda c, s: (0, c, s, 0, 0)),
        ],
        out_shape=[
            jax.ShapeDtypeStruct((nch, CHUNK, D_POOL), F32),
            jax.ShapeDtypeStruct((N_SSM_GROUPS, nch // C_TILE, CHUNK, SSM_GROUP, C_TILE), BF16),
        ],
        compiler_params=_params(("parallel", "parallel")),
        name="in_proj",
    )(x3, w_pool, w_ssm_t)


def _shift_rows(x, d, row, n):
    if d == 0:
        return x
    r = pltpu.roll(x, d % n, 0)
    if d > 0:
        return jnp.where(row >= d, r, 0.0)
    return jnp.where(row < n + d, r, 0.0)


def _pool_kernel(z_ref, w_ref, sc_ref, o_ref):
    n = z_ref.shape[0]
    gi = pl.program_id(1)
    row = lax.broadcasted_iota(jnp.int32, (n, 1), 0)

    for k, w in enumerate(POOL_WINDOWS):
        @pl.when(gi == k)
        def _(w=w):
            x = z_ref[...]
            half = w // 2
            pd, pu, span = x, x, 1
            while span < half:
                pd = pd + _shift_rows(pd, span, row, n)
                pu = pu + _shift_rows(pu, -span, row, n)
                span *= 2
            total = _shift_rows(pd, 1, row, n) + pu
            lo = jnp.maximum(row - half, 0)
            hi = jnp.minimum(row + half, n)
            cnt = (hi - lo).astype(F32)
            diff = (total / cnt - x).astype(BF16)
            o_ref[...] = _dot(diff, w_ref[...].astype(BF16)) * sc_ref[...]


def _pool(zp3, pool_w, pool_scale):
    b, s, _ = zp3.shape
    return pl.pallas_call(
        _pool_kernel,
        grid=(b, len(POOL_WINDOWS)),
        in_specs=[
            pl.BlockSpec((None, s, POOL_GROUP), lambda i, g: (i, 0, g)),
            pl.BlockSpec((None, POOL_GROUP, POOL_GROUP), lambda i, g: (g, 0, 0)),
            pl.BlockSpec((1, POOL_GROUP), lambda i, g: (0, g)),
        ],
        out_specs=pl.BlockSpec((None, s, POOL_GROUP), lambda i, g: (i, 0, g)),
        out_shape=jax.ShapeDtypeStruct((b, s, D_POOL), F32),
        compiler_params=_params(("parallel", "parallel")),
        name="pool",
    )(zp3, pool_w, pool_scale)


def _expand_consts():
    time = np.arange(CHUNK_W) // SSM_GROUP
    def onehot(e):
        m = np.zeros((CHUNK_W, LANES), np.float32)
        m[np.arange(CHUNK_W), e] = 1.0
        return m
    return np.stack([
        onehot(CHUNK - 1 - time),
        onehot(time),
        onehot(time + 1),
        onehot(CHUNK - time),
    ])


def _cmul_packed(x, p, q):
    return x * p + pltpu.roll(x, LANES // 2, 1) * q


def _ssm_kernel(u_ref, are_ref, aim_ref, ldt_ref, bt_re_ref, bt_im_ref, ccr_ref, ct_re_ref,
                ct_im_ref, d_ref, exp_ref, y_ref, *, seq_chunks):
    n_ct = u_ref.shape[0]
    nch = n_ct * C_TILE
    half = LANES // 2
    lane = lax.broadcasted_iota(jnp.int32, (1, LANES), 1)
    lo_half = lane < half

    def direction(di):
        a_re = are_ref[di]
        a_im = aim_ref[di]
        dt = jnp.exp(ldt_ref[di])
        mag = jnp.exp(a_re * dt)
        ang = a_im * dt
        lam = jnp.where(lo_half, mag * jnp.cos(ang), mag * jnp.sin(ang))
        lb_re = mag * jnp.cos(ang)
        lb_im = mag * jnp.sin(ang)
        den = a_re * a_re + a_im * a_im
        f_re = ((lb_re - 1.0) * a_re + lb_im * a_im) / den
        f_im = (lb_im * a_re - (lb_re - 1.0) * a_im) / den
        return lam, f_re, f_im

    def power_table(lam):
        e = lax.broadcasted_iota(jnp.int32, (LANES, 1), 0)
        tab = jnp.where(lo_half, 1.0, 0.0) * jnp.ones((LANES, 1), F32)
        sq = lam
        for k in range(7):
            p = jnp.where(lo_half, sq, pltpu.roll(sq, half, 1))
            q = jnp.where(lo_half, -pltpu.roll(sq, half, 1), sq)
            tab = jnp.where(((e >> k) & 1) == 1, _cmul_packed(tab, p, q), tab)
            sq = _cmul_packed(sq, p, q)
        return tab

    def tile_rows(x16):
        return jnp.broadcast_to(x16[None], (CHUNK, SSM_GROUP, LANES)).reshape(CHUNK_W, LANES)

    def expanded(tab, which, v_re, v_im, conj_sign):
        lexp = _dot(exp_ref[which], tab.astype(BF16))
        if conj_sign > 0:
            p = jnp.where(lo_half, v_re, v_re)
            q = jnp.where(lo_half, -v_im, v_im)
        else:
            p = jnp.where(lo_half, v_re, -v_re)
            q = jnp.where(lo_half, -v_im, -v_im)
        return lexp * tile_rows(p) + pltpu.roll(lexp, half, 1) * tile_rows(q)

    lam_f, ff_re, ff_im = direction(0)
    lam_b, fb_re, fb_im = direction(1)
    tab_f = power_table(lam_f)
    tab_b = power_table(lam_b)

    def bbar(bt_re, bt_im, f_re, f_im):
        return bt_re * f_re - bt_im * f_im, bt_re * f_im + bt_im * f_re

    bf_re, bf_im = bbar(bt_re_ref[0], bt_im_ref[0], ff_re, ff_im)
    bb_re, bb_im = bbar(bt_re_ref[1], bt_im_ref[1], fb_re, fb_im)

    pb1 = expanded(tab_f, 0, bf_re, bf_im, 1)
    pb2 = expanded(tab_b, 1, bb_re, bb_im, 1)
    pb3 = expanded(tab_b, 2, bb_re, bb_im, 1)
    ft_f = expanded(tab_f, 2, ct_re_ref[0], ct_im_ref[0], -1)
    ft_b = expanded(tab_b, 3, ct_re_ref[1], ct_im_ref[1], -1)

    row_w = lax.broadcasted_iota(jnp.int32, (CHUNK_W, 1), 0)
    last_blk = row_w >= CHUNK_W - SSM_GROUP
    pb2_lag0 = jnp.where(last_blk, pltpu.roll(pb2, CHUNK_W - SSM_GROUP, 0), 0.0)
    ccr_f = ccr_ref[0].astype(BF16)
    ccr_b = ccr_ref[1].astype(BF16)
    r_lo = _dot_nt(ccr_f, pb1.astype(BF16)) + _dot_nt(ccr_b, pb2_lag0.astype(BF16))
    co = lax.broadcasted_iota(jnp.int32, (SSM_GROUP, CHUNK_W), 0)
    col = lax.broadcasted_iota(jnp.int32, (SSM_GROUP, CHUNK_W), 1)
    r_lo = r_lo + jnp.where(col == CHUNK_W - SSM_GROUP + co, d_ref[...], 0.0)
    r_hi = _dot_nt(ccr_b, pb3.astype(BF16))
    r_t = jnp.concatenate([r_lo, r_hi], axis=1)
    g_t = jnp.concatenate(
        [pltpu.roll(r_t, SSM_GROUP * (tl + 1), 1) for tl in range(T_SUB)], axis=0
    ).astype(BF16)

    u = jnp.concatenate([u_ref[ct].reshape(CHUNK_W, C_TILE) for ct in range(n_ct)],
                        axis=1)
    e_mat = jnp.concatenate([pb1, pb2], axis=1).astype(BF16)
    xend = _dot_tn(e_mat, u)
    lanec = lax.broadcasted_iota(jnp.int32, (1, nch), 1) % seq_chunks
    ns = SSM_STATE

    def scan(re, im, tab, forward):
        lam_col = jnp.transpose(tab[CHUNK:CHUNK + 8, :])[:, 0:1]
        a, b = lam_col[:ns], lam_col[ns:]
        n_steps = int(math.log2(seq_chunks))

        def shifted(v, d):
            if forward:
                return jnp.where(lanec >= d, pltpu.roll(v, d, 1), 0.0)
            return jnp.where(lanec < seq_chunks - d, pltpu.roll(v, nch - d, 1), 0.0)

        for k in range(n_steps):
            sr, si = shifted(re, 1 << k), shifted(im, 1 << k)
            re, im = re + (sr * a - si * b), im + (sr * b + si * a)
            a, b = a * a - b * b, 2.0 * a * b
        return shifted(re, 1), shifted(im, 1)

    f_re, f_im = scan(xend[:ns], xend[ns:2 * ns], tab_f, True)
    b_re, b_im = scan(xend[2 * ns:3 * ns], xend[3 * ns:], tab_b, False)
    xin = jnp.concatenate([f_re, f_im, b_re, b_im], axis=0).astype(BF16)
    f_t = jnp.concatenate([ft_f, ft_b], axis=1).astype(BF16)

    toeplitz = jnp.concatenate(
        [g_t[:, CHUNK_W - LANES * th:2 * CHUNK_W - LANES * th] for th in range(CHUNK // T_SUB)],
        axis=0)
    y_t = _dot(toeplitz, u) + _dot(f_t, xin)
    for ct in range(n_ct):
        y_ref[ct] = y_t[:, ct * C_TILE:(ct + 1) * C_TILE].reshape(
            CHUNK, SSM_GROUP, C_TILE).astype(y_ref.dtype)


def _ssm(ut, a_re, a_im, log_dt, b_re, b_im, c_re, c_im, d, seq_chunks):
    g, n_ct = ut.shape[:2]
    n = SSM_STATE

    def lane_vec(a):
        a = jnp.transpose(a, (1, 0, 2))
        return jnp.concatenate([a, a], axis=-1)[:, :, None, :]

    are = lane_vec(a_re)
    aim = lane_vec(a_im)
    ldt = lane_vec(jnp.broadcast_to(log_dt[..., None], (2, g, n)))

    def bt(b):
        b = jnp.transpose(b, (1, 0, 3, 2))
        return jnp.concatenate([b, b], axis=-1)

    def ct(c):
        c = jnp.transpose(c, (1, 0, 2, 3))
        return jnp.concatenate([c, c], axis=-1)

    ccr = jnp.concatenate([jnp.transpose(c_re, (1, 0, 2, 3)),
                           -jnp.transpose(c_im, (1, 0, 2, 3))], axis=-1)
    d_col = d.reshape(g, SSM_GROUP, 1)
    exp_c = jnp.asarray(_expand_consts(), BF16)

    vec_spec = pl.BlockSpec((None, 2, 1, LANES), lambda i: (i, 0, 0, 0))
    mat_spec = pl.BlockSpec((None, 2, SSM_GROUP, LANES), lambda i: (i, 0, 0, 0))
    return pl.pallas_call(
        functools.partial(_ssm_kernel, seq_chunks=seq_chunks),
        grid=(g,),
        in_specs=[
            pl.BlockSpec((None, n_ct, CHUNK, SSM_GROUP, C_TILE), lambda i: (i, 0, 0, 0, 0)),
            vec_spec, vec_spec, vec_spec,
            mat_spec, mat_spec, mat_spec, mat_spec, mat_spec,
            pl.BlockSpec((None, SSM_GROUP, 1), lambda i: (i, 0, 0)),
            pl.BlockSpec((4, CHUNK_W, LANES), lambda i: (0, 0, 0)),
        ],
        out_specs=pl.BlockSpec((None, n_ct, CHUNK, SSM_GROUP, C_TILE), lambda i: (i, 0, 0, 0, 0)),
        out_shape=jax.ShapeDtypeStruct(ut.shape, BF16),
        compiler_params=_params(("parallel",)),
        name="ssm",
    )(ut, are, aim, ldt, bt(b_re), bt(b_im), ccr, ct(c_re), ct(c_im), d_col, exp_c)


def _mix_out_kernel(x_ref, a_ref, yt_ref, gwt_ref, gb_ref, wo_ref, h_ref):
    nc = x_ref.shape[0]
    y = jnp.concatenate([yt_ref[:, j, :, :].reshape(D_SSM, nc) for j in range(S_TILE)],
                        axis=1).astype(F32)
    z = 0.5 * y * (1.0 + jnp.tanh(math.sqrt(2.0 / math.pi) * (y + 0.044715 * (y * y * y))))
    gate = _sigmoid(_dot(gwt_ref[...], z.astype(BF16)) + gb_ref[...])
    s = (z * gate).astype(BF16)
    h = (_rows_by_time(x_ref) + _dot(_rows_by_time(a_ref).astype(BF16), wo_ref[:D_POOL, :])
         + _dot_tn(s, wo_ref[D_POOL:, :]))
    for j in range(S_TILE):
        h_ref[:, j, :] = h[j * nc:(j + 1) * nc, :]


def _mix_out(x3, a3, yt, glu_w_t, glu_b_col, w_out):
    nch = x3.shape[0]
    return pl.pallas_call(
        _mix_out_kernel,
        grid=(nch // C_TILE, CHUNK // S_TILE),
        in_specs=[
            pl.BlockSpec((C_TILE, S_TILE, D_MODEL), lambda c, t: (c, t, 0)),
            pl.BlockSpec((C_TILE, S_TILE, D_POOL), lambda c, t: (c, t, 0)),
            pl.BlockSpec((N_SSM_GROUPS, None, S_TILE, SSM_GROUP, C_TILE), lambda c, t: (0, c, t, 0, 0)),
            pl.BlockSpec((D_SSM, D_SSM), lambda c, t: (0, 0)),
            pl.BlockSpec((D_SSM, 1), lambda c, t: (0, 0)),
            pl.BlockSpec((D_MODEL, D_MODEL), lambda c, t: (0, 0)),
        ],
        out_specs=pl.BlockSpec((C_TILE, S_TILE, D_MODEL), lambda c, t: (c, t, 0)),
        out_shape=jax.ShapeDtypeStruct((nch, CHUNK, D_MODEL), F32),
        compiler_params=_params(("parallel", "parallel")),
        name="mix_out",
    )(x3, a3, yt, glu_w_t, glu_b_col, w_out)


def _pack_rows(x):
    b = lax.bitcast_convert_type(x.astype(BF16).astype(F32), U32)
    return (b[:, :HALF] & jnp.uint32(0xFFFF0000)) | (b[:, HALF:] >> 16)


def _unpack_rows(w):
    lo = lax.bitcast_convert_type(w & jnp.uint32(0xFFFF0000), F32)
    hi = lax.bitcast_convert_type(w << 16, F32)
    return lo, hi


def _split_bf16(x):
    hi = x.astype(BF16)
    return hi, (x - hi.astype(F32)).astype(BF16)


def _route(v32, wr_ref, br_ref):
    v_hi, v_lo = _split_bf16(v32)
    w_hi, w_lo = _split_bf16(wr_ref[...])
    logits = _dot(v_hi, w_hi) + (_dot(v_lo, w_hi) + _dot(v_hi, w_lo)) + br_ref[...]
    lane = lax.broadcasted_iota(jnp.int32, logits.shape, 1).astype(F32)
    neg = -jnp.inf
    none = float(ROUTER_W)
    lg = jnp.where(lane < N_EXPERT_GROUPS, logits, neg)
    mg = jnp.max(lg, axis=1, keepdims=True)
    grp_p = 1.0 / jnp.sum(jnp.exp(lg - mg), axis=1, keepdims=True)
    grp_idx = jnp.min(jnp.where(lg == mg, lane, none), axis=1, keepdims=True)
    first = N_EXPERT_GROUPS + grp_idx * EXPERTS_PER_GROUP
    sel = (lane >= first) & (lane < first + EXPERTS_PER_GROUP)
    le = jnp.where(sel, logits, neg)
    m1 = jnp.max(le, axis=1, keepdims=True)
    i1 = jnp.min(jnp.where(le == m1, lane, none), axis=1, keepdims=True)
    z = jnp.sum(jnp.exp(le - m1), axis=1, keepdims=True)
    le2 = jnp.where(lane == i1, neg, le)
    m2 = jnp.max(le2, axis=1, keepdims=True)
    i2 = jnp.min(jnp.where(le2 == m2, lane, none), axis=1, keepdims=True)
    p1 = 1.0 / z
    p2 = jnp.exp(m2 - m1) / z
    tot = p1 + p2
    return i1, i2, grp_p * (p1 / tot), grp_p * (p2 / tot)


def _split_planes(packed, ref):
    ref[0] = packed[:, :SC_ROW]
    ref[1] = packed[:, SC_ROW:]


def _router_kernel(h_ref, g_ref, wr_ref, br_ref, vp_ref, meta_t_ref, cnt_ref, carry_ref):
    @pl.when(pl.program_id(0) == 0)
    def _():
        carry_ref[...] = jnp.zeros_like(carry_ref)

    v32 = _rms(h_ref[...], g_ref[...])
    _split_planes(_pack_rows(v32), vp_ref)
    i1, i2, w1, w2 = _route(v32, wr_ref, br_ref)
    tm = v32.shape[0]
    lane = lax.broadcasted_iota(jnp.int32, (tm, ROUTER_W), 1).astype(F32)
    onehot = jnp.where(lane == i1, 1.0, jnp.where(lane == i2, 1.0, 0.0))
    r = lax.broadcasted_iota(jnp.int32, (tm, tm), 0)
    c = lax.broadcasted_iota(jnp.int32, (tm, tm), 1)
    below = jnp.where(c < r, 1.0, 0.0).astype(BF16)
    before = _dot(below, onehot.astype(BF16)) + carry_ref[...]
    rank1 = jnp.sum(jnp.where(lane == i1, before, 0.0), axis=1, keepdims=True)
    rank2 = jnp.sum(jnp.where(lane == i2, before, 0.0), axis=1, keepdims=True)
    carry = carry_ref[...] + jnp.sum(onehot, axis=0, keepdims=True)
    carry_ref[...] = carry
    cnt_ref[...] = carry
    e1 = i1 - N_EXPERT_GROUPS
    e2 = i2 - N_EXPERT_GROUPS
    meta = jnp.where(lane == 0, e1, jnp.where(lane == 1, e2, jnp.where(
        lane == 2, rank1, jnp.where(lane == 3, rank2, jnp.where(
            lane == 4, w1, jnp.where(lane == 5, w2, 0.0))))))
    meta_t_ref[...] = jnp.transpose(meta)[:META_ROWS, :]


def _router(h1, g_ffn, w_router, b_router, tm):
    t = h1.shape[0]
    return pl.pallas_call(
        _router_kernel,
        grid=(t // tm,),
        in_specs=[
            pl.BlockSpec((tm, D_MODEL), lambda i: (i, 0)),
            pl.BlockSpec((1, D_MODEL), lambda i: (0, 0)),
            pl.BlockSpec((D_MODEL, ROUTER_W), lambda i: (0, 0)),
            pl.BlockSpec((1, ROUTER_W), lambda i: (0, 0)),
        ],
        out_specs=[
            pl.BlockSpec((2, tm, SC_ROW), lambda i: (0, i, 0)),
            pl.BlockSpec((META_ROWS, tm), lambda i: (0, i)),
            pl.BlockSpec((1, ROUTER_W), lambda i: (0, 0)),
        ],
        out_shape=[
            jax.ShapeDtypeStruct((2, t, SC_ROW), U32),
            jax.ShapeDtypeStruct((META_ROWS, t), F32),
            jax.ShapeDtypeStruct((1, ROUTER_W), F32),
        ],
        scratch_shapes=[pltpu.VMEM((1, ROUTER_W), F32)],
        compiler_params=_params(("arbitrary",)),
        name="router",
    )(h1, g_ffn, w_router, b_router)


def _plan(meta_t, counts, n_tiles):
    e1 = meta_t[0].astype(jnp.int32)
    e2 = meta_t[1].astype(jnp.int32)
    rank1 = meta_t[2].astype(jnp.int32)
    rank2 = meta_t[3].astype(jnp.int32)
    cnt = counts[0, N_EXPERT_GROUPS:N_EXPERT_GROUPS + N_EXPERTS].astype(jnp.int32)
    padded = ((cnt + ROW_TILE - 1) // ROW_TILE) * ROW_TILE
    ends = jnp.cumsum(padded)
    starts = ends - padded
    experts = jnp.arange(N_EXPERTS, dtype=jnp.int32)
    pos1 = rank1 + jnp.sum(jnp.where(e1[None, :] == experts[:, None], starts[:, None], 0), axis=0)
    pos2 = rank2 + jnp.sum(jnp.where(e2[None, :] == experts[:, None], starts[:, None], 0), axis=0)
    tile_start = jnp.arange(n_tiles, dtype=jnp.int32) * ROW_TILE
    tile_expert = jnp.sum((tile_start[:, None] >= ends[None, :]).astype(jnp.int32), axis=1)
    tile_expert = jnp.minimum(tile_expert, N_EXPERTS - 1)
    rows_left = jnp.sum(jnp.where(tile_expert[:, None] == experts, cnt + starts, 0), axis=1) - tile_start
    n_valid = jnp.clip(rows_left, 0, ROW_TILE).astype(jnp.int32)
    changed = jnp.concatenate([jnp.ones((1,), jnp.int32),
                               (tile_expert[1:] != tile_expert[:-1]).astype(jnp.int32)])
    last_used = jnp.maximum(ends[-1] // ROW_TILE - 1, 0)
    block = jnp.minimum(jnp.arange(n_tiles, dtype=jnp.int32), last_used)
    tile_expert = jnp.sum(jnp.where(block[:, None] == jnp.arange(n_tiles)[None, :],
                                    tile_expert[None, :], 0), axis=1)
    plane = n_tiles * ROW_TILE
    half_rows = jnp.concatenate([pos1, pos1 + plane, pos2, pos2 + plane])[None]
    return half_rows, tile_expert, n_valid, changed, block


def _sc_mesh():
    return plsc.VectorSubcoreMesh(core_axis_name="c", subcore_axis_name="s")


def _sc_scatter_rows(rows, idx, n_out):
    t, width = rows.shape
    steps = t // SC_WINDOW

    @pl.kernel(out_type=jax.ShapeDtypeStruct((n_out, width), rows.dtype), mesh=_sc_mesh(),
               scratch_types=[], name="moe_scatter")
    def scatter(rows_hbm, idx_hbm, out_hbm):
        def body(rows_vmem, idx_vmem):
            pltpu.sync_copy(rows_vmem, out_hbm.at[idx_vmem.at[0]])

        pltpu.emit_pipeline(
            body,
            grid=(2, steps),
            in_specs=[pl.BlockSpec((SC_WINDOW, width), lambda k, j: (j, 0)),
                      pl.BlockSpec((1, SC_WINDOW), lambda k, j: (0, k * steps + j))],
            out_specs=[],
            core_axis_name=("c", "s"),
            dimension_semantics=(pltpu.PARALLEL, pltpu.PARALLEL),
        )(rows_hbm, idx_hbm)

    return scatter(rows, idx)


def _sc_gather_rows(table, idx):
    m = idx.shape[1]
    width = table.shape[1]
    steps = m // (2 * SC_WINDOW)

    @pl.kernel(out_type=jax.ShapeDtypeStruct((m, width), table.dtype), mesh=_sc_mesh(),
               scratch_types=[], name="moe_gather")
    def gather(table_hbm, idx_hbm, out_hbm):
        def body(idx_vmem, out_vmem):
            pltpu.sync_copy(table_hbm.at[idx_vmem.at[0]], out_vmem)

        pltpu.emit_pipeline(
            body,
            grid=(2, steps),
            in_specs=[pl.BlockSpec((1, SC_WINDOW), lambda k, j: (0, k * steps + j))],
            out_specs=[pl.BlockSpec((SC_WINDOW, width), lambda k, j: (k * steps + j, 0))],
            core_axis_name=("c", "s"),
            dimension_semantics=(pltpu.PARALLEL, pltpu.PARALLEL),
        )(idx_hbm, out_hbm)

    return gather(table, idx)


def _experts_kernel(te_ref, nv_ref, new_ref, blk_ref, xs_ref, wg_ref, wu_ref, wd_ref, ys_ref,
                    wg_s, wu_s, wd_s):
    r = pl.program_id(0)
    n_valid = nv_ref[r]

    @pl.when((new_ref[r] == 1) & (n_valid > 0))
    def _():
        wg_s[...] = wg_ref[...].astype(BF16)
        wu_s[...] = wu_ref[...].astype(BF16)
        wd_s[...] = wd_ref[...].astype(BF16)

    @pl.when(n_valid > 0)
    def _():
        parts = [p.astype(BF16) for p in _unpack_rows(xs_ref[0]) + _unpack_rows(xs_ref[1])]
        cols = (0, 2 * SC_ROW, SC_ROW, 3 * SC_ROW)
        hg = sum(_dot(p, wg_s[c:c + SC_ROW, :]) for p, c in zip(parts, cols))
        hu = sum(_dot(p, wu_s[c:c + SC_ROW, :]) for p, c in zip(parts, cols))
        row = lax.broadcasted_iota(jnp.int32, (ROW_TILE, 1), 0)
        hid = jnp.where(row < n_valid, hg * _sigmoid(hg) * hu, 0.0).astype(BF16)
        _split_planes(_pack_rows(_dot(hid, wd_s[...])), ys_ref)


def _experts(xs, tile_expert, n_valid, changed, block, w_gate, w_up, w_down, layer):
    n_tiles = xs.shape[1] // ROW_TILE
    base = layer * N_EXPERTS
    w_spec = pl.BlockSpec((None, D_MODEL, D_EXPERT), lambda r, te, nv, new, blk: (base + te[r], 0, 0))
    grid_spec = pltpu.PrefetchScalarGridSpec(
        num_scalar_prefetch=4,
        grid=(n_tiles,),
        in_specs=[
            pl.BlockSpec((2, ROW_TILE, SC_ROW), lambda r, te, nv, new, blk: (0, blk[r], 0)),
            w_spec, w_spec,
            pl.BlockSpec((None, D_EXPERT, D_MODEL), lambda r, te, nv, new, blk: (base + te[r], 0, 0)),
        ],
        out_specs=pl.BlockSpec((2, ROW_TILE, SC_ROW), lambda r, te, nv, new, blk: (0, blk[r], 0)),
        scratch_shapes=[
            pltpu.VMEM((D_MODEL, D_EXPERT), BF16),
            pltpu.VMEM((D_MODEL, D_EXPERT), BF16),
            pltpu.VMEM((D_EXPERT, D_MODEL), BF16),
        ],
    )
    return pl.pallas_call(
        _experts_kernel,
        grid_spec=grid_spec,
        out_shape=jax.ShapeDtypeStruct((2, n_tiles * ROW_TILE, SC_ROW), U32),
        compiler_params=_params(("arbitrary",)),
        name="experts",
    )(tile_expert, n_valid, changed, block, xs, w_gate, w_up, w_down)


def _ple_kernel(h_ref, yg_ref, meta_t_ref, p_ref, wg_ref, bg_ref, wp_ref, gf_ref, o_ref, *,
                final_norm):
    meta = jnp.transpose(meta_t_ref[...])
    w1 = meta[:, 4:5]
    w2 = meta[:, 5:6]
    q0, q2 = (w1 * u + w2 * v for u, v in zip(_unpack_rows(yg_ref[0]), _unpack_rows(yg_ref[2])))
    q1, q3 = (w1 * u + w2 * v for u, v in zip(_unpack_rows(yg_ref[1]), _unpack_rows(yg_ref[3])))
    moe = jnp.concatenate([q0, q1, q2, q3], axis=1)
    h = h_ref[...] + moe
    gate = _sigmoid(_dot(_rms(h).astype(BF16), wg_ref[...]) + bg_ref[...])
    h = h + gate * _dot(p_ref[...].astype(BF16), wp_ref[...])
    o_ref[...] = _rms(h, gf_ref[...]) if final_norm else h


def _ple(h1, yg, meta_t, p2, w_gate, b_gate, w_proj, g_final, final_norm, tm):
    t = h1.shape[0]
    return pl.pallas_call(
        functools.partial(_ple_kernel, final_norm=final_norm),
        grid=(t // tm,),
        in_specs=[
            pl.BlockSpec((tm, D_MODEL), lambda i: (i, 0)),
            pl.BlockSpec((4, tm, SC_ROW), lambda i: (0, i, 0)),
            pl.BlockSpec((META_ROWS, tm), lambda i: (0, i)),
            pl.BlockSpec((tm, D_PLE), lambda i: (i, 0)),
            pl.BlockSpec((D_MODEL, D_MODEL), lambda i: (0, 0)),
            pl.BlockSpec((1, D_MODEL), lambda i: (0, 0)),
            pl.BlockSpec((D_PLE, D_MODEL), lambda i: (0, 0)),
            pl.BlockSpec((1, D_MODEL), lambda i: (0, 0)),
        ],
        out_specs=pl.BlockSpec((tm, D_MODEL), lambda i: (i, 0)),
        out_shape=jax.ShapeDtypeStruct((t, D_MODEL), F32),
        compiler_params=_params(("parallel",)),
        name="ple",
    )(h1, yg, meta_t, p2, w_gate, b_gate, w_proj, g_final)


def kernel(x, p, g_mix, w_in, pool_w, pool_scale, ssm_a_re, ssm_a_im, ssm_log_dt, ssm_b_re,
           ssm_b_im, ssm_c_re, ssm_c_im, ssm_d, glu_w, glu_b, w_out, g_ffn, router_grp_w,
           router_grp_b, router_exp_w, router_exp_b, exp_w_gate, exp_w_up, exp_w_down, g_ple,
           ple_gate_w, ple_gate_b, ple_proj_w, g_final):
    bsz, seq, dm = x.shape
    depth = g_mix.shape[0]
    t = bsz * seq
    seq_chunks = seq // CHUNK
    nch = t // CHUNK
    tm = 512
    n_sorted = 2 * t + N_EXPERTS * ROW_TILE
    w_gate_all = exp_w_gate.reshape(depth * N_EXPERTS, dm, D_EXPERT)
    w_up_all = exp_w_up.reshape(depth * N_EXPERTS, dm, D_EXPERT)
    w_down_all = exp_w_down.reshape(depth * N_EXPERTS, D_EXPERT, dm)

    h = x.reshape(t, dm)
    for i in range(depth):
        w_in_b = (g_mix[i][:, None] * w_in[i]).astype(BF16)
        zp, ut = _in_proj(h.reshape(nch, CHUNK, dm), w_in_b[:, :D_POOL],
                          jnp.transpose(w_in_b[:, D_POOL:]))
        a = _pool(zp.reshape(bsz, seq, D_POOL), pool_w[i], pool_scale[i][None])
        yt = _ssm(ut, ssm_a_re[i], ssm_a_im[i], ssm_log_dt[i], ssm_b_re[i], ssm_b_im[i],
                  ssm_c_re[i], ssm_c_im[i], ssm_d[i], seq_chunks)
        h = _mix_out(h.reshape(nch, CHUNK, dm), a.reshape(nch, CHUNK, D_POOL), yt,
                     jnp.transpose(glu_w[i]).astype(BF16), glu_b[i][:, None],
                     w_out[i].astype(BF16)).reshape(t, dm)

        w_router = jnp.concatenate(
            [router_grp_w[i],
             jnp.transpose(router_exp_w[i], (1, 0, 2)).reshape(dm, N_EXPERTS),
             jnp.zeros((dm, ROUTER_W - N_EXPERT_GROUPS - N_EXPERTS), F32)], axis=1)
        b_router = jnp.concatenate(
            [router_grp_b[i], router_exp_b[i].reshape(N_EXPERTS),
             jnp.zeros((ROUTER_W - N_EXPERT_GROUPS - N_EXPERTS,), F32)])[None]
        vp, meta_t, counts = _router(h, g_ffn[i][None], w_router, b_router, tm)
        idx, tile_expert, n_valid, changed, block = _plan(meta_t, counts, n_sorted // ROW_TILE)
        xs = _sc_scatter_rows(vp.reshape(2 * t, SC_ROW), idx, 2 * n_sorted)
        ys = _experts(xs.reshape(2, n_sorted, SC_ROW), tile_expert, n_valid, changed, block,
                      w_gate_all, w_up_all, w_down_all, i)
        yg2 = _sc_gather_rows(ys.reshape(2 * n_sorted, SC_ROW), idx).reshape(4, t, SC_ROW)
        h = _ple(h, yg2, meta_t, p[i].reshape(t, D_PLE),
                 (g_ple[i][:, None] * ple_gate_w[i]).astype(BF16), ple_gate_b[i][None],
                 ple_proj_w[i].astype(BF16), g_final[None], i == depth - 1, tm)
    return h.reshape(bsz, seq, dm)
```

```python
import functools
import math

import numpy as np
import jax
import jax.numpy as jnp
from jax import lax
from jax.experimental import pallas as pl
from jax.experimental.pallas import tpu as pltpu
from jax.experimental.pallas import tpu_sc as plsc

F32 = jnp.float32
BF16 = jnp.bfloat16
U32 = jnp.uint32

D_MODEL = 1024
D_POOL = 512
D_SSM = 512
POOL_WINDOWS = (2, 4, 8, 16)
POOL_GROUP = 128
SSM_GROUP = 16
N_SSM_GROUPS = 32
SSM_STATE = 64
N_EXPERT_GROUPS = 4
EXPERTS_PER_GROUP = 8
N_EXPERTS = N_EXPERT_GROUPS * EXPERTS_PER_GROUP
D_EXPERT = 256
D_PLE = 256
RMS_EPS = 1e-6

LANES = 128
CHUNK = 32
CHUNK_W = CHUNK * SSM_GROUP
T_SUB = 8
S_TILE = 8
C_TILE = 128
ROUTER_W = LANES
HALF = D_MODEL // 2
ROW_TILE = 512
SC_WINDOW = 128
SC_ROW = HALF // 2
META_ROWS = 8
VMEM_LIMIT = 56 * 1024 * 1024


def _dot(a, b):
    return jnp.dot(a, b, preferred_element_type=F32)


def _dot_nt(a, b):
    return lax.dot_general(a, b, (((1,), (1,)), ((), ())), preferred_element_type=F32)


def _dot_tn(a, b):
    return lax.dot_general(a, b, (((0,), (0,)), ((), ())), preferred_element_type=F32)


def _rms(x, g=None):
    y = x * lax.rsqrt(jnp.mean(x * x, axis=-1, keepdims=True) + RMS_EPS)
    return y if g is None else y * g


def _sigmoid(x):
    return 1.0 / (1.0 + jnp.exp(-x))


def _params(sem):
    return pltpu.CompilerParams(dimension_semantics=sem, vmem_limit_bytes=VMEM_LIMIT)


def _regroup_matrix():
    n = C_TILE * S_TILE
    src = np.arange(n)
    dst = (src % S_TILE) * C_TILE + src // S_TILE
    p = np.zeros((n, n), np.float32)
    p[src, dst] = 1.0
    return p


def _in_proj_kernel(x_ref, wp_ref, wst_ref, perm_ref, zp_ref, ut_ref):
    nc = x_ref.shape[0]
    u = _rms(x_ref[...].reshape(nc * S_TILE, D_MODEL)).astype(BF16)
    zp_ref[...] = _dot(u, wp_ref[...]).reshape(zp_ref.shape)
    zt = _dot_nt(wst_ref[...], u).astype(BF16)
    zt = _dot(zt, perm_ref[...]).astype(BF16)
    for j in range(S_TILE):
        ut_ref[:, j, :, :] = zt[:, j * nc:(j + 1) * nc].reshape(N_SSM_GROUPS, SSM_GROUP, nc)


def _in_proj(x3, w_pool, w_ssm_t, perm):
    nch = x3.shape[0]
    return pl.pallas_call(
        _in_proj_kernel,
        grid=(nch // C_TILE, CHUNK // S_TILE),
        in_specs=[
            pl.BlockSpec((C_TILE, S_TILE, D_MODEL), lambda c, s: (c, s, 0)),
            pl.BlockSpec((D_MODEL, D_POOL), lambda c, s: (0, 0)),
            pl.BlockSpec((D_SSM, D_MODEL), lambda c, s: (0, 0)),
            pl.BlockSpec((C_TILE * S_TILE, C_TILE * S_TILE), lambda c, s: (0, 0)),
        ],
        out_specs=[
            pl.BlockSpec((C_TILE, S_TILE, D_POOL), lambda c, s: (c, s, 0)),
            pl.BlockSpec((N_SSM_GROUPS, None, S_TILE, SSM_GROUP, C_TILE), lambda c, s: (0, c, s, 0, 0)),
        ],
        out_shape=[
            jax.ShapeDtypeStruct((nch, CHUNK, D_POOL), F32),
            jax.ShapeDtypeStruct((N_SSM_GROUPS, nch // C_TILE, CHUNK, SSM_GROUP, C_TILE), BF16),
        ],
        compiler_params=_params(("parallel", "parallel")),
        name="in_proj",
    )(x3, w_pool, w_ssm_t, perm)


def _shift_rows(x, d, row, n):
    if d == 0:
        return x
    r = pltpu.roll(x, d % n, 0)
    if d > 0:
        return jnp.where(row >= d, r, 0.0)
    return jnp.where(row < n + d, r, 0.0)


def _pool_kernel(z_ref, w_ref, sc_ref, o_ref):
    n = z_ref.shape[0]
    gi = pl.program_id(1)
    row = lax.broadcasted_iota(jnp.int32, (n, 1), 0)

    for k, w in enumerate(POOL_WINDOWS):
        @pl.when(gi == k)
        def _(w=w):
            x = z_ref[...]
            half = w // 2
            pd, pu, span = x, x, 1
            while span < half:
                pd = pd + _shift_rows(pd, span, row, n)
                pu = pu + _shift_rows(pu, -span, row, n)
                span *= 2
            total = _shift_rows(pd, 1, row, n) + pu
            lo = jnp.maximum(row - half, 0)
            hi = jnp.minimum(row + half, n)
            cnt = (hi - lo).astype(F32)
            diff = (total / cnt - x).astype(BF16)
            o_ref[...] = _dot(diff, w_ref[...].astype(BF16)) * sc_ref[...]


def _pool(zp3, pool_w, pool_scale):
    b, s, _ = zp3.shape
    return pl.pallas_call(
        _pool_kernel,
        grid=(b, len(POOL_WINDOWS)),
        in_specs=[
            pl.BlockSpec((None, s, POOL_GROUP), lambda i, g: (i, 0, g)),
            pl.BlockSpec((None, POOL_GROUP, POOL_GROUP), lambda i, g: (g, 0, 0)),
            pl.BlockSpec((1, POOL_GROUP), lambda i, g: (0, g)),
        ],
        out_specs=pl.BlockSpec((None, s, POOL_GROUP), lambda i, g: (i, 0, g)),
        out_shape=jax.ShapeDtypeStruct((b, s, D_POOL), F32),
        compiler_params=_params(("parallel", "parallel")),
        name="pool",
    )(zp3, pool_w, pool_scale)


def _expand_consts():
    time = np.arange(CHUNK_W) // SSM_GROUP
    def onehot(e):
        m = np.zeros((CHUNK_W, LANES), np.float32)
        m[np.arange(CHUNK_W), e] = 1.0
        return m
    return np.stack([
        onehot(CHUNK - 1 - time),
        onehot(time),
        onehot(time + 1),
        onehot(CHUNK - time),
    ])


def _cmul_packed(x, p, q):
    return x * p + pltpu.roll(x, LANES // 2, 1) * q


def _ssm_kernel(u_ref, are_ref, aim_ref, ldt_ref, bt_re_ref, bt_im_ref, ccr_ref, ct_re_ref,
                ct_im_ref, d_ref, exp_ref, y_ref, *, seq_chunks):
    n_ct = u_ref.shape[0]
    nch = n_ct * C_TILE
    half = LANES // 2
    lane = lax.broadcasted_iota(jnp.int32, (1, LANES), 1)
    lo_half = lane < half

    def direction(di):
        a_re = are_ref[di]
        a_im = aim_ref[di]
        dt = jnp.exp(ldt_ref[di])
        mag = jnp.exp(a_re * dt)
        ang = a_im * dt
        lam = jnp.where(lo_half, mag * jnp.cos(ang), mag * jnp.sin(ang))
        lb_re = mag * jnp.cos(ang)
        lb_im = mag * jnp.sin(ang)
        den = a_re * a_re + a_im * a_im
        f_re = ((lb_re - 1.0) * a_re + lb_im * a_im) / den
        f_im = (lb_im * a_re - (lb_re - 1.0) * a_im) / den
        return lam, f_re, f_im

    def power_table(lam):
        e = lax.broadcasted_iota(jnp.int32, (LANES, 1), 0)
        tab = jnp.where(lo_half, 1.0, 0.0) * jnp.ones((LANES, 1), F32)
        sq = lam
        for k in range(7):
            p = jnp.where(lo_half, sq, pltpu.roll(sq, half, 1))
            q = jnp.where(lo_half, -pltpu.roll(sq, half, 1), sq)
            tab = jnp.where(((e >> k) & 1) == 1, _cmul_packed(tab, p, q), tab)
            sq = _cmul_packed(sq, p, q)
        return tab

    def tile_rows(x16):
        return jnp.broadcast_to(x16[None], (CHUNK, SSM_GROUP, LANES)).reshape(CHUNK_W, LANES)

    def expanded(tab, which, v_re, v_im, conj_sign):
        lexp = _dot(exp_ref[which], tab.astype(BF16))
        if conj_sign > 0:
            p = jnp.where(lo_half, v_re, v_re)
            q = jnp.where(lo_half, -v_im, v_im)
        else:
            p = jnp.where(lo_half, v_re, -v_re)
            q = jnp.where(lo_half, -v_im, -v_im)
        return lexp * tile_rows(p) + pltpu.roll(lexp, half, 1) * tile_rows(q)

    lam_f, ff_re, ff_im = direction(0)
    lam_b, fb_re, fb_im = direction(1)
    tab_f = power_table(lam_f)
    tab_b = power_table(lam_b)

    def bbar(bt_re, bt_im, f_re, f_im):
        return bt_re * f_re - bt_im * f_im, bt_re * f_im + bt_im * f_re

    bf_re, bf_im = bbar(bt_re_ref[0], bt_im_ref[0], ff_re, ff_im)
    bb_re, bb_im = bbar(bt_re_ref[1], bt_im_ref[1], fb_re, fb_im)

    pb1 = expanded(tab_f, 0, bf_re, bf_im, 1)
    pb2 = expanded(tab_b, 1, bb_re, bb_im, 1)
    pb3 = expanded(tab_b, 2, bb_re, bb_im, 1)
    ft_f = expanded(tab_f, 2, ct_re_ref[0], ct_im_ref[0], -1)
    ft_b = expanded(tab_b, 3, ct_re_ref[1], ct_im_ref[1], -1)

    row_w = lax.broadcasted_iota(jnp.int32, (CHUNK_W, 1), 0)
    last_blk = row_w >= CHUNK_W - SSM_GROUP
    pb2_lag0 = jnp.where(last_blk, pltpu.roll(pb2, CHUNK_W - SSM_GROUP, 0), 0.0)
    ccr_f = ccr_ref[0].astype(BF16)
    ccr_b = ccr_ref[1].astype(BF16)
    r_lo = _dot_nt(ccr_f, pb1.astype(BF16)) + _dot_nt(ccr_b, pb2_lag0.astype(BF16))
    co = lax.broadcasted_iota(jnp.int32, (SSM_GROUP, CHUNK_W), 0)
    col = lax.broadcasted_iota(jnp.int32, (SSM_GROUP, CHUNK_W), 1)
    r_lo = r_lo + jnp.where(col == CHUNK_W - SSM_GROUP + co, d_ref[...], 0.0)
    r_hi = _dot_nt(ccr_b, pb3.astype(BF16))
    r_t = jnp.concatenate([r_lo, r_hi], axis=1)
    g_t = jnp.concatenate(
        [pltpu.roll(r_t, SSM_GROUP * (tl + 1), 1) for tl in range(T_SUB)], axis=0
    ).astype(BF16)

    u = jnp.concatenate([u_ref[ct].reshape(CHUNK_W, C_TILE) for ct in range(n_ct)],
                        axis=1)
    e_mat = jnp.concatenate([pb1, pb2], axis=1).astype(BF16)
    xend = _dot_tn(e_mat, u)
    lanec = lax.broadcasted_iota(jnp.int32, (1, nch), 1) % seq_chunks
    ns = SSM_STATE

    def scan(re, im, tab, forward):
        lam_col = jnp.transpose(tab[CHUNK:CHUNK + 8, :])[:, 0:1]
        a, b = lam_col[:ns], lam_col[ns:]
        n_steps = int(math.log2(seq_chunks))

        def shifted(v, d):
            if forward:
                return jnp.where(lanec >= d, pltpu.roll(v, d, 1), 0.0)
            return jnp.where(lanec < seq_chunks - d, pltpu.roll(v, nch - d, 1), 0.0)

        for k in range(n_steps):
            sr, si = shifted(re, 1 << k), shifted(im, 1 << k)
            re, im = re + (sr * a - si * b), im + (sr * b + si * a)
            a, b = a * a - b * b, 2.0 * a * b
        return shifted(re, 1), shifted(im, 1)

    f_re, f_im = scan(xend[:ns], xend[ns:2 * ns], tab_f, True)
    b_re, b_im = scan(xend[2 * ns:3 * ns], xend[3 * ns:], tab_b, False)
    xin = jnp.concatenate([f_re, f_im, b_re, b_im], axis=0).astype(BF16)
    f_t = jnp.concatenate([ft_f, ft_b], axis=1).astype(BF16)

    toeplitz = jnp.concatenate(
        [g_t[:, CHUNK_W - LANES * th:2 * CHUNK_W - LANES * th] for th in range(CHUNK // T_SUB)],
        axis=0)
    y_t = _dot(toeplitz, u) + _dot(f_t, xin)
    for ct in range(n_ct):
        y_ref[ct] = y_t[:, ct * C_TILE:(ct + 1) * C_TILE].reshape(
            CHUNK, SSM_GROUP, C_TILE).astype(y_ref.dtype)


def _ssm(ut, a_re, a_im, log_dt, b_re, b_im, c_re, c_im, d, seq_chunks):
    g, n_ct = ut.shape[:2]
    n = SSM_STATE

    def lane_vec(a):
        a = jnp.transpose(a, (1, 0, 2))
        return jnp.concatenate([a, a], axis=-1)[:, :, None, :]

    are = lane_vec(a_re)
    aim = lane_vec(a_im)
    ldt = lane_vec(jnp.broadcast_to(log_dt[..., None], (2, g, n)))

    def bt(b):
        b = jnp.transpose(b, (1, 0, 3, 2))
        return jnp.concatenate([b, b], axis=-1)

    def ct(c):
        c = jnp.transpose(c, (1, 0, 2, 3))
        return jnp.concatenate([c, c], axis=-1)

    ccr = jnp.concatenate([jnp.transpose(c_re, (1, 0, 2, 3)),
                           -jnp.transpose(c_im, (1, 0, 2, 3))], axis=-1)
    d_col = d.reshape(g, SSM_GROUP, 1)
    exp_c = jnp.asarray(_expand_consts(), BF16)

    vec_spec = pl.BlockSpec((None, 2, 1, LANES), lambda i: (i, 0, 0, 0))
    mat_spec = pl.BlockSpec((None, 2, SSM_GROUP, LANES), lambda i: (i, 0, 0, 0))
    return pl.pallas_call(
        functools.partial(_ssm_kernel, seq_chunks=seq_chunks),
        grid=(g,),
        in_specs=[
            pl.BlockSpec((None, n_ct, CHUNK, SSM_GROUP, C_TILE), lambda i: (i, 0, 0, 0, 0)),
            vec_spec, vec_spec, vec_spec,
            mat_spec, mat_spec, mat_spec, mat_spec, mat_spec,
            pl.BlockSpec((None, SSM_GROUP, 1), lambda i: (i, 0, 0)),
            pl.BlockSpec((4, CHUNK_W, LANES), lambda i: (0, 0, 0)),
        ],
        out_specs=pl.BlockSpec((None, n_ct, CHUNK, SSM_GROUP, C_TILE), lambda i: (i, 0, 0, 0, 0)),
        out_shape=jax.ShapeDtypeStruct(ut.shape, BF16),
        compiler_params=_params(("parallel",)),
        name="ssm",
    )(ut, are, aim, ldt, bt(b_re), bt(b_im), ccr, ct(c_re), ct(c_im), d_col, exp_c)


def _mix_out_kernel(x_ref, a_ref, yt_ref, gwt_ref, gb_ref, wo_ref, perm_ref, h_ref):
    nc = x_ref.shape[0]
    rows = nc * S_TILE
    y = jnp.concatenate([yt_ref[:, j, :, :].reshape(D_SSM, nc) for j in range(S_TILE)],
                        axis=1).astype(F32)
    z = 0.5 * y * (1.0 + jnp.tanh(math.sqrt(2.0 / math.pi) * (y + 0.044715 * (y * y * y))))
    gate = _sigmoid(_dot(gwt_ref[...], z.astype(BF16)) + gb_ref[...])
    s = (z * gate).astype(BF16)
    s = _dot_nt(s, perm_ref[...]).astype(BF16)
    h = (x_ref[...].reshape(rows, D_MODEL)
         + _dot(a_ref[...].reshape(rows, D_POOL).astype(BF16), wo_ref[:D_POOL, :])
         + _dot_tn(s, wo_ref[D_POOL:, :]))
    h_ref[...] = h.reshape(h_ref.shape)


def _mix_out(x3, a3, yt, glu_w_t, glu_b_col, w_out, perm):
    nch = x3.shape[0]
    return pl.pallas_call(
        _mix_out_kernel,
        grid=(nch // C_TILE, CHUNK // S_TILE),
        in_specs=[
            pl.BlockSpec((C_TILE, S_TILE, D_MODEL), lambda c, t: (c, t, 0)),
            pl.BlockSpec((C_TILE, S_TILE, D_POOL), lambda c, t: (c, t, 0)),
            pl.BlockSpec((N_SSM_GROUPS, None, S_TILE, SSM_GROUP, C_TILE), lambda c, t: (0, c, t, 0, 0)),
            pl.BlockSpec((D_SSM, D_SSM), lambda c, t: (0, 0)),
            pl.BlockSpec((D_SSM, 1), lambda c, t: (0, 0)),
            pl.BlockSpec((D_MODEL, D_MODEL), lambda c, t: (0, 0)),
            pl.BlockSpec((C_TILE * S_TILE, C_TILE * S_TILE), lambda c, t: (0, 0)),
        ],
        out_specs=pl.BlockSpec((C_TILE, S_TILE, D_MODEL), lambda c, t: (c, t, 0)),
        out_shape=jax.ShapeDtypeStruct((nch, CHUNK, D_MODEL), F32),
        compiler_params=_params(("parallel", "parallel")),
        name="mix_out",
    )(x3, a3, yt, glu_w_t, glu_b_col, w_out, perm)


def _pack_rows(x):
    b = lax.bitcast_convert_type(x.astype(BF16).astype(F32), U32)
    return (b[:, :HALF] & jnp.uint32(0xFFFF0000)) | (b[:, HALF:] >> 16)


def _unpack_rows(w):
    lo = lax.bitcast_convert_type(w & jnp.uint32(0xFFFF0000), F32)
    hi = lax.bitcast_convert_type(w << 16, F32)
    return lo, hi


def _split_bf16(x):
    hi = x.astype(BF16)
    return hi, (x - hi.astype(F32)).astype(BF16)


def _route(v32, wr_ref, br_ref):
    v_hi, v_lo = _split_bf16(v32)
    w_hi, w_lo = _split_bf16(wr_ref[...])
    logits = _dot(v_hi, w_hi) + (_dot(v_lo, w_hi) + _dot(v_hi, w_lo)) + br_ref[...]
    lane = lax.broadcasted_iota(jnp.int32, logits.shape, 1).astype(F32)
    neg = -jnp.inf
    none = float(ROUTER_W)
    lg = jnp.where(lane < N_EXPERT_GROUPS, logits, neg)
    mg = jnp.max(lg, axis=1, keepdims=True)
    grp_p = 1.0 / jnp.sum(jnp.exp(lg - mg), axis=1, keepdims=True)
    grp_idx = jnp.min(jnp.where(lg == mg, lane, none), axis=1, keepdims=True)
    first = N_EXPERT_GROUPS + grp_idx * EXPERTS_PER_GROUP
    sel = (lane >= first) & (lane < first + EXPERTS_PER_GROUP)
    le = jnp.where(sel, logits, neg)
    m1 = jnp.max(le, axis=1, keepdims=True)
    i1 = jnp.min(jnp.where(le == m1, lane, none), axis=1, keepdims=True)
    z = jnp.sum(jnp.exp(le - m1), axis=1, keepdims=True)
    le2 = jnp.where(lane == i1, neg, le)
    m2 = jnp.max(le2, axis=1, keepdims=True)
    i2 = jnp.min(jnp.where(le2 == m2, lane, none), axis=1, keepdims=True)
    p1 = 1.0 / z
    p2 = jnp.exp(m2 - m1) / z
    tot = p1 + p2
    return i1, i2, grp_p * (p1 / tot), grp_p * (p2 / tot)


def _split_planes(packed, ref):
    ref[0] = packed[:, :SC_ROW]
    ref[1] = packed[:, SC_ROW:]


def _router_kernel(h_ref, g_ref, wr_ref, br_ref, vp_ref, meta_t_ref, cnt_ref, carry_ref):
    @pl.when(pl.program_id(0) == 0)
    def _():
        carry_ref[...] = jnp.zeros_like(carry_ref)

    v32 = _rms(h_ref[...], g_ref[...])
    _split_planes(_pack_rows(v32), vp_ref)
    i1, i2, w1, w2 = _route(v32, wr_ref, br_ref)
    tm = v32.shape[0]
    lane = lax.broadcasted_iota(jnp.int32, (tm, ROUTER_W), 1).astype(F32)
    onehot = jnp.where(lane == i1, 1.0, jnp.where(lane == i2, 1.0, 0.0))
    r = lax.broadcasted_iota(jnp.int32, (tm, tm), 0)
    c = lax.broadcasted_iota(jnp.int32, (tm, tm), 1)
    below = jnp.where(c < r, 1.0, 0.0).astype(BF16)
    before = _dot(below, onehot.astype(BF16)) + carry_ref[...]
    rank1 = jnp.sum(jnp.where(lane == i1, before, 0.0), axis=1, keepdims=True)
    rank2 = jnp.sum(jnp.where(lane == i2, before, 0.0), axis=1, keepdims=True)
    carry = carry_ref[...] + jnp.sum(onehot, axis=0, keepdims=True)
    carry_ref[...] = carry
    cnt_ref[...] = carry
    e1 = i1 - N_EXPERT_GROUPS
    e2 = i2 - N_EXPERT_GROUPS
    meta = jnp.where(lane == 0, e1, jnp.where(lane == 1, e2, jnp.where(
        lane == 2, rank1, jnp.where(lane == 3, rank2, jnp.where(
            lane == 4, w1, jnp.where(lane == 5, w2, 0.0))))))
    meta_t_ref[...] = jnp.transpose(meta)[:META_ROWS, :]


def _router(h1, g_ffn, w_router, b_router, tm):
    t = h1.shape[0]
    return pl.pallas_call(
        _router_kernel,
        grid=(t // tm,),
        in_specs=[
            pl.BlockSpec((tm, D_MODEL), lambda i: (i, 0)),
            pl.BlockSpec((1, D_MODEL), lambda i: (0, 0)),
            pl.BlockSpec((D_MODEL, ROUTER_W), lambda i: (0, 0)),
            pl.BlockSpec((1, ROUTER_W), lambda i: (0, 0)),
        ],
        out_specs=[
            pl.BlockSpec((2, tm, SC_ROW), lambda i: (0, i, 0)),
            pl.BlockSpec((META_ROWS, tm), lambda i: (0, i)),
            pl.BlockSpec((1, ROUTER_W), lambda i: (0, 0)),
        ],
        out_shape=[
            jax.ShapeDtypeStruct((2, t, SC_ROW), U32),
            jax.ShapeDtypeStruct((META_ROWS, t), F32),
            jax.ShapeDtypeStruct((1, ROUTER_W), F32),
        ],
        scratch_shapes=[pltpu.VMEM((1, ROUTER_W), F32)],
        compiler_params=_params(("arbitrary",)),
        name="router",
    )(h1, g_ffn, w_router, b_router)


def _plan(meta_t, counts, n_tiles):
    e1 = meta_t[0].astype(jnp.int32)
    e2 = meta_t[1].astype(jnp.int32)
    rank1 = meta_t[2].astype(jnp.int32)
    rank2 = meta_t[3].astype(jnp.int32)
    cnt = counts[0, N_EXPERT_GROUPS:N_EXPERT_GROUPS + N_EXPERTS].astype(jnp.int32)
    padded = ((cnt + ROW_TILE - 1) // ROW_TILE) * ROW_TILE
    ends = jnp.cumsum(padded)
    starts = ends - padded
    experts = jnp.arange(N_EXPERTS, dtype=jnp.int32)
    pos1 = rank1 + jnp.sum(jnp.where(e1[None, :] == experts[:, None], starts[:, None], 0), axis=0)
    pos2 = rank2 + jnp.sum(jnp.where(e2[None, :] == experts[:, None], starts[:, None], 0), axis=0)
    tile_start = jnp.arange(n_tiles, dtype=jnp.int32) * ROW_TILE
    tile_expert = jnp.sum((tile_start[:, None] >= ends[None, :]).astype(jnp.int32), axis=1)
    tile_expert = jnp.minimum(tile_expert, N_EXPERTS - 1)
    rows_left = jnp.sum(jnp.where(tile_expert[:, None] == experts, cnt + starts, 0), axis=1) - tile_start
    n_valid = jnp.clip(rows_left, 0, ROW_TILE).astype(jnp.int32)
    changed = jnp.concatenate([jnp.ones((1,), jnp.int32),
                               (tile_expert[1:] != tile_expert[:-1]).astype(jnp.int32)])
    last_used = jnp.maximum(ends[-1] // ROW_TILE - 1, 0)
    block = jnp.minimum(jnp.arange(n_tiles, dtype=jnp.int32), last_used)
    tile_expert = jnp.sum(jnp.where(block[:, None] == jnp.arange(n_tiles)[None, :],
                                    tile_expert[None, :], 0), axis=1)
    plane = n_tiles * ROW_TILE
    half_rows = jnp.concatenate([pos1, pos1 + plane, pos2, pos2 + plane])[None]
    return half_rows, tile_expert, n_valid, changed, block


def _sc_mesh():
    return plsc.VectorSubcoreMesh(core_axis_name="c", subcore_axis_name="s")


def _sc_scatter_rows(rows, idx, n_out):
    t, width = rows.shape
    steps = t // SC_WINDOW

    @pl.kernel(out_type=jax.ShapeDtypeStruct((n_out, width), rows.dtype), mesh=_sc_mesh(),
               scratch_types=[], name="moe_scatter")
    def scatter(rows_hbm, idx_hbm, out_hbm):
        def body(rows_vmem, idx_vmem):
            pltpu.sync_copy(rows_vmem, out_hbm.at[idx_vmem.at[0]])

        pltpu.emit_pipeline(
            body,
            grid=(2, steps),
            in_specs=[pl.BlockSpec((SC_WINDOW, width), lambda k, j: (j, 0)),
                      pl.BlockSpec((1, SC_WINDOW), lambda k, j: (0, k * steps + j))],
            out_specs=[],
            core_axis_name=("c", "s"),
            dimension_semantics=(pltpu.PARALLEL, pltpu.PARALLEL),
        )(rows_hbm, idx_hbm)

    return scatter(rows, idx)


def _sc_gather_rows(table, idx):
    m = idx.shape[1]
    width = table.shape[1]
    steps = m // (2 * SC_WINDOW)

    @pl.kernel(out_type=jax.ShapeDtypeStruct((m, width), table.dtype), mesh=_sc_mesh(),
               scratch_types=[], name="moe_gather")
    def gather(table_hbm, idx_hbm, out_hbm):
        def body(idx_vmem, out_vmem):
            pltpu.sync_copy(table_hbm.at[idx_vmem.at[0]], out_vmem)

        pltpu.emit_pipeline(
            body,
            grid=(2, steps),
            in_specs=[pl.BlockSpec((1, SC_WINDOW), lambda k, j: (0, k * steps + j))],
            out_specs=[pl.BlockSpec((SC_WINDOW, width), lambda k, j: (k * steps + j, 0))],
            core_axis_name=("c", "s"),
            dimension_semantics=(pltpu.PARALLEL, pltpu.PARALLEL),
        )(idx_hbm, out_hbm)

    return gather(table, idx)


def _experts_kernel(te_ref, nv_ref, new_ref, blk_ref, xs_ref, wg_ref, wu_ref, wd_ref, ys_ref,
                    wg_s, wu_s, wd_s):
    r = pl.program_id(0)
    n_valid = nv_ref[r]

    @pl.when((new_ref[r] == 1) & (n_valid > 0))
    def _():
        wg_s[...] = wg_ref[...].astype(BF16)
        wu_s[...] = wu_ref[...].astype(BF16)
        wd_s[...] = wd_ref[...].astype(BF16)

    @pl.when(n_valid > 0)
    def _():
        parts = [p.astype(BF16) for p in _unpack_rows(xs_ref[0]) + _unpack_rows(xs_ref[1])]
        cols = (0, 2 * SC_ROW, SC_ROW, 3 * SC_ROW)
        hg = sum(_dot(p, wg_s[c:c + SC_ROW, :]) for p, c in zip(parts, cols))
        hu = sum(_dot(p, wu_s[c:c + SC_ROW, :]) for p, c in zip(parts, cols))
        row = lax.broadcasted_iota(jnp.int32, (ROW_TILE, 1), 0)
        hid = jnp.where(row < n_valid, hg * _sigmoid(hg) * hu, 0.0).astype(BF16)
        _split_planes(_pack_rows(_dot(hid, wd_s[...])), ys_ref)


def _experts(xs, tile_expert, n_valid, changed, block, w_gate, w_up, w_down, layer):
    n_tiles = xs.shape[1] // ROW_TILE
    base = layer * N_EXPERTS
    w_spec = pl.BlockSpec((None, D_MODEL, D_EXPERT), lambda r, te, nv, new, blk: (base + te[r], 0, 0))
    grid_spec = pltpu.PrefetchScalarGridSpec(
        num_scalar_prefetch=4,
        grid=(n_tiles,),
        in_specs=[
            pl.BlockSpec((2, ROW_TILE, SC_ROW), lambda r, te, nv, new, blk: (0, blk[r], 0)),
            w_spec, w_spec,
            pl.BlockSpec((None, D_EXPERT, D_MODEL), lambda r, te, nv, new, blk: (base + te[r], 0, 0)),
        ],
        out_specs=pl.BlockSpec((2, ROW_TILE, SC_ROW), lambda r, te, nv, new, blk: (0, blk[r], 0)),
        scratch_shapes=[
            pltpu.VMEM((D_MODEL, D_EXPERT), BF16),
            pltpu.VMEM((D_MODEL, D_EXPERT), BF16),
            pltpu.VMEM((D_EXPERT, D_MODEL), BF16),
        ],
    )
    return pl.pallas_call(
        _experts_kernel,
        grid_spec=grid_spec,
        out_shape=jax.ShapeDtypeStruct((2, n_tiles * ROW_TILE, SC_ROW), U32),
        compiler_params=_params(("arbitrary",)),
        name="experts",
    )(tile_expert, n_valid, changed, block, xs, w_gate, w_up, w_down)


def _ple_kernel(h_ref, yg_ref, meta_t_ref, p_ref, wg_ref, bg_ref, wp_ref, gf_ref, o_ref, *,
                final_norm):
    meta = jnp.transpose(meta_t_ref[...])
    w1 = meta[:, 4:5]
    w2 = meta[:, 5:6]
    q0, q2 = (w1 * u + w2 * v for u, v in zip(_unpack_rows(yg_ref[0]), _unpack_rows(yg_ref[2])))
    q1, q3 = (w1 * u + w2 * v for u, v in zip(_unpack_rows(yg_ref[1]), _unpack_rows(yg_ref[3])))
    moe = jnp.concatenate([q0, q1, q2, q3], axis=1)
    h = h_ref[...] + moe
    gate = _sigmoid(_dot(_rms(h).astype(BF16), wg_ref[...]) + bg_ref[...])
    h = h + gate * _dot(p_ref[...].astype(BF16), wp_ref[...])
    o_ref[...] = _rms(h, gf_ref[...]) if final_norm else h


def _ple(h1, yg, meta_t, p2, w_gate, b_gate, w_proj, g_final, final_norm, tm):
    t = h1.shape[0]
    return pl.pallas_call(
        functools.partial(_ple_kernel, final_norm=final_norm),
        grid=(t // tm,),
        in_specs=[
            pl.BlockSpec((tm, D_MODEL), lambda i: (i, 0)),
            pl.BlockSpec((4, tm, SC_ROW), lambda i: (0, i, 0)),
            pl.BlockSpec((META_ROWS, tm), lambda i: (0, i)),
            pl.BlockSpec((tm, D_PLE), lambda i: (i, 0)),
            pl.BlockSpec((D_MODEL, D_MODEL), lambda i: (0, 0)),
            pl.BlockSpec((1, D_MODEL), lambda i: (0, 0)),
            pl.BlockSpec((D_PLE, D_MODEL), lambda i: (0, 0)),
            pl.BlockSpec((1, D_MODEL), lambda i: (0, 0)),
        ],
        out_specs=pl.BlockSpec((tm, D_MODEL), lambda i: (i, 0)),
        out_shape=jax.ShapeDtypeStruct((t, D_MODEL), F32),
        compiler_params=_params(("parallel",)),
        name="ple",
    )(h1, yg, meta_t, p2, w_gate, b_gate, w_proj, g_final)


def kernel(x, p, g_mix, w_in, pool_w, pool_scale, ssm_a_re, ssm_a_im, ssm_log_dt, ssm_b_re,
           ssm_b_im, ssm_c_re, ssm_c_im, ssm_d, glu_w, glu_b, w_out, g_ffn, router_grp_w,
           router_grp_b, router_exp_w, router_exp_b, exp_w_gate, exp_w_up, exp_w_down, g_ple,
           ple_gate_w, ple_gate_b, ple_proj_w, g_final):
    bsz, seq, dm = x.shape
    depth = g_mix.shape[0]
    t = bsz * seq
    seq_chunks = seq // CHUNK
    nch = t // CHUNK
    tm = 512
    n_sorted = 2 * t + N_EXPERTS * ROW_TILE
    w_gate_all = exp_w_gate.reshape(depth * N_EXPERTS, dm, D_EXPERT)
    w_up_all = exp_w_up.reshape(depth * N_EXPERTS, dm, D_EXPERT)
    w_down_all = exp_w_down.reshape(depth * N_EXPERTS, D_EXPERT, dm)

    perm = jnp.asarray(_regroup_matrix(), BF16)
    h = x.reshape(t, dm)
    for i in range(depth):
        w_in_b = (g_mix[i][:, None] * w_in[i]).astype(BF16)
        zp, ut = _in_proj(h.reshape(nch, CHUNK, dm), w_in_b[:, :D_POOL],
                          jnp.transpose(w_in_b[:, D_POOL:]), perm)
        a = _pool(zp.reshape(bsz, seq, D_POOL), pool_w[i], pool_scale[i][None])
        yt = _ssm(ut, ssm_a_re[i], ssm_a_im[i], ssm_log_dt[i], ssm_b_re[i], ssm_b_im[i],
                  ssm_c_re[i], ssm_c_im[i], ssm_d[i], seq_chunks)
        h = _mix_out(h.reshape(nch, CHUNK, dm), a.reshape(nch, CHUNK, D_POOL), yt,
                     jnp.transpose(glu_w[i]).astype(BF16), glu_b[i][:, None],
                     w_out[i].astype(BF16), perm).reshape(t, dm)

        w_router = jnp.concatenate(
            [router_grp_w[i],
             jnp.transpose(router_exp_w[i], (1, 0, 2)).reshape(dm, N_EXPERTS),
             jnp.zeros((dm, ROUTER_W - N_EXPERT_GROUPS - N_EXPERTS), F32)], axis=1)
        b_router = jnp.concatenate(
            [router_grp_b[i], router_exp_b[i].reshape(N_EXPERTS),
             jnp.zeros((ROUTER_W - N_EXPERT_GROUPS - N_EXPERTS,), F32)])[None]
        vp, meta_t, counts = _router(h, g_ffn[i][None], w_router, b_router, tm)
        idx, tile_expert, n_valid, changed, block = _plan(meta_t, counts, n_sorted // ROW_TILE)
        xs = _sc_scatter_rows(vp.reshape(2 * t, SC_ROW), idx, 2 * n_sorted)
        ys = _experts(xs.reshape(2, n_sorted, SC_ROW), tile_expert, n_valid, changed, block,
                      w_gate_all, w_up_all, w_down_all, i)
        yg2 = _sc_gather_rows(ys.reshape(2 * n_sorted, SC_ROW), idx).reshape(4, t, SC_ROW)
        h = _ple(h, yg2, meta_t, p[i].reshape(t, D_PLE),
                 (g_ple[i][:, None] * ple_gate_w[i]).astype(BF16), ple_gate_b[i][None],
                 ple_proj_w[i].astype(BF16), g_final[None], i == depth - 1, tm)
    return h.reshape(bsz, seq, dm)
```

```python
import functools
import math

import numpy as np
import jax
import jax.numpy as jnp
from jax import lax
from jax.experimental import pallas as pl
from jax.experimental.pallas import tpu as pltpu
from jax.experimental.pallas import tpu_sc as plsc

F32 = jnp.float32
BF16 = jnp.bfloat16
U32 = jnp.uint32

D_MODEL = 1024
D_POOL = 512
D_SSM = 512
POOL_WINDOWS = (2, 4, 8, 16)
POOL_GROUP = 128
SSM_GROUP = 16
N_SSM_GROUPS = 32
SSM_STATE = 64
N_EXPERT_GROUPS = 4
EXPERTS_PER_GROUP = 8
N_EXPERTS = N_EXPERT_GROUPS * EXPERTS_PER_GROUP
D_EXPERT = 256
D_PLE = 256
RMS_EPS = 1e-6

LANES = 128
CHUNK = 32
CHUNK_W = CHUNK * SSM_GROUP
T_SUB = 8
S_TILE = 8
C_TILE = 128
ROUTER_W = LANES
HALF = D_MODEL // 2
ROW_TILE = 512
SC_WINDOW = 128
SC_ROW = HALF // 2
META_ROWS = 8
GATHER_PARTS = 4
VMEM_LIMIT = 56 * 1024 * 1024


def _dot(a, b):
    return jnp.dot(a, b, preferred_element_type=F32)


def _dot_nt(a, b):
    return lax.dot_general(a, b, (((1,), (1,)), ((), ())), preferred_element_type=F32)


def _dot_tn(a, b):
    return lax.dot_general(a, b, (((0,), (0,)), ((), ())), preferred_element_type=F32)


def _rms(x, g=None):
    y = x * lax.rsqrt(jnp.mean(x * x, axis=-1, keepdims=True) + RMS_EPS)
    return y if g is None else y * g


def _sigmoid(x):
    return 1.0 / (1.0 + jnp.exp(-x))


def _params(sem):
    return pltpu.CompilerParams(dimension_semantics=sem, vmem_limit_bytes=VMEM_LIMIT)


def _regroup_matrix():
    n = C_TILE * S_TILE
    src = np.arange(n)
    dst = (src % S_TILE) * C_TILE + src // S_TILE
    p = np.zeros((n, n), np.float32)
    p[src, dst] = 1.0
    return p


def _in_proj_kernel(x_ref, wp_ref, wst_ref, perm_ref, zp_ref, ut_ref):
    nc = x_ref.shape[0]
    u = _rms(x_ref[...].reshape(nc * S_TILE, D_MODEL)).astype(BF16)
    zp_ref[...] = _dot(u, wp_ref[...]).reshape(zp_ref.shape)
    zt = _dot_nt(wst_ref[...], u).astype(BF16)
    zt = _dot(zt, perm_ref[...]).astype(BF16)
    for j in range(S_TILE):
        ut_ref[:, j, :, :] = zt[:, j * nc:(j + 1) * nc].reshape(N_SSM_GROUPS, SSM_GROUP, nc)


def _in_proj(x3, w_pool, w_ssm_t, perm):
    nch = x3.shape[0]
    return pl.pallas_call(
        _in_proj_kernel,
        grid=(nch // C_TILE, CHUNK // S_TILE),
        in_specs=[
            pl.BlockSpec((C_TILE, S_TILE, D_MODEL), lambda c, s: (c, s, 0)),
            pl.BlockSpec((D_MODEL, D_POOL), lambda c, s: (0, 0)),
            pl.BlockSpec((D_SSM, D_MODEL), lambda c, s: (0, 0)),
            pl.BlockSpec((C_TILE * S_TILE, C_TILE * S_TILE), lambda c, s: (0, 0)),
        ],
        out_specs=[
            pl.BlockSpec((C_TILE, S_TILE, D_POOL), lambda c, s: (c, s, 0)),
            pl.BlockSpec((N_SSM_GROUPS, None, S_TILE, SSM_GROUP, C_TILE), lambda c, s: (0, c, s, 0, 0)),
        ],
        out_shape=[
            jax.ShapeDtypeStruct((nch, CHUNK, D_POOL), F32),
            jax.ShapeDtypeStruct((N_SSM_GROUPS, nch // C_TILE, CHUNK, SSM_GROUP, C_TILE), BF16),
        ],
        compiler_params=_params(("parallel", "parallel")),
        name="in_proj",
    )(x3, w_pool, w_ssm_t, perm)


def _shift_rows(x, d, row, n):
    if d == 0:
        return x
    r = pltpu.roll(x, d % n, 0)
    if d > 0:
        return jnp.where(row >= d, r, 0.0)
    return jnp.where(row < n + d, r, 0.0)


def _pool_kernel(z_ref, w_ref, sc_ref, o_ref):
    n = z_ref.shape[0]
    gi = pl.program_id(1)
    row = lax.broadcasted_iota(jnp.int32, (n, 1), 0)

    for k, w in enumerate(POOL_WINDOWS):
        @pl.when(gi == k)
        def _(w=w):
            x = z_ref[...]
            half = w // 2
            pd, pu, span = x, x, 1
            while span < half:
                pd = pd + _shift_rows(pd, span, row, n)
                pu = pu + _shift_rows(pu, -span, row, n)
                span *= 2
            total = _shift_rows(pd, 1, row, n) + pu
            lo = jnp.maximum(row - half, 0)
            hi = jnp.minimum(row + half, n)
            cnt = (hi - lo).astype(F32)
            diff = (total / cnt - x).astype(BF16)
            o_ref[...] = _dot(diff, w_ref[...].astype(BF16)) * sc_ref[...]


def _pool(zp3, pool_w, pool_scale):
    b, s, _ = zp3.shape
    return pl.pallas_call(
        _pool_kernel,
        grid=(b, len(POOL_WINDOWS)),
        in_specs=[
            pl.BlockSpec((None, s, POOL_GROUP), lambda i, g: (i, 0, g)),
            pl.BlockSpec((None, POOL_GROUP, POOL_GROUP), lambda i, g: (g, 0, 0)),
            pl.BlockSpec((1, POOL_GROUP), lambda i, g: (0, g)),
        ],
        out_specs=pl.BlockSpec((None, s, POOL_GROUP), lambda i, g: (i, 0, g)),
        out_shape=jax.ShapeDtypeStruct((b, s, D_POOL), F32),
        compiler_params=_params(("parallel", "parallel")),
        name="pool",
    )(zp3, pool_w, pool_scale)


def _expand_consts():
    time = np.arange(CHUNK_W) // SSM_GROUP
    def onehot(e):
        m = np.zeros((CHUNK_W, LANES), np.float32)
        m[np.arange(CHUNK_W), e] = 1.0
        return m
    return np.stack([
        onehot(CHUNK - 1 - time),
        onehot(time),
        onehot(time + 1),
        onehot(CHUNK - time),
    ])


def _cmul_packed(x, p, q):
    return x * p + pltpu.roll(x, LANES // 2, 1) * q


def _ssm_kernel(u_ref, are_ref, aim_ref, ldt_ref, bt_re_ref, bt_im_ref, ccr_ref, ct_re_ref,
                ct_im_ref, d_ref, exp_ref, y_ref, *, seq_chunks):
    n_ct = u_ref.shape[0]
    nch = n_ct * C_TILE
    half = LANES // 2
    lane = lax.broadcasted_iota(jnp.int32, (1, LANES), 1)
    lo_half = lane < half

    def direction(di):
        a_re = are_ref[di]
        a_im = aim_ref[di]
        dt = jnp.exp(ldt_ref[di])
        mag = jnp.exp(a_re * dt)
        ang = a_im * dt
        lam = jnp.where(lo_half, mag * jnp.cos(ang), mag * jnp.sin(ang))
        lb_re = mag * jnp.cos(ang)
        lb_im = mag * jnp.sin(ang)
        den = a_re * a_re + a_im * a_im
        f_re = ((lb_re - 1.0) * a_re + lb_im * a_im) / den
        f_im = (lb_im * a_re - (lb_re - 1.0) * a_im) / den
        return lam, f_re, f_im

    def power_table(lam):
        e = lax.broadcasted_iota(jnp.int32, (LANES, 1), 0)
        tab = jnp.where(lo_half, 1.0, 0.0) * jnp.ones((LANES, 1), F32)
        sq = lam
        for k in range(7):
            p = jnp.where(lo_half, sq, pltpu.roll(sq, half, 1))
            q = jnp.where(lo_half, -pltpu.roll(sq, half, 1), sq)
            tab = jnp.where(((e >> k) & 1) == 1, _cmul_packed(tab, p, q), tab)
            sq = _cmul_packed(sq, p, q)
        return tab

    def tile_rows(x16):
        return jnp.broadcast_to(x16[None], (CHUNK, SSM_GROUP, LANES)).reshape(CHUNK_W, LANES)

    def expanded(tab, which, v_re, v_im, conj_sign):
        lexp = _dot(exp_ref[which], tab.astype(BF16))
        if conj_sign > 0:
            p = jnp.where(lo_half, v_re, v_re)
            q = jnp.where(lo_half, -v_im, v_im)
        else:
            p = jnp.where(lo_half, v_re, -v_re)
            q = jnp.where(lo_half, -v_im, -v_im)
        return lexp * tile_rows(p) + pltpu.roll(lexp, half, 1) * tile_rows(q)

    lam_f, ff_re, ff_im = direction(0)
    lam_b, fb_re, fb_im = direction(1)
    tab_f = power_table(lam_f)
    tab_b = power_table(lam_b)

    def bbar(bt_re, bt_im, f_re, f_im):
        return bt_re * f_re - bt_im * f_im, bt_re * f_im + bt_im * f_re

    bf_re, bf_im = bbar(bt_re_ref[0], bt_im_ref[0], ff_re, ff_im)
    bb_re, bb_im = bbar(bt_re_ref[1], bt_im_ref[1], fb_re, fb_im)

    pb1 = expanded(tab_f, 0, bf_re, bf_im, 1)
    pb2 = expanded(tab_b, 1, bb_re, bb_im, 1)
    pb3 = expanded(tab_b, 2, bb_re, bb_im, 1)
    ft_f = expanded(tab_f, 2, ct_re_ref[0], ct_im_ref[0], -1)
    ft_b = expanded(tab_b, 3, ct_re_ref[1], ct_im_ref[1], -1)

    row_w = lax.broadcasted_iota(jnp.int32, (CHUNK_W, 1), 0)
    last_blk = row_w >= CHUNK_W - SSM_GROUP
    pb2_lag0 = jnp.where(last_blk, pltpu.roll(pb2, CHUNK_W - SSM_GROUP, 0), 0.0)
    ccr_f = ccr_ref[0].astype(BF16)
    ccr_b = ccr_ref[1].astype(BF16)
    r_lo = _dot_nt(ccr_f, pb1.astype(BF16)) + _dot_nt(ccr_b, pb2_lag0.astype(BF16))
    co = lax.broadcasted_iota(jnp.int32, (SSM_GROUP, CHUNK_W), 0)
    col = lax.broadcasted_iota(jnp.int32, (SSM_GROUP, CHUNK_W), 1)
    r_lo = r_lo + jnp.where(col == CHUNK_W - SSM_GROUP + co, d_ref[...], 0.0)
    r_hi = _dot_nt(ccr_b, pb3.astype(BF16))
    r_t = jnp.concatenate([r_lo, r_hi], axis=1)
    g_t = jnp.concatenate(
        [pltpu.roll(r_t, SSM_GROUP * (tl + 1), 1) for tl in range(T_SUB)], axis=0
    ).astype(BF16)

    u = jnp.concatenate([u_ref[ct].reshape(CHUNK_W, C_TILE) for ct in range(n_ct)],
                        axis=1)
    e_mat = jnp.concatenate([pb1, pb2], axis=1).astype(BF16)
    xend = _dot_tn(e_mat, u)
    lanec = lax.broadcasted_iota(jnp.int32, (1, nch), 1) % seq_chunks
    ns = SSM_STATE

    def scan(re, im, tab, forward):
        lam_col = jnp.transpose(tab[CHUNK:CHUNK + 8, :])[:, 0:1]
        a, b = lam_col[:ns], lam_col[ns:]
        n_steps = int(math.log2(seq_chunks))

        def shifted(v, d):
            if forward:
                return jnp.where(lanec >= d, pltpu.roll(v, d, 1), 0.0)
            return jnp.where(lanec < seq_chunks - d, pltpu.roll(v, nch - d, 1), 0.0)

        for k in range(n_steps):
            sr, si = shifted(re, 1 << k), shifted(im, 1 << k)
            re, im = re + (sr * a - si * b), im + (sr * b + si * a)
            a, b = a * a - b * b, 2.0 * a * b
        return shifted(re, 1), shifted(im, 1)

    f_re, f_im = scan(xend[:ns], xend[ns:2 * ns], tab_f, True)
    b_re, b_im = scan(xend[2 * ns:3 * ns], xend[3 * ns:], tab_b, False)
    xin = jnp.concatenate([f_re, f_im, b_re, b_im], axis=0).astype(BF16)
    f_t = jnp.concatenate([ft_f, ft_b], axis=1).astype(BF16)

    toeplitz = jnp.concatenate(
        [g_t[:, CHUNK_W - LANES * th:2 * CHUNK_W - LANES * th] for th in range(CHUNK // T_SUB)],
        axis=0)
    y_t = _dot(toeplitz, u) + _dot(f_t, xin)
    for ct in range(n_ct):
        y_ref[ct] = y_t[:, ct * C_TILE:(ct + 1) * C_TILE].reshape(
            CHUNK, SSM_GROUP, C_TILE).astype(y_ref.dtype)


def _ssm(ut, a_re, a_im, log_dt, b_re, b_im, c_re, c_im, d, seq_chunks):
    g, n_ct = ut.shape[:2]
    n = SSM_STATE

    def lane_vec(a):
        a = jnp.transpose(a, (1, 0, 2))
        return jnp.concatenate([a, a], axis=-1)[:, :, None, :]

    are = lane_vec(a_re)
    aim = lane_vec(a_im)
    ldt = lane_vec(jnp.broadcast_to(log_dt[..., None], (2, g, n)))

    def bt(b):
        b = jnp.transpose(b, (1, 0, 3, 2))
        return jnp.concatenate([b, b], axis=-1)

    def ct(c):
        c = jnp.transpose(c, (1, 0, 2, 3))
        return jnp.concatenate([c, c], axis=-1)

    ccr = jnp.concatenate([jnp.transpose(c_re, (1, 0, 2, 3)),
                           -jnp.transpose(c_im, (1, 0, 2, 3))], axis=-1)
    d_col = d.reshape(g, SSM_GROUP, 1)
    exp_c = jnp.asarray(_expand_consts(), BF16)

    vec_spec = pl.BlockSpec((None, 2, 1, LANES), lambda i: (i, 0, 0, 0))
    mat_spec = pl.BlockSpec((None, 2, SSM_GROUP, LANES), lambda i: (i, 0, 0, 0))
    return pl.pallas_call(
        functools.partial(_ssm_kernel, seq_chunks=seq_chunks),
        grid=(g,),
        in_specs=[
            pl.BlockSpec((None, n_ct, CHUNK, SSM_GROUP, C_TILE), lambda i: (i, 0, 0, 0, 0)),
            vec_spec, vec_spec, vec_spec,
            mat_spec, mat_spec, mat_spec, mat_spec, mat_spec,
            pl.BlockSpec((None, SSM_GROUP, 1), lambda i: (i, 0, 0)),
            pl.BlockSpec((4, CHUNK_W, LANES), lambda i: (0, 0, 0)),
        ],
        out_specs=pl.BlockSpec((None, n_ct, CHUNK, SSM_GROUP, C_TILE), lambda i: (i, 0, 0, 0, 0)),
        out_shape=jax.ShapeDtypeStruct(ut.shape, BF16),
        compiler_params=_params(("parallel",)),
        name="ssm",
    )(ut, are, aim, ldt, bt(b_re), bt(b_im), ccr, ct(c_re), ct(c_im), d_col, exp_c)


def _mix_out_kernel(x_ref, a_ref, yt_ref, gwt_ref, gb_ref, wo_ref, perm_ref, h_ref):
    nc = x_ref.shape[0]
    rows = nc * S_TILE
    y = jnp.concatenate([yt_ref[:, j, :, :].reshape(D_SSM, nc) for j in range(S_TILE)],
                        axis=1).astype(F32)
    z = 0.5 * y * (1.0 + jnp.tanh(math.sqrt(2.0 / math.pi) * (y + 0.044715 * (y * y * y))))
    gate = _sigmoid(_dot(gwt_ref[...], z.astype(BF16)) + gb_ref[...])
    s = (z * gate).astype(BF16)
    s = _dot_nt(s, perm_ref[...]).astype(BF16)
    h = (x_ref[...].reshape(rows, D_MODEL)
         + _dot(a_ref[...].reshape(rows, D_POOL).astype(BF16), wo_ref[:D_POOL, :])
         + _dot_tn(s, wo_ref[D_POOL:, :]))
    h_ref[...] = h.reshape(h_ref.shape)


def _mix_out(x3, a3, yt, glu_w_t, glu_b_col, w_out, perm):
    nch = x3.shape[0]
    return pl.pallas_call(
        _mix_out_kernel,
        grid=(nch // C_TILE, CHUNK // S_TILE),
        in_specs=[
            pl.BlockSpec((C_TILE, S_TILE, D_MODEL), lambda c, t: (c, t, 0)),
            pl.BlockSpec((C_TILE, S_TILE, D_POOL), lambda c, t: (c, t, 0)),
            pl.BlockSpec((N_SSM_GROUPS, None, S_TILE, SSM_GROUP, C_TILE), lambda c, t: (0, c, t, 0, 0)),
            pl.BlockSpec((D_SSM, D_SSM), lambda c, t: (0, 0)),
            pl.BlockSpec((D_SSM, 1), lambda c, t: (0, 0)),
            pl.BlockSpec((D_MODEL, D_MODEL), lambda c, t: (0, 0)),
            pl.BlockSpec((C_TILE * S_TILE, C_TILE * S_TILE), lambda c, t: (0, 0)),
        ],
        out_specs=pl.BlockSpec((C_TILE, S_TILE, D_MODEL), lambda c, t: (c, t, 0)),
        out_shape=jax.ShapeDtypeStruct((nch, CHUNK, D_MODEL), F32),
        compiler_params=_params(("parallel", "parallel")),
        name="mix_out",
    )(x3, a3, yt, glu_w_t, glu_b_col, w_out, perm)


def _pack_rows(x):
    b = lax.bitcast_convert_type(x.astype(BF16).astype(F32), U32)
    return (b[:, :HALF] & jnp.uint32(0xFFFF0000)) | (b[:, HALF:] >> 16)


def _unpack_rows(w):
    lo = lax.bitcast_convert_type(w & jnp.uint32(0xFFFF0000), F32)
    hi = lax.bitcast_convert_type(w << 16, F32)
    return lo, hi


def _split_bf16(x):
    hi = x.astype(BF16)
    return hi, (x - hi.astype(F32)).astype(BF16)


def _route(v32, wr_ref, br_ref):
    v_hi, v_lo = _split_bf16(v32)
    w_hi, w_lo = _split_bf16(wr_ref[...])
    logits = _dot(v_hi, w_hi) + (_dot(v_lo, w_hi) + _dot(v_hi, w_lo)) + br_ref[...]
    lane = lax.broadcasted_iota(jnp.int32, logits.shape, 1).astype(F32)
    neg = -jnp.inf
    none = float(ROUTER_W)
    lg = jnp.where(lane < N_EXPERT_GROUPS, logits, neg)
    mg = jnp.max(lg, axis=1, keepdims=True)
    grp_p = 1.0 / jnp.sum(jnp.exp(lg - mg), axis=1, keepdims=True)
    grp_idx = jnp.min(jnp.where(lg == mg, lane, none), axis=1, keepdims=True)
    first = N_EXPERT_GROUPS + grp_idx * EXPERTS_PER_GROUP
    sel = (lane >= first) & (lane < first + EXPERTS_PER_GROUP)
    le = jnp.where(sel, logits, neg)
    m1 = jnp.max(le, axis=1, keepdims=True)
    i1 = jnp.min(jnp.where(le == m1, lane, none), axis=1, keepdims=True)
    z = jnp.sum(jnp.exp(le - m1), axis=1, keepdims=True)
    le2 = jnp.where(lane == i1, neg, le)
    m2 = jnp.max(le2, axis=1, keepdims=True)
    i2 = jnp.min(jnp.where(le2 == m2, lane, none), axis=1, keepdims=True)
    p1 = 1.0 / z
    p2 = jnp.exp(m2 - m1) / z
    tot = p1 + p2
    return i1, i2, grp_p * (p1 / tot), grp_p * (p2 / tot)


def _split_planes(packed, ref):
    ref[0] = packed[:, :SC_ROW]
    ref[1] = packed[:, SC_ROW:]


def _router_kernel(h_ref, g_ref, wr_ref, br_ref, vp_ref, meta_t_ref, cnt_ref, carry_ref):
    @pl.when(pl.program_id(0) == 0)
    def _():
        carry_ref[...] = jnp.zeros_like(carry_ref)

    v32 = _rms(h_ref[...], g_ref[...])
    _split_planes(_pack_rows(v32), vp_ref)
    i1, i2, w1, w2 = _route(v32, wr_ref, br_ref)
    tm = v32.shape[0]
    lane = lax.broadcasted_iota(jnp.int32, (tm, ROUTER_W), 1).astype(F32)
    onehot = jnp.where(lane == i1, 1.0, jnp.where(lane == i2, 1.0, 0.0))
    r = lax.broadcasted_iota(jnp.int32, (tm, tm), 0)
    c = lax.broadcasted_iota(jnp.int32, (tm, tm), 1)
    below = jnp.where(c < r, 1.0, 0.0).astype(BF16)
    before = _dot(below, onehot.astype(BF16)) + carry_ref[...]
    rank1 = jnp.sum(jnp.where(lane == i1, before, 0.0), axis=1, keepdims=True)
    rank2 = jnp.sum(jnp.where(lane == i2, before, 0.0), axis=1, keepdims=True)
    carry = carry_ref[...] + jnp.sum(onehot, axis=0, keepdims=True)
    carry_ref[...] = carry
    cnt_ref[...] = carry
    e1 = i1 - N_EXPERT_GROUPS
    e2 = i2 - N_EXPERT_GROUPS
    meta = jnp.where(lane == 0, e1, jnp.where(lane == 1, e2, jnp.where(
        lane == 2, rank1, jnp.where(lane == 3, rank2, jnp.where(
            lane == 4, w1, jnp.where(lane == 5, w2, 0.0))))))
    meta_t_ref[...] = jnp.transpose(meta)[:META_ROWS, :]


def _router(h1, g_ffn, w_router, b_router, tm):
    t = h1.shape[0]
    return pl.pallas_call(
        _router_kernel,
        grid=(t // tm,),
        in_specs=[
            pl.BlockSpec((tm, D_MODEL), lambda i: (i, 0)),
            pl.BlockSpec((1, D_MODEL), lambda i: (0, 0)),
            pl.BlockSpec((D_MODEL, ROUTER_W), lambda i: (0, 0)),
            pl.BlockSpec((1, ROUTER_W), lambda i: (0, 0)),
        ],
        out_specs=[
            pl.BlockSpec((2, tm, SC_ROW), lambda i: (0, i, 0)),
            pl.BlockSpec((META_ROWS, tm), lambda i: (0, i)),
            pl.BlockSpec((1, ROUTER_W), lambda i: (0, 0)),
        ],
        out_shape=[
            jax.ShapeDtypeStruct((2, t, SC_ROW), U32),
            jax.ShapeDtypeStruct((META_ROWS, t), F32),
            jax.ShapeDtypeStruct((1, ROUTER_W), F32),
        ],
        scratch_shapes=[pltpu.VMEM((1, ROUTER_W), F32)],
        compiler_params=_params(("arbitrary",)),
        name="router",
    )(h1, g_ffn, w_router, b_router)


def _plan(meta_t, counts, n_tiles):
    e1 = meta_t[0].astype(jnp.int32)
    e2 = meta_t[1].astype(jnp.int32)
    rank1 = meta_t[2].astype(jnp.int32)
    rank2 = meta_t[3].astype(jnp.int32)
    cnt = counts[0, N_EXPERT_GROUPS:N_EXPERT_GROUPS + N_EXPERTS].astype(jnp.int32)
    padded = ((cnt + ROW_TILE - 1) // ROW_TILE) * ROW_TILE
    ends = jnp.cumsum(padded)
    starts = ends - padded
    experts = jnp.arange(N_EXPERTS, dtype=jnp.int32)
    pos1 = rank1 + jnp.sum(jnp.where(e1[None, :] == experts[:, None], starts[:, None], 0), axis=0)
    pos2 = rank2 + jnp.sum(jnp.where(e2[None, :] == experts[:, None], starts[:, None], 0), axis=0)
    tile_start = jnp.arange(n_tiles, dtype=jnp.int32) * ROW_TILE
    tile_expert = jnp.sum((tile_start[:, None] >= ends[None, :]).astype(jnp.int32), axis=1)
    tile_expert = jnp.minimum(tile_expert, N_EXPERTS - 1)
    rows_left = jnp.sum(jnp.where(tile_expert[:, None] == experts, cnt + starts, 0), axis=1) - tile_start
    n_valid = jnp.clip(rows_left, 0, ROW_TILE).astype(jnp.int32)
    last_used = jnp.maximum(ends[-1] // ROW_TILE - 1, 0)
    block = jnp.minimum(jnp.arange(n_tiles, dtype=jnp.int32), last_used)
    tile_expert = jnp.sum(jnp.where(block[:, None] == jnp.arange(n_tiles)[None, :],
                                    tile_expert[None, :], 0), axis=1)
    plane = n_tiles * ROW_TILE
    half_rows = jnp.concatenate([pos1, pos1 + plane, pos2, pos2 + plane])[None]
    return half_rows, tile_expert, n_valid, block


def _sc_mesh():
    return plsc.VectorSubcoreMesh(core_axis_name="c", subcore_axis_name="s")


def _sc_scatter_rows(rows, idx, n_out):
    t, width = rows.shape
    steps = t // SC_WINDOW

    @pl.kernel(out_type=jax.ShapeDtypeStruct((n_out, width), rows.dtype), mesh=_sc_mesh(),
               scratch_types=[], name="moe_scatter")
    def scatter(rows_hbm, idx_hbm, out_hbm):
        def body(rows_vmem, idx_vmem):
            pltpu.sync_copy(rows_vmem, out_hbm.at[idx_vmem.at[0]])

        pltpu.emit_pipeline(
            body,
            grid=(2, steps),
            in_specs=[pl.BlockSpec((SC_WINDOW, width), lambda k, j: (j, 0)),
                      pl.BlockSpec((1, SC_WINDOW), lambda k, j: (0, k * steps + j))],
            out_specs=[],
            core_axis_name=("c", "s"),
            dimension_semantics=(pltpu.PARALLEL, pltpu.PARALLEL),
        )(rows_hbm, idx_hbm)

    return scatter(rows, idx)


def _sc_gather_rows(table, idx):
    m = idx.shape[1]
    width = table.shape[1]
    steps = m // (2 * SC_WINDOW)

    @pl.kernel(out_type=jax.ShapeDtypeStruct((m, width), table.dtype), mesh=_sc_mesh(),
               scratch_types=[], name="moe_gather")
    def gather(table_hbm, idx_hbm, out_hbm):
        def body(idx_vmem, out_vmem):
            pltpu.sync_copy(table_hbm.at[idx_vmem.at[0]], out_vmem)

        pltpu.emit_pipeline(
            body,
            grid=(2, steps),
            in_specs=[pl.BlockSpec((1, SC_WINDOW), lambda k, j: (0, k * steps + j))],
            out_specs=[pl.BlockSpec((SC_WINDOW, width), lambda k, j: (k * steps + j, 0))],
            core_axis_name=("c", "s"),
            dimension_semantics=(pltpu.PARALLEL, pltpu.PARALLEL),
        )(idx_hbm, out_hbm)

    return gather(table, idx)


def _cast_kernel(dep_ref, wg_ref, wu_ref, wd_ref, og_ref, ou_ref, od_ref):
    del dep_ref
    og_ref[...] = wg_ref[...].astype(BF16)
    ou_ref[...] = wu_ref[...].astype(BF16)
    od_ref[...] = wd_ref[...].astype(BF16)


def _cast_expert_weights(dep, w_gate, w_up, w_down, layer):
    base = layer * N_EXPERTS
    gu = pl.BlockSpec((None, D_MODEL, D_EXPERT), lambda e, dep: (base + e, 0, 0))
    dn = pl.BlockSpec((None, D_EXPERT, D_MODEL), lambda e, dep: (base + e, 0, 0))
    gu_o = pl.BlockSpec((None, D_MODEL, D_EXPERT), lambda e, dep: (e, 0, 0))
    dn_o = pl.BlockSpec((None, D_EXPERT, D_MODEL), lambda e, dep: (e, 0, 0))
    return pl.pallas_call(
        _cast_kernel,
        grid_spec=pltpu.PrefetchScalarGridSpec(
            num_scalar_prefetch=1, grid=(N_EXPERTS,),
            in_specs=[gu, gu, dn], out_specs=[gu_o, gu_o, dn_o]),
        out_shape=[
            jax.ShapeDtypeStruct((N_EXPERTS, D_MODEL, D_EXPERT), BF16),
            jax.ShapeDtypeStruct((N_EXPERTS, D_MODEL, D_EXPERT), BF16),
            jax.ShapeDtypeStruct((N_EXPERTS, D_EXPERT, D_MODEL), BF16),
        ],
        compiler_params=_params(("parallel",)),
        name="cast_experts",
    )(dep, w_gate, w_up, w_down)


def _experts_kernel(te_ref, nv_ref, blk_ref, xs_ref, wg_ref, wu_ref, wd_ref, ys_ref):
    r = pl.program_id(0)
    n_valid = nv_ref[r]

    @pl.when(n_valid > 0)
    def _():
        parts = [p.astype(BF16) for p in _unpack_rows(xs_ref[0]) + _unpack_rows(xs_ref[1])]
        cols = (0, 2 * SC_ROW, SC_ROW, 3 * SC_ROW)
        hg = sum(_dot(p, wg_ref[c:c + SC_ROW, :]) for p, c in zip(parts, cols))
        hu = sum(_dot(p, wu_ref[c:c + SC_ROW, :]) for p, c in zip(parts, cols))
        row = lax.broadcasted_iota(jnp.int32, (ROW_TILE, 1), 0)
        hid = jnp.where(row < n_valid, hg * _sigmoid(hg) * hu, 0.0).astype(BF16)
        _split_planes(_pack_rows(_dot(hid, wd_ref[...])), ys_ref)


def _experts(xs, tile_expert, n_valid, block, w_gate, w_up, w_down):
    n_tiles = xs.shape[1] // ROW_TILE
    w_spec = pl.BlockSpec((None, D_MODEL, D_EXPERT), lambda r, te, nv, blk: (te[r], 0, 0))
    grid_spec = pltpu.PrefetchScalarGridSpec(
        num_scalar_prefetch=3,
        grid=(n_tiles,),
        in_specs=[
            pl.BlockSpec((2, ROW_TILE, SC_ROW), lambda r, te, nv, blk: (0, blk[r], 0)),
            w_spec, w_spec,
            pl.BlockSpec((None, D_EXPERT, D_MODEL), lambda r, te, nv, blk: (te[r], 0, 0)),
        ],
        out_specs=pl.BlockSpec((2, ROW_TILE, SC_ROW), lambda r, te, nv, blk: (0, blk[r], 0)),
    )
    return pl.pallas_call(
        _experts_kernel,
        grid_spec=grid_spec,
        out_shape=jax.ShapeDtypeStruct((2, n_tiles * ROW_TILE, SC_ROW), U32),
        compiler_params=_params(("arbitrary",)),
        name="experts",
    )(tile_expert, n_valid, block, xs, w_gate, w_up, w_down)


def _ple_kernel(h_ref, yg_ref, meta_t_ref, p_ref, wg_ref, bg_ref, wp_ref, gf_ref, *rest, final_norm):
    o_ref = rest[-1]
    meta = jnp.transpose(meta_t_ref[...])
    w1 = meta[:, 4:5]
    w2 = meta[:, 5:6]
    q0, q2 = (w1 * u + w2 * v for u, v in zip(_unpack_rows(yg_ref[0]), _unpack_rows(yg_ref[2])))
    q1, q3 = (w1 * u + w2 * v for u, v in zip(_unpack_rows(yg_ref[1]), _unpack_rows(yg_ref[3])))
    moe = jnp.concatenate([q0, q1, q2, q3], axis=1)
    h = h_ref[...] + moe
    gate = _sigmoid(_dot(_rms(h).astype(BF16), wg_ref[...]) + bg_ref[...])
    h = h + gate * _dot(p_ref[...].astype(BF16), wp_ref[...])
    o_ref[...] = _rms(h, gf_ref[...]) if final_norm else h


def _ple(h1, yg, meta_t, p2, w_gate, b_gate, w_proj, g_final, final_norm, tm, part, prev_out):
    t = h1.shape[0]
    steps = t // tm // GATHER_PARTS
    off = part * steps
    in_specs = [
        pl.BlockSpec((tm, D_MODEL), lambda i: (i + off, 0)),
        pl.BlockSpec((4, tm, SC_ROW), lambda i: (0, i, 0)),
        pl.BlockSpec((META_ROWS, tm), lambda i: (0, i + off)),
        pl.BlockSpec((tm, D_PLE), lambda i: (i + off, 0)),
        pl.BlockSpec((D_MODEL, D_MODEL), lambda i: (0, 0)),
        pl.BlockSpec((1, D_MODEL), lambda i: (0, 0)),
        pl.BlockSpec((D_PLE, D_MODEL), lambda i: (0, 0)),
        pl.BlockSpec((1, D_MODEL), lambda i: (0, 0)),
    ]
    args = [h1, yg, meta_t, p2, w_gate, b_gate, w_proj, g_final]
    aliases = {}
    if prev_out is not None:
        in_specs.append(pl.BlockSpec(memory_space=pl.ANY))
        args.append(prev_out)
        aliases = {len(args) - 1: 0}
    return pl.pallas_call(
        functools.partial(_ple_kernel, final_norm=final_norm),
        grid=(steps,),
        in_specs=in_specs,
        out_specs=pl.BlockSpec((tm, D_MODEL), lambda i: (i + off, 0)),
        out_shape=jax.ShapeDtypeStruct((t, D_MODEL), F32),
        input_output_aliases=aliases,
        compiler_params=_params(("parallel",)),
        name="ple",
    )(*args)


def kernel(x, p, g_mix, w_in, pool_w, pool_scale, ssm_a_re, ssm_a_im, ssm_log_dt, ssm_b_re,
           ssm_b_im, ssm_c_re, ssm_c_im, ssm_d, glu_w, glu_b, w_out, g_ffn, router_grp_w,
           router_grp_b, router_exp_w, router_exp_b, exp_w_gate, exp_w_up, exp_w_down, g_ple,
           ple_gate_w, ple_gate_b, ple_proj_w, g_final):
    bsz, seq, dm = x.shape
    depth = g_mix.shape[0]
    t = bsz * seq
    seq_chunks = seq // CHUNK
    nch = t // CHUNK
    tm = 512
    n_sorted = 2 * t + N_EXPERTS * ROW_TILE
    w_gate_all = exp_w_gate.reshape(depth * N_EXPERTS, dm, D_EXPERT)
    w_up_all = exp_w_up.reshape(depth * N_EXPERTS, dm, D_EXPERT)
    w_down_all = exp_w_down.reshape(depth * N_EXPERTS, D_EXPERT, dm)

    perm = jnp.asarray(_regroup_matrix(), BF16)
    h = x.reshape(t, dm)
    for i in range(depth):
        w_in_b = (g_mix[i][:, None] * w_in[i]).astype(BF16)
        zp, ut = _in_proj(h.reshape(nch, CHUNK, dm), w_in_b[:, :D_POOL],
                          jnp.transpose(w_in_b[:, D_POOL:]), perm)
        a = _pool(zp.reshape(bsz, seq, D_POOL), pool_w[i], pool_scale[i][None])
        yt = _ssm(ut, ssm_a_re[i], ssm_a_im[i], ssm_log_dt[i], ssm_b_re[i], ssm_b_im[i],
                  ssm_c_re[i], ssm_c_im[i], ssm_d[i], seq_chunks)
        h = _mix_out(h.reshape(nch, CHUNK, dm), a.reshape(nch, CHUNK, D_POOL), yt,
                     jnp.transpose(glu_w[i]).astype(BF16), glu_b[i][:, None],
                     w_out[i].astype(BF16), perm).reshape(t, dm)

        w_router = jnp.concatenate(
            [router_grp_w[i],
             jnp.transpose(router_exp_w[i], (1, 0, 2)).reshape(dm, N_EXPERTS),
             jnp.zeros((dm, ROUTER_W - N_EXPERT_GROUPS - N_EXPERTS), F32)], axis=1)
        b_router = jnp.concatenate(
            [router_grp_b[i], router_exp_b[i].reshape(N_EXPERTS),
             jnp.zeros((ROUTER_W - N_EXPERT_GROUPS - N_EXPERTS,), F32)])[None]
        vp, meta_t, counts = _router(h, g_ffn[i][None], w_router, b_router, tm)
        idx, tile_expert, n_valid, block = _plan(meta_t, counts, n_sorted // ROW_TILE)
        xs = _sc_scatter_rows(vp.reshape(2 * t, SC_ROW), idx, 2 * n_sorted)
        w_gate_b, w_up_b, w_down_b = _cast_expert_weights(
            tile_expert, w_gate_all, w_up_all, w_down_all, i)
        ys = _experts(xs.reshape(2, n_sorted, SC_ROW), tile_expert, n_valid, block,
                      w_gate_b, w_up_b, w_down_b)
        ys2 = ys.reshape(2 * n_sorted, SC_ROW)
        idx4 = idx.reshape(4, t)
        tp = t // GATHER_PARTS
        ple_wg = (g_ple[i][:, None] * ple_gate_w[i]).astype(BF16)
        ple_wp = ple_proj_w[i].astype(BF16)
        out = None
        for q in range(GATHER_PARTS):
            yg_q = _sc_gather_rows(ys2, idx4[:, q * tp:(q + 1) * tp].reshape(1, 4 * tp))
            out = _ple(h, yg_q.reshape(4, tp, SC_ROW), meta_t, p[i].reshape(t, D_PLE), ple_wg,
                       ple_gate_b[i][None], ple_wp, g_final[None], i == depth - 1, tm, q, out)
        h = out
    return h.reshape(bsz, seq, dm)
```

```python
import functools
import math

import numpy as np
import jax
import jax.numpy as jnp
from jax import lax
from jax.experimental import pallas as pl
from jax.experimental.pallas import tpu as pltpu
from jax.experimental.pallas import tpu_sc as plsc

F32 = jnp.float32
BF16 = jnp.bfloat16
U32 = jnp.uint32

D_MODEL = 1024
D_POOL = 512
D_SSM = 512
POOL_WINDOWS = (2, 4, 8, 16)
POOL_GROUP = 128
SSM_GROUP = 16
N_SSM_GROUPS = 32
SSM_STATE = 64
N_EXPERT_GROUPS = 4
EXPERTS_PER_GROUP = 8
N_EXPERTS = N_EXPERT_GROUPS * EXPERTS_PER_GROUP
D_EXPERT = 256
D_PLE = 256
RMS_EPS = 1e-6

LANES = 128
CHUNK = 32
CHUNK_W = CHUNK * SSM_GROUP
T_SUB = 8
S_TILE = 8
C_TILE = 128
ROUTER_W = LANES
HALF = D_MODEL // 2
ROW_TILE = 512
SC_WINDOW = 128
SC_ROW = HALF // 2
META_ROWS = 8
GATHER_PARTS = 4
VMEM_LIMIT = 56 * 1024 * 1024


def _dot(a, b):
    return jnp.dot(a, b, preferred_element_type=F32)


def _dot_nt(a, b):
    return lax.dot_general(a, b, (((1,), (1,)), ((), ())), preferred_element_type=F32)


def _dot_tn(a, b):
    return lax.dot_general(a, b, (((0,), (0,)), ((), ())), preferred_element_type=F32)


def _rms(x, g=None):
    y = x * lax.rsqrt(jnp.mean(x * x, axis=-1, keepdims=True) + RMS_EPS)
    return y if g is None else y * g


def _sigmoid(x):
    return 1.0 / (1.0 + jnp.exp(-x))


def _params(sem):
    return pltpu.CompilerParams(dimension_semantics=sem, vmem_limit_bytes=VMEM_LIMIT)


def _regroup_matrix():
    n = C_TILE * S_TILE
    src = np.arange(n)
    dst = (src % S_TILE) * C_TILE + src // S_TILE
    p = np.zeros((n, n), np.float32)
    p[src, dst] = 1.0
    return p


def _in_proj_kernel(x_ref, wp_ref, wst_ref, perm_ref, zp_ref, ut_ref):
    nc = x_ref.shape[0]
    u = _rms(x_ref[...].reshape(nc * S_TILE, D_MODEL)).astype(BF16)
    zp_ref[...] = _dot(u, wp_ref[...]).reshape(zp_ref.shape)
    zt = _dot_nt(wst_ref[...], u).astype(BF16)
    zt = _dot(zt, perm_ref[...]).astype(BF16)
    for j in range(S_TILE):
        ut_ref[:, j, :, :] = zt[:, j * nc:(j + 1) * nc].reshape(N_SSM_GROUPS, SSM_GROUP, nc)


def _in_proj(x3, w_pool, w_ssm_t, perm):
    nch = x3.shape[0]
    return pl.pallas_call(
        _in_proj_kernel,
        grid=(nch // C_TILE, CHUNK // S_TILE),
        in_specs=[
            pl.BlockSpec((C_TILE, S_TILE, D_MODEL), lambda c, s: (c, s, 0)),
            pl.BlockSpec((D_MODEL, D_POOL), lambda c, s: (0, 0)),
            pl.BlockSpec((D_SSM, D_MODEL), lambda c, s: (0, 0)),
            pl.BlockSpec((C_TILE * S_TILE, C_TILE * S_TILE), lambda c, s: (0, 0)),
        ],
        out_specs=[
            pl.BlockSpec((C_TILE, S_TILE, D_POOL), lambda c, s: (c, s, 0)),
            pl.BlockSpec((N_SSM_GROUPS, None, S_TILE, SSM_GROUP, C_TILE), lambda c, s: (0, c, s, 0, 0)),
        ],
        out_shape=[
            jax.ShapeDtypeStruct((nch, CHUNK, D_POOL), F32),
            jax.ShapeDtypeStruct((N_SSM_GROUPS, nch // C_TILE, CHUNK, SSM_GROUP, C_TILE), BF16),
        ],
        compiler_params=_params(("parallel", "parallel")),
        name="in_proj",
    )(x3, w_pool, w_ssm_t, perm)


def _shift_rows(x, d, row, n):
    if d == 0:
        return x
    r = pltpu.roll(x, d % n, 0)
    if d > 0:
        return jnp.where(row >= d, r, 0.0)
    return jnp.where(row < n + d, r, 0.0)


def _pool_kernel(z_ref, w_ref, sc_ref, o_ref):
    n = z_ref.shape[0]
    gi = pl.program_id(1)
    row = lax.broadcasted_iota(jnp.int32, (n, 1), 0)

    for k, w in enumerate(POOL_WINDOWS):
        @pl.when(gi == k)
        def _(w=w):
            x = z_ref[...]
            half = w // 2
            pd, pu, span = x, x, 1
            while span < half:
                pd = pd + _shift_rows(pd, span, row, n)
                pu = pu + _shift_rows(pu, -span, row, n)
                span *= 2
            total = _shift_rows(pd, 1, row, n) + pu
            lo = jnp.maximum(row - half, 0)
            hi = jnp.minimum(row + half, n)
            cnt = (hi - lo).astype(F32)
            diff = (total / cnt - x).astype(BF16)
            o_ref[...] = (_dot(diff, w_ref[...].astype(BF16)) * sc_ref[...]).astype(BF16)


def _pool(zp3, pool_w, pool_scale):
    b, s, _ = zp3.shape
    return pl.pallas_call(
        _pool_kernel,
        grid=(b, len(POOL_WINDOWS)),
        in_specs=[
            pl.BlockSpec((None, s, POOL_GROUP), lambda i, g: (i, 0, g)),
            pl.BlockSpec((None, POOL_GROUP, POOL_GROUP), lambda i, g: (g, 0, 0)),
            pl.BlockSpec((1, POOL_GROUP), lambda i, g: (0, g)),
        ],
        out_specs=pl.BlockSpec((None, s, POOL_GROUP), lambda i, g: (i, 0, g)),
        out_shape=jax.ShapeDtypeStruct((b, s, D_POOL), BF16),
        compiler_params=_params(("parallel", "parallel")),
        name="pool",
    )(zp3, pool_w, pool_scale)


def _expand_consts():
    time = np.arange(CHUNK_W) // SSM_GROUP
    def onehot(e):
        m = np.zeros((CHUNK_W, LANES), np.float32)
        m[np.arange(CHUNK_W), e] = 1.0
        return m
    return np.stack([
        onehot(CHUNK - 1 - time),
        onehot(time),
        onehot(time + 1),
        onehot(CHUNK - time),
    ])


def _cmul_packed(x, p, q):
    return x * p + pltpu.roll(x, LANES // 2, 1) * q


def _ssm_kernel(u_ref, are_ref, aim_ref, ldt_ref, bt_re_ref, bt_im_ref, ccr_ref, ct_re_ref,
                ct_im_ref, d_ref, exp_ref, y_ref, *, seq_chunks):
    n_ct = u_ref.shape[0]
    nch = n_ct * C_TILE
    half = LANES // 2
    lane = lax.broadcasted_iota(jnp.int32, (1, LANES), 1)
    lo_half = lane < half

    def direction(di):
        a_re = are_ref[di]
        a_im = aim_ref[di]
        dt = jnp.exp(ldt_ref[di])
        mag = jnp.exp(a_re * dt)
        ang = a_im * dt
        lam = jnp.where(lo_half, mag * jnp.cos(ang), mag * jnp.sin(ang))
        lb_re = mag * jnp.cos(ang)
        lb_im = mag * jnp.sin(ang)
        den = a_re * a_re + a_im * a_im
        f_re = ((lb_re - 1.0) * a_re + lb_im * a_im) / den
        f_im = (lb_im * a_re - (lb_re - 1.0) * a_im) / den
        return lam, f_re, f_im

    def power_table(lam):
        e = lax.broadcasted_iota(jnp.int32, (LANES, 1), 0)
        tab = jnp.where(lo_half, 1.0, 0.0) * jnp.ones((LANES, 1), F32)
        sq = lam
        for k in range(7):
            p = jnp.where(lo_half, sq, pltpu.roll(sq, half, 1))
            q = jnp.where(lo_half, -pltpu.roll(sq, half, 1), sq)
            tab = jnp.where(((e >> k) & 1) == 1, _cmul_packed(tab, p, q), tab)
            sq = _cmul_packed(sq, p, q)
        return tab

    def tile_rows(x16):
        return jnp.broadcast_to(x16[None], (CHUNK, SSM_GROUP, LANES)).reshape(CHUNK_W, LANES)

    def expanded(tab, which, v_re, v_im, conj_sign):
        lexp = _dot(exp_ref[which], tab.astype(BF16))
        if conj_sign > 0:
            p = jnp.where(lo_half, v_re, v_re)
            q = jnp.where(lo_half, -v_im, v_im)
        else:
            p = jnp.where(lo_half, v_re, -v_re)
            q = jnp.where(lo_half, -v_im, -v_im)
        return lexp * tile_rows(p) + pltpu.roll(lexp, half, 1) * tile_rows(q)

    lam_f, ff_re, ff_im = direction(0)
    lam_b, fb_re, fb_im = direction(1)
    tab_f = power_table(lam_f)
    tab_b = power_table(lam_b)

    def bbar(bt_re, bt_im, f_re, f_im):
        return bt_re * f_re - bt_im * f_im, bt_re * f_im + bt_im * f_re

    bf_re, bf_im = bbar(bt_re_ref[0], bt_im_ref[0], ff_re, ff_im)
    bb_re, bb_im = bbar(bt_re_ref[1], bt_im_ref[1], fb_re, fb_im)

    pb1 = expanded(tab_f, 0, bf_re, bf_im, 1)
    pb2 = expanded(tab_b, 1, bb_re, bb_im, 1)
    pb3 = expanded(tab_b, 2, bb_re, bb_im, 1)
    ft_f = expanded(tab_f, 2, ct_re_ref[0], ct_im_ref[0], -1)
    ft_b = expanded(tab_b, 3, ct_re_ref[1], ct_im_ref[1], -1)

    row_w = lax.broadcasted_iota(jnp.int32, (CHUNK_W, 1), 0)
    last_blk = row_w >= CHUNK_W - SSM_GROUP
    pb2_lag0 = jnp.where(last_blk, pltpu.roll(pb2, CHUNK_W - SSM_GROUP, 0), 0.0)
    ccr_f = ccr_ref[0].astype(BF16)
    ccr_b = ccr_ref[1].astype(BF16)
    r_lo = _dot_nt(ccr_f, pb1.astype(BF16)) + _dot_nt(ccr_b, pb2_lag0.astype(BF16))
    co = lax.broadcasted_iota(jnp.int32, (SSM_GROUP, CHUNK_W), 0)
    col = lax.broadcasted_iota(jnp.int32, (SSM_GROUP, CHUNK_W), 1)
    r_lo = r_lo + jnp.where(col == CHUNK_W - SSM_GROUP + co, d_ref[...], 0.0)
    r_hi = _dot_nt(ccr_b, pb3.astype(BF16))
    r_t = jnp.concatenate([r_lo, r_hi], axis=1)
    g_t = jnp.concatenate(
        [pltpu.roll(r_t, SSM_GROUP * (tl + 1), 1) for tl in range(T_SUB)], axis=0
    ).astype(BF16)

    u = jnp.concatenate([u_ref[ct].reshape(CHUNK_W, C_TILE) for ct in range(n_ct)],
                        axis=1)
    e_mat = jnp.concatenate([pb1, pb2], axis=1).astype(BF16)
    xend = _dot_tn(e_mat, u)
    lanec = lax.broadcasted_iota(jnp.int32, (1, nch), 1) % seq_chunks
    ns = SSM_STATE

    def scan(re, im, tab, forward):
        lam_col = jnp.transpose(tab[CHUNK:CHUNK + 8, :])[:, 0:1]
        a, b = lam_col[:ns], lam_col[ns:]
        n_steps = int(math.log2(seq_chunks))

        def shifted(v, d):
            if forward:
                return jnp.where(lanec >= d, pltpu.roll(v, d, 1), 0.0)
            return jnp.where(lanec < seq_chunks - d, pltpu.roll(v, nch - d, 1), 0.0)

        for k in range(n_steps):
            sr, si = shifted(re, 1 << k), shifted(im, 1 << k)
            re, im = re + (sr * a - si * b), im + (sr * b + si * a)
            a, b = a * a - b * b, 2.0 * a * b
        return shifted(re, 1), shifted(im, 1)

    f_re, f_im = scan(xend[:ns], xend[ns:2 * ns], tab_f, True)
    b_re, b_im = scan(xend[2 * ns:3 * ns], xend[3 * ns:], tab_b, False)
    xin = jnp.concatenate([f_re, f_im, b_re, b_im], axis=0).astype(BF16)
    f_t = jnp.concatenate([ft_f, ft_b], axis=1).astype(BF16)

    toeplitz = jnp.concatenate(
        [g_t[:, CHUNK_W - LANES * th:2 * CHUNK_W - LANES * th] for th in range(CHUNK // T_SUB)],
        axis=0)
    y_t = _dot(toeplitz, u) + _dot(f_t, xin)
    for ct in range(n_ct):
        y_ref[ct] = y_t[:, ct * C_TILE:(ct + 1) * C_TILE].reshape(
            CHUNK, SSM_GROUP, C_TILE).astype(y_ref.dtype)


def _ssm(ut, a_re, a_im, log_dt, b_re, b_im, c_re, c_im, d, seq_chunks):
    g, n_ct = ut.shape[:2]
    n = SSM_STATE

    def lane_vec(a):
        a = jnp.transpose(a, (1, 0, 2))
        return jnp.concatenate([a, a], axis=-1)[:, :, None, :]

    are = lane_vec(a_re)
    aim = lane_vec(a_im)
    ldt = lane_vec(jnp.broadcast_to(log_dt[..., None], (2, g, n)))

    def bt(b):
        b = jnp.transpose(b, (1, 0, 3, 2))
        return jnp.concatenate([b, b], axis=-1)

    def ct(c):
        c = jnp.transpose(c, (1, 0, 2, 3))
        return jnp.concatenate([c, c], axis=-1)

    ccr = jnp.concatenate([jnp.transpose(c_re, (1, 0, 2, 3)),
                           -jnp.transpose(c_im, (1, 0, 2, 3))], axis=-1)
    d_col = d.reshape(g, SSM_GROUP, 1)
    exp_c = jnp.asarray(_expand_consts(), BF16)

    vec_spec = pl.BlockSpec((None, 2, 1, LANES), lambda i: (i, 0, 0, 0))
    mat_spec = pl.BlockSpec((None, 2, SSM_GROUP, LANES), lambda i: (i, 0, 0, 0))
    return pl.pallas_call(
        functools.partial(_ssm_kernel, seq_chunks=seq_chunks),
        grid=(g,),
        in_specs=[
            pl.BlockSpec((None, n_ct, CHUNK, SSM_GROUP, C_TILE), lambda i: (i, 0, 0, 0, 0)),
            vec_spec, vec_spec, vec_spec,
            mat_spec, mat_spec, mat_spec, mat_spec, mat_spec,
            pl.BlockSpec((None, SSM_GROUP, 1), lambda i: (i, 0, 0)),
            pl.BlockSpec((4, CHUNK_W, LANES), lambda i: (0, 0, 0)),
        ],
        out_specs=pl.BlockSpec((None, n_ct, CHUNK, SSM_GROUP, C_TILE), lambda i: (i, 0, 0, 0, 0)),
        out_shape=jax.ShapeDtypeStruct(ut.shape, BF16),
        compiler_params=_params(("parallel",)),
        name="ssm",
    )(ut, are, aim, ldt, bt(b_re), bt(b_im), ccr, ct(c_re), ct(c_im), d_col, exp_c)


def _mix_out_kernel(x_ref, a_ref, yt_ref, gwt_ref, gb_ref, wo_ref, perm_ref, h_ref):
    nc = x_ref.shape[0]
    rows = nc * S_TILE
    y = jnp.concatenate([yt_ref[:, j, :, :].reshape(D_SSM, nc) for j in range(S_TILE)],
                        axis=1).astype(F32)
    z = 0.5 * y * (1.0 + jnp.tanh(math.sqrt(2.0 / math.pi) * (y + 0.044715 * (y * y * y))))
    gate = _sigmoid(_dot(gwt_ref[...], z.astype(BF16)) + gb_ref[...])
    s = (z * gate).astype(BF16)
    s = _dot_nt(s, perm_ref[...]).astype(BF16)
    h = (x_ref[...].reshape(rows, D_MODEL)
         + _dot(a_ref[...].reshape(rows, D_POOL), wo_ref[:D_POOL, :])
         + _dot_tn(s, wo_ref[D_POOL:, :]))
    h_ref[...] = h.reshape(h_ref.shape)


def _mix_out(x3, a3, yt, glu_w_t, glu_b_col, w_out, perm):
    nch = x3.shape[0]
    return pl.pallas_call(
        _mix_out_kernel,
        grid=(nch // C_TILE, CHUNK // S_TILE),
        in_specs=[
            pl.BlockSpec((C_TILE, S_TILE, D_MODEL), lambda c, t: (c, t, 0)),
            pl.BlockSpec((C_TILE, S_TILE, D_POOL), lambda c, t: (c, t, 0)),
            pl.BlockSpec((N_SSM_GROUPS, None, S_TILE, SSM_GROUP, C_TILE), lambda c, t: (0, c, t, 0, 0)),
            pl.BlockSpec((D_SSM, D_SSM), lambda c, t: (0, 0)),
            pl.BlockSpec((D_SSM, 1), lambda c, t: (0, 0)),
            pl.BlockSpec((D_MODEL, D_MODEL), lambda c, t: (0, 0)),
            pl.BlockSpec((C_TILE * S_TILE, C_TILE * S_TILE), lambda c, t: (0, 0)),
        ],
        out_specs=pl.BlockSpec((C_TILE, S_TILE, D_MODEL), lambda c, t: (c, t, 0)),
        out_shape=jax.ShapeDtypeStruct((nch, CHUNK, D_MODEL), F32),
        compiler_params=_params(("parallel", "parallel")),
        name="mix_out",
    )(x3, a3, yt, glu_w_t, glu_b_col, w_out, perm)


def _pack_rows(x):
    b = lax.bitcast_convert_type(x.astype(BF16).astype(F32), U32)
    return (b[:, :HALF] & jnp.uint32(0xFFFF0000)) | (b[:, HALF:] >> 16)


def _unpack_rows(w):
    lo = lax.bitcast_convert_type(w & jnp.uint32(0xFFFF0000), F32)
    hi = lax.bitcast_convert_type(w << 16, F32)
    return lo, hi


def _split_bf16(x):
    hi = x.astype(BF16)
    return hi, (x - hi.astype(F32)).astype(BF16)


def _route(v32, wr_ref, br_ref):
    v_hi, v_lo = _split_bf16(v32)
    w_hi, w_lo = _split_bf16(wr_ref[...])
    logits = _dot(v_hi, w_hi) + (_dot(v_lo, w_hi) + _dot(v_hi, w_lo)) + br_ref[...]
    return jnp.transpose(logits)


def _top1(x, valid=None):
    n = x.shape[0]
    row = lax.broadcasted_iota(jnp.int32, x.shape, 0).astype(F32)
    if valid is not None:
        x = jnp.where(valid, x, -jnp.inf)
    m = jnp.max(x, axis=0, keepdims=True)
    idx = jnp.min(jnp.where(x == m, row, float(n)), axis=0, keepdims=True)
    return m, idx, x, row


def _split_planes(packed, ref):
    ref[0] = packed[:, :SC_ROW]
    ref[1] = packed[:, SC_ROW:]


def _router_kernel(h_ref, g_ref, wr_ref, br_ref, before_ref, vp_ref, meta_t_ref, cnt_ref, carry_ref):
    @pl.when(pl.program_id(0) == 0)
    def _():
        carry_ref[...] = jnp.zeros_like(carry_ref)

    v32 = _rms(h_ref[...], g_ref[...])
    _split_planes(_pack_rows(v32), vp_ref)
    lt = _route(v32, wr_ref, br_ref)
    tm = lt.shape[1]
    eg = EXPERTS_PER_GROUP

    grp = lt[:eg]
    grp_row = lax.broadcasted_iota(jnp.int32, grp.shape, 0)
    mg, grp_idx, grp, _ = _top1(grp, grp_row < N_EXPERT_GROUPS)
    grp_p = 1.0 / jnp.sum(jnp.exp(grp - mg), axis=0, keepdims=True)
    le = jnp.zeros((eg, tm), F32)
    for g in range(N_EXPERT_GROUPS):
        le = jnp.where(grp_idx == float(g), lt[eg * (g + 1):eg * (g + 2)], le)
    m1, i1, le, row = _top1(le)
    z = jnp.sum(jnp.exp(le - m1), axis=0, keepdims=True)
    m2, i2, _, _ = _top1(jnp.where(row == i1, -jnp.inf, le))
    p1 = 1.0 / z
    p2 = jnp.exp(m2 - m1) / z
    tot = p1 + p2
    w1 = grp_p * (p1 / tot)
    w2 = grp_p * (p2 / tot)
    e1 = grp_idx * eg + i1
    e2 = grp_idx * eg + i2

    erow = lax.broadcasted_iota(jnp.int32, (N_EXPERTS, tm), 0).astype(F32)
    onehot = jnp.where(erow == e1, 1.0, jnp.where(erow == e2, 1.0, 0.0))
    before = _dot(onehot.astype(BF16), before_ref[...]) + carry_ref[...]
    rank1 = jnp.sum(jnp.where(erow == e1, before, 0.0), axis=0, keepdims=True)
    rank2 = jnp.sum(jnp.where(erow == e2, before, 0.0), axis=0, keepdims=True)
    carry = carry_ref[...] + jnp.sum(onehot, axis=1, keepdims=True)
    carry_ref[...] = carry
    cnt_ref[...] = carry

    mrow = lax.broadcasted_iota(jnp.int32, (META_ROWS, tm), 0)
    meta_t_ref[...] = jnp.where(mrow == 0, e1, jnp.where(mrow == 1, e2, jnp.where(
        mrow == 2, rank1, jnp.where(mrow == 3, rank2, jnp.where(
            mrow == 4, w1, jnp.where(mrow == 5, w2, 0.0))))))


def _earlier_matrix(tm):
    return np.triu(np.ones((tm, tm), np.float32), k=1)


def _router(h1, g_ffn, w_router, b_router, tm):
    t = h1.shape[0]
    return pl.pallas_call(
        _router_kernel,
        grid=(t // tm,),
        in_specs=[
            pl.BlockSpec((tm, D_MODEL), lambda i: (i, 0)),
            pl.BlockSpec((1, D_MODEL), lambda i: (0, 0)),
            pl.BlockSpec((D_MODEL, ROUTER_W), lambda i: (0, 0)),
            pl.BlockSpec((1, ROUTER_W), lambda i: (0, 0)),
            pl.BlockSpec((tm, tm), lambda i: (0, 0)),
        ],
        out_specs=[
            pl.BlockSpec((2, tm, SC_ROW), lambda i: (0, i, 0)),
            pl.BlockSpec((META_ROWS, tm), lambda i: (0, i)),
            pl.BlockSpec((N_EXPERTS, 1), lambda i: (0, 0)),
        ],
        out_shape=[
            jax.ShapeDtypeStruct((2, t, SC_ROW), U32),
            jax.ShapeDtypeStruct((META_ROWS, t), F32),
            jax.ShapeDtypeStruct((N_EXPERTS, 1), F32),
        ],
        scratch_shapes=[pltpu.VMEM((N_EXPERTS, 1), F32)],
        compiler_params=_params(("arbitrary",)),
        name="router",
    )(h1, g_ffn, w_router, b_router, jnp.asarray(_earlier_matrix(tm), BF16))


def _plan(meta_t, counts, n_tiles):
    e1 = meta_t[0].astype(jnp.int32)
    e2 = meta_t[1].astype(jnp.int32)
    rank1 = meta_t[2].astype(jnp.int32)
    rank2 = meta_t[3].astype(jnp.int32)
    cnt = counts[:, 0].astype(jnp.int32)
    padded = ((cnt + ROW_TILE - 1) // ROW_TILE) * ROW_TILE
    ends = jnp.cumsum(padded)
    starts = ends - padded
    experts = jnp.arange(N_EXPERTS, dtype=jnp.int32)
    pos1 = rank1 + jnp.sum(jnp.where(e1[None, :] == experts[:, None], starts[:, None], 0), axis=0)
    pos2 = rank2 + jnp.sum(jnp.where(e2[None, :] == experts[:, None], starts[:, None], 0), axis=0)
    tile_start = jnp.arange(n_tiles, dtype=jnp.int32) * ROW_TILE
    tile_expert = jnp.sum((tile_start[:, None] >= ends[None, :]).astype(jnp.int32), axis=1)
    tile_expert = jnp.minimum(tile_expert, N_EXPERTS - 1)
    rows_left = jnp.sum(jnp.where(tile_expert[:, None] == experts, cnt + starts, 0), axis=1) - tile_start
    n_valid = jnp.clip(rows_left, 0, ROW_TILE).astype(jnp.int32)
    changed = jnp.concatenate([jnp.ones((1,), jnp.int32),
                               (tile_expert[1:] != tile_expert[:-1]).astype(jnp.int32)])
    last_used = jnp.maximum(ends[-1] // ROW_TILE - 1, 0)
    block = jnp.minimum(jnp.arange(n_tiles, dtype=jnp.int32), last_used)
    tile_expert = jnp.sum(jnp.where(block[:, None] == jnp.arange(n_tiles)[None, :],
                                    tile_expert[None, :], 0), axis=1)
    plane = n_tiles * ROW_TILE
    half_rows = jnp.concatenate([pos1, pos1 + plane, pos2, pos2 + plane])[None]
    return half_rows, tile_expert, n_valid, changed, block


def _sc_mesh():
    return plsc.VectorSubcoreMesh(core_axis_name="c", subcore_axis_name="s")


def _sc_scatter_rows(rows, idx, n_out):
    t, width = rows.shape
    steps = t // SC_WINDOW

    @pl.kernel(out_type=jax.ShapeDtypeStruct((n_out, width), rows.dtype), mesh=_sc_mesh(),
               scratch_types=[], name="moe_scatter")
    def scatter(rows_hbm, idx_hbm, out_hbm):
        def body(rows_vmem, idx_vmem):
            pltpu.sync_copy(rows_vmem, out_hbm.at[idx_vmem.at[0]])

        pltpu.emit_pipeline(
            body,
            grid=(2, steps),
            in_specs=[pl.BlockSpec((SC_WINDOW, width), lambda k, j: (j, 0)),
                      pl.BlockSpec((1, SC_WINDOW), lambda k, j: (0, k * steps + j))],
            out_specs=[],
            core_axis_name=("c", "s"),
            dimension_semantics=(pltpu.PARALLEL, pltpu.PARALLEL),
        )(rows_hbm, idx_hbm)

    return scatter(rows, idx)


def _sc_gather_rows(table, idx):
    m = idx.shape[1]
    width = table.shape[1]
    steps = m // (2 * SC_WINDOW)

    @pl.kernel(out_type=jax.ShapeDtypeStruct((m, width), table.dtype), mesh=_sc_mesh(),
               scratch_types=[], name="moe_gather")
    def gather(table_hbm, idx_hbm, out_hbm):
        def body(idx_vmem, out_vmem):
            pltpu.sync_copy(table_hbm.at[idx_vmem.at[0]], out_vmem)

        pltpu.emit_pipeline(
            body,
            grid=(2, steps),
            in_specs=[pl.BlockSpec((1, SC_WINDOW), lambda k, j: (0, k * steps + j))],
            out_specs=[pl.BlockSpec((SC_WINDOW, width), lambda k, j: (k * steps + j, 0))],
            core_axis_name=("c", "s"),
            dimension_semantics=(pltpu.PARALLEL, pltpu.PARALLEL),
        )(idx_hbm, out_hbm)

    return gather(table, idx)


def _experts_kernel(te_ref, nv_ref, new_ref, blk_ref, xs_ref, wg_ref, wu_ref, wd_ref, ys_ref,
                    wg_s, wu_s, wd_s):
    r = pl.program_id(0)
    n_valid = nv_ref[r]

    @pl.when((new_ref[r] == 1) & (n_valid > 0))
    def _():
        wg_s[...] = wg_ref[...].astype(BF16)
        wu_s[...] = wu_ref[...].astype(BF16)
        wd_s[...] = wd_ref[...].astype(BF16)

    @pl.when(n_valid > 0)
    def _():
        parts = [p.astype(BF16) for p in _unpack_rows(xs_ref[0]) + _unpack_rows(xs_ref[1])]
        cols = (0, 2 * SC_ROW, SC_ROW, 3 * SC_ROW)
        hg = sum(_dot(p, wg_s[c:c + SC_ROW, :]) for p, c in zip(parts, cols))
        hu = sum(_dot(p, wu_s[c:c + SC_ROW, :]) for p, c in zip(parts, cols))
        row = lax.broadcasted_iota(jnp.int32, (ROW_TILE, 1), 0)
        hid = jnp.where(row < n_valid, hg * _sigmoid(hg) * hu, 0.0).astype(BF16)
        _split_planes(_pack_rows(_dot(hid, wd_s[...])), ys_ref)


def _experts(xs, tile_expert, n_valid, changed, block, w_gate, w_up, w_down, layer):
    n_tiles = xs.shape[1] // ROW_TILE
    base = layer * N_EXPERTS
    w_spec = pl.BlockSpec((None, D_MODEL, D_EXPERT), lambda r, te, nv, new, blk: (base + te[r], 0, 0))
    grid_spec = pltpu.PrefetchScalarGridSpec(
        num_scalar_prefetch=4,
        grid=(n_tiles,),
        in_specs=[
            pl.BlockSpec((2, ROW_TILE, SC_ROW), lambda r, te, nv, new, blk: (0, blk[r], 0)),
            w_spec, w_spec,
            pl.BlockSpec((None, D_EXPERT, D_MODEL), lambda r, te, nv, new, blk: (base + te[r], 0, 0)),
        ],
        out_specs=pl.BlockSpec((2, ROW_TILE, SC_ROW), lambda r, te, nv, new, blk: (0, blk[r], 0)),
        scratch_shapes=[
            pltpu.VMEM((D_MODEL, D_EXPERT), BF16),
            pltpu.VMEM((D_MODEL, D_EXPERT), BF16),
            pltpu.VMEM((D_EXPERT, D_MODEL), BF16),
        ],
    )
    return pl.pallas_call(
        _experts_kernel,
        grid_spec=grid_spec,
        out_shape=jax.ShapeDtypeStruct((2, n_tiles * ROW_TILE, SC_ROW), U32),
        compiler_params=_params(("arbitrary",)),
        name="experts",
    )(tile_expert, n_valid, changed, block, xs, w_gate, w_up, w_down)


def _ple_kernel(h_ref, yg_ref, meta_t_ref, p_ref, wg_ref, bg_ref, wp_ref, gf_ref, *rest, final_norm):
    o_ref = rest[-1]
    meta = jnp.transpose(meta_t_ref[...])
    w1 = meta[:, 4:5]
    w2 = meta[:, 5:6]
    q0, q2 = (w1 * u + w2 * v for u, v in zip(_unpack_rows(yg_ref[0]), _unpack_rows(yg_ref[2])))
    q1, q3 = (w1 * u + w2 * v for u, v in zip(_unpack_rows(yg_ref[1]), _unpack_rows(yg_ref[3])))
    moe = jnp.concatenate([q0, q1, q2, q3], axis=1)
    h = h_ref[...] + moe
    gate = _sigmoid(_dot(_rms(h).astype(BF16), wg_ref[...]) + bg_ref[...])
    h = h + gate * _dot(p_ref[...].astype(BF16), wp_ref[...])
    o_ref[...] = _rms(h, gf_ref[...]) if final_norm else h


def _ple(h1, yg, meta_t, p2, w_gate, b_gate, w_proj, g_final, final_norm, tm, part, prev_out):
    t = h1.shape[0]
    steps = t // tm // GATHER_PARTS
    off = part * steps
    in_specs = [
        pl.BlockSpec((tm, D_MODEL), lambda i: (i + off, 0)),
        pl.BlockSpec((4, tm, SC_ROW), lambda i: (0, i, 0)),
        pl.BlockSpec((META_ROWS, tm), lambda i: (0, i + off)),
        pl.BlockSpec((tm, D_PLE), lambda i: (i + off, 0)),
        pl.BlockSpec((D_MODEL, D_MODEL), lambda i: (0, 0)),
        pl.BlockSpec((1, D_MODEL), lambda i: (0, 0)),
        pl.BlockSpec((D_PLE, D_MODEL), lambda i: (0, 0)),
        pl.BlockSpec((1, D_MODEL), lambda i: (0, 0)),
    ]
    args = [h1, yg, meta_t, p2, w_gate, b_gate, w_proj, g_final]
    aliases = {}
    if prev_out is not None:
        in_specs.append(pl.BlockSpec(memory_space=pl.ANY))
        args.append(prev_out)
        aliases = {len(args) - 1: 0}
    return pl.pallas_call(
        functools.partial(_ple_kernel, final_norm=final_norm),
        grid=(steps,),
        in_specs=in_specs,
        out_specs=pl.BlockSpec((tm, D_MODEL), lambda i: (i + off, 0)),
        out_shape=jax.ShapeDtypeStruct((t, D_MODEL), F32),
        input_output_aliases=aliases,
        compiler_params=_params(("parallel",)),
        name="ple",
    )(*args)


def kernel(x, p, g_mix, w_in, pool_w, pool_scale, ssm_a_re, ssm_a_im, ssm_log_dt, ssm_b_re,
           ssm_b_im, ssm_c_re, ssm_c_im, ssm_d, glu_w, glu_b, w_out, g_ffn, router_grp_w,
           router_grp_b, router_exp_w, router_exp_b, exp_w_gate, exp_w_up, exp_w_down, g_ple,
           ple_gate_w, ple_gate_b, ple_proj_w, g_final):
    bsz, seq, dm = x.shape
    depth = g_mix.shape[0]
    t = bsz * seq
    seq_chunks = seq // CHUNK
    nch = t // CHUNK
    tm = 512
    n_sorted = 2 * t + N_EXPERTS * ROW_TILE
    w_gate_all = exp_w_gate.reshape(depth * N_EXPERTS, dm, D_EXPERT)
    w_up_all = exp_w_up.reshape(depth * N_EXPERTS, dm, D_EXPERT)
    w_down_all = exp_w_down.reshape(depth * N_EXPERTS, D_EXPERT, dm)

    perm = jnp.asarray(_regroup_matrix(), BF16)
    h = x.reshape(t, dm)
    for i in range(depth):
        w_in_b = (g_mix[i][:, None] * w_in[i]).astype(BF16)
        zp, ut = _in_proj(h.reshape(nch, CHUNK, dm), w_in_b[:, :D_POOL],
                          jnp.transpose(w_in_b[:, D_POOL:]), perm)
        a = _pool(zp.reshape(bsz, seq, D_POOL), pool_w[i], pool_scale[i][None])
        yt = _ssm(ut, ssm_a_re[i], ssm_a_im[i], ssm_log_dt[i], ssm_b_re[i], ssm_b_im[i],
                  ssm_c_re[i], ssm_c_im[i], ssm_d[i], seq_chunks)
        h = _mix_out(h.reshape(nch, CHUNK, dm), a.reshape(nch, CHUNK, D_POOL), yt,
                     jnp.transpose(glu_w[i]).astype(BF16), glu_b[i][:, None],
                     w_out[i].astype(BF16), perm).reshape(t, dm)

        eg = EXPERTS_PER_GROUP
        w_router = jnp.concatenate(
            [router_grp_w[i], jnp.zeros((dm, eg - N_EXPERT_GROUPS), F32),
             jnp.transpose(router_exp_w[i], (1, 0, 2)).reshape(dm, N_EXPERTS),
             jnp.zeros((dm, ROUTER_W - eg - N_EXPERTS), F32)], axis=1)
        b_router = jnp.concatenate(
            [router_grp_b[i], jnp.zeros((eg - N_EXPERT_GROUPS,), F32),
             router_exp_b[i].reshape(N_EXPERTS),
             jnp.zeros((ROUTER_W - eg - N_EXPERTS,), F32)])[None]
        vp, meta_t, counts = _router(h, g_ffn[i][None], w_router, b_router, tm)
        idx, tile_expert, n_valid, changed, block = _plan(meta_t, counts, n_sorted // ROW_TILE)
        xs = _sc_scatter_rows(vp.reshape(2 * t, SC_ROW), idx, 2 * n_sorted)
        ys = _experts(xs.reshape(2, n_sorted, SC_ROW), tile_expert, n_valid, changed, block,
                      w_gate_all, w_up_all, w_down_all, i)
        ys2 = ys.reshape(2 * n_sorted, SC_ROW)
        idx4 = idx.reshape(4, t)
        tp = t // GATHER_PARTS
        ple_wg = (g_ple[i][:, None] * ple_gate_w[i]).astype(BF16)
        ple_wp = ple_proj_w[i].astype(BF16)
        out = None
        for q in range(GATHER_PARTS):
            yg_q = _sc_gather_rows(ys2, idx4[:, q * tp:(q + 1) * tp].reshape(1, 4 * tp))
            out = _ple(h, yg_q.reshape(4, tp, SC_ROW), meta_t, p[i].reshape(t, D_PLE), ple_wg,
                       ple_gate_b[i][None], ple_wp, g_final[None], i == depth - 1, tm, q, out)
        h = out
    return h.reshape(bsz, seq, dm)
```

```python
import functools
import math

import numpy as np
import jax
import jax.numpy as jnp
from jax import lax
from jax.experimental import pallas as pl
from jax.experimental.pallas import tpu as pltpu
from jax.experimental.pallas import tpu_sc as plsc

F32 = jnp.float32
BF16 = jnp.bfloat16
U32 = jnp.uint32

D_MODEL = 1024
D_POOL = 512
D_SSM = 512
POOL_WINDOWS = (2, 4, 8, 16)
POOL_GROUP = 128
SSM_GROUP = 16
N_SSM_GROUPS = 32
SSM_STATE = 64
N_EXPERT_GROUPS = 4
EXPERTS_PER_GROUP = 8
N_EXPERTS = N_EXPERT_GROUPS * EXPERTS_PER_GROUP
D_EXPERT = 256
D_PLE = 256
RMS_EPS = 1e-6

LANES = 128
CHUNK = 32
CHUNK_W = CHUNK * SSM_GROUP
T_SUB = 8
S_TILE = 8
C_TILE = 128
ROUTER_W = LANES
HALF = D_MODEL // 2
ROW_TILE = 1024
SC_WINDOW = 128
SC_ROW = HALF // 2
META_ROWS = 8
GATHER_PARTS = 4
VMEM_LIMIT = 56 * 1024 * 1024


def _dot(a, b):
    return jnp.dot(a, b, preferred_element_type=F32)


def _dot_nt(a, b):
    return lax.dot_general(a, b, (((1,), (1,)), ((), ())), preferred_element_type=F32)


def _dot_tn(a, b):
    return lax.dot_general(a, b, (((0,), (0,)), ((), ())), preferred_element_type=F32)


def _rms(x, g=None):
    y = x * lax.rsqrt(jnp.mean(x * x, axis=-1, keepdims=True) + RMS_EPS)
    return y if g is None else y * g


def _sigmoid(x):
    return 1.0 / (1.0 + jnp.exp(-x))


def _params(sem):
    return pltpu.CompilerParams(dimension_semantics=sem, vmem_limit_bytes=VMEM_LIMIT)


def _regroup_matrix():
    n = C_TILE * S_TILE
    src = np.arange(n)
    dst = (src % S_TILE) * C_TILE + src // S_TILE
    p = np.zeros((n, n), np.float32)
    p[src, dst] = 1.0
    return p


def _in_proj_kernel(x_ref, wp_ref, wst_ref, perm_ref, zp_ref, ut_ref):
    nc = x_ref.shape[0]
    u = _rms(x_ref[...].reshape(nc * S_TILE, D_MODEL)).astype(BF16)
    zp_ref[...] = _dot(u, wp_ref[...]).reshape(zp_ref.shape)
    zt = _dot_nt(wst_ref[...], u).astype(BF16)
    zt = _dot(zt, perm_ref[...]).astype(BF16)
    for j in range(S_TILE):
        ut_ref[:, j, :, :] = zt[:, j * nc:(j + 1) * nc].reshape(N_SSM_GROUPS, SSM_GROUP, nc)


def _in_proj(x3, w_pool, w_ssm_t, perm):
    nch = x3.shape[0]
    return pl.pallas_call(
        _in_proj_kernel,
        grid=(nch // C_TILE, CHUNK // S_TILE),
        in_specs=[
            pl.BlockSpec((C_TILE, S_TILE, D_MODEL), lambda c, s: (c, s, 0)),
            pl.BlockSpec((D_MODEL, D_POOL), lambda c, s: (0, 0)),
            pl.BlockSpec((D_SSM, D_MODEL), lambda c, s: (0, 0)),
            pl.BlockSpec((C_TILE * S_TILE, C_TILE * S_TILE), lambda c, s: (0, 0)),
        ],
        out_specs=[
            pl.BlockSpec((C_TILE, S_TILE, D_POOL), lambda c, s: (c, s, 0)),
            pl.BlockSpec((N_SSM_GROUPS, None, S_TILE, SSM_GROUP, C_TILE), lambda c, s: (0, c, s, 0, 0)),
        ],
        out_shape=[
            jax.ShapeDtypeStruct((nch, CHUNK, D_POOL), F32),
            jax.ShapeDtypeStruct((N_SSM_GROUPS, nch // C_TILE, CHUNK, SSM_GROUP, C_TILE), BF16),
        ],
        compiler_params=_params(("parallel", "parallel")),
        name="in_proj",
    )(x3, w_pool, w_ssm_t, perm)


def _shift_rows(x, d, row, n):
    if d == 0:
        return x
    r = pltpu.roll(x, d % n, 0)
    if d > 0:
        return jnp.where(row >= d, r, 0.0)
    return jnp.where(row < n + d, r, 0.0)


def _pool_kernel(z_ref, w_ref, sc_ref, o_ref):
    n = z_ref.shape[0]
    gi = pl.program_id(1)
    row = lax.broadcasted_iota(jnp.int32, (n, 1), 0)

    for k, w in enumerate(POOL_WINDOWS):
        @pl.when(gi == k)
        def _(w=w):
            x = z_ref[...]
            half = w // 2
            pd, pu, span = x, x, 1
            while span < half:
                pd = pd + _shift_rows(pd, span, row, n)
                pu = pu + _shift_rows(pu, -span, row, n)
                span *= 2
            total = _shift_rows(pd, 1, row, n) + pu
            lo = jnp.maximum(row - half, 0)
            hi = jnp.minimum(row + half, n)
            cnt = (hi - lo).astype(F32)
            diff = (total / cnt - x).astype(BF16)
            o_ref[...] = (_dot(diff, w_ref[...].astype(BF16)) * sc_ref[...]).astype(BF16)


def _pool(zp3, pool_w, pool_scale):
    b, s, _ = zp3.shape
    return pl.pallas_call(
        _pool_kernel,
        grid=(b, len(POOL_WINDOWS)),
        in_specs=[
            pl.BlockSpec((None, s, POOL_GROUP), lambda i, g: (i, 0, g)),
            pl.BlockSpec((None, POOL_GROUP, POOL_GROUP), lambda i, g: (g, 0, 0)),
            pl.BlockSpec((1, POOL_GROUP), lambda i, g: (0, g)),
        ],
        out_specs=pl.BlockSpec((None, s, POOL_GROUP), lambda i, g: (i, 0, g)),
        out_shape=jax.ShapeDtypeStruct((b, s, D_POOL), BF16),
        compiler_params=_params(("parallel", "parallel")),
        name="pool",
    )(zp3, pool_w, pool_scale)


def _expand_consts():
    time = np.arange(CHUNK_W) // SSM_GROUP
    def onehot(e):
        m = np.zeros((CHUNK_W, LANES), np.float32)
        m[np.arange(CHUNK_W), e] = 1.0
        return m
    return np.stack([
        onehot(CHUNK - 1 - time),
        onehot(time),
        onehot(time + 1),
        onehot(CHUNK - time),
    ])


def _cmul_packed(x, p, q):
    return x * p + pltpu.roll(x, LANES // 2, 1) * q


def _ssm_kernel(u_ref, are_ref, aim_ref, ldt_ref, bt_re_ref, bt_im_ref, ccr_ref, ct_re_ref,
                ct_im_ref, d_ref, exp_ref, y_ref, *, seq_chunks):
    n_ct = u_ref.shape[0]
    nch = n_ct * C_TILE
    half = LANES // 2
    lane = lax.broadcasted_iota(jnp.int32, (1, LANES), 1)
    lo_half = lane < half

    def direction(di):
        a_re = are_ref[di]
        a_im = aim_ref[di]
        dt = jnp.exp(ldt_ref[di])
        mag = jnp.exp(a_re * dt)
        ang = a_im * dt
        lam = jnp.where(lo_half, mag * jnp.cos(ang), mag * jnp.sin(ang))
        lb_re = mag * jnp.cos(ang)
        lb_im = mag * jnp.sin(ang)
        den = a_re * a_re + a_im * a_im
        f_re = ((lb_re - 1.0) * a_re + lb_im * a_im) / den
        f_im = (lb_im * a_re - (lb_re - 1.0) * a_im) / den
        return lam, f_re, f_im

    def power_table(lam):
        e = lax.broadcasted_iota(jnp.int32, (LANES, 1), 0)
        tab = jnp.where(lo_half, 1.0, 0.0) * jnp.ones((LANES, 1), F32)
        sq = lam
        for k in range(7):
            p = jnp.where(lo_half, sq, pltpu.roll(sq, half, 1))
            q = jnp.where(lo_half, -pltpu.roll(sq, half, 1), sq)
            tab = jnp.where(((e >> k) & 1) == 1, _cmul_packed(tab, p, q), tab)
            sq = _cmul_packed(sq, p, q)
        return tab

    def tile_rows(x16):
        return jnp.broadcast_to(x16[None], (CHUNK, SSM_GROUP, LANES)).reshape(CHUNK_W, LANES)

    def expanded(tab, which, v_re, v_im, conj_sign):
        lexp = _dot(exp_ref[which], tab.astype(BF16))
        if conj_sign > 0:
            p = jnp.where(lo_half, v_re, v_re)
            q = jnp.where(lo_half, -v_im, v_im)
        else:
            p = jnp.where(lo_half, v_re, -v_re)
            q = jnp.where(lo_half, -v_im, -v_im)
        return lexp * tile_rows(p) + pltpu.roll(lexp, half, 1) * tile_rows(q)

    lam_f, ff_re, ff_im = direction(0)
    lam_b, fb_re, fb_im = direction(1)
    tab_f = power_table(lam_f)
    tab_b = power_table(lam_b)

    def bbar(bt_re, bt_im, f_re, f_im):
        return bt_re * f_re - bt_im * f_im, bt_re * f_im + bt_im * f_re

    bf_re, bf_im = bbar(bt_re_ref[0], bt_im_ref[0], ff_re, ff_im)
    bb_re, bb_im = bbar(bt_re_ref[1], bt_im_ref[1], fb_re, fb_im)

    pb1 = expanded(tab_f, 0, bf_re, bf_im, 1)
    pb2 = expanded(tab_b, 1, bb_re, bb_im, 1)
    pb3 = expanded(tab_b, 2, bb_re, bb_im, 1)
    ft_f = expanded(tab_f, 2, ct_re_ref[0], ct_im_ref[0], -1)
    ft_b = expanded(tab_b, 3, ct_re_ref[1], ct_im_ref[1], -1)

    row_w = lax.broadcasted_iota(jnp.int32, (CHUNK_W, 1), 0)
    last_blk = row_w >= CHUNK_W - SSM_GROUP
    pb2_lag0 = jnp.where(last_blk, pltpu.roll(pb2, CHUNK_W - SSM_GROUP, 0), 0.0)
    ccr_f = ccr_ref[0].astype(BF16)
    ccr_b = ccr_ref[1].astype(BF16)
    r_lo = _dot_nt(ccr_f, pb1.astype(BF16)) + _dot_nt(ccr_b, pb2_lag0.astype(BF16))
    co = lax.broadcasted_iota(jnp.int32, (SSM_GROUP, CHUNK_W), 0)
    col = lax.broadcasted_iota(jnp.int32, (SSM_GROUP, CHUNK_W), 1)
    r_lo = r_lo + jnp.where(col == CHUNK_W - SSM_GROUP + co, d_ref[...], 0.0)
    r_hi = _dot_nt(ccr_b, pb3.astype(BF16))
    r_t = jnp.concatenate([r_lo, r_hi], axis=1)
    g_t = jnp.concatenate(
        [pltpu.roll(r_t, SSM_GROUP * (tl + 1), 1) for tl in range(T_SUB)], axis=0
    ).astype(BF16)

    u = jnp.concatenate([u_ref[ct].reshape(CHUNK_W, C_TILE) for ct in range(n_ct)],
                        axis=1)
    e_mat = jnp.concatenate([pb1, pb2], axis=1).astype(BF16)
    xend = _dot_tn(e_mat, u)
    lanec = lax.broadcasted_iota(jnp.int32, (1, nch), 1) % seq_chunks
    ns = SSM_STATE

    def scan(re, im, tab, forward):
        lam_col = jnp.transpose(tab[CHUNK:CHUNK + 8, :])[:, 0:1]
        a, b = lam_col[:ns], lam_col[ns:]
        n_steps = int(math.log2(seq_chunks))

        def shifted(v, d):
            if forward:
                return jnp.where(lanec >= d, pltpu.roll(v, d, 1), 0.0)
            return jnp.where(lanec < seq_chunks - d, pltpu.roll(v, nch - d, 1), 0.0)

        for k in range(n_steps):
            sr, si = shifted(re, 1 << k), shifted(im, 1 << k)
            re, im = re + (sr * a - si * b), im + (sr * b + si * a)
            a, b = a * a - b * b, 2.0 * a * b
        return shifted(re, 1), shifted(im, 1)

    f_re, f_im = scan(xend[:ns], xend[ns:2 * ns], tab_f, True)
    b_re, b_im = scan(xend[2 * ns:3 * ns], xend[3 * ns:], tab_b, False)
    xin = jnp.concatenate([f_re, f_im, b_re, b_im], axis=0).astype(BF16)
    f_t = jnp.concatenate([ft_f, ft_b], axis=1).astype(BF16)

    toeplitz = jnp.concatenate(
        [g_t[:, CHUNK_W - LANES * th:2 * CHUNK_W - LANES * th] for th in range(CHUNK // T_SUB)],
        axis=0)
    y_t = _dot(toeplitz, u) + _dot(f_t, xin)
    for ct in range(n_ct):
        y_ref[ct] = y_t[:, ct * C_TILE:(ct + 1) * C_TILE].reshape(
            CHUNK, SSM_GROUP, C_TILE).astype(y_ref.dtype)


def _ssm(ut, a_re, a_im, log_dt, b_re, b_im, c_re, c_im, d, seq_chunks):
    g, n_ct = ut.shape[:2]
    n = SSM_STATE

    def lane_vec(a):
        a = jnp.transpose(a, (1, 0, 2))
        return jnp.concatenate([a, a], axis=-1)[:, :, None, :]

    are = lane_vec(a_re)
    aim = lane_vec(a_im)
    ldt = lane_vec(jnp.broadcast_to(log_dt[..., None], (2, g, n)))

    def bt(b):
        b = jnp.transpose(b, (1, 0, 3, 2))
        return jnp.concatenate([b, b], axis=-1)

    def ct(c):
        c = jnp.transpose(c, (1, 0, 2, 3))
        return jnp.concatenate([c, c], axis=-1)

    ccr = jnp.concatenate([jnp.transpose(c_re, (1, 0, 2, 3)),
                           -jnp.transpose(c_im, (1, 0, 2, 3))], axis=-1)
    d_col = d.reshape(g, SSM_GROUP, 1)
    exp_c = jnp.asarray(_expand_consts(), BF16)

    vec_spec = pl.BlockSpec((None, 2, 1, LANES), lambda i: (i, 0, 0, 0))
    mat_spec = pl.BlockSpec((None, 2, SSM_GROUP, LANES), lambda i: (i, 0, 0, 0))
    return pl.pallas_call(
        functools.partial(_ssm_kernel, seq_chunks=seq_chunks),
        grid=(g,),
        in_specs=[
            pl.BlockSpec((None, n_ct, CHUNK, SSM_GROUP, C_TILE), lambda i: (i, 0, 0, 0, 0)),
            vec_spec, vec_spec, vec_spec,
            mat_spec, mat_spec, mat_spec, mat_spec, mat_spec,
            pl.BlockSpec((None, SSM_GROUP, 1), lambda i: (i, 0, 0)),
            pl.BlockSpec((4, CHUNK_W, LANES), lambda i: (0, 0, 0)),
        ],
        out_specs=pl.BlockSpec((None, n_ct, CHUNK, SSM_GROUP, C_TILE), lambda i: (i, 0, 0, 0, 0)),
        out_shape=jax.ShapeDtypeStruct(ut.shape, BF16),
        compiler_params=_params(("parallel",)),
        name="ssm",
    )(ut, are, aim, ldt, bt(b_re), bt(b_im), ccr, ct(c_re), ct(c_im), d_col, exp_c)


def _mix_out_kernel(x_ref, a_ref, yt_ref, gwt_ref, gb_ref, wo_ref, perm_ref, h_ref):
    nc = x_ref.shape[0]
    rows = nc * S_TILE
    y = jnp.concatenate([yt_ref[:, j, :, :].reshape(D_SSM, nc) for j in range(S_TILE)],
                        axis=1).astype(F32)
    z = 0.5 * y * (1.0 + jnp.tanh(math.sqrt(2.0 / math.pi) * (y + 0.044715 * (y * y * y))))
    gate = _sigmoid(_dot(gwt_ref[...], z.astype(BF16)) + gb_ref[...])
    s = (z * gate).astype(BF16)
    s = _dot_nt(s, perm_ref[...]).astype(BF16)
    h = (x_ref[...].reshape(rows, D_MODEL)
         + _dot(a_ref[...].reshape(rows, D_POOL), wo_ref[:D_POOL, :])
         + _dot_tn(s, wo_ref[D_POOL:, :]))
    h_ref[...] = h.reshape(h_ref.shape)


def _mix_out(x3, a3, yt, glu_w_t, glu_b_col, w_out, perm):
    nch = x3.shape[0]
    return pl.pallas_call(
        _mix_out_kernel,
        grid=(nch // C_TILE, CHUNK // S_TILE),
        in_specs=[
            pl.BlockSpec((C_TILE, S_TILE, D_MODEL), lambda c, t: (c, t, 0)),
            pl.BlockSpec((C_TILE, S_TILE, D_POOL), lambda c, t: (c, t, 0)),
            pl.BlockSpec((N_SSM_GROUPS, None, S_TILE, SSM_GROUP, C_TILE), lambda c, t: (0, c, t, 0, 0)),
            pl.BlockSpec((D_SSM, D_SSM), lambda c, t: (0, 0)),
            pl.BlockSpec((D_SSM, 1), lambda c, t: (0, 0)),
            pl.BlockSpec((D_MODEL, D_MODEL), lambda c, t: (0, 0)),
            pl.BlockSpec((C_TILE * S_TILE, C_TILE * S_TILE), lambda c, t: (0, 0)),
        ],
        out_specs=pl.BlockSpec((C_TILE, S_TILE, D_MODEL), lambda c, t: (c, t, 0)),
        out_shape=jax.ShapeDtypeStruct((nch, CHUNK, D_MODEL), F32),
        compiler_params=_params(("parallel", "parallel")),
        name="mix_out",
    )(x3, a3, yt, glu_w_t, glu_b_col, w_out, perm)


def _pack_rows(x):
    b = lax.bitcast_convert_type(x.astype(BF16).astype(F32), U32)
    return (b[:, :HALF] & jnp.uint32(0xFFFF0000)) | (b[:, HALF:] >> 16)


def _unpack_rows(w):
    lo = lax.bitcast_convert_type(w & jnp.uint32(0xFFFF0000), F32)
    hi = lax.bitcast_convert_type(w << 16, F32)
    return lo, hi


def _split_bf16(x):
    hi = x.astype(BF16)
    return hi, (x - hi.astype(F32)).astype(BF16)


def _route(v32, wr_ref, br_ref):
    v_hi, v_lo = _split_bf16(v32)
    w_hi, w_lo = _split_bf16(wr_ref[...])
    logits = _dot(v_hi, w_hi) + (_dot(v_lo, w_hi) + _dot(v_hi, w_lo)) + br_ref[...]
    return jnp.transpose(logits)


def _top1(x, valid=None):
    n = x.shape[0]
    row = lax.broadcasted_iota(jnp.int32, x.shape, 0).astype(F32)
    if valid is not None:
        x = jnp.where(valid, x, -jnp.inf)
    m = jnp.max(x, axis=0, keepdims=True)
    idx = jnp.min(jnp.where(x == m, row, float(n)), axis=0, keepdims=True)
    return m, idx, x, row


def _split_planes(packed, ref):
    ref[0] = packed[:, :SC_ROW]
    ref[1] = packed[:, SC_ROW:]


def _router_kernel(h_ref, g_ref, wr_ref, br_ref, before_ref, vp_ref, meta_t_ref, cnt_ref, carry_ref):
    @pl.when(pl.program_id(0) == 0)
    def _():
        carry_ref[...] = jnp.zeros_like(carry_ref)

    v32 = _rms(h_ref[...], g_ref[...])
    _split_planes(_pack_rows(v32), vp_ref)
    lt = _route(v32, wr_ref, br_ref)
    tm = lt.shape[1]
    eg = EXPERTS_PER_GROUP

    grp = lt[:eg]
    grp_row = lax.broadcasted_iota(jnp.int32, grp.shape, 0)
    mg, grp_idx, grp, _ = _top1(grp, grp_row < N_EXPERT_GROUPS)
    grp_p = 1.0 / jnp.sum(jnp.exp(grp - mg), axis=0, keepdims=True)
    le = jnp.zeros((eg, tm), F32)
    for g in range(N_EXPERT_GROUPS):
        le = jnp.where(grp_idx == float(g), lt[eg * (g + 1):eg * (g + 2)], le)
    m1, i1, le, row = _top1(le)
    z = jnp.sum(jnp.exp(le - m1), axis=0, keepdims=True)
    m2, i2, _, _ = _top1(jnp.where(row == i1, -jnp.inf, le))
    p1 = 1.0 / z
    p2 = jnp.exp(m2 - m1) / z
    tot = p1 + p2
    w1 = grp_p * (p1 / tot)
    w2 = grp_p * (p2 / tot)
    e1 = grp_idx * eg + i1
    e2 = grp_idx * eg + i2

    erow = lax.broadcasted_iota(jnp.int32, (N_EXPERTS, tm), 0).astype(F32)
    onehot = jnp.where(erow == e1, 1.0, jnp.where(erow == e2, 1.0, 0.0))
    before = _dot(onehot.astype(BF16), before_ref[...]) + carry_ref[...]
    rank1 = jnp.sum(jnp.where(erow == e1, before, 0.0), axis=0, keepdims=True)
    rank2 = jnp.sum(jnp.where(erow == e2, before, 0.0), axis=0, keepdims=True)
    carry = carry_ref[...] + jnp.sum(onehot, axis=1, keepdims=True)
    carry_ref[...] = carry
    cnt_ref[...] = carry

    mrow = lax.broadcasted_iota(jnp.int32, (META_ROWS, tm), 0)
    meta_t_ref[...] = jnp.where(mrow == 0, e1, jnp.where(mrow == 1, e2, jnp.where(
        mrow == 2, rank1, jnp.where(mrow == 3, rank2, jnp.where(
            mrow == 4, w1, jnp.where(mrow == 5, w2, 0.0))))))


def _earlier_matrix(tm):
    return np.triu(np.ones((tm, tm), np.float32), k=1)


def _router(h1, g_ffn, w_router, b_router, tm):
    t = h1.shape[0]
    return pl.pallas_call(
        _router_kernel,
        grid=(t // tm,),
        in_specs=[
            pl.BlockSpec((tm, D_MODEL), lambda i: (i, 0)),
            pl.BlockSpec((1, D_MODEL), lambda i: (0, 0)),
            pl.BlockSpec((D_MODEL, ROUTER_W), lambda i: (0, 0)),
            pl.BlockSpec((1, ROUTER_W), lambda i: (0, 0)),
            pl.BlockSpec((tm, tm), lambda i: (0, 0)),
        ],
        out_specs=[
            pl.BlockSpec((2, tm, SC_ROW), lambda i: (0, i, 0)),
            pl.BlockSpec((META_ROWS, tm), lambda i: (0, i)),
            pl.BlockSpec((N_EXPERTS, 1), lambda i: (0, 0)),
        ],
        out_shape=[
            jax.ShapeDtypeStruct((2, t, SC_ROW), U32),
            jax.ShapeDtypeStruct((META_ROWS, t), F32),
            jax.ShapeDtypeStruct((N_EXPERTS, 1), F32),
        ],
        scratch_shapes=[pltpu.VMEM((N_EXPERTS, 1), F32)],
        compiler_params=_params(("arbitrary",)),
        name="router",
    )(h1, g_ffn, w_router, b_router, jnp.asarray(_earlier_matrix(tm), BF16))


def _plan(meta_t, counts, n_tiles):
    e1 = meta_t[0].astype(jnp.int32)
    e2 = meta_t[1].astype(jnp.int32)
    rank1 = meta_t[2].astype(jnp.int32)
    rank2 = meta_t[3].astype(jnp.int32)
    cnt = counts[:, 0].astype(jnp.int32)
    padded = ((cnt + ROW_TILE - 1) // ROW_TILE) * ROW_TILE
    ends = jnp.cumsum(padded)
    starts = ends - padded
    experts = jnp.arange(N_EXPERTS, dtype=jnp.int32)
    pos1 = rank1 + jnp.sum(jnp.where(e1[None, :] == experts[:, None], starts[:, None], 0), axis=0)
    pos2 = rank2 + jnp.sum(jnp.where(e2[None, :] == experts[:, None], starts[:, None], 0), axis=0)
    tile_start = jnp.arange(n_tiles, dtype=jnp.int32) * ROW_TILE
    tile_expert = jnp.sum((tile_start[:, None] >= ends[None, :]).astype(jnp.int32), axis=1)
    tile_expert = jnp.minimum(tile_expert, N_EXPERTS - 1)
    rows_left = jnp.sum(jnp.where(tile_expert[:, None] == experts, cnt + starts, 0), axis=1) - tile_start
    n_valid = jnp.clip(rows_left, 0, ROW_TILE).astype(jnp.int32)
    changed = jnp.concatenate([jnp.ones((1,), jnp.int32),
                               (tile_expert[1:] != tile_expert[:-1]).astype(jnp.int32)])
    last_used = jnp.maximum(ends[-1] // ROW_TILE - 1, 0)
    block = jnp.minimum(jnp.arange(n_tiles, dtype=jnp.int32), last_used)
    tile_expert = jnp.sum(jnp.where(block[:, None] == jnp.arange(n_tiles)[None, :],
                                    tile_expert[None, :], 0), axis=1)
    plane = n_tiles * ROW_TILE
    half_rows = jnp.concatenate([pos1, pos1 + plane, pos2, pos2 + plane])[None]
    return half_rows, tile_expert, n_valid, changed, block


def _sc_mesh():
    return plsc.VectorSubcoreMesh(core_axis_name="c", subcore_axis_name="s")


def _sc_scatter_rows(rows, idx, n_out):
    t, width = rows.shape
    steps = t // SC_WINDOW

    @pl.kernel(out_type=jax.ShapeDtypeStruct((n_out, width), rows.dtype), mesh=_sc_mesh(),
               scratch_types=[], name="moe_scatter")
    def scatter(rows_hbm, idx_hbm, out_hbm):
        def body(rows_vmem, idx_vmem):
            pltpu.sync_copy(rows_vmem, out_hbm.at[idx_vmem.at[0]])

        pltpu.emit_pipeline(
            body,
            grid=(2, steps),
            in_specs=[pl.BlockSpec((SC_WINDOW, width), lambda k, j: (j, 0)),
                      pl.BlockSpec((1, SC_WINDOW), lambda k, j: (0, k * steps + j))],
            out_specs=[],
            core_axis_name=("c", "s"),
            dimension_semantics=(pltpu.PARALLEL, pltpu.PARALLEL),
        )(rows_hbm, idx_hbm)

    return scatter(rows, idx)


def _sc_gather_rows(table, idx):
    m = idx.shape[1]
    width = table.shape[1]
    steps = m // (2 * SC_WINDOW)

    @pl.kernel(out_type=jax.ShapeDtypeStruct((m, width), table.dtype), mesh=_sc_mesh(),
               scratch_types=[], name="moe_gather")
    def gather(table_hbm, idx_hbm, out_hbm):
        def body(idx_vmem, out_vmem):
            pltpu.sync_copy(table_hbm.at[idx_vmem.at[0]], out_vmem)

        pltpu.emit_pipeline(
            body,
            grid=(2, steps),
            in_specs=[pl.BlockSpec((1, SC_WINDOW), lambda k, j: (0, k * steps + j))],
            out_specs=[pl.BlockSpec((SC_WINDOW, width), lambda k, j: (k * steps + j, 0))],
            core_axis_name=("c", "s"),
            dimension_semantics=(pltpu.PARALLEL, pltpu.PARALLEL),
        )(idx_hbm, out_hbm)

    return gather(table, idx)


def _experts_kernel(te_ref, nv_ref, new_ref, blk_ref, xs_ref, wg_ref, wu_ref, wd_ref, ys_ref,
                    wg_s, wu_s, wd_s):
    r = pl.program_id(0)
    n_valid = nv_ref[r]

    @pl.when((new_ref[r] == 1) & (n_valid > 0))
    def _():
        wg_s[...] = wg_ref[...].astype(BF16)
        wu_s[...] = wu_ref[...].astype(BF16)
        wd_s[...] = wd_ref[...].astype(BF16)

    @pl.when(n_valid > 0)
    def _():
        parts = [p.astype(BF16) for p in _unpack_rows(xs_ref[0]) + _unpack_rows(xs_ref[1])]
        cols = (0, 2 * SC_ROW, SC_ROW, 3 * SC_ROW)
        hg = sum(_dot(p, wg_s[c:c + SC_ROW, :]) for p, c in zip(parts, cols))
        hu = sum(_dot(p, wu_s[c:c + SC_ROW, :]) for p, c in zip(parts, cols))
        row = lax.broadcasted_iota(jnp.int32, (ROW_TILE, 1), 0)
        hid = jnp.where(row < n_valid, hg * _sigmoid(hg) * hu, 0.0).astype(BF16)
        _split_planes(_pack_rows(_dot(hid, wd_s[...])), ys_ref)


def _experts(xs, tile_expert, n_valid, changed, block, w_gate, w_up, w_down, layer):
    n_tiles = xs.shape[1] // ROW_TILE
    base = layer * N_EXPERTS
    w_spec = pl.BlockSpec((None, D_MODEL, D_EXPERT), lambda r, te, nv, new, blk: (base + te[r], 0, 0))
    grid_spec = pltpu.PrefetchScalarGridSpec(
        num_scalar_prefetch=4,
        grid=(n_tiles,),
        in_specs=[
            pl.BlockSpec((2, ROW_TILE, SC_ROW), lambda r, te, nv, new, blk: (0, blk[r], 0)),
            w_spec, w_spec,
            pl.BlockSpec((None, D_EXPERT, D_MODEL), lambda r, te, nv, new, blk: (base + te[r], 0, 0)),
        ],
        out_specs=pl.BlockSpec((2, ROW_TILE, SC_ROW), lambda r, te, nv, new, blk: (0, blk[r], 0)),
        scratch_shapes=[
            pltpu.VMEM((D_MODEL, D_EXPERT), BF16),
            pltpu.VMEM((D_MODEL, D_EXPERT), BF16),
            pltpu.VMEM((D_EXPERT, D_MODEL), BF16),
        ],
    )
    return pl.pallas_call(
        _experts_kernel,
        grid_spec=grid_spec,
        out_shape=jax.ShapeDtypeStruct((2, n_tiles * ROW_TILE, SC_ROW), U32),
        compiler_params=_params(("arbitrary",)),
        name="experts",
    )(tile_expert, n_valid, changed, block, xs, w_gate, w_up, w_down)


def _ple_kernel(h_ref, yg_ref, meta_t_ref, p_ref, wg_ref, bg_ref, wp_ref, gf_ref, *rest, final_norm):
    o_ref = rest[-1]
    meta = jnp.transpose(meta_t_ref[...])
    w1 = meta[:, 4:5]
    w2 = meta[:, 5:6]
    q0, q2 = (w1 * u + w2 * v for u, v in zip(_unpack_rows(yg_ref[0]), _unpack_rows(yg_ref[2])))
    q1, q3 = (w1 * u + w2 * v for u, v in zip(_unpack_rows(yg_ref[1]), _unpack_rows(yg_ref[3])))
    moe = jnp.concatenate([q0, q1, q2, q3], axis=1)
    h = h_ref[...] + moe
    gate = _sigmoid(_dot(_rms(h).astype(BF16), wg_ref[...]) + bg_ref[...])
    h = h + gate * _dot(p_ref[...].astype(BF16), wp_ref[...])
    o_ref[...] = _rms(h, gf_ref[...]) if final_norm else h


def _ple(h1, yg, meta_t, p2, w_gate, b_gate, w_proj, g_final, final_norm, tm, part, prev_out):
    t = h1.shape[0]
    steps = t // tm // GATHER_PARTS
    off = part * steps
    in_specs = [
        pl.BlockSpec((tm, D_MODEL), lambda i: (i + off, 0)),
        pl.BlockSpec((4, tm, SC_ROW), lambda i: (0, i, 0)),
        pl.BlockSpec((META_ROWS, tm), lambda i: (0, i + off)),
        pl.BlockSpec((tm, D_PLE), lambda i: (i + off, 0)),
        pl.BlockSpec((D_MODEL, D_MODEL), lambda i: (0, 0)),
        pl.BlockSpec((1, D_MODEL), lambda i: (0, 0)),
        pl.BlockSpec((D_PLE, D_MODEL), lambda i: (0, 0)),
        pl.BlockSpec((1, D_MODEL), lambda i: (0, 0)),
    ]
    args = [h1, yg, meta_t, p2, w_gate, b_gate, w_proj, g_final]
    aliases = {}
    if prev_out is not None:
        in_specs.append(pl.BlockSpec(memory_space=pl.ANY))
        args.append(prev_out)
        aliases = {len(args) - 1: 0}
    return pl.pallas_call(
        functools.partial(_ple_kernel, final_norm=final_norm),
        grid=(steps,),
        in_specs=in_specs,
        out_specs=pl.BlockSpec((tm, D_MODEL), lambda i: (i + off, 0)),
        out_shape=jax.ShapeDtypeStruct((t, D_MODEL), F32),
        input_output_aliases=aliases,
        compiler_params=_params(("parallel",)),
        name="ple",
    )(*args)


def kernel(x, p, g_mix, w_in, pool_w, pool_scale, ssm_a_re, ssm_a_im, ssm_log_dt, ssm_b_re,
           ssm_b_im, ssm_c_re, ssm_c_im, ssm_d, glu_w, glu_b, w_out, g_ffn, router_grp_w,
           router_grp_b, router_exp_w, router_exp_b, exp_w_gate, exp_w_up, exp_w_down, g_ple,
           ple_gate_w, ple_gate_b, ple_proj_w, g_final):
    bsz, seq, dm = x.shape
    depth = g_mix.shape[0]
    t = bsz * seq
    seq_chunks = seq // CHUNK
    nch = t // CHUNK
    tm = 512
    n_sorted = 2 * t + N_EXPERTS * ROW_TILE
    w_gate_all = exp_w_gate.reshape(depth * N_EXPERTS, dm, D_EXPERT)
    w_up_all = exp_w_up.reshape(depth * N_EXPERTS, dm, D_EXPERT)
    w_down_all = exp_w_down.reshape(depth * N_EXPERTS, D_EXPERT, dm)

    perm = jnp.asarray(_regroup_matrix(), BF16)
    h = x.reshape(t, dm)
    for i in range(depth):
        w_in_b = (g_mix[i][:, None] * w_in[i]).astype(BF16)
        zp, ut = _in_proj(h.reshape(nch, CHUNK, dm), w_in_b[:, :D_POOL],
                          jnp.transpose(w_in_b[:, D_POOL:]), perm)
        a = _pool(zp.reshape(bsz, seq, D_POOL), pool_w[i], pool_scale[i][None])
        yt = _ssm(ut, ssm_a_re[i], ssm_a_im[i], ssm_log_dt[i], ssm_b_re[i], ssm_b_im[i],
                  ssm_c_re[i], ssm_c_im[i], ssm_d[i], seq_chunks)
        h = _mix_out(h.reshape(nch, CHUNK, dm), a.reshape(nch, CHUNK, D_POOL), yt,
                     jnp.transpose(glu_w[i]).astype(BF16), glu_b[i][:, None],
                     w_out[i].astype(BF16), perm).reshape(t, dm)

        eg = EXPERTS_PER_GROUP
        w_router = jnp.concatenate(
            [router_grp_w[i], jnp.zeros((dm, eg - N_EXPERT_GROUPS), F32),
             jnp.transpose(router_exp_w[i], (1, 0, 2)).reshape(dm, N_EXPERTS),
             jnp.zeros((dm, ROUTER_W - eg - N_EXPERTS), F32)], axis=1)
        b_router = jnp.concatenate(
            [router_grp_b[i], jnp.zeros((eg - N_EXPERT_GROUPS,), F32),
             router_exp_b[i].reshape(N_EXPERTS),
             jnp.zeros((ROUTER_W - eg - N_EXPERTS,), F32)])[None]
        vp, meta_t, counts = _router(h, g_ffn[i][None], w_router, b_router, tm)
        idx, tile_expert, n_valid, changed, block = _plan(meta_t, counts, n_sorted // ROW_TILE)
        xs = _sc_scatter_rows(vp.reshape(2 * t, SC_ROW), idx, 2 * n_sorted)
        ys = _experts(xs.reshape(2, n_sorted, SC_ROW), tile_expert, n_valid, changed, block,
                      w_gate_all, w_up_all, w_down_all, i)
        ys2 = ys.reshape(2 * n_sorted, SC_ROW)
        idx4 = idx.reshape(4, t)
        tp = t // GATHER_PARTS
        ple_wg = (g_ple[i][:, None] * ple_gate_w[i]).astype(BF16)
        ple_wp = ple_proj_w[i].astype(BF16)
        out = None
        for q in range(GATHER_PARTS):
            yg_q = _sc_gather_rows(ys2, idx4[:, q * tp:(q + 1) * tp].reshape(1, 4 * tp))
            out = _ple(h, yg_q.reshape(4, tp, SC_ROW), meta_t, p[i].reshape(t, D_PLE), ple_wg,
                       ple_gate_b[i][None], ple_wp, g_final[None], i == depth - 1, tm, q, out)
        h = out
    return h.reshape(bsz, seq, dm)
```

```python
import functools
import math

import numpy as np
import jax
import jax.numpy as jnp
from jax import lax
from jax.experimental import pallas as pl
from jax.experimental.pallas import tpu as pltpu
from jax.experimental.pallas import tpu_sc as plsc

F32 = jnp.float32
BF16 = jnp.bfloat16
U32 = jnp.uint32

D_MODEL = 1024
D_POOL = 512
D_SSM = 512
POOL_WINDOWS = (2, 4, 8, 16)
POOL_GROUP = 128
SSM_GROUP = 16
N_SSM_GROUPS = 32
SSM_STATE = 64
N_EXPERT_GROUPS = 4
EXPERTS_PER_GROUP = 8
N_EXPERTS = N_EXPERT_GROUPS * EXPERTS_PER_GROUP
D_EXPERT = 256
D_PLE = 256
RMS_EPS = 1e-6

LANES = 128
CHUNK = 32
CHUNK_W = CHUNK * SSM_GROUP
T_SUB = 8
S_TILE = 8
C_TILE = 128
ROUTER_W = LANES
HALF = D_MODEL // 2
ROW_TILE = 1024
SC_WINDOW = 128
SC_ROW = HALF // 2
META_ROWS = 8
GATHER_PARTS = 4
VMEM_LIMIT = 56 * 1024 * 1024


def _dot(a, b):
    return jnp.dot(a, b, preferred_element_type=F32)


def _dot_nt(a, b):
    return lax.dot_general(a, b, (((1,), (1,)), ((), ())), preferred_element_type=F32)


def _dot_tn(a, b):
    return lax.dot_general(a, b, (((0,), (0,)), ((), ())), preferred_element_type=F32)


def _rms(x, g=None):
    y = x * lax.rsqrt(jnp.mean(x * x, axis=-1, keepdims=True) + RMS_EPS)
    return y if g is None else y * g


def _sigmoid(x):
    return 1.0 / (1.0 + jnp.exp(-x))


def _params(sem):
    return pltpu.CompilerParams(dimension_semantics=sem, vmem_limit_bytes=VMEM_LIMIT)


def _regroup_matrix():
    n = C_TILE * S_TILE
    src = np.arange(n)
    dst = (src % S_TILE) * C_TILE + src // S_TILE
    p = np.zeros((n, n), np.float32)
    p[src, dst] = 1.0
    return p


def _in_proj_kernel(x_ref, wp_ref, wst_ref, perm_ref, zp_ref, ut_ref):
    nc = x_ref.shape[0]
    u = _rms(x_ref[...].reshape(nc * S_TILE, D_MODEL)).astype(BF16)
    zp_ref[...] = _dot(u, wp_ref[...]).reshape(zp_ref.shape)
    zt = _dot_nt(wst_ref[...], u).astype(BF16)
    zt = _dot(zt, perm_ref[...]).astype(BF16)
    for j in range(S_TILE):
        ut_ref[:, j, :, :] = zt[:, j * nc:(j + 1) * nc].reshape(N_SSM_GROUPS, SSM_GROUP, nc)


def _in_proj(x3, w_pool, w_ssm_t, perm):
    nch = x3.shape[0]
    return pl.pallas_call(
        _in_proj_kernel,
        grid=(nch // C_TILE, CHUNK // S_TILE),
        in_specs=[
            pl.BlockSpec((C_TILE, S_TILE, D_MODEL), lambda c, s: (c, s, 0)),
            pl.BlockSpec((D_MODEL, D_POOL), lambda c, s: (0, 0)),
            pl.BlockSpec((D_SSM, D_MODEL), lambda c, s: (0, 0)),
            pl.BlockSpec((C_TILE * S_TILE, C_TILE * S_TILE), lambda c, s: (0, 0)),
        ],
        out_specs=[
            pl.BlockSpec((C_TILE, S_TILE, D_POOL), lambda c, s: (c, s, 0)),
            pl.BlockSpec((N_SSM_GROUPS, None, S_TILE, SSM_GROUP, C_TILE), lambda c, s: (0, c, s, 0, 0)),
        ],
        out_shape=[
            jax.ShapeDtypeStruct((nch, CHUNK, D_POOL), F32),
            jax.ShapeDtypeStruct((N_SSM_GROUPS, nch // C_TILE, CHUNK, SSM_GROUP, C_TILE), BF16),
        ],
        compiler_params=_params(("parallel", "parallel")),
        name="in_proj",
    )(x3, w_pool, w_ssm_t, perm)


def _shift_rows(x, d, row, n):
    if d == 0:
        return x
    r = pltpu.roll(x, d % n, 0)
    if d > 0:
        return jnp.where(row >= d, r, 0.0)
    return jnp.where(row < n + d, r, 0.0)


def _pool_kernel(z_ref, w_ref, sc_ref, o_ref):
    n = z_ref.shape[0]
    gi = pl.program_id(1)
    row = lax.broadcasted_iota(jnp.int32, (n, 1), 0)

    for k, w in enumerate(POOL_WINDOWS):
        @pl.when(gi == k)
        def _(w=w):
            x = z_ref[...]
            half = w // 2
            pd, pu, span = x, x, 1
            while span < half:
                pd = pd + _shift_rows(pd, span, row, n)
                pu = pu + _shift_rows(pu, -span, row, n)
                span *= 2
            total = _shift_rows(pd, 1, row, n) + pu
            lo = jnp.maximum(row - half, 0)
            hi = jnp.minimum(row + half, n)
            cnt = (hi - lo).astype(F32)
            diff = (total / cnt - x).astype(BF16)
            o_ref[...] = (_dot(diff, w_ref[...].astype(BF16)) * sc_ref[...]).astype(BF16)


def _pool(zp3, pool_w, pool_scale):
    b, s, _ = zp3.shape
    return pl.pallas_call(
        _pool_kernel,
        grid=(b, len(POOL_WINDOWS)),
        in_specs=[
            pl.BlockSpec((None, s, POOL_GROUP), lambda i, g: (i, 0, g)),
            pl.BlockSpec((None, POOL_GROUP, POOL_GROUP), lambda i, g: (g, 0, 0)),
            pl.BlockSpec((1, POOL_GROUP), lambda i, g: (0, g)),
        ],
        out_specs=pl.BlockSpec((None, s, POOL_GROUP), lambda i, g: (i, 0, g)),
        out_shape=jax.ShapeDtypeStruct((b, s, D_POOL), BF16),
        compiler_params=_params(("parallel", "parallel")),
        name="pool",
    )(zp3, pool_w, pool_scale)


def _expand_consts():
    time = np.arange(CHUNK_W) // SSM_GROUP
    def onehot(e):
        m = np.zeros((CHUNK_W, LANES), np.float32)
        m[np.arange(CHUNK_W), e] = 1.0
        return m
    return np.stack([
        onehot(CHUNK - 1 - time),
        onehot(time),
        onehot(time + 1),
        onehot(CHUNK - time),
    ])


def _cmul_packed(x, p, q):
    return x * p + pltpu.roll(x, LANES // 2, 1) * q


def _ssm_kernel(u_ref, are_ref, aim_ref, ldt_ref, bt_re_ref, bt_im_ref, ccr_ref, ct_re_ref,
                ct_im_ref, d_ref, exp_ref, wg_ref, wu_ref, wd_ref, y_ref, og_ref, ou_ref, od_ref, *,
                seq_chunks):
    og_ref[...] = wg_ref[...].astype(BF16)
    ou_ref[...] = wu_ref[...].astype(BF16)
    od_ref[...] = wd_ref[...].astype(BF16)

    n_ct = u_ref.shape[0]
    nch = n_ct * C_TILE
    half = LANES // 2
    lane = lax.broadcasted_iota(jnp.int32, (1, LANES), 1)
    lo_half = lane < half

    def direction(di):
        a_re = are_ref[di]
        a_im = aim_ref[di]
        dt = jnp.exp(ldt_ref[di])
        mag = jnp.exp(a_re * dt)
        ang = a_im * dt
        lam = jnp.where(lo_half, mag * jnp.cos(ang), mag * jnp.sin(ang))
        lb_re = mag * jnp.cos(ang)
        lb_im = mag * jnp.sin(ang)
        den = a_re * a_re + a_im * a_im
        f_re = ((lb_re - 1.0) * a_re + lb_im * a_im) / den
        f_im = (lb_im * a_re - (lb_re - 1.0) * a_im) / den
        return lam, f_re, f_im

    def power_table(lam):
        e = lax.broadcasted_iota(jnp.int32, (LANES, 1), 0)
        tab = jnp.where(lo_half, 1.0, 0.0) * jnp.ones((LANES, 1), F32)
        sq = lam
        for k in range(7):
            p = jnp.where(lo_half, sq, pltpu.roll(sq, half, 1))
            q = jnp.where(lo_half, -pltpu.roll(sq, half, 1), sq)
            tab = jnp.where(((e >> k) & 1) == 1, _cmul_packed(tab, p, q), tab)
            sq = _cmul_packed(sq, p, q)
        return tab

    def tile_rows(x16):
        return jnp.broadcast_to(x16[None], (CHUNK, SSM_GROUP, LANES)).reshape(CHUNK_W, LANES)

    def expanded(tab, which, v_re, v_im, conj_sign):
        lexp = _dot(exp_ref[which], tab.astype(BF16))
        if conj_sign > 0:
            p = jnp.where(lo_half, v_re, v_re)
            q = jnp.where(lo_half, -v_im, v_im)
        else:
            p = jnp.where(lo_half, v_re, -v_re)
            q = jnp.where(lo_half, -v_im, -v_im)
        return lexp * tile_rows(p) + pltpu.roll(lexp, half, 1) * tile_rows(q)

    lam_f, ff_re, ff_im = direction(0)
    lam_b, fb_re, fb_im = direction(1)
    tab_f = power_table(lam_f)
    tab_b = power_table(lam_b)

    def bbar(bt_re, bt_im, f_re, f_im):
        return bt_re * f_re - bt_im * f_im, bt_re * f_im + bt_im * f_re

    bf_re, bf_im = bbar(bt_re_ref[0], bt_im_ref[0], ff_re, ff_im)
    bb_re, bb_im = bbar(bt_re_ref[1], bt_im_ref[1], fb_re, fb_im)

    pb1 = expanded(tab_f, 0, bf_re, bf_im, 1)
    pb2 = expanded(tab_b, 1, bb_re, bb_im, 1)
    pb3 = expanded(tab_b, 2, bb_re, bb_im, 1)
    ft_f = expanded(tab_f, 2, ct_re_ref[0], ct_im_ref[0], -1)
    ft_b = expanded(tab_b, 3, ct_re_ref[1], ct_im_ref[1], -1)

    row_w = lax.broadcasted_iota(jnp.int32, (CHUNK_W, 1), 0)
    last_blk = row_w >= CHUNK_W - SSM_GROUP
    pb2_lag0 = jnp.where(last_blk, pltpu.roll(pb2, CHUNK_W - SSM_GROUP, 0), 0.0)
    ccr_f = ccr_ref[0].astype(BF16)
    ccr_b = ccr_ref[1].astype(BF16)
    r_lo = _dot_nt(ccr_f, pb1.astype(BF16)) + _dot_nt(ccr_b, pb2_lag0.astype(BF16))
    co = lax.broadcasted_iota(jnp.int32, (SSM_GROUP, CHUNK_W), 0)
    col = lax.broadcasted_iota(jnp.int32, (SSM_GROUP, CHUNK_W), 1)
    r_lo = r_lo + jnp.where(col == CHUNK_W - SSM_GROUP + co, d_ref[...], 0.0)
    r_hi = _dot_nt(ccr_b, pb3.astype(BF16))
    r_t = jnp.concatenate([r_lo, r_hi], axis=1)
    g_t = jnp.concatenate(
        [pltpu.roll(r_t, SSM_GROUP * (tl + 1), 1) for tl in range(T_SUB)], axis=0
    ).astype(BF16)

    u = jnp.concatenate([u_ref[ct].reshape(CHUNK_W, C_TILE) for ct in range(n_ct)],
                        axis=1)
    e_mat = jnp.concatenate([pb1, pb2], axis=1).astype(BF16)
    xend = _dot_tn(e_mat, u)
    lanec = lax.broadcasted_iota(jnp.int32, (1, nch), 1) % seq_chunks
    ns = SSM_STATE

    def scan(re, im, tab, forward):
        lam_col = jnp.transpose(tab[CHUNK:CHUNK + 8, :])[:, 0:1]
        a, b = lam_col[:ns], lam_col[ns:]
        n_steps = int(math.log2(seq_chunks))

        def shifted(v, d):
            if forward:
                return jnp.where(lanec >= d, pltpu.roll(v, d, 1), 0.0)
            return jnp.where(lanec < seq_chunks - d, pltpu.roll(v, nch - d, 1), 0.0)

        for k in range(n_steps):
            sr, si = shifted(re, 1 << k), shifted(im, 1 << k)
            re, im = re + (sr * a - si * b), im + (sr * b + si * a)
            a, b = a * a - b * b, 2.0 * a * b
        return shifted(re, 1), shifted(im, 1)

    f_re, f_im = scan(xend[:ns], xend[ns:2 * ns], tab_f, True)
    b_re, b_im = scan(xend[2 * ns:3 * ns], xend[3 * ns:], tab_b, False)
    xin = jnp.concatenate([f_re, f_im, b_re, b_im], axis=0).astype(BF16)
    f_t = jnp.concatenate([ft_f, ft_b], axis=1).astype(BF16)

    toeplitz = jnp.concatenate(
        [g_t[:, CHUNK_W - LANES * th:2 * CHUNK_W - LANES * th] for th in range(CHUNK // T_SUB)],
        axis=0)
    y_t = _dot(toeplitz, u) + _dot(f_t, xin)
    for ct in range(n_ct):
        y_ref[ct] = y_t[:, ct * C_TILE:(ct + 1) * C_TILE].reshape(
            CHUNK, SSM_GROUP, C_TILE).astype(y_ref.dtype)


def _ssm(ut, a_re, a_im, log_dt, b_re, b_im, c_re, c_im, d, seq_chunks, w_gate, w_up, w_down, layer):
    g, n_ct = ut.shape[:2]
    assert N_EXPERTS % g == 0
    epg = N_EXPERTS // g
    base = layer * g
    n = SSM_STATE

    def lane_vec(a):
        a = jnp.transpose(a, (1, 0, 2))
        return jnp.concatenate([a, a], axis=-1)[:, :, None, :]

    are = lane_vec(a_re)
    aim = lane_vec(a_im)
    ldt = lane_vec(jnp.broadcast_to(log_dt[..., None], (2, g, n)))

    def bt(b):
        b = jnp.transpose(b, (1, 0, 3, 2))
        return jnp.concatenate([b, b], axis=-1)

    def ct(c):
        c = jnp.transpose(c, (1, 0, 2, 3))
        return jnp.concatenate([c, c], axis=-1)

    ccr = jnp.concatenate([jnp.transpose(c_re, (1, 0, 2, 3)),
                           -jnp.transpose(c_im, (1, 0, 2, 3))], axis=-1)
    d_col = d.reshape(g, SSM_GROUP, 1)
    exp_c = jnp.asarray(_expand_consts(), BF16)

    vec_spec = pl.BlockSpec((None, 2, 1, LANES), lambda i: (i, 0, 0, 0))
    mat_spec = pl.BlockSpec((None, 2, SSM_GROUP, LANES), lambda i: (i, 0, 0, 0))
    return pl.pallas_call(
        functools.partial(_ssm_kernel, seq_chunks=seq_chunks),
        grid=(g,),
        in_specs=[
            pl.BlockSpec((None, n_ct, CHUNK, SSM_GROUP, C_TILE), lambda i: (i, 0, 0, 0, 0)),
            vec_spec, vec_spec, vec_spec,
            mat_spec, mat_spec, mat_spec, mat_spec, mat_spec,
            pl.BlockSpec((None, SSM_GROUP, 1), lambda i: (i, 0, 0)),
            pl.BlockSpec((4, CHUNK_W, LANES), lambda i: (0, 0, 0)),
            pl.BlockSpec((epg, D_MODEL, D_EXPERT), lambda i: (base + i, 0, 0)),
            pl.BlockSpec((epg, D_MODEL, D_EXPERT), lambda i: (base + i, 0, 0)),
            pl.BlockSpec((epg, D_EXPERT, D_MODEL), lambda i: (base + i, 0, 0)),
        ],
        out_specs=[
            pl.BlockSpec((None, n_ct, CHUNK, SSM_GROUP, C_TILE), lambda i: (i, 0, 0, 0, 0)),
            pl.BlockSpec((epg, D_MODEL, D_EXPERT), lambda i: (i, 0, 0)),
            pl.BlockSpec((epg, D_MODEL, D_EXPERT), lambda i: (i, 0, 0)),
            pl.BlockSpec((epg, D_EXPERT, D_MODEL), lambda i: (i, 0, 0)),
        ],
        out_shape=[
            jax.ShapeDtypeStruct(ut.shape, BF16),
            jax.ShapeDtypeStruct((N_EXPERTS, D_MODEL, D_EXPERT), BF16),
            jax.ShapeDtypeStruct((N_EXPERTS, D_MODEL, D_EXPERT), BF16),
            jax.ShapeDtypeStruct((N_EXPERTS, D_EXPERT, D_MODEL), BF16),
        ],
        compiler_params=_params(("parallel",)),
        name="ssm",
    )(ut, are, aim, ldt, bt(b_re), bt(b_im), ccr, ct(c_re), ct(c_im), d_col, exp_c,
      w_gate, w_up, w_down)


def _mix_out_kernel(x_ref, a_ref, yt_ref, gwt_ref, gb_ref, wo_ref, perm_ref, h_ref):
    nc = x_ref.shape[0]
    rows = nc * S_TILE
    y = jnp.concatenate([yt_ref[:, j, :, :].reshape(D_SSM, nc) for j in range(S_TILE)],
                        axis=1).astype(F32)
    z = 0.5 * y * (1.0 + jnp.tanh(math.sqrt(2.0 / math.pi) * (y + 0.044715 * (y * y * y))))
    gate = _sigmoid(_dot(gwt_ref[...], z.astype(BF16)) + gb_ref[...])
    s = (z * gate).astype(BF16)
    s = _dot_nt(s, perm_ref[...]).astype(BF16)
    h = (x_ref[...].reshape(rows, D_MODEL)
         + _dot(a_ref[...].reshape(rows, D_POOL), wo_ref[:D_POOL, :])
         + _dot_tn(s, wo_ref[D_POOL:, :]))
    h_ref[...] = h.reshape(h_ref.shape)


def _mix_out(x3, a3, yt, glu_w_t, glu_b_col, w_out, perm):
    nch = x3.shape[0]
    return pl.pallas_call(
        _mix_out_kernel,
        grid=(nch // C_TILE, CHUNK // S_TILE),
        in_specs=[
            pl.BlockSpec((C_TILE, S_TILE, D_MODEL), lambda c, t: (c, t, 0)),
            pl.BlockSpec((C_TILE, S_TILE, D_POOL), lambda c, t: (c, t, 0)),
            pl.BlockSpec((N_SSM_GROUPS, None, S_TILE, SSM_GROUP, C_TILE), lambda c, t: (0, c, t, 0, 0)),
            pl.BlockSpec((D_SSM, D_SSM), lambda c, t: (0, 0)),
            pl.BlockSpec((D_SSM, 1), lambda c, t: (0, 0)),
            pl.BlockSpec((D_MODEL, D_MODEL), lambda c, t: (0, 0)),
            pl.BlockSpec((C_TILE * S_TILE, C_TILE * S_TILE), lambda c, t: (0, 0)),
        ],
        out_specs=pl.BlockSpec((C_TILE, S_TILE, D_MODEL), lambda c, t: (c, t, 0)),
        out_shape=jax.ShapeDtypeStruct((nch, CHUNK, D_MODEL), F32),
        compiler_params=_params(("parallel", "parallel")),
        name="mix_out",
    )(x3, a3, yt, glu_w_t, glu_b_col, w_out, perm)


def _pack_rows(x):
    b = lax.bitcast_convert_type(x.astype(BF16).astype(F32), U32)
    return (b[:, :HALF] & jnp.uint32(0xFFFF0000)) | (b[:, HALF:] >> 16)


def _unpack_rows(w):
    lo = lax.bitcast_convert_type(w & jnp.uint32(0xFFFF0000), F32)
    hi = lax.bitcast_convert_type(w << 16, F32)
    return lo, hi


def _split_bf16(x):
    hi = x.astype(BF16)
    return hi, (x - hi.astype(F32)).astype(BF16)


def _route(v32, wr_ref, br_ref):
    v_hi, v_lo = _split_bf16(v32)
    w_hi, w_lo = _split_bf16(wr_ref[...])
    logits = _dot(v_hi, w_hi) + (_dot(v_lo, w_hi) + _dot(v_hi, w_lo)) + br_ref[...]
    return jnp.transpose(logits)


def _top1(x, valid=None):
    n = x.shape[0]
    row = lax.broadcasted_iota(jnp.int32, x.shape, 0).astype(F32)
    if valid is not None:
        x = jnp.where(valid, x, -jnp.inf)
    m = jnp.max(x, axis=0, keepdims=True)
    idx = jnp.min(jnp.where(x == m, row, float(n)), axis=0, keepdims=True)
    return m, idx, x, row


def _split_planes(packed, ref):
    ref[0] = packed[:, :SC_ROW]
    ref[1] = packed[:, SC_ROW:]


def _router_kernel(h_ref, g_ref, wr_ref, br_ref, before_ref, vp_ref, meta_t_ref, cnt_ref, carry_ref):
    @pl.when(pl.program_id(0) == 0)
    def _():
        carry_ref[...] = jnp.zeros_like(carry_ref)

    v32 = _rms(h_ref[...], g_ref[...])
    _split_planes(_pack_rows(v32), vp_ref)
    lt = _route(v32, wr_ref, br_ref)
    tm = lt.shape[1]
    eg = EXPERTS_PER_GROUP

    grp = lt[:eg]
    grp_row = lax.broadcasted_iota(jnp.int32, grp.shape, 0)
    mg, grp_idx, grp, _ = _top1(grp, grp_row < N_EXPERT_GROUPS)
    grp_p = 1.0 / jnp.sum(jnp.exp(grp - mg), axis=0, keepdims=True)
    le = jnp.zeros((eg, tm), F32)
    for g in range(N_EXPERT_GROUPS):
        le = jnp.where(grp_idx == float(g), lt[eg * (g + 1):eg * (g + 2)], le)
    m1, i1, le, row = _top1(le)
    z = jnp.sum(jnp.exp(le - m1), axis=0, keepdims=True)
    m2, i2, _, _ = _top1(jnp.where(row == i1, -jnp.inf, le))
    p1 = 1.0 / z
    p2 = jnp.exp(m2 - m1) / z
    tot = p1 + p2
    w1 = grp_p * (p1 / tot)
    w2 = grp_p * (p2 / tot)
    e1 = grp_idx * eg + i1
    e2 = grp_idx * eg + i2

    erow = lax.broadcasted_iota(jnp.int32, (N_EXPERTS, tm), 0).astype(F32)
    onehot = jnp.where(erow == e1, 1.0, jnp.where(erow == e2, 1.0, 0.0))
    before = _dot(onehot.astype(BF16), before_ref[...]) + carry_ref[...]
    rank1 = jnp.sum(jnp.where(erow == e1, before, 0.0), axis=0, keepdims=True)
    rank2 = jnp.sum(jnp.where(erow == e2, before, 0.0), axis=0, keepdims=True)
    carry = carry_ref[...] + jnp.sum(onehot, axis=1, keepdims=True)
    carry_ref[...] = carry
    cnt_ref[...] = carry

    mrow = lax.broadcasted_iota(jnp.int32, (META_ROWS, tm), 0)
    meta_t_ref[...] = jnp.where(mrow == 0, e1, jnp.where(mrow == 1, e2, jnp.where(
        mrow == 2, rank1, jnp.where(mrow == 3, rank2, jnp.where(
            mrow == 4, w1, jnp.where(mrow == 5, w2, 0.0))))))


def _earlier_matrix(tm):
    return np.triu(np.ones((tm, tm), np.float32), k=1)


def _router(h1, g_ffn, w_router, b_router, tm):
    t = h1.shape[0]
    return pl.pallas_call(
        _router_kernel,
        grid=(t // tm,),
        in_specs=[
            pl.BlockSpec((tm, D_MODEL), lambda i: (i, 0)),
            pl.BlockSpec((1, D_MODEL), lambda i: (0, 0)),
            pl.BlockSpec((D_MODEL, ROUTER_W), lambda i: (0, 0)),
            pl.BlockSpec((1, ROUTER_W), lambda i: (0, 0)),
            pl.BlockSpec((tm, tm), lambda i: (0, 0)),
        ],
        out_specs=[
            pl.BlockSpec((2, tm, SC_ROW), lambda i: (0, i, 0)),
            pl.BlockSpec((META_ROWS, tm), lambda i: (0, i)),
            pl.BlockSpec((N_EXPERTS, 1), lambda i: (0, 0)),
        ],
        out_shape=[
            jax.ShapeDtypeStruct((2, t, SC_ROW), U32),
            jax.ShapeDtypeStruct((META_ROWS, t), F32),
            jax.ShapeDtypeStruct((N_EXPERTS, 1), F32),
        ],
        scratch_shapes=[pltpu.VMEM((N_EXPERTS, 1), F32)],
        compiler_params=_params(("arbitrary",)),
        name="router",
    )(h1, g_ffn, w_router, b_router, jnp.asarray(_earlier_matrix(tm), BF16))


def _plan(meta_t, counts, n_tiles):
    e1 = meta_t[0].astype(jnp.int32)
    e2 = meta_t[1].astype(jnp.int32)
    rank1 = meta_t[2].astype(jnp.int32)
    rank2 = meta_t[3].astype(jnp.int32)
    cnt = counts[:, 0].astype(jnp.int32)
    padded = ((cnt + ROW_TILE - 1) // ROW_TILE) * ROW_TILE
    ends = jnp.cumsum(padded)
    starts = ends - padded
    experts = jnp.arange(N_EXPERTS, dtype=jnp.int32)
    pos1 = rank1 + jnp.sum(jnp.where(e1[None, :] == experts[:, None], starts[:, None], 0), axis=0)
    pos2 = rank2 + jnp.sum(jnp.where(e2[None, :] == experts[:, None], starts[:, None], 0), axis=0)
    tile_start = jnp.arange(n_tiles, dtype=jnp.int32) * ROW_TILE
    tile_expert = jnp.sum((tile_start[:, None] >= ends[None, :]).astype(jnp.int32), axis=1)
    tile_expert = jnp.minimum(tile_expert, N_EXPERTS - 1)
    rows_left = jnp.sum(jnp.where(tile_expert[:, None] == experts, cnt + starts, 0), axis=1) - tile_start
    n_valid = jnp.clip(rows_left, 0, ROW_TILE).astype(jnp.int32)
    last_used = jnp.maximum(ends[-1] // ROW_TILE - 1, 0)
    block = jnp.minimum(jnp.arange(n_tiles, dtype=jnp.int32), last_used)
    tile_expert = jnp.sum(jnp.where(block[:, None] == jnp.arange(n_tiles)[None, :],
                                    tile_expert[None, :], 0), axis=1)
    plane = n_tiles * ROW_TILE
    half_rows = jnp.concatenate([pos1, pos1 + plane, pos2, pos2 + plane])[None]
    return half_rows, tile_expert, n_valid, block


def _sc_mesh():
    return plsc.VectorSubcoreMesh(core_axis_name="c", subcore_axis_name="s")


def _sc_scatter_rows(rows, idx, n_out):
    t, width = rows.shape
    steps = t // SC_WINDOW

    @pl.kernel(out_type=jax.ShapeDtypeStruct((n_out, width), rows.dtype), mesh=_sc_mesh(),
               scratch_types=[], name="moe_scatter")
    def scatter(rows_hbm, idx_hbm, out_hbm):
        def body(rows_vmem, idx_vmem):
            pltpu.sync_copy(rows_vmem, out_hbm.at[idx_vmem.at[0]])

        pltpu.emit_pipeline(
            body,
            grid=(2, steps),
            in_specs=[pl.BlockSpec((SC_WINDOW, width), lambda k, j: (j, 0)),
                      pl.BlockSpec((1, SC_WINDOW), lambda k, j: (0, k * steps + j))],
            out_specs=[],
            core_axis_name=("c", "s"),
            dimension_semantics=(pltpu.PARALLEL, pltpu.PARALLEL),
        )(rows_hbm, idx_hbm)

    return scatter(rows, idx)


def _sc_gather_rows(table, idx):
    m = idx.shape[1]
    width = table.shape[1]
    steps = m // (2 * SC_WINDOW)

    @pl.kernel(out_type=jax.ShapeDtypeStruct((m, width), table.dtype), mesh=_sc_mesh(),
               scratch_types=[], name="moe_gather")
    def gather(table_hbm, idx_hbm, out_hbm):
        def body(idx_vmem, out_vmem):
            pltpu.sync_copy(table_hbm.at[idx_vmem.at[0]], out_vmem)

        pltpu.emit_pipeline(
            body,
            grid=(2, steps),
            in_specs=[pl.BlockSpec((1, SC_WINDOW), lambda k, j: (0, k * steps + j))],
            out_specs=[pl.BlockSpec((SC_WINDOW, width), lambda k, j: (k * steps + j, 0))],
            core_axis_name=("c", "s"),
            dimension_semantics=(pltpu.PARALLEL, pltpu.PARALLEL),
        )(idx_hbm, out_hbm)

    return gather(table, idx)


def _experts_kernel(te_ref, nv_ref, blk_ref, xs_ref, wg_ref, wu_ref, wd_ref, ys_ref):
    r = pl.program_id(0)
    n_valid = nv_ref[r]

    @pl.when(n_valid > 0)
    def _():
        parts = [p.astype(BF16) for p in _unpack_rows(xs_ref[0]) + _unpack_rows(xs_ref[1])]
        cols = (0, 2 * SC_ROW, SC_ROW, 3 * SC_ROW)
        hg = sum(_dot(p, wg_ref[c:c + SC_ROW, :]) for p, c in zip(parts, cols))
        hu = sum(_dot(p, wu_ref[c:c + SC_ROW, :]) for p, c in zip(parts, cols))
        row = lax.broadcasted_iota(jnp.int32, (ROW_TILE, 1), 0)
        hid = jnp.where(row < n_valid, hg * _sigmoid(hg) * hu, 0.0).astype(BF16)
        _split_planes(_pack_rows(_dot(hid, wd_ref[...])), ys_ref)


def _experts(xs, tile_expert, n_valid, block, w_gate, w_up, w_down):
    n_tiles = xs.shape[1] // ROW_TILE
    w_spec = pl.BlockSpec((None, D_MODEL, D_EXPERT), lambda r, te, nv, blk: (te[r], 0, 0))
    grid_spec = pltpu.PrefetchScalarGridSpec(
        num_scalar_prefetch=3,
        grid=(n_tiles,),
        in_specs=[
            pl.BlockSpec((2, ROW_TILE, SC_ROW), lambda r, te, nv, blk: (0, blk[r], 0)),
            w_spec, w_spec,
            pl.BlockSpec((None, D_EXPERT, D_MODEL), lambda r, te, nv, blk: (te[r], 0, 0)),
        ],
        out_specs=pl.BlockSpec((2, ROW_TILE, SC_ROW), lambda r, te, nv, blk: (0, blk[r], 0)),
    )
    return pl.pallas_call(
        _experts_kernel,
        grid_spec=grid_spec,
        out_shape=jax.ShapeDtypeStruct((2, n_tiles * ROW_TILE, SC_ROW), U32),
        compiler_params=_params(("arbitrary",)),
        name="experts",
    )(tile_expert, n_valid, block, xs, w_gate, w_up, w_down)


def _ple_kernel(h_ref, yg_ref, meta_t_ref, p_ref, wg_ref, bg_ref, wp_ref, gf_ref, *rest, final_norm):
    o_ref = rest[-1]
    meta = jnp.transpose(meta_t_ref[...])
    w1 = meta[:, 4:5]
    w2 = meta[:, 5:6]
    q0, q2 = (w1 * u + w2 * v for u, v in zip(_unpack_rows(yg_ref[0]), _unpack_rows(yg_ref[2])))
    q1, q3 = (w1 * u + w2 * v for u, v in zip(_unpack_rows(yg_ref[1]), _unpack_rows(yg_ref[3])))
    moe = jnp.concatenate([q0, q1, q2, q3], axis=1)
    h = h_ref[...] + moe
    gate = _sigmoid(_dot(_rms(h).astype(BF16), wg_ref[...]) + bg_ref[...])
    h = h + gate * _dot(p_ref[...].astype(BF16), wp_ref[...])
    o_ref[...] = _rms(h, gf_ref[...]) if final_norm else h


def _ple(h1, yg, meta_t, p2, w_gate, b_gate, w_proj, g_final, final_norm, tm, part, prev_out):
    t = h1.shape[0]
    steps = t // tm // GATHER_PARTS
    off = part * steps
    in_specs = [
        pl.BlockSpec((tm, D_MODEL), lambda i: (i + off, 0)),
        pl.BlockSpec((4, tm, SC_ROW), lambda i: (0, i, 0)),
        pl.BlockSpec((META_ROWS, tm), lambda i: (0, i + off)),
        pl.BlockSpec((tm, D_PLE), lambda i: (i + off, 0)),
        pl.BlockSpec((D_MODEL, D_MODEL), lambda i: (0, 0)),
        pl.BlockSpec((1, D_MODEL), lambda i: (0, 0)),
        pl.BlockSpec((D_PLE, D_MODEL), lambda i: (0, 0)),
        pl.BlockSpec((1, D_MODEL), lambda i: (0, 0)),
    ]
    args = [h1, yg, meta_t, p2, w_gate, b_gate, w_proj, g_final]
    aliases = {}
    if prev_out is not None:
        in_specs.append(pl.BlockSpec(memory_space=pl.ANY))
        args.append(prev_out)
        aliases = {len(args) - 1: 0}
    return pl.pallas_call(
        functools.partial(_ple_kernel, final_norm=final_norm),
        grid=(steps,),
        in_specs=in_specs,
        out_specs=pl.BlockSpec((tm, D_MODEL), lambda i: (i + off, 0)),
        out_shape=jax.ShapeDtypeStruct((t, D_MODEL), F32),
        input_output_aliases=aliases,
        compiler_params=_params(("parallel",)),
        name="ple",
    )(*args)


def kernel(x, p, g_mix, w_in, pool_w, pool_scale, ssm_a_re, ssm_a_im, ssm_log_dt, ssm_b_re,
           ssm_b_im, ssm_c_re, ssm_c_im, ssm_d, glu_w, glu_b, w_out, g_ffn, router_grp_w,
           router_grp_b, router_exp_w, router_exp_b, exp_w_gate, exp_w_up, exp_w_down, g_ple,
           ple_gate_w, ple_gate_b, ple_proj_w, g_final):
    bsz, seq, dm = x.shape
    depth = g_mix.shape[0]
    t = bsz * seq
    seq_chunks = seq // CHUNK
    nch = t // CHUNK
    tm = 512
    n_sorted = 2 * t + N_EXPERTS * ROW_TILE
    w_gate_all = exp_w_gate.reshape(depth * N_EXPERTS, dm, D_EXPERT)
    w_up_all = exp_w_up.reshape(depth * N_EXPERTS, dm, D_EXPERT)
    w_down_all = exp_w_down.reshape(depth * N_EXPERTS, D_EXPERT, dm)

    perm = jnp.asarray(_regroup_matrix(), BF16)
    h = x.reshape(t, dm)
    for i in range(depth):
        w_in_b = (g_mix[i][:, None] * w_in[i]).astype(BF16)
        zp, ut = _in_proj(h.reshape(nch, CHUNK, dm), w_in_b[:, :D_POOL],
                          jnp.transpose(w_in_b[:, D_POOL:]), perm)
        a = _pool(zp.reshape(bsz, seq, D_POOL), pool_w[i], pool_scale[i][None])
        yt, w_gate_b, w_up_b, w_down_b = _ssm(
            ut, ssm_a_re[i], ssm_a_im[i], ssm_log_dt[i], ssm_b_re[i], ssm_b_im[i], ssm_c_re[i],
            ssm_c_im[i], ssm_d[i], seq_chunks, w_gate_all, w_up_all, w_down_all, i)
        h = _mix_out(h.reshape(nch, CHUNK, dm), a.reshape(nch, CHUNK, D_POOL), yt,
                     jnp.transpose(glu_w[i]).astype(BF16), glu_b[i][:, None],
                     w_out[i].astype(BF16), perm).reshape(t, dm)

        eg = EXPERTS_PER_GROUP
        w_router = jnp.concatenate(
            [router_grp_w[i], jnp.zeros((dm, eg - N_EXPERT_GROUPS), F32),
             jnp.transpose(router_exp_w[i], (1, 0, 2)).reshape(dm, N_EXPERTS),
             jnp.zeros((dm, ROUTER_W - eg - N_EXPERTS), F32)], axis=1)
        b_router = jnp.concatenate(
            [router_grp_b[i], jnp.zeros((eg - N_EXPERT_GROUPS,), F32),
             router_exp_b[i].reshape(N_EXPERTS),
             jnp.zeros((ROUTER_W - eg - N_EXPERTS,), F32)])[None]
        vp, meta_t, counts = _router(h, g_ffn[i][None], w_router, b_router, tm)
        idx, tile_expert, n_valid, block = _plan(meta_t, counts, n_sorted // ROW_TILE)
        xs = _sc_scatter_rows(vp.reshape(2 * t, SC_ROW), idx, 2 * n_sorted)
        ys = _experts(xs.reshape(2, n_sorted, SC_ROW), tile_expert, n_valid, block,
                      w_gate_b, w_up_b, w_down_b)
        ys2 = ys.reshape(2 * n_sorted, SC_ROW)
        idx4 = idx.reshape(4, t)
        tp = t // GATHER_PARTS
        ple_wg = (g_ple[i][:, None] * ple_gate_w[i]).astype(BF16)
        ple_wp = ple_proj_w[i].astype(BF16)
        out = None
        for q in range(GATHER_PARTS):
            yg_q = _sc_gather_rows(ys2, idx4[:, q * tp:(q + 1) * tp].reshape(1, 4 * tp))
            out = _ple(h, yg_q.reshape(4, tp, SC_ROW), meta_t, p[i].reshape(t, D_PLE), ple_wg,
                       ple_gate_b[i][None], ple_wp, g_final[None], i == depth - 1, tm, q, out)
        h = out
    return h.reshape(bsz, seq, dm)
```

```python
import functools
import math

import numpy as np
import jax
import jax.numpy as jnp
from jax import lax
from jax.experimental import pallas as pl
from jax.experimental.pallas import tpu as pltpu
from jax.experimental.pallas import tpu_sc as plsc

F32 = jnp.float32
BF16 = jnp.bfloat16
U32 = jnp.uint32

D_MODEL = 1024
D_POOL = 512
D_SSM = 512
POOL_WINDOWS = (2, 4, 8, 16)
POOL_GROUP = 128
SSM_GROUP = 16
N_SSM_GROUPS = 32
SSM_STATE = 64
N_EXPERT_GROUPS = 4
EXPERTS_PER_GROUP = 8
N_EXPERTS = N_EXPERT_GROUPS * EXPERTS_PER_GROUP
D_EXPERT = 256
D_PLE = 256
RMS_EPS = 1e-6

LANES = 128
CHUNK = 32
CHUNK_W = CHUNK * SSM_GROUP
T_SUB = 8
S_TILE = 8
C_TILE = 128
ROUTER_W = LANES
HALF = D_MODEL // 2
ROW_TILE = 1024
SC_WINDOW = 128
SC_ROW = HALF // 2
META_ROWS = 8
GATHER_PARTS = 4
VMEM_LIMIT = 56 * 1024 * 1024


def _dot(a, b):
    return jnp.dot(a, b, preferred_element_type=F32)


def _dot_nt(a, b):
    return lax.dot_general(a, b, (((1,), (1,)), ((), ())), preferred_element_type=F32)


def _dot_tn(a, b):
    return lax.dot_general(a, b, (((0,), (0,)), ((), ())), preferred_element_type=F32)


def _rms(x, g=None):
    y = x * lax.rsqrt(jnp.mean(x * x, axis=-1, keepdims=True) + RMS_EPS)
    return y if g is None else y * g


def _sigmoid(x):
    return 1.0 / (1.0 + jnp.exp(-x))


def _params(sem):
    return pltpu.CompilerParams(dimension_semantics=sem, vmem_limit_bytes=VMEM_LIMIT)


def _regroup_matrix():
    n = C_TILE * S_TILE
    src = np.arange(n)
    dst = (src % S_TILE) * C_TILE + src // S_TILE
    p = np.zeros((n, n), np.float32)
    p[src, dst] = 1.0
    return p


def _in_proj_kernel(x_ref, wp_ref, wst_ref, perm_ref, zp_ref, ut_ref):
    nc = x_ref.shape[0]
    u = _rms(x_ref[...].reshape(nc * S_TILE, D_MODEL)).astype(BF16)
    zp_ref[...] = _dot(u, wp_ref[...]).reshape(zp_ref.shape)
    zt = _dot_nt(wst_ref[...], u).astype(BF16)
    zt = _dot(zt, perm_ref[...]).astype(BF16)
    for j in range(S_TILE):
        ut_ref[:, j, :, :] = zt[:, j * nc:(j + 1) * nc].reshape(N_SSM_GROUPS, SSM_GROUP, nc)


def _in_proj(x3, w_pool, w_ssm_t, perm):
    nch = x3.shape[0]
    return pl.pallas_call(
        _in_proj_kernel,
        grid=(nch // C_TILE, CHUNK // S_TILE),
        in_specs=[
            pl.BlockSpec((C_TILE, S_TILE, D_MODEL), lambda c, s: (c, s, 0)),
            pl.BlockSpec((D_MODEL, D_POOL), lambda c, s: (0, 0)),
            pl.BlockSpec((D_SSM, D_MODEL), lambda c, s: (0, 0)),
            pl.BlockSpec((C_TILE * S_TILE, C_TILE * S_TILE), lambda c, s: (0, 0)),
        ],
        out_specs=[
            pl.BlockSpec((C_TILE, S_TILE, D_POOL), lambda c, s: (c, s, 0)),
            pl.BlockSpec((N_SSM_GROUPS, None, S_TILE, SSM_GROUP, C_TILE), lambda c, s: (0, c, s, 0, 0)),
        ],
        out_shape=[
            jax.ShapeDtypeStruct((nch, CHUNK, D_POOL), F32),
            jax.ShapeDtypeStruct((N_SSM_GROUPS, nch // C_TILE, CHUNK, SSM_GROUP, C_TILE), BF16),
        ],
        compiler_params=_params(("parallel", "parallel")),
        name="in_proj",
    )(x3, w_pool, w_ssm_t, perm)


def _shift_rows(x, d, row, n):
    if d == 0:
        return x
    r = pltpu.roll(x, d % n, 0)
    if d > 0:
        return jnp.where(row >= d, r, 0.0)
    return jnp.where(row < n + d, r, 0.0)


def _pool_kernel(z_ref, w_ref, sc_ref, o_ref):
    n = z_ref.shape[0]
    gi = pl.program_id(1)
    row = lax.broadcasted_iota(jnp.int32, (n, 1), 0)

    for k, w in enumerate(POOL_WINDOWS):
        @pl.when(gi == k)
        def _(w=w):
            x = z_ref[...]
            half = w // 2
            pd, pu, span = x, x, 1
            while span < half:
                pd = pd + _shift_rows(pd, span, row, n)
                pu = pu + _shift_rows(pu, -span, row, n)
                span *= 2
            total = _shift_rows(pd, 1, row, n) + pu
            lo = jnp.maximum(row - half, 0)
            hi = jnp.minimum(row + half, n)
            cnt = (hi - lo).astype(F32)
            diff = (total / cnt - x).astype(BF16)
            o_ref[...] = (_dot(diff, w_ref[...].astype(BF16)) * sc_ref[...]).astype(BF16)


def _pool(zp3, pool_w, pool_scale):
    b, s, _ = zp3.shape
    return pl.pallas_call(
        _pool_kernel,
        grid=(b, len(POOL_WINDOWS)),
        in_specs=[
            pl.BlockSpec((None, s, POOL_GROUP), lambda i, g: (i, 0, g)),
            pl.BlockSpec((None, POOL_GROUP, POOL_GROUP), lambda i, g: (g, 0, 0)),
            pl.BlockSpec((1, POOL_GROUP), lambda i, g: (0, g)),
        ],
        out_specs=pl.BlockSpec((None, s, POOL_GROUP), lambda i, g: (i, 0, g)),
        out_shape=jax.ShapeDtypeStruct((b, s, D_POOL), BF16),
        compiler_params=_params(("parallel", "parallel")),
        name="pool",
    )(zp3, pool_w, pool_scale)


def _expand_consts():
    time = np.arange(CHUNK_W) // SSM_GROUP
    def onehot(e):
        m = np.zeros((CHUNK_W, LANES), np.float32)
        m[np.arange(CHUNK_W), e] = 1.0
        return m
    return np.stack([
        onehot(CHUNK - 1 - time),
        onehot(time),
        onehot(time + 1),
        onehot(CHUNK - time),
    ])


def _cmul_packed(x, p, q):
    return x * p + pltpu.roll(x, LANES // 2, 1) * q


def _ssm_kernel(u_ref, are_ref, aim_ref, ldt_ref, bt_re_ref, bt_im_ref, ccr_ref, ct_re_ref,
                ct_im_ref, d_ref, exp_ref, wg_ref, wu_ref, wd_ref, y_ref, og_ref, ou_ref, od_ref, *,
                seq_chunks):
    og_ref[...] = wg_ref[...].astype(BF16)
    ou_ref[...] = wu_ref[...].astype(BF16)
    od_ref[...] = wd_ref[...].astype(BF16)

    n_ct = u_ref.shape[0]
    nch = n_ct * C_TILE
    half = LANES // 2
    lane = lax.broadcasted_iota(jnp.int32, (1, LANES), 1)
    lo_half = lane < half

    def direction(di):
        a_re = are_ref[di]
        a_im = aim_ref[di]
        dt = jnp.exp(ldt_ref[di])
        mag = jnp.exp(a_re * dt)
        ang = a_im * dt
        lam = jnp.where(lo_half, mag * jnp.cos(ang), mag * jnp.sin(ang))
        lb_re = mag * jnp.cos(ang)
        lb_im = mag * jnp.sin(ang)
        den = a_re * a_re + a_im * a_im
        f_re = ((lb_re - 1.0) * a_re + lb_im * a_im) / den
        f_im = (lb_im * a_re - (lb_re - 1.0) * a_im) / den
        return lam, f_re, f_im

    def power_table(lam):
        e = lax.broadcasted_iota(jnp.int32, (LANES, 1), 0)
        tab = jnp.where(lo_half, 1.0, 0.0) * jnp.ones((LANES, 1), F32)
        sq = lam
        for k in range(7):
            p = jnp.where(lo_half, sq, pltpu.roll(sq, half, 1))
            q = jnp.where(lo_half, -pltpu.roll(sq, half, 1), sq)
            tab = jnp.where(((e >> k) & 1) == 1, _cmul_packed(tab, p, q), tab)
            sq = _cmul_packed(sq, p, q)
        return tab

    def tile_rows(x16):
        return jnp.broadcast_to(x16[None], (CHUNK, SSM_GROUP, LANES)).reshape(CHUNK_W, LANES)

    def expanded(tab, which, v_re, v_im, conj_sign):
        lexp = _dot(exp_ref[which], tab.astype(BF16))
        if conj_sign > 0:
            p = jnp.where(lo_half, v_re, v_re)
            q = jnp.where(lo_half, -v_im, v_im)
        else:
            p = jnp.where(lo_half, v_re, -v_re)
            q = jnp.where(lo_half, -v_im, -v_im)
        return lexp * tile_rows(p) + pltpu.roll(lexp, half, 1) * tile_rows(q)

    lam_f, ff_re, ff_im = direction(0)
    lam_b, fb_re, fb_im = direction(1)
    tab_f = power_table(lam_f)
    tab_b = power_table(lam_b)

    def bbar(bt_re, bt_im, f_re, f_im):
        return bt_re * f_re - bt_im * f_im, bt_re * f_im + bt_im * f_re

    bf_re, bf_im = bbar(bt_re_ref[0], bt_im_ref[0], ff_re, ff_im)
    bb_re, bb_im = bbar(bt_re_ref[1], bt_im_ref[1], fb_re, fb_im)

    pb1 = expanded(tab_f, 0, bf_re, bf_im, 1)
    pb2 = expanded(tab_b, 1, bb_re, bb_im, 1)
    pb3 = expanded(tab_b, 2, bb_re, bb_im, 1)
    ft_f = expanded(tab_f, 2, ct_re_ref[0], ct_im_ref[0], -1)
    ft_b = expanded(tab_b, 3, ct_re_ref[1], ct_im_ref[1], -1)

    row_w = lax.broadcasted_iota(jnp.int32, (CHUNK_W, 1), 0)
    last_blk = row_w >= CHUNK_W - SSM_GROUP
    pb2_lag0 = jnp.where(last_blk, pltpu.roll(pb2, CHUNK_W - SSM_GROUP, 0), 0.0)
    ccr_f = ccr_ref[0].astype(BF16)
    ccr_b = ccr_ref[1].astype(BF16)
    r_lo = _dot_nt(ccr_f, pb1.astype(BF16)) + _dot_nt(ccr_b, pb2_lag0.astype(BF16))
    co = lax.broadcasted_iota(jnp.int32, (SSM_GROUP, CHUNK_W), 0)
    col = lax.broadcasted_iota(jnp.int32, (SSM_GROUP, CHUNK_W), 1)
    r_lo = r_lo + jnp.where(col == CHUNK_W - SSM_GROUP + co, d_ref[...], 0.0)
    r_hi = _dot_nt(ccr_b, pb3.astype(BF16))
    r_t = jnp.concatenate([r_lo, r_hi], axis=1)
    g_t = jnp.concatenate(
        [pltpu.roll(r_t, SSM_GROUP * (tl + 1), 1) for tl in range(T_SUB)], axis=0
    ).astype(BF16)

    u = jnp.concatenate([u_ref[ct].reshape(CHUNK_W, C_TILE) for ct in range(n_ct)],
                        axis=1)
    e_mat = jnp.concatenate([pb1, pb2], axis=1).astype(BF16)
    xend = _dot_tn(e_mat, u)
    lanec = lax.broadcasted_iota(jnp.int32, (1, nch), 1) % seq_chunks
    ns = SSM_STATE

    def scan(re, im, tab, forward):
        lam_col = jnp.transpose(tab[CHUNK:CHUNK + 8, :])[:, 0:1]
        a, b = lam_col[:ns], lam_col[ns:]
        n_steps = int(math.log2(seq_chunks))

        def shifted(v, d):
            if forward:
                return jnp.where(lanec >= d, pltpu.roll(v, d, 1), 0.0)
            return jnp.where(lanec < seq_chunks - d, pltpu.roll(v, nch - d, 1), 0.0)

        for k in range(n_steps):
            sr, si = shifted(re, 1 << k), shifted(im, 1 << k)
            re, im = re + (sr * a - si * b), im + (sr * b + si * a)
            a, b = a * a - b * b, 2.0 * a * b
        return shifted(re, 1), shifted(im, 1)

    f_re, f_im = scan(xend[:ns], xend[ns:2 * ns], tab_f, True)
    b_re, b_im = scan(xend[2 * ns:3 * ns], xend[3 * ns:], tab_b, False)
    xin = jnp.concatenate([f_re, f_im, b_re, b_im], axis=0).astype(BF16)
    f_t = jnp.concatenate([ft_f, ft_b], axis=1).astype(BF16)

    toeplitz = jnp.concatenate(
        [g_t[:, CHUNK_W - LANES * th:2 * CHUNK_W - LANES * th] for th in range(CHUNK // T_SUB)],
        axis=0)
    y_t = _dot(toeplitz, u) + _dot(f_t, xin)
    for ct in range(n_ct):
        y_ref[ct] = y_t[:, ct * C_TILE:(ct + 1) * C_TILE].reshape(
            CHUNK, SSM_GROUP, C_TILE).astype(y_ref.dtype)


def _ssm(ut, a_re, a_im, log_dt, b_re, b_im, c_re, c_im, d, seq_chunks, w_gate, w_up, w_down, layer):
    g, n_ct = ut.shape[:2]
    assert N_EXPERTS % g == 0
    epg = N_EXPERTS // g
    base = layer * g
    n = SSM_STATE

    def lane_vec(a):
        a = jnp.transpose(a, (1, 0, 2))
        return jnp.concatenate([a, a], axis=-1)[:, :, None, :]

    are = lane_vec(a_re)
    aim = lane_vec(a_im)
    ldt = lane_vec(jnp.broadcast_to(log_dt[..., None], (2, g, n)))

    def bt(b):
        b = jnp.transpose(b, (1, 0, 3, 2))
        return jnp.concatenate([b, b], axis=-1)

    def ct(c):
        c = jnp.transpose(c, (1, 0, 2, 3))
        return jnp.concatenate([c, c], axis=-1)

    ccr = jnp.concatenate([jnp.transpose(c_re, (1, 0, 2, 3)),
                           -jnp.transpose(c_im, (1, 0, 2, 3))], axis=-1)
    d_col = d.reshape(g, SSM_GROUP, 1)
    exp_c = jnp.asarray(_expand_consts(), BF16)

    vec_spec = pl.BlockSpec((None, 2, 1, LANES), lambda i: (i, 0, 0, 0))
    mat_spec = pl.BlockSpec((None, 2, SSM_GROUP, LANES), lambda i: (i, 0, 0, 0))
    return pl.pallas_call(
        functools.partial(_ssm_kernel, seq_chunks=seq_chunks),
        grid=(g,),
        in_specs=[
            pl.BlockSpec((None, n_ct, CHUNK, SSM_GROUP, C_TILE), lambda i: (i, 0, 0, 0, 0)),
            vec_spec, vec_spec, vec_spec,
            mat_spec, mat_spec, mat_spec, mat_spec, mat_spec,
            pl.BlockSpec((None, SSM_GROUP, 1), lambda i: (i, 0, 0)),
            pl.BlockSpec((4, CHUNK_W, LANES), lambda i: (0, 0, 0)),
            pl.BlockSpec((epg, D_MODEL, D_EXPERT), lambda i: (base + i, 0, 0)),
            pl.BlockSpec((epg, D_MODEL, D_EXPERT), lambda i: (base + i, 0, 0)),
            pl.BlockSpec((epg, D_EXPERT, D_MODEL), lambda i: (base + i, 0, 0)),
        ],
        out_specs=[
            pl.BlockSpec((None, n_ct, CHUNK, SSM_GROUP, C_TILE), lambda i: (i, 0, 0, 0, 0)),
            pl.BlockSpec((epg, D_MODEL, D_EXPERT), lambda i: (i, 0, 0)),
            pl.BlockSpec((epg, D_MODEL, D_EXPERT), lambda i: (i, 0, 0)),
            pl.BlockSpec((epg, D_EXPERT, D_MODEL), lambda i: (i, 0, 0)),
        ],
        out_shape=[
            jax.ShapeDtypeStruct(ut.shape, BF16),
            jax.ShapeDtypeStruct((N_EXPERTS, D_MODEL, D_EXPERT), BF16),
            jax.ShapeDtypeStruct((N_EXPERTS, D_MODEL, D_EXPERT), BF16),
            jax.ShapeDtypeStruct((N_EXPERTS, D_EXPERT, D_MODEL), BF16),
        ],
        compiler_params=_params(("parallel",)),
        name="ssm",
    )(ut, are, aim, ldt, bt(b_re), bt(b_im), ccr, ct(c_re), ct(c_im), d_col, exp_c,
      w_gate, w_up, w_down)


def _mix_out_kernel(x_ref, a_ref, yt_ref, gwt_ref, gb_ref, wo_ref, perm_ref, h_ref):
    nc = x_ref.shape[0]
    rows = nc * S_TILE
    y = jnp.concatenate([yt_ref[:, j, :, :].reshape(D_SSM, nc) for j in range(S_TILE)],
                        axis=1).astype(F32)
    z = 0.5 * y * (1.0 + jnp.tanh(math.sqrt(2.0 / math.pi) * (y + 0.044715 * (y * y * y))))
    gate = _sigmoid(_dot(gwt_ref[...], z.astype(BF16)) + gb_ref[...])
    s = (z * gate).astype(BF16)
    s = _dot_nt(s, perm_ref[...]).astype(BF16)
    h = (x_ref[...].reshape(rows, D_MODEL)
         + _dot(a_ref[...].reshape(rows, D_POOL), wo_ref[:D_POOL, :])
         + _dot_tn(s, wo_ref[D_POOL:, :]))
    h_ref[...] = h.reshape(h_ref.shape)


def _mix_out(x3, a3, yt, glu_w_t, glu_b_col, w_out, perm):
    nch = x3.shape[0]
    return pl.pallas_call(
        _mix_out_kernel,
        grid=(nch // C_TILE, CHUNK // S_TILE),
        in_specs=[
            pl.BlockSpec((C_TILE, S_TILE, D_MODEL), lambda c, t: (c, t, 0)),
            pl.BlockSpec((C_TILE, S_TILE, D_POOL), lambda c, t: (c, t, 0)),
            pl.BlockSpec((N_SSM_GROUPS, None, S_TILE, SSM_GROUP, C_TILE), lambda c, t: (0, c, t, 0, 0)),
            pl.BlockSpec((D_SSM, D_SSM), lambda c, t: (0, 0)),
            pl.BlockSpec((D_SSM, 1), lambda c, t: (0, 0)),
            pl.BlockSpec((D_MODEL, D_MODEL), lambda c, t: (0, 0)),
            pl.BlockSpec((C_TILE * S_TILE, C_TILE * S_TILE), lambda c, t: (0, 0)),
        ],
        out_specs=pl.BlockSpec((C_TILE, S_TILE, D_MODEL), lambda c, t: (c, t, 0)),
        out_shape=jax.ShapeDtypeStruct((nch, CHUNK, D_MODEL), F32),
        compiler_params=_params(("parallel", "parallel")),
        name="mix_out",
    )(x3, a3, yt, glu_w_t, glu_b_col, w_out, perm)


def _pack_rows(x):
    b = lax.bitcast_convert_type(x.astype(BF16).astype(F32), U32)
    return (b[:, :HALF] & jnp.uint32(0xFFFF0000)) | (b[:, HALF:] >> 16)


def _unpack_rows(w):
    lo = lax.bitcast_convert_type(w & jnp.uint32(0xFFFF0000), F32)
    hi = lax.bitcast_convert_type(w << 16, F32)
    return lo, hi


def _split_bf16(x):
    hi = x.astype(BF16)
    return hi, (x - hi.astype(F32)).astype(BF16)


def _route(v32, wr_ref, br_ref):
    v_hi, v_lo = _split_bf16(v32)
    w_hi, w_lo = _split_bf16(wr_ref[...])
    logits = _dot(v_hi, w_hi) + (_dot(v_lo, w_hi) + _dot(v_hi, w_lo)) + br_ref[...]
    return jnp.transpose(logits)


def _top1(x, valid=None):
    n = x.shape[0]
    row = lax.broadcasted_iota(jnp.int32, x.shape, 0).astype(F32)
    if valid is not None:
        x = jnp.where(valid, x, -jnp.inf)
    m = jnp.max(x, axis=0, keepdims=True)
    idx = jnp.min(jnp.where(x == m, row, float(n)), axis=0, keepdims=True)
    return m, idx, x, row


def _split_planes(packed, ref):
    ref[0] = packed[:, :SC_ROW]
    ref[1] = packed[:, SC_ROW:]


def _router_kernel(h_ref, g_ref, wr_ref, br_ref, before_ref, vp_ref, meta_t_ref, cnt_ref, carry_ref):
    @pl.when(pl.program_id(0) == 0)
    def _():
        carry_ref[...] = jnp.zeros_like(carry_ref)

    v32 = _rms(h_ref[...], g_ref[...])
    _split_planes(_pack_rows(v32), vp_ref)
    lt = _route(v32, wr_ref, br_ref)
    tm = lt.shape[1]
    eg = EXPERTS_PER_GROUP

    grp = lt[:eg]
    grp_row = lax.broadcasted_iota(jnp.int32, grp.shape, 0)
    mg, grp_idx, grp, _ = _top1(grp, grp_row < N_EXPERT_GROUPS)
    grp_p = 1.0 / jnp.sum(jnp.exp(grp - mg), axis=0, keepdims=True)
    le = jnp.zeros((eg, tm), F32)
    for g in range(N_EXPERT_GROUPS):
        le = jnp.where(grp_idx == float(g), lt[eg * (g + 1):eg * (g + 2)], le)
    m1, i1, le, row = _top1(le)
    z = jnp.sum(jnp.exp(le - m1), axis=0, keepdims=True)
    m2, i2, _, _ = _top1(jnp.where(row == i1, -jnp.inf, le))
    p1 = 1.0 / z
    p2 = jnp.exp(m2 - m1) / z
    tot = p1 + p2
    w1 = grp_p * (p1 / tot)
    w2 = grp_p * (p2 / tot)
    e1 = grp_idx * eg + i1
    e2 = grp_idx * eg + i2

    erow = lax.broadcasted_iota(jnp.int32, (N_EXPERTS, tm), 0).astype(F32)
    onehot = jnp.where(erow == e1, 1.0, jnp.where(erow == e2, 1.0, 0.0))
    before = _dot(onehot.astype(BF16), before_ref[...]) + carry_ref[...]
    rank1 = jnp.sum(jnp.where(erow == e1, before, 0.0), axis=0, keepdims=True)
    rank2 = jnp.sum(jnp.where(erow == e2, before, 0.0), axis=0, keepdims=True)
    carry = carry_ref[...] + jnp.sum(onehot, axis=1, keepdims=True)
    carry_ref[...] = carry
    cnt_ref[...] = carry

    mrow = lax.broadcasted_iota(jnp.int32, (META_ROWS, tm), 0)
    meta_t_ref[...] = jnp.where(mrow == 0, e1, jnp.where(mrow == 1, e2, jnp.where(
        mrow == 2, rank1, jnp.where(mrow == 3, rank2, jnp.where(
            mrow == 4, w1, jnp.where(mrow == 5, w2, 0.0))))))


def _earlier_matrix(tm):
    return np.triu(np.ones((tm, tm), np.float32), k=1)


def _router(h1, g_ffn, w_router, b_router, tm):
    t = h1.shape[0]
    return pl.pallas_call(
        _router_kernel,
        grid=(t // tm,),
        in_specs=[
            pl.BlockSpec((tm, D_MODEL), lambda i: (i, 0)),
            pl.BlockSpec((1, D_MODEL), lambda i: (0, 0)),
            pl.BlockSpec((D_MODEL, ROUTER_W), lambda i: (0, 0)),
            pl.BlockSpec((1, ROUTER_W), lambda i: (0, 0)),
            pl.BlockSpec((tm, tm), lambda i: (0, 0)),
        ],
        out_specs=[
            pl.BlockSpec((2, tm, SC_ROW), lambda i: (0, i, 0)),
            pl.BlockSpec((META_ROWS, tm), lambda i: (0, i)),
            pl.BlockSpec((N_EXPERTS, 1), lambda i: (0, 0)),
        ],
        out_shape=[
            jax.ShapeDtypeStruct((2, t, SC_ROW), U32),
            jax.ShapeDtypeStruct((META_ROWS, t), F32),
            jax.ShapeDtypeStruct((N_EXPERTS, 1), F32),
        ],
        scratch_shapes=[pltpu.VMEM((N_EXPERTS, 1), F32)],
        compiler_params=_params(("arbitrary",)),
        name="router",
    )(h1, g_ffn, w_router, b_router, jnp.asarray(_earlier_matrix(tm), BF16))


def _plan(meta_t, counts, n_tiles):
    e1 = meta_t[0].astype(jnp.int32)
    e2 = meta_t[1].astype(jnp.int32)
    rank1 = meta_t[2].astype(jnp.int32)
    rank2 = meta_t[3].astype(jnp.int32)
    cnt = counts[:, 0].astype(jnp.int32)
    padded = ((cnt + ROW_TILE - 1) // ROW_TILE) * ROW_TILE
    ends = jnp.cumsum(padded)
    starts = ends - padded
    experts = jnp.arange(N_EXPERTS, dtype=jnp.int32)
    pos1 = rank1 + jnp.sum(jnp.where(e1[None, :] == experts[:, None], starts[:, None], 0), axis=0)
    pos2 = rank2 + jnp.sum(jnp.where(e2[None, :] == experts[:, None], starts[:, None], 0), axis=0)
    tile_start = jnp.arange(n_tiles, dtype=jnp.int32) * ROW_TILE
    tile_expert = jnp.sum((tile_start[:, None] >= ends[None, :]).astype(jnp.int32), axis=1)
    tile_expert = jnp.minimum(tile_expert, N_EXPERTS - 1)
    rows_left = jnp.sum(jnp.where(tile_expert[:, None] == experts, cnt + starts, 0), axis=1) - tile_start
    n_valid = jnp.clip(rows_left, 0, ROW_TILE).astype(jnp.int32)
    last_used = jnp.maximum(ends[-1] // ROW_TILE - 1, 0)
    block = jnp.minimum(jnp.arange(n_tiles, dtype=jnp.int32), last_used)
    tile_expert = jnp.sum(jnp.where(block[:, None] == jnp.arange(n_tiles)[None, :],
                                    tile_expert[None, :], 0), axis=1)
    plane = n_tiles * ROW_TILE
    half_rows = jnp.concatenate([pos1, pos1 + plane, pos2, pos2 + plane])[None]
    return half_rows, tile_expert, n_valid, block


def _sc_mesh():
    return plsc.VectorSubcoreMesh(core_axis_name="c", subcore_axis_name="s")


def _sc_scatter_rows(rows, idx, n_out):
    t, width = rows.shape
    steps = t // SC_WINDOW

    @pl.kernel(out_type=jax.ShapeDtypeStruct((n_out, width), rows.dtype), mesh=_sc_mesh(),
               scratch_types=[], name="moe_scatter")
    def scatter(rows_hbm, idx_hbm, out_hbm):
        def body(rows_vmem, idx_vmem):
            pltpu.sync_copy(rows_vmem, out_hbm.at[idx_vmem.at[0]])

        pltpu.emit_pipeline(
            body,
            grid=(2, steps),
            in_specs=[pl.BlockSpec((SC_WINDOW, width), lambda k, j: (j, 0)),
                      pl.BlockSpec((1, SC_WINDOW), lambda k, j: (0, k * steps + j))],
            out_specs=[],
            core_axis_name=("c", "s"),
            dimension_semantics=(pltpu.PARALLEL, pltpu.PARALLEL),
        )(rows_hbm, idx_hbm)

    return scatter(rows, idx)


def _sc_gather_rows(table, idx):
    m = idx.shape[1]
    width = table.shape[1]
    steps = m // (2 * SC_WINDOW)

    @pl.kernel(out_type=jax.ShapeDtypeStruct((m, width), table.dtype), mesh=_sc_mesh(),
               scratch_types=[], name="moe_gather")
    def gather(table_hbm, idx_hbm, out_hbm):
        def body(idx_vmem, out_vmem):
            pltpu.sync_copy(table_hbm.at[idx_vmem.at[0]], out_vmem)

        pltpu.emit_pipeline(
            body,
            grid=(2, steps),
            in_specs=[pl.BlockSpec((1, SC_WINDOW), lambda k, j: (0, k * steps + j))],
            out_specs=[pl.BlockSpec((SC_WINDOW, width), lambda k, j: (k * steps + j, 0))],
            core_axis_name=("c", "s"),
            dimension_semantics=(pltpu.PARALLEL, pltpu.PARALLEL),
        )(idx_hbm, out_hbm)

    return gather(table, idx)


def _experts_kernel(te_ref, nv_ref, blk_ref, xs_ref, wg_ref, wu_ref, wd_ref, ys_ref):
    r = pl.program_id(0)
    n_valid = nv_ref[r]

    @pl.when(n_valid > 0)
    def _():
        parts = [p.astype(BF16) for p in _unpack_rows(xs_ref[0]) + _unpack_rows(xs_ref[1])]
        cols = (0, 2 * SC_ROW, SC_ROW, 3 * SC_ROW)
        hg = sum(_dot(p, wg_ref[c:c + SC_ROW, :]) for p, c in zip(parts, cols))
        hu = sum(_dot(p, wu_ref[c:c + SC_ROW, :]) for p, c in zip(parts, cols))
        row = lax.broadcasted_iota(jnp.int32, (ROW_TILE, 1), 0)
        hid = jnp.where(row < n_valid, hg * _sigmoid(hg) * hu, 0.0).astype(BF16)
        _split_planes(_pack_rows(_dot(hid, wd_ref[...])), ys_ref)


def _experts(xs, tile_expert, n_valid, block, w_gate, w_up, w_down):
    n_tiles = xs.shape[1] // ROW_TILE
    w_spec = pl.BlockSpec((None, D_MODEL, D_EXPERT), lambda r, te, nv, blk: (te[r], 0, 0))
    grid_spec = pltpu.PrefetchScalarGridSpec(
        num_scalar_prefetch=3,
        grid=(n_tiles,),
        in_specs=[
            pl.BlockSpec((2, ROW_TILE, SC_ROW), lambda r, te, nv, blk: (0, blk[r], 0)),
            w_spec, w_spec,
            pl.BlockSpec((None, D_EXPERT, D_MODEL), lambda r, te, nv, blk: (te[r], 0, 0)),
        ],
        out_specs=pl.BlockSpec((2, ROW_TILE, SC_ROW), lambda r, te, nv, blk: (0, blk[r], 0)),
    )
    return pl.pallas_call(
        _experts_kernel,
        grid_spec=grid_spec,
        out_shape=jax.ShapeDtypeStruct((2, n_tiles * ROW_TILE, SC_ROW), U32),
        compiler_params=_params(("arbitrary",)),
        name="experts",
    )(tile_expert, n_valid, block, xs, w_gate, w_up, w_down)


def _ple_kernel(h_ref, yg_ref, meta_t_ref, p_ref, wg_ref, bg_ref, wp_ref, gf_ref, *rest, final_norm):
    o_ref = rest[-1]
    meta = jnp.transpose(meta_t_ref[...])
    w1 = meta[:, 4:5]
    w2 = meta[:, 5:6]
    q0, q2 = (w1 * u + w2 * v for u, v in zip(_unpack_rows(yg_ref[0]), _unpack_rows(yg_ref[2])))
    q1, q3 = (w1 * u + w2 * v for u, v in zip(_unpack_rows(yg_ref[1]), _unpack_rows(yg_ref[3])))
    moe = jnp.concatenate([q0, q1, q2, q3], axis=1)
    h = h_ref[...] + moe
    gate = _sigmoid(_dot(_rms(h).astype(BF16), wg_ref[...]) + bg_ref[...])
    h = h + gate * _dot(p_ref[...].astype(BF16), wp_ref[...])
    o_ref[...] = _rms(h, gf_ref[...]) if final_norm else h


def _ple(h1, yg, meta_t, p2, w_gate, b_gate, w_proj, g_final, final_norm, tm, part, prev_out):
    t = h1.shape[0]
    steps = t // tm // GATHER_PARTS
    off = part * steps
    in_specs = [
        pl.BlockSpec((tm, D_MODEL), lambda i: (i + off, 0)),
        pl.BlockSpec((4, tm, SC_ROW), lambda i: (0, i, 0)),
        pl.BlockSpec((META_ROWS, tm), lambda i: (0, i + off)),
        pl.BlockSpec((tm, D_PLE), lambda i: (i + off, 0)),
        pl.BlockSpec((D_MODEL, D_MODEL), lambda i: (0, 0)),
        pl.BlockSpec((1, D_MODEL), lambda i: (0, 0)),
        pl.BlockSpec((D_PLE, D_MODEL), lambda i: (0, 0)),
        pl.BlockSpec((1, D_MODEL), lambda i: (0, 0)),
    ]
    args = [h1, yg, meta_t, p2, w_gate, b_gate, w_proj, g_final]
    aliases = {}
    if prev_out is not None:
        in_specs.append(pl.BlockSpec(memory_space=pl.ANY))
        args.append(prev_out)
        aliases = {len(args) - 1: 0}
    return pl.pallas_call(
        functools.partial(_ple_kernel, final_norm=final_norm),
        grid=(steps,),
        in_specs=in_specs,
        out_specs=pl.BlockSpec((tm, D_MODEL), lambda i: (i + off, 0)),
        out_shape=jax.ShapeDtypeStruct((t, D_MODEL), F32),
        input_output_aliases=aliases,
        compiler_params=_params(("parallel",)),
        name="ple",
    )(*args)


def kernel(x, p, g_mix, w_in, pool_w, pool_scale, ssm_a_re, ssm_a_im, ssm_log_dt, ssm_b_re,
           ssm_b_im, ssm_c_re, ssm_c_im, ssm_d, glu_w, glu_b, w_out, g_ffn, router_grp_w,
           router_grp_b, router_exp_w, router_exp_b, exp_w_gate, exp_w_up, exp_w_down, g_ple,
           ple_gate_w, ple_gate_b, ple_proj_w, g_final):
    bsz, seq, dm = x.shape
    depth = g_mix.shape[0]
    t = bsz * seq
    seq_chunks = seq // CHUNK
    nch = t // CHUNK
    tm = 1024
    n_sorted = 2 * t + N_EXPERTS * ROW_TILE
    w_gate_all = exp_w_gate.reshape(depth * N_EXPERTS, dm, D_EXPERT)
    w_up_all = exp_w_up.reshape(depth * N_EXPERTS, dm, D_EXPERT)
    w_down_all = exp_w_down.reshape(depth * N_EXPERTS, D_EXPERT, dm)

    perm = jnp.asarray(_regroup_matrix(), BF16)
    h = x.reshape(t, dm)
    for i in range(depth):
        w_in_b = (g_mix[i][:, None] * w_in[i]).astype(BF16)
        zp, ut = _in_proj(h.reshape(nch, CHUNK, dm), w_in_b[:, :D_POOL],
                          jnp.transpose(w_in_b[:, D_POOL:]), perm)
        a = _pool(zp.reshape(bsz, seq, D_POOL), pool_w[i], pool_scale[i][None])
        yt, w_gate_b, w_up_b, w_down_b = _ssm(
            ut, ssm_a_re[i], ssm_a_im[i], ssm_log_dt[i], ssm_b_re[i], ssm_b_im[i], ssm_c_re[i],
            ssm_c_im[i], ssm_d[i], seq_chunks, w_gate_all, w_up_all, w_down_all, i)
        h = _mix_out(h.reshape(nch, CHUNK, dm), a.reshape(nch, CHUNK, D_POOL), yt,
                     jnp.transpose(glu_w[i]).astype(BF16), glu_b[i][:, None],
                     w_out[i].astype(BF16), perm).reshape(t, dm)

        eg = EXPERTS_PER_GROUP
        w_router = jnp.concatenate(
            [router_grp_w[i], jnp.zeros((dm, eg - N_EXPERT_GROUPS), F32),
             jnp.transpose(router_exp_w[i], (1, 0, 2)).reshape(dm, N_EXPERTS),
             jnp.zeros((dm, ROUTER_W - eg - N_EXPERTS), F32)], axis=1)
        b_router = jnp.concatenate(
            [router_grp_b[i], jnp.zeros((eg - N_EXPERT_GROUPS,), F32),
             router_exp_b[i].reshape(N_EXPERTS),
             jnp.zeros((ROUTER_W - eg - N_EXPERTS,), F32)])[None]
        vp, meta_t, counts = _router(h, g_ffn[i][None], w_router, b_router, tm)
        idx, tile_expert, n_valid, block = _plan(meta_t, counts, n_sorted // ROW_TILE)
        xs = _sc_scatter_rows(vp.reshape(2 * t, SC_ROW), idx, 2 * n_sorted)
        ys = _experts(xs.reshape(2, n_sorted, SC_ROW), tile_expert, n_valid, block,
                      w_gate_b, w_up_b, w_down_b)
        ys2 = ys.reshape(2 * n_sorted, SC_ROW)
        idx4 = idx.reshape(4, t)
        tp = t // GATHER_PARTS
        ple_wg = (g_ple[i][:, None] * ple_gate_w[i]).astype(BF16)
        ple_wp = ple_proj_w[i].astype(BF16)
        out = None
        for q in range(GATHER_PARTS):
            yg_q = _sc_gather_rows(ys2, idx4[:, q * tp:(q + 1) * tp].reshape(1, 4 * tp))
            out = _ple(h, yg_q.reshape(4, tp, SC_ROW), meta_t, p[i].reshape(t, D_PLE), ple_wg,
                       ple_gate_b[i][None], ple_wp, g_final[None], i == depth - 1, tm, q, out)
        h = out
    return h.reshape(bsz, seq, dm)
```

```python
import functools
import math

import numpy as np
import jax
import jax.numpy as jnp
from jax import lax
from jax.experimental import pallas as pl
from jax.experimental.pallas import tpu as pltpu
from jax.experimental.pallas import tpu_sc as plsc

F32 = jnp.float32
BF16 = jnp.bfloat16
U32 = jnp.uint32

D_MODEL = 1024
D_POOL = 512
D_SSM = 512
POOL_WINDOWS = (2, 4, 8, 16)
POOL_GROUP = 128
SSM_GROUP = 16
N_SSM_GROUPS = 32
SSM_STATE = 64
N_EXPERT_GROUPS = 4
EXPERTS_PER_GROUP = 8
N_EXPERTS = N_EXPERT_GROUPS * EXPERTS_PER_GROUP
D_EXPERT = 256
D_PLE = 256
RMS_EPS = 1e-6

LANES = 128
CHUNK = 32
CHUNK_W = CHUNK * SSM_GROUP
T_SUB = 8
S_TILE = 8
C_TILE = 128
ROUTER_W = LANES
HALF = D_MODEL // 2
ROW_TILE = 1024
SC_WINDOW = 128
SC_ROW = HALF // 2
META_ROWS = 8
GATHER_PARTS = 4
VMEM_LIMIT = 56 * 1024 * 1024


def _dot(a, b):
    return jnp.dot(a, b, preferred_element_type=F32)


def _dot_nt(a, b):
    return lax.dot_general(a, b, (((1,), (1,)), ((), ())), preferred_element_type=F32)


def _dot_tn(a, b):
    return lax.dot_general(a, b, (((0,), (0,)), ((), ())), preferred_element_type=F32)


def _rms(x, g=None):
    y = x * lax.rsqrt(jnp.mean(x * x, axis=-1, keepdims=True) + RMS_EPS)
    return y if g is None else y * g


def _sigmoid(x):
    return 1.0 / (1.0 + jnp.exp(-x))


def _params(sem):
    return pltpu.CompilerParams(dimension_semantics=sem, vmem_limit_bytes=VMEM_LIMIT)


def _tile_step():
    n_s = pl.num_programs(1)
    return pl.program_id(0) * n_s + pl.program_id(1), pl.num_programs(0) * n_s


def _row_copies(hbm, buf, sem, step_idx, slot_idx, to_hbm):
    n_s = pl.num_programs(1)
    c0 = (step_idx // n_s) * C_TILE
    s0 = (step_idx % n_s) * S_TILE
    out = []
    for j in range(S_TILE):
        far, near = hbm.at[pl.ds(c0, C_TILE), s0 + j, :], buf.at[slot_idx, j]
        src, dst = (near, far) if to_hbm else (far, near)
        out.append(pltpu.make_async_copy(src, dst, sem.at[slot_idx]))
    return out


def _load_time_major_rows(hbm, buf, sem):
    step, n_steps = _tile_step()
    slot = step % 2

    @pl.when(step == 0)
    def _():
        for cp in _row_copies(hbm, buf, sem, step, slot, False):
            cp.start()

    @pl.when(step + 1 < n_steps)
    def _():
        for cp in _row_copies(hbm, buf, sem, step + 1, 1 - slot, False):
            cp.start()

    for cp in _row_copies(hbm, buf, sem, step, slot, False):
        cp.wait()
    return buf[slot].reshape(S_TILE * C_TILE, buf.shape[-1])


def _store_time_major_rows(val, hbm, buf, sem):
    step, n_steps = _tile_step()
    slot = step % 2

    @pl.when(step >= 2)
    def _():
        for cp in _row_copies(hbm, buf, sem, step - 2, slot, True):
            cp.wait()

    buf[slot] = val.reshape(S_TILE, C_TILE, val.shape[-1])
    for cp in _row_copies(hbm, buf, sem, step, slot, True):
        cp.start()

    @pl.when(step == n_steps - 1)
    def _():
        @pl.when(step >= 1)
        def _():
            for cp in _row_copies(hbm, buf, sem, step - 1, 1 - slot, True):
                cp.wait()
        for cp in _row_copies(hbm, buf, sem, step, slot, True):
            cp.wait()


def _in_proj_kernel(x_hbm, wp_ref, wst_ref, zp_hbm, ut_ref, xbuf, zbuf, xsem, zsem):
    nc = C_TILE
    u = _rms(_load_time_major_rows(x_hbm, xbuf, xsem)).astype(BF16)
    zt = _dot_nt(wst_ref[...], u).astype(BF16)
    for j in range(S_TILE):
        ut_ref[:, j, :, :] = zt[:, j * nc:(j + 1) * nc].reshape(N_SSM_GROUPS, SSM_GROUP, nc)
    _store_time_major_rows(_dot(u, wp_ref[...]), zp_hbm, zbuf, zsem)


def _in_proj(x3, w_pool, w_ssm_t):
    nch = x3.shape[0]
    return pl.pallas_call(
        _in_proj_kernel,
        grid=(nch // C_TILE, CHUNK // S_TILE),
        in_specs=[
            pl.BlockSpec(memory_space=pl.ANY),
            pl.BlockSpec((D_MODEL, D_POOL), lambda c, s: (0, 0)),
            pl.BlockSpec((D_SSM, D_MODEL), lambda c, s: (0, 0)),
        ],
        out_specs=[
            pl.BlockSpec(memory_space=pl.ANY),
            pl.BlockSpec((N_SSM_GROUPS, None, S_TILE, SSM_GROUP, C_TILE), lambda c, s: (0, c, s, 0, 0)),
        ],
        out_shape=[
            jax.ShapeDtypeStruct((nch, CHUNK, D_POOL), F32),
            jax.ShapeDtypeStruct((N_SSM_GROUPS, nch // C_TILE, CHUNK, SSM_GROUP, C_TILE), BF16),
        ],
        scratch_shapes=[
            pltpu.VMEM((2, S_TILE, C_TILE, D_MODEL), F32),
            pltpu.VMEM((2, S_TILE, C_TILE, D_POOL), F32),
            pltpu.SemaphoreType.DMA((2,)),
            pltpu.SemaphoreType.DMA((2,)),
        ],
        compiler_params=_params(("arbitrary", "arbitrary")),
        name="in_proj",
    )(x3, w_pool, w_ssm_t)


def _shift_rows(x, d, row, n):
    if d == 0:
        return x
    r = pltpu.roll(x, d % n, 0)
    if d > 0:
        return jnp.where(row >= d, r, 0.0)
    return jnp.where(row < n + d, r, 0.0)


def _pool_kernel(z_ref, w_ref, sc_ref, o_ref):
    n = z_ref.shape[0]
    gi = pl.program_id(1)
    row = lax.broadcasted_iota(jnp.int32, (n, 1), 0)

    for k, w in enumerate(POOL_WINDOWS):
        @pl.when(gi == k)
        def _(w=w):
            x = z_ref[...]
            half = w // 2
            pd, pu, span = x, x, 1
            while span < half:
                pd = pd + _shift_rows(pd, span, row, n)
                pu = pu + _shift_rows(pu, -span, row, n)
                span *= 2
            total = _shift_rows(pd, 1, row, n) + pu
            lo = jnp.maximum(row - half, 0)
            hi = jnp.minimum(row + half, n)
            cnt = (hi - lo).astype(F32)
            diff = (total / cnt - x).astype(BF16)
            o_ref[...] = _dot(diff, w_ref[...].astype(BF16)) * sc_ref[...]


def _pool(zp3, pool_w, pool_scale):
    b, s, _ = zp3.shape
    return pl.pallas_call(
        _pool_kernel,
        grid=(b, len(POOL_WINDOWS)),
        in_specs=[
            pl.BlockSpec((None, s, POOL_GROUP), lambda i, g: (i, 0, g)),
            pl.BlockSpec((None, POOL_GROUP, POOL_GROUP), lambda i, g: (g, 0, 0)),
            pl.BlockSpec((1, POOL_GROUP), lambda i, g: (0, g)),
        ],
        out_specs=pl.BlockSpec((None, s, POOL_GROUP), lambda i, g: (i, 0, g)),
        out_shape=jax.ShapeDtypeStruct((b, s, D_POOL), F32),
        compiler_params=_params(("parallel", "parallel")),
        name="pool",
    )(zp3, pool_w, pool_scale)


def _expand_consts():
    time = np.arange(CHUNK_W) // SSM_GROUP
    def onehot(e):
        m = np.zeros((CHUNK_W, LANES), np.float32)
        m[np.arange(CHUNK_W), e] = 1.0
        return m
    return np.stack([
        onehot(CHUNK - 1 - time),
        onehot(time),
        onehot(time + 1),
        onehot(CHUNK - time),
    ])


def _cmul_packed(x, p, q):
    return x * p + pltpu.roll(x, LANES // 2, 1) * q


def _ssm_kernel(u_ref, are_ref, aim_ref, ldt_ref, bt_re_ref, bt_im_ref, ccr_ref, ct_re_ref,
                ct_im_ref, d_ref, exp_ref, wg_ref, wu_ref, wd_ref, y_ref, og_ref, ou_ref, od_ref, *,
                seq_chunks):
    og_ref[...] = wg_ref[...].astype(BF16)
    ou_ref[...] = wu_ref[...].astype(BF16)
    od_ref[...] = wd_ref[...].astype(BF16)

    n_ct = u_ref.shape[0]
    nch = n_ct * C_TILE
    half = LANES // 2
    lane = lax.broadcasted_iota(jnp.int32, (1, LANES), 1)
    lo_half = lane < half

    def direction(di):
        a_re = are_ref[di]
        a_im = aim_ref[di]
        dt = jnp.exp(ldt_ref[di])
        mag = jnp.exp(a_re * dt)
        ang = a_im * dt
        lam = jnp.where(lo_half, mag * jnp.cos(ang), mag * jnp.sin(ang))
        lb_re = mag * jnp.cos(ang)
        lb_im = mag * jnp.sin(ang)
        den = a_re * a_re + a_im * a_im
        f_re = ((lb_re - 1.0) * a_re + lb_im * a_im) / den
        f_im = (lb_im * a_re - (lb_re - 1.0) * a_im) / den
        return lam, f_re, f_im

    def power_table(lam):
        e = lax.broadcasted_iota(jnp.int32, (LANES, 1), 0)
        tab = jnp.where(lo_half, 1.0, 0.0) * jnp.ones((LANES, 1), F32)
        sq = lam
        for k in range(7):
            p = jnp.where(lo_half, sq, pltpu.roll(sq, half, 1))
            q = jnp.where(lo_half, -pltpu.roll(sq, half, 1), sq)
            tab = jnp.where(((e >> k) & 1) == 1, _cmul_packed(tab, p, q), tab)
            sq = _cmul_packed(sq, p, q)
        return tab

    def tile_rows(x16):
        return jnp.broadcast_to(x16[None], (CHUNK, SSM_GROUP, LANES)).reshape(CHUNK_W, LANES)

    def expanded(tab, which, v_re, v_im, conj_sign):
        lexp = _dot(exp_ref[which], tab.astype(BF16))
        if conj_sign > 0:
            p = jnp.where(lo_half, v_re, v_re)
            q = jnp.where(lo_half, -v_im, v_im)
        else:
            p = jnp.where(lo_half, v_re, -v_re)
            q = jnp.where(lo_half, -v_im, -v_im)
        return lexp * tile_rows(p) + pltpu.roll(lexp, half, 1) * tile_rows(q)

    lam_f, ff_re, ff_im = direction(0)
    lam_b, fb_re, fb_im = direction(1)
    tab_f = power_table(lam_f)
    tab_b = power_table(lam_b)

    def bbar(bt_re, bt_im, f_re, f_im):
        return bt_re * f_re - bt_im * f_im, bt_re * f_im + bt_im * f_re

    bf_re, bf_im = bbar(bt_re_ref[0], bt_im_ref[0], ff_re, ff_im)
    bb_re, bb_im = bbar(bt_re_ref[1], bt_im_ref[1], fb_re, fb_im)

    pb1 = expanded(tab_f, 0, bf_re, bf_im, 1)
    pb2 = expanded(tab_b, 1, bb_re, bb_im, 1)
    pb3 = expanded(tab_b, 2, bb_re, bb_im, 1)
    ft_f = expanded(tab_f, 2, ct_re_ref[0], ct_im_ref[0], -1)
    ft_b = expanded(tab_b, 3, ct_re_ref[1], ct_im_ref[1], -1)

    row_w = lax.broadcasted_iota(jnp.int32, (CHUNK_W, 1), 0)
    last_blk = row_w >= CHUNK_W - SSM_GROUP
    pb2_lag0 = jnp.where(last_blk, pltpu.roll(pb2, CHUNK_W - SSM_GROUP, 0), 0.0)
    ccr_f = ccr_ref[0].astype(BF16)
    ccr_b = ccr_ref[1].astype(BF16)
    r_lo = _dot_nt(ccr_f, pb1.astype(BF16)) + _dot_nt(ccr_b, pb2_lag0.astype(BF16))
    co = lax.broadcasted_iota(jnp.int32, (SSM_GROUP, CHUNK_W), 0)
    col = lax.broadcasted_iota(jnp.int32, (SSM_GROUP, CHUNK_W), 1)
    r_lo = r_lo + jnp.where(col == CHUNK_W - SSM_GROUP + co, d_ref[...], 0.0)
    r_hi = _dot_nt(ccr_b, pb3.astype(BF16))
    r_t = jnp.concatenate([r_lo, r_hi], axis=1)
    g_t = jnp.concatenate(
        [pltpu.roll(r_t, SSM_GROUP * (tl + 1), 1) for tl in range(T_SUB)], axis=0
    ).astype(BF16)

    u = jnp.concatenate([u_ref[ct].reshape(CHUNK_W, C_TILE) for ct in range(n_ct)],
                        axis=1)
    e_mat = jnp.concatenate([pb1, pb2], axis=1).astype(BF16)
    xend = _dot_tn(e_mat, u)
    lanec = lax.broadcasted_iota(jnp.int32, (1, nch), 1) % seq_chunks
    ns = SSM_STATE

    def scan(re, im, tab, forward):
        lam_col = jnp.transpose(tab[CHUNK:CHUNK + 8, :])[:, 0:1]
        a, b = lam_col[:ns], lam_col[ns:]
        n_steps = int(math.log2(seq_chunks))

        def shifted(v, d):
            if forward:
                return jnp.where(lanec >= d, pltpu.roll(v, d, 1), 0.0)
            return jnp.where(lanec < seq_chunks - d, pltpu.roll(v, nch - d, 1), 0.0)

        for k in range(n_steps):
            sr, si = shifted(re, 1 << k), shifted(im, 1 << k)
            re, im = re + (sr * a - si * b), im + (sr * b + si * a)
            a, b = a * a - b * b, 2.0 * a * b
        return shifted(re, 1), shifted(im, 1)

    f_re, f_im = scan(xend[:ns], xend[ns:2 * ns], tab_f, True)
    b_re, b_im = scan(xend[2 * ns:3 * ns], xend[3 * ns:], tab_b, False)
    xin = jnp.concatenate([f_re, f_im, b_re, b_im], axis=0).astype(BF16)
    f_t = jnp.concatenate([ft_f, ft_b], axis=1).astype(BF16)

    toeplitz = jnp.concatenate(
        [g_t[:, CHUNK_W - LANES * th:2 * CHUNK_W - LANES * th] for th in range(CHUNK // T_SUB)],
        axis=0)
    y_t = _dot(toeplitz, u) + _dot(f_t, xin)
    for ct in range(n_ct):
        y_ref[ct] = y_t[:, ct * C_TILE:(ct + 1) * C_TILE].reshape(
            CHUNK, SSM_GROUP, C_TILE).astype(y_ref.dtype)


def _ssm(ut, a_re, a_im, log_dt, b_re, b_im, c_re, c_im, d, seq_chunks, w_gate, w_up, w_down, layer):
    g, n_ct = ut.shape[:2]
    assert N_EXPERTS % g == 0
    epg = N_EXPERTS // g
    base = layer * g
    n = SSM_STATE

    def lane_vec(a):
        a = jnp.transpose(a, (1, 0, 2))
        return jnp.concatenate([a, a], axis=-1)[:, :, None, :]

    are = lane_vec(a_re)
    aim = lane_vec(a_im)
    ldt = lane_vec(jnp.broadcast_to(log_dt[..., None], (2, g, n)))

    def bt(b):
        b = jnp.transpose(b, (1, 0, 3, 2))
        return jnp.concatenate([b, b], axis=-1)

    def ct(c):
        c = jnp.transpose(c, (1, 0, 2, 3))
        return jnp.concatenate([c, c], axis=-1)

    ccr = jnp.concatenate([jnp.transpose(c_re, (1, 0, 2, 3)),
                           -jnp.transpose(c_im, (1, 0, 2, 3))], axis=-1)
    d_col = d.reshape(g, SSM_GROUP, 1)
    exp_c = jnp.asarray(_expand_consts(), BF16)

    vec_spec = pl.BlockSpec((None, 2, 1, LANES), lambda i: (i, 0, 0, 0))
    mat_spec = pl.BlockSpec((None, 2, SSM_GROUP, LANES), lambda i: (i, 0, 0, 0))
    return pl.pallas_call(
        functools.partial(_ssm_kernel, seq_chunks=seq_chunks),
        grid=(g,),
        in_specs=[
            pl.BlockSpec((None, n_ct, CHUNK, SSM_GROUP, C_TILE), lambda i: (i, 0, 0, 0, 0)),
            vec_spec, vec_spec, vec_spec,
            mat_spec, mat_spec, mat_spec, mat_spec, mat_spec,
            pl.BlockSpec((None, SSM_GROUP, 1), lambda i: (i, 0, 0)),
            pl.BlockSpec((4, CHUNK_W, LANES), lambda i: (0, 0, 0)),
            pl.BlockSpec((epg, D_MODEL, D_EXPERT), lambda i: (base + i, 0, 0)),
            pl.BlockSpec((epg, D_MODEL, D_EXPERT), lambda i: (base + i, 0, 0)),
            pl.BlockSpec((epg, D_EXPERT, D_MODEL), lambda i: (base + i, 0, 0)),
        ],
        out_specs=[
            pl.BlockSpec((None, n_ct, CHUNK, SSM_GROUP, C_TILE), lambda i: (i, 0, 0, 0, 0)),
            pl.BlockSpec((epg, D_MODEL, D_EXPERT), lambda i: (i, 0, 0)),
            pl.BlockSpec((epg, D_MODEL, D_EXPERT), lambda i: (i, 0, 0)),
            pl.BlockSpec((epg, D_EXPERT, D_MODEL), lambda i: (i, 0, 0)),
        ],
        out_shape=[
            jax.ShapeDtypeStruct(ut.shape, BF16),
            jax.ShapeDtypeStruct((N_EXPERTS, D_MODEL, D_EXPERT), BF16),
            jax.ShapeDtypeStruct((N_EXPERTS, D_MODEL, D_EXPERT), BF16),
            jax.ShapeDtypeStruct((N_EXPERTS, D_EXPERT, D_MODEL), BF16),
        ],
        compiler_params=_params(("parallel",)),
        name="ssm",
    )(ut, are, aim, ldt, bt(b_re), bt(b_im), ccr, ct(c_re), ct(c_im), d_col, exp_c,
      w_gate, w_up, w_down)


def _mix_out_kernel(x_hbm, a_hbm, yt_ref, gwt_ref, gb_ref, wo_ref, h_hbm, xbuf, abuf, hbuf,
                    xsem, asem, hsem):
    nc = C_TILE
    x = _load_time_major_rows(x_hbm, xbuf, xsem)
    a = _load_time_major_rows(a_hbm, abuf, asem).astype(BF16)
    y = jnp.concatenate([yt_ref[:, j, :, :].reshape(D_SSM, nc) for j in range(S_TILE)],
                        axis=1).astype(F32)
    z = 0.5 * y * (1.0 + jnp.tanh(math.sqrt(2.0 / math.pi) * (y + 0.044715 * (y * y * y))))
    gate = _sigmoid(_dot(gwt_ref[...], z.astype(BF16)) + gb_ref[...])
    s = (z * gate).astype(BF16)
    h = x + _dot(a, wo_ref[:D_POOL, :]) + _dot_tn(s, wo_ref[D_POOL:, :])
    _store_time_major_rows(h, h_hbm, hbuf, hsem)


def _mix_out(x3, a3, yt, glu_w_t, glu_b_col, w_out):
    nch = x3.shape[0]
    return pl.pallas_call(
        _mix_out_kernel,
        grid=(nch // C_TILE, CHUNK // S_TILE),
        in_specs=[
            pl.BlockSpec(memory_space=pl.ANY),
            pl.BlockSpec(memory_space=pl.ANY),
            pl.BlockSpec((N_SSM_GROUPS, None, S_TILE, SSM_GROUP, C_TILE), lambda c, t: (0, c, t, 0, 0)),
            pl.BlockSpec((D_SSM, D_SSM), lambda c, t: (0, 0)),
            pl.BlockSpec((D_SSM, 1), lambda c, t: (0, 0)),
            pl.BlockSpec((D_MODEL, D_MODEL), lambda c, t: (0, 0)),
        ],
        out_specs=pl.BlockSpec(memory_space=pl.ANY),
        out_shape=jax.ShapeDtypeStruct((nch, CHUNK, D_MODEL), F32),
        scratch_shapes=[
            pltpu.VMEM((2, S_TILE, C_TILE, D_MODEL), F32),
            pltpu.VMEM((2, S_TILE, C_TILE, D_POOL), F32),
            pltpu.VMEM((2, S_TILE, C_TILE, D_MODEL), F32),
            pltpu.SemaphoreType.DMA((2,)),
            pltpu.SemaphoreType.DMA((2,)),
            pltpu.SemaphoreType.DMA((2,)),
        ],
        compiler_params=_params(("arbitrary", "arbitrary")),
        name="mix_out",
    )(x3, a3, yt, glu_w_t, glu_b_col, w_out)


def _pack_rows(x):
    b = lax.bitcast_convert_type(x.astype(BF16).astype(F32), U32)
    return (b[:, :HALF] & jnp.uint32(0xFFFF0000)) | (b[:, HALF:] >> 16)


def _unpack_rows(w):
    lo = lax.bitcast_convert_type(w & jnp.uint32(0xFFFF0000), F32)
    hi = lax.bitcast_convert_type(w << 16, F32)
    return lo, hi


def _split_bf16(x):
    hi = x.astype(BF16)
    return hi, (x - hi.astype(F32)).astype(BF16)


def _route(v32, wr_ref, br_ref):
    v_hi, v_lo = _split_bf16(v32)
    w_hi, w_lo = _split_bf16(wr_ref[...])
    logits = _dot(v_hi, w_hi) + (_dot(v_lo, w_hi) + _dot(v_hi, w_lo)) + br_ref[...]
    return jnp.transpose(logits)


def _top1(x, valid=None):
    n = x.shape[0]
    row = lax.broadcasted_iota(jnp.int32, x.shape, 0).astype(F32)
    if valid is not None:
        x = jnp.where(valid, x, -jnp.inf)
    m = jnp.max(x, axis=0, keepdims=True)
    idx = jnp.min(jnp.where(x == m, row, float(n)), axis=0, keepdims=True)
    return m, idx, x, row


def _split_planes(packed, ref):
    ref[0] = packed[:, :SC_ROW]
    ref[1] = packed[:, SC_ROW:]


def _router_kernel(h_ref, g_ref, wr_ref, br_ref, before_ref, vp_ref, meta_t_ref, cnt_ref, carry_ref):
    @pl.when(pl.program_id(0) == 0)
    def _():
        carry_ref[...] = jnp.zeros_like(carry_ref)

    v32 = _rms(h_ref[...], g_ref[...])
    _split_planes(_pack_rows(v32), vp_ref)
    lt = _route(v32, wr_ref, br_ref)
    tm = lt.shape[1]
    eg = EXPERTS_PER_GROUP

    grp = lt[:eg]
    grp_row = lax.broadcasted_iota(jnp.int32, grp.shape, 0)
    mg, grp_idx, grp, _ = _top1(grp, grp_row < N_EXPERT_GROUPS)
    grp_p = 1.0 / jnp.sum(jnp.exp(grp - mg), axis=0, keepdims=True)
    le = jnp.zeros((eg, tm), F32)
    for g in range(N_EXPERT_GROUPS):
        le = jnp.where(grp_idx == float(g), lt[eg * (g + 1):eg * (g + 2)], le)
    m1, i1, le, row = _top1(le)
    z = jnp.sum(jnp.exp(le - m1), axis=0, keepdims=True)
    m2, i2, _, _ = _top1(jnp.where(row == i1, -jnp.inf, le))
    p1 = 1.0 / z
    p2 = jnp.exp(m2 - m1) / z
    tot = p1 + p2
    w1 = grp_p * (p1 / tot)
    w2 = grp_p * (p2 / tot)
    e1 = grp_idx * eg + i1
    e2 = grp_idx * eg + i2

    erow = lax.broadcasted_iota(jnp.int32, (N_EXPERTS, tm), 0).astype(F32)
    onehot = jnp.where(erow == e1, 1.0, jnp.where(erow == e2, 1.0, 0.0))
    before = _dot(onehot.astype(BF16), before_ref[...]) + carry_ref[...]
    rank1 = jnp.sum(jnp.where(erow == e1, before, 0.0), axis=0, keepdims=True)
    rank2 = jnp.sum(jnp.where(erow == e2, before, 0.0), axis=0, keepdims=True)
    carry = carry_ref[...] + jnp.sum(onehot, axis=1, keepdims=True)
    carry_ref[...] = carry
    cnt_ref[...] = carry

    mrow = lax.broadcasted_iota(jnp.int32, (META_ROWS, tm), 0)
    meta_t_ref[...] = jnp.where(mrow == 0, e1, jnp.where(mrow == 1, e2, jnp.where(
        mrow == 2, rank1, jnp.where(mrow == 3, rank2, jnp.where(
            mrow == 4, w1, jnp.where(mrow == 5, w2, 0.0))))))


def _earlier_matrix(tm):
    return np.triu(np.ones((tm, tm), np.float32), k=1)


def _router(h1, g_ffn, w_router, b_router, tm):
    t = h1.shape[0]
    return pl.pallas_call(
        _router_kernel,
        grid=(t // tm,),
        in_specs=[
            pl.BlockSpec((tm, D_MODEL), lambda i: (i, 0)),
            pl.BlockSpec((1, D_MODEL), lambda i: (0, 0)),
            pl.BlockSpec((D_MODEL, ROUTER_W), lambda i: (0, 0)),
            pl.BlockSpec((1, ROUTER_W), lambda i: (0, 0)),
            pl.BlockSpec((tm, tm), lambda i: (0, 0)),
        ],
        out_specs=[
            pl.BlockSpec((2, tm, SC_ROW), lambda i: (0, i, 0)),
            pl.BlockSpec((META_ROWS, tm), lambda i: (0, i)),
            pl.BlockSpec((N_EXPERTS, 1), lambda i: (0, 0)),
        ],
        out_shape=[
            jax.ShapeDtypeStruct((2, t, SC_ROW), U32),
            jax.ShapeDtypeStruct((META_ROWS, t), F32),
            jax.ShapeDtypeStruct((N_EXPERTS, 1), F32),
        ],
        scratch_shapes=[pltpu.VMEM((N_EXPERTS, 1), F32)],
        compiler_params=_params(("arbitrary",)),
        name="router",
    )(h1, g_ffn, w_router, b_router, jnp.asarray(_earlier_matrix(tm), BF16))


def _plan(meta_t, counts, n_tiles):
    e1 = meta_t[0].astype(jnp.int32)
    e2 = meta_t[1].astype(jnp.int32)
    rank1 = meta_t[2].astype(jnp.int32)
    rank2 = meta_t[3].astype(jnp.int32)
    cnt = counts[:, 0].astype(jnp.int32)
    padded = ((cnt + ROW_TILE - 1) // ROW_TILE) * ROW_TILE
    ends = jnp.cumsum(padded)
    starts = ends - padded
    experts = jnp.arange(N_EXPERTS, dtype=jnp.int32)
    pos1 = rank1 + jnp.sum(jnp.where(e1[None, :] == experts[:, None], starts[:, None], 0), axis=0)
    pos2 = rank2 + jnp.sum(jnp.where(e2[None, :] == experts[:, None], starts[:, None], 0), axis=0)
    tile_start = jnp.arange(n_tiles, dtype=jnp.int32) * ROW_TILE
    tile_expert = jnp.sum((tile_start[:, None] >= ends[None, :]).astype(jnp.int32), axis=1)
    tile_expert = jnp.minimum(tile_expert, N_EXPERTS - 1)
    rows_left = jnp.sum(jnp.where(tile_expert[:, None] == experts, cnt + starts, 0), axis=1) - tile_start
    n_valid = jnp.clip(rows_left, 0, ROW_TILE).astype(jnp.int32)
    last_used = jnp.maximum(ends[-1] // ROW_TILE - 1, 0)
    block = jnp.minimum(jnp.arange(n_tiles, dtype=jnp.int32), last_used)
    tile_expert = jnp.sum(jnp.where(block[:, None] == jnp.arange(n_tiles)[None, :],
                                    tile_expert[None, :], 0), axis=1)
    plane = n_tiles * ROW_TILE
    half_rows = jnp.concatenate([pos1, pos1 + plane, pos2, pos2 + plane])[None]
    return half_rows, tile_expert, n_valid, block


def _sc_mesh():
    return plsc.VectorSubcoreMesh(core_axis_name="c", subcore_axis_name="s")


def _sc_scatter_rows(rows, idx, n_out):
    t, width = rows.shape
    steps = t // SC_WINDOW

    @pl.kernel(out_type=jax.ShapeDtypeStruct((n_out, width), rows.dtype), mesh=_sc_mesh(),
               scratch_types=[], name="moe_scatter")
    def scatter(rows_hbm, idx_hbm, out_hbm):
        def body(rows_vmem, idx_vmem):
            pltpu.sync_copy(rows_vmem, out_hbm.at[idx_vmem.at[0]])

        pltpu.emit_pipeline(
            body,
            grid=(2, steps),
            in_specs=[pl.BlockSpec((SC_WINDOW, width), lambda k, j: (j, 0)),
                      pl.BlockSpec((1, SC_WINDOW), lambda k, j: (0, k * steps + j))],
            out_specs=[],
            core_axis_name=("c", "s"),
            dimension_semantics=(pltpu.PARALLEL, pltpu.PARALLEL),
        )(rows_hbm, idx_hbm)

    return scatter(rows, idx)


def _sc_gather_rows(table, idx):
    m = idx.shape[1]
    width = table.shape[1]
    steps = m // (2 * SC_WINDOW)

    @pl.kernel(out_type=jax.ShapeDtypeStruct((m, width), table.dtype), mesh=_sc_mesh(),
               scratch_types=[], name="moe_gather")
    def gather(table_hbm, idx_hbm, out_hbm):
        def body(idx_vmem, out_vmem):
            pltpu.sync_copy(table_hbm.at[idx_vmem.at[0]], out_vmem)

        pltpu.emit_pipeline(
            body,
            grid=(2, steps),
            in_specs=[pl.BlockSpec((1, SC_WINDOW), lambda k, j: (0, k * steps + j))],
            out_specs=[pl.BlockSpec((SC_WINDOW, width), lambda k, j: (k * steps + j, 0))],
            core_axis_name=("c", "s"),
            dimension_semantics=(pltpu.PARALLEL, pltpu.PARALLEL),
        )(idx_hbm, out_hbm)

    return gather(table, idx)


def _experts_kernel(te_ref, nv_ref, blk_ref, xs_ref, wg_ref, wu_ref, wd_ref, ys_ref):
    r = pl.program_id(0)
    n_valid = nv_ref[r]

    @pl.when(n_valid > 0)
    def _():
        parts = [p.astype(BF16) for p in _unpack_rows(xs_ref[0]) + _unpack_rows(xs_ref[1])]
        cols = (0, 2 * SC_ROW, SC_ROW, 3 * SC_ROW)
        hg = sum(_dot(p, wg_ref[c:c + SC_ROW, :]) for p, c in zip(parts, cols))
        hu = sum(_dot(p, wu_ref[c:c + SC_ROW, :]) for p, c in zip(parts, cols))
        row = lax.broadcasted_iota(jnp.int32, (ROW_TILE, 1), 0)
        hid = jnp.where(row < n_valid, hg * _sigmoid(hg) * hu, 0.0).astype(BF16)
        _split_planes(_pack_rows(_dot(hid, wd_ref[...])), ys_ref)


def _experts(xs, tile_expert, n_valid, block, w_gate, w_up, w_down):
    n_tiles = xs.shape[1] // ROW_TILE
    w_spec = pl.BlockSpec((None, D_MODEL, D_EXPERT), lambda r, te, nv, blk: (te[r], 0, 0))
    grid_spec = pltpu.PrefetchScalarGridSpec(
        num_scalar_prefetch=3,
        grid=(n_tiles,),
        in_specs=[
            pl.BlockSpec((2, ROW_TILE, SC_ROW), lambda r, te, nv, blk: (0, blk[r], 0)),
            w_spec, w_spec,
            pl.BlockSpec((None, D_EXPERT, D_MODEL), lambda r, te, nv, blk: (te[r], 0, 0)),
        ],
        out_specs=pl.BlockSpec((2, ROW_TILE, SC_ROW), lambda r, te, nv, blk: (0, blk[r], 0)),
    )
    return pl.pallas_call(
        _experts_kernel,
        grid_spec=grid_spec,
        out_shape=jax.ShapeDtypeStruct((2, n_tiles * ROW_TILE, SC_ROW), U32),
        compiler_params=_params(("arbitrary",)),
        name="experts",
    )(tile_expert, n_valid, block, xs, w_gate, w_up, w_down)


def _ple_kernel(h_ref, yg_ref, meta_t_ref, p_ref, wg_ref, bg_ref, wp_ref, gf_ref, *rest, final_norm):
    o_ref = rest[-1]
    meta = jnp.transpose(meta_t_ref[...])
    w1 = meta[:, 4:5]
    w2 = meta[:, 5:6]
    q0, q2 = (w1 * u + w2 * v for u, v in zip(_unpack_rows(yg_ref[0]), _unpack_rows(yg_ref[2])))
    q1, q3 = (w1 * u + w2 * v for u, v in zip(_unpack_rows(yg_ref[1]), _unpack_rows(yg_ref[3])))
    moe = jnp.concatenate([q0, q1, q2, q3], axis=1)
    h = h_ref[...] + moe
    gate = _sigmoid(_dot(_rms(h).astype(BF16), wg_ref[...]) + bg_ref[...])
    h = h + gate * _dot(p_ref[...].astype(BF16), wp_ref[...])
    o_ref[...] = _rms(h, gf_ref[...]) if final_norm else h


def _ple(h1, yg, meta_t, p2, w_gate, b_gate, w_proj, g_final, final_norm, tm, part, prev_out):
    t = h1.shape[0]
    steps = t // tm // GATHER_PARTS
    off = part * steps
    in_specs = [
        pl.BlockSpec((tm, D_MODEL), lambda i: (i + off, 0)),
        pl.BlockSpec((4, tm, SC_ROW), lambda i: (0, i, 0)),
        pl.BlockSpec((META_ROWS, tm), lambda i: (0, i + off)),
        pl.BlockSpec((tm, D_PLE), lambda i: (i + off, 0)),
        pl.BlockSpec((D_MODEL, D_MODEL), lambda i: (0, 0)),
        pl.BlockSpec((1, D_MODEL), lambda i: (0, 0)),
        pl.BlockSpec((D_PLE, D_MODEL), lambda i: (0, 0)),
        pl.BlockSpec((1, D_MODEL), lambda i: (0, 0)),
    ]
    args = [h1, yg, meta_t, p2, w_gate, b_gate, w_proj, g_final]
    aliases = {}
    if prev_out is not None:
        in_specs.append(pl.BlockSpec(memory_space=pl.ANY))
        args.append(prev_out)
        aliases = {len(args) - 1: 0}
    return pl.pallas_call(
        functools.partial(_ple_kernel, final_norm=final_norm),
        grid=(steps,),
        in_specs=in_specs,
        out_specs=pl.BlockSpec((tm, D_MODEL), lambda i: (i + off, 0)),
        out_shape=jax.ShapeDtypeStruct((t, D_MODEL), F32),
        input_output_aliases=aliases,
        compiler_params=_params(("parallel",)),
        name="ple",
    )(*args)


def kernel(x, p, g_mix, w_in, pool_w, pool_scale, ssm_a_re, ssm_a_im, ssm_log_dt, ssm_b_re,
           ssm_b_im, ssm_c_re, ssm_c_im, ssm_d, glu_w, glu_b, w_out, g_ffn, router_grp_w,
           router_grp_b, router_exp_w, router_exp_b, exp_w_gate, exp_w_up, exp_w_down, g_ple,
           ple_gate_w, ple_gate_b, ple_proj_w, g_final):
    bsz, seq, dm = x.shape
    depth = g_mix.shape[0]
    t = bsz * seq
    seq_chunks = seq // CHUNK
    nch = t // CHUNK
    tm = 1024
    n_sorted = 2 * t + N_EXPERTS * ROW_TILE
    w_gate_all = exp_w_gate.reshape(depth * N_EXPERTS, dm, D_EXPERT)
    w_up_all = exp_w_up.reshape(depth * N_EXPERTS, dm, D_EXPERT)
    w_down_all = exp_w_down.reshape(depth * N_EXPERTS, D_EXPERT, dm)

    h = x.reshape(t, dm)
    for i in range(depth):
        w_in_b = (g_mix[i][:, None] * w_in[i]).astype(BF16)
        zp, ut = _in_proj(h.reshape(nch, CHUNK, dm), w_in_b[:, :D_POOL],
                          jnp.transpose(w_in_b[:, D_POOL:]))
        a = _pool(zp.reshape(bsz, seq, D_POOL), pool_w[i], pool_scale[i][None])
        yt, w_gate_b, w_up_b, w_down_b = _ssm(
            ut, ssm_a_re[i], ssm_a_im[i], ssm_log_dt[i], ssm_b_re[i], ssm_b_im[i], ssm_c_re[i],
            ssm_c_im[i], ssm_d[i], seq_chunks, w_gate_all, w_up_all, w_down_all, i)
        h = _mix_out(h.reshape(nch, CHUNK, dm), a.reshape(nch, CHUNK, D_POOL), yt,
                     jnp.transpose(glu_w[i]).astype(BF16), glu_b[i][:, None],
                     w_out[i].astype(BF16)).reshape(t, dm)

        eg = EXPERTS_PER_GROUP
        w_router = jnp.concatenate(
            [router_grp_w[i], jnp.zeros((dm, eg - N_EXPERT_GROUPS), F32),
             jnp.transpose(router_exp_w[i], (1, 0, 2)).reshape(dm, N_EXPERTS),
             jnp.zeros((dm, ROUTER_W - eg - N_EXPERTS), F32)], axis=1)
        b_router = jnp.concatenate(
            [router_grp_b[i], jnp.zeros((eg - N_EXPERT_GROUPS,), F32),
             router_exp_b[i].reshape(N_EXPERTS),
             jnp.zeros((ROUTER_W - eg - N_EXPERTS,), F32)])[None]
        vp, meta_t, counts = _router(h, g_ffn[i][None], w_router, b_router, tm)
        idx, tile_expert, n_valid, block = _plan(meta_t, counts, n_sorted // ROW_TILE)
        xs = _sc_scatter_rows(vp.reshape(2 * t, SC_ROW), idx, 2 * n_sorted)
        ys = _experts(xs.reshape(2, n_sorted, SC_ROW), tile_expert, n_valid, block,
                      w_gate_b, w_up_b, w_down_b)
        ys2 = ys.reshape(2 * n_sorted, SC_ROW)
        idx4 = idx.reshape(4, t)
        tp = t // GATHER_PARTS
        ple_wg = (g_ple[i][:, None] * ple_gate_w[i]).astype(BF16)
        ple_wp = ple_proj_w[i].astype(BF16)
        out = None
        for q in range(GATHER_PARTS):
            yg_q = _sc_gather_rows(ys2, idx4[:, q * tp:(q + 1) * tp].reshape(1, 4 * tp))
            out = _ple(h, yg_q.reshape(4, tp, SC_ROW), meta_t, p[i].reshape(t, D_PLE), ple_wg,
                       ple_gate_b[i][None], ple_wp, g_final[None], i == depth - 1, tm, q, out)
        h = out
    return h.reshape(bsz, seq, dm)
```

```python
import functools
import math

import numpy as np
import jax
import jax.numpy as jnp
from jax import lax
from jax.experimental import pallas as pl
from jax.experimental.pallas import tpu as pltpu
from jax.experimental.pallas import tpu_sc as plsc

F32 = jnp.float32
BF16 = jnp.bfloat16
U32 = jnp.uint32

D_MODEL = 1024
D_POOL = 512
D_SSM = 512
POOL_WINDOWS = (2, 4, 8, 16)
POOL_GROUP = 128
SSM_GROUP = 16
N_SSM_GROUPS = 32
SSM_STATE = 64
N_EXPERT_GROUPS = 4
EXPERTS_PER_GROUP = 8
N_EXPERTS = N_EXPERT_GROUPS * EXPERTS_PER_GROUP
D_EXPERT = 256
D_PLE = 256
RMS_EPS = 1e-6

LANES = 128
CHUNK = 32
CHUNK_W = CHUNK * SSM_GROUP
T_SUB = 8
S_TILE = 8
C_TILE = 128
ROUTER_W = LANES
HALF = D_MODEL // 2
ROW_TILE = 1024
SUB_TILE = 256
SC_WINDOW = 128
SC_ROW = HALF // 2
META_ROWS = 8
GATHER_PARTS = 4
VMEM_LIMIT = 56 * 1024 * 1024


def _dot(a, b):
    return jnp.dot(a, b, preferred_element_type=F32)


def _dot_nt(a, b):
    return lax.dot_general(a, b, (((1,), (1,)), ((), ())), preferred_element_type=F32)


def _dot_tn(a, b):
    return lax.dot_general(a, b, (((0,), (0,)), ((), ())), preferred_element_type=F32)


def _rms(x, g=None):
    y = x * lax.rsqrt(jnp.mean(x * x, axis=-1, keepdims=True) + RMS_EPS)
    return y if g is None else y * g


def _sigmoid(x):
    return 1.0 / (1.0 + jnp.exp(-x))


def _params(sem):
    return pltpu.CompilerParams(dimension_semantics=sem, vmem_limit_bytes=VMEM_LIMIT)


def _tile_step():
    n_s = pl.num_programs(1)
    return pl.program_id(0) * n_s + pl.program_id(1), pl.num_programs(0) * n_s


def _row_copies(hbm, buf, sem, step_idx, slot_idx, to_hbm):
    n_s = pl.num_programs(1)
    c0 = (step_idx // n_s) * C_TILE
    s0 = (step_idx % n_s) * S_TILE
    out = []
    for j in range(S_TILE):
        far, near = hbm.at[pl.ds(c0, C_TILE), s0 + j, :], buf.at[slot_idx, j]
        src, dst = (near, far) if to_hbm else (far, near)
        out.append(pltpu.make_async_copy(src, dst, sem.at[slot_idx]))
    return out


def _load_time_major_rows(hbm, buf, sem):
    step, n_steps = _tile_step()
    slot = step % 2

    @pl.when(step == 0)
    def _():
        for cp in _row_copies(hbm, buf, sem, step, slot, False):
            cp.start()

    @pl.when(step + 1 < n_steps)
    def _():
        for cp in _row_copies(hbm, buf, sem, step + 1, 1 - slot, False):
            cp.start()

    for cp in _row_copies(hbm, buf, sem, step, slot, False):
        cp.wait()
    return buf[slot].reshape(S_TILE * C_TILE, buf.shape[-1])


def _store_time_major_rows(val, hbm, buf, sem):
    step, n_steps = _tile_step()
    slot = step % 2

    @pl.when(step >= 2)
    def _():
        for cp in _row_copies(hbm, buf, sem, step - 2, slot, True):
            cp.wait()

    buf[slot] = val.reshape(S_TILE, C_TILE, val.shape[-1])
    for cp in _row_copies(hbm, buf, sem, step, slot, True):
        cp.start()

    @pl.when(step == n_steps - 1)
    def _():
        @pl.when(step >= 1)
        def _():
            for cp in _row_copies(hbm, buf, sem, step - 1, 1 - slot, True):
                cp.wait()
        for cp in _row_copies(hbm, buf, sem, step, slot, True):
            cp.wait()


def _in_proj_kernel(x_hbm, wp_ref, wst_ref, zp_hbm, ut_ref, xbuf, zbuf, xsem, zsem):
    nc = C_TILE
    u = _rms(_load_time_major_rows(x_hbm, xbuf, xsem)).astype(BF16)
    zt = _dot_nt(wst_ref[...], u).astype(BF16)
    for j in range(S_TILE):
        ut_ref[:, j, :, :] = zt[:, j * nc:(j + 1) * nc].reshape(N_SSM_GROUPS, SSM_GROUP, nc)
    _store_time_major_rows(_dot(u, wp_ref[...]), zp_hbm, zbuf, zsem)


def _in_proj(x3, w_pool, w_ssm_t):
    nch = x3.shape[0]
    return pl.pallas_call(
        _in_proj_kernel,
        grid=(nch // C_TILE, CHUNK // S_TILE),
        in_specs=[
            pl.BlockSpec(memory_space=pl.ANY),
            pl.BlockSpec((D_MODEL, D_POOL), lambda c, s: (0, 0)),
            pl.BlockSpec((D_SSM, D_MODEL), lambda c, s: (0, 0)),
        ],
        out_specs=[
            pl.BlockSpec(memory_space=pl.ANY),
            pl.BlockSpec((N_SSM_GROUPS, None, S_TILE, SSM_GROUP, C_TILE), lambda c, s: (0, c, s, 0, 0)),
        ],
        out_shape=[
            jax.ShapeDtypeStruct((nch, CHUNK, D_POOL), F32),
            jax.ShapeDtypeStruct((N_SSM_GROUPS, nch // C_TILE, CHUNK, SSM_GROUP, C_TILE), BF16),
        ],
        scratch_shapes=[
            pltpu.VMEM((2, S_TILE, C_TILE, D_MODEL), F32),
            pltpu.VMEM((2, S_TILE, C_TILE, D_POOL), F32),
            pltpu.SemaphoreType.DMA((2,)),
            pltpu.SemaphoreType.DMA((2,)),
        ],
        compiler_params=_params(("arbitrary", "arbitrary")),
        name="in_proj",
    )(x3, w_pool, w_ssm_t)


def _shift_rows(x, d, row, n):
    if d == 0:
        return x
    r = pltpu.roll(x, d % n, 0)
    if d > 0:
        return jnp.where(row >= d, r, 0.0)
    return jnp.where(row < n + d, r, 0.0)


def _pool_kernel(z_ref, w_ref, sc_ref, o_ref):
    n = z_ref.shape[0]
    gi = pl.program_id(1)
    row = lax.broadcasted_iota(jnp.int32, (n, 1), 0)

    for k, w in enumerate(POOL_WINDOWS):
        @pl.when(gi == k)
        def _(w=w):
            x = z_ref[...]
            half = w // 2
            pd, pu, span = x, x, 1
            while span < half:
                pd = pd + _shift_rows(pd, span, row, n)
                pu = pu + _shift_rows(pu, -span, row, n)
                span *= 2
            total = _shift_rows(pd, 1, row, n) + pu
            lo = jnp.maximum(row - half, 0)
            hi = jnp.minimum(row + half, n)
            cnt = (hi - lo).astype(F32)
            diff = (total / cnt - x).astype(BF16)
            o_ref[...] = _dot(diff, w_ref[...].astype(BF16)) * sc_ref[...]


def _pool(zp3, pool_w, pool_scale):
    b, s, _ = zp3.shape
    return pl.pallas_call(
        _pool_kernel,
        grid=(b, len(POOL_WINDOWS)),
        in_specs=[
            pl.BlockSpec((None, s, POOL_GROUP), lambda i, g: (i, 0, g)),
            pl.BlockSpec((None, POOL_GROUP, POOL_GROUP), lambda i, g: (g, 0, 0)),
            pl.BlockSpec((1, POOL_GROUP), lambda i, g: (0, g)),
        ],
        out_specs=pl.BlockSpec((None, s, POOL_GROUP), lambda i, g: (i, 0, g)),
        out_shape=jax.ShapeDtypeStruct((b, s, D_POOL), F32),
        compiler_params=_params(("parallel", "parallel")),
        name="pool",
    )(zp3, pool_w, pool_scale)


def _expand_consts():
    time = np.arange(CHUNK_W) // SSM_GROUP
    def onehot(e):
        m = np.zeros((CHUNK_W, LANES), np.float32)
        m[np.arange(CHUNK_W), e] = 1.0
        return m
    return np.stack([
        onehot(CHUNK - 1 - time),
        onehot(time),
        onehot(time + 1),
        onehot(CHUNK - time),
    ])


def _cmul_packed(x, p, q):
    return x * p + pltpu.roll(x, LANES // 2, 1) * q


def _ssm_kernel(u_ref, vec_ref, mat_ref, exp_ref, wg_ref, wu_ref, wd_ref, y_ref, og_ref, ou_ref,
                od_ref, *, seq_chunks):
    og_ref[...] = wg_ref[...].astype(BF16)
    ou_ref[...] = wu_ref[...].astype(BF16)
    od_ref[...] = wd_ref[...].astype(BF16)

    n_ct = u_ref.shape[0]
    nch = n_ct * C_TILE
    half = LANES // 2
    lane = lax.broadcasted_iota(jnp.int32, (1, LANES), 1)
    lo_half = lane < half

    def direction(di):
        a_re = vec_ref[di, 0:1]
        a_im = vec_ref[di, 1:2]
        dt = jnp.exp(vec_ref[di, 2:3])
        mag = jnp.exp(a_re * dt)
        ang = a_im * dt
        lam = jnp.where(lo_half, mag * jnp.cos(ang), mag * jnp.sin(ang))
        lb_re = mag * jnp.cos(ang)
        lb_im = mag * jnp.sin(ang)
        den = a_re * a_re + a_im * a_im
        f_re = ((lb_re - 1.0) * a_re + lb_im * a_im) / den
        f_im = (lb_im * a_re - (lb_re - 1.0) * a_im) / den
        return lam, f_re, f_im

    def power_table(lam):
        e = lax.broadcasted_iota(jnp.int32, (LANES, 1), 0)
        tab = jnp.where(lo_half, 1.0, 0.0) * jnp.ones((LANES, 1), F32)
        sq = lam
        for k in range(7):
            p = jnp.where(lo_half, sq, pltpu.roll(sq, half, 1))
            q = jnp.where(lo_half, -pltpu.roll(sq, half, 1), sq)
            tab = jnp.where(((e >> k) & 1) == 1, _cmul_packed(tab, p, q), tab)
            sq = _cmul_packed(sq, p, q)
        return tab

    def tile_rows(x16):
        return jnp.broadcast_to(x16[None], (CHUNK, SSM_GROUP, LANES)).reshape(CHUNK_W, LANES)

    def expanded(tab, which, v_re, v_im, conj_sign):
        lexp = _dot(exp_ref[which], tab.astype(BF16))
        if conj_sign > 0:
            p = jnp.where(lo_half, v_re, v_re)
            q = jnp.where(lo_half, -v_im, v_im)
        else:
            p = jnp.where(lo_half, v_re, -v_re)
            q = jnp.where(lo_half, -v_im, -v_im)
        return lexp * tile_rows(p) + pltpu.roll(lexp, half, 1) * tile_rows(q)

    lam_f, ff_re, ff_im = direction(0)
    lam_b, fb_re, fb_im = direction(1)
    tab_f = power_table(lam_f)
    tab_b = power_table(lam_b)

    def bbar(bt_re, bt_im, f_re, f_im):
        return bt_re * f_re - bt_im * f_im, bt_re * f_im + bt_im * f_re

    def mat(di, k):
        return mat_ref[di, k * SSM_GROUP:(k + 1) * SSM_GROUP, :]

    bf_re, bf_im = bbar(mat(0, 0), mat(0, 1), ff_re, ff_im)
    bb_re, bb_im = bbar(mat(1, 0), mat(1, 1), fb_re, fb_im)

    pb1 = expanded(tab_f, 0, bf_re, bf_im, 1)
    pb2 = expanded(tab_b, 1, bb_re, bb_im, 1)
    pb3 = expanded(tab_b, 2, bb_re, bb_im, 1)
    ft_f = expanded(tab_f, 2, mat(0, 3), mat(0, 4), -1)
    ft_b = expanded(tab_b, 3, mat(1, 3), mat(1, 4), -1)

    row_w = lax.broadcasted_iota(jnp.int32, (CHUNK_W, 1), 0)
    last_blk = row_w >= CHUNK_W - SSM_GROUP
    pb2_lag0 = jnp.where(last_blk, pltpu.roll(pb2, CHUNK_W - SSM_GROUP, 0), 0.0)
    ccr_f = mat(0, 2).astype(BF16)
    ccr_b = mat(1, 2).astype(BF16)
    r_lo = _dot_nt(ccr_f, pb1.astype(BF16)) + _dot_nt(ccr_b, pb2_lag0.astype(BF16))
    co = lax.broadcasted_iota(jnp.int32, (SSM_GROUP, CHUNK_W), 0)
    col = lax.broadcasted_iota(jnp.int32, (SSM_GROUP, CHUNK_W), 1)
    r_lo = r_lo + jnp.where(col == CHUNK_W - SSM_GROUP + co, mat(0, 5)[:, 0:1], 0.0)
    r_hi = _dot_nt(ccr_b, pb3.astype(BF16))
    r_t = jnp.concatenate([r_lo, r_hi], axis=1)
    g_t = jnp.concatenate(
        [pltpu.roll(r_t, SSM_GROUP * (tl + 1), 1) for tl in range(T_SUB)], axis=0
    ).astype(BF16)

    u = jnp.concatenate([u_ref[ct].reshape(CHUNK_W, C_TILE) for ct in range(n_ct)],
                        axis=1)
    e_mat = jnp.concatenate([pb1, pb2], axis=1).astype(BF16)
    xend = _dot_tn(e_mat, u)
    lanec = lax.broadcasted_iota(jnp.int32, (1, nch), 1) % seq_chunks
    ns = SSM_STATE

    def scan(re, im, tab, forward):
        lam_col = jnp.transpose(tab[CHUNK:CHUNK + 8, :])[:, 0:1]
        a, b = lam_col[:ns], lam_col[ns:]
        n_steps = int(math.log2(seq_chunks))

        def shifted(v, d):
            if forward:
                return jnp.where(lanec >= d, pltpu.roll(v, d, 1), 0.0)
            return jnp.where(lanec < seq_chunks - d, pltpu.roll(v, nch - d, 1), 0.0)

        for k in range(n_steps):
            sr, si = shifted(re, 1 << k), shifted(im, 1 << k)
            re, im = re + (sr * a - si * b), im + (sr * b + si * a)
            a, b = a * a - b * b, 2.0 * a * b
        return shifted(re, 1), shifted(im, 1)

    f_re, f_im = scan(xend[:ns], xend[ns:2 * ns], tab_f, True)
    b_re, b_im = scan(xend[2 * ns:3 * ns], xend[3 * ns:], tab_b, False)
    xin = jnp.concatenate([f_re, f_im, b_re, b_im], axis=0).astype(BF16)
    f_t = jnp.concatenate([ft_f, ft_b], axis=1).astype(BF16)

    toeplitz = jnp.concatenate(
        [g_t[:, CHUNK_W - LANES * th:2 * CHUNK_W - LANES * th] for th in range(CHUNK // T_SUB)],
        axis=0)
    y_t = _dot(toeplitz, u) + _dot(f_t, xin)
    for ct in range(n_ct):
        y_ref[ct] = y_t[:, ct * C_TILE:(ct + 1) * C_TILE].reshape(
            CHUNK, SSM_GROUP, C_TILE).astype(y_ref.dtype)


def _ssm(ut, a_re, a_im, log_dt, b_re, b_im, c_re, c_im, d, seq_chunks, w_gate, w_up, w_down, layer):
    g, n_ct = ut.shape[:2]
    assert N_EXPERTS % g == 0
    epg = N_EXPERTS // g
    base = layer * g
    n = SSM_STATE

    def per_group(a):
        return jnp.swapaxes(a, 0, 1)

    def dup(a):
        return jnp.concatenate([a, a], axis=-1)

    vecs = dup(jnp.stack([per_group(a_re), per_group(a_im),
                          jnp.broadcast_to(per_group(log_dt)[..., None], (g, 2, n))], axis=2))
    cr, ci = per_group(c_re), per_group(c_im)
    d_blk = jnp.broadcast_to(d.reshape(g, 1, SSM_GROUP, 1), (g, 2, SSM_GROUP, LANES))
    mats = jnp.concatenate(
        [dup(jnp.swapaxes(per_group(b_re), 2, 3)), dup(jnp.swapaxes(per_group(b_im), 2, 3)),
         jnp.concatenate([cr, -ci], axis=-1), dup(cr), dup(ci), d_blk], axis=2)
    exp_c = jnp.asarray(_expand_consts(), BF16)

    return pl.pallas_call(
        functools.partial(_ssm_kernel, seq_chunks=seq_chunks),
        grid=(g,),
        in_specs=[
            pl.BlockSpec((None, n_ct, CHUNK, SSM_GROUP, C_TILE), lambda i: (i, 0, 0, 0, 0)),
            pl.BlockSpec((None, 2, 3, LANES), lambda i: (i, 0, 0, 0)),
            pl.BlockSpec((None, 2, 6 * SSM_GROUP, LANES), lambda i: (i, 0, 0, 0)),
            pl.BlockSpec((4, CHUNK_W, LANES), lambda i: (0, 0, 0)),
            pl.BlockSpec((epg, D_MODEL, D_EXPERT), lambda i: (base + i, 0, 0)),
            pl.BlockSpec((epg, D_MODEL, D_EXPERT), lambda i: (base + i, 0, 0)),
            pl.BlockSpec((epg, D_EXPERT, D_MODEL), lambda i: (base + i, 0, 0)),
        ],
        out_specs=[
            pl.BlockSpec((None, n_ct, CHUNK, SSM_GROUP, C_TILE), lambda i: (i, 0, 0, 0, 0)),
            pl.BlockSpec((epg, D_MODEL, D_EXPERT), lambda i: (i, 0, 0)),
            pl.BlockSpec((epg, D_MODEL, D_EXPERT), lambda i: (i, 0, 0)),
            pl.BlockSpec((epg, D_EXPERT, D_MODEL), lambda i: (i, 0, 0)),
        ],
        out_shape=[
            jax.ShapeDtypeStruct(ut.shape, BF16),
            jax.ShapeDtypeStruct((N_EXPERTS, D_MODEL, D_EXPERT), BF16),
            jax.ShapeDtypeStruct((N_EXPERTS, D_MODEL, D_EXPERT), BF16),
            jax.ShapeDtypeStruct((N_EXPERTS, D_EXPERT, D_MODEL), BF16),
        ],
        compiler_params=_params(("parallel",)),
        name="ssm",
    )(ut, vecs, mats, exp_c, w_gate, w_up, w_down)


def _mix_out_kernel(x_hbm, a_hbm, yt_ref, gwt_ref, gb_ref, wo_ref, h_hbm, xbuf, abuf, hbuf,
                    xsem, asem, hsem):
    nc = C_TILE
    x = _load_time_major_rows(x_hbm, xbuf, xsem)
    a = _load_time_major_rows(a_hbm, abuf, asem).astype(BF16)
    y = jnp.concatenate([yt_ref[:, j, :, :].reshape(D_SSM, nc) for j in range(S_TILE)],
                        axis=1).astype(F32)
    z = 0.5 * y * (1.0 + jnp.tanh(math.sqrt(2.0 / math.pi) * (y + 0.044715 * (y * y * y))))
    gate = _sigmoid(_dot(gwt_ref[...], z.astype(BF16)) + gb_ref[...])
    s = (z * gate).astype(BF16)
    h = x + _dot(a, wo_ref[:D_POOL, :]) + _dot_tn(s, wo_ref[D_POOL:, :])
    _store_time_major_rows(h, h_hbm, hbuf, hsem)


def _mix_out(x3, a3, yt, glu_w_t, glu_b_col, w_out):
    nch = x3.shape[0]
    return pl.pallas_call(
        _mix_out_kernel,
        grid=(nch // C_TILE, CHUNK // S_TILE),
        in_specs=[
            pl.BlockSpec(memory_space=pl.ANY),
            pl.BlockSpec(memory_space=pl.ANY),
            pl.BlockSpec((N_SSM_GROUPS, None, S_TILE, SSM_GROUP, C_TILE), lambda c, t: (0, c, t, 0, 0)),
            pl.BlockSpec((D_SSM, D_SSM), lambda c, t: (0, 0)),
            pl.BlockSpec((D_SSM, 1), lambda c, t: (0, 0)),
            pl.BlockSpec((D_MODEL, D_MODEL), lambda c, t: (0, 0)),
        ],
        out_specs=pl.BlockSpec(memory_space=pl.ANY),
        out_shape=jax.ShapeDtypeStruct((nch, CHUNK, D_MODEL), F32),
        scratch_shapes=[
            pltpu.VMEM((2, S_TILE, C_TILE, D_MODEL), F32),
            pltpu.VMEM((2, S_TILE, C_TILE, D_POOL), F32),
            pltpu.VMEM((2, S_TILE, C_TILE, D_MODEL), F32),
            pltpu.SemaphoreType.DMA((2,)),
            pltpu.SemaphoreType.DMA((2,)),
            pltpu.SemaphoreType.DMA((2,)),
        ],
        compiler_params=_params(("arbitrary", "arbitrary")),
        name="mix_out",
    )(x3, a3, yt, glu_w_t, glu_b_col, w_out)


def _pack_rows(x):
    b = lax.bitcast_convert_type(x.astype(BF16).astype(F32), U32)
    return (b[:, :HALF] & jnp.uint32(0xFFFF0000)) | (b[:, HALF:] >> 16)


def _unpack_rows(w):
    lo = lax.bitcast_convert_type(w & jnp.uint32(0xFFFF0000), F32)
    hi = lax.bitcast_convert_type(w << 16, F32)
    return lo, hi


def _split_bf16(x):
    hi = x.astype(BF16)
    return hi, (x - hi.astype(F32)).astype(BF16)


def _route(v32, wr_ref, br_ref):
    v_hi, v_lo = _split_bf16(v32)
    w_hi, w_lo = _split_bf16(wr_ref[...])
    logits = _dot(v_hi, w_hi) + (_dot(v_lo, w_hi) + _dot(v_hi, w_lo)) + br_ref[...]
    return jnp.transpose(logits)


def _top1(x, valid=None):
    n = x.shape[0]
    row = lax.broadcasted_iota(jnp.int32, x.shape, 0).astype(F32)
    if valid is not None:
        x = jnp.where(valid, x, -jnp.inf)
    m = jnp.max(x, axis=0, keepdims=True)
    idx = jnp.min(jnp.where(x == m, row, float(n)), axis=0, keepdims=True)
    return m, idx, x, row


def _split_planes(packed, ref):
    ref[0] = packed[:, :SC_ROW]
    ref[1] = packed[:, SC_ROW:]


def _router_kernel(h_ref, g_ref, wr_ref, br_ref, before_ref, vp_ref, meta_t_ref, cnt_ref, carry_ref):
    @pl.when(pl.program_id(0) == 0)
    def _():
        carry_ref[...] = jnp.zeros_like(carry_ref)

    v32 = _rms(h_ref[...], g_ref[...])
    _split_planes(_pack_rows(v32), vp_ref)
    lt = _route(v32, wr_ref, br_ref)
    tm = lt.shape[1]
    eg = EXPERTS_PER_GROUP

    grp = lt[:eg]
    grp_row = lax.broadcasted_iota(jnp.int32, grp.shape, 0)
    mg, grp_idx, grp, _ = _top1(grp, grp_row < N_EXPERT_GROUPS)
    grp_p = 1.0 / jnp.sum(jnp.exp(grp - mg), axis=0, keepdims=True)
    le = jnp.zeros((eg, tm), F32)
    for g in range(N_EXPERT_GROUPS):
        le = jnp.where(grp_idx == float(g), lt[eg * (g + 1):eg * (g + 2)], le)
    m1, i1, le, row = _top1(le)
    z = jnp.sum(jnp.exp(le - m1), axis=0, keepdims=True)
    m2, i2, _, _ = _top1(jnp.where(row == i1, -jnp.inf, le))
    p1 = 1.0 / z
    p2 = jnp.exp(m2 - m1) / z
    tot = p1 + p2
    w1 = grp_p * (p1 / tot)
    w2 = grp_p * (p2 / tot)
    e1 = grp_idx * eg + i1
    e2 = grp_idx * eg + i2

    erow = lax.broadcasted_iota(jnp.int32, (N_EXPERTS, tm), 0).astype(F32)
    onehot = jnp.where(erow == e1, 1.0, jnp.where(erow == e2, 1.0, 0.0))
    before = _dot(onehot.astype(BF16), before_ref[...]) + carry_ref[...]
    rank1 = jnp.sum(jnp.where(erow == e1, before, 0.0), axis=0, keepdims=True)
    rank2 = jnp.sum(jnp.where(erow == e2, before, 0.0), axis=0, keepdims=True)
    carry = carry_ref[...] + jnp.sum(onehot, axis=1, keepdims=True)
    carry_ref[...] = carry
    cnt_ref[...] = carry

    mrow = lax.broadcasted_iota(jnp.int32, (META_ROWS, tm), 0)
    meta_t_ref[...] = jnp.where(mrow == 0, e1, jnp.where(mrow == 1, e2, jnp.where(
        mrow == 2, rank1, jnp.where(mrow == 3, rank2, jnp.where(
            mrow == 4, w1, jnp.where(mrow == 5, w2, 0.0))))))


def _earlier_matrix(tm):
    return np.triu(np.ones((tm, tm), np.float32), k=1)


def _router(h1, g_ffn, w_router, b_router, tm):
    t = h1.shape[0]
    return pl.pallas_call(
        _router_kernel,
        grid=(t // tm,),
        in_specs=[
            pl.BlockSpec((tm, D_MODEL), lambda i: (i, 0)),
            pl.BlockSpec((1, D_MODEL), lambda i: (0, 0)),
            pl.BlockSpec((D_MODEL, ROUTER_W), lambda i: (0, 0)),
            pl.BlockSpec((1, ROUTER_W), lambda i: (0, 0)),
            pl.BlockSpec((tm, tm), lambda i: (0, 0)),
        ],
        out_specs=[
            pl.BlockSpec((2, tm, SC_ROW), lambda i: (0, i, 0)),
            pl.BlockSpec((META_ROWS, tm), lambda i: (0, i)),
            pl.BlockSpec((N_EXPERTS, 1), lambda i: (0, 0)),
        ],
        out_shape=[
            jax.ShapeDtypeStruct((2, t, SC_ROW), U32),
            jax.ShapeDtypeStruct((META_ROWS, t), F32),
            jax.ShapeDtypeStruct((N_EXPERTS, 1), F32),
        ],
        scratch_shapes=[pltpu.VMEM((N_EXPERTS, 1), F32)],
        compiler_params=_params(("arbitrary",)),
        name="router",
    )(h1, g_ffn, w_router, b_router, jnp.asarray(_earlier_matrix(tm), BF16))


def _plan(meta_t, counts, n_tiles):
    e1 = meta_t[0].astype(jnp.int32)
    e2 = meta_t[1].astype(jnp.int32)
    rank1 = meta_t[2].astype(jnp.int32)
    rank2 = meta_t[3].astype(jnp.int32)
    cnt = counts[:, 0].astype(jnp.int32)
    padded = ((cnt + ROW_TILE - 1) // ROW_TILE) * ROW_TILE
    ends = jnp.cumsum(padded)
    starts = ends - padded
    experts = jnp.arange(N_EXPERTS, dtype=jnp.int32)
    pos1 = rank1 + jnp.sum(jnp.where(e1[None, :] == experts[:, None], starts[:, None], 0), axis=0)
    pos2 = rank2 + jnp.sum(jnp.where(e2[None, :] == experts[:, None], starts[:, None], 0), axis=0)
    tile_start = jnp.arange(n_tiles, dtype=jnp.int32) * ROW_TILE
    tile_expert = jnp.sum((tile_start[:, None] >= ends[None, :]).astype(jnp.int32), axis=1)
    tile_expert = jnp.minimum(tile_expert, N_EXPERTS - 1)
    rows_left = jnp.sum(jnp.where(tile_expert[:, None] == experts, cnt + starts, 0), axis=1) - tile_start
    n_valid = jnp.clip(rows_left, 0, ROW_TILE).astype(jnp.int32)
    last_used = jnp.maximum(ends[-1] // ROW_TILE - 1, 0)
    block = jnp.minimum(jnp.arange(n_tiles, dtype=jnp.int32), last_used)
    tile_expert = jnp.sum(jnp.where(block[:, None] == jnp.arange(n_tiles)[None, :],
                                    tile_expert[None, :], 0), axis=1)
    plane = n_tiles * ROW_TILE
    half_rows = jnp.concatenate([pos1, pos1 + plane, pos2, pos2 + plane])[None]
    return half_rows, tile_expert, n_valid, block


def _sc_mesh():
    return plsc.VectorSubcoreMesh(core_axis_name="c", subcore_axis_name="s")


def _sc_scatter_rows(rows, idx, n_out):
    t, width = rows.shape
    steps = t // SC_WINDOW

    @pl.kernel(out_type=jax.ShapeDtypeStruct((n_out, width), rows.dtype), mesh=_sc_mesh(),
               scratch_types=[], name="moe_scatter")
    def scatter(rows_hbm, idx_hbm, out_hbm):
        def body(rows_vmem, idx_vmem):
            pltpu.sync_copy(rows_vmem, out_hbm.at[idx_vmem.at[0]])

        pltpu.emit_pipeline(
            body,
            grid=(2, steps),
            in_specs=[pl.BlockSpec((SC_WINDOW, width), lambda k, j: (j, 0)),
                      pl.BlockSpec((1, SC_WINDOW), lambda k, j: (0, k * steps + j))],
            out_specs=[],
            core_axis_name=("c", "s"),
            dimension_semantics=(pltpu.PARALLEL, pltpu.PARALLEL),
        )(rows_hbm, idx_hbm)

    return scatter(rows, idx)


def _sc_gather_rows(table, idx):
    m = idx.shape[1]
    width = table.shape[1]
    steps = m // (2 * SC_WINDOW)

    @pl.kernel(out_type=jax.ShapeDtypeStruct((m, width), table.dtype), mesh=_sc_mesh(),
               scratch_types=[], name="moe_gather")
    def gather(table_hbm, idx_hbm, out_hbm):
        def body(idx_vmem, out_vmem):
            pltpu.sync_copy(table_hbm.at[idx_vmem.at[0]], out_vmem)

        pltpu.emit_pipeline(
            body,
            grid=(2, steps),
            in_specs=[pl.BlockSpec((1, SC_WINDOW), lambda k, j: (0, k * steps + j))],
            out_specs=[pl.BlockSpec((SC_WINDOW, width), lambda k, j: (k * steps + j, 0))],
            core_axis_name=("c", "s"),
            dimension_semantics=(pltpu.PARALLEL, pltpu.PARALLEL),
        )(idx_hbm, out_hbm)

    return gather(table, idx)


def _experts_kernel(te_ref, nv_ref, blk_ref, xs_ref, wg_ref, wu_ref, wd_ref, ys_ref):
    r = pl.program_id(0)
    n_valid = nv_ref[r]

    def run(start, size):
        rows = slice(start, start + size)
        parts = [p.astype(BF16)
                 for p in _unpack_rows(xs_ref[0, rows, :]) + _unpack_rows(xs_ref[1, rows, :])]
        cols = (0, 2 * SC_ROW, SC_ROW, 3 * SC_ROW)
        hg = sum(_dot(p, wg_ref[c:c + SC_ROW, :]) for p, c in zip(parts, cols))
        hu = sum(_dot(p, wu_ref[c:c + SC_ROW, :]) for p, c in zip(parts, cols))
        row = start + lax.broadcasted_iota(jnp.int32, (size, 1), 0)
        hid = jnp.where(row < n_valid, hg * _sigmoid(hg) * hu, 0.0).astype(BF16)
        packed = _pack_rows(_dot(hid, wd_ref[...]))
        ys_ref[0, rows, :] = packed[:, :SC_ROW]
        ys_ref[1, rows, :] = packed[:, SC_ROW:]

    n_sub = ROW_TILE // SUB_TILE

    @pl.when(n_valid > (n_sub - 1) * SUB_TILE)
    def _():
        run(0, ROW_TILE)

    for q in range(n_sub - 1):
        @pl.when((n_valid > q * SUB_TILE) & (n_valid <= (n_sub - 1) * SUB_TILE))
        def _(q=q):
            run(q * SUB_TILE, SUB_TILE)


def _experts(xs, tile_expert, n_valid, block, w_gate, w_up, w_down):
    n_tiles = xs.shape[1] // ROW_TILE
    w_spec = pl.BlockSpec((None, D_MODEL, D_EXPERT), lambda r, te, nv, blk: (te[r], 0, 0))
    grid_spec = pltpu.PrefetchScalarGridSpec(
        num_scalar_prefetch=3,
        grid=(n_tiles,),
        in_specs=[
            pl.BlockSpec((2, ROW_TILE, SC_ROW), lambda r, te, nv, blk: (0, blk[r], 0)),
            w_spec, w_spec,
            pl.BlockSpec((None, D_EXPERT, D_MODEL), lambda r, te, nv, blk: (te[r], 0, 0)),
        ],
        out_specs=pl.BlockSpec((2, ROW_TILE, SC_ROW), lambda r, te, nv, blk: (0, blk[r], 0)),
    )
    return pl.pallas_call(
        _experts_kernel,
        grid_spec=grid_spec,
        out_shape=jax.ShapeDtypeStruct((2, n_tiles * ROW_TILE, SC_ROW), U32),
        compiler_params=_params(("arbitrary",)),
        name="experts",
    )(tile_expert, n_valid, block, xs, w_gate, w_up, w_down)


def _ple_kernel(h_ref, yg_ref, meta_t_ref, p_ref, wg_ref, bg_ref, wp_ref, gf_ref, *rest, final_norm):
    o_ref = rest[-1]
    meta = jnp.transpose(meta_t_ref[...])
    w1 = meta[:, 4:5]
    w2 = meta[:, 5:6]
    q0, q2 = (w1 * u + w2 * v for u, v in zip(_unpack_rows(yg_ref[0]), _unpack_rows(yg_ref[2])))
    q1, q3 = (w1 * u + w2 * v for u, v in zip(_unpack_rows(yg_ref[1]), _unpack_rows(yg_ref[3])))
    moe = jnp.concatenate([q0, q1, q2, q3], axis=1)
    h = h_ref[...] + moe
    gate = _sigmoid(_dot(_rms(h).astype(BF16), wg_ref[...]) + bg_ref[...])
    h = h + gate * _dot(p_ref[...].astype(BF16), wp_ref[...])
    o_ref[...] = _rms(h, gf_ref[...]) if final_norm else h


def _ple(h1, yg, meta_t, p2, w_gate, b_gate, w_proj, g_final, final_norm, tm, part, prev_out):
    t = h1.shape[0]
    steps = t // tm // GATHER_PARTS
    off = part * steps
    in_specs = [
        pl.BlockSpec((tm, D_MODEL), lambda i: (i + off, 0)),
        pl.BlockSpec((4, tm, SC_ROW), lambda i: (0, i, 0)),
        pl.BlockSpec((META_ROWS, tm), lambda i: (0, i + off)),
        pl.BlockSpec((tm, D_PLE), lambda i: (i + off, 0)),
        pl.BlockSpec((D_MODEL, D_MODEL), lambda i: (0, 0)),
        pl.BlockSpec((1, D_MODEL), lambda i: (0, 0)),
        pl.BlockSpec((D_PLE, D_MODEL), lambda i: (0, 0)),
        pl.BlockSpec((1, D_MODEL), lambda i: (0, 0)),
    ]
    args = [h1, yg, meta_t, p2, w_gate, b_gate, w_proj, g_final]
    aliases = {}
    if prev_out is not None:
        in_specs.append(pl.BlockSpec(memory_space=pl.ANY))
        args.append(prev_out)
        aliases = {len(args) - 1: 0}
    return pl.pallas_call(
        functools.partial(_ple_kernel, final_norm=final_norm),
        grid=(steps,),
        in_specs=in_specs,
        out_specs=pl.BlockSpec((tm, D_MODEL), lambda i: (i + off, 0)),
        out_shape=jax.ShapeDtypeStruct((t, D_MODEL), F32),
        input_output_aliases=aliases,
        compiler_params=_params(("parallel",)),
        name="ple",
    )(*args)


def kernel(x, p, g_mix, w_in, pool_w, pool_scale, ssm_a_re, ssm_a_im, ssm_log_dt, ssm_b_re,
           ssm_b_im, ssm_c_re, ssm_c_im, ssm_d, glu_w, glu_b, w_out, g_ffn, router_grp_w,
           router_grp_b, router_exp_w, router_exp_b, exp_w_gate, exp_w_up, exp_w_down, g_ple,
           ple_gate_w, ple_gate_b, ple_proj_w, g_final):
    bsz, seq, dm = x.shape
    depth = g_mix.shape[0]
    t = bsz * seq
    seq_chunks = seq // CHUNK
    nch = t // CHUNK
    tm = 1024
    n_sorted = 2 * t + N_EXPERTS * ROW_TILE
    w_gate_all = exp_w_gate.reshape(depth * N_EXPERTS, dm, D_EXPERT)
    w_up_all = exp_w_up.reshape(depth * N_EXPERTS, dm, D_EXPERT)
    w_down_all = exp_w_down.reshape(depth * N_EXPERTS, D_EXPERT, dm)

    h = x.reshape(t, dm)
    for i in range(depth):
        w_in_b = (g_mix[i][:, None] * w_in[i]).astype(BF16)
        zp, ut = _in_proj(h.reshape(nch, CHUNK, dm), w_in_b[:, :D_POOL],
                          jnp.transpose(w_in_b[:, D_POOL:]))
        a = _pool(zp.reshape(bsz, seq, D_POOL), pool_w[i], pool_scale[i][None])
        yt, w_gate_b, w_up_b, w_down_b = _ssm(
            ut, ssm_a_re[i], ssm_a_im[i], ssm_log_dt[i], ssm_b_re[i], ssm_b_im[i], ssm_c_re[i],
            ssm_c_im[i], ssm_d[i], seq_chunks, w_gate_all, w_up_all, w_down_all, i)
        h = _mix_out(h.reshape(nch, CHUNK, dm), a.reshape(nch, CHUNK, D_POOL), yt,
                     jnp.transpose(glu_w[i]).astype(BF16), glu_b[i][:, None],
                     w_out[i].astype(BF16)).reshape(t, dm)

        eg = EXPERTS_PER_GROUP
        w_router = jnp.concatenate(
            [router_grp_w[i], jnp.zeros((dm, eg - N_EXPERT_GROUPS), F32),
             jnp.transpose(router_exp_w[i], (1, 0, 2)).reshape(dm, N_EXPERTS),
             jnp.zeros((dm, ROUTER_W - eg - N_EXPERTS), F32)], axis=1)
        b_router = jnp.concatenate(
            [router_grp_b[i], jnp.zeros((eg - N_EXPERT_GROUPS,), F32),
             router_exp_b[i].reshape(N_EXPERTS),
             jnp.zeros((ROUTER_W - eg - N_EXPERTS,), F32)])[None]
        vp, meta_t, counts = _router(h, g_ffn[i][None], w_router, b_router, tm)
        idx, tile_expert, n_valid, block = _plan(meta_t, counts, n_sorted // ROW_TILE)
        xs = _sc_scatter_rows(vp.reshape(2 * t, SC_ROW), idx, 2 * n_sorted)
        ys = _experts(xs.reshape(2, n_sorted, SC_ROW), tile_expert, n_valid, block,
                      w_gate_b, w_up_b, w_down_b)
        ys2 = ys.reshape(2 * n_sorted, SC_ROW)
        idx4 = idx.reshape(4, t)
        tp = t // GATHER_PARTS
        ple_wg = (g_ple[i][:, None] * ple_gate_w[i]).astype(BF16)
        ple_wp = ple_proj_w[i].astype(BF16)
        out = None
        for q in range(GATHER_PARTS):
            yg_q = _sc_gather_rows(ys2, idx4[:, q * tp:(q + 1) * tp].reshape(1, 4 * tp))
            out = _ple(h, yg_q.reshape(4, tp, SC_ROW), meta_t, p[i].reshape(t, D_PLE), ple_wg,
                       ple_gate_b[i][None], ple_wp, g_final[None], i == depth - 1, tm, q, out)
        h = out
    return h.reshape(bsz, seq, dm)
```

```python
import functools
import math

import numpy as np
import jax
import jax.numpy as jnp
from jax import lax
from jax.experimental import pallas as pl
from jax.experimental.pallas import tpu as pltpu
from jax.experimental.pallas import tpu_sc as plsc

F32 = jnp.float32
BF16 = jnp.bfloat16
U32 = jnp.uint32

D_MODEL = 1024
D_POOL = 512
D_SSM = 512
POOL_WINDOWS = (2, 4, 8, 16)
POOL_GROUP = 128
SSM_GROUP = 16
N_SSM_GROUPS = 32
SSM_STATE = 64
N_EXPERT_GROUPS = 4
EXPERTS_PER_GROUP = 8
N_EXPERTS = N_EXPERT_GROUPS * EXPERTS_PER_GROUP
D_EXPERT = 256
D_PLE = 256
RMS_EPS = 1e-6

LANES = 128
CHUNK = 32
CHUNK_W = CHUNK * SSM_GROUP
T_SUB = 8
S_TILE = 8
C_TILE = 128
ROUTER_W = LANES
HALF = D_MODEL // 2
ROW_TILE = 1024
SC_WINDOW = 128
SC_ROW = HALF // 2
META_ROWS = 8
GATHER_PARTS = 2
VMEM_LIMIT = 56 * 1024 * 1024


def _dot(a, b):
    return jnp.dot(a, b, preferred_element_type=F32)


def _dot_nt(a, b):
    return lax.dot_general(a, b, (((1,), (1,)), ((), ())), preferred_element_type=F32)


def _dot_tn(a, b):
    return lax.dot_general(a, b, (((0,), (0,)), ((), ())), preferred_element_type=F32)


def _rms(x, g=None):
    y = x * lax.rsqrt(jnp.mean(x * x, axis=-1, keepdims=True) + RMS_EPS)
    return y if g is None else y * g


def _sigmoid(x):
    return 1.0 / (1.0 + jnp.exp(-x))


def _params(sem):
    return pltpu.CompilerParams(dimension_semantics=sem, vmem_limit_bytes=VMEM_LIMIT)


def _tile_step():
    n_s = pl.num_programs(1)
    return pl.program_id(0) * n_s + pl.program_id(1), pl.num_programs(0) * n_s


def _row_copies(hbm, buf, sem, step_idx, slot_idx, to_hbm):
    n_s = pl.num_programs(1)
    c0 = (step_idx // n_s) * C_TILE
    s0 = (step_idx % n_s) * S_TILE
    out = []
    for j in range(S_TILE):
        far, near = hbm.at[pl.ds(c0, C_TILE), s0 + j, :], buf.at[slot_idx, j]
        src, dst = (near, far) if to_hbm else (far, near)
        out.append(pltpu.make_async_copy(src, dst, sem.at[slot_idx]))
    return out


def _load_time_major_rows(hbm, buf, sem):
    step, n_steps = _tile_step()
    slot = step % 2

    @pl.when(step == 0)
    def _():
        for cp in _row_copies(hbm, buf, sem, step, slot, False):
            cp.start()

    @pl.when(step + 1 < n_steps)
    def _():
        for cp in _row_copies(hbm, buf, sem, step + 1, 1 - slot, False):
            cp.start()

    for cp in _row_copies(hbm, buf, sem, step, slot, False):
        cp.wait()
    return buf[slot].reshape(S_TILE * C_TILE, buf.shape[-1])


def _store_time_major_rows(val, hbm, buf, sem):
    step, n_steps = _tile_step()
    slot = step % 2

    @pl.when(step >= 2)
    def _():
        for cp in _row_copies(hbm, buf, sem, step - 2, slot, True):
            cp.wait()

    buf[slot] = val.reshape(S_TILE, C_TILE, val.shape[-1])
    for cp in _row_copies(hbm, buf, sem, step, slot, True):
        cp.start()

    @pl.when(step == n_steps - 1)
    def _():
        @pl.when(step >= 1)
        def _():
            for cp in _row_copies(hbm, buf, sem, step - 1, 1 - slot, True):
                cp.wait()
        for cp in _row_copies(hbm, buf, sem, step, slot, True):
            cp.wait()


def _in_proj_kernel(x_hbm, wp_ref, wst_ref, zp_hbm, ut_ref, xbuf, zbuf, xsem, zsem):
    nc = C_TILE
    u = _rms(_load_time_major_rows(x_hbm, xbuf, xsem)).astype(BF16)
    zt = _dot_nt(wst_ref[...], u).astype(BF16)
    for j in range(S_TILE):
        ut_ref[:, j, :, :] = zt[:, j * nc:(j + 1) * nc].reshape(N_SSM_GROUPS, SSM_GROUP, nc)
    _store_time_major_rows(_dot(u, wp_ref[...]), zp_hbm, zbuf, zsem)


def _in_proj(x3, w_pool, w_ssm_t):
    nch = x3.shape[0]
    return pl.pallas_call(
        _in_proj_kernel,
        grid=(nch // C_TILE, CHUNK // S_TILE),
        in_specs=[
            pl.BlockSpec(memory_space=pl.ANY),
            pl.BlockSpec((D_MODEL, D_POOL), lambda c, s: (0, 0)),
            pl.BlockSpec((D_SSM, D_MODEL), lambda c, s: (0, 0)),
        ],
        out_specs=[
            pl.BlockSpec(memory_space=pl.ANY),
            pl.BlockSpec((N_SSM_GROUPS, None, S_TILE, SSM_GROUP, C_TILE), lambda c, s: (0, c, s, 0, 0)),
        ],
        out_shape=[
            jax.ShapeDtypeStruct((nch, CHUNK, D_POOL), F32),
            jax.ShapeDtypeStruct((N_SSM_GROUPS, nch // C_TILE, CHUNK, SSM_GROUP, C_TILE), BF16),
        ],
        scratch_shapes=[
            pltpu.VMEM((2, S_TILE, C_TILE, D_MODEL), F32),
            pltpu.VMEM((2, S_TILE, C_TILE, D_POOL), F32),
            pltpu.SemaphoreType.DMA((2,)),
            pltpu.SemaphoreType.DMA((2,)),
        ],
        compiler_params=_params(("arbitrary", "arbitrary")),
        name="in_proj",
    )(x3, w_pool, w_ssm_t)


def _shift_rows(x, d, row, n):
    if d == 0:
        return x
    r = pltpu.roll(x, d % n, 0)
    if d > 0:
        return jnp.where(row >= d, r, 0.0)
    return jnp.where(row < n + d, r, 0.0)


def _pool_kernel(z_ref, w_ref, sc_ref, o_ref):
    n = z_ref.shape[0]
    gi = pl.program_id(1)
    row = lax.broadcasted_iota(jnp.int32, (n, 1), 0)

    for k, w in enumerate(POOL_WINDOWS):
        @pl.when(gi == k)
        def _(w=w):
            x = z_ref[...]
            half = w // 2
            pd, pu, span = x, x, 1
            while span < half:
                pd = pd + _shift_rows(pd, span, row, n)
                pu = pu + _shift_rows(pu, -span, row, n)
                span *= 2
            total = _shift_rows(pd, 1, row, n) + pu
            lo = jnp.maximum(row - half, 0)
            hi = jnp.minimum(row + half, n)
            cnt = (hi - lo).astype(F32)
            diff = (total / cnt - x).astype(BF16)
            o_ref[...] = _dot(diff, w_ref[...].astype(BF16)) * sc_ref[...]


def _pool(zp3, pool_w, pool_scale):
    b, s, _ = zp3.shape
    return pl.pallas_call(
        _pool_kernel,
        grid=(b, len(POOL_WINDOWS)),
        in_specs=[
            pl.BlockSpec((None, s, POOL_GROUP), lambda i, g: (i, 0, g)),
            pl.BlockSpec((None, POOL_GROUP, POOL_GROUP), lambda i, g: (g, 0, 0)),
            pl.BlockSpec((1, POOL_GROUP), lambda i, g: (0, g)),
        ],
        out_specs=pl.BlockSpec((None, s, POOL_GROUP), lambda i, g: (i, 0, g)),
        out_shape=jax.ShapeDtypeStruct((b, s, D_POOL), F32),
        compiler_params=_params(("parallel", "parallel")),
        name="pool",
    )(zp3, pool_w, pool_scale)


def _expand_consts():
    time = np.arange(CHUNK_W) // SSM_GROUP
    def onehot(e):
        m = np.zeros((CHUNK_W, LANES), np.float32)
        m[np.arange(CHUNK_W), e] = 1.0
        return m
    return np.stack([
        onehot(CHUNK - 1 - time),
        onehot(time),
        onehot(time + 1),
        onehot(CHUNK - time),
    ])


def _cmul_packed(x, p, q):
    return x * p + pltpu.roll(x, LANES // 2, 1) * q


def _ssm_kernel(u_ref, vec_ref, mat_ref, exp_ref, wg_ref, wu_ref, wd_ref, y_ref, og_ref, ou_ref,
                od_ref, *, seq_chunks):
    og_ref[...] = wg_ref[...].astype(BF16)
    ou_ref[...] = wu_ref[...].astype(BF16)
    od_ref[...] = wd_ref[...].astype(BF16)

    n_ct = u_ref.shape[0]
    nch = n_ct * C_TILE
    half = LANES // 2
    lane = lax.broadcasted_iota(jnp.int32, (1, LANES), 1)
    lo_half = lane < half

    def direction(di):
        a_re = vec_ref[di, 0:1]
        a_im = vec_ref[di, 1:2]
        dt = jnp.exp(vec_ref[di, 2:3])
        mag = jnp.exp(a_re * dt)
        ang = a_im * dt
        lam = jnp.where(lo_half, mag * jnp.cos(ang), mag * jnp.sin(ang))
        lb_re = mag * jnp.cos(ang)
        lb_im = mag * jnp.sin(ang)
        den = a_re * a_re + a_im * a_im
        f_re = ((lb_re - 1.0) * a_re + lb_im * a_im) / den
        f_im = (lb_im * a_re - (lb_re - 1.0) * a_im) / den
        return lam, f_re, f_im

    def power_table(lam):
        e = lax.broadcasted_iota(jnp.int32, (LANES, 1), 0)
        tab = jnp.where(lo_half, 1.0, 0.0) * jnp.ones((LANES, 1), F32)
        sq = lam
        for k in range(7):
            p = jnp.where(lo_half, sq, pltpu.roll(sq, half, 1))
            q = jnp.where(lo_half, -pltpu.roll(sq, half, 1), sq)
            tab = jnp.where(((e >> k) & 1) == 1, _cmul_packed(tab, p, q), tab)
            sq = _cmul_packed(sq, p, q)
        return tab

    def tile_rows(x16):
        return jnp.broadcast_to(x16[None], (CHUNK, SSM_GROUP, LANES)).reshape(CHUNK_W, LANES)

    def expanded(tab, which, v_re, v_im, conj_sign):
        lexp = _dot(exp_ref[which], tab.astype(BF16))
        if conj_sign > 0:
            p = jnp.where(lo_half, v_re, v_re)
            q = jnp.where(lo_half, -v_im, v_im)
        else:
            p = jnp.where(lo_half, v_re, -v_re)
            q = jnp.where(lo_half, -v_im, -v_im)
        return lexp * tile_rows(p) + pltpu.roll(lexp, half, 1) * tile_rows(q)

    lam_f, ff_re, ff_im = direction(0)
    lam_b, fb_re, fb_im = direction(1)
    tab_f = power_table(lam_f)
    tab_b = power_table(lam_b)

    def bbar(bt_re, bt_im, f_re, f_im):
        return bt_re * f_re - bt_im * f_im, bt_re * f_im + bt_im * f_re

    def mat(di, k):
        return mat_ref[di, k * SSM_GROUP:(k + 1) * SSM_GROUP, :]

    bf_re, bf_im = bbar(mat(0, 0), mat(0, 1), ff_re, ff_im)
    bb_re, bb_im = bbar(mat(1, 0), mat(1, 1), fb_re, fb_im)

    pb1 = expanded(tab_f, 0, bf_re, bf_im, 1)
    pb2 = expanded(tab_b, 1, bb_re, bb_im, 1)
    pb3 = expanded(tab_b, 2, bb_re, bb_im, 1)
    ft_f = expanded(tab_f, 2, mat(0, 3), mat(0, 4), -1)
    ft_b = expanded(tab_b, 3, mat(1, 3), mat(1, 4), -1)

    row_w = lax.broadcasted_iota(jnp.int32, (CHUNK_W, 1), 0)
    last_blk = row_w >= CHUNK_W - SSM_GROUP
    pb2_lag0 = jnp.where(last_blk, pltpu.roll(pb2, CHUNK_W - SSM_GROUP, 0), 0.0)
    ccr_f = mat(0, 2).astype(BF16)
    ccr_b = mat(1, 2).astype(BF16)
    r_lo = _dot_nt(ccr_f, pb1.astype(BF16)) + _dot_nt(ccr_b, pb2_lag0.astype(BF16))
    co = lax.broadcasted_iota(jnp.int32, (SSM_GROUP, CHUNK_W), 0)
    col = lax.broadcasted_iota(jnp.int32, (SSM_GROUP, CHUNK_W), 1)
    r_lo = r_lo + jnp.where(col == CHUNK_W - SSM_GROUP + co, mat(0, 5)[:, 0:1], 0.0)
    r_hi = _dot_nt(ccr_b, pb3.astype(BF16))
    r_t = jnp.concatenate([r_lo, r_hi], axis=1)
    g_t = jnp.concatenate(
        [pltpu.roll(r_t, SSM_GROUP * (tl + 1), 1) for tl in range(T_SUB)], axis=0
    ).astype(BF16)

    u = jnp.concatenate([u_ref[ct].reshape(CHUNK_W, C_TILE) for ct in range(n_ct)],
                        axis=1)
    e_mat = jnp.concatenate([pb1, pb2], axis=1).astype(BF16)
    xend = _dot_tn(e_mat, u)
    lanec = lax.broadcasted_iota(jnp.int32, (1, nch), 1) % seq_chunks
    ns = SSM_STATE

    def scan(re, im, tab, forward):
        lam_col = jnp.transpose(tab[CHUNK:CHUNK + 8, :])[:, 0:1]
        a, b = lam_col[:ns], lam_col[ns:]
        n_steps = int(math.log2(seq_chunks))

        def shifted(v, d):
            if forward:
                return jnp.where(lanec >= d, pltpu.roll(v, d, 1), 0.0)
            return jnp.where(lanec < seq_chunks - d, pltpu.roll(v, nch - d, 1), 0.0)

        for k in range(n_steps):
            sr, si = shifted(re, 1 << k), shifted(im, 1 << k)
            re, im = re + (sr * a - si * b), im + (sr * b + si * a)
            a, b = a * a - b * b, 2.0 * a * b
        return shifted(re, 1), shifted(im, 1)

    f_re, f_im = scan(xend[:ns], xend[ns:2 * ns], tab_f, True)
    b_re, b_im = scan(xend[2 * ns:3 * ns], xend[3 * ns:], tab_b, False)
    xin = jnp.concatenate([f_re, f_im, b_re, b_im], axis=0).astype(BF16)
    f_t = jnp.concatenate([ft_f, ft_b], axis=1).astype(BF16)

    toeplitz = jnp.concatenate(
        [g_t[:, CHUNK_W - LANES * th:2 * CHUNK_W - LANES * th] for th in range(CHUNK // T_SUB)],
        axis=0)
    y_t = _dot(toeplitz, u) + _dot(f_t, xin)
    for ct in range(n_ct):
        y_ref[ct] = y_t[:, ct * C_TILE:(ct + 1) * C_TILE].reshape(
            CHUNK, SSM_GROUP, C_TILE).astype(y_ref.dtype)


def _ssm(ut, a_re, a_im, log_dt, b_re, b_im, c_re, c_im, d, seq_chunks, w_gate, w_up, w_down, layer):
    g, n_ct = ut.shape[:2]
    assert N_EXPERTS % g == 0
    epg = N_EXPERTS // g
    base = layer * g
    n = SSM_STATE

    def per_group(a):
        return jnp.swapaxes(a, 0, 1)

    def dup(a):
        return jnp.concatenate([a, a], axis=-1)

    vecs = dup(jnp.stack([per_group(a_re), per_group(a_im),
                          jnp.broadcast_to(per_group(log_dt)[..., None], (g, 2, n))], axis=2))
    cr, ci = per_group(c_re), per_group(c_im)
    d_blk = jnp.broadcast_to(d.reshape(g, 1, SSM_GROUP, 1), (g, 2, SSM_GROUP, LANES))
    mats = jnp.concatenate(
        [dup(jnp.swapaxes(per_group(b_re), 2, 3)), dup(jnp.swapaxes(per_group(b_im), 2, 3)),
         jnp.concatenate([cr, -ci], axis=-1), dup(cr), dup(ci), d_blk], axis=2)
    exp_c = jnp.asarray(_expand_consts(), BF16)

    return pl.pallas_call(
        functools.partial(_ssm_kernel, seq_chunks=seq_chunks),
        grid=(g,),
        in_specs=[
            pl.BlockSpec((None, n_ct, CHUNK, SSM_GROUP, C_TILE), lambda i: (i, 0, 0, 0, 0)),
            pl.BlockSpec((None, 2, 3, LANES), lambda i: (i, 0, 0, 0)),
            pl.BlockSpec((None, 2, 6 * SSM_GROUP, LANES), lambda i: (i, 0, 0, 0)),
            pl.BlockSpec((4, CHUNK_W, LANES), lambda i: (0, 0, 0)),
            pl.BlockSpec((epg, D_MODEL, D_EXPERT), lambda i: (base + i, 0, 0)),
            pl.BlockSpec((epg, D_MODEL, D_EXPERT), lambda i: (base + i, 0, 0)),
            pl.BlockSpec((epg, D_EXPERT, D_MODEL), lambda i: (base + i, 0, 0)),
        ],
        out_specs=[
            pl.BlockSpec((None, n_ct, CHUNK, SSM_GROUP, C_TILE), lambda i: (i, 0, 0, 0, 0)),
            pl.BlockSpec((epg, D_MODEL, D_EXPERT), lambda i: (i, 0, 0)),
            pl.BlockSpec((epg, D_MODEL, D_EXPERT), lambda i: (i, 0, 0)),
            pl.BlockSpec((epg, D_EXPERT, D_MODEL), lambda i: (i, 0, 0)),
        ],
        out_shape=[
            jax.ShapeDtypeStruct(ut.shape, BF16),
            jax.ShapeDtypeStruct((N_EXPERTS, D_MODEL, D_EXPERT), BF16),
            jax.ShapeDtypeStruct((N_EXPERTS, D_MODEL, D_EXPERT), BF16),
            jax.ShapeDtypeStruct((N_EXPERTS, D_EXPERT, D_MODEL), BF16),
        ],
        compiler_params=_params(("parallel",)),
        name="ssm",
    )(ut, vecs, mats, exp_c, w_gate, w_up, w_down)


def _mix_out_kernel(x_hbm, a_hbm, yt_ref, gwt_ref, gb_ref, wo_ref, h_hbm, xbuf, abuf, hbuf,
                    xsem, asem, hsem):
    nc = C_TILE
    x = _load_time_major_rows(x_hbm, xbuf, xsem)
    a = _load_time_major_rows(a_hbm, abuf, asem).astype(BF16)
    y = jnp.concatenate([yt_ref[:, j, :, :].reshape(D_SSM, nc) for j in range(S_TILE)],
                        axis=1).astype(F32)
    z = 0.5 * y * (1.0 + jnp.tanh(math.sqrt(2.0 / math.pi) * (y + 0.044715 * (y * y * y))))
    gate = _sigmoid(_dot(gwt_ref[...], z.astype(BF16)) + gb_ref[...])
    s = (z * gate).astype(BF16)
    h = x + _dot(a, wo_ref[:D_POOL, :]) + _dot_tn(s, wo_ref[D_POOL:, :])
    _store_time_major_rows(h, h_hbm, hbuf, hsem)


def _mix_out(x3, a3, yt, glu_w_t, glu_b_col, w_out):
    nch = x3.shape[0]
    return pl.pallas_call(
        _mix_out_kernel,
        grid=(nch // C_TILE, CHUNK // S_TILE),
        in_specs=[
            pl.BlockSpec(memory_space=pl.ANY),
            pl.BlockSpec(memory_space=pl.ANY),
            pl.BlockSpec((N_SSM_GROUPS, None, S_TILE, SSM_GROUP, C_TILE), lambda c, t: (0, c, t, 0, 0)),
            pl.BlockSpec((D_SSM, D_SSM), lambda c, t: (0, 0)),
            pl.BlockSpec((D_SSM, 1), lambda c, t: (0, 0)),
            pl.BlockSpec((D_MODEL, D_MODEL), lambda c, t: (0, 0)),
        ],
        out_specs=pl.BlockSpec(memory_space=pl.ANY),
        out_shape=jax.ShapeDtypeStruct((nch, CHUNK, D_MODEL), F32),
        scratch_shapes=[
            pltpu.VMEM((2, S_TILE, C_TILE, D_MODEL), F32),
            pltpu.VMEM((2, S_TILE, C_TILE, D_POOL), F32),
            pltpu.VMEM((2, S_TILE, C_TILE, D_MODEL), F32),
            pltpu.SemaphoreType.DMA((2,)),
            pltpu.SemaphoreType.DMA((2,)),
            pltpu.SemaphoreType.DMA((2,)),
        ],
        compiler_params=_params(("arbitrary", "arbitrary")),
        name="mix_out",
    )(x3, a3, yt, glu_w_t, glu_b_col, w_out)


def _pack_rows(x):
    b = lax.bitcast_convert_type(x.astype(BF16).astype(F32), U32)
    return (b[:, :HALF] & jnp.uint32(0xFFFF0000)) | (b[:, HALF:] >> 16)


def _unpack_rows(w):
    lo = lax.bitcast_convert_type(w & jnp.uint32(0xFFFF0000), F32)
    hi = lax.bitcast_convert_type(w << 16, F32)
    return lo, hi


def _split_bf16(x):
    hi = x.astype(BF16)
    return hi, (x - hi.astype(F32)).astype(BF16)


def _route(v32, wr_ref, br_ref):
    v_hi, v_lo = _split_bf16(v32)
    w_hi, w_lo = _split_bf16(wr_ref[...])
    logits = _dot(v_hi, w_hi) + (_dot(v_lo, w_hi) + _dot(v_hi, w_lo)) + br_ref[...]
    return jnp.transpose(logits)


def _top1(x, valid=None):
    n = x.shape[0]
    row = lax.broadcasted_iota(jnp.int32, x.shape, 0).astype(F32)
    if valid is not None:
        x = jnp.where(valid, x, -jnp.inf)
    m = jnp.max(x, axis=0, keepdims=True)
    idx = jnp.min(jnp.where(x == m, row, float(n)), axis=0, keepdims=True)
    return m, idx, x, row


def _split_planes(packed, ref):
    ref[0] = packed[:, :SC_ROW]
    ref[1] = packed[:, SC_ROW:]


def _router_kernel(h_ref, g_ref, wr_ref, br_ref, before_ref, vp_ref, meta_t_ref, cnt_ref, carry_ref):
    @pl.when(pl.program_id(0) == 0)
    def _():
        carry_ref[...] = jnp.zeros_like(carry_ref)

    v32 = _rms(h_ref[...], g_ref[...])
    _split_planes(_pack_rows(v32), vp_ref)
    lt = _route(v32, wr_ref, br_ref)
    tm = lt.shape[1]
    eg = EXPERTS_PER_GROUP

    grp = lt[:eg]
    grp_row = lax.broadcasted_iota(jnp.int32, grp.shape, 0)
    mg, grp_idx, grp, _ = _top1(grp, grp_row < N_EXPERT_GROUPS)
    grp_p = 1.0 / jnp.sum(jnp.exp(grp - mg), axis=0, keepdims=True)
    le = jnp.zeros((eg, tm), F32)
    for g in range(N_EXPERT_GROUPS):
        le = jnp.where(grp_idx == float(g), lt[eg * (g + 1):eg * (g + 2)], le)
    m1, i1, le, row = _top1(le)
    z = jnp.sum(jnp.exp(le - m1), axis=0, keepdims=True)
    m2, i2, _, _ = _top1(jnp.where(row == i1, -jnp.inf, le))
    p1 = 1.0 / z
    p2 = jnp.exp(m2 - m1) / z
    tot = p1 + p2
    w1 = grp_p * (p1 / tot)
    w2 = grp_p * (p2 / tot)
    e1 = grp_idx * eg + i1
    e2 = grp_idx * eg + i2

    erow = lax.broadcasted_iota(jnp.int32, (N_EXPERTS, tm), 0).astype(F32)
    onehot = jnp.where(erow == e1, 1.0, jnp.where(erow == e2, 1.0, 0.0))
    before = _dot(onehot.astype(BF16), before_ref[...]) + carry_ref[...]
    rank1 = jnp.sum(jnp.where(erow == e1, before, 0.0), axis=0, keepdims=True)
    rank2 = jnp.sum(jnp.where(erow == e2, before, 0.0), axis=0, keepdims=True)
    carry = carry_ref[...] + jnp.sum(onehot, axis=1, keepdims=True)
    carry_ref[...] = carry
    cnt_ref[...] = carry

    mrow = lax.broadcasted_iota(jnp.int32, (META_ROWS, tm), 0)
    meta_t_ref[...] = jnp.where(mrow == 0, e1, jnp.where(mrow == 1, e2, jnp.where(
        mrow == 2, rank1, jnp.where(mrow == 3, rank2, jnp.where(
            mrow == 4, w1, jnp.where(mrow == 5, w2, 0.0))))))


def _earlier_matrix(tm):
    return np.triu(np.ones((tm, tm), np.float32), k=1)


def _router(h1, g_ffn, w_router, b_router, tm):
    t = h1.shape[0]
    return pl.pallas_call(
        _router_kernel,
        grid=(t // tm,),
        in_specs=[
            pl.BlockSpec((tm, D_MODEL), lambda i: (i, 0)),
            pl.BlockSpec((1, D_MODEL), lambda i: (0, 0)),
            pl.BlockSpec((D_MODEL, ROUTER_W), lambda i: (0, 0)),
            pl.BlockSpec((1, ROUTER_W), lambda i: (0, 0)),
            pl.BlockSpec((tm, tm), lambda i: (0, 0)),
        ],
        out_specs=[
            pl.BlockSpec((2, tm, SC_ROW), lambda i: (0, i, 0)),
            pl.BlockSpec((META_ROWS, tm), lambda i: (0, i)),
            pl.BlockSpec((N_EXPERTS, 1), lambda i: (0, 0)),
        ],
        out_shape=[
            jax.ShapeDtypeStruct((2, t, SC_ROW), U32),
            jax.ShapeDtypeStruct((META_ROWS, t), F32),
            jax.ShapeDtypeStruct((N_EXPERTS, 1), F32),
        ],
        scratch_shapes=[pltpu.VMEM((N_EXPERTS, 1), F32)],
        compiler_params=_params(("arbitrary",)),
        name="router",
    )(h1, g_ffn, w_router, b_router, jnp.asarray(_earlier_matrix(tm), BF16))


def _plan(meta_t, counts, n_tiles):
    e1 = meta_t[0].astype(jnp.int32)
    e2 = meta_t[1].astype(jnp.int32)
    rank1 = meta_t[2].astype(jnp.int32)
    rank2 = meta_t[3].astype(jnp.int32)
    cnt = counts[:, 0].astype(jnp.int32)
    padded = ((cnt + ROW_TILE - 1) // ROW_TILE) * ROW_TILE
    ends = jnp.cumsum(padded)
    starts = ends - padded
    experts = jnp.arange(N_EXPERTS, dtype=jnp.int32)
    pos1 = rank1 + jnp.sum(jnp.where(e1[None, :] == experts[:, None], starts[:, None], 0), axis=0)
    pos2 = rank2 + jnp.sum(jnp.where(e2[None, :] == experts[:, None], starts[:, None], 0), axis=0)
    tile_start = jnp.arange(n_tiles, dtype=jnp.int32) * ROW_TILE
    tile_expert = jnp.sum((tile_start[:, None] >= ends[None, :]).astype(jnp.int32), axis=1)
    tile_expert = jnp.minimum(tile_expert, N_EXPERTS - 1)
    rows_left = jnp.sum(jnp.where(tile_expert[:, None] == experts, cnt + starts, 0), axis=1) - tile_start
    n_valid = jnp.clip(rows_left, 0, ROW_TILE).astype(jnp.int32)
    last_used = jnp.maximum(ends[-1] // ROW_TILE - 1, 0)
    block = jnp.minimum(jnp.arange(n_tiles, dtype=jnp.int32), last_used)
    tile_expert = jnp.sum(jnp.where(block[:, None] == jnp.arange(n_tiles)[None, :],
                                    tile_expert[None, :], 0), axis=1)
    plane = n_tiles * ROW_TILE
    half_rows = jnp.concatenate([pos1, pos1 + plane, pos2, pos2 + plane])[None]
    return half_rows, tile_expert, n_valid, block


def _sc_mesh():
    return plsc.VectorSubcoreMesh(core_axis_name="c", subcore_axis_name="s")


def _sc_scatter_rows(rows, idx, n_out):
    t, width = rows.shape
    steps = t // SC_WINDOW

    @pl.kernel(out_type=jax.ShapeDtypeStruct((n_out, width), rows.dtype), mesh=_sc_mesh(),
               scratch_types=[], name="moe_scatter")
    def scatter(rows_hbm, idx_hbm, out_hbm):
        def body(rows_vmem, idx_vmem):
            pltpu.sync_copy(rows_vmem, out_hbm.at[idx_vmem.at[0]])

        pltpu.emit_pipeline(
            body,
            grid=(2, steps),
            in_specs=[pl.BlockSpec((SC_WINDOW, width), lambda k, j: (j, 0)),
                      pl.BlockSpec((1, SC_WINDOW), lambda k, j: (0, k * steps + j))],
            out_specs=[],
            core_axis_name=("c", "s"),
            dimension_semantics=(pltpu.PARALLEL, pltpu.PARALLEL),
        )(rows_hbm, idx_hbm)

    return scatter(rows, idx)


def _sc_gather_rows(table, idx):
    m = idx.shape[1]
    width = table.shape[1]
    steps = m // (2 * SC_WINDOW)

    @pl.kernel(out_type=jax.ShapeDtypeStruct((m, width), table.dtype), mesh=_sc_mesh(),
               scratch_types=[], name="moe_gather")
    def gather(table_hbm, idx_hbm, out_hbm):
        def body(idx_vmem, out_vmem):
            pltpu.sync_copy(table_hbm.at[idx_vmem.at[0]], out_vmem)

        pltpu.emit_pipeline(
            body,
            grid=(2, steps),
            in_specs=[pl.BlockSpec((1, SC_WINDOW), lambda k, j: (0, k * steps + j))],
            out_specs=[pl.BlockSpec((SC_WINDOW, width), lambda k, j: (k * steps + j, 0))],
            core_axis_name=("c", "s"),
            dimension_semantics=(pltpu.PARALLEL, pltpu.PARALLEL),
        )(idx_hbm, out_hbm)

    return gather(table, idx)


def _experts_kernel(te_ref, nv_ref, blk_ref, xs_ref, wg_ref, wu_ref, wd_ref, ys_ref):
    r = pl.program_id(0)
    n_valid = nv_ref[r]

    @pl.when(n_valid > 0)
    def _():
        parts = [p.astype(BF16) for p in _unpack_rows(xs_ref[0]) + _unpack_rows(xs_ref[1])]
        cols = (0, 2 * SC_ROW, SC_ROW, 3 * SC_ROW)
        hg = sum(_dot(p, wg_ref[c:c + SC_ROW, :]) for p, c in zip(parts, cols))
        hu = sum(_dot(p, wu_ref[c:c + SC_ROW, :]) for p, c in zip(parts, cols))
        row = lax.broadcasted_iota(jnp.int32, (ROW_TILE, 1), 0)
        hid = jnp.where(row < n_valid, hg * _sigmoid(hg) * hu, 0.0).astype(BF16)
        _split_planes(_pack_rows(_dot(hid, wd_ref[...])), ys_ref)


def _experts(xs, tile_expert, n_valid, block, w_gate, w_up, w_down):
    n_tiles = xs.shape[1] // ROW_TILE
    w_spec = pl.BlockSpec((None, D_MODEL, D_EXPERT), lambda r, te, nv, blk: (te[r], 0, 0))
    grid_spec = pltpu.PrefetchScalarGridSpec(
        num_scalar_prefetch=3,
        grid=(n_tiles,),
        in_specs=[
            pl.BlockSpec((2, ROW_TILE, SC_ROW), lambda r, te, nv, blk: (0, blk[r], 0)),
            w_spec, w_spec,
            pl.BlockSpec((None, D_EXPERT, D_MODEL), lambda r, te, nv, blk: (te[r], 0, 0)),
        ],
        out_specs=pl.BlockSpec((2, ROW_TILE, SC_ROW), lambda r, te, nv, blk: (0, blk[r], 0)),
    )
    return pl.pallas_call(
        _experts_kernel,
        grid_spec=grid_spec,
        out_shape=jax.ShapeDtypeStruct((2, n_tiles * ROW_TILE, SC_ROW), U32),
        compiler_params=_params(("arbitrary",)),
        name="experts",
    )(tile_expert, n_valid, block, xs, w_gate, w_up, w_down)


def _ple_kernel(h_ref, yg_ref, meta_t_ref, p_ref, wg_ref, bg_ref, wp_ref, gf_ref, *rest, final_norm):
    o_ref = rest[-1]
    meta = jnp.transpose(meta_t_ref[...])
    w1 = meta[:, 4:5]
    w2 = meta[:, 5:6]
    q0, q2 = (w1 * u + w2 * v for u, v in zip(_unpack_rows(yg_ref[0]), _unpack_rows(yg_ref[2])))
    q1, q3 = (w1 * u + w2 * v for u, v in zip(_unpack_rows(yg_ref[1]), _unpack_rows(yg_ref[3])))
    moe = jnp.concatenate([q0, q1, q2, q3], axis=1)
    h = h_ref[...] + moe
    gate = _sigmoid(_dot(_rms(h).astype(BF16), wg_ref[...]) + bg_ref[...])
    h = h + gate * _dot(p_ref[...].astype(BF16), wp_ref[...])
    o_ref[...] = _rms(h, gf_ref[...]) if final_norm else h


def _ple(h1, yg, meta_t, p2, w_gate, b_gate, w_proj, g_final, final_norm, tm, part, prev_out):
    t = h1.shape[0]
    steps = t // tm // GATHER_PARTS
    off = part * steps
    in_specs = [
        pl.BlockSpec((tm, D_MODEL), lambda i: (i + off, 0)),
        pl.BlockSpec((4, tm, SC_ROW), lambda i: (0, i, 0)),
        pl.BlockSpec((META_ROWS, tm), lambda i: (0, i + off)),
        pl.BlockSpec((tm, D_PLE), lambda i: (i + off, 0)),
        pl.BlockSpec((D_MODEL, D_MODEL), lambda i: (0, 0)),
        pl.BlockSpec((1, D_MODEL), lambda i: (0, 0)),
        pl.BlockSpec((D_PLE, D_MODEL), lambda i: (0, 0)),
        pl.BlockSpec((1, D_MODEL), lambda i: (0, 0)),
    ]
    args = [h1, yg, meta_t, p2, w_gate, b_gate, w_proj, g_final]
    aliases = {}
    if prev_out is not None:
        in_specs.append(pl.BlockSpec(memory_space=pl.ANY))
        args.append(prev_out)
        aliases = {len(args) - 1: 0}
    return pl.pallas_call(
        functools.partial(_ple_kernel, final_norm=final_norm),
        grid=(steps,),
        in_specs=in_specs,
        out_specs=pl.BlockSpec((tm, D_MODEL), lambda i: (i + off, 0)),
        out_shape=jax.ShapeDtypeStruct((t, D_MODEL), F32),
        input_output_aliases=aliases,
        compiler_params=_params(("parallel",)),
        name="ple",
    )(*args)


def kernel(x, p, g_mix, w_in, pool_w, pool_scale, ssm_a_re, ssm_a_im, ssm_log_dt, ssm_b_re,
           ssm_b_im, ssm_c_re, ssm_c_im, ssm_d, glu_w, glu_b, w_out, g_ffn, router_grp_w,
           router_grp_b, router_exp_w, router_exp_b, exp_w_gate, exp_w_up, exp_w_down, g_ple,
           ple_gate_w, ple_gate_b, ple_proj_w, g_final):
    bsz, seq, dm = x.shape
    depth = g_mix.shape[0]
    t = bsz * seq
    seq_chunks = seq // CHUNK
    nch = t // CHUNK
    tm = 1024
    n_sorted = 2 * t + N_EXPERTS * ROW_TILE
    w_gate_all = exp_w_gate.reshape(depth * N_EXPERTS, dm, D_EXPERT)
    w_up_all = exp_w_up.reshape(depth * N_EXPERTS, dm, D_EXPERT)
    w_down_all = exp_w_down.reshape(depth * N_EXPERTS, D_EXPERT, dm)

    h = x.reshape(t, dm)
    for i in range(depth):
        w_in_b = (g_mix[i][:, None] * w_in[i]).astype(BF16)
        zp, ut = _in_proj(h.reshape(nch, CHUNK, dm), w_in_b[:, :D_POOL],
                          jnp.transpose(w_in_b[:, D_POOL:]))
        a = _pool(zp.reshape(bsz, seq, D_POOL), pool_w[i], pool_scale[i][None])
        yt, w_gate_b, w_up_b, w_down_b = _ssm(
            ut, ssm_a_re[i], ssm_a_im[i], ssm_log_dt[i], ssm_b_re[i], ssm_b_im[i], ssm_c_re[i],
            ssm_c_im[i], ssm_d[i], seq_chunks, w_gate_all, w_up_all, w_down_all, i)
        h = _mix_out(h.reshape(nch, CHUNK, dm), a.reshape(nch, CHUNK, D_POOL), yt,
                     jnp.transpose(glu_w[i]).astype(BF16), glu_b[i][:, None],
                     w_out[i].astype(BF16)).reshape(t, dm)

        eg = EXPERTS_PER_GROUP
        w_router = jnp.concatenate(
            [router_grp_w[i], jnp.zeros((dm, eg - N_EXPERT_GROUPS), F32),
             jnp.transpose(router_exp_w[i], (1, 0, 2)).reshape(dm, N_EXPERTS),
             jnp.zeros((dm, ROUTER_W - eg - N_EXPERTS), F32)], axis=1)
        b_router = jnp.concatenate(
            [router_grp_b[i], jnp.zeros((eg - N_EXPERT_GROUPS,), F32),
             router_exp_b[i].reshape(N_EXPERTS),
             jnp.zeros((ROUTER_W - eg - N_EXPERTS,), F32)])[None]
        vp, meta_t, counts = _router(h, g_ffn[i][None], w_router, b_router, tm)
        idx, tile_expert, n_valid, block = _plan(meta_t, counts, n_sorted // ROW_TILE)
        xs = _sc_scatter_rows(vp.reshape(2 * t, SC_ROW), idx, 2 * n_sorted)
        ys = _experts(xs.reshape(2, n_sorted, SC_ROW), tile_expert, n_valid, block,
                      w_gate_b, w_up_b, w_down_b)
        ys2 = ys.reshape(2 * n_sorted, SC_ROW)
        idx4 = idx.reshape(4, t)
        tp = t // GATHER_PARTS
        ple_wg = (g_ple[i][:, None] * ple_gate_w[i]).astype(BF16)
        ple_wp = ple_proj_w[i].astype(BF16)
        out = None
        for q in range(GATHER_PARTS):
            yg_q = _sc_gather_rows(ys2, idx4[:, q * tp:(q + 1) * tp].reshape(1, 4 * tp))
            out = _ple(h, yg_q.reshape(4, tp, SC_ROW), meta_t, p[i].reshape(t, D_PLE), ple_wg,
                       ple_gate_b[i][None], ple_wp, g_final[None], i == depth - 1, tm, q, out)
        h = out
    return h.reshape(bsz, seq, dm)
```

```python
import functools
import math

import numpy as np
import jax
import jax.numpy as jnp
from jax import lax
from jax.experimental import pallas as pl
from jax.experimental.pallas import tpu as pltpu
from jax.experimental.pallas import tpu_sc as plsc

F32 = jnp.float32
BF16 = jnp.bfloat16
U32 = jnp.uint32

D_MODEL = 1024
D_POOL = 512
D_SSM = 512
POOL_WINDOWS = (2, 4, 8, 16)
POOL_GROUP = 128
SSM_GROUP = 16
N_SSM_GROUPS = 32
SSM_STATE = 64
N_EXPERT_GROUPS = 4
EXPERTS_PER_GROUP = 8
N_EXPERTS = N_EXPERT_GROUPS * EXPERTS_PER_GROUP
D_EXPERT = 256
D_PLE = 256
RMS_EPS = 1e-6

LANES = 128
CHUNK = 32
CHUNK_W = CHUNK * SSM_GROUP
T_SUB = 8
S_TILE = 8
C_TILE = 128
ROUTER_W = LANES
HALF = D_MODEL // 2
ROW_TILE = 1024
SC_WINDOW = 128
SC_ROW = HALF // 2
META_ROWS = 8
GATHER_PARTS = 2
VMEM_LIMIT = 56 * 1024 * 1024


def _dot(a, b):
    return jnp.dot(a, b, preferred_element_type=F32)


def _dot_nt(a, b):
    return lax.dot_general(a, b, (((1,), (1,)), ((), ())), preferred_element_type=F32)


def _dot_tn(a, b):
    return lax.dot_general(a, b, (((0,), (0,)), ((), ())), preferred_element_type=F32)


def _rms(x, g=None):
    y = x * lax.rsqrt(jnp.mean(x * x, axis=-1, keepdims=True) + RMS_EPS)
    return y if g is None else y * g


def _sigmoid(x):
    return 1.0 / (1.0 + jnp.exp(-x))


def _params(sem):
    return pltpu.CompilerParams(dimension_semantics=sem, vmem_limit_bytes=VMEM_LIMIT)


def _tile_step():
    n_s = pl.num_programs(1)
    return pl.program_id(0) * n_s + pl.program_id(1), pl.num_programs(0) * n_s


def _row_copies(hbm, buf, sem, step_idx, slot_idx, to_hbm):
    n_s = pl.num_programs(1)
    c0 = (step_idx // n_s) * C_TILE
    s0 = (step_idx % n_s) * S_TILE
    out = []
    for j in range(S_TILE):
        far, near = hbm.at[pl.ds(c0, C_TILE), s0 + j, :], buf.at[slot_idx, j]
        src, dst = (near, far) if to_hbm else (far, near)
        out.append(pltpu.make_async_copy(src, dst, sem.at[slot_idx]))
    return out


def _load_time_major_rows(hbm, buf, sem):
    step, n_steps = _tile_step()
    slot = step % 2

    @pl.when(step == 0)
    def _():
        for cp in _row_copies(hbm, buf, sem, step, slot, False):
            cp.start()

    @pl.when(step + 1 < n_steps)
    def _():
        for cp in _row_copies(hbm, buf, sem, step + 1, 1 - slot, False):
            cp.start()

    for cp in _row_copies(hbm, buf, sem, step, slot, False):
        cp.wait()
    return buf[slot].reshape(S_TILE * C_TILE, buf.shape[-1])


def _store_time_major_rows(val, hbm, buf, sem):
    step, n_steps = _tile_step()
    slot = step % 2

    @pl.when(step >= 2)
    def _():
        for cp in _row_copies(hbm, buf, sem, step - 2, slot, True):
            cp.wait()

    buf[slot] = val.reshape(S_TILE, C_TILE, val.shape[-1])
    for cp in _row_copies(hbm, buf, sem, step, slot, True):
        cp.start()

    @pl.when(step == n_steps - 1)
    def _():
        @pl.when(step >= 1)
        def _():
            for cp in _row_copies(hbm, buf, sem, step - 1, 1 - slot, True):
                cp.wait()
        for cp in _row_copies(hbm, buf, sem, step, slot, True):
            cp.wait()


def _in_proj_kernel(x_hbm, wp_ref, wst_ref, zp_hbm, ut_ref, xbuf, zbuf, xsem, zsem):
    nc = C_TILE
    u = _rms(_load_time_major_rows(x_hbm, xbuf, xsem)).astype(BF16)
    zt = _dot_nt(wst_ref[...], u).astype(BF16)
    for j in range(S_TILE):
        ut_ref[:, j, :, :] = zt[:, j * nc:(j + 1) * nc].reshape(N_SSM_GROUPS, SSM_GROUP, nc)
    _store_time_major_rows(_dot(u, wp_ref[...]), zp_hbm, zbuf, zsem)


def _in_proj(x3, w_pool, w_ssm_t):
    nch = x3.shape[0]
    return pl.pallas_call(
        _in_proj_kernel,
        grid=(nch // C_TILE, CHUNK // S_TILE),
        in_specs=[
            pl.BlockSpec(memory_space=pl.ANY),
            pl.BlockSpec((D_MODEL, D_POOL), lambda c, s: (0, 0)),
            pl.BlockSpec((D_SSM, D_MODEL), lambda c, s: (0, 0)),
        ],
        out_specs=[
            pl.BlockSpec(memory_space=pl.ANY),
            pl.BlockSpec((N_SSM_GROUPS, None, S_TILE, SSM_GROUP, C_TILE), lambda c, s: (0, c, s, 0, 0)),
        ],
        out_shape=[
            jax.ShapeDtypeStruct((nch, CHUNK, D_POOL), F32),
            jax.ShapeDtypeStruct((N_SSM_GROUPS, nch // C_TILE, CHUNK, SSM_GROUP, C_TILE), BF16),
        ],
        scratch_shapes=[
            pltpu.VMEM((2, S_TILE, C_TILE, D_MODEL), F32),
            pltpu.VMEM((2, S_TILE, C_TILE, D_POOL), F32),
            pltpu.SemaphoreType.DMA((2,)),
            pltpu.SemaphoreType.DMA((2,)),
        ],
        compiler_params=_params(("arbitrary", "arbitrary")),
        name="in_proj",
    )(x3, w_pool, w_ssm_t)


def _shift_rows(x, d, row, n):
    if d == 0:
        return x
    r = pltpu.roll(x, d % n, 0)
    if d > 0:
        return jnp.where(row >= d, r, 0.0)
    return jnp.where(row < n + d, r, 0.0)


def _pool_kernel(z_ref, w_ref, sc_ref, o_ref):
    n = z_ref.shape[0]
    gi = pl.program_id(1)
    row = lax.broadcasted_iota(jnp.int32, (n, 1), 0)

    for k, w in enumerate(POOL_WINDOWS):
        @pl.when(gi == k)
        def _(w=w):
            x = z_ref[...]
            half = w // 2
            pd, pu, span = x, x, 1
            while span < half:
                pd = pd + _shift_rows(pd, span, row, n)
                pu = pu + _shift_rows(pu, -span, row, n)
                span *= 2
            total = _shift_rows(pd, 1, row, n) + pu
            lo = jnp.maximum(row - half, 0)
            hi = jnp.minimum(row + half, n)
            cnt = (hi - lo).astype(F32)
            diff = (total / cnt - x).astype(BF16)
            o_ref[...] = _dot(diff, w_ref[...].astype(BF16)) * sc_ref[...]


def _pool(zp3, pool_w, pool_scale):
    b, s, _ = zp3.shape
    return pl.pallas_call(
        _pool_kernel,
        grid=(b, len(POOL_WINDOWS)),
        in_specs=[
            pl.BlockSpec((None, s, POOL_GROUP), lambda i, g: (i, 0, g)),
            pl.BlockSpec((None, POOL_GROUP, POOL_GROUP), lambda i, g: (g, 0, 0)),
            pl.BlockSpec((1, POOL_GROUP), lambda i, g: (0, g)),
        ],
        out_specs=pl.BlockSpec((None, s, POOL_GROUP), lambda i, g: (i, 0, g)),
        out_shape=jax.ShapeDtypeStruct((b, s, D_POOL), F32),
        compiler_params=_params(("parallel", "parallel")),
        name="pool",
    )(zp3, pool_w, pool_scale)


def _expand_consts():
    time = np.arange(CHUNK_W) // SSM_GROUP
    def onehot(e):
        m = np.zeros((CHUNK_W, LANES), np.float32)
        m[np.arange(CHUNK_W), e] = 1.0
        return m
    return np.stack([
        onehot(CHUNK - 1 - time),
        onehot(time),
        onehot(time + 1),
        onehot(CHUNK - time),
    ])


def _cmul_packed(x, p, q):
    return x * p + pltpu.roll(x, LANES // 2, 1) * q


def _ssm_kernel(u_ref, vec_ref, mat_ref, exp_ref, wg_ref, wu_ref, wd_ref, y_ref, og_ref, ou_ref,
                od_ref, *, seq_chunks):
    og_ref[...] = wg_ref[...].astype(BF16)
    ou_ref[...] = wu_ref[...].astype(BF16)
    od_ref[...] = wd_ref[...].astype(BF16)

    n_ct = u_ref.shape[0]
    nch = n_ct * C_TILE
    half = LANES // 2
    lane = lax.broadcasted_iota(jnp.int32, (1, LANES), 1)
    lo_half = lane < half

    def direction(di):
        a_re = vec_ref[di, 0:1]
        a_im = vec_ref[di, 1:2]
        dt = jnp.exp(vec_ref[di, 2:3])
        mag = jnp.exp(a_re * dt)
        ang = a_im * dt
        lam = jnp.where(lo_half, mag * jnp.cos(ang), mag * jnp.sin(ang))
        lb_re = mag * jnp.cos(ang)
        lb_im = mag * jnp.sin(ang)
        den = a_re * a_re + a_im * a_im
        f_re = ((lb_re - 1.0) * a_re + lb_im * a_im) / den
        f_im = (lb_im * a_re - (lb_re - 1.0) * a_im) / den
        return lam, f_re, f_im

    def power_table(lam):
        e = lax.broadcasted_iota(jnp.int32, (LANES, 1), 0)
        tab = jnp.where(lo_half, 1.0, 0.0) * jnp.ones((LANES, 1), F32)
        sq = lam
        for k in range(7):
            p = jnp.where(lo_half, sq, pltpu.roll(sq, half, 1))
            q = jnp.where(lo_half, -pltpu.roll(sq, half, 1), sq)
            tab = jnp.where(((e >> k) & 1) == 1, _cmul_packed(tab, p, q), tab)
            sq = _cmul_packed(sq, p, q)
        return tab

    def fwd_bwd(f, b):
        return jnp.where(lo_half, f, b)

    lam_f, ff_re, ff_im = direction(0)
    lam_b, fb_re, fb_im = direction(1)
    tab_f = power_table(lam_f)
    tab_b = power_table(lam_b)
    swap_f = pltpu.roll(tab_f, half, 1)
    swap_b = pltpu.roll(tab_b, half, 1)
    pow_re = (jnp.where(lo_half, tab_f, 0.0), jnp.where(lo_half, 0.0, swap_b))
    pow_im = (jnp.where(lo_half, swap_f, 0.0), jnp.where(lo_half, 0.0, tab_b))

    def expand(which_f, which_b):
        parts = [(w, d) for w, d in ((which_f, 0), (which_b, 1)) if w is not None]
        onehot = jnp.concatenate([exp_ref[w] for w, _ in parts], axis=1)
        re = jnp.concatenate([pow_re[d] for _, d in parts], axis=0).astype(BF16)
        im = jnp.concatenate([pow_im[d] for _, d in parts], axis=0).astype(BF16)
        return _dot(onehot, re), _dot(onehot, im)

    def mat(di, k):
        return mat_ref[di, k * SSM_GROUP:(k + 1) * SSM_GROUP, :]

    def bbar(bt_re, bt_im, f_re, f_im):
        return bt_re * f_re - bt_im * f_im, bt_re * f_im + bt_im * f_re

    bf_re, bf_im = bbar(mat(0, 0), mat(0, 1), ff_re, ff_im)
    bb_re, bb_im = bbar(mat(1, 0), mat(1, 1), fb_re, fb_im)
    b_re_t = fwd_bwd(bf_re, bb_re)
    b_im_t = fwd_bwd(bf_im, bb_im)
    c_re_t = fwd_bwd(mat(0, 3), mat(1, 3))
    c_im_t = fwd_bwd(mat(0, 4), mat(1, 4))

    def times(l_re, l_im, v_re, v_im):
        l_re = l_re.reshape(CHUNK, SSM_GROUP, LANES)
        l_im = l_im.reshape(CHUNK, SSM_GROUP, LANES)
        re = l_re * v_re[None] - l_im * v_im[None]
        im = l_re * v_im[None] + l_im * v_re[None]
        return re.reshape(CHUNK_W, LANES), im.reshape(CHUNK_W, LANES)

    e_re, e_im = times(*expand(0, 1), b_re_t, b_im_t)
    e_mat = jnp.concatenate([e_re, e_im], axis=1).astype(BF16)
    p3_re, p3_im = times(*expand(None, 2), b_re_t, b_im_t)
    p3_mat = jnp.concatenate([p3_re, p3_im], axis=1).astype(BF16)
    f_re, f_im = times(*expand(2, 3), c_re_t, c_im_t)
    f_t = jnp.concatenate([f_re, -f_im], axis=1).astype(BF16)

    ccr_f, ccr_b = mat(0, 2), mat(1, 2)
    zero = jnp.zeros_like(ccr_f)
    w_f = jnp.concatenate([jnp.where(lo_half, ccr_f, zero),
                           jnp.where(lo_half, pltpu.roll(ccr_f, half, 1), zero)], axis=1).astype(BF16)
    w_b = jnp.concatenate([jnp.where(lo_half, zero, pltpu.roll(ccr_b, half, 1)),
                           jnp.where(lo_half, zero, ccr_b)], axis=1).astype(BF16)
    row_w = lax.broadcasted_iota(jnp.int32, (CHUNK_W, 1), 0)
    last_blk = row_w >= CHUNK_W - SSM_GROUP
    lag0 = jnp.where(last_blk, pltpu.roll(e_mat.astype(F32), CHUNK_W - SSM_GROUP, 0), 0.0).astype(BF16)
    r_lo = _dot_nt(w_f, e_mat) + _dot_nt(w_b, lag0)
    co = lax.broadcasted_iota(jnp.int32, (SSM_GROUP, CHUNK_W), 0)
    col = lax.broadcasted_iota(jnp.int32, (SSM_GROUP, CHUNK_W), 1)
    r_lo = r_lo + jnp.where(col == CHUNK_W - SSM_GROUP + co, mat(0, 5)[:, 0:1], 0.0)
    r_hi = _dot_nt(w_b, p3_mat)
    r_t = jnp.concatenate([r_lo, r_hi], axis=1)
    g_t = jnp.concatenate(
        [pltpu.roll(r_t, SSM_GROUP * (tl + 1), 1) for tl in range(T_SUB)], axis=0
    ).astype(BF16)

    u = jnp.concatenate([u_ref[ct].reshape(CHUNK_W, C_TILE) for ct in range(n_ct)],
                        axis=1)
    xend = _dot_tn(u, e_mat)
    x_re, x_im = xend[:, :LANES], xend[:, LANES:]
    rowc = lax.broadcasted_iota(jnp.int32, (nch, 1), 0) % seq_chunks
    lam_l_f, lam_l_b = tab_f[CHUNK:CHUNK + 1, :], tab_b[CHUNK:CHUNK + 1, :]
    a = fwd_bwd(lam_l_f, pltpu.roll(lam_l_b, half, 1))
    b = fwd_bwd(pltpu.roll(lam_l_f, half, 1), lam_l_b)

    def shifted(v, d):
        earlier = jnp.where(rowc >= d, pltpu.roll(v, d, 0), 0.0)
        later = jnp.where(rowc < seq_chunks - d, pltpu.roll(v, nch - d, 0), 0.0)
        return jnp.where(lo_half, earlier, later)

    for k in range(int(math.log2(seq_chunks))):
        s_re, s_im = shifted(x_re, 1 << k), shifted(x_im, 1 << k)
        x_re, x_im = x_re + (s_re * a - s_im * b), x_im + (s_re * b + s_im * a)
        a, b = a * a - b * b, 2.0 * a * b
    xin = jnp.concatenate([shifted(x_re, 1), shifted(x_im, 1)], axis=1).astype(BF16)

    toeplitz = jnp.concatenate(
        [g_t[:, CHUNK_W - LANES * th:2 * CHUNK_W - LANES * th] for th in range(CHUNK // T_SUB)],
        axis=0)
    y_t = _dot(toeplitz, u) + _dot_nt(f_t, xin)
    for ct in range(n_ct):
        y_ref[ct] = y_t[:, ct * C_TILE:(ct + 1) * C_TILE].reshape(
            CHUNK, SSM_GROUP, C_TILE).astype(y_ref.dtype)


def _ssm(ut, a_re, a_im, log_dt, b_re, b_im, c_re, c_im, d, seq_chunks, w_gate, w_up, w_down, layer):
    g, n_ct = ut.shape[:2]
    assert N_EXPERTS % g == 0
    epg = N_EXPERTS // g
    base = layer * g
    n = SSM_STATE

    def per_group(a):
        return jnp.swapaxes(a, 0, 1)

    def dup(a):
        return jnp.concatenate([a, a], axis=-1)

    vecs = dup(jnp.stack([per_group(a_re), per_group(a_im),
                          jnp.broadcast_to(per_group(log_dt)[..., None], (g, 2, n))], axis=2))
    cr, ci = per_group(c_re), per_group(c_im)
    d_blk = jnp.broadcast_to(d.reshape(g, 1, SSM_GROUP, 1), (g, 2, SSM_GROUP, LANES))
    mats = jnp.concatenate(
        [dup(jnp.swapaxes(per_group(b_re), 2, 3)), dup(jnp.swapaxes(per_group(b_im), 2, 3)),
         jnp.concatenate([cr, -ci], axis=-1), dup(cr), dup(ci), d_blk], axis=2)
    exp_c = jnp.asarray(_expand_consts(), BF16)

    return pl.pallas_call(
        functools.partial(_ssm_kernel, seq_chunks=seq_chunks),
        grid=(g,),
        in_specs=[
            pl.BlockSpec((None, n_ct, CHUNK, SSM_GROUP, C_TILE), lambda i: (i, 0, 0, 0, 0)),
            pl.BlockSpec((None, 2, 3, LANES), lambda i: (i, 0, 0, 0)),
            pl.BlockSpec((None, 2, 6 * SSM_GROUP, LANES), lambda i: (i, 0, 0, 0)),
            pl.BlockSpec((4, CHUNK_W, LANES), lambda i: (0, 0, 0)),
            pl.BlockSpec((epg, D_MODEL, D_EXPERT), lambda i: (base + i, 0, 0)),
            pl.BlockSpec((epg, D_MODEL, D_EXPERT), lambda i: (base + i, 0, 0)),
            pl.BlockSpec((epg, D_EXPERT, D_MODEL), lambda i: (base + i, 0, 0)),
        ],
        out_specs=[
            pl.BlockSpec((None, n_ct, CHUNK, SSM_GROUP, C_TILE), lambda i: (i, 0, 0, 0, 0)),
            pl.BlockSpec((epg, D_MODEL, D_EXPERT), lambda i: (i, 0, 0)),
            pl.BlockSpec((epg, D_MODEL, D_EXPERT), lambda i: (i, 0, 0)),
            pl.BlockSpec((epg, D_EXPERT, D_MODEL), lambda i: (i, 0, 0)),
        ],
        out_shape=[
            jax.ShapeDtypeStruct(ut.shape, BF16),
            jax.ShapeDtypeStruct((N_EXPERTS, D_MODEL, D_EXPERT), BF16),
            jax.ShapeDtypeStruct((N_EXPERTS, D_MODEL, D_EXPERT), BF16),
            jax.ShapeDtypeStruct((N_EXPERTS, D_EXPERT, D_MODEL), BF16),
        ],
        compiler_params=_params(("parallel",)),
        name="ssm",
    )(ut, vecs, mats, exp_c, w_gate, w_up, w_down)


def _mix_out_kernel(x_hbm, a_hbm, yt_ref, gwt_ref, gb_ref, wo_ref, h_hbm, xbuf, abuf, hbuf,
                    xsem, asem, hsem):
    nc = C_TILE
    x = _load_time_major_rows(x_hbm, xbuf, xsem)
    a = _load_time_major_rows(a_hbm, abuf, asem).astype(BF16)
    y = jnp.concatenate([yt_ref[:, j, :, :].reshape(D_SSM, nc) for j in range(S_TILE)],
                        axis=1).astype(F32)
    z = 0.5 * y * (1.0 + jnp.tanh(math.sqrt(2.0 / math.pi) * (y + 0.044715 * (y * y * y))))
    gate = _sigmoid(_dot(gwt_ref[...], z.astype(BF16)) + gb_ref[...])
    s = (z * gate).astype(BF16)
    h = x + _dot(a, wo_ref[:D_POOL, :]) + _dot_tn(s, wo_ref[D_POOL:, :])
    _store_time_major_rows(h, h_hbm, hbuf, hsem)


def _mix_out(x3, a3, yt, glu_w_t, glu_b_col, w_out):
    nch = x3.shape[0]
    return pl.pallas_call(
        _mix_out_kernel,
        grid=(nch // C_TILE, CHUNK // S_TILE),
        in_specs=[
            pl.BlockSpec(memory_space=pl.ANY),
            pl.BlockSpec(memory_space=pl.ANY),
            pl.BlockSpec((N_SSM_GROUPS, None, S_TILE, SSM_GROUP, C_TILE), lambda c, t: (0, c, t, 0, 0)),
            pl.BlockSpec((D_SSM, D_SSM), lambda c, t: (0, 0)),
            pl.BlockSpec((D_SSM, 1), lambda c, t: (0, 0)),
            pl.BlockSpec((D_MODEL, D_MODEL), lambda c, t: (0, 0)),
        ],
        out_specs=pl.BlockSpec(memory_space=pl.ANY),
        out_shape=jax.ShapeDtypeStruct((nch, CHUNK, D_MODEL), F32),
        scratch_shapes=[
            pltpu.VMEM((2, S_TILE, C_TILE, D_MODEL), F32),
            pltpu.VMEM((2, S_TILE, C_TILE, D_POOL), F32),
            pltpu.VMEM((2, S_TILE, C_TILE, D_MODEL), F32),
            pltpu.SemaphoreType.DMA((2,)),
            pltpu.SemaphoreType.DMA((2,)),
            pltpu.SemaphoreType.DMA((2,)),
        ],
        compiler_params=_params(("arbitrary", "arbitrary")),
        name="mix_out",
    )(x3, a3, yt, glu_w_t, glu_b_col, w_out)


def _pack_rows(x):
    b = lax.bitcast_convert_type(x.astype(BF16).astype(F32), U32)
    return (b[:, :HALF] & jnp.uint32(0xFFFF0000)) | (b[:, HALF:] >> 16)


def _unpack_rows(w):
    lo = lax.bitcast_convert_type(w & jnp.uint32(0xFFFF0000), F32)
    hi = lax.bitcast_convert_type(w << 16, F32)
    return lo, hi


def _split_bf16(x):
    hi = x.astype(BF16)
    return hi, (x - hi.astype(F32)).astype(BF16)


def _route(v32, wr_ref, br_ref):
    v_hi, v_lo = _split_bf16(v32)
    w_hi, w_lo = _split_bf16(wr_ref[...])
    logits = _dot(v_hi, w_hi) + (_dot(v_lo, w_hi) + _dot(v_hi, w_lo)) + br_ref[...]
    return jnp.transpose(logits)


def _top1(x, valid=None):
    n = x.shape[0]
    row = lax.broadcasted_iota(jnp.int32, x.shape, 0).astype(F32)
    if valid is not None:
        x = jnp.where(valid, x, -jnp.inf)
    m = jnp.max(x, axis=0, keepdims=True)
    idx = jnp.min(jnp.where(x == m, row, float(n)), axis=0, keepdims=True)
    return m, idx, x, row


def _split_planes(packed, ref):
    ref[0] = packed[:, :SC_ROW]
    ref[1] = packed[:, SC_ROW:]


def _router_kernel(h_ref, g_ref, wr_ref, br_ref, before_ref, vp_ref, meta_t_ref, cnt_ref, carry_ref):
    @pl.when(pl.program_id(0) == 0)
    def _():
        carry_ref[...] = jnp.zeros_like(carry_ref)

    v32 = _rms(h_ref[...], g_ref[...])
    _split_planes(_pack_rows(v32), vp_ref)
    lt = _route(v32, wr_ref, br_ref)
    tm = lt.shape[1]
    eg = EXPERTS_PER_GROUP

    grp = lt[:eg]
    grp_row = lax.broadcasted_iota(jnp.int32, grp.shape, 0)
    mg, grp_idx, grp, _ = _top1(grp, grp_row < N_EXPERT_GROUPS)
    grp_p = 1.0 / jnp.sum(jnp.exp(grp - mg), axis=0, keepdims=True)
    le = jnp.zeros((eg, tm), F32)
    for g in range(N_EXPERT_GROUPS):
        le = jnp.where(grp_idx == float(g), lt[eg * (g + 1):eg * (g + 2)], le)
    m1, i1, le, row = _top1(le)
    z = jnp.sum(jnp.exp(le - m1), axis=0, keepdims=True)
    m2, i2, _, _ = _top1(jnp.where(row == i1, -jnp.inf, le))
    p1 = 1.0 / z
    p2 = jnp.exp(m2 - m1) / z
    tot = p1 + p2
    w1 = grp_p * (p1 / tot)
    w2 = grp_p * (p2 / tot)
    e1 = grp_idx * eg + i1
    e2 = grp_idx * eg + i2

    erow = lax.broadcasted_iota(jnp.int32, (N_EXPERTS, tm), 0).astype(F32)
    onehot = jnp.where(erow == e1, 1.0, jnp.where(erow == e2, 1.0, 0.0))
    before = _dot(onehot.astype(BF16), before_ref[...]) + carry_ref[...]
    rank1 = jnp.sum(jnp.where(erow == e1, before, 0.0), axis=0, keepdims=True)
    rank2 = jnp.sum(jnp.where(erow == e2, before, 0.0), axis=0, keepdims=True)
    carry = carry_ref[...] + jnp.sum(onehot, axis=1, keepdims=True)
    carry_ref[...] = carry
    cnt_ref[...] = carry

    mrow = lax.broadcasted_iota(jnp.int32, (META_ROWS, tm), 0)
    meta_t_ref[...] = jnp.where(mrow == 0, e1, jnp.where(mrow == 1, e2, jnp.where(
        mrow == 2, rank1, jnp.where(mrow == 3, rank2, jnp.where(
            mrow == 4, w1, jnp.where(mrow == 5, w2, 0.0))))))


def _earlier_matrix(tm):
    return np.triu(np.ones((tm, tm), np.float32), k=1)


def _router(h1, g_ffn, w_router, b_router, tm):
    t = h1.shape[0]
    return pl.pallas_call(
        _router_kernel,
        grid=(t // tm,),
        in_specs=[
            pl.BlockSpec((tm, D_MODEL), lambda i: (i, 0)),
            pl.BlockSpec((1, D_MODEL), lambda i: (0, 0)),
            pl.BlockSpec((D_MODEL, ROUTER_W), lambda i: (0, 0)),
            pl.BlockSpec((1, ROUTER_W), lambda i: (0, 0)),
            pl.BlockSpec((tm, tm), lambda i: (0, 0)),
        ],
        out_specs=[
            pl.BlockSpec((2, tm, SC_ROW), lambda i: (0, i, 0)),
            pl.BlockSpec((META_ROWS, tm), lambda i: (0, i)),
            pl.BlockSpec((N_EXPERTS, 1), lambda i: (0, 0)),
        ],
        out_shape=[
            jax.ShapeDtypeStruct((2, t, SC_ROW), U32),
            jax.ShapeDtypeStruct((META_ROWS, t), F32),
            jax.ShapeDtypeStruct((N_EXPERTS, 1), F32),
        ],
        scratch_shapes=[pltpu.VMEM((N_EXPERTS, 1), F32)],
        compiler_params=_params(("arbitrary",)),
        name="router",
    )(h1, g_ffn, w_router, b_router, jnp.asarray(_earlier_matrix(tm), BF16))


def _plan(meta_t, counts, n_tiles):
    e1 = meta_t[0].astype(jnp.int32)
    e2 = meta_t[1].astype(jnp.int32)
    rank1 = meta_t[2].astype(jnp.int32)
    rank2 = meta_t[3].astype(jnp.int32)
    cnt = counts[:, 0].astype(jnp.int32)
    padded = ((cnt + ROW_TILE - 1) // ROW_TILE) * ROW_TILE
    ends = jnp.cumsum(padded)
    starts = ends - padded
    experts = jnp.arange(N_EXPERTS, dtype=jnp.int32)
    pos1 = rank1 + jnp.sum(jnp.where(e1[None, :] == experts[:, None], starts[:, None], 0), axis=0)
    pos2 = rank2 + jnp.sum(jnp.where(e2[None, :] == experts[:, None], starts[:, None], 0), axis=0)
    tile_start = jnp.arange(n_tiles, dtype=jnp.int32) * ROW_TILE
    tile_expert = jnp.sum((tile_start[:, None] >= ends[None, :]).astype(jnp.int32), axis=1)
    tile_expert = jnp.minimum(tile_expert, N_EXPERTS - 1)
    rows_left = jnp.sum(jnp.where(tile_expert[:, None] == experts, cnt + starts, 0), axis=1) - tile_start
    n_valid = jnp.clip(rows_left, 0, ROW_TILE).astype(jnp.int32)
    last_used = jnp.maximum(ends[-1] // ROW_TILE - 1, 0)
    block = jnp.minimum(jnp.arange(n_tiles, dtype=jnp.int32), last_used)
    tile_expert = jnp.sum(jnp.where(block[:, None] == jnp.arange(n_tiles)[None, :],
                                    tile_expert[None, :], 0), axis=1)
    plane = n_tiles * ROW_TILE
    half_rows = jnp.concatenate([pos1, pos1 + plane, pos2, pos2 + plane])[None]
    return half_rows, tile_expert, n_valid, block


def _sc_mesh():
    return plsc.VectorSubcoreMesh(core_axis_name="c", subcore_axis_name="s")


def _sc_scatter_rows(rows, idx, n_out):
    t, width = rows.shape
    steps = t // SC_WINDOW

    @pl.kernel(out_type=jax.ShapeDtypeStruct((n_out, width), rows.dtype), mesh=_sc_mesh(),
               scratch_types=[], name="moe_scatter")
    def scatter(rows_hbm, idx_hbm, out_hbm):
        def body(rows_vmem, idx_vmem):
            pltpu.sync_copy(rows_vmem, out_hbm.at[idx_vmem.at[0]])

        pltpu.emit_pipeline(
            body,
            grid=(2, steps),
            in_specs=[pl.BlockSpec((SC_WINDOW, width), lambda k, j: (j, 0)),
                      pl.BlockSpec((1, SC_WINDOW), lambda k, j: (0, k * steps + j))],
            out_specs=[],
            core_axis_name=("c", "s"),
            dimension_semantics=(pltpu.PARALLEL, pltpu.PARALLEL),
        )(rows_hbm, idx_hbm)

    return scatter(rows, idx)


def _sc_gather_rows(table, idx):
    m = idx.shape[1]
    width = table.shape[1]
    steps = m // (2 * SC_WINDOW)

    @pl.kernel(out_type=jax.ShapeDtypeStruct((m, width), table.dtype), mesh=_sc_mesh(),
               scratch_types=[], name="moe_gather")
    def gather(table_hbm, idx_hbm, out_hbm):
        def body(idx_vmem, out_vmem):
            pltpu.sync_copy(table_hbm.at[idx_vmem.at[0]], out_vmem)

        pltpu.emit_pipeline(
            body,
            grid=(2, steps),
            in_specs=[pl.BlockSpec((1, SC_WINDOW), lambda k, j: (0, k * steps + j))],
            out_specs=[pl.BlockSpec((SC_WINDOW, width), lambda k, j: (k * steps + j, 0))],
            core_axis_name=("c", "s"),
            dimension_semantics=(pltpu.PARALLEL, pltpu.PARALLEL),
        )(idx_hbm, out_hbm)

    return gather(table, idx)


def _experts_kernel(te_ref, nv_ref, blk_ref, xs_ref, wg_ref, wu_ref, wd_ref, ys_ref):
    r = pl.program_id(0)
    n_valid = nv_ref[r]

    @pl.when(n_valid > 0)
    def _():
        parts = [p.astype(BF16) for p in _unpack_rows(xs_ref[0]) + _unpack_rows(xs_ref[1])]
        cols = (0, 2 * SC_ROW, SC_ROW, 3 * SC_ROW)
        hg = sum(_dot(p, wg_ref[c:c + SC_ROW, :]) for p, c in zip(parts, cols))
        hu = sum(_dot(p, wu_ref[c:c + SC_ROW, :]) for p, c in zip(parts, cols))
        row = lax.broadcasted_iota(jnp.int32, (ROW_TILE, 1), 0)
        hid = jnp.where(row < n_valid, hg * _sigmoid(hg) * hu, 0.0).astype(BF16)
        _split_planes(_pack_rows(_dot(hid, wd_ref[...])), ys_ref)


def _experts(xs, tile_expert, n_valid, block, w_gate, w_up, w_down):
    n_tiles = xs.shape[1] // ROW_TILE
    w_spec = pl.BlockSpec((None, D_MODEL, D_EXPERT), lambda r, te, nv, blk: (te[r], 0, 0))
    grid_spec = pltpu.PrefetchScalarGridSpec(
        num_scalar_prefetch=3,
        grid=(n_tiles,),
        in_specs=[
            pl.BlockSpec((2, ROW_TILE, SC_ROW), lambda r, te, nv, blk: (0, blk[r], 0)),
            w_spec, w_spec,
            pl.BlockSpec((None, D_EXPERT, D_MODEL), lambda r, te, nv, blk: (te[r], 0, 0)),
        ],
        out_specs=pl.BlockSpec((2, ROW_TILE, SC_ROW), lambda r, te, nv, blk: (0, blk[r], 0)),
    )
    return pl.pallas_call(
        _experts_kernel,
        grid_spec=grid_spec,
        out_shape=jax.ShapeDtypeStruct((2, n_tiles * ROW_TILE, SC_ROW), U32),
        compiler_params=_params(("arbitrary",)),
        name="experts",
    )(tile_expert, n_valid, block, xs, w_gate, w_up, w_down)


def _ple_kernel(h_ref, yg_ref, meta_t_ref, p_ref, wg_ref, bg_ref, wp_ref, gf_ref, *rest, final_norm):
    o_ref = rest[-1]
    meta = jnp.transpose(meta_t_ref[...])
    w1 = meta[:, 4:5]
    w2 = meta[:, 5:6]
    q0, q2 = (w1 * u + w2 * v for u, v in zip(_unpack_rows(yg_ref[0]), _unpack_rows(yg_ref[2])))
    q1, q3 = (w1 * u + w2 * v for u, v in zip(_unpack_rows(yg_ref[1]), _unpack_rows(yg_ref[3])))
    moe = jnp.concatenate([q0, q1, q2, q3], axis=1)
    h = h_ref[...] + moe
    gate = _sigmoid(_dot(_rms(h).astype(BF16), wg_ref[...]) + bg_ref[...])
    h = h + gate * _dot(p_ref[...].astype(BF16), wp_ref[...])
    o_ref[...] = _rms(h, gf_ref[...]) if final_norm else h


def _ple(h1, yg, meta_t, p2, w_gate, b_gate, w_proj, g_final, final_norm, tm, part, prev_out):
    t = h1.shape[0]
    steps = t // tm // GATHER_PARTS
    off = part * steps
    in_specs = [
        pl.BlockSpec((tm, D_MODEL), lambda i: (i + off, 0)),
        pl.BlockSpec((4, tm, SC_ROW), lambda i: (0, i, 0)),
        pl.BlockSpec((META_ROWS, tm), lambda i: (0, i + off)),
        pl.BlockSpec((tm, D_PLE), lambda i: (i + off, 0)),
        pl.BlockSpec((D_MODEL, D_MODEL), lambda i: (0, 0)),
        pl.BlockSpec((1, D_MODEL), lambda i: (0, 0)),
        pl.BlockSpec((D_PLE, D_MODEL), lambda i: (0, 0)),
        pl.BlockSpec((1, D_MODEL), lambda i: (0, 0)),
    ]
    args = [h1, yg, meta_t, p2, w_gate, b_gate, w_proj, g_final]
    aliases = {}
    if prev_out is not None:
        in_specs.append(pl.BlockSpec(memory_space=pl.ANY))
        args.append(prev_out)
        aliases = {len(args) - 1: 0}
    return pl.pallas_call(
        functools.partial(_ple_kernel, final_norm=final_norm),
        grid=(steps,),
        in_specs=in_specs,
        out_specs=pl.BlockSpec((tm, D_MODEL), lambda i: (i + off, 0)),
        out_shape=jax.ShapeDtypeStruct((t, D_MODEL), F32),
        input_output_aliases=aliases,
        compiler_params=_params(("parallel",)),
        name="ple",
    )(*args)


def kernel(x, p, g_mix, w_in, pool_w, pool_scale, ssm_a_re, ssm_a_im, ssm_log_dt, ssm_b_re,
           ssm_b_im, ssm_c_re, ssm_c_im, ssm_d, glu_w, glu_b, w_out, g_ffn, router_grp_w,
           router_grp_b, router_exp_w, router_exp_b, exp_w_gate, exp_w_up, exp_w_down, g_ple,
           ple_gate_w, ple_gate_b, ple_proj_w, g_final):
    bsz, seq, dm = x.shape
    depth = g_mix.shape[0]
    t = bsz * seq
    seq_chunks = seq // CHUNK
    nch = t // CHUNK
    tm = 1024
    n_sorted = 2 * t + N_EXPERTS * ROW_TILE
    w_gate_all = exp_w_gate.reshape(depth * N_EXPERTS, dm, D_EXPERT)
    w_up_all = exp_w_up.reshape(depth * N_EXPERTS, dm, D_EXPERT)
    w_down_all = exp_w_down.reshape(depth * N_EXPERTS, D_EXPERT, dm)

    h = x.reshape(t, dm)
    for i in range(depth):
        w_in_b = (g_mix[i][:, None] * w_in[i]).astype(BF16)
        zp, ut = _in_proj(h.reshape(nch, CHUNK, dm), w_in_b[:, :D_POOL],
                          jnp.transpose(w_in_b[:, D_POOL:]))
        a = _pool(zp.reshape(bsz, seq, D_POOL), pool_w[i], pool_scale[i][None])
        yt, w_gate_b, w_up_b, w_down_b = _ssm(
            ut, ssm_a_re[i], ssm_a_im[i], ssm_log_dt[i], ssm_b_re[i], ssm_b_im[i], ssm_c_re[i],
            ssm_c_im[i], ssm_d[i], seq_chunks, w_gate_all, w_up_all, w_down_all, i)
        h = _mix_out(h.reshape(nch, CHUNK, dm), a.reshape(nch, CHUNK, D_POOL), yt,
                     jnp.transpose(glu_w[i]).astype(BF16), glu_b[i][:, None],
                     w_out[i].astype(BF16)).reshape(t, dm)

        eg = EXPERTS_PER_GROUP
        w_router = jnp.concatenate(
            [router_grp_w[i], jnp.zeros((dm, eg - N_EXPERT_GROUPS), F32),
             jnp.transpose(router_exp_w[i], (1, 0, 2)).reshape(dm, N_EXPERTS),
             jnp.zeros((dm, ROUTER_W - eg - N_EXPERTS), F32)], axis=1)
        b_router = jnp.concatenate(
            [router_grp_b[i], jnp.zeros((eg - N_EXPERT_GROUPS,), F32),
             router_exp_b[i].reshape(N_EXPERTS),
             jnp.zeros((ROUTER_W - eg - N_EXPERTS,), F32)])[None]
        vp, meta_t, counts = _router(h, g_ffn[i][None], w_router, b_router, tm)
        idx, tile_expert, n_valid, block = _plan(meta_t, counts, n_sorted // ROW_TILE)
        xs = _sc_scatter_rows(vp.reshape(2 * t, SC_ROW), idx, 2 * n_sorted)
        ys = _experts(xs.reshape(2, n_sorted, SC_ROW), tile_expert, n_valid, block,
                      w_gate_b, w_up_b, w_down_b)
        ys2 = ys.reshape(2 * n_sorted, SC_ROW)
        idx4 = idx.reshape(4, t)
        tp = t // GATHER_PARTS
        ple_wg = (g_ple[i][:, None] * ple_gate_w[i]).astype(BF16)
        ple_wp = ple_proj_w[i].astype(BF16)
        out = None
        for q in range(GATHER_PARTS):
            yg_q = _sc_gather_rows(ys2, idx4[:, q * tp:(q + 1) * tp].reshape(1, 4 * tp))
            out = _ple(h, yg_q.reshape(4, tp, SC_ROW), meta_t, p[i].reshape(t, D_PLE), ple_wg,
                       ple_gate_b[i][None], ple_wp, g_final[None], i == depth - 1, tm, q, out)
        h = out
    return h.reshape(bsz, seq, dm)
```

```python
import functools
import math

import numpy as np
import jax
import jax.numpy as jnp
from jax import lax
from jax.experimental import pallas as pl
from jax.experimental.pallas import tpu as pltpu
from jax.experimental.pallas import tpu_sc as plsc

F32 = jnp.float32
BF16 = jnp.bfloat16
U32 = jnp.uint32

D_MODEL = 1024
D_POOL = 512
D_SSM = 512
POOL_WINDOWS = (2, 4, 8, 16)
POOL_GROUP = 128
SSM_GROUP = 16
N_SSM_GROUPS = 32
SSM_STATE = 64
N_EXPERT_GROUPS = 4
EXPERTS_PER_GROUP = 8
N_EXPERTS = N_EXPERT_GROUPS * EXPERTS_PER_GROUP
D_EXPERT = 256
D_PLE = 256
RMS_EPS = 1e-6

LANES = 128
CHUNK = 32
CHUNK_W = CHUNK * SSM_GROUP
T_SUB = 8
S_TILE = 8
C_TILE = 128
ROUTER_W = LANES
HALF = D_MODEL // 2
ROW_TILE = 512
SC_WINDOW = 128
SC_ROW = HALF // 2
META_ROWS = 8
MOE_PARTS = 2
VMEM_LIMIT = 56 * 1024 * 1024


def _dot(a, b):
    return jnp.dot(a, b, preferred_element_type=F32)


def _dot_nt(a, b):
    return lax.dot_general(a, b, (((1,), (1,)), ((), ())), preferred_element_type=F32)


def _dot_tn(a, b):
    return lax.dot_general(a, b, (((0,), (0,)), ((), ())), preferred_element_type=F32)


def _rms(x, g=None):
    y = x * lax.rsqrt(jnp.mean(x * x, axis=-1, keepdims=True) + RMS_EPS)
    return y if g is None else y * g


def _sigmoid(x):
    return 1.0 / (1.0 + jnp.exp(-x))


def _params(sem):
    return pltpu.CompilerParams(dimension_semantics=sem, vmem_limit_bytes=VMEM_LIMIT)


def _tile_step():
    n_s = pl.num_programs(1)
    return pl.program_id(0) * n_s + pl.program_id(1), pl.num_programs(0) * n_s


def _row_copies(hbm, buf, sem, step_idx, slot_idx, to_hbm):
    n_s = pl.num_programs(1)
    c0 = (step_idx // n_s) * C_TILE
    s0 = (step_idx % n_s) * S_TILE
    out = []
    for j in range(S_TILE):
        far, near = hbm.at[pl.ds(c0, C_TILE), s0 + j, :], buf.at[slot_idx, j]
        src, dst = (near, far) if to_hbm else (far, near)
        out.append(pltpu.make_async_copy(src, dst, sem.at[slot_idx]))
    return out


def _load_time_major_rows(hbm, buf, sem):
    step, n_steps = _tile_step()
    slot = step % 2

    @pl.when(step == 0)
    def _():
        for cp in _row_copies(hbm, buf, sem, step, slot, False):
            cp.start()

    @pl.when(step + 1 < n_steps)
    def _():
        for cp in _row_copies(hbm, buf, sem, step + 1, 1 - slot, False):
            cp.start()

    for cp in _row_copies(hbm, buf, sem, step, slot, False):
        cp.wait()
    return buf[slot].reshape(S_TILE * C_TILE, buf.shape[-1])


def _store_time_major_rows(val, hbm, buf, sem):
    step, n_steps = _tile_step()
    slot = step % 2

    @pl.when(step >= 2)
    def _():
        for cp in _row_copies(hbm, buf, sem, step - 2, slot, True):
            cp.wait()

    buf[slot] = val.reshape(S_TILE, C_TILE, val.shape[-1])
    for cp in _row_copies(hbm, buf, sem, step, slot, True):
        cp.start()

    @pl.when(step == n_steps - 1)
    def _():
        @pl.when(step >= 1)
        def _():
            for cp in _row_copies(hbm, buf, sem, step - 1, 1 - slot, True):
                cp.wait()
        for cp in _row_copies(hbm, buf, sem, step, slot, True):
            cp.wait()


def _in_proj_kernel(x_hbm, wp_ref, wst_ref, zp_hbm, ut_ref, xbuf, zbuf, xsem, zsem):
    nc = C_TILE
    u = _rms(_load_time_major_rows(x_hbm, xbuf, xsem)).astype(BF16)
    zt = _dot_nt(wst_ref[...], u).astype(BF16)
    for j in range(S_TILE):
        ut_ref[:, j, :, :] = zt[:, j * nc:(j + 1) * nc].reshape(N_SSM_GROUPS, SSM_GROUP, nc)
    _store_time_major_rows(_dot(u, wp_ref[...]), zp_hbm, zbuf, zsem)


def _in_proj(x3, w_pool, w_ssm_t):
    nch = x3.shape[0]
    return pl.pallas_call(
        _in_proj_kernel,
        grid=(nch // C_TILE, CHUNK // S_TILE),
        in_specs=[
            pl.BlockSpec(memory_space=pl.ANY),
            pl.BlockSpec((D_MODEL, D_POOL), lambda c, s: (0, 0)),
            pl.BlockSpec((D_SSM, D_MODEL), lambda c, s: (0, 0)),
        ],
        out_specs=[
            pl.BlockSpec(memory_space=pl.ANY),
            pl.BlockSpec((N_SSM_GROUPS, None, S_TILE, SSM_GROUP, C_TILE), lambda c, s: (0, c, s, 0, 0)),
        ],
        out_shape=[
            jax.ShapeDtypeStruct((nch, CHUNK, D_POOL), F32),
            jax.ShapeDtypeStruct((N_SSM_GROUPS, nch // C_TILE, CHUNK, SSM_GROUP, C_TILE), BF16),
        ],
        scratch_shapes=[
            pltpu.VMEM((2, S_TILE, C_TILE, D_MODEL), F32),
            pltpu.VMEM((2, S_TILE, C_TILE, D_POOL), F32),
            pltpu.SemaphoreType.DMA((2,)),
            pltpu.SemaphoreType.DMA((2,)),
        ],
        compiler_params=_params(("arbitrary", "arbitrary")),
        name="in_proj",
    )(x3, w_pool, w_ssm_t)


def _shift_rows(x, d, row, n):
    if d == 0:
        return x
    r = pltpu.roll(x, d % n, 0)
    if d > 0:
        return jnp.where(row >= d, r, 0.0)
    return jnp.where(row < n + d, r, 0.0)


def _pool_kernel(z_ref, w_ref, sc_ref, o_ref):
    n = z_ref.shape[0]
    gi = pl.program_id(1)
    row = lax.broadcasted_iota(jnp.int32, (n, 1), 0)

    for k, w in enumerate(POOL_WINDOWS):
        @pl.when(gi == k)
        def _(w=w):
            x = z_ref[...]
            half = w // 2
            pd, pu, span = x, x, 1
            while span < half:
                pd = pd + _shift_rows(pd, span, row, n)
                pu = pu + _shift_rows(pu, -span, row, n)
                span *= 2
            total = _shift_rows(pd, 1, row, n) + pu
            lo = jnp.maximum(row - half, 0)
            hi = jnp.minimum(row + half, n)
            cnt = (hi - lo).astype(F32)
            diff = (total / cnt - x).astype(BF16)
            o_ref[...] = _dot(diff, w_ref[...].astype(BF16)) * sc_ref[...]


def _pool(zp3, pool_w, pool_scale):
    b, s, _ = zp3.shape
    return pl.pallas_call(
        _pool_kernel,
        grid=(b, len(POOL_WINDOWS)),
        in_specs=[
            pl.BlockSpec((None, s, POOL_GROUP), lambda i, g: (i, 0, g)),
            pl.BlockSpec((None, POOL_GROUP, POOL_GROUP), lambda i, g: (g, 0, 0)),
            pl.BlockSpec((1, POOL_GROUP), lambda i, g: (0, g)),
        ],
        out_specs=pl.BlockSpec((None, s, POOL_GROUP), lambda i, g: (i, 0, g)),
        out_shape=jax.ShapeDtypeStruct((b, s, D_POOL), F32),
        compiler_params=_params(("parallel", "parallel")),
        name="pool",
    )(zp3, pool_w, pool_scale)


def _expand_consts():
    time = np.arange(CHUNK_W) // SSM_GROUP
    def onehot(e):
        m = np.zeros((CHUNK_W, LANES), np.float32)
        m[np.arange(CHUNK_W), e] = 1.0
        return m
    return np.stack([
        onehot(CHUNK - 1 - time),
        onehot(time),
        onehot(time + 1),
        onehot(CHUNK - time),
    ])


def _cmul_packed(x, p, q):
    return x * p + pltpu.roll(x, LANES // 2, 1) * q


def _ssm_kernel(u_ref, vec_ref, mat_ref, exp_ref, wg_ref, wu_ref, wd_ref, y_ref, og_ref, ou_ref,
                od_ref, *, seq_chunks):
    og_ref[...] = wg_ref[...].astype(BF16)
    ou_ref[...] = wu_ref[...].astype(BF16)
    od_ref[...] = wd_ref[...].astype(BF16)

    n_ct = u_ref.shape[0]
    nch = n_ct * C_TILE
    half = LANES // 2
    lane = lax.broadcasted_iota(jnp.int32, (1, LANES), 1)
    lo_half = lane < half

    def direction(di):
        a_re = vec_ref[di, 0:1]
        a_im = vec_ref[di, 1:2]
        dt = jnp.exp(vec_ref[di, 2:3])
        mag = jnp.exp(a_re * dt)
        ang = a_im * dt
        lam = jnp.where(lo_half, mag * jnp.cos(ang), mag * jnp.sin(ang))
        lb_re = mag * jnp.cos(ang)
        lb_im = mag * jnp.sin(ang)
        den = a_re * a_re + a_im * a_im
        f_re = ((lb_re - 1.0) * a_re + lb_im * a_im) / den
        f_im = (lb_im * a_re - (lb_re - 1.0) * a_im) / den
        return lam, f_re, f_im

    def power_table(lam):
        e = lax.broadcasted_iota(jnp.int32, (LANES, 1), 0)
        tab = jnp.where(lo_half, 1.0, 0.0) * jnp.ones((LANES, 1), F32)
        sq = lam
        for k in range(7):
            p = jnp.where(lo_half, sq, pltpu.roll(sq, half, 1))
            q = jnp.where(lo_half, -pltpu.roll(sq, half, 1), sq)
            tab = jnp.where(((e >> k) & 1) == 1, _cmul_packed(tab, p, q), tab)
            sq = _cmul_packed(sq, p, q)
        return tab

    def tile_rows(x16):
        return jnp.broadcast_to(x16[None], (CHUNK, SSM_GROUP, LANES)).reshape(CHUNK_W, LANES)

    def expanded(tab, which, v_re, v_im, conj_sign):
        lexp = _dot(exp_ref[which], tab.astype(BF16))
        if conj_sign > 0:
            p = jnp.where(lo_half, v_re, v_re)
            q = jnp.where(lo_half, -v_im, v_im)
        else:
            p = jnp.where(lo_half, v_re, -v_re)
            q = jnp.where(lo_half, -v_im, -v_im)
        return lexp * tile_rows(p) + pltpu.roll(lexp, half, 1) * tile_rows(q)

    lam_f, ff_re, ff_im = direction(0)
    lam_b, fb_re, fb_im = direction(1)
    tab_f = power_table(lam_f)
    tab_b = power_table(lam_b)

    def bbar(bt_re, bt_im, f_re, f_im):
        return bt_re * f_re - bt_im * f_im, bt_re * f_im + bt_im * f_re

    def mat(di, k):
        return mat_ref[di, k * SSM_GROUP:(k + 1) * SSM_GROUP, :]

    bf_re, bf_im = bbar(mat(0, 0), mat(0, 1), ff_re, ff_im)
    bb_re, bb_im = bbar(mat(1, 0), mat(1, 1), fb_re, fb_im)

    pb1 = expanded(tab_f, 0, bf_re, bf_im, 1)
    pb2 = expanded(tab_b, 1, bb_re, bb_im, 1)
    pb3 = expanded(tab_b, 2, bb_re, bb_im, 1)
    ft_f = expanded(tab_f, 2, mat(0, 3), mat(0, 4), -1)
    ft_b = expanded(tab_b, 3, mat(1, 3), mat(1, 4), -1)

    row_w = lax.broadcasted_iota(jnp.int32, (CHUNK_W, 1), 0)
    last_blk = row_w >= CHUNK_W - SSM_GROUP
    pb2_lag0 = jnp.where(last_blk, pltpu.roll(pb2, CHUNK_W - SSM_GROUP, 0), 0.0)
    ccr_f = mat(0, 2).astype(BF16)
    ccr_b = mat(1, 2).astype(BF16)
    r_lo = _dot_nt(ccr_f, pb1.astype(BF16)) + _dot_nt(ccr_b, pb2_lag0.astype(BF16))
    co = lax.broadcasted_iota(jnp.int32, (SSM_GROUP, CHUNK_W), 0)
    col = lax.broadcasted_iota(jnp.int32, (SSM_GROUP, CHUNK_W), 1)
    r_lo = r_lo + jnp.where(col == CHUNK_W - SSM_GROUP + co, mat(0, 5)[:, 0:1], 0.0)
    r_hi = _dot_nt(ccr_b, pb3.astype(BF16))
    r_t = jnp.concatenate([r_lo, r_hi], axis=1)
    g_t = jnp.concatenate(
        [pltpu.roll(r_t, SSM_GROUP * (tl + 1), 1) for tl in range(T_SUB)], axis=0
    ).astype(BF16)

    u = jnp.concatenate([u_ref[ct].reshape(CHUNK_W, C_TILE) for ct in range(n_ct)],
                        axis=1)
    e_mat = jnp.concatenate([pb1, pb2], axis=1).astype(BF16)
    xend = _dot_tn(e_mat, u)
    lanec = lax.broadcasted_iota(jnp.int32, (1, nch), 1) % seq_chunks
    ns = SSM_STATE

    def scan(re, im, tab, forward):
        lam_col = jnp.transpose(tab[CHUNK:CHUNK + 8, :])[:, 0:1]
        a, b = lam_col[:ns], lam_col[ns:]
        n_steps = int(math.log2(seq_chunks))

        def shifted(v, d):
            if forward:
                return jnp.where(lanec >= d, pltpu.roll(v, d, 1), 0.0)
            return jnp.where(lanec < seq_chunks - d, pltpu.roll(v, nch - d, 1), 0.0)

        for k in range(n_steps):
            sr, si = shifted(re, 1 << k), shifted(im, 1 << k)
            re, im = re + (sr * a - si * b), im + (sr * b + si * a)
            a, b = a * a - b * b, 2.0 * a * b
        return shifted(re, 1), shifted(im, 1)

    f_re, f_im = scan(xend[:ns], xend[ns:2 * ns], tab_f, True)
    b_re, b_im = scan(xend[2 * ns:3 * ns], xend[3 * ns:], tab_b, False)
    xin = jnp.concatenate([f_re, f_im, b_re, b_im], axis=0).astype(BF16)
    f_t = jnp.concatenate([ft_f, ft_b], axis=1).astype(BF16)

    toeplitz = jnp.concatenate(
        [g_t[:, CHUNK_W - LANES * th:2 * CHUNK_W - LANES * th] for th in range(CHUNK // T_SUB)],
        axis=0)
    y_t = _dot(toeplitz, u) + _dot(f_t, xin)
    for ct in range(n_ct):
        y_ref[ct] = y_t[:, ct * C_TILE:(ct + 1) * C_TILE].reshape(
            CHUNK, SSM_GROUP, C_TILE).astype(y_ref.dtype)


def _ssm(ut, a_re, a_im, log_dt, b_re, b_im, c_re, c_im, d, seq_chunks, w_gate, w_up, w_down, layer):
    g, n_ct = ut.shape[:2]
    assert N_EXPERTS % g == 0
    epg = N_EXPERTS // g
    base = layer * g
    n = SSM_STATE

    def per_group(a):
        return jnp.swapaxes(a, 0, 1)

    def dup(a):
        return jnp.concatenate([a, a], axis=-1)

    vecs = dup(jnp.stack([per_group(a_re), per_group(a_im),
                          jnp.broadcast_to(per_group(log_dt)[..., None], (g, 2, n))], axis=2))
    cr, ci = per_group(c_re), per_group(c_im)
    d_blk = jnp.broadcast_to(d.reshape(g, 1, SSM_GROUP, 1), (g, 2, SSM_GROUP, LANES))
    mats = jnp.concatenate(
        [dup(jnp.swapaxes(per_group(b_re), 2, 3)), dup(jnp.swapaxes(per_group(b_im), 2, 3)),
         jnp.concatenate([cr, -ci], axis=-1), dup(cr), dup(ci), d_blk], axis=2)
    exp_c = jnp.asarray(_expand_consts(), BF16)

    return pl.pallas_call(
        functools.partial(_ssm_kernel, seq_chunks=seq_chunks),
        grid=(g,),
        in_specs=[
            pl.BlockSpec((None, n_ct, CHUNK, SSM_GROUP, C_TILE), lambda i: (i, 0, 0, 0, 0)),
            pl.BlockSpec((None, 2, 3, LANES), lambda i: (i, 0, 0, 0)),
            pl.BlockSpec((None, 2, 6 * SSM_GROUP, LANES), lambda i: (i, 0, 0, 0)),
            pl.BlockSpec((4, CHUNK_W, LANES), lambda i: (0, 0, 0)),
            pl.BlockSpec((epg, D_MODEL, D_EXPERT), lambda i: (base + i, 0, 0)),
            pl.BlockSpec((epg, D_MODEL, D_EXPERT), lambda i: (base + i, 0, 0)),
            pl.BlockSpec((epg, D_EXPERT, D_MODEL), lambda i: (base + i, 0, 0)),
        ],
        out_specs=[
            pl.BlockSpec((None, n_ct, CHUNK, SSM_GROUP, C_TILE), lambda i: (i, 0, 0, 0, 0)),
            pl.BlockSpec((epg, D_MODEL, D_EXPERT), lambda i: (i, 0, 0)),
            pl.BlockSpec((epg, D_MODEL, D_EXPERT), lambda i: (i, 0, 0)),
            pl.BlockSpec((epg, D_EXPERT, D_MODEL), lambda i: (i, 0, 0)),
        ],
        out_shape=[
            jax.ShapeDtypeStruct(ut.shape, BF16),
            jax.ShapeDtypeStruct((N_EXPERTS, D_MODEL, D_EXPERT), BF16),
            jax.ShapeDtypeStruct((N_EXPERTS, D_MODEL, D_EXPERT), BF16),
            jax.ShapeDtypeStruct((N_EXPERTS, D_EXPERT, D_MODEL), BF16),
        ],
        compiler_params=_params(("parallel",)),
        name="ssm",
    )(ut, vecs, mats, exp_c, w_gate, w_up, w_down)


def _mix_out_kernel(x_hbm, a_hbm, yt_ref, gwt_ref, gb_ref, wo_ref, h_hbm, xbuf, abuf, hbuf,
                    xsem, asem, hsem):
    nc = C_TILE
    x = _load_time_major_rows(x_hbm, xbuf, xsem)
    a = _load_time_major_rows(a_hbm, abuf, asem).astype(BF16)
    y = jnp.concatenate([yt_ref[:, j, :, :].reshape(D_SSM, nc) for j in range(S_TILE)],
                        axis=1).astype(F32)
    z = 0.5 * y * (1.0 + jnp.tanh(math.sqrt(2.0 / math.pi) * (y + 0.044715 * (y * y * y))))
    gate = _sigmoid(_dot(gwt_ref[...], z.astype(BF16)) + gb_ref[...])
    s = (z * gate).astype(BF16)
    h = x + _dot(a, wo_ref[:D_POOL, :]) + _dot_tn(s, wo_ref[D_POOL:, :])
    _store_time_major_rows(h, h_hbm, hbuf, hsem)


def _mix_out(x3, a3, yt, glu_w_t, glu_b_col, w_out):
    nch = x3.shape[0]
    return pl.pallas_call(
        _mix_out_kernel,
        grid=(nch // C_TILE, CHUNK // S_TILE),
        in_specs=[
            pl.BlockSpec(memory_space=pl.ANY),
            pl.BlockSpec(memory_space=pl.ANY),
            pl.BlockSpec((N_SSM_GROUPS, None, S_TILE, SSM_GROUP, C_TILE), lambda c, t: (0, c, t, 0, 0)),
            pl.BlockSpec((D_SSM, D_SSM), lambda c, t: (0, 0)),
            pl.BlockSpec((D_SSM, 1), lambda c, t: (0, 0)),
            pl.BlockSpec((D_MODEL, D_MODEL), lambda c, t: (0, 0)),
        ],
        out_specs=pl.BlockSpec(memory_space=pl.ANY),
        out_shape=jax.ShapeDtypeStruct((nch, CHUNK, D_MODEL), F32),
        scratch_shapes=[
            pltpu.VMEM((2, S_TILE, C_TILE, D_MODEL), F32),
            pltpu.VMEM((2, S_TILE, C_TILE, D_POOL), F32),
            pltpu.VMEM((2, S_TILE, C_TILE, D_MODEL), F32),
            pltpu.SemaphoreType.DMA((2,)),
            pltpu.SemaphoreType.DMA((2,)),
            pltpu.SemaphoreType.DMA((2,)),
        ],
        compiler_params=_params(("arbitrary", "arbitrary")),
        name="mix_out",
    )(x3, a3, yt, glu_w_t, glu_b_col, w_out)


def _pack_rows(x):
    b = lax.bitcast_convert_type(x.astype(BF16).astype(F32), U32)
    return (b[:, :HALF] & jnp.uint32(0xFFFF0000)) | (b[:, HALF:] >> 16)


def _unpack_rows(w):
    lo = lax.bitcast_convert_type(w & jnp.uint32(0xFFFF0000), F32)
    hi = lax.bitcast_convert_type(w << 16, F32)
    return lo, hi


def _split_bf16(x):
    hi = x.astype(BF16)
    return hi, (x - hi.astype(F32)).astype(BF16)


def _route(v32, wr_ref, br_ref):
    v_hi, v_lo = _split_bf16(v32)
    w_hi, w_lo = _split_bf16(wr_ref[...])
    logits = _dot(v_hi, w_hi) + (_dot(v_lo, w_hi) + _dot(v_hi, w_lo)) + br_ref[...]
    return jnp.transpose(logits)


def _top1(x, valid=None):
    n = x.shape[0]
    row = lax.broadcasted_iota(jnp.int32, x.shape, 0).astype(F32)
    if valid is not None:
        x = jnp.where(valid, x, -jnp.inf)
    m = jnp.max(x, axis=0, keepdims=True)
    idx = jnp.min(jnp.where(x == m, row, float(n)), axis=0, keepdims=True)
    return m, idx, x, row


def _split_planes(packed, ref):
    ref[0] = packed[:, :SC_ROW]
    ref[1] = packed[:, SC_ROW:]


def _router_kernel(h_ref, g_ref, wr_ref, br_ref, before_ref, vp_ref, meta_t_ref, cnt_ref, carry_ref):
    @pl.when(pl.program_id(0) == 0)
    def _():
        carry_ref[...] = jnp.zeros_like(carry_ref)

    v32 = _rms(h_ref[...], g_ref[...])
    _split_planes(_pack_rows(v32), vp_ref)
    lt = _route(v32, wr_ref, br_ref)
    tm = lt.shape[1]
    eg = EXPERTS_PER_GROUP

    grp = lt[:eg]
    grp_row = lax.broadcasted_iota(jnp.int32, grp.shape, 0)
    mg, grp_idx, grp, _ = _top1(grp, grp_row < N_EXPERT_GROUPS)
    grp_p = 1.0 / jnp.sum(jnp.exp(grp - mg), axis=0, keepdims=True)
    le = jnp.zeros((eg, tm), F32)
    for g in range(N_EXPERT_GROUPS):
        le = jnp.where(grp_idx == float(g), lt[eg * (g + 1):eg * (g + 2)], le)
    m1, i1, le, row = _top1(le)
    z = jnp.sum(jnp.exp(le - m1), axis=0, keepdims=True)
    m2, i2, _, _ = _top1(jnp.where(row == i1, -jnp.inf, le))
    p1 = 1.0 / z
    p2 = jnp.exp(m2 - m1) / z
    tot = p1 + p2
    w1 = grp_p * (p1 / tot)
    w2 = grp_p * (p2 / tot)
    e1 = grp_idx * eg + i1
    e2 = grp_idx * eg + i2

    erow = lax.broadcasted_iota(jnp.int32, (N_EXPERTS, tm), 0).astype(F32)
    onehot = jnp.where(erow == e1, 1.0, jnp.where(erow == e2, 1.0, 0.0))
    before = _dot(onehot.astype(BF16), before_ref[...]) + carry_ref[...]
    rank1 = jnp.sum(jnp.where(erow == e1, before, 0.0), axis=0, keepdims=True)
    rank2 = jnp.sum(jnp.where(erow == e2, before, 0.0), axis=0, keepdims=True)
    carry = carry_ref[...] + jnp.sum(onehot, axis=1, keepdims=True)
    carry_ref[...] = carry
    cnt_ref[...] = carry

    mrow = lax.broadcasted_iota(jnp.int32, (META_ROWS, tm), 0)
    meta_t_ref[...] = jnp.where(mrow == 0, e1, jnp.where(mrow == 1, e2, jnp.where(
        mrow == 2, rank1, jnp.where(mrow == 3, rank2, jnp.where(
            mrow == 4, w1, jnp.where(mrow == 5, w2, 0.0))))))


def _earlier_matrix(tm):
    return np.triu(np.ones((tm, tm), np.float32), k=1)


def _router(h1, g_ffn, w_router, b_router, tm, part):
    t = h1.shape[0] // MOE_PARTS
    off = part * (t // tm)
    return pl.pallas_call(
        _router_kernel,
        grid=(t // tm,),
        in_specs=[
            pl.BlockSpec((tm, D_MODEL), lambda i: (i + off, 0)),
            pl.BlockSpec((1, D_MODEL), lambda i: (0, 0)),
            pl.BlockSpec((D_MODEL, ROUTER_W), lambda i: (0, 0)),
            pl.BlockSpec((1, ROUTER_W), lambda i: (0, 0)),
            pl.BlockSpec((tm, tm), lambda i: (0, 0)),
        ],
        out_specs=[
            pl.BlockSpec((2, tm, SC_ROW), lambda i: (0, i, 0)),
            pl.BlockSpec((META_ROWS, tm), lambda i: (0, i)),
            pl.BlockSpec((N_EXPERTS, 1), lambda i: (0, 0)),
        ],
        out_shape=[
            jax.ShapeDtypeStruct((2, t, SC_ROW), U32),
            jax.ShapeDtypeStruct((META_ROWS, t), F32),
            jax.ShapeDtypeStruct((N_EXPERTS, 1), F32),
        ],
        scratch_shapes=[pltpu.VMEM((N_EXPERTS, 1), F32)],
        compiler_params=_params(("arbitrary",)),
        name="router",
    )(h1, g_ffn, w_router, b_router, jnp.asarray(_earlier_matrix(tm), BF16))


def _plan(meta_t, counts, n_tiles):
    e1 = meta_t[0].astype(jnp.int32)
    e2 = meta_t[1].astype(jnp.int32)
    rank1 = meta_t[2].astype(jnp.int32)
    rank2 = meta_t[3].astype(jnp.int32)
    cnt = counts[:, 0].astype(jnp.int32)
    padded = ((cnt + ROW_TILE - 1) // ROW_TILE) * ROW_TILE
    ends = jnp.cumsum(padded)
    starts = ends - padded
    experts = jnp.arange(N_EXPERTS, dtype=jnp.int32)
    pos1 = rank1 + jnp.sum(jnp.where(e1[None, :] == experts[:, None], starts[:, None], 0), axis=0)
    pos2 = rank2 + jnp.sum(jnp.where(e2[None, :] == experts[:, None], starts[:, None], 0), axis=0)
    tile_start = jnp.arange(n_tiles, dtype=jnp.int32) * ROW_TILE
    tile_expert = jnp.sum((tile_start[:, None] >= ends[None, :]).astype(jnp.int32), axis=1)
    tile_expert = jnp.minimum(tile_expert, N_EXPERTS - 1)
    rows_left = jnp.sum(jnp.where(tile_expert[:, None] == experts, cnt + starts, 0), axis=1) - tile_start
    n_valid = jnp.clip(rows_left, 0, ROW_TILE).astype(jnp.int32)
    last_used = jnp.maximum(ends[-1] // ROW_TILE - 1, 0)
    block = jnp.minimum(jnp.arange(n_tiles, dtype=jnp.int32), last_used)
    tile_expert = jnp.sum(jnp.where(block[:, None] == jnp.arange(n_tiles)[None, :],
                                    tile_expert[None, :], 0), axis=1)
    plane = n_tiles * ROW_TILE
    half_rows = jnp.concatenate([pos1, pos1 + plane, pos2, pos2 + plane])[None]
    return half_rows, tile_expert, n_valid, block


def _sc_mesh():
    return plsc.VectorSubcoreMesh(core_axis_name="c", subcore_axis_name="s")


def _sc_scatter_rows(rows, idx, n_out):
    t, width = rows.shape
    steps = t // SC_WINDOW

    @pl.kernel(out_type=jax.ShapeDtypeStruct((n_out, width), rows.dtype), mesh=_sc_mesh(),
               scratch_types=[], name="moe_scatter")
    def scatter(rows_hbm, idx_hbm, out_hbm):
        def body(rows_vmem, idx_vmem):
            pltpu.sync_copy(rows_vmem, out_hbm.at[idx_vmem.at[0]])

        pltpu.emit_pipeline(
            body,
            grid=(2, steps),
            in_specs=[pl.BlockSpec((SC_WINDOW, width), lambda k, j: (j, 0)),
                      pl.BlockSpec((1, SC_WINDOW), lambda k, j: (0, k * steps + j))],
            out_specs=[],
            core_axis_name=("c", "s"),
            dimension_semantics=(pltpu.PARALLEL, pltpu.PARALLEL),
        )(rows_hbm, idx_hbm)

    return scatter(rows, idx)


def _sc_gather_rows(table, idx):
    m = idx.shape[1]
    width = table.shape[1]
    steps = m // (2 * SC_WINDOW)

    @pl.kernel(out_type=jax.ShapeDtypeStruct((m, width), table.dtype), mesh=_sc_mesh(),
               scratch_types=[], name="moe_gather")
    def gather(table_hbm, idx_hbm, out_hbm):
        def body(idx_vmem, out_vmem):
            pltpu.sync_copy(table_hbm.at[idx_vmem.at[0]], out_vmem)

        pltpu.emit_pipeline(
            body,
            grid=(2, steps),
            in_specs=[pl.BlockSpec((1, SC_WINDOW), lambda k, j: (0, k * steps + j))],
            out_specs=[pl.BlockSpec((SC_WINDOW, width), lambda k, j: (k * steps + j, 0))],
            core_axis_name=("c", "s"),
            dimension_semantics=(pltpu.PARALLEL, pltpu.PARALLEL),
        )(idx_hbm, out_hbm)

    return gather(table, idx)


def _experts_kernel(te_ref, nv_ref, blk_ref, xs_ref, wg_ref, wu_ref, wd_ref, ys_ref):
    r = pl.program_id(0)
    n_valid = nv_ref[r]

    @pl.when(n_valid > 0)
    def _():
        parts = [p.astype(BF16) for p in _unpack_rows(xs_ref[0]) + _unpack_rows(xs_ref[1])]
        cols = (0, 2 * SC_ROW, SC_ROW, 3 * SC_ROW)
        hg = sum(_dot(p, wg_ref[c:c + SC_ROW, :]) for p, c in zip(parts, cols))
        hu = sum(_dot(p, wu_ref[c:c + SC_ROW, :]) for p, c in zip(parts, cols))
        row = lax.broadcasted_iota(jnp.int32, (ROW_TILE, 1), 0)
        hid = jnp.where(row < n_valid, hg * _sigmoid(hg) * hu, 0.0).astype(BF16)
        _split_planes(_pack_rows(_dot(hid, wd_ref[...])), ys_ref)


def _experts(xs, tile_expert, n_valid, block, w_gate, w_up, w_down):
    n_tiles = xs.shape[1] // ROW_TILE
    w_spec = pl.BlockSpec((None, D_MODEL, D_EXPERT), lambda r, te, nv, blk: (te[r], 0, 0))
    grid_spec = pltpu.PrefetchScalarGridSpec(
        num_scalar_prefetch=3,
        grid=(n_tiles,),
        in_specs=[
            pl.BlockSpec((2, ROW_TILE, SC_ROW), lambda r, te, nv, blk: (0, blk[r], 0)),
            w_spec, w_spec,
            pl.BlockSpec((None, D_EXPERT, D_MODEL), lambda r, te, nv, blk: (te[r], 0, 0)),
        ],
        out_specs=pl.BlockSpec((2, ROW_TILE, SC_ROW), lambda r, te, nv, blk: (0, blk[r], 0)),
    )
    return pl.pallas_call(
        _experts_kernel,
        grid_spec=grid_spec,
        out_shape=jax.ShapeDtypeStruct((2, n_tiles * ROW_TILE, SC_ROW), U32),
        compiler_params=_params(("arbitrary",)),
        name="experts",
    )(tile_expert, n_valid, block, xs, w_gate, w_up, w_down)


def _ple_kernel(h_ref, yg_ref, meta_t_ref, p_ref, wg_ref, bg_ref, wp_ref, gf_ref, *rest, final_norm):
    o_ref = rest[-1]
    meta = jnp.transpose(meta_t_ref[...])
    w1 = meta[:, 4:5]
    w2 = meta[:, 5:6]
    q0, q2 = (w1 * u + w2 * v for u, v in zip(_unpack_rows(yg_ref[0]), _unpack_rows(yg_ref[2])))
    q1, q3 = (w1 * u + w2 * v for u, v in zip(_unpack_rows(yg_ref[1]), _unpack_rows(yg_ref[3])))
    moe = jnp.concatenate([q0, q1, q2, q3], axis=1)
    h = h_ref[...] + moe
    gate = _sigmoid(_dot(_rms(h).astype(BF16), wg_ref[...]) + bg_ref[...])
    h = h + gate * _dot(p_ref[...].astype(BF16), wp_ref[...])
    o_ref[...] = _rms(h, gf_ref[...]) if final_norm else h


def _ple(h1, yg, meta_t, p2, w_gate, b_gate, w_proj, g_final, final_norm, tm, part, prev_out):
    t = h1.shape[0]
    steps = t // tm // MOE_PARTS
    off = part * steps
    in_specs = [
        pl.BlockSpec((tm, D_MODEL), lambda i: (i + off, 0)),
        pl.BlockSpec((4, tm, SC_ROW), lambda i: (0, i, 0)),
        pl.BlockSpec((META_ROWS, tm), lambda i: (0, i)),
        pl.BlockSpec((tm, D_PLE), lambda i: (i + off, 0)),
        pl.BlockSpec((D_MODEL, D_MODEL), lambda i: (0, 0)),
        pl.BlockSpec((1, D_MODEL), lambda i: (0, 0)),
        pl.BlockSpec((D_PLE, D_MODEL), lambda i: (0, 0)),
        pl.BlockSpec((1, D_MODEL), lambda i: (0, 0)),
    ]
    args = [h1, yg, meta_t, p2, w_gate, b_gate, w_proj, g_final]
    aliases = {}
    if prev_out is not None:
        in_specs.append(pl.BlockSpec(memory_space=pl.ANY))
        args.append(prev_out)
        aliases = {len(args) - 1: 0}
    return pl.pallas_call(
        functools.partial(_ple_kernel, final_norm=final_norm),
        grid=(steps,),
        in_specs=in_specs,
        out_specs=pl.BlockSpec((tm, D_MODEL), lambda i: (i + off, 0)),
        out_shape=jax.ShapeDtypeStruct((t, D_MODEL), F32),
        input_output_aliases=aliases,
        compiler_params=_params(("parallel",)),
        name="ple",
    )(*args)


def kernel(x, p, g_mix, w_in, pool_w, pool_scale, ssm_a_re, ssm_a_im, ssm_log_dt, ssm_b_re,
           ssm_b_im, ssm_c_re, ssm_c_im, ssm_d, glu_w, glu_b, w_out, g_ffn, router_grp_w,
           router_grp_b, router_exp_w, router_exp_b, exp_w_gate, exp_w_up, exp_w_down, g_ple,
           ple_gate_w, ple_gate_b, ple_proj_w, g_final):
    bsz, seq, dm = x.shape
    depth = g_mix.shape[0]
    t = bsz * seq
    seq_chunks = seq // CHUNK
    nch = t // CHUNK
    tm = 1024
    tp = t // MOE_PARTS
    n_sorted = 2 * tp + N_EXPERTS * ROW_TILE
    w_gate_all = exp_w_gate.reshape(depth * N_EXPERTS, dm, D_EXPERT)
    w_up_all = exp_w_up.reshape(depth * N_EXPERTS, dm, D_EXPERT)
    w_down_all = exp_w_down.reshape(depth * N_EXPERTS, D_EXPERT, dm)

    h = x.reshape(t, dm)
    for i in range(depth):
        w_in_b = (g_mix[i][:, None] * w_in[i]).astype(BF16)
        zp, ut = _in_proj(h.reshape(nch, CHUNK, dm), w_in_b[:, :D_POOL],
                          jnp.transpose(w_in_b[:, D_POOL:]))
        a = _pool(zp.reshape(bsz, seq, D_POOL), pool_w[i], pool_scale[i][None])
        yt, w_gate_b, w_up_b, w_down_b = _ssm(
            ut, ssm_a_re[i], ssm_a_im[i], ssm_log_dt[i], ssm_b_re[i], ssm_b_im[i], ssm_c_re[i],
            ssm_c_im[i], ssm_d[i], seq_chunks, w_gate_all, w_up_all, w_down_all, i)
        h = _mix_out(h.reshape(nch, CHUNK, dm), a.reshape(nch, CHUNK, D_POOL), yt,
                     jnp.transpose(glu_w[i]).astype(BF16), glu_b[i][:, None],
                     w_out[i].astype(BF16)).reshape(t, dm)

        eg = EXPERTS_PER_GROUP
        w_router = jnp.concatenate(
            [router_grp_w[i], jnp.zeros((dm, eg - N_EXPERT_GROUPS), F32),
             jnp.transpose(router_exp_w[i], (1, 0, 2)).reshape(dm, N_EXPERTS),
             jnp.zeros((dm, ROUTER_W - eg - N_EXPERTS), F32)], axis=1)
        b_router = jnp.concatenate(
            [router_grp_b[i], jnp.zeros((eg - N_EXPERT_GROUPS,), F32),
             router_exp_b[i].reshape(N_EXPERTS),
             jnp.zeros((ROUTER_W - eg - N_EXPERTS,), F32)])[None]
        ple_wg = (g_ple[i][:, None] * ple_gate_w[i]).astype(BF16)
        ple_wp = ple_proj_w[i].astype(BF16)
        plans = []
        for q in range(MOE_PARTS):
            vp, meta_t, counts = _router(h, g_ffn[i][None], w_router, b_router, tm, q)
            idx, tile_expert, n_valid, block = _plan(meta_t, counts, n_sorted // ROW_TILE)
            xs = _sc_scatter_rows(vp.reshape(2 * tp, SC_ROW), idx, 2 * n_sorted)
            plans.append((xs, idx, tile_expert, n_valid, block, meta_t))
        outs = []
        for xs, idx, tile_expert, n_valid, block, meta_t in plans:
            ys = _experts(xs.reshape(2, n_sorted, SC_ROW), tile_expert, n_valid, block,
                          w_gate_b, w_up_b, w_down_b)
            outs.append(_sc_gather_rows(ys.reshape(2 * n_sorted, SC_ROW), idx))
        out = None
        for q, (yg, plan) in enumerate(zip(outs, plans)):
            out = _ple(h, yg.reshape(4, tp, SC_ROW), plan[5], p[i].reshape(t, D_PLE), ple_wg,
                       ple_gate_b[i][None], ple_wp, g_final[None], i == depth - 1, tm, q, out)
        h = out
    return h.reshape(bsz, seq, dm)
```

```python
import functools
import math

import numpy as np
import jax
import jax.numpy as jnp
from jax import lax
from jax.experimental import pallas as pl
from jax.experimental.pallas import tpu as pltpu
from jax.experimental.pallas import tpu_sc as plsc

F32 = jnp.float32
BF16 = jnp.bfloat16
U32 = jnp.uint32

D_MODEL = 1024
D_POOL = 512
D_SSM = 512
POOL_WINDOWS = (2, 4, 8, 16)
POOL_GROUP = 128
SSM_GROUP = 16
N_SSM_GROUPS = 32
SSM_STATE = 64
N_EXPERT_GROUPS = 4
EXPERTS_PER_GROUP = 8
N_EXPERTS = N_EXPERT_GROUPS * EXPERTS_PER_GROUP
D_EXPERT = 256
D_PLE = 256
RMS_EPS = 1e-6

LANES = 128
CHUNK = 32
CHUNK_W = CHUNK * SSM_GROUP
T_SUB = 8
S_TILE = 8
C_TILE = 128
ROUTER_W = LANES
HALF = D_MODEL // 2
ROW_TILE = 512
SC_WINDOW = 128
SC_ROW = HALF // 2
META_ROWS = 8
GATHER_PARTS = 2
VMEM_LIMIT = 56 * 1024 * 1024


def _dot(a, b):
    return jnp.dot(a, b, preferred_element_type=F32)


def _dot_nt(a, b):
    return lax.dot_general(a, b, (((1,), (1,)), ((), ())), preferred_element_type=F32)


def _dot_tn(a, b):
    return lax.dot_general(a, b, (((0,), (0,)), ((), ())), preferred_element_type=F32)


def _rms(x, g=None):
    y = x * lax.rsqrt(jnp.mean(x * x, axis=-1, keepdims=True) + RMS_EPS)
    return y if g is None else y * g


def _sigmoid(x):
    return 1.0 / (1.0 + jnp.exp(-x))


def _params(sem):
    return pltpu.CompilerParams(dimension_semantics=sem, vmem_limit_bytes=VMEM_LIMIT)


def _tile_step():
    n_s = pl.num_programs(1)
    return pl.program_id(0) * n_s + pl.program_id(1), pl.num_programs(0) * n_s


def _row_copies(hbm, buf, sem, step_idx, slot_idx, to_hbm):
    n_s = pl.num_programs(1)
    c0 = (step_idx // n_s) * C_TILE
    s0 = (step_idx % n_s) * S_TILE
    out = []
    for j in range(S_TILE):
        far, near = hbm.at[pl.ds(c0, C_TILE), s0 + j, :], buf.at[slot_idx, j]
        src, dst = (near, far) if to_hbm else (far, near)
        out.append(pltpu.make_async_copy(src, dst, sem.at[slot_idx]))
    return out


def _load_time_major_rows(hbm, buf, sem):
    step, n_steps = _tile_step()
    slot = step % 2

    @pl.when(step == 0)
    def _():
        for cp in _row_copies(hbm, buf, sem, step, slot, False):
            cp.start()

    @pl.when(step + 1 < n_steps)
    def _():
        for cp in _row_copies(hbm, buf, sem, step + 1, 1 - slot, False):
            cp.start()

    for cp in _row_copies(hbm, buf, sem, step, slot, False):
        cp.wait()
    return buf[slot].reshape(S_TILE * C_TILE, buf.shape[-1])


def _store_time_major_rows(val, hbm, buf, sem):
    step, n_steps = _tile_step()
    slot = step % 2

    @pl.when(step >= 2)
    def _():
        for cp in _row_copies(hbm, buf, sem, step - 2, slot, True):
            cp.wait()

    buf[slot] = val.reshape(S_TILE, C_TILE, val.shape[-1])
    for cp in _row_copies(hbm, buf, sem, step, slot, True):
        cp.start()

    @pl.when(step == n_steps - 1)
    def _():
        @pl.when(step >= 1)
        def _():
            for cp in _row_copies(hbm, buf, sem, step - 1, 1 - slot, True):
                cp.wait()
        for cp in _row_copies(hbm, buf, sem, step, slot, True):
            cp.wait()


def _in_proj_kernel(x_hbm, wp_ref, wst_ref, zp_hbm, ut_ref, xbuf, zbuf, xsem, zsem):
    nc = C_TILE
    u = _rms(_load_time_major_rows(x_hbm, xbuf, xsem)).astype(BF16)
    zt = _dot_nt(wst_ref[...], u).astype(BF16)
    for j in range(S_TILE):
        ut_ref[:, j, :, :] = zt[:, j * nc:(j + 1) * nc].reshape(N_SSM_GROUPS, SSM_GROUP, nc)
    _store_time_major_rows(_dot(u, wp_ref[...]), zp_hbm, zbuf, zsem)


def _in_proj(x3, w_pool, w_ssm_t):
    nch = x3.shape[0]
    return pl.pallas_call(
        _in_proj_kernel,
        grid=(nch // C_TILE, CHUNK // S_TILE),
        in_specs=[
            pl.BlockSpec(memory_space=pl.ANY),
            pl.BlockSpec((D_MODEL, D_POOL), lambda c, s: (0, 0)),
            pl.BlockSpec((D_SSM, D_MODEL), lambda c, s: (0, 0)),
        ],
        out_specs=[
            pl.BlockSpec(memory_space=pl.ANY),
            pl.BlockSpec((N_SSM_GROUPS, None, S_TILE, SSM_GROUP, C_TILE), lambda c, s: (0, c, s, 0, 0)),
        ],
        out_shape=[
            jax.ShapeDtypeStruct((nch, CHUNK, D_POOL), F32),
            jax.ShapeDtypeStruct((N_SSM_GROUPS, nch // C_TILE, CHUNK, SSM_GROUP, C_TILE), BF16),
        ],
        scratch_shapes=[
            pltpu.VMEM((2, S_TILE, C_TILE, D_MODEL), F32),
            pltpu.VMEM((2, S_TILE, C_TILE, D_POOL), F32),
            pltpu.SemaphoreType.DMA((2,)),
            pltpu.SemaphoreType.DMA((2,)),
        ],
        compiler_params=_params(("arbitrary", "arbitrary")),
        name="in_proj",
    )(x3, w_pool, w_ssm_t)


def _shift_rows(x, d, row, n):
    if d == 0:
        return x
    r = pltpu.roll(x, d % n, 0)
    if d > 0:
        return jnp.where(row >= d, r, 0.0)
    return jnp.where(row < n + d, r, 0.0)


def _pool_kernel(z_ref, w_ref, sc_ref, o_ref):
    n = z_ref.shape[0]
    gi = pl.program_id(1)
    row = lax.broadcasted_iota(jnp.int32, (n, 1), 0)

    for k, w in enumerate(POOL_WINDOWS):
        @pl.when(gi == k)
        def _(w=w):
            x = z_ref[...]
            half = w // 2
            pd, pu, span = x, x, 1
            while span < half:
                pd = pd + _shift_rows(pd, span, row, n)
                pu = pu + _shift_rows(pu, -span, row, n)
                span *= 2
            total = _shift_rows(pd, 1, row, n) + pu
            lo = jnp.maximum(row - half, 0)
            hi = jnp.minimum(row + half, n)
            cnt = (hi - lo).astype(F32)
            diff = (total / cnt - x).astype(BF16)
            o_ref[...] = _dot(diff, w_ref[...].astype(BF16)) * sc_ref[...]


def _pool(zp3, pool_w, pool_scale):
    b, s, _ = zp3.shape
    return pl.pallas_call(
        _pool_kernel,
        grid=(b, len(POOL_WINDOWS)),
        in_specs=[
            pl.BlockSpec((None, s, POOL_GROUP), lambda i, g: (i, 0, g)),
            pl.BlockSpec((None, POOL_GROUP, POOL_GROUP), lambda i, g: (g, 0, 0)),
            pl.BlockSpec((1, POOL_GROUP), lambda i, g: (0, g)),
        ],
        out_specs=pl.BlockSpec((None, s, POOL_GROUP), lambda i, g: (i, 0, g)),
        out_shape=jax.ShapeDtypeStruct((b, s, D_POOL), F32),
        compiler_params=_params(("parallel", "parallel")),
        name="pool",
    )(zp3, pool_w, pool_scale)


def _expand_consts():
    time = np.arange(CHUNK_W) // SSM_GROUP
    def onehot(e):
        m = np.zeros((CHUNK_W, LANES), np.float32)
        m[np.arange(CHUNK_W), e] = 1.0
        return m
    return np.stack([
        onehot(CHUNK - 1 - time),
        onehot(time),
        onehot(time + 1),
        onehot(CHUNK - time),
    ])


def _cmul_packed(x, p, q):
    return x * p + pltpu.roll(x, LANES // 2, 1) * q


def _ssm_kernel(u_ref, vec_ref, mat_ref, exp_ref, wg_ref, wu_ref, wd_ref, y_ref, og_ref, ou_ref,
                od_ref, *, seq_chunks):
    og_ref[...] = wg_ref[...].astype(BF16)
    ou_ref[...] = wu_ref[...].astype(BF16)
    od_ref[...] = wd_ref[...].astype(BF16)

    n_ct = u_ref.shape[0]
    nch = n_ct * C_TILE
    half = LANES // 2
    lane = lax.broadcasted_iota(jnp.int32, (1, LANES), 1)
    lo_half = lane < half

    def direction(di):
        a_re = vec_ref[di, 0:1]
        a_im = vec_ref[di, 1:2]
        dt = jnp.exp(vec_ref[di, 2:3])
        mag = jnp.exp(a_re * dt)
        ang = a_im * dt
        lam = jnp.where(lo_half, mag * jnp.cos(ang), mag * jnp.sin(ang))
        lb_re = mag * jnp.cos(ang)
        lb_im = mag * jnp.sin(ang)
        den = a_re * a_re + a_im * a_im
        f_re = ((lb_re - 1.0) * a_re + lb_im * a_im) / den
        f_im = (lb_im * a_re - (lb_re - 1.0) * a_im) / den
        return lam, f_re, f_im

    def power_table(lam):
        e = lax.broadcasted_iota(jnp.int32, (LANES, 1), 0)
        tab = jnp.where(lo_half, 1.0, 0.0) * jnp.ones((LANES, 1), F32)
        sq = lam
        for k in range(7):
            p = jnp.where(lo_half, sq, pltpu.roll(sq, half, 1))
            q = jnp.where(lo_half, -pltpu.roll(sq, half, 1), sq)
            tab = jnp.where(((e >> k) & 1) == 1, _cmul_packed(tab, p, q), tab)
            sq = _cmul_packed(sq, p, q)
        return tab

    def tile_rows(x16):
        return jnp.broadcast_to(x16[None], (CHUNK, SSM_GROUP, LANES)).reshape(CHUNK_W, LANES)

    def expanded(tab, which, v_re, v_im, conj_sign):
        lexp = _dot(exp_ref[which], tab.astype(BF16))
        if conj_sign > 0:
            p = jnp.where(lo_half, v_re, v_re)
            q = jnp.where(lo_half, -v_im, v_im)
        else:
            p = jnp.where(lo_half, v_re, -v_re)
            q = jnp.where(lo_half, -v_im, -v_im)
        return lexp * tile_rows(p) + pltpu.roll(lexp, half, 1) * tile_rows(q)

    lam_f, ff_re, ff_im = direction(0)
    lam_b, fb_re, fb_im = direction(1)
    tab_f = power_table(lam_f)
    tab_b = power_table(lam_b)

    def bbar(bt_re, bt_im, f_re, f_im):
        return bt_re * f_re - bt_im * f_im, bt_re * f_im + bt_im * f_re

    def mat(di, k):
        return mat_ref[di, k * SSM_GROUP:(k + 1) * SSM_GROUP, :]

    bf_re, bf_im = bbar(mat(0, 0), mat(0, 1), ff_re, ff_im)
    bb_re, bb_im = bbar(mat(1, 0), mat(1, 1), fb_re, fb_im)

    pb1 = expanded(tab_f, 0, bf_re, bf_im, 1)
    pb2 = expanded(tab_b, 1, bb_re, bb_im, 1)
    pb3 = expanded(tab_b, 2, bb_re, bb_im, 1)
    ft_f = expanded(tab_f, 2, mat(0, 3), mat(0, 4), -1)
    ft_b = expanded(tab_b, 3, mat(1, 3), mat(1, 4), -1)

    row_w = lax.broadcasted_iota(jnp.int32, (CHUNK_W, 1), 0)
    last_blk = row_w >= CHUNK_W - SSM_GROUP
    pb2_lag0 = jnp.where(last_blk, pltpu.roll(pb2, CHUNK_W - SSM_GROUP, 0), 0.0)
    ccr_f = mat(0, 2).astype(BF16)
    ccr_b = mat(1, 2).astype(BF16)
    r_lo = _dot_nt(ccr_f, pb1.astype(BF16)) + _dot_nt(ccr_b, pb2_lag0.astype(BF16))
    co = lax.broadcasted_iota(jnp.int32, (SSM_GROUP, CHUNK_W), 0)
    col = lax.broadcasted_iota(jnp.int32, (SSM_GROUP, CHUNK_W), 1)
    r_lo = r_lo + jnp.where(col == CHUNK_W - SSM_GROUP + co, mat(0, 5)[:, 0:1], 0.0)
    r_hi = _dot_nt(ccr_b, pb3.astype(BF16))
    r_t = jnp.concatenate([r_lo, r_hi], axis=1)
    g_t = jnp.concatenate(
        [pltpu.roll(r_t, SSM_GROUP * (tl + 1), 1) for tl in range(T_SUB)], axis=0
    ).astype(BF16)

    u = jnp.concatenate([u_ref[ct].reshape(CHUNK_W, C_TILE) for ct in range(n_ct)],
                        axis=1)
    e_mat = jnp.concatenate([pb1, pb2], axis=1).astype(BF16)
    xend = _dot_tn(e_mat, u)
    lanec = lax.broadcasted_iota(jnp.int32, (1, nch), 1) % seq_chunks
    ns = SSM_STATE

    def scan(re, im, tab, forward):
        lam_col = jnp.transpose(tab[CHUNK:CHUNK + 8, :])[:, 0:1]
        a, b = lam_col[:ns], lam_col[ns:]
        n_steps = int(math.log2(seq_chunks))

        def shifted(v, d):
            if forward:
                return jnp.where(lanec >= d, pltpu.roll(v, d, 1), 0.0)
            return jnp.where(lanec < seq_chunks - d, pltpu.roll(v, nch - d, 1), 0.0)

        for k in range(n_steps):
            sr, si = shifted(re, 1 << k), shifted(im, 1 << k)
            re, im = re + (sr * a - si * b), im + (sr * b + si * a)
            a, b = a * a - b * b, 2.0 * a * b
        return shifted(re, 1), shifted(im, 1)

    f_re, f_im = scan(xend[:ns], xend[ns:2 * ns], tab_f, True)
    b_re, b_im = scan(xend[2 * ns:3 * ns], xend[3 * ns:], tab_b, False)
    xin = jnp.concatenate([f_re, f_im, b_re, b_im], axis=0).astype(BF16)
    f_t = jnp.concatenate([ft_f, ft_b], axis=1).astype(BF16)

    toeplitz = jnp.concatenate(
        [g_t[:, CHUNK_W - LANES * th:2 * CHUNK_W - LANES * th] for th in range(CHUNK // T_SUB)],
        axis=0)
    y_t = _dot(toeplitz, u) + _dot(f_t, xin)
    for ct in range(n_ct):
        y_ref[ct] = y_t[:, ct * C_TILE:(ct + 1) * C_TILE].reshape(
            CHUNK, SSM_GROUP, C_TILE).astype(y_ref.dtype)


def _ssm(ut, a_re, a_im, log_dt, b_re, b_im, c_re, c_im, d, seq_chunks, w_gate, w_up, w_down, layer):
    g, n_ct = ut.shape[:2]
    assert N_EXPERTS % g == 0
    epg = N_EXPERTS // g
    base = layer * g
    n = SSM_STATE

    def per_group(a):
        return jnp.swapaxes(a, 0, 1)

    def dup(a):
        return jnp.concatenate([a, a], axis=-1)

    vecs = dup(jnp.stack([per_group(a_re), per_group(a_im),
                          jnp.broadcast_to(per_group(log_dt)[..., None], (g, 2, n))], axis=2))
    cr, ci = per_group(c_re), per_group(c_im)
    d_blk = jnp.broadcast_to(d.reshape(g, 1, SSM_GROUP, 1), (g, 2, SSM_GROUP, LANES))
    mats = jnp.concatenate(
        [dup(jnp.swapaxes(per_group(b_re), 2, 3)), dup(jnp.swapaxes(per_group(b_im), 2, 3)),
         jnp.concatenate([cr, -ci], axis=-1), dup(cr), dup(ci), d_blk], axis=2)
    exp_c = jnp.asarray(_expand_consts(), BF16)

    return pl.pallas_call(
        functools.partial(_ssm_kernel, seq_chunks=seq_chunks),
        grid=(g,),
        in_specs=[
            pl.BlockSpec((None, n_ct, CHUNK, SSM_GROUP, C_TILE), lambda i: (i, 0, 0, 0, 0)),
            pl.BlockSpec((None, 2, 3, LANES), lambda i: (i, 0, 0, 0)),
            pl.BlockSpec((None, 2, 6 * SSM_GROUP, LANES), lambda i: (i, 0, 0, 0)),
            pl.BlockSpec((4, CHUNK_W, LANES), lambda i: (0, 0, 0)),
            pl.BlockSpec((epg, D_MODEL, D_EXPERT), lambda i: (base + i, 0, 0)),
            pl.BlockSpec((epg, D_MODEL, D_EXPERT), lambda i: (base + i, 0, 0)),
            pl.BlockSpec((epg, D_EXPERT, D_MODEL), lambda i: (base + i, 0, 0)),
        ],
        out_specs=[
            pl.BlockSpec((None, n_ct, CHUNK, SSM_GROUP, C_TILE), lambda i: (i, 0, 0, 0, 0)),
            pl.BlockSpec((epg, D_MODEL, D_EXPERT), lambda i: (i, 0, 0)),
            pl.BlockSpec((epg, D_MODEL, D_EXPERT), lambda i: (i, 0, 0)),
            pl.BlockSpec((epg, D_EXPERT, D_MODEL), lambda i: (i, 0, 0)),
        ],
        out_shape=[
            jax.ShapeDtypeStruct(ut.shape, BF16),
            jax.ShapeDtypeStruct((N_EXPERTS, D_MODEL, D_EXPERT), BF16),
            jax.ShapeDtypeStruct((N_EXPERTS, D_MODEL, D_EXPERT), BF16),
            jax.ShapeDtypeStruct((N_EXPERTS, D_EXPERT, D_MODEL), BF16),
        ],
        compiler_params=_params(("parallel",)),
        name="ssm",
    )(ut, vecs, mats, exp_c, w_gate, w_up, w_down)


def _mix_out_kernel(x_hbm, a_hbm, yt_ref, gwt_ref, gb_ref, wo_ref, h_hbm, xbuf, abuf, hbuf,
                    xsem, asem, hsem):
    nc = C_TILE
    x = _load_time_major_rows(x_hbm, xbuf, xsem)
    a = _load_time_major_rows(a_hbm, abuf, asem).astype(BF16)
    y = jnp.concatenate([yt_ref[:, j, :, :].reshape(D_SSM, nc) for j in range(S_TILE)],
                        axis=1).astype(F32)
    z = 0.5 * y * (1.0 + jnp.tanh(math.sqrt(2.0 / math.pi) * (y + 0.044715 * (y * y * y))))
    gate = _sigmoid(_dot(gwt_ref[...], z.astype(BF16)) + gb_ref[...])
    s = (z * gate).astype(BF16)
    h = x + _dot(a, wo_ref[:D_POOL, :]) + _dot_tn(s, wo_ref[D_POOL:, :])
    _store_time_major_rows(h, h_hbm, hbuf, hsem)


def _mix_out(x3, a3, yt, glu_w_t, glu_b_col, w_out):
    nch = x3.shape[0]
    return pl.pallas_call(
        _mix_out_kernel,
        grid=(nch // C_TILE, CHUNK // S_TILE),
        in_specs=[
            pl.BlockSpec(memory_space=pl.ANY),
            pl.BlockSpec(memory_space=pl.ANY),
            pl.BlockSpec((N_SSM_GROUPS, None, S_TILE, SSM_GROUP, C_TILE), lambda c, t: (0, c, t, 0, 0)),
            pl.BlockSpec((D_SSM, D_SSM), lambda c, t: (0, 0)),
            pl.BlockSpec((D_SSM, 1), lambda c, t: (0, 0)),
            pl.BlockSpec((D_MODEL, D_MODEL), lambda c, t: (0, 0)),
        ],
        out_specs=pl.BlockSpec(memory_space=pl.ANY),
        out_shape=jax.ShapeDtypeStruct((nch, CHUNK, D_MODEL), F32),
        scratch_shapes=[
            pltpu.VMEM((2, S_TILE, C_TILE, D_MODEL), F32),
            pltpu.VMEM((2, S_TILE, C_TILE, D_POOL), F32),
            pltpu.VMEM((2, S_TILE, C_TILE, D_MODEL), F32),
            pltpu.SemaphoreType.DMA((2,)),
            pltpu.SemaphoreType.DMA((2,)),
            pltpu.SemaphoreType.DMA((2,)),
        ],
        compiler_params=_params(("arbitrary", "arbitrary")),
        name="mix_out",
    )(x3, a3, yt, glu_w_t, glu_b_col, w_out)


def _pack_rows(x):
    b = lax.bitcast_convert_type(x.astype(BF16).astype(F32), U32)
    return (b[:, :HALF] & jnp.uint32(0xFFFF0000)) | (b[:, HALF:] >> 16)


def _unpack_rows(w):
    lo = lax.bitcast_convert_type(w & jnp.uint32(0xFFFF0000), F32)
    hi = lax.bitcast_convert_type(w << 16, F32)
    return lo, hi


def _split_bf16(x):
    hi = x.astype(BF16)
    return hi, (x - hi.astype(F32)).astype(BF16)


def _route(v32, wr_ref, br_ref):
    v_hi, v_lo = _split_bf16(v32)
    w_hi, w_lo = _split_bf16(wr_ref[...])
    logits = _dot(v_hi, w_hi) + (_dot(v_lo, w_hi) + _dot(v_hi, w_lo)) + br_ref[...]
    return jnp.transpose(logits)


def _top1(x, valid=None):
    n = x.shape[0]
    row = lax.broadcasted_iota(jnp.int32, x.shape, 0).astype(F32)
    if valid is not None:
        x = jnp.where(valid, x, -jnp.inf)
    m = jnp.max(x, axis=0, keepdims=True)
    idx = jnp.min(jnp.where(x == m, row, float(n)), axis=0, keepdims=True)
    return m, idx, x, row


def _split_planes(packed, ref):
    ref[0] = packed[:, :SC_ROW]
    ref[1] = packed[:, SC_ROW:]


def _router_kernel(h_ref, g_ref, wr_ref, br_ref, before_ref, vp_ref, meta_t_ref, cnt_ref, carry_ref):
    @pl.when(pl.program_id(0) == 0)
    def _():
        carry_ref[...] = jnp.zeros_like(carry_ref)

    v32 = _rms(h_ref[...], g_ref[...])
    _split_planes(_pack_rows(v32), vp_ref)
    lt = _route(v32, wr_ref, br_ref)
    tm = lt.shape[1]
    eg = EXPERTS_PER_GROUP

    grp = lt[:eg]
    grp_row = lax.broadcasted_iota(jnp.int32, grp.shape, 0)
    mg, grp_idx, grp, _ = _top1(grp, grp_row < N_EXPERT_GROUPS)
    grp_p = 1.0 / jnp.sum(jnp.exp(grp - mg), axis=0, keepdims=True)
    le = jnp.zeros((eg, tm), F32)
    for g in range(N_EXPERT_GROUPS):
        le = jnp.where(grp_idx == float(g), lt[eg * (g + 1):eg * (g + 2)], le)
    m1, i1, le, row = _top1(le)
    z = jnp.sum(jnp.exp(le - m1), axis=0, keepdims=True)
    m2, i2, _, _ = _top1(jnp.where(row == i1, -jnp.inf, le))
    p1 = 1.0 / z
    p2 = jnp.exp(m2 - m1) / z
    tot = p1 + p2
    w1 = grp_p * (p1 / tot)
    w2 = grp_p * (p2 / tot)
    e1 = grp_idx * eg + i1
    e2 = grp_idx * eg + i2

    erow = lax.broadcasted_iota(jnp.int32, (N_EXPERTS, tm), 0).astype(F32)
    onehot = jnp.where(erow == e1, 1.0, jnp.where(erow == e2, 1.0, 0.0))
    before = _dot(onehot.astype(BF16), before_ref[...]) + carry_ref[...]
    rank1 = jnp.sum(jnp.where(erow == e1, before, 0.0), axis=0, keepdims=True)
    rank2 = jnp.sum(jnp.where(erow == e2, before, 0.0), axis=0, keepdims=True)
    carry = carry_ref[...] + jnp.sum(onehot, axis=1, keepdims=True)
    carry_ref[...] = carry
    cnt_ref[...] = carry

    mrow = lax.broadcasted_iota(jnp.int32, (META_ROWS, tm), 0)
    meta_t_ref[...] = jnp.where(mrow == 0, e1, jnp.where(mrow == 1, e2, jnp.where(
        mrow == 2, rank1, jnp.where(mrow == 3, rank2, jnp.where(
            mrow == 4, w1, jnp.where(mrow == 5, w2, 0.0))))))


def _earlier_matrix(tm):
    return np.triu(np.ones((tm, tm), np.float32), k=1)


def _router(h1, g_ffn, w_router, b_router, tm):
    t = h1.shape[0]
    return pl.pallas_call(
        _router_kernel,
        grid=(t // tm,),
        in_specs=[
            pl.BlockSpec((tm, D_MODEL), lambda i: (i, 0)),
            pl.BlockSpec((1, D_MODEL), lambda i: (0, 0)),
            pl.BlockSpec((D_MODEL, ROUTER_W), lambda i: (0, 0)),
            pl.BlockSpec((1, ROUTER_W), lambda i: (0, 0)),
            pl.BlockSpec((tm, tm), lambda i: (0, 0)),
        ],
        out_specs=[
            pl.BlockSpec((2, tm, SC_ROW), lambda i: (0, i, 0)),
            pl.BlockSpec((META_ROWS, tm), lambda i: (0, i)),
            pl.BlockSpec((N_EXPERTS, 1), lambda i: (0, 0)),
        ],
        out_shape=[
            jax.ShapeDtypeStruct((2, t, SC_ROW), U32),
            jax.ShapeDtypeStruct((META_ROWS, t), F32),
            jax.ShapeDtypeStruct((N_EXPERTS, 1), F32),
        ],
        scratch_shapes=[pltpu.VMEM((N_EXPERTS, 1), F32)],
        compiler_params=_params(("arbitrary",)),
        name="router",
    )(h1, g_ffn, w_router, b_router, jnp.asarray(_earlier_matrix(tm), BF16))


def _plan(meta_t, counts, n_tiles):
    e1 = meta_t[0].astype(jnp.int32)
    e2 = meta_t[1].astype(jnp.int32)
    rank1 = meta_t[2].astype(jnp.int32)
    rank2 = meta_t[3].astype(jnp.int32)
    cnt = counts[:, 0].astype(jnp.int32)
    padded = ((cnt + ROW_TILE - 1) // ROW_TILE) * ROW_TILE
    ends = jnp.cumsum(padded)
    starts = ends - padded
    experts = jnp.arange(N_EXPERTS, dtype=jnp.int32)
    pos1 = rank1 + jnp.sum(jnp.where(e1[None, :] == experts[:, None], starts[:, None], 0), axis=0)
    pos2 = rank2 + jnp.sum(jnp.where(e2[None, :] == experts[:, None], starts[:, None], 0), axis=0)
    tile_start = jnp.arange(n_tiles, dtype=jnp.int32) * ROW_TILE
    tile_expert = jnp.sum((tile_start[:, None] >= ends[None, :]).astype(jnp.int32), axis=1)
    tile_expert = jnp.minimum(tile_expert, N_EXPERTS - 1)
    rows_left = jnp.sum(jnp.where(tile_expert[:, None] == experts, cnt + starts, 0), axis=1) - tile_start
    n_valid = jnp.clip(rows_left, 0, ROW_TILE).astype(jnp.int32)
    last_used = jnp.maximum(ends[-1] // ROW_TILE - 1, 0)
    block = jnp.minimum(jnp.arange(n_tiles, dtype=jnp.int32), last_used)
    tile_expert = jnp.sum(jnp.where(block[:, None] == jnp.arange(n_tiles)[None, :],
                                    tile_expert[None, :], 0), axis=1)
    plane = n_tiles * ROW_TILE
    half_rows = jnp.concatenate([pos1, pos1 + plane, pos2, pos2 + plane])[None]
    return half_rows, tile_expert, n_valid, block


def _sc_mesh():
    return plsc.VectorSubcoreMesh(core_axis_name="c", subcore_axis_name="s")


def _sc_scatter_rows(rows, idx, n_out):
    t, width = rows.shape
    steps = t // SC_WINDOW

    @pl.kernel(out_type=jax.ShapeDtypeStruct((n_out, width), rows.dtype), mesh=_sc_mesh(),
               scratch_types=[], name="moe_scatter")
    def scatter(rows_hbm, idx_hbm, out_hbm):
        def body(rows_vmem, idx_vmem):
            pltpu.sync_copy(rows_vmem, out_hbm.at[idx_vmem.at[0]])

        pltpu.emit_pipeline(
            body,
            grid=(2, steps),
            in_specs=[pl.BlockSpec((SC_WINDOW, width), lambda k, j: (j, 0)),
                      pl.BlockSpec((1, SC_WINDOW), lambda k, j: (0, k * steps + j))],
            out_specs=[],
            core_axis_name=("c", "s"),
            dimension_semantics=(pltpu.PARALLEL, pltpu.PARALLEL),
        )(rows_hbm, idx_hbm)

    return scatter(rows, idx)


def _sc_gather_rows(table, idx):
    m = idx.shape[1]
    width = table.shape[1]
    steps = m // (2 * SC_WINDOW)

    @pl.kernel(out_type=jax.ShapeDtypeStruct((m, width), table.dtype), mesh=_sc_mesh(),
               scratch_types=[], name="moe_gather")
    def gather(table_hbm, idx_hbm, out_hbm):
        def body(idx_vmem, out_vmem):
            pltpu.sync_copy(table_hbm.at[idx_vmem.at[0]], out_vmem)

        pltpu.emit_pipeline(
            body,
            grid=(2, steps),
            in_specs=[pl.BlockSpec((1, SC_WINDOW), lambda k, j: (0, k * steps + j))],
            out_specs=[pl.BlockSpec((SC_WINDOW, width), lambda k, j: (k * steps + j, 0))],
            core_axis_name=("c", "s"),
            dimension_semantics=(pltpu.PARALLEL, pltpu.PARALLEL),
        )(idx_hbm, out_hbm)

    return gather(table, idx)


def _experts_kernel(te_ref, nv_ref, blk_ref, xs_ref, wg_ref, wu_ref, wd_ref, ys_ref):
    r = pl.program_id(0)
    n_valid = nv_ref[r]

    @pl.when(n_valid > 0)
    def _():
        parts = [p.astype(BF16) for p in _unpack_rows(xs_ref[0]) + _unpack_rows(xs_ref[1])]
        cols = (0, 2 * SC_ROW, SC_ROW, 3 * SC_ROW)
        hg = sum(_dot(p, wg_ref[c:c + SC_ROW, :]) for p, c in zip(parts, cols))
        hu = sum(_dot(p, wu_ref[c:c + SC_ROW, :]) for p, c in zip(parts, cols))
        row = lax.broadcasted_iota(jnp.int32, (ROW_TILE, 1), 0)
        hid = jnp.where(row < n_valid, hg * _sigmoid(hg) * hu, 0.0).astype(BF16)
        _split_planes(_pack_rows(_dot(hid, wd_ref[...])), ys_ref)


def _experts(xs, tile_expert, n_valid, block, w_gate, w_up, w_down):
    n_tiles = xs.shape[1] // ROW_TILE
    w_spec = pl.BlockSpec((None, D_MODEL, D_EXPERT), lambda r, te, nv, blk: (te[r], 0, 0))
    grid_spec = pltpu.PrefetchScalarGridSpec(
        num_scalar_prefetch=3,
        grid=(n_tiles,),
        in_specs=[
            pl.BlockSpec((2, ROW_TILE, SC_ROW), lambda r, te, nv, blk: (0, blk[r], 0)),
            w_spec, w_spec,
            pl.BlockSpec((None, D_EXPERT, D_MODEL), lambda r, te, nv, blk: (te[r], 0, 0)),
        ],
        out_specs=pl.BlockSpec((2, ROW_TILE, SC_ROW), lambda r, te, nv, blk: (0, blk[r], 0)),
    )
    return pl.pallas_call(
        _experts_kernel,
        grid_spec=grid_spec,
        out_shape=jax.ShapeDtypeStruct((2, n_tiles * ROW_TILE, SC_ROW), U32),
        compiler_params=_params(("arbitrary",)),
        name="experts",
    )(tile_expert, n_valid, block, xs, w_gate, w_up, w_down)


def _ple_kernel(h_ref, yg_ref, meta_t_ref, p_ref, wg_ref, bg_ref, wp_ref, gf_ref, *rest, final_norm):
    o_ref = rest[-1]
    meta = jnp.transpose(meta_t_ref[...])
    w1 = meta[:, 4:5]
    w2 = meta[:, 5:6]
    q0, q2 = (w1 * u + w2 * v for u, v in zip(_unpack_rows(yg_ref[0]), _unpack_rows(yg_ref[2])))
    q1, q3 = (w1 * u + w2 * v for u, v in zip(_unpack_rows(yg_ref[1]), _unpack_rows(yg_ref[3])))
    moe = jnp.concatenate([q0, q1, q2, q3], axis=1)
    h = h_ref[...] + moe
    gate = _sigmoid(_dot(_rms(h).astype(BF16), wg_ref[...]) + bg_ref[...])
    h = h + gate * _dot(p_ref[...].astype(BF16), wp_ref[...])
    o_ref[...] = _rms(h, gf_ref[...]) if final_norm else h


def _ple(h1, yg, meta_t, p2, w_gate, b_gate, w_proj, g_final, final_norm, tm, part, prev_out):
    t = h1.shape[0]
    steps = t // tm // GATHER_PARTS
    off = part * steps
    in_specs = [
        pl.BlockSpec((tm, D_MODEL), lambda i: (i + off, 0)),
        pl.BlockSpec((4, tm, SC_ROW), lambda i: (0, i, 0)),
        pl.BlockSpec((META_ROWS, tm), lambda i: (0, i + off)),
        pl.BlockSpec((tm, D_PLE), lambda i: (i + off, 0)),
        pl.BlockSpec((D_MODEL, D_MODEL), lambda i: (0, 0)),
        pl.BlockSpec((1, D_MODEL), lambda i: (0, 0)),
        pl.BlockSpec((D_PLE, D_MODEL), lambda i: (0, 0)),
        pl.BlockSpec((1, D_MODEL), lambda i: (0, 0)),
    ]
    args = [h1, yg, meta_t, p2, w_gate, b_gate, w_proj, g_final]
    aliases = {}
    if prev_out is not None:
        in_specs.append(pl.BlockSpec(memory_space=pl.ANY))
        args.append(prev_out)
        aliases = {len(args) - 1: 0}
    return pl.pallas_call(
        functools.partial(_ple_kernel, final_norm=final_norm),
        grid=(steps,),
        in_specs=in_specs,
        out_specs=pl.BlockSpec((tm, D_MODEL), lambda i: (i + off, 0)),
        out_shape=jax.ShapeDtypeStruct((t, D_MODEL), F32),
        input_output_aliases=aliases,
        compiler_params=_params(("parallel",)),
        name="ple",
    )(*args)


def kernel(x, p, g_mix, w_in, pool_w, pool_scale, ssm_a_re, ssm_a_im, ssm_log_dt, ssm_b_re,
           ssm_b_im, ssm_c_re, ssm_c_im, ssm_d, glu_w, glu_b, w_out, g_ffn, router_grp_w,
           router_grp_b, router_exp_w, router_exp_b, exp_w_gate, exp_w_up, exp_w_down, g_ple,
           ple_gate_w, ple_gate_b, ple_proj_w, g_final):
    bsz, seq, dm = x.shape
    depth = g_mix.shape[0]
    t = bsz * seq
    seq_chunks = seq // CHUNK
    nch = t // CHUNK
    tm = 1024
    n_sorted = 2 * t + N_EXPERTS * ROW_TILE
    w_gate_all = exp_w_gate.reshape(depth * N_EXPERTS, dm, D_EXPERT)
    w_up_all = exp_w_up.reshape(depth * N_EXPERTS, dm, D_EXPERT)
    w_down_all = exp_w_down.reshape(depth * N_EXPERTS, D_EXPERT, dm)

    h = x.reshape(t, dm)
    for i in range(depth):
        w_in_b = (g_mix[i][:, None] * w_in[i]).astype(BF16)
        zp, ut = _in_proj(h.reshape(nch, CHUNK, dm), w_in_b[:, :D_POOL],
                          jnp.transpose(w_in_b[:, D_POOL:]))
        a = _pool(zp.reshape(bsz, seq, D_POOL), pool_w[i], pool_scale[i][None])
        yt, w_gate_b, w_up_b, w_down_b = _ssm(
            ut, ssm_a_re[i], ssm_a_im[i], ssm_log_dt[i], ssm_b_re[i], ssm_b_im[i], ssm_c_re[i],
            ssm_c_im[i], ssm_d[i], seq_chunks, w_gate_all, w_up_all, w_down_all, i)
        h = _mix_out(h.reshape(nch, CHUNK, dm), a.reshape(nch, CHUNK, D_POOL), yt,
                     jnp.transpose(glu_w[i]).astype(BF16), glu_b[i][:, None],
                     w_out[i].astype(BF16)).reshape(t, dm)

        eg = EXPERTS_PER_GROUP
        w_router = jnp.concatenate(
            [router_grp_w[i], jnp.zeros((dm, eg - N_EXPERT_GROUPS), F32),
             jnp.transpose(router_exp_w[i], (1, 0, 2)).reshape(dm, N_EXPERTS),
             jnp.zeros((dm, ROUTER_W - eg - N_EXPERTS), F32)], axis=1)
        b_router = jnp.concatenate(
            [router_grp_b[i], jnp.zeros((eg - N_EXPERT_GROUPS,), F32),
             router_exp_b[i].reshape(N_EXPERTS),
             jnp.zeros((ROUTER_W - eg - N_EXPERTS,), F32)])[None]
        vp, meta_t, counts = _router(h, g_ffn[i][None], w_router, b_router, tm)
        idx, tile_expert, n_valid, block = _plan(meta_t, counts, n_sorted // ROW_TILE)
        xs = _sc_scatter_rows(vp.reshape(2 * t, SC_ROW), idx, 2 * n_sorted)
        ys = _experts(xs.reshape(2, n_sorted, SC_ROW), tile_expert, n_valid, block,
                      w_gate_b, w_up_b, w_down_b)
        ys2 = ys.reshape(2 * n_sorted, SC_ROW)
        idx4 = idx.reshape(4, t)
        tp = t // GATHER_PARTS
        ple_wg = (g_ple[i][:, None] * ple_gate_w[i]).astype(BF16)
        ple_wp = ple_proj_w[i].astype(BF16)
        out = None
        for q in range(GATHER_PARTS):
            yg_q = _sc_gather_rows(ys2, idx4[:, q * tp:(q + 1) * tp].reshape(1, 4 * tp))
            out = _ple(h, yg_q.reshape(4, tp, SC_ROW), meta_t, p[i].reshape(t, D_PLE), ple_wg,
                       ple_gate_b[i][None], ple_wp, g_final[None], i == depth - 1, tm, q, out)
        h = out
    return h.reshape(bsz, seq, dm)
```

```python
import functools
import math

import numpy as np
import jax
import jax.numpy as jnp
from jax import lax
from jax.experimental import pallas as pl
from jax.experimental.pallas import tpu as pltpu
from jax.experimental.pallas import tpu_sc as plsc

F32 = jnp.float32
BF16 = jnp.bfloat16
U32 = jnp.uint32

D_MODEL = 1024
D_POOL = 512
D_SSM = 512
POOL_WINDOWS = (2, 4, 8, 16)
POOL_GROUP = 128
SSM_GROUP = 16
N_SSM_GROUPS = 32
SSM_STATE = 64
N_EXPERT_GROUPS = 4
EXPERTS_PER_GROUP = 8
N_EXPERTS = N_EXPERT_GROUPS * EXPERTS_PER_GROUP
D_EXPERT = 256
D_PLE = 256
RMS_EPS = 1e-6

LANES = 128
CHUNK = 32
CHUNK_W = CHUNK * SSM_GROUP
T_SUB = 8
S_TILE = 8
C_TILE = 128
ROUTER_W = LANES
HALF = D_MODEL // 2
ROW_TILE = 1024
SC_WINDOW = 128
SC_ROW = HALF // 2
META_ROWS = 8
GATHER_PARTS = 2
VMEM_LIMIT = 56 * 1024 * 1024


def _dot(a, b):
    return jnp.dot(a, b, preferred_element_type=F32)


def _dot_nt(a, b):
    return lax.dot_general(a, b, (((1,), (1,)), ((), ())), preferred_element_type=F32)


def _dot_tn(a, b):
    return lax.dot_general(a, b, (((0,), (0,)), ((), ())), preferred_element_type=F32)


def _rms(x, g=None):
    y = x * lax.rsqrt(jnp.mean(x * x, axis=-1, keepdims=True) + RMS_EPS)
    return y if g is None else y * g


def _sigmoid(x):
    return 1.0 / (1.0 + jnp.exp(-x))


def _params(sem):
    return pltpu.CompilerParams(dimension_semantics=sem, vmem_limit_bytes=VMEM_LIMIT)


def _tile_step():
    n_s = pl.num_programs(1)
    return pl.program_id(0) * n_s + pl.program_id(1), pl.num_programs(0) * n_s


def _row_copies(hbm, buf, sem, step_idx, slot_idx, to_hbm):
    n_s = pl.num_programs(1)
    c0 = (step_idx // n_s) * C_TILE
    s0 = (step_idx % n_s) * S_TILE
    out = []
    for j in range(S_TILE):
        far, near = hbm.at[pl.ds(c0, C_TILE), s0 + j, :], buf.at[slot_idx, j]
        src, dst = (near, far) if to_hbm else (far, near)
        out.append(pltpu.make_async_copy(src, dst, sem.at[slot_idx]))
    return out


def _load_time_major_rows(hbm, buf, sem):
    step, n_steps = _tile_step()
    slot = step % 2

    @pl.when(step == 0)
    def _():
        for cp in _row_copies(hbm, buf, sem, step, slot, False):
            cp.start()

    @pl.when(step + 1 < n_steps)
    def _():
        for cp in _row_copies(hbm, buf, sem, step + 1, 1 - slot, False):
            cp.start()

    for cp in _row_copies(hbm, buf, sem, step, slot, False):
        cp.wait()
    return buf[slot].reshape(S_TILE * C_TILE, buf.shape[-1])


def _store_time_major_rows(val, hbm, buf, sem):
    step, n_steps = _tile_step()
    slot = step % 2

    @pl.when(step >= 2)
    def _():
        for cp in _row_copies(hbm, buf, sem, step - 2, slot, True):
            cp.wait()

    buf[slot] = val.reshape(S_TILE, C_TILE, val.shape[-1])
    for cp in _row_copies(hbm, buf, sem, step, slot, True):
        cp.start()

    @pl.when(step == n_steps - 1)
    def _():
        @pl.when(step >= 1)
        def _():
            for cp in _row_copies(hbm, buf, sem, step - 1, 1 - slot, True):
                cp.wait()
        for cp in _row_copies(hbm, buf, sem, step, slot, True):
            cp.wait()


def _in_proj_kernel(x_hbm, wp_ref, wst_ref, zp_hbm, ut_ref, xbuf, zbuf, xsem, zsem):
    nc = C_TILE
    u = _rms(_load_time_major_rows(x_hbm, xbuf, xsem)).astype(BF16)
    zt = _dot_nt(wst_ref[...], u).astype(BF16)
    for j in range(S_TILE):
        ut_ref[:, j, :, :] = zt[:, j * nc:(j + 1) * nc].reshape(N_SSM_GROUPS, SSM_GROUP, nc)
    _store_time_major_rows(_dot(u, wp_ref[...]), zp_hbm, zbuf, zsem)


def _in_proj(x3, w_pool, w_ssm_t):
    nch = x3.shape[0]
    return pl.pallas_call(
        _in_proj_kernel,
        grid=(nch // C_TILE, CHUNK // S_TILE),
        in_specs=[
            pl.BlockSpec(memory_space=pl.ANY),
            pl.BlockSpec((D_MODEL, D_POOL), lambda c, s: (0, 0)),
            pl.BlockSpec((D_SSM, D_MODEL), lambda c, s: (0, 0)),
        ],
        out_specs=[
            pl.BlockSpec(memory_space=pl.ANY),
            pl.BlockSpec((N_SSM_GROUPS, None, S_TILE, SSM_GROUP, C_TILE), lambda c, s: (0, c, s, 0, 0)),
        ],
        out_shape=[
            jax.ShapeDtypeStruct((nch, CHUNK, D_POOL), F32),
            jax.ShapeDtypeStruct((N_SSM_GROUPS, nch // C_TILE, CHUNK, SSM_GROUP, C_TILE), BF16),
        ],
        scratch_shapes=[
            pltpu.VMEM((2, S_TILE, C_TILE, D_MODEL), F32),
            pltpu.VMEM((2, S_TILE, C_TILE, D_POOL), F32),
            pltpu.SemaphoreType.DMA((2,)),
            pltpu.SemaphoreType.DMA((2,)),
        ],
        compiler_params=_params(("arbitrary", "arbitrary")),
        name="in_proj",
    )(x3, w_pool, w_ssm_t)


def _shift_rows(x, d, row, n):
    if d == 0:
        return x
    r = pltpu.roll(x, d % n, 0)
    if d > 0:
        return jnp.where(row >= d, r, 0.0)
    return jnp.where(row < n + d, r, 0.0)


def _pool_kernel(z_ref, w_ref, sc_ref, o_ref):
    n = z_ref.shape[0]
    gi = pl.program_id(1)
    row = lax.broadcasted_iota(jnp.int32, (n, 1), 0)

    for k, w in enumerate(POOL_WINDOWS):
        @pl.when(gi == k)
        def _(w=w):
            x = z_ref[...]
            half = w // 2
            pd, pu, span = x, x, 1
            while span < half:
                pd = pd + _shift_rows(pd, span, row, n)
                pu = pu + _shift_rows(pu, -span, row, n)
                span *= 2
            total = _shift_rows(pd, 1, row, n) + pu
            lo = jnp.maximum(row - half, 0)
            hi = jnp.minimum(row + half, n)
            cnt = (hi - lo).astype(F32)
            diff = (total / cnt - x).astype(BF16)
            o_ref[...] = _dot(diff, w_ref[...].astype(BF16)) * sc_ref[...]


def _pool(zp3, pool_w, pool_scale):
    b, s, _ = zp3.shape
    return pl.pallas_call(
        _pool_kernel,
        grid=(b, len(POOL_WINDOWS)),
        in_specs=[
            pl.BlockSpec((None, s, POOL_GROUP), lambda i, g: (i, 0, g)),
            pl.BlockSpec((None, POOL_GROUP, POOL_GROUP), lambda i, g: (g, 0, 0)),
            pl.BlockSpec((1, POOL_GROUP), lambda i, g: (0, g)),
        ],
        out_specs=pl.BlockSpec((None, s, POOL_GROUP), lambda i, g: (i, 0, g)),
        out_shape=jax.ShapeDtypeStruct((b, s, D_POOL), F32),
        compiler_params=_params(("parallel", "parallel")),
        name="pool",
    )(zp3, pool_w, pool_scale)


def _expand_consts():
    time = np.arange(CHUNK_W) // SSM_GROUP
    def onehot(e):
        m = np.zeros((CHUNK_W, LANES), np.float32)
        m[np.arange(CHUNK_W), e] = 1.0
        return m
    return np.stack([
        onehot(CHUNK - 1 - time),
        onehot(time),
        onehot(time + 1),
        onehot(CHUNK - time),
    ])


def _cmul_packed(x, p, q):
    return x * p + pltpu.roll(x, LANES // 2, 1) * q


def _ssm_kernel(u_ref, vec_ref, mat_ref, exp_ref, wg_ref, wu_ref, wd_ref, y_ref, og_ref, ou_ref,
                od_ref, *, seq_chunks):
    og_ref[...] = wg_ref[...].astype(BF16)
    ou_ref[...] = wu_ref[...].astype(BF16)
    od_ref[...] = wd_ref[...].astype(BF16)

    n_ct = u_ref.shape[0]
    nch = n_ct * C_TILE
    half = LANES // 2
    lane = lax.broadcasted_iota(jnp.int32, (1, LANES), 1)
    lo_half = lane < half

    def direction(di):
        a_re = vec_ref[di, 0:1]
        a_im = vec_ref[di, 1:2]
        dt = jnp.exp(vec_ref[di, 2:3])
        mag = jnp.exp(a_re * dt)
        ang = a_im * dt
        lam = jnp.where(lo_half, mag * jnp.cos(ang), mag * jnp.sin(ang))
        lb_re = mag * jnp.cos(ang)
        lb_im = mag * jnp.sin(ang)
        den = a_re * a_re + a_im * a_im
        f_re = ((lb_re - 1.0) * a_re + lb_im * a_im) / den
        f_im = (lb_im * a_re - (lb_re - 1.0) * a_im) / den
        return lam, f_re, f_im

    def power_table(lam):
        e = lax.broadcasted_iota(jnp.int32, (LANES, 1), 0)
        tab = jnp.where(lo_half, 1.0, 0.0) * jnp.ones((LANES, 1), F32)
        sq = lam
        for k in range(7):
            p = jnp.where(lo_half, sq, pltpu.roll(sq, half, 1))
            q = jnp.where(lo_half, -pltpu.roll(sq, half, 1), sq)
            tab = jnp.where(((e >> k) & 1) == 1, _cmul_packed(tab, p, q), tab)
            sq = _cmul_packed(sq, p, q)
        return tab

    def tile_rows(x16):
        return jnp.broadcast_to(x16[None], (CHUNK, SSM_GROUP, LANES)).reshape(CHUNK_W, LANES)

    def expanded(tab, which, v_re, v_im, conj_sign):
        lexp = _dot(exp_ref[which], tab.astype(BF16))
        if conj_sign > 0:
            p = jnp.where(lo_half, v_re, v_re)
            q = jnp.where(lo_half, -v_im, v_im)
        else:
            p = jnp.where(lo_half, v_re, -v_re)
            q = jnp.where(lo_half, -v_im, -v_im)
        return lexp * tile_rows(p) + pltpu.roll(lexp, half, 1) * tile_rows(q)

    lam_f, ff_re, ff_im = direction(0)
    lam_b, fb_re, fb_im = direction(1)
    tab_f = power_table(lam_f)
    tab_b = power_table(lam_b)

    def bbar(bt_re, bt_im, f_re, f_im):
        return bt_re * f_re - bt_im * f_im, bt_re * f_im + bt_im * f_re

    def mat(di, k):
        return mat_ref[di, k * SSM_GROUP:(k + 1) * SSM_GROUP, :]

    bf_re, bf_im = bbar(mat(0, 0), mat(0, 1), ff_re, ff_im)
    bb_re, bb_im = bbar(mat(1, 0), mat(1, 1), fb_re, fb_im)

    pb1 = expanded(tab_f, 0, bf_re, bf_im, 1)
    pb2 = expanded(tab_b, 1, bb_re, bb_im, 1)
    pb3 = expanded(tab_b, 2, bb_re, bb_im, 1)
    ft_f = expanded(tab_f, 2, mat(0, 3), mat(0, 4), -1)
    ft_b = expanded(tab_b, 3, mat(1, 3), mat(1, 4), -1)

    row_w = lax.broadcasted_iota(jnp.int32, (CHUNK_W, 1), 0)
    last_blk = row_w >= CHUNK_W - SSM_GROUP
    pb2_lag0 = jnp.where(last_blk, pltpu.roll(pb2, CHUNK_W - SSM_GROUP, 0), 0.0)
    ccr_f = mat(0, 2).astype(BF16)
    ccr_b = mat(1, 2).astype(BF16)
    r_lo = _dot_nt(ccr_f, pb1.astype(BF16)) + _dot_nt(ccr_b, pb2_lag0.astype(BF16))
    co = lax.broadcasted_iota(jnp.int32, (SSM_GROUP, CHUNK_W), 0)
    col = lax.broadcasted_iota(jnp.int32, (SSM_GROUP, CHUNK_W), 1)
    r_lo = r_lo + jnp.where(col == CHUNK_W - SSM_GROUP + co, mat(0, 5)[:, 0:1], 0.0)
    r_hi = _dot_nt(ccr_b, pb3.astype(BF16))
    r_t = jnp.concatenate([r_lo, r_hi], axis=1)
    g_t = jnp.concatenate(
        [pltpu.roll(r_t, SSM_GROUP * (tl + 1), 1) for tl in range(T_SUB)], axis=0
    ).astype(BF16)

    u = jnp.concatenate([u_ref[ct].reshape(CHUNK_W, C_TILE) for ct in range(n_ct)],
                        axis=1)
    e_mat = jnp.concatenate([pb1, pb2], axis=1).astype(BF16)
    xend = _dot_tn(e_mat, u)
    lanec = lax.broadcasted_iota(jnp.int32, (1, nch), 1) % seq_chunks
    ns = SSM_STATE

    def scan(re, im, tab, forward):
        lam_col = jnp.transpose(tab[CHUNK:CHUNK + 8, :])[:, 0:1]
        a, b = lam_col[:ns], lam_col[ns:]
        n_steps = int(math.log2(seq_chunks))

        def shifted(v, d):
            if forward:
                return jnp.where(lanec >= d, pltpu.roll(v, d, 1), 0.0)
            return jnp.where(lanec < seq_chunks - d, pltpu.roll(v, nch - d, 1), 0.0)

        for k in range(n_steps):
            sr, si = shifted(re, 1 << k), shifted(im, 1 << k)
            re, im = re + (sr * a - si * b), im + (sr * b + si * a)
            a, b = a * a - b * b, 2.0 * a * b
        return shifted(re, 1), shifted(im, 1)

    f_re, f_im = scan(xend[:ns], xend[ns:2 * ns], tab_f, True)
    b_re, b_im = scan(xend[2 * ns:3 * ns], xend[3 * ns:], tab_b, False)
    xin = jnp.concatenate([f_re, f_im, b_re, b_im], axis=0).astype(BF16)
    f_t = jnp.concatenate([ft_f, ft_b], axis=1).astype(BF16)

    toeplitz = jnp.concatenate(
        [g_t[:, CHUNK_W - LANES * th:2 * CHUNK_W - LANES * th] for th in range(CHUNK // T_SUB)],
        axis=0)
    y_t = _dot(toeplitz, u) + _dot(f_t, xin)
    for ct in range(n_ct):
        y_ref[ct] = y_t[:, ct * C_TILE:(ct + 1) * C_TILE].reshape(
            CHUNK, SSM_GROUP, C_TILE).astype(y_ref.dtype)


def _ssm(ut, a_re, a_im, log_dt, b_re, b_im, c_re, c_im, d, seq_chunks, w_gate, w_up, w_down, layer):
    g, n_ct = ut.shape[:2]
    assert N_EXPERTS % g == 0
    epg = N_EXPERTS // g
    base = layer * g
    n = SSM_STATE

    def per_group(a):
        return jnp.swapaxes(a, 0, 1)

    def dup(a):
        return jnp.concatenate([a, a], axis=-1)

    vecs = dup(jnp.stack([per_group(a_re), per_group(a_im),
                          jnp.broadcast_to(per_group(log_dt)[..., None], (g, 2, n))], axis=2))
    cr, ci = per_group(c_re), per_group(c_im)
    d_blk = jnp.broadcast_to(d.reshape(g, 1, SSM_GROUP, 1), (g, 2, SSM_GROUP, LANES))
    mats = jnp.concatenate(
        [dup(jnp.swapaxes(per_group(b_re), 2, 3)), dup(jnp.swapaxes(per_group(b_im), 2, 3)),
         jnp.concatenate([cr, -ci], axis=-1), dup(cr), dup(ci), d_blk], axis=2)
    exp_c = jnp.asarray(_expand_consts(), BF16)

    return pl.pallas_call(
        functools.partial(_ssm_kernel, seq_chunks=seq_chunks),
        grid=(g,),
        in_specs=[
            pl.BlockSpec((None, n_ct, CHUNK, SSM_GROUP, C_TILE), lambda i: (i, 0, 0, 0, 0)),
            pl.BlockSpec((None, 2, 3, LANES), lambda i: (i, 0, 0, 0)),
            pl.BlockSpec((None, 2, 6 * SSM_GROUP, LANES), lambda i: (i, 0, 0, 0)),
            pl.BlockSpec((4, CHUNK_W, LANES), lambda i: (0, 0, 0)),
            pl.BlockSpec((epg, D_MODEL, D_EXPERT), lambda i: (base + i, 0, 0)),
            pl.BlockSpec((epg, D_MODEL, D_EXPERT), lambda i: (base + i, 0, 0)),
            pl.BlockSpec((epg, D_EXPERT, D_MODEL), lambda i: (base + i, 0, 0)),
        ],
        out_specs=[
            pl.BlockSpec((None, n_ct, CHUNK, SSM_GROUP, C_TILE), lambda i: (i, 0, 0, 0, 0)),
            pl.BlockSpec((epg, D_MODEL, D_EXPERT), lambda i: (i, 0, 0)),
            pl.BlockSpec((epg, D_MODEL, D_EXPERT), lambda i: (i, 0, 0)),
            pl.BlockSpec((epg, D_EXPERT, D_MODEL), lambda i: (i, 0, 0)),
        ],
        out_shape=[
            jax.ShapeDtypeStruct(ut.shape, BF16),
            jax.ShapeDtypeStruct((N_EXPERTS, D_MODEL, D_EXPERT), BF16),
            jax.ShapeDtypeStruct((N_EXPERTS, D_MODEL, D_EXPERT), BF16),
            jax.ShapeDtypeStruct((N_EXPERTS, D_EXPERT, D_MODEL), BF16),
        ],
        compiler_params=_params(("parallel",)),
        name="ssm",
    )(ut, vecs, mats, exp_c, w_gate, w_up, w_down)


def _mix_out_kernel(x_hbm, a_hbm, yt_ref, gwt_ref, gb_ref, wo_ref, h_hbm, xbuf, abuf, hbuf,
                    xsem, asem, hsem):
    nc = C_TILE
    x = _load_time_major_rows(x_hbm, xbuf, xsem)
    a = _load_time_major_rows(a_hbm, abuf, asem).astype(BF16)
    y = jnp.concatenate([yt_ref[:, j, :, :].reshape(D_SSM, nc) for j in range(S_TILE)],
                        axis=1).astype(F32)
    z = 0.5 * y * (1.0 + jnp.tanh(math.sqrt(2.0 / math.pi) * (y + 0.044715 * (y * y * y))))
    gate = _sigmoid(_dot(gwt_ref[...], z.astype(BF16)) + gb_ref[...])
    s = (z * gate).astype(BF16)
    h = x + _dot(a, wo_ref[:D_POOL, :]) + _dot_tn(s, wo_ref[D_POOL:, :])
    _store_time_major_rows(h, h_hbm, hbuf, hsem)


def _mix_out(x3, a3, yt, glu_w_t, glu_b_col, w_out):
    nch = x3.shape[0]
    return pl.pallas_call(
        _mix_out_kernel,
        grid=(nch // C_TILE, CHUNK // S_TILE),
        in_specs=[
            pl.BlockSpec(memory_space=pl.ANY),
            pl.BlockSpec(memory_space=pl.ANY),
            pl.BlockSpec((N_SSM_GROUPS, None, S_TILE, SSM_GROUP, C_TILE), lambda c, t: (0, c, t, 0, 0)),
            pl.BlockSpec((D_SSM, D_SSM), lambda c, t: (0, 0)),
            pl.BlockSpec((D_SSM, 1), lambda c, t: (0, 0)),
            pl.BlockSpec((D_MODEL, D_MODEL), lambda c, t: (0, 0)),
        ],
        out_specs=pl.BlockSpec(memory_space=pl.ANY),
        out_shape=jax.ShapeDtypeStruct((nch, CHUNK, D_MODEL), F32),
        scratch_shapes=[
            pltpu.VMEM((2, S_TILE, C_TILE, D_MODEL), F32),
            pltpu.VMEM((2, S_TILE, C_TILE, D_POOL), F32),
            pltpu.VMEM((2, S_TILE, C_TILE, D_MODEL), F32),
            pltpu.SemaphoreType.DMA((2,)),
            pltpu.SemaphoreType.DMA((2,)),
            pltpu.SemaphoreType.DMA((2,)),
        ],
        compiler_params=_params(("arbitrary", "arbitrary")),
        name="mix_out",
    )(x3, a3, yt, glu_w_t, glu_b_col, w_out)


def _pack_rows(x):
    b = lax.bitcast_convert_type(x.astype(BF16).astype(F32), U32)
    return (b[:, :HALF] & jnp.uint32(0xFFFF0000)) | (b[:, HALF:] >> 16)


def _unpack_rows(w):
    lo = lax.bitcast_convert_type(w & jnp.uint32(0xFFFF0000), F32)
    hi = lax.bitcast_convert_type(w << 16, F32)
    return lo, hi


def _split_bf16(x):
    hi = x.astype(BF16)
    return hi, (x - hi.astype(F32)).astype(BF16)


def _route(v32, wr_ref, br_ref):
    v_hi, v_lo = _split_bf16(v32)
    w_hi, w_lo = _split_bf16(wr_ref[...])
    logits = _dot(v_hi, w_hi) + (_dot(v_lo, w_hi) + _dot(v_hi, w_lo)) + br_ref[...]
    return jnp.transpose(logits)


def _top1(x, valid=None):
    n = x.shape[0]
    row = lax.broadcasted_iota(jnp.int32, x.shape, 0).astype(F32)
    if valid is not None:
        x = jnp.where(valid, x, -jnp.inf)
    m = jnp.max(x, axis=0, keepdims=True)
    idx = jnp.min(jnp.where(x == m, row, float(n)), axis=0, keepdims=True)
    return m, idx, x, row


def _split_planes(packed, ref):
    ref[0] = packed[:, :SC_ROW]
    ref[1] = packed[:, SC_ROW:]


def _router_kernel(h_ref, g_ref, wr_ref, br_ref, before_ref, vp_ref, meta_t_ref, cnt_ref, carry_ref):
    @pl.when(pl.program_id(0) == 0)
    def _():
        carry_ref[...] = jnp.zeros_like(carry_ref)

    v32 = _rms(h_ref[...], g_ref[...])
    _split_planes(_pack_rows(v32), vp_ref)
    lt = _route(v32, wr_ref, br_ref)
    tm = lt.shape[1]
    eg = EXPERTS_PER_GROUP

    grp = lt[:eg]
    grp_row = lax.broadcasted_iota(jnp.int32, grp.shape, 0)
    mg, grp_idx, grp, _ = _top1(grp, grp_row < N_EXPERT_GROUPS)
    grp_p = 1.0 / jnp.sum(jnp.exp(grp - mg), axis=0, keepdims=True)
    le = jnp.zeros((eg, tm), F32)
    for g in range(N_EXPERT_GROUPS):
        le = jnp.where(grp_idx == float(g), lt[eg * (g + 1):eg * (g + 2)], le)
    m1, i1, le, row = _top1(le)
    z = jnp.sum(jnp.exp(le - m1), axis=0, keepdims=True)
    m2, i2, _, _ = _top1(jnp.where(row == i1, -jnp.inf, le))
    p1 = 1.0 / z
    p2 = jnp.exp(m2 - m1) / z
    tot = p1 + p2
    w1 = grp_p * (p1 / tot)
    w2 = grp_p * (p2 / tot)
    e1 = grp_idx * eg + i1
    e2 = grp_idx * eg + i2

    erow = lax.broadcasted_iota(jnp.int32, (N_EXPERTS, tm), 0).astype(F32)
    onehot = jnp.where(erow == e1, 1.0, jnp.where(erow == e2, 1.0, 0.0))
    before = _dot(onehot.astype(BF16), before_ref[...]) + carry_ref[...]
    rank1 = jnp.sum(jnp.where(erow == e1, before, 0.0), axis=0, keepdims=True)
    rank2 = jnp.sum(jnp.where(erow == e2, before, 0.0), axis=0, keepdims=True)
    carry = carry_ref[...] + jnp.sum(onehot, axis=1, keepdims=True)
    carry_ref[...] = carry
    cnt_ref[...] = carry

    mrow = lax.broadcasted_iota(jnp.int32, (META_ROWS, tm), 0)
    meta_t_ref[...] = jnp.where(mrow == 0, e1, jnp.where(mrow == 1, e2, jnp.where(
        mrow == 2, rank1, jnp.where(mrow == 3, rank2, jnp.where(
            mrow == 4, w1, jnp.where(mrow == 5, w2, 0.0))))))


def _earlier_matrix(tm):
    return np.triu(np.ones((tm, tm), np.float32), k=1)


def _router(h1, g_ffn, w_router, b_router, tm):
    t = h1.shape[0]
    return pl.pallas_call(
        _router_kernel,
        grid=(t // tm,),
        in_specs=[
            pl.BlockSpec((tm, D_MODEL), lambda i: (i, 0)),
            pl.BlockSpec((1, D_MODEL), lambda i: (0, 0)),
            pl.BlockSpec((D_MODEL, ROUTER_W), lambda i: (0, 0)),
            pl.BlockSpec((1, ROUTER_W), lambda i: (0, 0)),
            pl.BlockSpec((tm, tm), lambda i: (0, 0)),
        ],
        out_specs=[
            pl.BlockSpec((2, tm, SC_ROW), lambda i: (0, i, 0)),
            pl.BlockSpec((META_ROWS, tm), lambda i: (0, i)),
            pl.BlockSpec((N_EXPERTS, 1), lambda i: (0, 0)),
        ],
        out_shape=[
            jax.ShapeDtypeStruct((2, t, SC_ROW), U32),
            jax.ShapeDtypeStruct((META_ROWS, t), F32),
            jax.ShapeDtypeStruct((N_EXPERTS, 1), F32),
        ],
        scratch_shapes=[pltpu.VMEM((N_EXPERTS, 1), F32)],
        compiler_params=_params(("arbitrary",)),
        name="router",
    )(h1, g_ffn, w_router, b_router, jnp.asarray(_earlier_matrix(tm), BF16))


def _plan(meta_t, counts, n_tiles):
    e1 = meta_t[0].astype(jnp.int32)
    e2 = meta_t[1].astype(jnp.int32)
    rank1 = meta_t[2].astype(jnp.int32)
    rank2 = meta_t[3].astype(jnp.int32)
    cnt = counts[:, 0].astype(jnp.int32)
    padded = ((cnt + ROW_TILE - 1) // ROW_TILE) * ROW_TILE
    ends = jnp.cumsum(padded)
    starts = ends - padded
    experts = jnp.arange(N_EXPERTS, dtype=jnp.int32)
    pos1 = rank1 + jnp.sum(jnp.where(e1[None, :] == experts[:, None], starts[:, None], 0), axis=0)
    pos2 = rank2 + jnp.sum(jnp.where(e2[None, :] == experts[:, None], starts[:, None], 0), axis=0)
    tile_start = jnp.arange(n_tiles, dtype=jnp.int32) * ROW_TILE
    tile_expert = jnp.sum((tile_start[:, None] >= ends[None, :]).astype(jnp.int32), axis=1)
    tile_expert = jnp.minimum(tile_expert, N_EXPERTS - 1)
    rows_left = jnp.sum(jnp.where(tile_expert[:, None] == experts, cnt + starts, 0), axis=1) - tile_start
    n_valid = jnp.clip(rows_left, 0, ROW_TILE).astype(jnp.int32)
    last_used = jnp.maximum(ends[-1] // ROW_TILE - 1, 0)
    block = jnp.minimum(jnp.arange(n_tiles, dtype=jnp.int32), last_used)
    tile_expert = jnp.sum(jnp.where(block[:, None] == jnp.arange(n_tiles)[None, :],
                                    tile_expert[None, :], 0), axis=1)
    plane = n_tiles * ROW_TILE
    half_rows = jnp.concatenate([pos1, pos1 + plane, pos2, pos2 + plane])[None]
    return half_rows, tile_expert, n_valid, block


def _sc_mesh():
    return plsc.VectorSubcoreMesh(core_axis_name="c", subcore_axis_name="s")


def _sc_scatter_rows(rows, idx, n_out):
    t, width = rows.shape
    steps = t // SC_WINDOW
    half = steps // 2

    @pl.kernel(out_type=jax.ShapeDtypeStruct((n_out, width), rows.dtype), mesh=_sc_mesh(),
               scratch_types=[], name="moe_scatter")
    def scatter(rows_hbm, idx_hbm, out_hbm):
        def body(rows_vmem, idx0_vmem, idx1_vmem):
            pltpu.sync_copy(rows_vmem, out_hbm.at[idx0_vmem.at[0]])
            pltpu.sync_copy(rows_vmem, out_hbm.at[idx1_vmem.at[0]])

        pltpu.emit_pipeline(
            body,
            grid=(2, half),
            in_specs=[pl.BlockSpec((SC_WINDOW, width), lambda c, j: (c * half + j, 0)),
                      pl.BlockSpec((1, SC_WINDOW), lambda c, j: (0, c * half + j)),
                      pl.BlockSpec((1, SC_WINDOW), lambda c, j: (0, steps + c * half + j))],
            out_specs=[],
            core_axis_name=("c", "s"),
            dimension_semantics=(pltpu.PARALLEL, pltpu.PARALLEL),
        )(rows_hbm, idx_hbm, idx_hbm)

    return scatter(rows, idx)


def _sc_gather_rows(table, idx):
    m = idx.shape[1]
    width = table.shape[1]
    steps = m // (2 * SC_WINDOW)

    @pl.kernel(out_type=jax.ShapeDtypeStruct((m, width), table.dtype), mesh=_sc_mesh(),
               scratch_types=[], name="moe_gather")
    def gather(table_hbm, idx_hbm, out_hbm):
        def body(idx_vmem, out_vmem):
            pltpu.sync_copy(table_hbm.at[idx_vmem.at[0]], out_vmem)

        pltpu.emit_pipeline(
            body,
            grid=(2, steps),
            in_specs=[pl.BlockSpec((1, SC_WINDOW), lambda k, j: (0, k * steps + j))],
            out_specs=[pl.BlockSpec((SC_WINDOW, width), lambda k, j: (k * steps + j, 0))],
            core_axis_name=("c", "s"),
            dimension_semantics=(pltpu.PARALLEL, pltpu.PARALLEL),
        )(idx_hbm, out_hbm)

    return gather(table, idx)


def _experts_kernel(te_ref, nv_ref, blk_ref, xs_ref, wg_ref, wu_ref, wd_ref, ys_ref):
    r = pl.program_id(0)
    n_valid = nv_ref[r]

    @pl.when(n_valid > 0)
    def _():
        parts = [p.astype(BF16) for p in _unpack_rows(xs_ref[0]) + _unpack_rows(xs_ref[1])]
        cols = (0, 2 * SC_ROW, SC_ROW, 3 * SC_ROW)
        hg = sum(_dot(p, wg_ref[c:c + SC_ROW, :]) for p, c in zip(parts, cols))
        hu = sum(_dot(p, wu_ref[c:c + SC_ROW, :]) for p, c in zip(parts, cols))
        row = lax.broadcasted_iota(jnp.int32, (ROW_TILE, 1), 0)
        hid = jnp.where(row < n_valid, hg * _sigmoid(hg) * hu, 0.0).astype(BF16)
        _split_planes(_pack_rows(_dot(hid, wd_ref[...])), ys_ref)


def _experts(xs, tile_expert, n_valid, block, w_gate, w_up, w_down):
    n_tiles = xs.shape[1] // ROW_TILE
    w_spec = pl.BlockSpec((None, D_MODEL, D_EXPERT), lambda r, te, nv, blk: (te[r], 0, 0))
    grid_spec = pltpu.PrefetchScalarGridSpec(
        num_scalar_prefetch=3,
        grid=(n_tiles,),
        in_specs=[
            pl.BlockSpec((2, ROW_TILE, SC_ROW), lambda r, te, nv, blk: (0, blk[r], 0)),
            w_spec, w_spec,
            pl.BlockSpec((None, D_EXPERT, D_MODEL), lambda r, te, nv, blk: (te[r], 0, 0)),
        ],
        out_specs=pl.BlockSpec((2, ROW_TILE, SC_ROW), lambda r, te, nv, blk: (0, blk[r], 0)),
    )
    return pl.pallas_call(
        _experts_kernel,
        grid_spec=grid_spec,
        out_shape=jax.ShapeDtypeStruct((2, n_tiles * ROW_TILE, SC_ROW), U32),
        compiler_params=_params(("arbitrary",)),
        name="experts",
    )(tile_expert, n_valid, block, xs, w_gate, w_up, w_down)


def _ple_kernel(h_ref, yg_ref, meta_t_ref, p_ref, wg_ref, bg_ref, wp_ref, gf_ref, *rest, final_norm):
    o_ref = rest[-1]
    meta = jnp.transpose(meta_t_ref[...])
    w1 = meta[:, 4:5]
    w2 = meta[:, 5:6]
    q0, q2 = (w1 * u + w2 * v for u, v in zip(_unpack_rows(yg_ref[0]), _unpack_rows(yg_ref[2])))
    q1, q3 = (w1 * u + w2 * v for u, v in zip(_unpack_rows(yg_ref[1]), _unpack_rows(yg_ref[3])))
    moe = jnp.concatenate([q0, q1, q2, q3], axis=1)
    h = h_ref[...] + moe
    gate = _sigmoid(_dot(_rms(h).astype(BF16), wg_ref[...]) + bg_ref[...])
    h = h + gate * _dot(p_ref[...].astype(BF16), wp_ref[...])
    o_ref[...] = _rms(h, gf_ref[...]) if final_norm else h


def _ple(h1, yg, meta_t, p2, w_gate, b_gate, w_proj, g_final, final_norm, tm, part, prev_out):
    t = h1.shape[0]
    steps = t // tm // GATHER_PARTS
    off = part * steps
    in_specs = [
        pl.BlockSpec((tm, D_MODEL), lambda i: (i + off, 0)),
        pl.BlockSpec((4, tm, SC_ROW), lambda i: (0, i, 0)),
        pl.BlockSpec((META_ROWS, tm), lambda i: (0, i + off)),
        pl.BlockSpec((tm, D_PLE), lambda i: (i + off, 0)),
        pl.BlockSpec((D_MODEL, D_MODEL), lambda i: (0, 0)),
        pl.BlockSpec((1, D_MODEL), lambda i: (0, 0)),
        pl.BlockSpec((D_PLE, D_MODEL), lambda i: (0, 0)),
        pl.BlockSpec((1, D_MODEL), lambda i: (0, 0)),
    ]
    args = [h1, yg, meta_t, p2, w_gate, b_gate, w_proj, g_final]
    aliases = {}
    if prev_out is not None:
        in_specs.append(pl.BlockSpec(memory_space=pl.ANY))
        args.append(prev_out)
        aliases = {len(args) - 1: 0}
    return pl.pallas_call(
        functools.partial(_ple_kernel, final_norm=final_norm),
        grid=(steps,),
        in_specs=in_specs,
        out_specs=pl.BlockSpec((tm, D_MODEL), lambda i: (i + off, 0)),
        out_shape=jax.ShapeDtypeStruct((t, D_MODEL), F32),
        input_output_aliases=aliases,
        compiler_params=_params(("parallel",)),
        name="ple",
    )(*args)


def kernel(x, p, g_mix, w_in, pool_w, pool_scale, ssm_a_re, ssm_a_im, ssm_log_dt, ssm_b_re,
           ssm_b_im, ssm_c_re, ssm_c_im, ssm_d, glu_w, glu_b, w_out, g_ffn, router_grp_w,
           router_grp_b, router_exp_w, router_exp_b, exp_w_gate, exp_w_up, exp_w_down, g_ple,
           ple_gate_w, ple_gate_b, ple_proj_w, g_final):
    bsz, seq, dm = x.shape
    depth = g_mix.shape[0]
    t = bsz * seq
    seq_chunks = seq // CHUNK
    nch = t // CHUNK
    tm = 1024
    n_sorted = 2 * t + N_EXPERTS * ROW_TILE
    w_gate_all = exp_w_gate.reshape(depth * N_EXPERTS, dm, D_EXPERT)
    w_up_all = exp_w_up.reshape(depth * N_EXPERTS, dm, D_EXPERT)
    w_down_all = exp_w_down.reshape(depth * N_EXPERTS, D_EXPERT, dm)

    h = x.reshape(t, dm)
    for i in range(depth):
        w_in_b = (g_mix[i][:, None] * w_in[i]).astype(BF16)
        zp, ut = _in_proj(h.reshape(nch, CHUNK, dm), w_in_b[:, :D_POOL],
                          jnp.transpose(w_in_b[:, D_POOL:]))
        a = _pool(zp.reshape(bsz, seq, D_POOL), pool_w[i], pool_scale[i][None])
        yt, w_gate_b, w_up_b, w_down_b = _ssm(
            ut, ssm_a_re[i], ssm_a_im[i], ssm_log_dt[i], ssm_b_re[i], ssm_b_im[i], ssm_c_re[i],
            ssm_c_im[i], ssm_d[i], seq_chunks, w_gate_all, w_up_all, w_down_all, i)
        h = _mix_out(h.reshape(nch, CHUNK, dm), a.reshape(nch, CHUNK, D_POOL), yt,
                     jnp.transpose(glu_w[i]).astype(BF16), glu_b[i][:, None],
                     w_out[i].astype(BF16)).reshape(t, dm)

        eg = EXPERTS_PER_GROUP
        w_router = jnp.concatenate(
            [router_grp_w[i], jnp.zeros((dm, eg - N_EXPERT_GROUPS), F32),
             jnp.transpose(router_exp_w[i], (1, 0, 2)).reshape(dm, N_EXPERTS),
             jnp.zeros((dm, ROUTER_W - eg - N_EXPERTS), F32)], axis=1)
        b_router = jnp.concatenate(
            [router_grp_b[i], jnp.zeros((eg - N_EXPERT_GROUPS,), F32),
             router_exp_b[i].reshape(N_EXPERTS),
             jnp.zeros((ROUTER_W - eg - N_EXPERTS,), F32)])[None]
        vp, meta_t, counts = _router(h, g_ffn[i][None], w_router, b_router, tm)
        idx, tile_expert, n_valid, block = _plan(meta_t, counts, n_sorted // ROW_TILE)
        xs = _sc_scatter_rows(vp.reshape(2 * t, SC_ROW), idx, 2 * n_sorted)
        ys = _experts(xs.reshape(2, n_sorted, SC_ROW), tile_expert, n_valid, block,
                      w_gate_b, w_up_b, w_down_b)
        ys2 = ys.reshape(2 * n_sorted, SC_ROW)
        idx4 = idx.reshape(4, t)
        tp = t // GATHER_PARTS
        ple_wg = (g_ple[i][:, None] * ple_gate_w[i]).astype(BF16)
        ple_wp = ple_proj_w[i].astype(BF16)
        out = None
        for q in range(GATHER_PARTS):
            yg_q = _sc_gather_rows(ys2, idx4[:, q * tp:(q + 1) * tp].reshape(1, 4 * tp))
            out = _ple(h, yg_q.reshape(4, tp, SC_ROW), meta_t, p[i].reshape(t, D_PLE), ple_wg,
                       ple_gate_b[i][None], ple_wp, g_final[None], i == depth - 1, tm, q, out)
        h = out
    return h.reshape(bsz, seq, dm)
```

```python
import functools
import math

import numpy as np
import jax
import jax.numpy as jnp
from jax import lax
from jax.experimental import pallas as pl
from jax.experimental.pallas import tpu as pltpu
from jax.experimental.pallas import tpu_sc as plsc

F32 = jnp.float32
BF16 = jnp.bfloat16
U32 = jnp.uint32

D_MODEL = 1024
D_POOL = 512
D_SSM = 512
POOL_WINDOWS = (2, 4, 8, 16)
POOL_GROUP = 128
SSM_GROUP = 16
N_SSM_GROUPS = 32
SSM_STATE = 64
N_EXPERT_GROUPS = 4
EXPERTS_PER_GROUP = 8
N_EXPERTS = N_EXPERT_GROUPS * EXPERTS_PER_GROUP
D_EXPERT = 256
D_PLE = 256
RMS_EPS = 1e-6

LANES = 128
CHUNK = 32
CHUNK_W = CHUNK * SSM_GROUP
T_SUB = 8
S_TILE = 8
C_TILE = 128
ROUTER_W = LANES
HALF = D_MODEL // 2
ROW_TILE = 1024
SC_WINDOW = 128
SC_ROW = HALF // 2
META_ROWS = 8
GATHER_PARTS = 2
VMEM_LIMIT = 56 * 1024 * 1024


def _dot(a, b):
    return jnp.dot(a, b, preferred_element_type=F32)


def _dot_nt(a, b):
    return lax.dot_general(a, b, (((1,), (1,)), ((), ())), preferred_element_type=F32)


def _dot_tn(a, b):
    return lax.dot_general(a, b, (((0,), (0,)), ((), ())), preferred_element_type=F32)


def _rms(x, g=None):
    y = x * lax.rsqrt(jnp.mean(x * x, axis=-1, keepdims=True) + RMS_EPS)
    return y if g is None else y * g


def _sigmoid(x):
    return 1.0 / (1.0 + jnp.exp(-x))


def _params(sem):
    return pltpu.CompilerParams(dimension_semantics=sem, vmem_limit_bytes=VMEM_LIMIT)


def _tile_step():
    n_s = pl.num_programs(1)
    return pl.program_id(0) * n_s + pl.program_id(1), pl.num_programs(0) * n_s


def _row_copies(hbm, buf, sem, step_idx, slot_idx, to_hbm):
    n_s = pl.num_programs(1)
    c0 = (step_idx // n_s) * C_TILE
    s0 = (step_idx % n_s) * S_TILE
    out = []
    for j in range(S_TILE):
        far, near = hbm.at[pl.ds(c0, C_TILE), s0 + j, :], buf.at[slot_idx, j]
        src, dst = (near, far) if to_hbm else (far, near)
        out.append(pltpu.make_async_copy(src, dst, sem.at[slot_idx]))
    return out


def _load_time_major_rows(hbm, buf, sem):
    step, n_steps = _tile_step()
    slot = step % 2

    @pl.when(step == 0)
    def _():
        for cp in _row_copies(hbm, buf, sem, step, slot, False):
            cp.start()

    @pl.when(step + 1 < n_steps)
    def _():
        for cp in _row_copies(hbm, buf, sem, step + 1, 1 - slot, False):
            cp.start()

    for cp in _row_copies(hbm, buf, sem, step, slot, False):
        cp.wait()
    return buf[slot].reshape(S_TILE * C_TILE, buf.shape[-1])


def _store_time_major_rows(val, hbm, buf, sem):
    step, n_steps = _tile_step()
    slot = step % 2

    @pl.when(step >= 2)
    def _():
        for cp in _row_copies(hbm, buf, sem, step - 2, slot, True):
            cp.wait()

    buf[slot] = val.reshape(S_TILE, C_TILE, val.shape[-1])
    for cp in _row_copies(hbm, buf, sem, step, slot, True):
        cp.start()

    @pl.when(step == n_steps - 1)
    def _():
        @pl.when(step >= 1)
        def _():
            for cp in _row_copies(hbm, buf, sem, step - 1, 1 - slot, True):
                cp.wait()
        for cp in _row_copies(hbm, buf, sem, step, slot, True):
            cp.wait()


def _in_proj_kernel(x_hbm, wp_ref, wst_ref, zp_hbm, ut_ref, xbuf, zbuf, xsem, zsem):
    nc = C_TILE
    u = _rms(_load_time_major_rows(x_hbm, xbuf, xsem)).astype(BF16)
    zt = _dot_nt(wst_ref[...], u).astype(BF16)
    for j in range(S_TILE):
        ut_ref[:, j, :, :] = zt[:, j * nc:(j + 1) * nc].reshape(N_SSM_GROUPS, SSM_GROUP, nc)
    _store_time_major_rows(_dot(u, wp_ref[...]), zp_hbm, zbuf, zsem)


def _in_proj(x3, w_pool, w_ssm_t):
    nch = x3.shape[0]
    return pl.pallas_call(
        _in_proj_kernel,
        grid=(nch // C_TILE, CHUNK // S_TILE),
        in_specs=[
            pl.BlockSpec(memory_space=pl.ANY),
            pl.BlockSpec((D_MODEL, D_POOL), lambda c, s: (0, 0)),
            pl.BlockSpec((D_SSM, D_MODEL), lambda c, s: (0, 0)),
        ],
        out_specs=[
            pl.BlockSpec(memory_space=pl.ANY),
            pl.BlockSpec((N_SSM_GROUPS, None, S_TILE, SSM_GROUP, C_TILE), lambda c, s: (0, c, s, 0, 0)),
        ],
        out_shape=[
            jax.ShapeDtypeStruct((nch, CHUNK, D_POOL), F32),
            jax.ShapeDtypeStruct((N_SSM_GROUPS, nch // C_TILE, CHUNK, SSM_GROUP, C_TILE), BF16),
        ],
        scratch_shapes=[
            pltpu.VMEM((2, S_TILE, C_TILE, D_MODEL), F32),
            pltpu.VMEM((2, S_TILE, C_TILE, D_POOL), F32),
            pltpu.SemaphoreType.DMA((2,)),
            pltpu.SemaphoreType.DMA((2,)),
        ],
        compiler_params=_params(("arbitrary", "arbitrary")),
        name="in_proj",
    )(x3, w_pool, w_ssm_t)


def _shift_rows(x, d, row, n):
    if d == 0:
        return x
    r = pltpu.roll(x, d % n, 0)
    if d > 0:
        return jnp.where(row >= d, r, 0.0)
    return jnp.where(row < n + d, r, 0.0)


def _pool_kernel(z_ref, w_ref, sc_ref, o_ref):
    n = z_ref.shape[0]
    gi = pl.program_id(1)
    row = lax.broadcasted_iota(jnp.int32, (n, 1), 0)

    for k, w in enumerate(POOL_WINDOWS):
        @pl.when(gi == k)
        def _(w=w):
            x = z_ref[...]
            half = w // 2
            pd, pu, span = x, x, 1
            while span < half:
                pd = pd + _shift_rows(pd, span, row, n)
                pu = pu + _shift_rows(pu, -span, row, n)
                span *= 2
            total = _shift_rows(pd, 1, row, n) + pu
            lo = jnp.maximum(row - half, 0)
            hi = jnp.minimum(row + half, n)
            cnt = (hi - lo).astype(F32)
            diff = (total / cnt - x).astype(BF16)
            o_ref[...] = _dot(diff, w_ref[...].astype(BF16)) * sc_ref[...]


def _pool(zp3, pool_w, pool_scale):
    b, s, _ = zp3.shape
    return pl.pallas_call(
        _pool_kernel,
        grid=(b, len(POOL_WINDOWS)),
        in_specs=[
            pl.BlockSpec((None, s, POOL_GROUP), lambda i, g: (i, 0, g)),
            pl.BlockSpec((None, POOL_GROUP, POOL_GROUP), lambda i, g: (g, 0, 0)),
            pl.BlockSpec((1, POOL_GROUP), lambda i, g: (0, g)),
        ],
        out_specs=pl.BlockSpec((None, s, POOL_GROUP), lambda i, g: (i, 0, g)),
        out_shape=jax.ShapeDtypeStruct((b, s, D_POOL), F32),
        compiler_params=_params(("parallel", "parallel")),
        name="pool",
    )(zp3, pool_w, pool_scale)


def _expand_consts():
    time = np.arange(CHUNK_W) // SSM_GROUP
    def onehot(e):
        m = np.zeros((CHUNK_W, LANES), np.float32)
        m[np.arange(CHUNK_W), e] = 1.0
        return m
    return np.stack([
        onehot(CHUNK - 1 - time),
        onehot(time),
        onehot(time + 1),
        onehot(CHUNK - time),
    ])


def _cmul_packed(x, p, q):
    return x * p + pltpu.roll(x, LANES // 2, 1) * q


def _ssm_kernel(u_ref, vec_ref, mat_ref, exp_ref, wg_ref, wu_ref, wd_ref, y_ref, og_ref, ou_ref,
                od_ref, *, seq_chunks):
    og_ref[...] = wg_ref[...].astype(BF16)
    ou_ref[...] = wu_ref[...].astype(BF16)
    od_ref[...] = wd_ref[...].astype(BF16)

    n_ct = u_ref.shape[0]
    nch = n_ct * C_TILE
    half = LANES // 2
    lane = lax.broadcasted_iota(jnp.int32, (1, LANES), 1)
    lo_half = lane < half

    def direction(di):
        a_re = vec_ref[di, 0:1]
        a_im = vec_ref[di, 1:2]
        dt = jnp.exp(vec_ref[di, 2:3])
        mag = jnp.exp(a_re * dt)
        ang = a_im * dt
        lam = jnp.where(lo_half, mag * jnp.cos(ang), mag * jnp.sin(ang))
        lb_re = mag * jnp.cos(ang)
        lb_im = mag * jnp.sin(ang)
        den = a_re * a_re + a_im * a_im
        f_re = ((lb_re - 1.0) * a_re + lb_im * a_im) / den
        f_im = (lb_im * a_re - (lb_re - 1.0) * a_im) / den
        return lam, f_re, f_im

    def power_table(lam):
        e = lax.broadcasted_iota(jnp.int32, (LANES, 1), 0)
        tab = jnp.where(lo_half, 1.0, 0.0) * jnp.ones((LANES, 1), F32)
        sq = lam
        for k in range(7):
            p = jnp.where(lo_half, sq, pltpu.roll(sq, half, 1))
            q = jnp.where(lo_half, -pltpu.roll(sq, half, 1), sq)
            tab = jnp.where(((e >> k) & 1) == 1, _cmul_packed(tab, p, q), tab)
            sq = _cmul_packed(sq, p, q)
        return tab

    def tile_rows(x16):
        return jnp.broadcast_to(x16[None], (CHUNK, SSM_GROUP, LANES)).reshape(CHUNK_W, LANES)

    def expanded(tab, which, v_re, v_im, conj_sign):
        lexp = _dot(exp_ref[which], tab.astype(BF16))
        if conj_sign > 0:
            p = jnp.where(lo_half, v_re, v_re)
            q = jnp.where(lo_half, -v_im, v_im)
        else:
            p = jnp.where(lo_half, v_re, -v_re)
            q = jnp.where(lo_half, -v_im, -v_im)
        return lexp * tile_rows(p) + pltpu.roll(lexp, half, 1) * tile_rows(q)

    lam_f, ff_re, ff_im = direction(0)
    lam_b, fb_re, fb_im = direction(1)
    tab_f = power_table(lam_f)
    tab_b = power_table(lam_b)

    def bbar(bt_re, bt_im, f_re, f_im):
        return bt_re * f_re - bt_im * f_im, bt_re * f_im + bt_im * f_re

    def mat(di, k):
        return mat_ref[di, k * SSM_GROUP:(k + 1) * SSM_GROUP, :]

    bf_re, bf_im = bbar(mat(0, 0), mat(0, 1), ff_re, ff_im)
    bb_re, bb_im = bbar(mat(1, 0), mat(1, 1), fb_re, fb_im)

    pb1 = expanded(tab_f, 0, bf_re, bf_im, 1)
    pb2 = expanded(tab_b, 1, bb_re, bb_im, 1)
    pb3 = expanded(tab_b, 2, bb_re, bb_im, 1)
    ft_f = expanded(tab_f, 2, mat(0, 3), mat(0, 4), -1)
    ft_b = expanded(tab_b, 3, mat(1, 3), mat(1, 4), -1)

    row_w = lax.broadcasted_iota(jnp.int32, (CHUNK_W, 1), 0)
    last_blk = row_w >= CHUNK_W - SSM_GROUP
    pb2_lag0 = jnp.where(last_blk, pltpu.roll(pb2, CHUNK_W - SSM_GROUP, 0), 0.0)
    ccr_f = mat(0, 2).astype(BF16)
    ccr_b = mat(1, 2).astype(BF16)
    r_lo = _dot_nt(ccr_f, pb1.astype(BF16)) + _dot_nt(ccr_b, pb2_lag0.astype(BF16))
    co = lax.broadcasted_iota(jnp.int32, (SSM_GROUP, CHUNK_W), 0)
    col = lax.broadcasted_iota(jnp.int32, (SSM_GROUP, CHUNK_W), 1)
    r_lo = r_lo + jnp.where(col == CHUNK_W - SSM_GROUP + co, mat(0, 5)[:, 0:1], 0.0)
    r_hi = _dot_nt(ccr_b, pb3.astype(BF16))
    r_t = jnp.concatenate([r_lo, r_hi], axis=1)
    g_t = jnp.concatenate(
        [pltpu.roll(r_t, SSM_GROUP * (tl + 1), 1) for tl in range(T_SUB)], axis=0
    ).astype(BF16)

    u = jnp.concatenate([u_ref[ct].reshape(CHUNK_W, C_TILE) for ct in range(n_ct)],
                        axis=1)
    e_mat = jnp.concatenate([pb1, pb2], axis=1).astype(BF16)
    xend = _dot_tn(e_mat, u)
    lanec = lax.broadcasted_iota(jnp.int32, (1, nch), 1) % seq_chunks
    ns = SSM_STATE

    def scan(re, im, tab, forward):
        lam_col = jnp.transpose(tab[CHUNK:CHUNK + 8, :])[:, 0:1]
        a, b = lam_col[:ns], lam_col[ns:]
        n_steps = int(math.log2(seq_chunks))

        def shifted(v, d):
            if forward:
                return jnp.where(lanec >= d, pltpu.roll(v, d, 1), 0.0)
            return jnp.where(lanec < seq_chunks - d, pltpu.roll(v, nch - d, 1), 0.0)

        for k in range(n_steps):
            sr, si = shifted(re, 1 << k), shifted(im, 1 << k)
            re, im = re + (sr * a - si * b), im + (sr * b + si * a)
            a, b = a * a - b * b, 2.0 * a * b
        return shifted(re, 1), shifted(im, 1)

    f_re, f_im = scan(xend[:ns], xend[ns:2 * ns], tab_f, True)
    b_re, b_im = scan(xend[2 * ns:3 * ns], xend[3 * ns:], tab_b, False)
    xin = jnp.concatenate([f_re, f_im, b_re, b_im], axis=0).astype(BF16)
    f_t = jnp.concatenate([ft_f, ft_b], axis=1).astype(BF16)

    toeplitz = jnp.concatenate(
        [g_t[:, CHUNK_W - LANES * th:2 * CHUNK_W - LANES * th] for th in range(CHUNK // T_SUB)],
        axis=0)
    y_t = _dot(toeplitz, u) + _dot(f_t, xin)
    for ct in range(n_ct):
        y_ref[ct] = y_t[:, ct * C_TILE:(ct + 1) * C_TILE].reshape(
            CHUNK, SSM_GROUP, C_TILE).astype(y_ref.dtype)


def _ssm(ut, a_re, a_im, log_dt, b_re, b_im, c_re, c_im, d, seq_chunks, w_gate, w_up, w_down, layer):
    g, n_ct = ut.shape[:2]
    assert N_EXPERTS % g == 0
    epg = N_EXPERTS // g
    base = layer * g
    n = SSM_STATE

    def per_group(a):
        return jnp.swapaxes(a, 0, 1)

    def dup(a):
        return jnp.concatenate([a, a], axis=-1)

    vecs = dup(jnp.stack([per_group(a_re), per_group(a_im),
                          jnp.broadcast_to(per_group(log_dt)[..., None], (g, 2, n))], axis=2))
    cr, ci = per_group(c_re), per_group(c_im)
    d_blk = jnp.broadcast_to(d.reshape(g, 1, SSM_GROUP, 1), (g, 2, SSM_GROUP, LANES))
    mats = jnp.concatenate(
        [dup(jnp.swapaxes(per_group(b_re), 2, 3)), dup(jnp.swapaxes(per_group(b_im), 2, 3)),
         jnp.concatenate([cr, -ci], axis=-1), dup(cr), dup(ci), d_blk], axis=2)
    exp_c = jnp.asarray(_expand_consts(), BF16)

    return pl.pallas_call(
        functools.partial(_ssm_kernel, seq_chunks=seq_chunks),
        grid=(g,),
        in_specs=[
            pl.BlockSpec((None, n_ct, CHUNK, SSM_GROUP, C_TILE), lambda i: (i, 0, 0, 0, 0)),
            pl.BlockSpec((None, 2, 3, LANES), lambda i: (i, 0, 0, 0)),
            pl.BlockSpec((None, 2, 6 * SSM_GROUP, LANES), lambda i: (i, 0, 0, 0)),
            pl.BlockSpec((4, CHUNK_W, LANES), lambda i: (0, 0, 0)),
            pl.BlockSpec((epg, D_MODEL, D_EXPERT), lambda i: (base + i, 0, 0)),
            pl.BlockSpec((epg, D_MODEL, D_EXPERT), lambda i: (base + i, 0, 0)),
            pl.BlockSpec((epg, D_EXPERT, D_MODEL), lambda i: (base + i, 0, 0)),
        ],
        out_specs=[
            pl.BlockSpec((None, n_ct, CHUNK, SSM_GROUP, C_TILE), lambda i: (i, 0, 0, 0, 0)),
            pl.BlockSpec((epg, D_MODEL, D_EXPERT), lambda i: (i, 0, 0)),
            pl.BlockSpec((epg, D_MODEL, D_EXPERT), lambda i: (i, 0, 0)),
            pl.BlockSpec((epg, D_EXPERT, D_MODEL), lambda i: (i, 0, 0)),
        ],
        out_shape=[
            jax.ShapeDtypeStruct(ut.shape, BF16),
            jax.ShapeDtypeStruct((N_EXPERTS, D_MODEL, D_EXPERT), BF16),
            jax.ShapeDtypeStruct((N_EXPERTS, D_MODEL, D_EXPERT), BF16),
            jax.ShapeDtypeStruct((N_EXPERTS, D_EXPERT, D_MODEL), BF16),
        ],
        compiler_params=_params(("parallel",)),
        name="ssm",
    )(ut, vecs, mats, exp_c, w_gate, w_up, w_down)


def _mix_out_kernel(x_hbm, a_hbm, yt_ref, gwt_ref, gb_ref, wo_ref, h_hbm, xbuf, abuf, hbuf, wo_s,
                    xsem, asem, hsem):
    nc = C_TILE

    @pl.when(_tile_step()[0] == 0)
    def _():
        wo_s[...] = wo_ref[...].astype(BF16)

    x = _load_time_major_rows(x_hbm, xbuf, xsem)
    a = _load_time_major_rows(a_hbm, abuf, asem).astype(BF16)
    y = jnp.concatenate([yt_ref[:, j, :, :].reshape(D_SSM, nc) for j in range(S_TILE)],
                        axis=1).astype(F32)
    z = 0.5 * y * (1.0 + jnp.tanh(math.sqrt(2.0 / math.pi) * (y + 0.044715 * (y * y * y))))
    gate = _sigmoid(_dot(gwt_ref[...], z.astype(BF16)) + gb_ref[...])
    s = (z * gate).astype(BF16)
    h = x + _dot(a, wo_s[:D_POOL, :]) + _dot_tn(s, wo_s[D_POOL:, :])
    _store_time_major_rows(h, h_hbm, hbuf, hsem)


def _mix_out(x3, a3, yt, glu_w_t, glu_b_col, w_out, layer):
    nch = x3.shape[0]
    return pl.pallas_call(
        _mix_out_kernel,
        grid=(nch // C_TILE, CHUNK // S_TILE),
        in_specs=[
            pl.BlockSpec(memory_space=pl.ANY),
            pl.BlockSpec(memory_space=pl.ANY),
            pl.BlockSpec((N_SSM_GROUPS, None, S_TILE, SSM_GROUP, C_TILE), lambda c, t: (0, c, t, 0, 0)),
            pl.BlockSpec((D_SSM, D_SSM), lambda c, t: (0, 0)),
            pl.BlockSpec((D_SSM, 1), lambda c, t: (0, 0)),
            pl.BlockSpec((D_MODEL, D_MODEL), lambda c, t: (layer, 0), pipeline_mode=pl.Buffered(1)),
        ],
        out_specs=pl.BlockSpec(memory_space=pl.ANY),
        out_shape=jax.ShapeDtypeStruct((nch, CHUNK, D_MODEL), F32),
        scratch_shapes=[
            pltpu.VMEM((2, S_TILE, C_TILE, D_MODEL), F32),
            pltpu.VMEM((2, S_TILE, C_TILE, D_POOL), F32),
            pltpu.VMEM((2, S_TILE, C_TILE, D_MODEL), F32),
            pltpu.VMEM((D_MODEL, D_MODEL), BF16),
            pltpu.SemaphoreType.DMA((2,)),
            pltpu.SemaphoreType.DMA((2,)),
            pltpu.SemaphoreType.DMA((2,)),
        ],
        compiler_params=_params(("arbitrary", "arbitrary")),
        name="mix_out",
    )(x3, a3, yt, glu_w_t, glu_b_col, w_out)


def _pack_rows(x):
    b = lax.bitcast_convert_type(x.astype(BF16).astype(F32), U32)
    return (b[:, :HALF] & jnp.uint32(0xFFFF0000)) | (b[:, HALF:] >> 16)


def _unpack_rows(w):
    lo = lax.bitcast_convert_type(w & jnp.uint32(0xFFFF0000), F32)
    hi = lax.bitcast_convert_type(w << 16, F32)
    return lo, hi


def _split_bf16(x):
    hi = x.astype(BF16)
    return hi, (x - hi.astype(F32)).astype(BF16)


def _route(v32, wr_ref, br_ref):
    v_hi, v_lo = _split_bf16(v32)
    w_hi, w_lo = _split_bf16(wr_ref[...])
    logits = _dot(v_hi, w_hi) + (_dot(v_lo, w_hi) + _dot(v_hi, w_lo)) + br_ref[...]
    return jnp.transpose(logits)


def _top1(x, valid=None):
    n = x.shape[0]
    row = lax.broadcasted_iota(jnp.int32, x.shape, 0).astype(F32)
    if valid is not None:
        x = jnp.where(valid, x, -jnp.inf)
    m = jnp.max(x, axis=0, keepdims=True)
    idx = jnp.min(jnp.where(x == m, row, float(n)), axis=0, keepdims=True)
    return m, idx, x, row


def _split_planes(packed, ref):
    ref[0] = packed[:, :SC_ROW]
    ref[1] = packed[:, SC_ROW:]


def _router_kernel(h_ref, g_ref, wr_ref, br_ref, before_ref, vp_ref, meta_t_ref, cnt_ref, carry_ref):
    @pl.when(pl.program_id(0) == 0)
    def _():
        carry_ref[...] = jnp.zeros_like(carry_ref)

    v32 = _rms(h_ref[...], g_ref[...])
    _split_planes(_pack_rows(v32), vp_ref)
    lt = _route(v32, wr_ref, br_ref)
    tm = lt.shape[1]
    eg = EXPERTS_PER_GROUP

    grp = lt[:eg]
    grp_row = lax.broadcasted_iota(jnp.int32, grp.shape, 0)
    mg, grp_idx, grp, _ = _top1(grp, grp_row < N_EXPERT_GROUPS)
    grp_p = 1.0 / jnp.sum(jnp.exp(grp - mg), axis=0, keepdims=True)
    le = jnp.zeros((eg, tm), F32)
    for g in range(N_EXPERT_GROUPS):
        le = jnp.where(grp_idx == float(g), lt[eg * (g + 1):eg * (g + 2)], le)
    m1, i1, le, row = _top1(le)
    z = jnp.sum(jnp.exp(le - m1), axis=0, keepdims=True)
    m2, i2, _, _ = _top1(jnp.where(row == i1, -jnp.inf, le))
    p1 = 1.0 / z
    p2 = jnp.exp(m2 - m1) / z
    tot = p1 + p2
    w1 = grp_p * (p1 / tot)
    w2 = grp_p * (p2 / tot)
    e1 = grp_idx * eg + i1
    e2 = grp_idx * eg + i2

    erow = lax.broadcasted_iota(jnp.int32, (N_EXPERTS, tm), 0).astype(F32)
    onehot = jnp.where(erow == e1, 1.0, jnp.where(erow == e2, 1.0, 0.0))
    before = _dot(onehot.astype(BF16), before_ref[...]) + carry_ref[...]
    rank1 = jnp.sum(jnp.where(erow == e1, before, 0.0), axis=0, keepdims=True)
    rank2 = jnp.sum(jnp.where(erow == e2, before, 0.0), axis=0, keepdims=True)
    carry = carry_ref[...] + jnp.sum(onehot, axis=1, keepdims=True)
    carry_ref[...] = carry
    cnt_ref[...] = carry

    mrow = lax.broadcasted_iota(jnp.int32, (META_ROWS, tm), 0)
    meta_t_ref[...] = jnp.where(mrow == 0, e1, jnp.where(mrow == 1, e2, jnp.where(
        mrow == 2, rank1, jnp.where(mrow == 3, rank2, jnp.where(
            mrow == 4, w1, jnp.where(mrow == 5, w2, 0.0))))))


def _earlier_matrix(tm):
    return np.triu(np.ones((tm, tm), np.float32), k=1)


def _router(h1, g_ffn, w_router, b_router, tm):
    t = h1.shape[0]
    return pl.pallas_call(
        _router_kernel,
        grid=(t // tm,),
        in_specs=[
            pl.BlockSpec((tm, D_MODEL), lambda i: (i, 0)),
            pl.BlockSpec((1, D_MODEL), lambda i: (0, 0)),
            pl.BlockSpec((D_MODEL, ROUTER_W), lambda i: (0, 0)),
            pl.BlockSpec((1, ROUTER_W), lambda i: (0, 0)),
            pl.BlockSpec((tm, tm), lambda i: (0, 0)),
        ],
        out_specs=[
            pl.BlockSpec((2, tm, SC_ROW), lambda i: (0, i, 0)),
            pl.BlockSpec((META_ROWS, tm), lambda i: (0, i)),
            pl.BlockSpec((N_EXPERTS, 1), lambda i: (0, 0)),
        ],
        out_shape=[
            jax.ShapeDtypeStruct((2, t, SC_ROW), U32),
            jax.ShapeDtypeStruct((META_ROWS, t), F32),
            jax.ShapeDtypeStruct((N_EXPERTS, 1), F32),
        ],
        scratch_shapes=[pltpu.VMEM((N_EXPERTS, 1), F32)],
        compiler_params=_params(("arbitrary",)),
        name="router",
    )(h1, g_ffn, w_router, b_router, jnp.asarray(_earlier_matrix(tm), BF16))


def _plan(meta_t, counts, n_tiles):
    e1 = meta_t[0].astype(jnp.int32)
    e2 = meta_t[1].astype(jnp.int32)
    rank1 = meta_t[2].astype(jnp.int32)
    rank2 = meta_t[3].astype(jnp.int32)
    cnt = counts[:, 0].astype(jnp.int32)
    padded = ((cnt + ROW_TILE - 1) // ROW_TILE) * ROW_TILE
    ends = jnp.cumsum(padded)
    starts = ends - padded
    experts = jnp.arange(N_EXPERTS, dtype=jnp.int32)
    pos1 = rank1 + jnp.sum(jnp.where(e1[None, :] == experts[:, None], starts[:, None], 0), axis=0)
    pos2 = rank2 + jnp.sum(jnp.where(e2[None, :] == experts[:, None], starts[:, None], 0), axis=0)
    tile_start = jnp.arange(n_tiles, dtype=jnp.int32) * ROW_TILE
    tile_expert = jnp.sum((tile_start[:, None] >= ends[None, :]).astype(jnp.int32), axis=1)
    tile_expert = jnp.minimum(tile_expert, N_EXPERTS - 1)
    rows_left = jnp.sum(jnp.where(tile_expert[:, None] == experts, cnt + starts, 0), axis=1) - tile_start
    n_valid = jnp.clip(rows_left, 0, ROW_TILE).astype(jnp.int32)
    last_used = jnp.maximum(ends[-1] // ROW_TILE - 1, 0)
    block = jnp.minimum(jnp.arange(n_tiles, dtype=jnp.int32), last_used)
    tile_expert = jnp.sum(jnp.where(block[:, None] == jnp.arange(n_tiles)[None, :],
                                    tile_expert[None, :], 0), axis=1)
    plane = n_tiles * ROW_TILE
    half_rows = jnp.concatenate([pos1, pos1 + plane, pos2, pos2 + plane])[None]
    return half_rows, tile_expert, n_valid, block


def _sc_mesh():
    return plsc.VectorSubcoreMesh(core_axis_name="c", subcore_axis_name="s")


def _sc_scatter_rows(rows, idx, n_out):
    t, width = rows.shape
    steps = t // SC_WINDOW
    half = steps // 2

    @pl.kernel(out_type=jax.ShapeDtypeStruct((n_out, width), rows.dtype), mesh=_sc_mesh(),
               scratch_types=[], name="moe_scatter")
    def scatter(rows_hbm, idx_hbm, out_hbm):
        def body(rows_vmem, idx0_vmem, idx1_vmem):
            pltpu.sync_copy(rows_vmem, out_hbm.at[idx0_vmem.at[0]])
            pltpu.sync_copy(rows_vmem, out_hbm.at[idx1_vmem.at[0]])

        pltpu.emit_pipeline(
            body,
            grid=(2, half),
            in_specs=[pl.BlockSpec((SC_WINDOW, width), lambda c, j: (c * half + j, 0)),
                      pl.BlockSpec((1, SC_WINDOW), lambda c, j: (0, c * half + j)),
                      pl.BlockSpec((1, SC_WINDOW), lambda c, j: (0, steps + c * half + j))],
            out_specs=[],
            core_axis_name=("c", "s"),
            dimension_semantics=(pltpu.PARALLEL, pltpu.PARALLEL),
        )(rows_hbm, idx_hbm, idx_hbm)

    return scatter(rows, idx)


def _sc_gather_rows(table, idx):
    m = idx.shape[1]
    width = table.shape[1]
    steps = m // (2 * SC_WINDOW)

    @pl.kernel(out_type=jax.ShapeDtypeStruct((m, width), table.dtype), mesh=_sc_mesh(),
               scratch_types=[], name="moe_gather")
    def gather(table_hbm, idx_hbm, out_hbm):
        def body(idx_vmem, out_vmem):
            pltpu.sync_copy(table_hbm.at[idx_vmem.at[0]], out_vmem)

        pltpu.emit_pipeline(
            body,
            grid=(2, steps),
            in_specs=[pl.BlockSpec((1, SC_WINDOW), lambda k, j: (0, k * steps + j))],
            out_specs=[pl.BlockSpec((SC_WINDOW, width), lambda k, j: (k * steps + j, 0))],
            core_axis_name=("c", "s"),
            dimension_semantics=(pltpu.PARALLEL, pltpu.PARALLEL),
        )(idx_hbm, out_hbm)

    return gather(table, idx)


def _experts_kernel(te_ref, nv_ref, blk_ref, xs_ref, wg_ref, wu_ref, wd_ref, ys_ref):
    r = pl.program_id(0)
    n_valid = nv_ref[r]

    @pl.when(n_valid > 0)
    def _():
        parts = [p.astype(BF16) for p in _unpack_rows(xs_ref[0]) + _unpack_rows(xs_ref[1])]
        cols = (0, 2 * SC_ROW, SC_ROW, 3 * SC_ROW)
        hg = sum(_dot(p, wg_ref[c:c + SC_ROW, :]) for p, c in zip(parts, cols))
        hu = sum(_dot(p, wu_ref[c:c + SC_ROW, :]) for p, c in zip(parts, cols))
        row = lax.broadcasted_iota(jnp.int32, (ROW_TILE, 1), 0)
        hid = jnp.where(row < n_valid, hg * _sigmoid(hg) * hu, 0.0).astype(BF16)
        _split_planes(_pack_rows(_dot(hid, wd_ref[...])), ys_ref)


def _experts(xs, tile_expert, n_valid, block, w_gate, w_up, w_down):
    n_tiles = xs.shape[1] // ROW_TILE
    w_spec = pl.BlockSpec((None, D_MODEL, D_EXPERT), lambda r, te, nv, blk: (te[r], 0, 0))
    grid_spec = pltpu.PrefetchScalarGridSpec(
        num_scalar_prefetch=3,
        grid=(n_tiles,),
        in_specs=[
            pl.BlockSpec((2, ROW_TILE, SC_ROW), lambda r, te, nv, blk: (0, blk[r], 0)),
            w_spec, w_spec,
            pl.BlockSpec((None, D_EXPERT, D_MODEL), lambda r, te, nv, blk: (te[r], 0, 0)),
        ],
        out_specs=pl.BlockSpec((2, ROW_TILE, SC_ROW), lambda r, te, nv, blk: (0, blk[r], 0)),
    )
    return pl.pallas_call(
        _experts_kernel,
        grid_spec=grid_spec,
        out_shape=jax.ShapeDtypeStruct((2, n_tiles * ROW_TILE, SC_ROW), U32),
        compiler_params=_params(("arbitrary",)),
        name="experts",
    )(tile_expert, n_valid, block, xs, w_gate, w_up, w_down)


def _ple_kernel(h_ref, yg_ref, meta_t_ref, p_ref, gp_ref, wg_ref, bg_ref, wp_ref, gf_ref, *rest,
                final_norm):
    o_ref, wg_s, wp_s = rest[-3:]

    @pl.when(pl.program_id(0) == 0)
    def _():
        g_col = jnp.transpose(jnp.broadcast_to(gp_ref[...], (8, D_MODEL)))[:, 0:1]
        wg_s[...] = (wg_ref[...] * g_col).astype(BF16)
        wp_s[...] = wp_ref[...].astype(BF16)

    meta = jnp.transpose(meta_t_ref[...])
    w1 = meta[:, 4:5]
    w2 = meta[:, 5:6]
    q0, q2 = (w1 * u + w2 * v for u, v in zip(_unpack_rows(yg_ref[0]), _unpack_rows(yg_ref[2])))
    q1, q3 = (w1 * u + w2 * v for u, v in zip(_unpack_rows(yg_ref[1]), _unpack_rows(yg_ref[3])))
    moe = jnp.concatenate([q0, q1, q2, q3], axis=1)
    h = h_ref[...] + moe
    gate = _sigmoid(_dot(_rms(h).astype(BF16), wg_s[...]) + bg_ref[...])
    h = h + gate * _dot(p_ref[...].astype(BF16), wp_s[...])
    o_ref[...] = _rms(h, gf_ref[...]) if final_norm else h


def _ple(h1, yg, meta_t, p2, g_ple, w_gate, b_gate, w_proj, g_final, layer, final_norm, tm, part,
         prev_out):
    t = h1.shape[0]
    steps = t // tm // GATHER_PARTS
    off = part * steps
    in_specs = [
        pl.BlockSpec((tm, D_MODEL), lambda i: (i + off, 0)),
        pl.BlockSpec((4, tm, SC_ROW), lambda i: (0, i, 0)),
        pl.BlockSpec((META_ROWS, tm), lambda i: (0, i + off)),
        pl.BlockSpec((tm, D_PLE), lambda i: (i + off, 0)),
        pl.BlockSpec((1, D_MODEL), lambda i: (0, 0)),
        pl.BlockSpec((D_MODEL, D_MODEL), lambda i: (layer, 0), pipeline_mode=pl.Buffered(1)),
        pl.BlockSpec((1, D_MODEL), lambda i: (0, 0)),
        pl.BlockSpec((D_PLE, D_MODEL), lambda i: (layer, 0), pipeline_mode=pl.Buffered(1)),
        pl.BlockSpec((1, D_MODEL), lambda i: (0, 0)),
    ]
    args = [h1, yg, meta_t, p2, g_ple, w_gate, b_gate, w_proj, g_final]
    aliases = {}
    if prev_out is not None:
        in_specs.append(pl.BlockSpec(memory_space=pl.ANY))
        args.append(prev_out)
        aliases = {len(args) - 1: 0}
    return pl.pallas_call(
        functools.partial(_ple_kernel, final_norm=final_norm),
        grid=(steps,),
        in_specs=in_specs,
        out_specs=pl.BlockSpec((tm, D_MODEL), lambda i: (i + off, 0)),
        out_shape=jax.ShapeDtypeStruct((t, D_MODEL), F32),
        input_output_aliases=aliases,
        scratch_shapes=[pltpu.VMEM((D_MODEL, D_MODEL), BF16), pltpu.VMEM((D_PLE, D_MODEL), BF16)],
        compiler_params=_params(("arbitrary",)),
        name="ple",
    )(*args)


def kernel(x, p, g_mix, w_in, pool_w, pool_scale, ssm_a_re, ssm_a_im, ssm_log_dt, ssm_b_re,
           ssm_b_im, ssm_c_re, ssm_c_im, ssm_d, glu_w, glu_b, w_out, g_ffn, router_grp_w,
           router_grp_b, router_exp_w, router_exp_b, exp_w_gate, exp_w_up, exp_w_down, g_ple,
           ple_gate_w, ple_gate_b, ple_proj_w, g_final):
    bsz, seq, dm = x.shape
    depth = g_mix.shape[0]
    t = bsz * seq
    seq_chunks = seq // CHUNK
    nch = t // CHUNK
    tm = 1024
    n_sorted = 2 * t + N_EXPERTS * ROW_TILE
    w_gate_all = exp_w_gate.reshape(depth * N_EXPERTS, dm, D_EXPERT)
    w_up_all = exp_w_up.reshape(depth * N_EXPERTS, dm, D_EXPERT)
    w_down_all = exp_w_down.reshape(depth * N_EXPERTS, D_EXPERT, dm)

    h = x.reshape(t, dm)
    for i in range(depth):
        w_in_b = (g_mix[i][:, None] * w_in[i]).astype(BF16)
        zp, ut = _in_proj(h.reshape(nch, CHUNK, dm), w_in_b[:, :D_POOL],
                          jnp.transpose(w_in_b[:, D_POOL:]))
        a = _pool(zp.reshape(bsz, seq, D_POOL), pool_w[i], pool_scale[i][None])
        yt, w_gate_b, w_up_b, w_down_b = _ssm(
            ut, ssm_a_re[i], ssm_a_im[i], ssm_log_dt[i], ssm_b_re[i], ssm_b_im[i], ssm_c_re[i],
            ssm_c_im[i], ssm_d[i], seq_chunks, w_gate_all, w_up_all, w_down_all, i)
        h = _mix_out(h.reshape(nch, CHUNK, dm), a.reshape(nch, CHUNK, D_POOL), yt,
                     jnp.transpose(glu_w[i]).astype(BF16), glu_b[i][:, None],
                     w_out.reshape(depth * dm, dm), i).reshape(t, dm)

        eg = EXPERTS_PER_GROUP
        w_router = jnp.concatenate(
            [router_grp_w[i], jnp.zeros((dm, eg - N_EXPERT_GROUPS), F32),
             jnp.transpose(router_exp_w[i], (1, 0, 2)).reshape(dm, N_EXPERTS),
             jnp.zeros((dm, ROUTER_W - eg - N_EXPERTS), F32)], axis=1)
        b_router = jnp.concatenate(
            [router_grp_b[i], jnp.zeros((eg - N_EXPERT_GROUPS,), F32),
             router_exp_b[i].reshape(N_EXPERTS),
             jnp.zeros((ROUTER_W - eg - N_EXPERTS,), F32)])[None]
        vp, meta_t, counts = _router(h, g_ffn[i][None], w_router, b_router, tm)
        idx, tile_expert, n_valid, block = _plan(meta_t, counts, n_sorted // ROW_TILE)
        xs = _sc_scatter_rows(vp.reshape(2 * t, SC_ROW), idx, 2 * n_sorted)
        ys = _experts(xs.reshape(2, n_sorted, SC_ROW), tile_expert, n_valid, block,
                      w_gate_b, w_up_b, w_down_b)
        ys2 = ys.reshape(2 * n_sorted, SC_ROW)
        idx4 = idx.reshape(4, t)
        tp = t // GATHER_PARTS
        out = None
        for q in range(GATHER_PARTS):
            yg_q = _sc_gather_rows(ys2, idx4[:, q * tp:(q + 1) * tp].reshape(1, 4 * tp))
            out = _ple(h, yg_q.reshape(4, tp, SC_ROW), meta_t, p[i].reshape(t, D_PLE), g_ple[i][None],
                       ple_gate_w.reshape(depth * dm, dm), ple_gate_b[i][None],
                       ple_proj_w.reshape(depth * D_PLE, dm), g_final[None], i, i == depth - 1, tm, q, out)
        h = out
    return h.reshape(bsz, seq, dm)
```

```python
import functools
import math

import numpy as np
import jax
import jax.numpy as jnp
from jax import lax
from jax.experimental import pallas as pl
from jax.experimental.pallas import tpu as pltpu
from jax.experimental.pallas import tpu_sc as plsc

F32 = jnp.float32
BF16 = jnp.bfloat16
U32 = jnp.uint32

D_MODEL = 1024
D_POOL = 512
D_SSM = 512
POOL_WINDOWS = (2, 4, 8, 16)
POOL_GROUP = 128
SSM_GROUP = 16
N_SSM_GROUPS = 32
SSM_STATE = 64
N_EXPERT_GROUPS = 4
EXPERTS_PER_GROUP = 8
N_EXPERTS = N_EXPERT_GROUPS * EXPERTS_PER_GROUP
D_EXPERT = 256
D_PLE = 256
RMS_EPS = 1e-6

LANES = 128
CHUNK = 32
CHUNK_W = CHUNK * SSM_GROUP
T_SUB = 8
S_TILE = 8
C_TILE = 128
ROUTER_W = LANES
HALF = D_MODEL // 2
ROW_TILE = 1024
SC_WINDOW = 128
SC_ROW = HALF // 2
META_ROWS = 8
GATHER_PARTS = 2
VMEM_LIMIT = 56 * 1024 * 1024


def _dot(a, b):
    return jnp.dot(a, b, preferred_element_type=F32)


def _dot_nt(a, b):
    return lax.dot_general(a, b, (((1,), (1,)), ((), ())), preferred_element_type=F32)


def _dot_tn(a, b):
    return lax.dot_general(a, b, (((0,), (0,)), ((), ())), preferred_element_type=F32)


def _rms(x, g=None):
    y = x * lax.rsqrt(jnp.mean(x * x, axis=-1, keepdims=True) + RMS_EPS)
    return y if g is None else y * g


def _sigmoid(x):
    return 1.0 / (1.0 + jnp.exp(-x))


def _params(sem):
    return pltpu.CompilerParams(dimension_semantics=sem, vmem_limit_bytes=VMEM_LIMIT)


def _tile_step():
    n_s = pl.num_programs(1)
    return pl.program_id(0) * n_s + pl.program_id(1), pl.num_programs(0) * n_s


def _row_copies(hbm, buf, sem, step_idx, slot_idx, to_hbm):
    n_s = pl.num_programs(1)
    c0 = (step_idx // n_s) * C_TILE
    s0 = (step_idx % n_s) * S_TILE
    out = []
    for j in range(S_TILE):
        far, near = hbm.at[pl.ds(c0, C_TILE), s0 + j, :], buf.at[slot_idx, j]
        src, dst = (near, far) if to_hbm else (far, near)
        out.append(pltpu.make_async_copy(src, dst, sem.at[slot_idx]))
    return out


def _load_time_major_rows(hbm, buf, sem):
    step, n_steps = _tile_step()
    slot = step % 2

    @pl.when(step == 0)
    def _():
        for cp in _row_copies(hbm, buf, sem, step, slot, False):
            cp.start()

    @pl.when(step + 1 < n_steps)
    def _():
        for cp in _row_copies(hbm, buf, sem, step + 1, 1 - slot, False):
            cp.start()

    for cp in _row_copies(hbm, buf, sem, step, slot, False):
        cp.wait()
    return buf[slot].reshape(S_TILE * C_TILE, buf.shape[-1])


def _store_time_major_rows(val, hbm, buf, sem):
    step, n_steps = _tile_step()
    slot = step % 2

    @pl.when(step >= 2)
    def _():
        for cp in _row_copies(hbm, buf, sem, step - 2, slot, True):
            cp.wait()

    buf[slot] = val.reshape(S_TILE, C_TILE, val.shape[-1])
    for cp in _row_copies(hbm, buf, sem, step, slot, True):
        cp.start()

    @pl.when(step == n_steps - 1)
    def _():
        @pl.when(step >= 1)
        def _():
            for cp in _row_copies(hbm, buf, sem, step - 1, 1 - slot, True):
                cp.wait()
        for cp in _row_copies(hbm, buf, sem, step, slot, True):
            cp.wait()


def _in_proj_kernel(x_hbm, wp_ref, wst_ref, zp_hbm, ut_ref, xbuf, zbuf, xsem, zsem):
    nc = C_TILE
    u = _rms(_load_time_major_rows(x_hbm, xbuf, xsem)).astype(BF16)
    zt = _dot_nt(wst_ref[...], u).astype(BF16)
    for j in range(S_TILE):
        ut_ref[:, j, :, :] = zt[:, j * nc:(j + 1) * nc].reshape(N_SSM_GROUPS, SSM_GROUP, nc)
    _store_time_major_rows(_dot(u, wp_ref[...]), zp_hbm, zbuf, zsem)


def _in_proj(x3, w_pool, w_ssm_t):
    nch = x3.shape[0]
    return pl.pallas_call(
        _in_proj_kernel,
        grid=(nch // C_TILE, CHUNK // S_TILE),
        in_specs=[
            pl.BlockSpec(memory_space=pl.ANY),
            pl.BlockSpec((D_MODEL, D_POOL), lambda c, s: (0, 0)),
            pl.BlockSpec((D_SSM, D_MODEL), lambda c, s: (0, 0)),
        ],
        out_specs=[
            pl.BlockSpec(memory_space=pl.ANY),
            pl.BlockSpec((N_SSM_GROUPS, None, S_TILE, SSM_GROUP, C_TILE), lambda c, s: (0, c, s, 0, 0)),
        ],
        out_shape=[
            jax.ShapeDtypeStruct((nch, CHUNK, D_POOL), F32),
            jax.ShapeDtypeStruct((N_SSM_GROUPS, nch // C_TILE, CHUNK, SSM_GROUP, C_TILE), BF16),
        ],
        scratch_shapes=[
            pltpu.VMEM((2, S_TILE, C_TILE, D_MODEL), F32),
            pltpu.VMEM((2, S_TILE, C_TILE, D_POOL), F32),
            pltpu.SemaphoreType.DMA((2,)),
            pltpu.SemaphoreType.DMA((2,)),
        ],
        compiler_params=_params(("arbitrary", "arbitrary")),
        name="in_proj",
    )(x3, w_pool, w_ssm_t)


def _shift_rows(x, d, row, n):
    if d == 0:
        return x
    r = pltpu.roll(x, d % n, 0)
    if d > 0:
        return jnp.where(row >= d, r, 0.0)
    return jnp.where(row < n + d, r, 0.0)


def _pool_kernel(z_ref, w_ref, sc_ref, o_ref):
    n = z_ref.shape[0]
    gi = pl.program_id(1)
    row = lax.broadcasted_iota(jnp.int32, (n, 1), 0)

    for k, w in enumerate(POOL_WINDOWS):
        @pl.when(gi == k)
        def _(w=w):
            x = z_ref[...]
            half = w // 2
            pd, pu, span = x, x, 1
            while span < half:
                pd = pd + _shift_rows(pd, span, row, n)
                pu = pu + _shift_rows(pu, -span, row, n)
                span *= 2
            total = _shift_rows(pd, 1, row, n) + pu
            lo = jnp.maximum(row - half, 0)
            hi = jnp.minimum(row + half, n)
            cnt = (hi - lo).astype(F32)
            diff = (total / cnt - x).astype(BF16)
            o_ref[...] = _dot(diff, w_ref[...].astype(BF16)) * sc_ref[...]


def _pool(zp3, pool_w, pool_scale):
    b, s, _ = zp3.shape
    return pl.pallas_call(
        _pool_kernel,
        grid=(b, len(POOL_WINDOWS)),
        in_specs=[
            pl.BlockSpec((None, s, POOL_GROUP), lambda i, g: (i, 0, g)),
            pl.BlockSpec((None, POOL_GROUP, POOL_GROUP), lambda i, g: (g, 0, 0)),
            pl.BlockSpec((1, POOL_GROUP), lambda i, g: (0, g)),
        ],
        out_specs=pl.BlockSpec((None, s, POOL_GROUP), lambda i, g: (i, 0, g)),
        out_shape=jax.ShapeDtypeStruct((b, s, D_POOL), F32),
        compiler_params=_params(("parallel", "parallel")),
        name="pool",
    )(zp3, pool_w, pool_scale)


def _expand_consts():
    time = np.arange(CHUNK_W) // SSM_GROUP
    def onehot(e):
        m = np.zeros((CHUNK_W, LANES), np.float32)
        m[np.arange(CHUNK_W), e] = 1.0
        return m
    return np.stack([
        onehot(CHUNK - 1 - time),
        onehot(time),
        onehot(time + 1),
        onehot(CHUNK - time),
    ])


def _cmul_packed(x, p, q):
    return x * p + pltpu.roll(x, LANES // 2, 1) * q


def _ssm_kernel(u_ref, vec_ref, mat_ref, exp_ref, wg_ref, wu_ref, wd_ref, y_ref, og_ref, ou_ref,
                od_ref, *, seq_chunks):
    og_ref[...] = wg_ref[...].astype(BF16)
    ou_ref[...] = wu_ref[...].astype(BF16)
    od_ref[...] = wd_ref[...].astype(BF16)

    n_ct = u_ref.shape[0]
    nch = n_ct * C_TILE
    half = LANES // 2
    lane = lax.broadcasted_iota(jnp.int32, (1, LANES), 1)
    lo_half = lane < half

    def direction(di):
        a_re = vec_ref[di, 0:1]
        a_im = vec_ref[di, 1:2]
        dt = jnp.exp(vec_ref[di, 2:3])
        mag = jnp.exp(a_re * dt)
        ang = a_im * dt
        lam = jnp.where(lo_half, mag * jnp.cos(ang), mag * jnp.sin(ang))
        lb_re = mag * jnp.cos(ang)
        lb_im = mag * jnp.sin(ang)
        den = a_re * a_re + a_im * a_im
        f_re = ((lb_re - 1.0) * a_re + lb_im * a_im) / den
        f_im = (lb_im * a_re - (lb_re - 1.0) * a_im) / den
        return lam, f_re, f_im

    def power_table(lam):
        e = lax.broadcasted_iota(jnp.int32, (LANES, 1), 0)
        tab = jnp.where(lo_half, 1.0, 0.0) * jnp.ones((LANES, 1), F32)
        sq = lam
        for k in range(7):
            p = jnp.where(lo_half, sq, pltpu.roll(sq, half, 1))
            q = jnp.where(lo_half, -pltpu.roll(sq, half, 1), sq)
            tab = jnp.where(((e >> k) & 1) == 1, _cmul_packed(tab, p, q), tab)
            sq = _cmul_packed(sq, p, q)
        return tab

    def tile_rows(x16):
        return jnp.broadcast_to(x16[None], (CHUNK, SSM_GROUP, LANES)).reshape(CHUNK_W, LANES)

    def expanded(tab, which, v_re, v_im, conj_sign):
        lexp = _dot(exp_ref[which], tab.astype(BF16))
        if conj_sign > 0:
            p = jnp.where(lo_half, v_re, v_re)
            q = jnp.where(lo_half, -v_im, v_im)
        else:
            p = jnp.where(lo_half, v_re, -v_re)
            q = jnp.where(lo_half, -v_im, -v_im)
        return lexp * tile_rows(p) + pltpu.roll(lexp, half, 1) * tile_rows(q)

    lam_f, ff_re, ff_im = direction(0)
    lam_b, fb_re, fb_im = direction(1)
    tab_f = power_table(lam_f)
    tab_b = power_table(lam_b)

    def bbar(bt_re, bt_im, f_re, f_im):
        return bt_re * f_re - bt_im * f_im, bt_re * f_im + bt_im * f_re

    def mat(di, k):
        return mat_ref[di, k * SSM_GROUP:(k + 1) * SSM_GROUP, :]

    bf_re, bf_im = bbar(mat(0, 0), mat(0, 1), ff_re, ff_im)
    bb_re, bb_im = bbar(mat(1, 0), mat(1, 1), fb_re, fb_im)

    pb1 = expanded(tab_f, 0, bf_re, bf_im, 1)
    pb2 = expanded(tab_b, 1, bb_re, bb_im, 1)
    pb3 = expanded(tab_b, 2, bb_re, bb_im, 1)
    ft_f = expanded(tab_f, 2, mat(0, 3), mat(0, 4), -1)
    ft_b = expanded(tab_b, 3, mat(1, 3), mat(1, 4), -1)

    row_w = lax.broadcasted_iota(jnp.int32, (CHUNK_W, 1), 0)
    last_blk = row_w >= CHUNK_W - SSM_GROUP
    pb2_lag0 = jnp.where(last_blk, pltpu.roll(pb2, CHUNK_W - SSM_GROUP, 0), 0.0)
    ccr_f = mat(0, 2).astype(BF16)
    ccr_b = mat(1, 2).astype(BF16)
    r_lo = _dot_nt(ccr_f, pb1.astype(BF16)) + _dot_nt(ccr_b, pb2_lag0.astype(BF16))
    co = lax.broadcasted_iota(jnp.int32, (SSM_GROUP, CHUNK_W), 0)
    col = lax.broadcasted_iota(jnp.int32, (SSM_GROUP, CHUNK_W), 1)
    r_lo = r_lo + jnp.where(col == CHUNK_W - SSM_GROUP + co, mat(0, 5)[:, 0:1], 0.0)
    r_hi = _dot_nt(ccr_b, pb3.astype(BF16))
    r_t = jnp.concatenate([r_lo, r_hi], axis=1)
    g_t = jnp.concatenate(
        [pltpu.roll(r_t, SSM_GROUP * (tl + 1), 1) for tl in range(T_SUB)], axis=0
    ).astype(BF16)

    u = jnp.concatenate([u_ref[ct].reshape(CHUNK_W, C_TILE) for ct in range(n_ct)],
                        axis=1)
    e_mat = jnp.concatenate([pb1, pb2], axis=1).astype(BF16)
    xend = _dot_tn(e_mat, u)
    lanec = lax.broadcasted_iota(jnp.int32, (1, nch), 1) % seq_chunks
    ns = SSM_STATE

    def scan(re, im, tab, forward):
        lam_col = jnp.transpose(tab[CHUNK:CHUNK + 8, :])[:, 0:1]
        a, b = lam_col[:ns], lam_col[ns:]
        n_steps = int(math.log2(seq_chunks))

        def shifted(v, d):
            if forward:
                return jnp.where(lanec >= d, pltpu.roll(v, d, 1), 0.0)
            return jnp.where(lanec < seq_chunks - d, pltpu.roll(v, nch - d, 1), 0.0)

        for k in range(n_steps):
            sr, si = shifted(re, 1 << k), shifted(im, 1 << k)
            re, im = re + (sr * a - si * b), im + (sr * b + si * a)
            a, b = a * a - b * b, 2.0 * a * b
        return shifted(re, 1), shifted(im, 1)

    f_re, f_im = scan(xend[:ns], xend[ns:2 * ns], tab_f, True)
    b_re, b_im = scan(xend[2 * ns:3 * ns], xend[3 * ns:], tab_b, False)
    xin = jnp.concatenate([f_re, f_im, b_re, b_im], axis=0).astype(BF16)
    f_t = jnp.concatenate([ft_f, ft_b], axis=1).astype(BF16)

    toeplitz = jnp.concatenate(
        [g_t[:, CHUNK_W - LANES * th:2 * CHUNK_W - LANES * th] for th in range(CHUNK // T_SUB)],
        axis=0)
    y_t = _dot(toeplitz, u) + _dot(f_t, xin)
    for ct in range(n_ct):
        y_ref[ct] = y_t[:, ct * C_TILE:(ct + 1) * C_TILE].reshape(
            CHUNK, SSM_GROUP, C_TILE).astype(y_ref.dtype)


def _ssm(ut, a_re, a_im, log_dt, b_re, b_im, c_re, c_im, d, seq_chunks, w_gate, w_up, w_down, layer):
    g, n_ct = ut.shape[:2]
    assert N_EXPERTS % g == 0
    epg = N_EXPERTS // g
    base = layer * g
    n = SSM_STATE

    def per_group(a):
        return jnp.swapaxes(a, 0, 1)

    def dup(a):
        return jnp.concatenate([a, a], axis=-1)

    vecs = dup(jnp.stack([per_group(a_re), per_group(a_im),
                          jnp.broadcast_to(per_group(log_dt)[..., None], (g, 2, n))], axis=2))
    cr, ci = per_group(c_re), per_group(c_im)
    d_blk = jnp.broadcast_to(d.reshape(g, 1, SSM_GROUP, 1), (g, 2, SSM_GROUP, LANES))
    mats = jnp.concatenate(
        [dup(jnp.swapaxes(per_group(b_re), 2, 3)), dup(jnp.swapaxes(per_group(b_im), 2, 3)),
         jnp.concatenate([cr, -ci], axis=-1), dup(cr), dup(ci), d_blk], axis=2)
    exp_c = jnp.asarray(_expand_consts(), BF16)

    return pl.pallas_call(
        functools.partial(_ssm_kernel, seq_chunks=seq_chunks),
        grid=(g,),
        in_specs=[
            pl.BlockSpec((None, n_ct, CHUNK, SSM_GROUP, C_TILE), lambda i: (i, 0, 0, 0, 0)),
            pl.BlockSpec((None, 2, 3, LANES), lambda i: (i, 0, 0, 0)),
            pl.BlockSpec((None, 2, 6 * SSM_GROUP, LANES), lambda i: (i, 0, 0, 0)),
            pl.BlockSpec((4, CHUNK_W, LANES), lambda i: (0, 0, 0)),
            pl.BlockSpec((epg, D_MODEL, D_EXPERT), lambda i: (base + i, 0, 0)),
            pl.BlockSpec((epg, D_MODEL, D_EXPERT), lambda i: (base + i, 0, 0)),
            pl.BlockSpec((epg, D_EXPERT, D_MODEL), lambda i: (base + i, 0, 0)),
        ],
        out_specs=[
            pl.BlockSpec((None, n_ct, CHUNK, SSM_GROUP, C_TILE), lambda i: (i, 0, 0, 0, 0)),
            pl.BlockSpec((epg, D_MODEL, D_EXPERT), lambda i: (i, 0, 0)),
            pl.BlockSpec((epg, D_MODEL, D_EXPERT), lambda i: (i, 0, 0)),
            pl.BlockSpec((epg, D_EXPERT, D_MODEL), lambda i: (i, 0, 0)),
        ],
        out_shape=[
            jax.ShapeDtypeStruct(ut.shape, BF16),
            jax.ShapeDtypeStruct((N_EXPERTS, D_MODEL, D_EXPERT), BF16),
            jax.ShapeDtypeStruct((N_EXPERTS, D_MODEL, D_EXPERT), BF16),
            jax.ShapeDtypeStruct((N_EXPERTS, D_EXPERT, D_MODEL), BF16),
        ],
        compiler_params=_params(("parallel",)),
        name="ssm",
    )(ut, vecs, mats, exp_c, w_gate, w_up, w_down)


def _mix_out_kernel(x_hbm, a_hbm, yt_ref, gwt_ref, gb_ref, wo_ref, h_hbm, xbuf, abuf, hbuf,
                    xsem, asem, hsem):
    nc = C_TILE
    x = _load_time_major_rows(x_hbm, xbuf, xsem)
    a = _load_time_major_rows(a_hbm, abuf, asem).astype(BF16)
    y = jnp.concatenate([yt_ref[:, j, :, :].reshape(D_SSM, nc) for j in range(S_TILE)],
                        axis=1).astype(F32)
    z = 0.5 * y * (1.0 + jnp.tanh(math.sqrt(2.0 / math.pi) * (y + 0.044715 * (y * y * y))))
    gate = _sigmoid(_dot(gwt_ref[...], z.astype(BF16)) + gb_ref[...])
    s = (z * gate).astype(BF16)
    h = x + _dot(a, wo_ref[:D_POOL, :]) + _dot_tn(s, wo_ref[D_POOL:, :])
    _store_time_major_rows(h, h_hbm, hbuf, hsem)


def _mix_out(x3, a3, yt, glu_w_t, glu_b_col, w_out):
    nch = x3.shape[0]
    return pl.pallas_call(
        _mix_out_kernel,
        grid=(nch // C_TILE, CHUNK // S_TILE),
        in_specs=[
            pl.BlockSpec(memory_space=pl.ANY),
            pl.BlockSpec(memory_space=pl.ANY),
            pl.BlockSpec((N_SSM_GROUPS, None, S_TILE, SSM_GROUP, C_TILE), lambda c, t: (0, c, t, 0, 0)),
            pl.BlockSpec((D_SSM, D_SSM), lambda c, t: (0, 0)),
            pl.BlockSpec((D_SSM, 1), lambda c, t: (0, 0)),
            pl.BlockSpec((D_MODEL, D_MODEL), lambda c, t: (0, 0)),
        ],
        out_specs=pl.BlockSpec(memory_space=pl.ANY),
        out_shape=jax.ShapeDtypeStruct((nch, CHUNK, D_MODEL), F32),
        scratch_shapes=[
            pltpu.VMEM((2, S_TILE, C_TILE, D_MODEL), F32),
            pltpu.VMEM((2, S_TILE, C_TILE, D_POOL), F32),
            pltpu.VMEM((2, S_TILE, C_TILE, D_MODEL), F32),
            pltpu.SemaphoreType.DMA((2,)),
            pltpu.SemaphoreType.DMA((2,)),
            pltpu.SemaphoreType.DMA((2,)),
        ],
        compiler_params=_params(("arbitrary", "arbitrary")),
        name="mix_out",
    )(x3, a3, yt, glu_w_t, glu_b_col, w_out)


def _pack_rows(x):
    b = lax.bitcast_convert_type(x.astype(BF16).astype(F32), U32)
    return (b[:, :HALF] & jnp.uint32(0xFFFF0000)) | (b[:, HALF:] >> 16)


def _unpack_rows(w):
    lo = lax.bitcast_convert_type(w & jnp.uint32(0xFFFF0000), F32)
    hi = lax.bitcast_convert_type(w << 16, F32)
    return lo, hi


def _split_bf16(x):
    hi = x.astype(BF16)
    return hi, (x - hi.astype(F32)).astype(BF16)


def _route(v32, wr_ref, br_ref):
    v_hi, v_lo = _split_bf16(v32)
    w_hi, w_lo = _split_bf16(wr_ref[...])
    both = _dot(v_hi, jnp.concatenate([w_hi, w_lo], axis=1))
    logits = both[:, :ROUTER_W] + (both[:, ROUTER_W:] + _dot(v_lo, w_hi)) + br_ref[...]
    return jnp.transpose(logits)


def _top1(x, valid=None):
    n = x.shape[0]
    row = lax.broadcasted_iota(jnp.int32, x.shape, 0).astype(F32)
    if valid is not None:
        x = jnp.where(valid, x, -jnp.inf)
    m = jnp.max(x, axis=0, keepdims=True)
    idx = jnp.min(jnp.where(x == m, row, float(n)), axis=0, keepdims=True)
    return m, idx, x, row


def _split_planes(packed, ref):
    ref[0] = packed[:, :SC_ROW]
    ref[1] = packed[:, SC_ROW:]


def _router_kernel(h_ref, g_ref, wr_ref, br_ref, before_ref, vp_ref, meta_t_ref, cnt_ref, carry_ref):
    @pl.when(pl.program_id(0) == 0)
    def _():
        carry_ref[...] = jnp.zeros_like(carry_ref)

    v32 = _rms(h_ref[...], g_ref[...])
    _split_planes(_pack_rows(v32), vp_ref)
    lt = _route(v32, wr_ref, br_ref)
    tm = lt.shape[1]
    eg = EXPERTS_PER_GROUP

    grp = lt[:eg]
    grp_row = lax.broadcasted_iota(jnp.int32, grp.shape, 0)
    mg, grp_idx, grp, _ = _top1(grp, grp_row < N_EXPERT_GROUPS)
    grp_p = 1.0 / jnp.sum(jnp.exp(grp - mg), axis=0, keepdims=True)
    le = jnp.zeros((eg, tm), F32)
    for g in range(N_EXPERT_GROUPS):
        le = jnp.where(grp_idx == float(g), lt[eg * (g + 1):eg * (g + 2)], le)
    m1, i1, le, row = _top1(le)
    z = jnp.sum(jnp.exp(le - m1), axis=0, keepdims=True)
    m2, i2, _, _ = _top1(jnp.where(row == i1, -jnp.inf, le))
    p1 = 1.0 / z
    p2 = jnp.exp(m2 - m1) / z
    tot = p1 + p2
    w1 = grp_p * (p1 / tot)
    w2 = grp_p * (p2 / tot)
    e1 = grp_idx * eg + i1
    e2 = grp_idx * eg + i2

    erow = lax.broadcasted_iota(jnp.int32, (N_EXPERTS, tm), 0).astype(F32)
    onehot = jnp.where(erow == e1, 1.0, jnp.where(erow == e2, 1.0, 0.0))
    before = _dot(onehot.astype(BF16), before_ref[...]) + carry_ref[...]
    rank1 = jnp.sum(jnp.where(erow == e1, before, 0.0), axis=0, keepdims=True)
    rank2 = jnp.sum(jnp.where(erow == e2, before, 0.0), axis=0, keepdims=True)
    carry = carry_ref[...] + jnp.sum(onehot, axis=1, keepdims=True)
    carry_ref[...] = carry
    cnt_ref[...] = carry

    mrow = lax.broadcasted_iota(jnp.int32, (META_ROWS, tm), 0)
    meta_t_ref[...] = jnp.where(mrow == 0, e1, jnp.where(mrow == 1, e2, jnp.where(
        mrow == 2, rank1, jnp.where(mrow == 3, rank2, jnp.where(
            mrow == 4, w1, jnp.where(mrow == 5, w2, 0.0))))))


def _earlier_matrix(tm):
    return np.triu(np.ones((tm, tm), np.float32), k=1)


def _router(h1, g_ffn, w_router, b_router, tm):
    t = h1.shape[0]
    return pl.pallas_call(
        _router_kernel,
        grid=(t // tm,),
        in_specs=[
            pl.BlockSpec((tm, D_MODEL), lambda i: (i, 0)),
            pl.BlockSpec((1, D_MODEL), lambda i: (0, 0)),
            pl.BlockSpec((D_MODEL, ROUTER_W), lambda i: (0, 0)),
            pl.BlockSpec((1, ROUTER_W), lambda i: (0, 0)),
            pl.BlockSpec((tm, tm), lambda i: (0, 0)),
        ],
        out_specs=[
            pl.BlockSpec((2, tm, SC_ROW), lambda i: (0, i, 0)),
            pl.BlockSpec((META_ROWS, tm), lambda i: (0, i)),
            pl.BlockSpec((N_EXPERTS, 1), lambda i: (0, 0)),
        ],
        out_shape=[
            jax.ShapeDtypeStruct((2, t, SC_ROW), U32),
            jax.ShapeDtypeStruct((META_ROWS, t), F32),
            jax.ShapeDtypeStruct((N_EXPERTS, 1), F32),
        ],
        scratch_shapes=[pltpu.VMEM((N_EXPERTS, 1), F32)],
        compiler_params=_params(("arbitrary",)),
        name="router",
    )(h1, g_ffn, w_router, b_router, jnp.asarray(_earlier_matrix(tm), BF16))


def _plan(meta_t, counts, n_tiles):
    e1 = meta_t[0].astype(jnp.int32)
    e2 = meta_t[1].astype(jnp.int32)
    rank1 = meta_t[2].astype(jnp.int32)
    rank2 = meta_t[3].astype(jnp.int32)
    cnt = counts[:, 0].astype(jnp.int32)
    padded = ((cnt + ROW_TILE - 1) // ROW_TILE) * ROW_TILE
    ends = jnp.cumsum(padded)
    starts = ends - padded
    experts = jnp.arange(N_EXPERTS, dtype=jnp.int32)
    pos1 = rank1 + jnp.sum(jnp.where(e1[None, :] == experts[:, None], starts[:, None], 0), axis=0)
    pos2 = rank2 + jnp.sum(jnp.where(e2[None, :] == experts[:, None], starts[:, None], 0), axis=0)
    tile_start = jnp.arange(n_tiles, dtype=jnp.int32) * ROW_TILE
    tile_expert = jnp.sum((tile_start[:, None] >= ends[None, :]).astype(jnp.int32), axis=1)
    tile_expert = jnp.minimum(tile_expert, N_EXPERTS - 1)
    rows_left = jnp.sum(jnp.where(tile_expert[:, None] == experts, cnt + starts, 0), axis=1) - tile_start
    n_valid = jnp.clip(rows_left, 0, ROW_TILE).astype(jnp.int32)
    last_used = jnp.maximum(ends[-1] // ROW_TILE - 1, 0)
    block = jnp.minimum(jnp.arange(n_tiles, dtype=jnp.int32), last_used)
    tile_expert = jnp.sum(jnp.where(block[:, None] == jnp.arange(n_tiles)[None, :],
                                    tile_expert[None, :], 0), axis=1)
    plane = n_tiles * ROW_TILE
    half_rows = jnp.concatenate([pos1, pos1 + plane, pos2, pos2 + plane])[None]
    return half_rows, tile_expert, n_valid, block


def _sc_mesh():
    return plsc.VectorSubcoreMesh(core_axis_name="c", subcore_axis_name="s")


def _sc_scatter_rows(rows, idx, n_out):
    t, width = rows.shape
    steps = t // SC_WINDOW
    half = steps // 2

    @pl.kernel(out_type=jax.ShapeDtypeStruct((n_out, width), rows.dtype), mesh=_sc_mesh(),
               scratch_types=[], name="moe_scatter")
    def scatter(rows_hbm, idx_hbm, out_hbm):
        def body(rows_vmem, idx0_vmem, idx1_vmem):
            pltpu.sync_copy(rows_vmem, out_hbm.at[idx0_vmem.at[0]])
            pltpu.sync_copy(rows_vmem, out_hbm.at[idx1_vmem.at[0]])

        pltpu.emit_pipeline(
            body,
            grid=(2, half),
            in_specs=[pl.BlockSpec((SC_WINDOW, width), lambda c, j: (c * half + j, 0)),
                      pl.BlockSpec((1, SC_WINDOW), lambda c, j: (0, c * half + j)),
                      pl.BlockSpec((1, SC_WINDOW), lambda c, j: (0, steps + c * half + j))],
            out_specs=[],
            core_axis_name=("c", "s"),
            dimension_semantics=(pltpu.PARALLEL, pltpu.PARALLEL),
        )(rows_hbm, idx_hbm, idx_hbm)

    return scatter(rows, idx)


def _sc_gather_rows(table, idx):
    m = idx.shape[1]
    width = table.shape[1]
    steps = m // (2 * SC_WINDOW)

    @pl.kernel(out_type=jax.ShapeDtypeStruct((m, width), table.dtype), mesh=_sc_mesh(),
               scratch_types=[], name="moe_gather")
    def gather(table_hbm, idx_hbm, out_hbm):
        def body(idx_vmem, out_vmem):
            pltpu.sync_copy(table_hbm.at[idx_vmem.at[0]], out_vmem)

        pltpu.emit_pipeline(
            body,
            grid=(2, steps),
            in_specs=[pl.BlockSpec((1, SC_WINDOW), lambda k, j: (0, k * steps + j))],
            out_specs=[pl.BlockSpec((SC_WINDOW, width), lambda k, j: (k * steps + j, 0))],
            core_axis_name=("c", "s"),
            dimension_semantics=(pltpu.PARALLEL, pltpu.PARALLEL),
        )(idx_hbm, out_hbm)

    return gather(table, idx)


def _experts_kernel(te_ref, nv_ref, blk_ref, xs_ref, wg_ref, wu_ref, wd_ref, ys_ref):
    r = pl.program_id(0)
    n_valid = nv_ref[r]

    @pl.when(n_valid > 0)
    def _():
        parts = [p.astype(BF16) for p in _unpack_rows(xs_ref[0]) + _unpack_rows(xs_ref[1])]
        cols = (0, 2 * SC_ROW, SC_ROW, 3 * SC_ROW)
        hg = sum(_dot(p, wg_ref[c:c + SC_ROW, :]) for p, c in zip(parts, cols))
        hu = sum(_dot(p, wu_ref[c:c + SC_ROW, :]) for p, c in zip(parts, cols))
        row = lax.broadcasted_iota(jnp.int32, (ROW_TILE, 1), 0)
        hid = jnp.where(row < n_valid, hg * _sigmoid(hg) * hu, 0.0).astype(BF16)
        _split_planes(_pack_rows(_dot(hid, wd_ref[...])), ys_ref)


def _experts(xs, tile_expert, n_valid, block, w_gate, w_up, w_down):
    n_tiles = xs.shape[1] // ROW_TILE
    w_spec = pl.BlockSpec((None, D_MODEL, D_EXPERT), lambda r, te, nv, blk: (te[r], 0, 0))
    grid_spec = pltpu.PrefetchScalarGridSpec(
        num_scalar_prefetch=3,
        grid=(n_tiles,),
        in_specs=[
            pl.BlockSpec((2, ROW_TILE, SC_ROW), lambda r, te, nv, blk: (0, blk[r], 0)),
            w_spec, w_spec,
            pl.BlockSpec((None, D_EXPERT, D_MODEL), lambda r, te, nv, blk: (te[r], 0, 0)),
        ],
        out_specs=pl.BlockSpec((2, ROW_TILE, SC_ROW), lambda r, te, nv, blk: (0, blk[r], 0)),
    )
    return pl.pallas_call(
        _experts_kernel,
        grid_spec=grid_spec,
        out_shape=jax.ShapeDtypeStruct((2, n_tiles * ROW_TILE, SC_ROW), U32),
        compiler_params=_params(("arbitrary",)),
        name="experts",
    )(tile_expert, n_valid, block, xs, w_gate, w_up, w_down)


def _ple_kernel(h_ref, yg_ref, meta_t_ref, p_ref, wg_ref, bg_ref, wp_ref, gf_ref, *rest, final_norm):
    o_ref = rest[-1]
    meta = jnp.transpose(meta_t_ref[...])
    w1 = meta[:, 4:5]
    w2 = meta[:, 5:6]
    q0, q2 = (w1 * u + w2 * v for u, v in zip(_unpack_rows(yg_ref[0]), _unpack_rows(yg_ref[2])))
    q1, q3 = (w1 * u + w2 * v for u, v in zip(_unpack_rows(yg_ref[1]), _unpack_rows(yg_ref[3])))
    moe = jnp.concatenate([q0, q1, q2, q3], axis=1)
    h = h_ref[...] + moe
    gate = _sigmoid(_dot(_rms(h).astype(BF16), wg_ref[...]) + bg_ref[...])
    h = h + gate * _dot(p_ref[...].astype(BF16), wp_ref[...])
    o_ref[...] = _rms(h, gf_ref[...]) if final_norm else h


def _ple(h1, yg, meta_t, p2, w_gate, b_gate, w_proj, g_final, final_norm, tm, part, prev_out):
    t = h1.shape[0]
    steps = t // tm // GATHER_PARTS
    off = part * steps
    in_specs = [
        pl.BlockSpec((tm, D_MODEL), lambda i: (i + off, 0)),
        pl.BlockSpec((4, tm, SC_ROW), lambda i: (0, i, 0)),
        pl.BlockSpec((META_ROWS, tm), lambda i: (0, i + off)),
        pl.BlockSpec((tm, D_PLE), lambda i: (i + off, 0)),
        pl.BlockSpec((D_MODEL, D_MODEL), lambda i: (0, 0)),
        pl.BlockSpec((1, D_MODEL), lambda i: (0, 0)),
        pl.BlockSpec((D_PLE, D_MODEL), lambda i: (0, 0)),
        pl.BlockSpec((1, D_MODEL), lambda i: (0, 0)),
    ]
    args = [h1, yg, meta_t, p2, w_gate, b_gate, w_proj, g_final]
    aliases = {}
    if prev_out is not None:
        in_specs.append(pl.BlockSpec(memory_space=pl.ANY))
        args.append(prev_out)
        aliases = {len(args) - 1: 0}
    return pl.pallas_call(
        functools.partial(_ple_kernel, final_norm=final_norm),
        grid=(steps,),
        in_specs=in_specs,
        out_specs=pl.BlockSpec((tm, D_MODEL), lambda i: (i + off, 0)),
        out_shape=jax.ShapeDtypeStruct((t, D_MODEL), F32),
        input_output_aliases=aliases,
        compiler_params=_params(("parallel",)),
        name="ple",
    )(*args)


def kernel(x, p, g_mix, w_in, pool_w, pool_scale, ssm_a_re, ssm_a_im, ssm_log_dt, ssm_b_re,
           ssm_b_im, ssm_c_re, ssm_c_im, ssm_d, glu_w, glu_b, w_out, g_ffn, router_grp_w,
           router_grp_b, router_exp_w, router_exp_b, exp_w_gate, exp_w_up, exp_w_down, g_ple,
           ple_gate_w, ple_gate_b, ple_proj_w, g_final):
    bsz, seq, dm = x.shape
    depth = g_mix.shape[0]
    t = bsz * seq
    seq_chunks = seq // CHUNK
    nch = t // CHUNK
    tm = 1024
    n_sorted = 2 * t + N_EXPERTS * ROW_TILE
    w_gate_all = exp_w_gate.reshape(depth * N_EXPERTS, dm, D_EXPERT)
    w_up_all = exp_w_up.reshape(depth * N_EXPERTS, dm, D_EXPERT)
    w_down_all = exp_w_down.reshape(depth * N_EXPERTS, D_EXPERT, dm)

    h = x.reshape(t, dm)
    for i in range(depth):
        w_in_b = (g_mix[i][:, None] * w_in[i]).astype(BF16)
        zp, ut = _in_proj(h.reshape(nch, CHUNK, dm), w_in_b[:, :D_POOL],
                          jnp.transpose(w_in_b[:, D_POOL:]))
        a = _pool(zp.reshape(bsz, seq, D_POOL), pool_w[i], pool_scale[i][None])
        yt, w_gate_b, w_up_b, w_down_b = _ssm(
            ut, ssm_a_re[i], ssm_a_im[i], ssm_log_dt[i], ssm_b_re[i], ssm_b_im[i], ssm_c_re[i],
            ssm_c_im[i], ssm_d[i], seq_chunks, w_gate_all, w_up_all, w_down_all, i)
        h = _mix_out(h.reshape(nch, CHUNK, dm), a.reshape(nch, CHUNK, D_POOL), yt,
                     jnp.transpose(glu_w[i]).astype(BF16), glu_b[i][:, None],
                     w_out[i].astype(BF16)).reshape(t, dm)

        eg = EXPERTS_PER_GROUP
        w_router = jnp.concatenate(
            [router_grp_w[i], jnp.zeros((dm, eg - N_EXPERT_GROUPS), F32),
             jnp.transpose(router_exp_w[i], (1, 0, 2)).reshape(dm, N_EXPERTS),
             jnp.zeros((dm, ROUTER_W - eg - N_EXPERTS), F32)], axis=1)
        b_router = jnp.concatenate(
            [router_grp_b[i], jnp.zeros((eg - N_EXPERT_GROUPS,), F32),
             router_exp_b[i].reshape(N_EXPERTS),
             jnp.zeros((ROUTER_W - eg - N_EXPERTS,), F32)])[None]
        vp, meta_t, counts = _router(h, g_ffn[i][None], w_router, b_router, tm)
        idx, tile_expert, n_valid, block = _plan(meta_t, counts, n_sorted // ROW_TILE)
        xs = _sc_scatter_rows(vp.reshape(2 * t, SC_ROW), idx, 2 * n_sorted)
        ys = _experts(xs.reshape(2, n_sorted, SC_ROW), tile_expert, n_valid, block,
                      w_gate_b, w_up_b, w_down_b)
        ys2 = ys.reshape(2 * n_sorted, SC_ROW)
        idx4 = idx.reshape(4, t)
        tp = t // GATHER_PARTS
        ple_wg = (g_ple[i][:, None] * ple_gate_w[i]).astype(BF16)
        ple_wp = ple_proj_w[i].astype(BF16)
        out = None
        for q in range(GATHER_PARTS):
            yg_q = _sc_gather_rows(ys2, idx4[:, q * tp:(q + 1) * tp].reshape(1, 4 * tp))
            out = _ple(h, yg_q.reshape(4, tp, SC_ROW), meta_t, p[i].reshape(t, D_PLE), ple_wg,
                       ple_gate_b[i][None], ple_wp, g_final[None], i == depth - 1, tm, q, out)
        h = out
    return h.reshape(bsz, seq, dm)
```

```python
import functools
import math

import numpy as np
import jax
import jax.numpy as jnp
from jax import lax
from jax.experimental import pallas as pl
from jax.experimental.pallas import tpu as pltpu
from jax.experimental.pallas import tpu_sc as plsc

F32 = jnp.float32
BF16 = jnp.bfloat16
U32 = jnp.uint32

D_MODEL = 1024
D_POOL = 512
D_SSM = 512
POOL_WINDOWS = (2, 4, 8, 16)
POOL_GROUP = 128
SSM_GROUP = 16
N_SSM_GROUPS = 32
SSM_STATE = 64
N_EXPERT_GROUPS = 4
EXPERTS_PER_GROUP = 8
N_EXPERTS = N_EXPERT_GROUPS * EXPERTS_PER_GROUP
D_EXPERT = 256
D_PLE = 256
RMS_EPS = 1e-6

LANES = 128
CHUNK = 32
CHUNK_W = CHUNK * SSM_GROUP
T_SUB = 8
S_TILE = 8
C_TILE = 128
ROUTER_W = LANES
HALF = D_MODEL // 2
ROW_TILE = 1024
SC_WINDOW = 128
SC_ROW = HALF // 2
META_ROWS = 8
GATHER_PARTS = 2
VMEM_LIMIT = 56 * 1024 * 1024


def _dot(a, b):
    return jnp.dot(a, b, preferred_element_type=F32)


def _dot_nt(a, b):
    return lax.dot_general(a, b, (((1,), (1,)), ((), ())), preferred_element_type=F32)


def _dot_tn(a, b):
    return lax.dot_general(a, b, (((0,), (0,)), ((), ())), preferred_element_type=F32)


def _rms(x, g=None):
    y = x * lax.rsqrt(jnp.mean(x * x, axis=-1, keepdims=True) + RMS_EPS)
    return y if g is None else y * g


def _sigmoid(x):
    return 1.0 / (1.0 + jnp.exp(-x))


def _params(sem):
    return pltpu.CompilerParams(dimension_semantics=sem, vmem_limit_bytes=VMEM_LIMIT)


def _tile_step():
    n_s = pl.num_programs(1)
    return pl.program_id(0) * n_s + pl.program_id(1), pl.num_programs(0) * n_s


def _row_copies(hbm, buf, sem, step_idx, slot_idx, to_hbm):
    n_s = pl.num_programs(1)
    c0 = (step_idx // n_s) * C_TILE
    s0 = (step_idx % n_s) * S_TILE
    out = []
    for j in range(S_TILE):
        far, near = hbm.at[pl.ds(c0, C_TILE), s0 + j, :], buf.at[slot_idx, j]
        src, dst = (near, far) if to_hbm else (far, near)
        out.append(pltpu.make_async_copy(src, dst, sem.at[slot_idx]))
    return out


def _load_time_major_rows(hbm, buf, sem):
    step, n_steps = _tile_step()
    slot = step % 2

    @pl.when(step == 0)
    def _():
        for cp in _row_copies(hbm, buf, sem, step, slot, False):
            cp.start()

    @pl.when(step + 1 < n_steps)
    def _():
        for cp in _row_copies(hbm, buf, sem, step + 1, 1 - slot, False):
            cp.start()

    for cp in _row_copies(hbm, buf, sem, step, slot, False):
        cp.wait()
    return buf[slot].reshape(S_TILE * C_TILE, buf.shape[-1])


def _store_time_major_rows(val, hbm, buf, sem):
    step, n_steps = _tile_step()
    slot = step % 2

    @pl.when(step >= 2)
    def _():
        for cp in _row_copies(hbm, buf, sem, step - 2, slot, True):
            cp.wait()

    buf[slot] = val.reshape(S_TILE, C_TILE, val.shape[-1])
    for cp in _row_copies(hbm, buf, sem, step, slot, True):
        cp.start()

    @pl.when(step == n_steps - 1)
    def _():
        @pl.when(step >= 1)
        def _():
            for cp in _row_copies(hbm, buf, sem, step - 1, 1 - slot, True):
                cp.wait()
        for cp in _row_copies(hbm, buf, sem, step, slot, True):
            cp.wait()


def _in_proj_kernel(x_hbm, wp_ref, wst_ref, zp_hbm, ut_ref, xbuf, zbuf, xsem, zsem):
    nc = C_TILE
    u = _rms(_load_time_major_rows(x_hbm, xbuf, xsem)).astype(BF16)
    zt = _dot_nt(wst_ref[...], u).astype(BF16)
    for j in range(S_TILE):
        ut_ref[:, j, :, :] = zt[:, j * nc:(j + 1) * nc].reshape(N_SSM_GROUPS, SSM_GROUP, nc)
    _store_time_major_rows(_dot(u, wp_ref[...]), zp_hbm, zbuf, zsem)


def _in_proj(x3, w_pool, w_ssm_t):
    nch = x3.shape[0]
    return pl.pallas_call(
        _in_proj_kernel,
        grid=(nch // C_TILE, CHUNK // S_TILE),
        in_specs=[
            pl.BlockSpec(memory_space=pl.ANY),
            pl.BlockSpec((D_MODEL, D_POOL), lambda c, s: (0, 0)),
            pl.BlockSpec((D_SSM, D_MODEL), lambda c, s: (0, 0)),
        ],
        out_specs=[
            pl.BlockSpec(memory_space=pl.ANY),
            pl.BlockSpec((N_SSM_GROUPS, None, S_TILE, SSM_GROUP, C_TILE), lambda c, s: (0, c, s, 0, 0)),
        ],
        out_shape=[
            jax.ShapeDtypeStruct((nch, CHUNK, D_POOL), F32),
            jax.ShapeDtypeStruct((N_SSM_GROUPS, nch // C_TILE, CHUNK, SSM_GROUP, C_TILE), BF16),
        ],
        scratch_shapes=[
            pltpu.VMEM((2, S_TILE, C_TILE, D_MODEL), F32),
            pltpu.VMEM((2, S_TILE, C_TILE, D_POOL), F32),
            pltpu.SemaphoreType.DMA((2,)),
            pltpu.SemaphoreType.DMA((2,)),
        ],
        compiler_params=_params(("arbitrary", "arbitrary")),
        name="in_proj",
    )(x3, w_pool, w_ssm_t)


def _shift_rows(x, d, row, n):
    if d == 0:
        return x
    r = pltpu.roll(x, d % n, 0)
    if d > 0:
        return jnp.where(row >= d, r, 0.0)
    return jnp.where(row < n + d, r, 0.0)


def _pool_kernel(z_ref, w_ref, sc_ref, o_ref):
    n = z_ref.shape[0]
    gi = pl.program_id(1)
    row = lax.broadcasted_iota(jnp.int32, (n, 1), 0)

    for k, w in enumerate(POOL_WINDOWS):
        @pl.when(gi == k)
        def _(w=w):
            x = z_ref[...]
            half = w // 2
            pd, pu, span = x, x, 1
            while span < half:
                pd = pd + _shift_rows(pd, span, row, n)
                pu = pu + _shift_rows(pu, -span, row, n)
                span *= 2
            total = _shift_rows(pd, 1, row, n) + pu
            lo = jnp.maximum(row - half, 0)
            hi = jnp.minimum(row + half, n)
            cnt = (hi - lo).astype(F32)
            diff = (total / cnt - x).astype(BF16)
            o_ref[...] = _dot(diff, w_ref[...].astype(BF16)) * sc_ref[...]


def _pool(zp3, pool_w, pool_scale):
    b, s, _ = zp3.shape
    return pl.pallas_call(
        _pool_kernel,
        grid=(b, len(POOL_WINDOWS)),
        in_specs=[
            pl.BlockSpec((None, s, POOL_GROUP), lambda i, g: (i, 0, g)),
            pl.BlockSpec((None, POOL_GROUP, POOL_GROUP), lambda i, g: (g, 0, 0)),
            pl.BlockSpec((1, POOL_GROUP), lambda i, g: (0, g)),
        ],
        out_specs=pl.BlockSpec((None, s, POOL_GROUP), lambda i, g: (i, 0, g)),
        out_shape=jax.ShapeDtypeStruct((b, s, D_POOL), F32),
        compiler_params=_params(("parallel", "parallel")),
        name="pool",
    )(zp3, pool_w, pool_scale)


def _expand_consts():
    time = np.arange(CHUNK_W) // SSM_GROUP
    def onehot(e):
        m = np.zeros((CHUNK_W, LANES), np.float32)
        m[np.arange(CHUNK_W), e] = 1.0
        return m
    return np.stack([
        onehot(CHUNK - 1 - time),
        onehot(time),
        onehot(time + 1),
        onehot(CHUNK - time),
    ])


def _ssm_kernel(u_ref, vec_ref, mat_ref, exp_ref, wg_ref, wu_ref, wd_ref, y_ref, og_ref, ou_ref,
                od_ref, *, seq_chunks):
    og_ref[...] = wg_ref[...].astype(BF16)
    ou_ref[...] = wu_ref[...].astype(BF16)
    od_ref[...] = wd_ref[...].astype(BF16)

    n_ct = u_ref.shape[0]
    nch = n_ct * C_TILE
    half = LANES // 2
    lane = lax.broadcasted_iota(jnp.int32, (1, LANES), 1)
    lo_half = lane < half

    def direction(di):
        a_re = vec_ref[di, 0:1]
        a_im = vec_ref[di, 1:2]
        dt = jnp.exp(vec_ref[di, 2:3])
        mag = jnp.exp(a_re * dt)
        ang = a_im * dt
        lb_re = mag * jnp.cos(ang)
        lb_im = mag * jnp.sin(ang)
        den = a_re * a_re + a_im * a_im
        f_re = ((lb_re - 1.0) * a_re + lb_im * a_im) / den
        f_im = (lb_im * a_re - (lb_re - 1.0) * a_im) / den
        return (lb_re, lb_im), f_re, f_im

    def power_table(lam):
        e = lax.broadcasted_iota(jnp.int32, (LANES, 1), 0)
        sq_re, sq_im = lam
        t_re = jnp.ones((LANES, LANES), F32)
        t_im = jnp.zeros((LANES, LANES), F32)
        for k in range(CHUNK.bit_length()):
            bit = ((e >> k) & 1) == 1
            t_re, t_im = (jnp.where(bit, t_re * sq_re - t_im * sq_im, t_re),
                          jnp.where(bit, t_re * sq_im + t_im * sq_re, t_im))
            sq_re, sq_im = sq_re * sq_re - sq_im * sq_im, 2.0 * sq_re * sq_im
        return jnp.where(lo_half, t_re, t_im), jnp.where(lo_half, t_im, t_re)

    def tile_rows(x16):
        return jnp.broadcast_to(x16[None], (CHUNK, SSM_GROUP, LANES)).reshape(CHUNK_W, LANES)

    def expanded(tab, which, v_re, v_im, conj_sign):
        lexp = _dot(exp_ref[which], jnp.concatenate(tab, axis=1).astype(BF16))
        if conj_sign > 0:
            p = jnp.where(lo_half, v_re, v_re)
            q = jnp.where(lo_half, -v_im, v_im)
        else:
            p = jnp.where(lo_half, v_re, -v_re)
            q = jnp.where(lo_half, -v_im, -v_im)
        return lexp[:, :LANES] * tile_rows(p) + lexp[:, LANES:] * tile_rows(q)

    lam_f, ff_re, ff_im = direction(0)
    lam_b, fb_re, fb_im = direction(1)
    tabs_f = power_table(lam_f)
    tabs_b = power_table(lam_b)
    tab_f, tab_b = tabs_f[0], tabs_b[0]

    def bbar(bt_re, bt_im, f_re, f_im):
        return bt_re * f_re - bt_im * f_im, bt_re * f_im + bt_im * f_re

    def mat(di, k):
        return mat_ref[di, k * SSM_GROUP:(k + 1) * SSM_GROUP, :]

    bf_re, bf_im = bbar(mat(0, 0), mat(0, 1), ff_re, ff_im)
    bb_re, bb_im = bbar(mat(1, 0), mat(1, 1), fb_re, fb_im)

    pb1 = expanded(tabs_f, 0, bf_re, bf_im, 1)
    pb2 = expanded(tabs_b, 1, bb_re, bb_im, 1)
    pb3 = expanded(tabs_b, 2, bb_re, bb_im, 1)
    ft_f = expanded(tabs_f, 2, mat(0, 3), mat(0, 4), -1)
    ft_b = expanded(tabs_b, 3, mat(1, 3), mat(1, 4), -1)

    row_w = lax.broadcasted_iota(jnp.int32, (CHUNK_W, 1), 0)
    last_blk = row_w >= CHUNK_W - SSM_GROUP
    pb2_lag0 = jnp.where(last_blk, pltpu.roll(pb2, CHUNK_W - SSM_GROUP, 0), 0.0)
    ccr_f = mat(0, 2).astype(BF16)
    ccr_b = mat(1, 2).astype(BF16)
    r_lo = _dot_nt(ccr_f, pb1.astype(BF16)) + _dot_nt(ccr_b, pb2_lag0.astype(BF16))
    co = lax.broadcasted_iota(jnp.int32, (SSM_GROUP, CHUNK_W), 0)
    col = lax.broadcasted_iota(jnp.int32, (SSM_GROUP, CHUNK_W), 1)
    r_lo = r_lo + jnp.where(col == CHUNK_W - SSM_GROUP + co, mat(0, 5)[:, 0:1], 0.0)
    r_hi = _dot_nt(ccr_b, pb3.astype(BF16))
    r_t = jnp.concatenate([r_lo, r_hi], axis=1)
    g_t = jnp.concatenate(
        [pltpu.roll(r_t, SSM_GROUP * (tl + 1), 1) for tl in range(T_SUB)], axis=0
    ).astype(BF16)

    u = jnp.concatenate([u_ref[ct].reshape(CHUNK_W, C_TILE) for ct in range(n_ct)],
                        axis=1)
    e_mat = jnp.concatenate([pb1, pb2], axis=1).astype(BF16)
    xend = _dot_tn(e_mat, u)
    lanec = lax.broadcasted_iota(jnp.int32, (1, nch), 1) % seq_chunks
    ns = SSM_STATE

    def scan(re, im, tab, forward):
        lam_col = jnp.transpose(tab[CHUNK:CHUNK + 8, :])[:, 0:1]
        a, b = lam_col[:ns], lam_col[ns:]
        n_steps = int(math.log2(seq_chunks))

        def rolled(v, d):
            if d % LANES == 0:
                return jnp.concatenate([v[:, nch - d:], v[:, :nch - d]], axis=1)
            return pltpu.roll(v, d, 1)

        def shifted(v, d):
            if forward:
                return jnp.where(lanec >= d, rolled(v, d), 0.0)
            return jnp.where(lanec < seq_chunks - d, rolled(v, nch - d), 0.0)

        for k in range(n_steps):
            sr, si = shifted(re, 1 << k), shifted(im, 1 << k)
            re, im = re + (sr * a - si * b), im + (sr * b + si * a)
            a, b = a * a - b * b, 2.0 * a * b
        return shifted(re, 1), shifted(im, 1)

    f_re, f_im = scan(xend[:ns], xend[ns:2 * ns], tab_f, True)
    b_re, b_im = scan(xend[2 * ns:3 * ns], xend[3 * ns:], tab_b, False)
    xin = jnp.concatenate([f_re, f_im, b_re, b_im], axis=0).astype(BF16)
    f_t = jnp.concatenate([ft_f, ft_b], axis=1).astype(BF16)

    toeplitz = jnp.concatenate(
        [g_t[:, CHUNK_W - LANES * th:2 * CHUNK_W - LANES * th] for th in range(CHUNK // T_SUB)],
        axis=0)
    y_t = _dot(toeplitz, u) + _dot(f_t, xin)
    for ct in range(n_ct):
        y_ref[ct] = y_t[:, ct * C_TILE:(ct + 1) * C_TILE].reshape(
            CHUNK, SSM_GROUP, C_TILE).astype(y_ref.dtype)


def _ssm(ut, a_re, a_im, log_dt, b_re, b_im, c_re, c_im, d, seq_chunks, w_gate, w_up, w_down, layer):
    g, n_ct = ut.shape[:2]
    assert N_EXPERTS % g == 0
    epg = N_EXPERTS // g
    base = layer * g
    n = SSM_STATE

    def per_group(a):
        return jnp.swapaxes(a, 0, 1)

    def dup(a):
        return jnp.concatenate([a, a], axis=-1)

    vecs = dup(jnp.stack([per_group(a_re), per_group(a_im),
                          jnp.broadcast_to(per_group(log_dt)[..., None], (g, 2, n))], axis=2))
    cr, ci = per_group(c_re), per_group(c_im)
    d_blk = jnp.broadcast_to(d.reshape(g, 1, SSM_GROUP, 1), (g, 2, SSM_GROUP, LANES))
    mats = jnp.concatenate(
        [dup(jnp.swapaxes(per_group(b_re), 2, 3)), dup(jnp.swapaxes(per_group(b_im), 2, 3)),
         jnp.concatenate([cr, -ci], axis=-1), dup(cr), dup(ci), d_blk], axis=2)
    exp_c = jnp.asarray(_expand_consts(), BF16)

    return pl.pallas_call(
        functools.partial(_ssm_kernel, seq_chunks=seq_chunks),
        grid=(g,),
        in_specs=[
            pl.BlockSpec((None, n_ct, CHUNK, SSM_GROUP, C_TILE), lambda i: (i, 0, 0, 0, 0)),
            pl.BlockSpec((None, 2, 3, LANES), lambda i: (i, 0, 0, 0)),
            pl.BlockSpec((None, 2, 6 * SSM_GROUP, LANES), lambda i: (i, 0, 0, 0)),
            pl.BlockSpec((4, CHUNK_W, LANES), lambda i: (0, 0, 0)),
            pl.BlockSpec((epg, D_MODEL, D_EXPERT), lambda i: (base + i, 0, 0)),
            pl.BlockSpec((epg, D_MODEL, D_EXPERT), lambda i: (base + i, 0, 0)),
            pl.BlockSpec((epg, D_EXPERT, D_MODEL), lambda i: (base + i, 0, 0)),
        ],
        out_specs=[
            pl.BlockSpec((None, n_ct, CHUNK, SSM_GROUP, C_TILE), lambda i: (i, 0, 0, 0, 0)),
            pl.BlockSpec((epg, D_MODEL, D_EXPERT), lambda i: (i, 0, 0)),
            pl.BlockSpec((epg, D_MODEL, D_EXPERT), lambda i: (i, 0, 0)),
            pl.BlockSpec((epg, D_EXPERT, D_MODEL), lambda i: (i, 0, 0)),
        ],
        out_shape=[
            jax.ShapeDtypeStruct(ut.shape, BF16),
            jax.ShapeDtypeStruct((N_EXPERTS, D_MODEL, D_EXPERT), BF16),
            jax.ShapeDtypeStruct((N_EXPERTS, D_MODEL, D_EXPERT), BF16),
            jax.ShapeDtypeStruct((N_EXPERTS, D_EXPERT, D_MODEL), BF16),
        ],
        compiler_params=_params(("parallel",)),
        name="ssm",
    )(ut, vecs, mats, exp_c, w_gate, w_up, w_down)


def _mix_out_kernel(x_hbm, a_hbm, yt_ref, gwt_ref, gb_ref, wo_ref, h_hbm, xbuf, abuf, hbuf,
                    xsem, asem, hsem):
    nc = C_TILE
    x = _load_time_major_rows(x_hbm, xbuf, xsem)
    a = _load_time_major_rows(a_hbm, abuf, asem).astype(BF16)
    y = jnp.concatenate([yt_ref[:, j, :, :].reshape(D_SSM, nc) for j in range(S_TILE)],
                        axis=1).astype(F32)
    z = 0.5 * y * (1.0 + jnp.tanh(math.sqrt(2.0 / math.pi) * (y + 0.044715 * (y * y * y))))
    gate = _sigmoid(_dot(gwt_ref[...], z.astype(BF16)) + gb_ref[...])
    s = (z * gate).astype(BF16)
    h = x + _dot(a, wo_ref[:D_POOL, :]) + _dot_tn(s, wo_ref[D_POOL:, :])
    _store_time_major_rows(h, h_hbm, hbuf, hsem)


def _mix_out(x3, a3, yt, glu_w_t, glu_b_col, w_out):
    nch = x3.shape[0]
    return pl.pallas_call(
        _mix_out_kernel,
        grid=(nch // C_TILE, CHUNK // S_TILE),
        in_specs=[
            pl.BlockSpec(memory_space=pl.ANY),
            pl.BlockSpec(memory_space=pl.ANY),
            pl.BlockSpec((N_SSM_GROUPS, None, S_TILE, SSM_GROUP, C_TILE), lambda c, t: (0, c, t, 0, 0)),
            pl.BlockSpec((D_SSM, D_SSM), lambda c, t: (0, 0)),
            pl.BlockSpec((D_SSM, 1), lambda c, t: (0, 0)),
            pl.BlockSpec((D_MODEL, D_MODEL), lambda c, t: (0, 0)),
        ],
        out_specs=pl.BlockSpec(memory_space=pl.ANY),
        out_shape=jax.ShapeDtypeStruct((nch, CHUNK, D_MODEL), F32),
        scratch_shapes=[
            pltpu.VMEM((2, S_TILE, C_TILE, D_MODEL), F32),
            pltpu.VMEM((2, S_TILE, C_TILE, D_POOL), F32),
            pltpu.VMEM((2, S_TILE, C_TILE, D_MODEL), F32),
            pltpu.SemaphoreType.DMA((2,)),
            pltpu.SemaphoreType.DMA((2,)),
            pltpu.SemaphoreType.DMA((2,)),
        ],
        compiler_params=_params(("arbitrary", "arbitrary")),
        name="mix_out",
    )(x3, a3, yt, glu_w_t, glu_b_col, w_out)


def _pack_rows(x):
    b = lax.bitcast_convert_type(x.astype(BF16).astype(F32), U32)
    return (b[:, :HALF] & jnp.uint32(0xFFFF0000)) | (b[:, HALF:] >> 16)


def _unpack_rows(w):
    lo = lax.bitcast_convert_type(w & jnp.uint32(0xFFFF0000), F32)
    hi = lax.bitcast_convert_type(w << 16, F32)
    return lo, hi


def _split_bf16(x):
    hi = x.astype(BF16)
    return hi, (x - hi.astype(F32)).astype(BF16)


def _route(v32, wr_ref, br_ref):
    v_hi, v_lo = _split_bf16(v32)
    w_hi, w_lo = _split_bf16(wr_ref[...])
    both = _dot(v_hi, jnp.concatenate([w_hi, w_lo], axis=1))
    logits = both[:, :ROUTER_W] + (both[:, ROUTER_W:] + _dot(v_lo, w_hi)) + br_ref[...]
    return jnp.transpose(logits)


def _top1(x, valid=None):
    n = x.shape[0]
    row = lax.broadcasted_iota(jnp.int32, x.shape, 0).astype(F32)
    if valid is not None:
        x = jnp.where(valid, x, -jnp.inf)
    m = jnp.max(x, axis=0, keepdims=True)
    idx = jnp.min(jnp.where(x == m, row, float(n)), axis=0, keepdims=True)
    return m, idx, x, row


def _split_planes(packed, ref):
    ref[0] = packed[:, :SC_ROW]
    ref[1] = packed[:, SC_ROW:]


def _router_kernel(h_ref, g_ref, wr_ref, br_ref, before_ref, vp_ref, meta_t_ref, cnt_ref, carry_ref):
    @pl.when(pl.program_id(0) == 0)
    def _():
        carry_ref[...] = jnp.zeros_like(carry_ref)

    v32 = _rms(h_ref[...], g_ref[...])
    _split_planes(_pack_rows(v32), vp_ref)
    lt = _route(v32, wr_ref, br_ref)
    tm = lt.shape[1]
    eg = EXPERTS_PER_GROUP

    grp = lt[:eg]
    grp_row = lax.broadcasted_iota(jnp.int32, grp.shape, 0)
    mg, grp_idx, grp, _ = _top1(grp, grp_row < N_EXPERT_GROUPS)
    grp_p = 1.0 / jnp.sum(jnp.exp(grp - mg), axis=0, keepdims=True)
    le = jnp.zeros((eg, tm), F32)
    for g in range(N_EXPERT_GROUPS):
        le = jnp.where(grp_idx == float(g), lt[eg * (g + 1):eg * (g + 2)], le)
    m1, i1, le, row = _top1(le)
    z = jnp.sum(jnp.exp(le - m1), axis=0, keepdims=True)
    m2, i2, _, _ = _top1(jnp.where(row == i1, -jnp.inf, le))
    p1 = 1.0 / z
    p2 = jnp.exp(m2 - m1) / z
    tot = p1 + p2
    w1 = grp_p * (p1 / tot)
    w2 = grp_p * (p2 / tot)
    e1 = grp_idx * eg + i1
    e2 = grp_idx * eg + i2

    erow = lax.broadcasted_iota(jnp.int32, (N_EXPERTS, tm), 0).astype(F32)
    onehot = jnp.where(erow == e1, 1.0, jnp.where(erow == e2, 1.0, 0.0))
    before = _dot(onehot.astype(BF16), before_ref[...]) + carry_ref[...]
    rank1 = jnp.sum(jnp.where(erow == e1, before, 0.0), axis=0, keepdims=True)
    rank2 = jnp.sum(jnp.where(erow == e2, before, 0.0), axis=0, keepdims=True)
    carry = carry_ref[...] + jnp.sum(onehot, axis=1, keepdims=True)
    carry_ref[...] = carry
    cnt_ref[...] = carry

    mrow = lax.broadcasted_iota(jnp.int32, (META_ROWS, tm), 0)
    meta_t_ref[...] = jnp.where(mrow == 0, e1, jnp.where(mrow == 1, e2, jnp.where(
        mrow == 2, rank1, jnp.where(mrow == 3, rank2, jnp.where(
            mrow == 4, w1, jnp.where(mrow == 5, w2, 0.0))))))


def _earlier_matrix(tm):
    return np.triu(np.ones((tm, tm), np.float32), k=1)


def _router(h1, g_ffn, w_router, b_router, tm):
    t = h1.shape[0]
    return pl.pallas_call(
        _router_kernel,
        grid=(t // tm,),
        in_specs=[
            pl.BlockSpec((tm, D_MODEL), lambda i: (i, 0)),
            pl.BlockSpec((1, D_MODEL), lambda i: (0, 0)),
            pl.BlockSpec((D_MODEL, ROUTER_W), lambda i: (0, 0)),
            pl.BlockSpec((1, ROUTER_W), lambda i: (0, 0)),
            pl.BlockSpec((tm, tm), lambda i: (0, 0)),
        ],
        out_specs=[
            pl.BlockSpec((2, tm, SC_ROW), lambda i: (0, i, 0)),
            pl.BlockSpec((META_ROWS, tm), lambda i: (0, i)),
            pl.BlockSpec((N_EXPERTS, 1), lambda i: (0, 0)),
        ],
        out_shape=[
            jax.ShapeDtypeStruct((2, t, SC_ROW), U32),
            jax.ShapeDtypeStruct((META_ROWS, t), F32),
            jax.ShapeDtypeStruct((N_EXPERTS, 1), F32),
        ],
        scratch_shapes=[pltpu.VMEM((N_EXPERTS, 1), F32)],
        compiler_params=_params(("arbitrary",)),
        name="router",
    )(h1, g_ffn, w_router, b_router, jnp.asarray(_earlier_matrix(tm), BF16))


def _plan(meta_t, counts, n_tiles):
    e1 = meta_t[0].astype(jnp.int32)
    e2 = meta_t[1].astype(jnp.int32)
    rank1 = meta_t[2].astype(jnp.int32)
    rank2 = meta_t[3].astype(jnp.int32)
    cnt = counts[:, 0].astype(jnp.int32)
    padded = ((cnt + ROW_TILE - 1) // ROW_TILE) * ROW_TILE
    ends = jnp.cumsum(padded)
    starts = ends - padded
    experts = jnp.arange(N_EXPERTS, dtype=jnp.int32)
    pos1 = rank1 + jnp.sum(jnp.where(e1[None, :] == experts[:, None], starts[:, None], 0), axis=0)
    pos2 = rank2 + jnp.sum(jnp.where(e2[None, :] == experts[:, None], starts[:, None], 0), axis=0)
    tile_start = jnp.arange(n_tiles, dtype=jnp.int32) * ROW_TILE
    tile_expert = jnp.sum((tile_start[:, None] >= ends[None, :]).astype(jnp.int32), axis=1)
    tile_expert = jnp.minimum(tile_expert, N_EXPERTS - 1)
    rows_left = jnp.sum(jnp.where(tile_expert[:, None] == experts, cnt + starts, 0), axis=1) - tile_start
    n_valid = jnp.clip(rows_left, 0, ROW_TILE).astype(jnp.int32)
    last_used = jnp.maximum(ends[-1] // ROW_TILE - 1, 0)
    block = jnp.minimum(jnp.arange(n_tiles, dtype=jnp.int32), last_used)
    tile_expert = jnp.sum(jnp.where(block[:, None] == jnp.arange(n_tiles)[None, :],
                                    tile_expert[None, :], 0), axis=1)
    plane = n_tiles * ROW_TILE
    half_rows = jnp.concatenate([pos1, pos1 + plane, pos2, pos2 + plane])[None]
    return half_rows, tile_expert, n_valid, block


def _sc_mesh():
    return plsc.VectorSubcoreMesh(core_axis_name="c", subcore_axis_name="s")


def _sc_scatter_rows(rows, idx, n_out):
    t, width = rows.shape
    steps = t // SC_WINDOW
    half = steps // 2

    @pl.kernel(out_type=jax.ShapeDtypeStruct((n_out, width), rows.dtype), mesh=_sc_mesh(),
               scratch_types=[], name="moe_scatter")
    def scatter(rows_hbm, idx_hbm, out_hbm):
        def body(rows_vmem, idx0_vmem, idx1_vmem):
            pltpu.sync_copy(rows_vmem, out_hbm.at[idx0_vmem.at[0]])
            pltpu.sync_copy(rows_vmem, out_hbm.at[idx1_vmem.at[0]])

        pltpu.emit_pipeline(
            body,
            grid=(2, half),
            in_specs=[pl.BlockSpec((SC_WINDOW, width), lambda c, j: (c * half + j, 0)),
                      pl.BlockSpec((1, SC_WINDOW), lambda c, j: (0, c * half + j)),
                      pl.BlockSpec((1, SC_WINDOW), lambda c, j: (0, steps + c * half + j))],
            out_specs=[],
            core_axis_name=("c", "s"),
            dimension_semantics=(pltpu.PARALLEL, pltpu.PARALLEL),
        )(rows_hbm, idx_hbm, idx_hbm)

    return scatter(rows, idx)


def _sc_gather_rows(table, idx):
    m = idx.shape[1]
    width = table.shape[1]
    steps = m // (2 * SC_WINDOW)

    @pl.kernel(out_type=jax.ShapeDtypeStruct((m, width), table.dtype), mesh=_sc_mesh(),
               scratch_types=[], name="moe_gather")
    def gather(table_hbm, idx_hbm, out_hbm):
        def body(idx_vmem, out_vmem):
            pltpu.sync_copy(table_hbm.at[idx_vmem.at[0]], out_vmem)

        pltpu.emit_pipeline(
            body,
            grid=(2, steps),
            in_specs=[pl.BlockSpec((1, SC_WINDOW), lambda k, j: (0, k * steps + j))],
            out_specs=[pl.BlockSpec((SC_WINDOW, width), lambda k, j: (k * steps + j, 0))],
            core_axis_name=("c", "s"),
            dimension_semantics=(pltpu.PARALLEL, pltpu.PARALLEL),
        )(idx_hbm, out_hbm)

    return gather(table, idx)


def _experts_kernel(te_ref, nv_ref, blk_ref, xs_ref, wg_ref, wu_ref, wd_ref, ys_ref):
    r = pl.program_id(0)
    n_valid = nv_ref[r]

    @pl.when(n_valid > 0)
    def _():
        parts = [p.astype(BF16) for p in _unpack_rows(xs_ref[0]) + _unpack_rows(xs_ref[1])]
        cols = (0, 2 * SC_ROW, SC_ROW, 3 * SC_ROW)
        hg = sum(_dot(p, wg_ref[c:c + SC_ROW, :]) for p, c in zip(parts, cols))
        hu = sum(_dot(p, wu_ref[c:c + SC_ROW, :]) for p, c in zip(parts, cols))
        row = lax.broadcasted_iota(jnp.int32, (ROW_TILE, 1), 0)
        hid = jnp.where(row < n_valid, hg * _sigmoid(hg) * hu, 0.0).astype(BF16)
        _split_planes(_pack_rows(_dot(hid, wd_ref[...])), ys_ref)


def _experts(xs, tile_expert, n_valid, block, w_gate, w_up, w_down):
    n_tiles = xs.shape[1] // ROW_TILE
    w_spec = pl.BlockSpec((None, D_MODEL, D_EXPERT), lambda r, te, nv, blk: (te[r], 0, 0))
    grid_spec = pltpu.PrefetchScalarGridSpec(
        num_scalar_prefetch=3,
        grid=(n_tiles,),
        in_specs=[
            pl.BlockSpec((2, ROW_TILE, SC_ROW), lambda r, te, nv, blk: (0, blk[r], 0)),
            w_spec, w_spec,
            pl.BlockSpec((None, D_EXPERT, D_MODEL), lambda r, te, nv, blk: (te[r], 0, 0)),
        ],
        out_specs=pl.BlockSpec((2, ROW_TILE, SC_ROW), lambda r, te, nv, blk: (0, blk[r], 0)),
    )
    return pl.pallas_call(
        _experts_kernel,
        grid_spec=grid_spec,
        out_shape=jax.ShapeDtypeStruct((2, n_tiles * ROW_TILE, SC_ROW), U32),
        compiler_params=_params(("arbitrary",)),
        name="experts",
    )(tile_expert, n_valid, block, xs, w_gate, w_up, w_down)


def _ple_kernel(h_ref, yg_ref, meta_t_ref, p_ref, wg_ref, bg_ref, wp_ref, gf_ref, *rest, final_norm):
    o_ref = rest[-1]
    meta = jnp.transpose(meta_t_ref[...])
    w1 = meta[:, 4:5]
    w2 = meta[:, 5:6]
    q0, q2 = (w1 * u + w2 * v for u, v in zip(_unpack_rows(yg_ref[0]), _unpack_rows(yg_ref[2])))
    q1, q3 = (w1 * u + w2 * v for u, v in zip(_unpack_rows(yg_ref[1]), _unpack_rows(yg_ref[3])))
    moe = jnp.concatenate([q0, q1, q2, q3], axis=1)
    h = h_ref[...] + moe
    gate = _sigmoid(_dot(_rms(h).astype(BF16), wg_ref[...]) + bg_ref[...])
    h = h + gate * _dot(p_ref[...].astype(BF16), wp_ref[...])
    o_ref[...] = _rms(h, gf_ref[...]) if final_norm else h


def _ple(h1, yg, meta_t, p2, w_gate, b_gate, w_proj, g_final, final_norm, tm, part, prev_out):
    t = h1.shape[0]
    steps = t // tm // GATHER_PARTS
    off = part * steps
    in_specs = [
        pl.BlockSpec((tm, D_MODEL), lambda i: (i + off, 0)),
        pl.BlockSpec((4, tm, SC_ROW), lambda i: (0, i, 0)),
        pl.BlockSpec((META_ROWS, tm), lambda i: (0, i + off)),
        pl.BlockSpec((tm, D_PLE), lambda i: (i + off, 0)),
        pl.BlockSpec((D_MODEL, D_MODEL), lambda i: (0, 0)),
        pl.BlockSpec((1, D_MODEL), lambda i: (0, 0)),
        pl.BlockSpec((D_PLE, D_MODEL), lambda i: (0, 0)),
        pl.BlockSpec((1, D_MODEL), lambda i: (0, 0)),
    ]
    args = [h1, yg, meta_t, p2, w_gate, b_gate, w_proj, g_final]
    aliases = {}
    if prev_out is not None:
        in_specs.append(pl.BlockSpec(memory_space=pl.ANY))
        args.append(prev_out)
        aliases = {len(args) - 1: 0}
    return pl.pallas_call(
        functools.partial(_ple_kernel, final_norm=final_norm),
        grid=(steps,),
        in_specs=in_specs,
        out_specs=pl.BlockSpec((tm, D_MODEL), lambda i: (i + off, 0)),
        out_shape=jax.ShapeDtypeStruct((t, D_MODEL), F32),
        input_output_aliases=aliases,
        compiler_params=_params(("parallel",)),
        name="ple",
    )(*args)


def kernel(x, p, g_mix, w_in, pool_w, pool_scale, ssm_a_re, ssm_a_im, ssm_log_dt, ssm_b_re,
           ssm_b_im, ssm_c_re, ssm_c_im, ssm_d, glu_w, glu_b, w_out, g_ffn, router_grp_w,
           router_grp_b, router_exp_w, router_exp_b, exp_w_gate, exp_w_up, exp_w_down, g_ple,
           ple_gate_w, ple_gate_b, ple_proj_w, g_final):
    bsz, seq, dm = x.shape
    depth = g_mix.shape[0]
    t = bsz * seq
    seq_chunks = seq // CHUNK
    nch = t // CHUNK
    tm = 1024
    n_sorted = 2 * t + N_EXPERTS * ROW_TILE
    w_gate_all = exp_w_gate.reshape(depth * N_EXPERTS, dm, D_EXPERT)
    w_up_all = exp_w_up.reshape(depth * N_EXPERTS, dm, D_EXPERT)
    w_down_all = exp_w_down.reshape(depth * N_EXPERTS, D_EXPERT, dm)

    h = x.reshape(t, dm)
    for i in range(depth):
        w_in_b = (g_mix[i][:, None] * w_in[i]).astype(BF16)
        zp, ut = _in_proj(h.reshape(nch, CHUNK, dm), w_in_b[:, :D_POOL],
                          jnp.transpose(w_in_b[:, D_POOL:]))
        a = _pool(zp.reshape(bsz, seq, D_POOL), pool_w[i], pool_scale[i][None])
        yt, w_gate_b, w_up_b, w_down_b = _ssm(
            ut, ssm_a_re[i], ssm_a_im[i], ssm_log_dt[i], ssm_b_re[i], ssm_b_im[i], ssm_c_re[i],
            ssm_c_im[i], ssm_d[i], seq_chunks, w_gate_all, w_up_all, w_down_all, i)
        h = _mix_out(h.reshape(nch, CHUNK, dm), a.reshape(nch, CHUNK, D_POOL), yt,
                     jnp.transpose(glu_w[i]).astype(BF16), glu_b[i][:, None],
                     w_out[i].astype(BF16)).reshape(t, dm)

        eg = EXPERTS_PER_GROUP
        w_router = jnp.concatenate(
            [router_grp_w[i], jnp.zeros((dm, eg - N_EXPERT_GROUPS), F32),
             jnp.transpose(router_exp_w[i], (1, 0, 2)).reshape(dm, N_EXPERTS),
             jnp.zeros((dm, ROUTER_W - eg - N_EXPERTS), F32)], axis=1)
        b_router = jnp.concatenate(
            [router_grp_b[i], jnp.zeros((eg - N_EXPERT_GROUPS,), F32),
             router_exp_b[i].reshape(N_EXPERTS),
             jnp.zeros((ROUTER_W - eg - N_EXPERTS,), F32)])[None]
        vp, meta_t, counts = _router(h, g_ffn[i][None], w_router, b_router, tm)
        idx, tile_expert, n_valid, block = _plan(meta_t, counts, n_sorted // ROW_TILE)
        xs = _sc_scatter_rows(vp.reshape(2 * t, SC_ROW), idx, 2 * n_sorted)
        ys = _experts(xs.reshape(2, n_sorted, SC_ROW), tile_expert, n_valid, block,
                      w_gate_b, w_up_b, w_down_b)
        ys2 = ys.reshape(2 * n_sorted, SC_ROW)
        idx4 = idx.reshape(4, t)
        tp = t // GATHER_PARTS
        ple_wg = (g_ple[i][:, None] * ple_gate_w[i]).astype(BF16)
        ple_wp = ple_proj_w[i].astype(BF16)
        out = None
        for q in range(GATHER_PARTS):
            yg_q = _sc_gather_rows(ys2, idx4[:, q * tp:(q + 1) * tp].reshape(1, 4 * tp))
            out = _ple(h, yg_q.reshape(4, tp, SC_ROW), meta_t, p[i].reshape(t, D_PLE), ple_wg,
                       ple_gate_b[i][None], ple_wp, g_final[None], i == depth - 1, tm, q, out)
        h = out
    return h.reshape(bsz, seq, dm)
```

```python
import functools
import math

import numpy as np
import jax
import jax.numpy as jnp
from jax import lax
from jax.experimental import pallas as pl
from jax.experimental.pallas import tpu as pltpu
from jax.experimental.pallas import tpu_sc as plsc

F32 = jnp.float32
BF16 = jnp.bfloat16
U32 = jnp.uint32

D_MODEL = 1024
D_POOL = 512
D_SSM = 512
POOL_WINDOWS = (2, 4, 8, 16)
POOL_GROUP = 128
SSM_GROUP = 16
N_SSM_GROUPS = 32
SSM_STATE = 64
N_EXPERT_GROUPS = 4
EXPERTS_PER_GROUP = 8
N_EXPERTS = N_EXPERT_GROUPS * EXPERTS_PER_GROUP
D_EXPERT = 256
D_PLE = 256
RMS_EPS = 1e-6

LANES = 128
CHUNK = 32
CHUNK_W = CHUNK * SSM_GROUP
T_SUB = 8
S_TILE = 8
C_TILE = 128
ROUTER_W = LANES
HALF = D_MODEL // 2
ROW_TILE = 1024
SC_WINDOW = 128
SC_ROW = HALF // 2
META_ROWS = 8
GATHER_PARTS = 2
VMEM_LIMIT = 56 * 1024 * 1024


def _dot(a, b):
    return jnp.dot(a, b, preferred_element_type=F32)


def _dot_nt(a, b):
    return lax.dot_general(a, b, (((1,), (1,)), ((), ())), preferred_element_type=F32)


def _dot_tn(a, b):
    return lax.dot_general(a, b, (((0,), (0,)), ((), ())), preferred_element_type=F32)


def _rms(x, g=None):
    y = x * lax.rsqrt(jnp.mean(x * x, axis=-1, keepdims=True) + RMS_EPS)
    return y if g is None else y * g


def _sigmoid(x):
    return 1.0 / (1.0 + jnp.exp(-x))


def _params(sem):
    return pltpu.CompilerParams(dimension_semantics=sem, vmem_limit_bytes=VMEM_LIMIT)


def _tile_step():
    n_s = pl.num_programs(1)
    return pl.program_id(0) * n_s + pl.program_id(1), pl.num_programs(0) * n_s


def _row_copies(hbm, buf, sem, step_idx, slot_idx, to_hbm):
    n_s = pl.num_programs(1)
    c0 = (step_idx // n_s) * C_TILE
    s0 = (step_idx % n_s) * S_TILE
    out = []
    for j in range(S_TILE):
        far, near = hbm.at[pl.ds(c0, C_TILE), s0 + j, :], buf.at[slot_idx, j]
        src, dst = (near, far) if to_hbm else (far, near)
        out.append(pltpu.make_async_copy(src, dst, sem.at[slot_idx]))
    return out


def _load_time_major_rows(hbm, buf, sem):
    step, n_steps = _tile_step()
    slot = step % 2

    @pl.when(step == 0)
    def _():
        for cp in _row_copies(hbm, buf, sem, step, slot, False):
            cp.start()

    @pl.when(step + 1 < n_steps)
    def _():
        for cp in _row_copies(hbm, buf, sem, step + 1, 1 - slot, False):
            cp.start()

    for cp in _row_copies(hbm, buf, sem, step, slot, False):
        cp.wait()
    return buf[slot].reshape(S_TILE * C_TILE, buf.shape[-1])


def _store_time_major_rows(val, hbm, buf, sem):
    step, n_steps = _tile_step()
    slot = step % 2

    @pl.when(step >= 2)
    def _():
        for cp in _row_copies(hbm, buf, sem, step - 2, slot, True):
            cp.wait()

    buf[slot] = val.reshape(S_TILE, C_TILE, val.shape[-1])
    for cp in _row_copies(hbm, buf, sem, step, slot, True):
        cp.start()

    @pl.when(step == n_steps - 1)
    def _():
        @pl.when(step >= 1)
        def _():
            for cp in _row_copies(hbm, buf, sem, step - 1, 1 - slot, True):
                cp.wait()
        for cp in _row_copies(hbm, buf, sem, step, slot, True):
            cp.wait()


def _in_proj_kernel(x_hbm, wp_ref, wst_ref, zp_ref, ut_ref, xbuf, xsem):
    nc = C_TILE
    u = _rms(_load_time_major_rows(x_hbm, xbuf, xsem)).astype(BF16)
    zp_ref[...] = _dot(u, wp_ref[...]).reshape(zp_ref.shape).astype(zp_ref.dtype)
    zt = _dot_nt(wst_ref[...], u).astype(BF16)
    for j in range(S_TILE):
        ut_ref[:, j, :, :] = zt[:, j * nc:(j + 1) * nc].reshape(N_SSM_GROUPS, SSM_GROUP, nc)


def _in_proj(x3, w_pool, w_ssm_t):
    nch = x3.shape[0]
    return pl.pallas_call(
        _in_proj_kernel,
        grid=(nch // C_TILE, CHUNK // S_TILE),
        in_specs=[
            pl.BlockSpec(memory_space=pl.ANY),
            pl.BlockSpec((D_MODEL, D_POOL), lambda c, s: (0, 0)),
            pl.BlockSpec((D_SSM, D_MODEL), lambda c, s: (0, 0)),
        ],
        out_specs=[
            pl.BlockSpec((S_TILE, C_TILE, D_POOL), lambda c, s: (s, c, 0)),
            pl.BlockSpec((N_SSM_GROUPS, None, S_TILE, SSM_GROUP, C_TILE), lambda c, s: (0, c, s, 0, 0)),
        ],
        out_shape=[
            jax.ShapeDtypeStruct((CHUNK, nch, D_POOL), BF16),
            jax.ShapeDtypeStruct((N_SSM_GROUPS, nch // C_TILE, CHUNK, SSM_GROUP, C_TILE), BF16),
        ],
        scratch_shapes=[
            pltpu.VMEM((2, S_TILE, C_TILE, D_MODEL), F32),
            pltpu.SemaphoreType.DMA((2,)),
        ],
        compiler_params=_params(("arbitrary", "arbitrary")),
        name="in_proj",
    )(x3, w_pool, w_ssm_t)


def _pool_kernel(z_ref, w_ref, sc_ref, o_ref):
    n_c = z_ref.shape[1]
    gi = pl.program_id(1)
    chunk = lax.broadcasted_iota(jnp.int32, (n_c, 1), 0)

    def plane(p):
        if p < 0:
            return jnp.where(chunk >= 1, pltpu.roll(z_ref[p + CHUNK].astype(F32), 1, 0), 0.0)
        if p >= CHUNK:
            return jnp.where(chunk < n_c - 1, pltpu.roll(z_ref[p - CHUNK].astype(F32), n_c - 1, 0), 0.0)
        return z_ref[p].astype(F32)

    for k, w in enumerate(POOL_WINDOWS):
        @pl.when(gi == k)
        def _(w=w):
            half = w // 2
            total = plane(-half)
            for p in range(-half + 1, half):
                total = total + plane(p)
            diffs = []
            for s_t in range(CHUNK):
                t = chunk * CHUNK + s_t
                cnt = (jnp.minimum(t + half, n_c * CHUNK) - jnp.maximum(t - half, 0)).astype(F32)
                diffs.append((total / cnt - plane(s_t)).astype(BF16))
                if s_t + 1 < CHUNK:
                    total = total + plane(s_t + half) - plane(s_t - half)
            diff = jnp.concatenate(diffs, axis=0)
            out = _dot(diff, w_ref[...].astype(BF16)) * sc_ref[...]
            o_ref[...] = out.reshape(o_ref.shape).astype(o_ref.dtype)


def _pool(zp_t, pool_w, pool_scale, bsz):
    _, nch, _ = zp_t.shape
    n_c = nch // bsz
    return pl.pallas_call(
        _pool_kernel,
        grid=(bsz, len(POOL_WINDOWS)),
        in_specs=[
            pl.BlockSpec((CHUNK, n_c, POOL_GROUP), lambda i, g: (0, i, g)),
            pl.BlockSpec((None, POOL_GROUP, POOL_GROUP), lambda i, g: (g, 0, 0)),
            pl.BlockSpec((1, POOL_GROUP), lambda i, g: (0, g)),
        ],
        out_specs=pl.BlockSpec((CHUNK, n_c, POOL_GROUP), lambda i, g: (0, i, g)),
        out_shape=jax.ShapeDtypeStruct(zp_t.shape, BF16),
        compiler_params=_params(("parallel", "parallel")),
        name="pool",
    )(zp_t, pool_w, pool_scale)


def _expand_consts():
    time = np.arange(CHUNK_W) // SSM_GROUP
    def onehot(e):
        m = np.zeros((CHUNK_W, LANES), np.float32)
        m[np.arange(CHUNK_W), e] = 1.0
        return m
    return np.stack([
        onehot(CHUNK - 1 - time),
        onehot(time),
        onehot(time + 1),
        onehot(CHUNK - time),
    ])


def _ssm_kernel(u_ref, vec_ref, mat_ref, exp_ref, wg_ref, wu_ref, wd_ref, y_ref, og_ref, ou_ref,
                od_ref, *, seq_chunks):
    og_ref[...] = wg_ref[...].astype(BF16)
    ou_ref[...] = wu_ref[...].astype(BF16)
    od_ref[...] = wd_ref[...].astype(BF16)

    n_ct = u_ref.shape[0]
    nch = n_ct * C_TILE
    half = LANES // 2
    lane = lax.broadcasted_iota(jnp.int32, (1, LANES), 1)
    lo_half = lane < half

    def direction(di):
        a_re = vec_ref[di, 0:1]
        a_im = vec_ref[di, 1:2]
        dt = jnp.exp(vec_ref[di, 2:3])
        mag = jnp.exp(a_re * dt)
        ang = a_im * dt
        lb_re = mag * jnp.cos(ang)
        lb_im = mag * jnp.sin(ang)
        den = a_re * a_re + a_im * a_im
        f_re = ((lb_re - 1.0) * a_re + lb_im * a_im) / den
        f_im = (lb_im * a_re - (lb_re - 1.0) * a_im) / den
        return (lb_re, lb_im), f_re, f_im

    def power_table(lam):
        e = lax.broadcasted_iota(jnp.int32, (LANES, 1), 0)
        sq_re, sq_im = lam
        t_re = jnp.ones((LANES, LANES), F32)
        t_im = jnp.zeros((LANES, LANES), F32)
        for k in range(CHUNK.bit_length()):
            bit = ((e >> k) & 1) == 1
            t_re, t_im = (jnp.where(bit, t_re * sq_re - t_im * sq_im, t_re),
                          jnp.where(bit, t_re * sq_im + t_im * sq_re, t_im))
            sq_re, sq_im = sq_re * sq_re - sq_im * sq_im, 2.0 * sq_re * sq_im
        return jnp.where(lo_half, t_re, t_im), jnp.where(lo_half, t_im, t_re)

    def tile_rows(x16):
        return jnp.broadcast_to(x16[None], (CHUNK, SSM_GROUP, LANES)).reshape(CHUNK_W, LANES)

    def expanded(tab, which, v_re, v_im, conj_sign):
        lexp = _dot(exp_ref[which], jnp.concatenate(tab, axis=1).astype(BF16))
        if conj_sign > 0:
            p = jnp.where(lo_half, v_re, v_re)
            q = jnp.where(lo_half, -v_im, v_im)
        else:
            p = jnp.where(lo_half, v_re, -v_re)
            q = jnp.where(lo_half, -v_im, -v_im)
        return lexp[:, :LANES] * tile_rows(p) + lexp[:, LANES:] * tile_rows(q)

    lam_f, ff_re, ff_im = direction(0)
    lam_b, fb_re, fb_im = direction(1)
    tabs_f = power_table(lam_f)
    tabs_b = power_table(lam_b)
    tab_f, tab_b = tabs_f[0], tabs_b[0]

    def bbar(bt_re, bt_im, f_re, f_im):
        return bt_re * f_re - bt_im * f_im, bt_re * f_im + bt_im * f_re

    def mat(di, k):
        return mat_ref[di, k * SSM_GROUP:(k + 1) * SSM_GROUP, :]

    bf_re, bf_im = bbar(mat(0, 0), mat(0, 1), ff_re, ff_im)
    bb_re, bb_im = bbar(mat(1, 0), mat(1, 1), fb_re, fb_im)

    pb1 = expanded(tabs_f, 0, bf_re, bf_im, 1)
    pb2 = expanded(tabs_b, 1, bb_re, bb_im, 1)
    pb3 = expanded(tabs_b, 2, bb_re, bb_im, 1)
    ft_f = expanded(tabs_f, 2, mat(0, 3), mat(0, 4), -1)
    ft_b = expanded(tabs_b, 3, mat(1, 3), mat(1, 4), -1)

    row_w = lax.broadcasted_iota(jnp.int32, (CHUNK_W, 1), 0)
    last_blk = row_w >= CHUNK_W - SSM_GROUP
    pb2_lag0 = jnp.where(last_blk, pltpu.roll(pb2, CHUNK_W - SSM_GROUP, 0), 0.0)
    ccr_f = mat(0, 2).astype(BF16)
    ccr_b = mat(1, 2).astype(BF16)
    r_lo = _dot_nt(ccr_f, pb1.astype(BF16)) + _dot_nt(ccr_b, pb2_lag0.astype(BF16))
    co = lax.broadcasted_iota(jnp.int32, (SSM_GROUP, CHUNK_W), 0)
    col = lax.broadcasted_iota(jnp.int32, (SSM_GROUP, CHUNK_W), 1)
    r_lo = r_lo + jnp.where(col == CHUNK_W - SSM_GROUP + co, mat(0, 5)[:, 0:1], 0.0)
    r_hi = _dot_nt(ccr_b, pb3.astype(BF16))
    r_t = jnp.concatenate([r_lo, r_hi], axis=1)
    g_t = jnp.concatenate(
        [pltpu.roll(r_t, SSM_GROUP * (tl + 1), 1) for tl in range(T_SUB)], axis=0
    ).astype(BF16)

    u = jnp.concatenate([u_ref[ct].reshape(CHUNK_W, C_TILE) for ct in range(n_ct)],
                        axis=1)
    e_mat = jnp.concatenate([pb1, pb2], axis=1).astype(BF16)
    xend = _dot_tn(e_mat, u)
    lanec = lax.broadcasted_iota(jnp.int32, (1, nch), 1) % seq_chunks
    ns = SSM_STATE

    def scan(re, im, tab, forward):
        lam_col = jnp.transpose(tab[CHUNK:CHUNK + 8, :])[:, 0:1]
        a, b = lam_col[:ns], lam_col[ns:]
        n_steps = int(math.log2(seq_chunks))

        def rolled(v, d):
            if d % LANES == 0:
                return jnp.concatenate([v[:, nch - d:], v[:, :nch - d]], axis=1)
            return pltpu.roll(v, d, 1)

        def shifted(v, d):
            if forward:
                return jnp.where(lanec >= d, rolled(v, d), 0.0)
            return jnp.where(lanec < seq_chunks - d, rolled(v, nch - d), 0.0)

        for k in range(n_steps):
            sr, si = shifted(re, 1 << k), shifted(im, 1 << k)
            re, im = re + (sr * a - si * b), im + (sr * b + si * a)
            a, b = a * a - b * b, 2.0 * a * b
        return shifted(re, 1), shifted(im, 1)

    f_re, f_im = scan(xend[:ns], xend[ns:2 * ns], tab_f, True)
    b_re, b_im = scan(xend[2 * ns:3 * ns], xend[3 * ns:], tab_b, False)
    xin = jnp.concatenate([f_re, f_im, b_re, b_im], axis=0).astype(BF16)
    f_t = jnp.concatenate([ft_f, ft_b], axis=1).astype(BF16)

    toeplitz = jnp.concatenate(
        [g_t[:, CHUNK_W - LANES * th:2 * CHUNK_W - LANES * th] for th in range(CHUNK // T_SUB)],
        axis=0)
    y_t = _dot(toeplitz, u) + _dot(f_t, xin)
    for ct in range(n_ct):
        y_ref[ct] = y_t[:, ct * C_TILE:(ct + 1) * C_TILE].reshape(
            CHUNK, SSM_GROUP, C_TILE).astype(y_ref.dtype)


def _ssm(ut, a_re, a_im, log_dt, b_re, b_im, c_re, c_im, d, seq_chunks, w_gate, w_up, w_down, layer):
    g, n_ct = ut.shape[:2]
    assert N_EXPERTS % g == 0
    epg = N_EXPERTS // g
    base = layer * g
    n = SSM_STATE

    def per_group(a):
        return jnp.swapaxes(a, 0, 1)

    def dup(a):
        return jnp.concatenate([a, a], axis=-1)

    vecs = dup(jnp.stack([per_group(a_re), per_group(a_im),
                          jnp.broadcast_to(per_group(log_dt)[..., None], (g, 2, n))], axis=2))
    cr, ci = per_group(c_re), per_group(c_im)
    d_blk = jnp.broadcast_to(d.reshape(g, 1, SSM_GROUP, 1), (g, 2, SSM_GROUP, LANES))
    mats = jnp.concatenate(
        [dup(jnp.swapaxes(per_group(b_re), 2, 3)), dup(jnp.swapaxes(per_group(b_im), 2, 3)),
         jnp.concatenate([cr, -ci], axis=-1), dup(cr), dup(ci), d_blk], axis=2)
    exp_c = jnp.asarray(_expand_consts(), BF16)

    return pl.pallas_call(
        functools.partial(_ssm_kernel, seq_chunks=seq_chunks),
        grid=(g,),
        in_specs=[
            pl.BlockSpec((None, n_ct, CHUNK, SSM_GROUP, C_TILE), lambda i: (i, 0, 0, 0, 0)),
            pl.BlockSpec((None, 2, 3, LANES), lambda i: (i, 0, 0, 0)),
            pl.BlockSpec((None, 2, 6 * SSM_GROUP, LANES), lambda i: (i, 0, 0, 0)),
            pl.BlockSpec((4, CHUNK_W, LANES), lambda i: (0, 0, 0)),
            pl.BlockSpec((epg, D_MODEL, D_EXPERT), lambda i: (base + i, 0, 0)),
            pl.BlockSpec((epg, D_MODEL, D_EXPERT), lambda i: (base + i, 0, 0)),
            pl.BlockSpec((epg, D_EXPERT, D_MODEL), lambda i: (base + i, 0, 0)),
        ],
        out_specs=[
            pl.BlockSpec((None, n_ct, CHUNK, SSM_GROUP, C_TILE), lambda i: (i, 0, 0, 0, 0)),
            pl.BlockSpec((epg, D_MODEL, D_EXPERT), lambda i: (i, 0, 0)),
            pl.BlockSpec((epg, D_MODEL, D_EXPERT), lambda i: (i, 0, 0)),
            pl.BlockSpec((epg, D_EXPERT, D_MODEL), lambda i: (i, 0, 0)),
        ],
        out_shape=[
            jax.ShapeDtypeStruct(ut.shape, BF16),
            jax.ShapeDtypeStruct((N_EXPERTS, D_MODEL, D_EXPERT), BF16),
            jax.ShapeDtypeStruct((N_EXPERTS, D_MODEL, D_EXPERT), BF16),
            jax.ShapeDtypeStruct((N_EXPERTS, D_EXPERT, D_MODEL), BF16),
        ],
        compiler_params=_params(("parallel",)),
        name="ssm",
    )(ut, vecs, mats, exp_c, w_gate, w_up, w_down)


def _mix_out_kernel(x_hbm, a_ref, yt_ref, gwt_ref, gb_ref, wo_ref, h_hbm, xbuf, hbuf, xsem, hsem):
    nc = C_TILE
    x = _load_time_major_rows(x_hbm, xbuf, xsem)
    a = a_ref[...].reshape(S_TILE * nc, D_POOL)
    y = jnp.concatenate([yt_ref[:, j, :, :].reshape(D_SSM, nc) for j in range(S_TILE)],
                        axis=1).astype(F32)
    z = 0.5 * y * (1.0 + jnp.tanh(math.sqrt(2.0 / math.pi) * (y + 0.044715 * (y * y * y))))
    gate = _sigmoid(_dot(gwt_ref[...], z.astype(BF16)) + gb_ref[...])
    s = (z * gate).astype(BF16)
    h = x + _dot(a, wo_ref[:D_POOL, :]) + _dot_tn(s, wo_ref[D_POOL:, :])
    _store_time_major_rows(h, h_hbm, hbuf, hsem)


def _mix_out(x3, a3, yt, glu_w_t, glu_b_col, w_out):
    nch = x3.shape[0]
    return pl.pallas_call(
        _mix_out_kernel,
        grid=(nch // C_TILE, CHUNK // S_TILE),
        in_specs=[
            pl.BlockSpec(memory_space=pl.ANY),
            pl.BlockSpec((S_TILE, C_TILE, D_POOL), lambda c, t: (t, c, 0)),
            pl.BlockSpec((N_SSM_GROUPS, None, S_TILE, SSM_GROUP, C_TILE), lambda c, t: (0, c, t, 0, 0)),
            pl.BlockSpec((D_SSM, D_SSM), lambda c, t: (0, 0)),
            pl.BlockSpec((D_SSM, 1), lambda c, t: (0, 0)),
            pl.BlockSpec((D_MODEL, D_MODEL), lambda c, t: (0, 0)),
        ],
        out_specs=pl.BlockSpec(memory_space=pl.ANY),
        out_shape=jax.ShapeDtypeStruct((nch, CHUNK, D_MODEL), F32),
        scratch_shapes=[
            pltpu.VMEM((2, S_TILE, C_TILE, D_MODEL), F32),
            pltpu.VMEM((2, S_TILE, C_TILE, D_MODEL), F32),
            pltpu.SemaphoreType.DMA((2,)),
            pltpu.SemaphoreType.DMA((2,)),
        ],
        compiler_params=_params(("arbitrary", "arbitrary")),
        name="mix_out",
    )(x3, a3, yt, glu_w_t, glu_b_col, w_out)


def _pack_rows(x):
    b = lax.bitcast_convert_type(x.astype(BF16).astype(F32), U32)
    return (b[:, :HALF] & jnp.uint32(0xFFFF0000)) | (b[:, HALF:] >> 16)


def _unpack_rows(w):
    lo = lax.bitcast_convert_type(w & jnp.uint32(0xFFFF0000), F32)
    hi = lax.bitcast_convert_type(w << 16, F32)
    return lo, hi


def _split_bf16(x):
    hi = x.astype(BF16)
    return hi, (x - hi.astype(F32)).astype(BF16)


def _route(v32, wr_ref, br_ref):
    v_hi, v_lo = _split_bf16(v32)
    w_hi, w_lo = _split_bf16(wr_ref[...])
    both = _dot(v_hi, jnp.concatenate([w_hi, w_lo], axis=1))
    logits = both[:, :ROUTER_W] + (both[:, ROUTER_W:] + _dot(v_lo, w_hi)) + br_ref[...]
    return jnp.transpose(logits)


def _top1(x, valid=None):
    n = x.shape[0]
    row = lax.broadcasted_iota(jnp.int32, x.shape, 0).astype(F32)
    if valid is not None:
        x = jnp.where(valid, x, -jnp.inf)
    m = jnp.max(x, axis=0, keepdims=True)
    idx = jnp.min(jnp.where(x == m, row, float(n)), axis=0, keepdims=True)
    return m, idx, x, row


def _split_planes(packed, ref):
    ref[0] = packed[:, :SC_ROW]
    ref[1] = packed[:, SC_ROW:]


def _router_kernel(h_ref, g_ref, wr_ref, br_ref, before_ref, vp_ref, meta_t_ref, cnt_ref, carry_ref):
    @pl.when(pl.program_id(0) == 0)
    def _():
        carry_ref[...] = jnp.zeros_like(carry_ref)

    v32 = _rms(h_ref[...], g_ref[...])
    _split_planes(_pack_rows(v32), vp_ref)
    lt = _route(v32, wr_ref, br_ref)
    tm = lt.shape[1]
    eg = EXPERTS_PER_GROUP

    grp = lt[:eg]
    grp_row = lax.broadcasted_iota(jnp.int32, grp.shape, 0)
    mg, grp_idx, grp, _ = _top1(grp, grp_row < N_EXPERT_GROUPS)
    grp_p = 1.0 / jnp.sum(jnp.exp(grp - mg), axis=0, keepdims=True)
    le = jnp.zeros((eg, tm), F32)
    for g in range(N_EXPERT_GROUPS):
        le = jnp.where(grp_idx == float(g), lt[eg * (g + 1):eg * (g + 2)], le)
    m1, i1, le, row = _top1(le)
    z = jnp.sum(jnp.exp(le - m1), axis=0, keepdims=True)
    m2, i2, _, _ = _top1(jnp.where(row == i1, -jnp.inf, le))
    p1 = 1.0 / z
    p2 = jnp.exp(m2 - m1) / z
    tot = p1 + p2
    w1 = grp_p * (p1 / tot)
    w2 = grp_p * (p2 / tot)
    e1 = grp_idx * eg + i1
    e2 = grp_idx * eg + i2

    erow = lax.broadcasted_iota(jnp.int32, (N_EXPERTS, tm), 0).astype(F32)
    onehot = jnp.where(erow == e1, 1.0, jnp.where(erow == e2, 1.0, 0.0))
    before = _dot(onehot.astype(BF16), before_ref[...]) + carry_ref[...]
    rank1 = jnp.sum(jnp.where(erow == e1, before, 0.0), axis=0, keepdims=True)
    rank2 = jnp.sum(jnp.where(erow == e2, before, 0.0), axis=0, keepdims=True)
    carry = carry_ref[...] + jnp.sum(onehot, axis=1, keepdims=True)
    carry_ref[...] = carry
    cnt_ref[...] = carry

    mrow = lax.broadcasted_iota(jnp.int32, (META_ROWS, tm), 0)
    meta_t_ref[...] = jnp.where(mrow == 0, e1, jnp.where(mrow == 1, e2, jnp.where(
        mrow == 2, rank1, jnp.where(mrow == 3, rank2, jnp.where(
            mrow == 4, w1, jnp.where(mrow == 5, w2, 0.0))))))


def _earlier_matrix(tm):
    return np.triu(np.ones((tm, tm), np.float32), k=1)


def _router(h1, g_ffn, w_router, b_router, tm):
    t = h1.shape[0]
    return pl.pallas_call(
        _router_kernel,
        grid=(t // tm,),
        in_specs=[
            pl.BlockSpec((tm, D_MODEL), lambda i: (i, 0)),
            pl.BlockSpec((1, D_MODEL), lambda i: (0, 0)),
            pl.BlockSpec((D_MODEL, ROUTER_W), lambda i: (0, 0)),
            pl.BlockSpec((1, ROUTER_W), lambda i: (0, 0)),
            pl.BlockSpec((tm, tm), lambda i: (0, 0)),
        ],
        out_specs=[
            pl.BlockSpec((2, tm, SC_ROW), lambda i: (0, i, 0)),
            pl.BlockSpec((META_ROWS, tm), lambda i: (0, i)),
            pl.BlockSpec((N_EXPERTS, 1), lambda i: (0, 0)),
        ],
        out_shape=[
            jax.ShapeDtypeStruct((2, t, SC_ROW), U32),
            jax.ShapeDtypeStruct((META_ROWS, t), F32),
            jax.ShapeDtypeStruct((N_EXPERTS, 1), F32),
        ],
        scratch_shapes=[pltpu.VMEM((N_EXPERTS, 1), F32)],
        compiler_params=_params(("arbitrary",)),
        name="router",
    )(h1, g_ffn, w_router, b_router, jnp.asarray(_earlier_matrix(tm), BF16))


def _plan(meta_t, counts, n_tiles):
    e1 = meta_t[0].astype(jnp.int32)
    e2 = meta_t[1].astype(jnp.int32)
    rank1 = meta_t[2].astype(jnp.int32)
    rank2 = meta_t[3].astype(jnp.int32)
    cnt = counts[:, 0].astype(jnp.int32)
    padded = ((cnt + ROW_TILE - 1) // ROW_TILE) * ROW_TILE
    ends = jnp.cumsum(padded)
    starts = ends - padded
    experts = jnp.arange(N_EXPERTS, dtype=jnp.int32)
    pos1 = rank1 + jnp.sum(jnp.where(e1[None, :] == experts[:, None], starts[:, None], 0), axis=0)
    pos2 = rank2 + jnp.sum(jnp.where(e2[None, :] == experts[:, None], starts[:, None], 0), axis=0)
    tile_start = jnp.arange(n_tiles, dtype=jnp.int32) * ROW_TILE
    tile_expert = jnp.sum((tile_start[:, None] >= ends[None, :]).astype(jnp.int32), axis=1)
    tile_expert = jnp.minimum(tile_expert, N_EXPERTS - 1)
    rows_left = jnp.sum(jnp.where(tile_expert[:, None] == experts, cnt + starts, 0), axis=1) - tile_start
    n_valid = jnp.clip(rows_left, 0, ROW_TILE).astype(jnp.int32)
    last_used = jnp.maximum(ends[-1] // ROW_TILE - 1, 0)
    block = jnp.minimum(jnp.arange(n_tiles, dtype=jnp.int32), last_used)
    tile_expert = jnp.sum(jnp.where(block[:, None] == jnp.arange(n_tiles)[None, :],
                                    tile_expert[None, :], 0), axis=1)
    plane = n_tiles * ROW_TILE
    half_rows = jnp.concatenate([pos1, pos1 + plane, pos2, pos2 + plane])[None]
    return half_rows, tile_expert, n_valid, block


def _sc_mesh():
    return plsc.VectorSubcoreMesh(core_axis_name="c", subcore_axis_name="s")


def _sc_scatter_rows(rows, idx, n_out):
    t, width = rows.shape
    steps = t // SC_WINDOW
    half = steps // 2

    @pl.kernel(out_type=jax.ShapeDtypeStruct((n_out, width), rows.dtype), mesh=_sc_mesh(),
               scratch_types=[], name="moe_scatter")
    def scatter(rows_hbm, idx_hbm, out_hbm):
        def body(rows_vmem, idx0_vmem, idx1_vmem):
            pltpu.sync_copy(rows_vmem, out_hbm.at[idx0_vmem.at[0]])
            pltpu.sync_copy(rows_vmem, out_hbm.at[idx1_vmem.at[0]])

        pltpu.emit_pipeline(
            body,
            grid=(2, half),
            in_specs=[pl.BlockSpec((SC_WINDOW, width), lambda c, j: (c * half + j, 0)),
                      pl.BlockSpec((1, SC_WINDOW), lambda c, j: (0, c * half + j)),
                      pl.BlockSpec((1, SC_WINDOW), lambda c, j: (0, steps + c * half + j))],
            out_specs=[],
            core_axis_name=("c", "s"),
            dimension_semantics=(pltpu.PARALLEL, pltpu.PARALLEL),
        )(rows_hbm, idx_hbm, idx_hbm)

    return scatter(rows, idx)


def _sc_gather_rows(table, idx):
    m = idx.shape[1]
    width = table.shape[1]
    steps = m // (2 * SC_WINDOW)

    @pl.kernel(out_type=jax.ShapeDtypeStruct((m, width), table.dtype), mesh=_sc_mesh(),
               scratch_types=[], name="moe_gather")
    def gather(table_hbm, idx_hbm, out_hbm):
        def body(idx_vmem, out_vmem):
            pltpu.sync_copy(table_hbm.at[idx_vmem.at[0]], out_vmem)

        pltpu.emit_pipeline(
            body,
            grid=(2, steps),
            in_specs=[pl.BlockSpec((1, SC_WINDOW), lambda k, j: (0, k * steps + j))],
            out_specs=[pl.BlockSpec((SC_WINDOW, width), lambda k, j: (k * steps + j, 0))],
            core_axis_name=("c", "s"),
            dimension_semantics=(pltpu.PARALLEL, pltpu.PARALLEL),
        )(idx_hbm, out_hbm)

    return gather(table, idx)


def _experts_kernel(te_ref, nv_ref, blk_ref, xs_ref, wg_ref, wu_ref, wd_ref, ys_ref):
    r = pl.program_id(0)
    n_valid = nv_ref[r]

    @pl.when(n_valid > 0)
    def _():
        parts = [p.astype(BF16) for p in _unpack_rows(xs_ref[0]) + _unpack_rows(xs_ref[1])]
        cols = (0, 2 * SC_ROW, SC_ROW, 3 * SC_ROW)
        hg = sum(_dot(p, wg_ref[c:c + SC_ROW, :]) for p, c in zip(parts, cols))
        hu = sum(_dot(p, wu_ref[c:c + SC_ROW, :]) for p, c in zip(parts, cols))
        row = lax.broadcasted_iota(jnp.int32, (ROW_TILE, 1), 0)
        hid = jnp.where(row < n_valid, hg * _sigmoid(hg) * hu, 0.0).astype(BF16)
        _split_planes(_pack_rows(_dot(hid, wd_ref[...])), ys_ref)


def _experts(xs, tile_expert, n_valid, block, w_gate, w_up, w_down):
    n_tiles = xs.shape[1] // ROW_TILE
    w_spec = pl.BlockSpec((None, D_MODEL, D_EXPERT), lambda r, te, nv, blk: (te[r], 0, 0))
    grid_spec = pltpu.PrefetchScalarGridSpec(
        num_scalar_prefetch=3,
        grid=(n_tiles,),
        in_specs=[
            pl.BlockSpec((2, ROW_TILE, SC_ROW), lambda r, te, nv, blk: (0, blk[r], 0)),
            w_spec, w_spec,
            pl.BlockSpec((None, D_EXPERT, D_MODEL), lambda r, te, nv, blk: (te[r], 0, 0)),
        ],
        out_specs=pl.BlockSpec((2, ROW_TILE, SC_ROW), lambda r, te, nv, blk: (0, blk[r], 0)),
    )
    return pl.pallas_call(
        _experts_kernel,
        grid_spec=grid_spec,
        out_shape=jax.ShapeDtypeStruct((2, n_tiles * ROW_TILE, SC_ROW), U32),
        compiler_params=_params(("arbitrary",)),
        name="experts",
    )(tile_expert, n_valid, block, xs, w_gate, w_up, w_down)


def _ple_kernel(h_ref, yg_ref, meta_t_ref, p_ref, wg_ref, bg_ref, wp_ref, gf_ref, *rest, final_norm):
    o_ref = rest[-1]
    meta = jnp.transpose(meta_t_ref[...])
    w1 = meta[:, 4:5]
    w2 = meta[:, 5:6]
    q0, q2 = (w1 * u + w2 * v for u, v in zip(_unpack_rows(yg_ref[0]), _unpack_rows(yg_ref[2])))
    q1, q3 = (w1 * u + w2 * v for u, v in zip(_unpack_rows(yg_ref[1]), _unpack_rows(yg_ref[3])))
    moe = jnp.concatenate([q0, q1, q2, q3], axis=1)
    h = h_ref[...] + moe
    gate = _sigmoid(_dot(_rms(h).astype(BF16), wg_ref[...]) + bg_ref[...])
    h = h + gate * _dot(p_ref[...].astype(BF16), wp_ref[...])
    o_ref[...] = _rms(h, gf_ref[...]) if final_norm else h


def _ple(h1, yg, meta_t, p2, w_gate, b_gate, w_proj, g_final, final_norm, tm, part, prev_out):
    t = h1.shape[0]
    steps = t // tm // GATHER_PARTS
    off = part * steps
    in_specs = [
        pl.BlockSpec((tm, D_MODEL), lambda i: (i + off, 0)),
        pl.BlockSpec((4, tm, SC_ROW), lambda i: (0, i, 0)),
        pl.BlockSpec((META_ROWS, tm), lambda i: (0, i + off)),
        pl.BlockSpec((tm, D_PLE), lambda i: (i + off, 0)),
        pl.BlockSpec((D_MODEL, D_MODEL), lambda i: (0, 0)),
        pl.BlockSpec((1, D_MODEL), lambda i: (0, 0)),
        pl.BlockSpec((D_PLE, D_MODEL), lambda i: (0, 0)),
        pl.BlockSpec((1, D_MODEL), lambda i: (0, 0)),
    ]
    args = [h1, yg, meta_t, p2, w_gate, b_gate, w_proj, g_final]
    aliases = {}
    if prev_out is not None:
        in_specs.append(pl.BlockSpec(memory_space=pl.ANY))
        args.append(prev_out)
        aliases = {len(args) - 1: 0}
    return pl.pallas_call(
        functools.partial(_ple_kernel, final_norm=final_norm),
        grid=(steps,),
        in_specs=in_specs,
        out_specs=pl.BlockSpec((tm, D_MODEL), lambda i: (i + off, 0)),
        out_shape=jax.ShapeDtypeStruct((t, D_MODEL), F32),
        input_output_aliases=aliases,
        compiler_params=_params(("parallel",)),
        name="ple",
    )(*args)


def kernel(x, p, g_mix, w_in, pool_w, pool_scale, ssm_a_re, ssm_a_im, ssm_log_dt, ssm_b_re,
           ssm_b_im, ssm_c_re, ssm_c_im, ssm_d, glu_w, glu_b, w_out, g_ffn, router_grp_w,
           router_grp_b, router_exp_w, router_exp_b, exp_w_gate, exp_w_up, exp_w_down, g_ple,
           ple_gate_w, ple_gate_b, ple_proj_w, g_final):
    bsz, seq, dm = x.shape
    depth = g_mix.shape[0]
    t = bsz * seq
    seq_chunks = seq // CHUNK
    nch = t // CHUNK
    tm = 1024
    n_sorted = 2 * t + N_EXPERTS * ROW_TILE
    w_gate_all = exp_w_gate.reshape(depth * N_EXPERTS, dm, D_EXPERT)
    w_up_all = exp_w_up.reshape(depth * N_EXPERTS, dm, D_EXPERT)
    w_down_all = exp_w_down.reshape(depth * N_EXPERTS, D_EXPERT, dm)

    h = x.reshape(t, dm)
    for i in range(depth):
        w_in_b = (g_mix[i][:, None] * w_in[i]).astype(BF16)
        zp, ut = _in_proj(h.reshape(nch, CHUNK, dm), w_in_b[:, :D_POOL],
                          jnp.transpose(w_in_b[:, D_POOL:]))
        a = _pool(zp, pool_w[i], pool_scale[i][None], bsz)
        yt, w_gate_b, w_up_b, w_down_b = _ssm(
            ut, ssm_a_re[i], ssm_a_im[i], ssm_log_dt[i], ssm_b_re[i], ssm_b_im[i], ssm_c_re[i],
            ssm_c_im[i], ssm_d[i], seq_chunks, w_gate_all, w_up_all, w_down_all, i)
        h = _mix_out(h.reshape(nch, CHUNK, dm), a, yt,
                     jnp.transpose(glu_w[i]).astype(BF16), glu_b[i][:, None],
                     w_out[i].astype(BF16)).reshape(t, dm)

        eg = EXPERTS_PER_GROUP
        w_router = jnp.concatenate(
            [router_grp_w[i], jnp.zeros((dm, eg - N_EXPERT_GROUPS), F32),
             jnp.transpose(router_exp_w[i], (1, 0, 2)).reshape(dm, N_EXPERTS),
             jnp.zeros((dm, ROUTER_W - eg - N_EXPERTS), F32)], axis=1)
        b_router = jnp.concatenate(
            [router_grp_b[i], jnp.zeros((eg - N_EXPERT_GROUPS,), F32),
             router_exp_b[i].reshape(N_EXPERTS),
             jnp.zeros((ROUTER_W - eg - N_EXPERTS,), F32)])[None]
        vp, meta_t, counts = _router(h, g_ffn[i][None], w_router, b_router, tm)
        idx, tile_expert, n_valid, block = _plan(meta_t, counts, n_sorted // ROW_TILE)
        xs = _sc_scatter_rows(vp.reshape(2 * t, SC_ROW), idx, 2 * n_sorted)
        ys = _experts(xs.reshape(2, n_sorted, SC_ROW), tile_expert, n_valid, block,
                      w_gate_b, w_up_b, w_down_b)
        ys2 = ys.reshape(2 * n_sorted, SC_ROW)
        idx4 = idx.reshape(4, t)
        tp = t // GATHER_PARTS
        ple_wg = (g_ple[i][:, None] * ple_gate_w[i]).astype(BF16)
        ple_wp = ple_proj_w[i].astype(BF16)
        out = None
        for q in range(GATHER_PARTS):
            yg_q = _sc_gather_rows(ys2, idx4[:, q * tp:(q + 1) * tp].reshape(1, 4 * tp))
            out = _ple(h, yg_q.reshape(4, tp, SC_ROW), meta_t, p[i].reshape(t, D_PLE), ple_wg,
                       ple_gate_b[i][None], ple_wp, g_final[None], i == depth - 1, tm, q, out)
        h = out
    return h.reshape(bsz, seq, dm)
```

```python
import functools
import math

import numpy as np
import jax
import jax.numpy as jnp
from jax import lax
from jax.experimental import pallas as pl
from jax.experimental.pallas import tpu as pltpu
from jax.experimental.pallas import tpu_sc as plsc

F32 = jnp.float32
BF16 = jnp.bfloat16
U32 = jnp.uint32

D_MODEL = 1024
D_POOL = 512
D_SSM = 512
POOL_WINDOWS = (2, 4, 8, 16)
POOL_GROUP = 128
SSM_GROUP = 16
N_SSM_GROUPS = 32
SSM_STATE = 64
N_EXPERT_GROUPS = 4
EXPERTS_PER_GROUP = 8
N_EXPERTS = N_EXPERT_GROUPS * EXPERTS_PER_GROUP
D_EXPERT = 256
D_PLE = 256
RMS_EPS = 1e-6

LANES = 128
CHUNK = 32
CHUNK_W = CHUNK * SSM_GROUP
T_SUB = 8
S_TILE = 8
C_TILE = 128
ROUTER_W = LANES
HALF = D_MODEL // 2
ROW_TILE = 1280
SC_WINDOW = 128
SC_ROW = HALF // 2
META_ROWS = 8
GATHER_PARTS = 2
VMEM_LIMIT = 56 * 1024 * 1024


def _dot(a, b):
    return jnp.dot(a, b, preferred_element_type=F32)


def _dot_nt(a, b):
    return lax.dot_general(a, b, (((1,), (1,)), ((), ())), preferred_element_type=F32)


def _dot_tn(a, b):
    return lax.dot_general(a, b, (((0,), (0,)), ((), ())), preferred_element_type=F32)


def _rms(x, g=None):
    y = x * lax.rsqrt(jnp.mean(x * x, axis=-1, keepdims=True) + RMS_EPS)
    return y if g is None else y * g


def _sigmoid(x):
    return 1.0 / (1.0 + jnp.exp(-x))


def _params(sem):
    return pltpu.CompilerParams(dimension_semantics=sem, vmem_limit_bytes=VMEM_LIMIT)


def _tile_step():
    n_s = pl.num_programs(1)
    return pl.program_id(0) * n_s + pl.program_id(1), pl.num_programs(0) * n_s


def _row_copies(hbm, buf, sem, step_idx, slot_idx, to_hbm):
    n_s = pl.num_programs(1)
    c0 = (step_idx // n_s) * C_TILE
    s0 = (step_idx % n_s) * S_TILE
    out = []
    for j in range(S_TILE):
        far, near = hbm.at[pl.ds(c0, C_TILE), s0 + j, :], buf.at[slot_idx, j]
        src, dst = (near, far) if to_hbm else (far, near)
        out.append(pltpu.make_async_copy(src, dst, sem.at[slot_idx]))
    return out


def _load_time_major_rows(hbm, buf, sem):
    step, n_steps = _tile_step()
    slot = step % 2

    @pl.when(step == 0)
    def _():
        for cp in _row_copies(hbm, buf, sem, step, slot, False):
            cp.start()

    @pl.when(step + 1 < n_steps)
    def _():
        for cp in _row_copies(hbm, buf, sem, step + 1, 1 - slot, False):
            cp.start()

    for cp in _row_copies(hbm, buf, sem, step, slot, False):
        cp.wait()
    return buf[slot].reshape(S_TILE * C_TILE, buf.shape[-1])


def _store_time_major_rows(val, hbm, buf, sem):
    step, n_steps = _tile_step()
    slot = step % 2

    @pl.when(step >= 2)
    def _():
        for cp in _row_copies(hbm, buf, sem, step - 2, slot, True):
            cp.wait()

    buf[slot] = val.reshape(S_TILE, C_TILE, val.shape[-1])
    for cp in _row_copies(hbm, buf, sem, step, slot, True):
        cp.start()

    @pl.when(step == n_steps - 1)
    def _():
        @pl.when(step >= 1)
        def _():
            for cp in _row_copies(hbm, buf, sem, step - 1, 1 - slot, True):
                cp.wait()
        for cp in _row_copies(hbm, buf, sem, step, slot, True):
            cp.wait()


def _in_proj_kernel(x_hbm, wp_ref, wst_ref, zp_ref, ut_ref, xbuf, xsem):
    nc = C_TILE
    u = _rms(_load_time_major_rows(x_hbm, xbuf, xsem)).astype(BF16)
    zp_ref[...] = _dot(u, wp_ref[...]).reshape(zp_ref.shape).astype(zp_ref.dtype)
    zt = _dot_nt(wst_ref[...], u).astype(BF16)
    for j in range(S_TILE):
        ut_ref[:, j, :, :] = zt[:, j * nc:(j + 1) * nc].reshape(N_SSM_GROUPS, SSM_GROUP, nc)


def _in_proj(x3, w_pool, w_ssm_t):
    nch = x3.shape[0]
    return pl.pallas_call(
        _in_proj_kernel,
        grid=(nch // C_TILE, CHUNK // S_TILE),
        in_specs=[
            pl.BlockSpec(memory_space=pl.ANY),
            pl.BlockSpec((D_MODEL, D_POOL), lambda c, s: (0, 0)),
            pl.BlockSpec((D_SSM, D_MODEL), lambda c, s: (0, 0)),
        ],
        out_specs=[
            pl.BlockSpec((S_TILE, C_TILE, D_POOL), lambda c, s: (s, c, 0)),
            pl.BlockSpec((N_SSM_GROUPS, None, S_TILE, SSM_GROUP, C_TILE), lambda c, s: (0, c, s, 0, 0)),
        ],
        out_shape=[
            jax.ShapeDtypeStruct((CHUNK, nch, D_POOL), BF16),
            jax.ShapeDtypeStruct((N_SSM_GROUPS, nch // C_TILE, CHUNK, SSM_GROUP, C_TILE), BF16),
        ],
        scratch_shapes=[
            pltpu.VMEM((2, S_TILE, C_TILE, D_MODEL), F32),
            pltpu.SemaphoreType.DMA((2,)),
        ],
        compiler_params=_params(("arbitrary", "arbitrary")),
        name="in_proj",
    )(x3, w_pool, w_ssm_t)


def _pool_kernel(z_ref, w_ref, sc_ref, o_ref):
    n_c = z_ref.shape[1]
    gi = pl.program_id(1)
    chunk = lax.broadcasted_iota(jnp.int32, (n_c, 1), 0)

    def plane(p):
        if p < 0:
            return jnp.where(chunk >= 1, pltpu.roll(z_ref[p + CHUNK].astype(F32), 1, 0), 0.0)
        if p >= CHUNK:
            return jnp.where(chunk < n_c - 1, pltpu.roll(z_ref[p - CHUNK].astype(F32), n_c - 1, 0), 0.0)
        return z_ref[p].astype(F32)

    for k, w in enumerate(POOL_WINDOWS):
        @pl.when(gi == k)
        def _(w=w):
            half = w // 2
            total = plane(-half)
            for p in range(-half + 1, half):
                total = total + plane(p)
            diffs = []
            for s_t in range(CHUNK):
                t = chunk * CHUNK + s_t
                cnt = (jnp.minimum(t + half, n_c * CHUNK) - jnp.maximum(t - half, 0)).astype(F32)
                diffs.append((total / cnt - plane(s_t)).astype(BF16))
                if s_t + 1 < CHUNK:
                    total = total + plane(s_t + half) - plane(s_t - half)
            diff = jnp.concatenate(diffs, axis=0)
            out = _dot(diff, w_ref[...].astype(BF16)) * sc_ref[...]
            o_ref[...] = out.reshape(o_ref.shape).astype(o_ref.dtype)


def _pool(zp_t, pool_w, pool_scale, bsz):
    _, nch, _ = zp_t.shape
    n_c = nch // bsz
    return pl.pallas_call(
        _pool_kernel,
        grid=(bsz, len(POOL_WINDOWS)),
        in_specs=[
            pl.BlockSpec((CHUNK, n_c, POOL_GROUP), lambda i, g: (0, i, g)),
            pl.BlockSpec((None, POOL_GROUP, POOL_GROUP), lambda i, g: (g, 0, 0)),
            pl.BlockSpec((1, POOL_GROUP), lambda i, g: (0, g)),
        ],
        out_specs=pl.BlockSpec((CHUNK, n_c, POOL_GROUP), lambda i, g: (0, i, g)),
        out_shape=jax.ShapeDtypeStruct(zp_t.shape, BF16),
        compiler_params=_params(("parallel", "parallel")),
        name="pool",
    )(zp_t, pool_w, pool_scale)


def _expand_consts():
    time = np.arange(CHUNK_W) // SSM_GROUP
    def onehot(e):
        m = np.zeros((CHUNK_W, LANES), np.float32)
        m[np.arange(CHUNK_W), e] = 1.0
        return m
    return np.stack([
        onehot(CHUNK - 1 - time),
        onehot(time),
        onehot(time + 1),
        onehot(CHUNK - time),
    ])


def _ssm_kernel(u_ref, vec_ref, mat_ref, exp_ref, wg_ref, wu_ref, wd_ref, y_ref, og_ref, ou_ref,
                od_ref, *, seq_chunks):
    og_ref[...] = wg_ref[...].astype(BF16)
    ou_ref[...] = wu_ref[...].astype(BF16)
    od_ref[...] = wd_ref[...].astype(BF16)

    n_ct = u_ref.shape[0]
    nch = n_ct * C_TILE
    half = LANES // 2
    lane = lax.broadcasted_iota(jnp.int32, (1, LANES), 1)
    lo_half = lane < half

    def direction(di):
        a_re = vec_ref[di, 0:1]
        a_im = vec_ref[di, 1:2]
        dt = jnp.exp(vec_ref[di, 2:3])
        mag = jnp.exp(a_re * dt)
        ang = a_im * dt
        lb_re = mag * jnp.cos(ang)
        lb_im = mag * jnp.sin(ang)
        den = a_re * a_re + a_im * a_im
        f_re = ((lb_re - 1.0) * a_re + lb_im * a_im) / den
        f_im = (lb_im * a_re - (lb_re - 1.0) * a_im) / den
        return (lb_re, lb_im), f_re, f_im

    def power_table(lam):
        e = lax.broadcasted_iota(jnp.int32, (LANES, 1), 0)
        sq_re, sq_im = lam
        t_re = jnp.ones((LANES, LANES), F32)
        t_im = jnp.zeros((LANES, LANES), F32)
        for k in range(CHUNK.bit_length()):
            bit = ((e >> k) & 1) == 1
            t_re, t_im = (jnp.where(bit, t_re * sq_re - t_im * sq_im, t_re),
                          jnp.where(bit, t_re * sq_im + t_im * sq_re, t_im))
            sq_re, sq_im = sq_re * sq_re - sq_im * sq_im, 2.0 * sq_re * sq_im
        return jnp.where(lo_half, t_re, t_im), jnp.where(lo_half, t_im, t_re)

    def tile_rows(x16):
        return jnp.broadcast_to(x16[None], (CHUNK, SSM_GROUP, LANES)).reshape(CHUNK_W, LANES)

    def expanded(tab, which, v_re, v_im, conj_sign):
        lexp = _dot(exp_ref[which], jnp.concatenate(tab, axis=1).astype(BF16))
        if conj_sign > 0:
            p = jnp.where(lo_half, v_re, v_re)
            q = jnp.where(lo_half, -v_im, v_im)
        else:
            p = jnp.where(lo_half, v_re, -v_re)
            q = jnp.where(lo_half, -v_im, -v_im)
        return lexp[:, :LANES] * tile_rows(p) + lexp[:, LANES:] * tile_rows(q)

    lam_f, ff_re, ff_im = direction(0)
    lam_b, fb_re, fb_im = direction(1)
    tabs_f = power_table(lam_f)
    tabs_b = power_table(lam_b)
    tab_f, tab_b = tabs_f[0], tabs_b[0]

    def bbar(bt_re, bt_im, f_re, f_im):
        return bt_re * f_re - bt_im * f_im, bt_re * f_im + bt_im * f_re

    def mat(di, k):
        return mat_ref[di, k * SSM_GROUP:(k + 1) * SSM_GROUP, :]

    bf_re, bf_im = bbar(mat(0, 0), mat(0, 1), ff_re, ff_im)
    bb_re, bb_im = bbar(mat(1, 0), mat(1, 1), fb_re, fb_im)

    pb1 = expanded(tabs_f, 0, bf_re, bf_im, 1)
    pb2 = expanded(tabs_b, 1, bb_re, bb_im, 1)
    pb3 = expanded(tabs_b, 2, bb_re, bb_im, 1)
    ft_f = expanded(tabs_f, 2, mat(0, 3), mat(0, 4), -1)
    ft_b = expanded(tabs_b, 3, mat(1, 3), mat(1, 4), -1)

    row_w = lax.broadcasted_iota(jnp.int32, (CHUNK_W, 1), 0)
    last_blk = row_w >= CHUNK_W - SSM_GROUP
    pb2_lag0 = jnp.where(last_blk, pltpu.roll(pb2, CHUNK_W - SSM_GROUP, 0), 0.0)
    ccr_f = mat(0, 2).astype(BF16)
    ccr_b = mat(1, 2).astype(BF16)
    r_lo = _dot_nt(ccr_f, pb1.astype(BF16)) + _dot_nt(ccr_b, pb2_lag0.astype(BF16))
    co = lax.broadcasted_iota(jnp.int32, (SSM_GROUP, CHUNK_W), 0)
    col = lax.broadcasted_iota(jnp.int32, (SSM_GROUP, CHUNK_W), 1)
    r_lo = r_lo + jnp.where(col == CHUNK_W - SSM_GROUP + co, mat(0, 5)[:, 0:1], 0.0)
    r_hi = _dot_nt(ccr_b, pb3.astype(BF16))
    r_t = jnp.concatenate([r_lo, r_hi], axis=1)
    g_t = jnp.concatenate(
        [pltpu.roll(r_t, SSM_GROUP * (tl + 1), 1) for tl in range(T_SUB)], axis=0
    ).astype(BF16)

    u = jnp.concatenate([u_ref[ct].reshape(CHUNK_W, C_TILE) for ct in range(n_ct)],
                        axis=1)
    e_mat = jnp.concatenate([pb1, pb2], axis=1).astype(BF16)
    xend = _dot_tn(e_mat, u)
    lanec = lax.broadcasted_iota(jnp.int32, (1, nch), 1) % seq_chunks
    ns = SSM_STATE

    def scan(re, im, tab, forward):
        lam_col = jnp.transpose(tab[CHUNK:CHUNK + 8, :])[:, 0:1]
        a, b = lam_col[:ns], lam_col[ns:]
        n_steps = int(math.log2(seq_chunks))

        def rolled(v, d):
            if d % LANES == 0:
                return jnp.concatenate([v[:, nch - d:], v[:, :nch - d]], axis=1)
            return pltpu.roll(v, d, 1)

        def shifted(v, d):
            if forward:
                return jnp.where(lanec >= d, rolled(v, d), 0.0)
            return jnp.where(lanec < seq_chunks - d, rolled(v, nch - d), 0.0)

        for k in range(n_steps):
            sr, si = shifted(re, 1 << k), shifted(im, 1 << k)
            re, im = re + (sr * a - si * b), im + (sr * b + si * a)
            a, b = a * a - b * b, 2.0 * a * b
        return shifted(re, 1), shifted(im, 1)

    f_re, f_im = scan(xend[:ns], xend[ns:2 * ns], tab_f, True)
    b_re, b_im = scan(xend[2 * ns:3 * ns], xend[3 * ns:], tab_b, False)
    xin = jnp.concatenate([f_re, f_im, b_re, b_im], axis=0).astype(BF16)
    f_t = jnp.concatenate([ft_f, ft_b], axis=1).astype(BF16)

    toeplitz = jnp.concatenate(
        [g_t[:, CHUNK_W - LANES * th:2 * CHUNK_W - LANES * th] for th in range(CHUNK // T_SUB)],
        axis=0)
    y_t = _dot(toeplitz, u) + _dot(f_t, xin)
    for ct in range(n_ct):
        y_ref[ct] = y_t[:, ct * C_TILE:(ct + 1) * C_TILE].reshape(
            CHUNK, SSM_GROUP, C_TILE).astype(y_ref.dtype)


def _ssm(ut, a_re, a_im, log_dt, b_re, b_im, c_re, c_im, d, seq_chunks, w_gate, w_up, w_down, layer):
    g, n_ct = ut.shape[:2]
    assert N_EXPERTS % g == 0
    epg = N_EXPERTS // g
    base = layer * g
    n = SSM_STATE

    def per_group(a):
        return jnp.swapaxes(a, 0, 1)

    def dup(a):
        return jnp.concatenate([a, a], axis=-1)

    vecs = dup(jnp.stack([per_group(a_re), per_group(a_im),
                          jnp.broadcast_to(per_group(log_dt)[..., None], (g, 2, n))], axis=2))
    cr, ci = per_group(c_re), per_group(c_im)
    d_blk = jnp.broadcast_to(d.reshape(g, 1, SSM_GROUP, 1), (g, 2, SSM_GROUP, LANES))
    mats = jnp.concatenate(
        [dup(jnp.swapaxes(per_group(b_re), 2, 3)), dup(jnp.swapaxes(per_group(b_im), 2, 3)),
         jnp.concatenate([cr, -ci], axis=-1), dup(cr), dup(ci), d_blk], axis=2)
    exp_c = jnp.asarray(_expand_consts(), BF16)

    return pl.pallas_call(
        functools.partial(_ssm_kernel, seq_chunks=seq_chunks),
        grid=(g,),
        in_specs=[
            pl.BlockSpec((None, n_ct, CHUNK, SSM_GROUP, C_TILE), lambda i: (i, 0, 0, 0, 0)),
            pl.BlockSpec((None, 2, 3, LANES), lambda i: (i, 0, 0, 0)),
            pl.BlockSpec((None, 2, 6 * SSM_GROUP, LANES), lambda i: (i, 0, 0, 0)),
            pl.BlockSpec((4, CHUNK_W, LANES), lambda i: (0, 0, 0)),
            pl.BlockSpec((epg, D_MODEL, D_EXPERT), lambda i: (base + i, 0, 0)),
            pl.BlockSpec((epg, D_MODEL, D_EXPERT), lambda i: (base + i, 0, 0)),
            pl.BlockSpec((epg, D_EXPERT, D_MODEL), lambda i: (base + i, 0, 0)),
        ],
        out_specs=[
            pl.BlockSpec((None, n_ct, CHUNK, SSM_GROUP, C_TILE), lambda i: (i, 0, 0, 0, 0)),
            pl.BlockSpec((epg, D_MODEL, D_EXPERT), lambda i: (i, 0, 0)),
            pl.BlockSpec((epg, D_MODEL, D_EXPERT), lambda i: (i, 0, 0)),
            pl.BlockSpec((epg, D_EXPERT, D_MODEL), lambda i: (i, 0, 0)),
        ],
        out_shape=[
            jax.ShapeDtypeStruct(ut.shape, BF16),
            jax.ShapeDtypeStruct((N_EXPERTS, D_MODEL, D_EXPERT), BF16),
            jax.ShapeDtypeStruct((N_EXPERTS, D_MODEL, D_EXPERT), BF16),
            jax.ShapeDtypeStruct((N_EXPERTS, D_EXPERT, D_MODEL), BF16),
        ],
        compiler_params=_params(("parallel",)),
        name="ssm",
    )(ut, vecs, mats, exp_c, w_gate, w_up, w_down)


def _mix_out_kernel(x_hbm, a_ref, yt_ref, gwt_ref, gb_ref, wo_ref, h_hbm, xbuf, hbuf, xsem, hsem):
    nc = C_TILE
    x = _load_time_major_rows(x_hbm, xbuf, xsem)
    a = a_ref[...].reshape(S_TILE * nc, D_POOL)
    y = jnp.concatenate([yt_ref[:, j, :, :].reshape(D_SSM, nc) for j in range(S_TILE)],
                        axis=1).astype(F32)
    z = 0.5 * y * (1.0 + jnp.tanh(math.sqrt(2.0 / math.pi) * (y + 0.044715 * (y * y * y))))
    gate = _sigmoid(_dot(gwt_ref[...], z.astype(BF16)) + gb_ref[...])
    s = (z * gate).astype(BF16)
    h = x + _dot(a, wo_ref[:D_POOL, :]) + _dot_tn(s, wo_ref[D_POOL:, :])
    _store_time_major_rows(h, h_hbm, hbuf, hsem)


def _mix_out(x3, a3, yt, glu_w_t, glu_b_col, w_out):
    nch = x3.shape[0]
    return pl.pallas_call(
        _mix_out_kernel,
        grid=(nch // C_TILE, CHUNK // S_TILE),
        in_specs=[
            pl.BlockSpec(memory_space=pl.ANY),
            pl.BlockSpec((S_TILE, C_TILE, D_POOL), lambda c, t: (t, c, 0)),
            pl.BlockSpec((N_SSM_GROUPS, None, S_TILE, SSM_GROUP, C_TILE), lambda c, t: (0, c, t, 0, 0)),
            pl.BlockSpec((D_SSM, D_SSM), lambda c, t: (0, 0)),
            pl.BlockSpec((D_SSM, 1), lambda c, t: (0, 0)),
            pl.BlockSpec((D_MODEL, D_MODEL), lambda c, t: (0, 0)),
        ],
        out_specs=pl.BlockSpec(memory_space=pl.ANY),
        out_shape=jax.ShapeDtypeStruct((nch, CHUNK, D_MODEL), F32),
        scratch_shapes=[
            pltpu.VMEM((2, S_TILE, C_TILE, D_MODEL), F32),
            pltpu.VMEM((2, S_TILE, C_TILE, D_MODEL), F32),
            pltpu.SemaphoreType.DMA((2,)),
            pltpu.SemaphoreType.DMA((2,)),
        ],
        compiler_params=_params(("arbitrary", "arbitrary")),
        name="mix_out",
    )(x3, a3, yt, glu_w_t, glu_b_col, w_out)


def _pack_rows(x):
    b = lax.bitcast_convert_type(x.astype(BF16).astype(F32), U32)
    return (b[:, :HALF] & jnp.uint32(0xFFFF0000)) | (b[:, HALF:] >> 16)


def _unpack_rows(w):
    lo = lax.bitcast_convert_type(w & jnp.uint32(0xFFFF0000), F32)
    hi = lax.bitcast_convert_type(w << 16, F32)
    return lo, hi


def _split_bf16(x):
    hi = x.astype(BF16)
    return hi, (x - hi.astype(F32)).astype(BF16)


def _route(v32, wr_ref, br_ref):
    v_hi, v_lo = _split_bf16(v32)
    w_hi, w_lo = _split_bf16(wr_ref[...])
    both = _dot(v_hi, jnp.concatenate([w_hi, w_lo], axis=1))
    logits = both[:, :ROUTER_W] + (both[:, ROUTER_W:] + _dot(v_lo, w_hi)) + br_ref[...]
    return jnp.transpose(logits)


def _top1(x, valid=None):
    n = x.shape[0]
    row = lax.broadcasted_iota(jnp.int32, x.shape, 0).astype(F32)
    if valid is not None:
        x = jnp.where(valid, x, -jnp.inf)
    m = jnp.max(x, axis=0, keepdims=True)
    idx = jnp.min(jnp.where(x == m, row, float(n)), axis=0, keepdims=True)
    return m, idx, x, row


def _split_planes(packed, ref):
    ref[0] = packed[:, :SC_ROW]
    ref[1] = packed[:, SC_ROW:]


def _router_kernel(h_ref, g_ref, wr_ref, br_ref, before_ref, vp_ref, meta_t_ref, cnt_ref, carry_ref):
    @pl.when(pl.program_id(0) == 0)
    def _():
        carry_ref[...] = jnp.zeros_like(carry_ref)

    v32 = _rms(h_ref[...], g_ref[...])
    _split_planes(_pack_rows(v32), vp_ref)
    lt = _route(v32, wr_ref, br_ref)
    tm = lt.shape[1]
    eg = EXPERTS_PER_GROUP

    grp = lt[:eg]
    grp_row = lax.broadcasted_iota(jnp.int32, grp.shape, 0)
    mg, grp_idx, grp, _ = _top1(grp, grp_row < N_EXPERT_GROUPS)
    grp_p = 1.0 / jnp.sum(jnp.exp(grp - mg), axis=0, keepdims=True)
    le = jnp.zeros((eg, tm), F32)
    for g in range(N_EXPERT_GROUPS):
        le = jnp.where(grp_idx == float(g), lt[eg * (g + 1):eg * (g + 2)], le)
    m1, i1, le, row = _top1(le)
    z = jnp.sum(jnp.exp(le - m1), axis=0, keepdims=True)
    m2, i2, _, _ = _top1(jnp.where(row == i1, -jnp.inf, le))
    p1 = 1.0 / z
    p2 = jnp.exp(m2 - m1) / z
    tot = p1 + p2
    w1 = grp_p * (p1 / tot)
    w2 = grp_p * (p2 / tot)
    e1 = grp_idx * eg + i1
    e2 = grp_idx * eg + i2

    erow = lax.broadcasted_iota(jnp.int32, (N_EXPERTS, tm), 0).astype(F32)
    onehot = jnp.where(erow == e1, 1.0, jnp.where(erow == e2, 1.0, 0.0))
    before = _dot(onehot.astype(BF16), before_ref[...]) + carry_ref[...]
    rank1 = jnp.sum(jnp.where(erow == e1, before, 0.0), axis=0, keepdims=True)
    rank2 = jnp.sum(jnp.where(erow == e2, before, 0.0), axis=0, keepdims=True)
    carry = carry_ref[...] + jnp.sum(onehot, axis=1, keepdims=True)
    carry_ref[...] = carry
    cnt_ref[...] = carry

    mrow = lax.broadcasted_iota(jnp.int32, (META_ROWS, tm), 0)
    meta_t_ref[...] = jnp.where(mrow == 0, e1, jnp.where(mrow == 1, e2, jnp.where(
        mrow == 2, rank1, jnp.where(mrow == 3, rank2, jnp.where(
            mrow == 4, w1, jnp.where(mrow == 5, w2, 0.0))))))


def _earlier_matrix(tm):
    return np.triu(np.ones((tm, tm), np.float32), k=1)


def _router(h1, g_ffn, w_router, b_router, tm):
    t = h1.shape[0]
    return pl.pallas_call(
        _router_kernel,
        grid=(t // tm,),
        in_specs=[
            pl.BlockSpec((tm, D_MODEL), lambda i: (i, 0)),
            pl.BlockSpec((1, D_MODEL), lambda i: (0, 0)),
            pl.BlockSpec((D_MODEL, ROUTER_W), lambda i: (0, 0)),
            pl.BlockSpec((1, ROUTER_W), lambda i: (0, 0)),
            pl.BlockSpec((tm, tm), lambda i: (0, 0)),
        ],
        out_specs=[
            pl.BlockSpec((2, tm, SC_ROW), lambda i: (0, i, 0)),
            pl.BlockSpec((META_ROWS, tm), lambda i: (0, i)),
            pl.BlockSpec((N_EXPERTS, 1), lambda i: (0, 0)),
        ],
        out_shape=[
            jax.ShapeDtypeStruct((2, t, SC_ROW), U32),
            jax.ShapeDtypeStruct((META_ROWS, t), F32),
            jax.ShapeDtypeStruct((N_EXPERTS, 1), F32),
        ],
        scratch_shapes=[pltpu.VMEM((N_EXPERTS, 1), F32)],
        compiler_params=_params(("arbitrary",)),
        name="router",
    )(h1, g_ffn, w_router, b_router, jnp.asarray(_earlier_matrix(tm), BF16))


def _plan(meta_t, counts, n_tiles):
    e1 = meta_t[0].astype(jnp.int32)
    e2 = meta_t[1].astype(jnp.int32)
    rank1 = meta_t[2].astype(jnp.int32)
    rank2 = meta_t[3].astype(jnp.int32)
    cnt = counts[:, 0].astype(jnp.int32)
    padded = ((cnt + ROW_TILE - 1) // ROW_TILE) * ROW_TILE
    ends = jnp.cumsum(padded)
    starts = ends - padded
    experts = jnp.arange(N_EXPERTS, dtype=jnp.int32)
    pos1 = rank1 + jnp.sum(jnp.where(e1[None, :] == experts[:, None], starts[:, None], 0), axis=0)
    pos2 = rank2 + jnp.sum(jnp.where(e2[None, :] == experts[:, None], starts[:, None], 0), axis=0)
    tile_start = jnp.arange(n_tiles, dtype=jnp.int32) * ROW_TILE
    tile_expert = jnp.sum((tile_start[:, None] >= ends[None, :]).astype(jnp.int32), axis=1)
    tile_expert = jnp.minimum(tile_expert, N_EXPERTS - 1)
    rows_left = jnp.sum(jnp.where(tile_expert[:, None] == experts, cnt + starts, 0), axis=1) - tile_start
    n_valid = jnp.clip(rows_left, 0, ROW_TILE).astype(jnp.int32)
    last_used = jnp.maximum(ends[-1] // ROW_TILE - 1, 0)
    block = jnp.minimum(jnp.arange(n_tiles, dtype=jnp.int32), last_used)
    tile_expert = jnp.sum(jnp.where(block[:, None] == jnp.arange(n_tiles)[None, :],
                                    tile_expert[None, :], 0), axis=1)
    plane = n_tiles * ROW_TILE
    half_rows = jnp.concatenate([pos1, pos1 + plane, pos2, pos2 + plane])[None]
    return half_rows, tile_expert, n_valid, block


def _sc_mesh():
    return plsc.VectorSubcoreMesh(core_axis_name="c", subcore_axis_name="s")


def _sc_scatter_rows(rows, idx, n_out):
    t, width = rows.shape
    steps = t // SC_WINDOW
    half = steps // 2

    @pl.kernel(out_type=jax.ShapeDtypeStruct((n_out, width), rows.dtype), mesh=_sc_mesh(),
               scratch_types=[], name="moe_scatter")
    def scatter(rows_hbm, idx_hbm, out_hbm):
        def body(rows_vmem, idx0_vmem, idx1_vmem):
            pltpu.sync_copy(rows_vmem, out_hbm.at[idx0_vmem.at[0]])
            pltpu.sync_copy(rows_vmem, out_hbm.at[idx1_vmem.at[0]])

        pltpu.emit_pipeline(
            body,
            grid=(2, half),
            in_specs=[pl.BlockSpec((SC_WINDOW, width), lambda c, j: (c * half + j, 0)),
                      pl.BlockSpec((1, SC_WINDOW), lambda c, j: (0, c * half + j)),
                      pl.BlockSpec((1, SC_WINDOW), lambda c, j: (0, steps + c * half + j))],
            out_specs=[],
            core_axis_name=("c", "s"),
            dimension_semantics=(pltpu.PARALLEL, pltpu.PARALLEL),
        )(rows_hbm, idx_hbm, idx_hbm)

    return scatter(rows, idx)


def _sc_gather_rows(table, idx):
    m = idx.shape[1]
    width = table.shape[1]
    steps = m // (2 * SC_WINDOW)

    @pl.kernel(out_type=jax.ShapeDtypeStruct((m, width), table.dtype), mesh=_sc_mesh(),
               scratch_types=[], name="moe_gather")
    def gather(table_hbm, idx_hbm, out_hbm):
        def body(idx_vmem, out_vmem):
            pltpu.sync_copy(table_hbm.at[idx_vmem.at[0]], out_vmem)

        pltpu.emit_pipeline(
            body,
            grid=(2, steps),
            in_specs=[pl.BlockSpec((1, SC_WINDOW), lambda k, j: (0, k * steps + j))],
            out_specs=[pl.BlockSpec((SC_WINDOW, width), lambda k, j: (k * steps + j, 0))],
            core_axis_name=("c", "s"),
            dimension_semantics=(pltpu.PARALLEL, pltpu.PARALLEL),
        )(idx_hbm, out_hbm)

    return gather(table, idx)


def _experts_kernel(te_ref, nv_ref, blk_ref, xs_ref, wg_ref, wu_ref, wd_ref, ys_ref):
    r = pl.program_id(0)
    n_valid = nv_ref[r]

    @pl.when(n_valid > 0)
    def _():
        parts = [p.astype(BF16) for p in _unpack_rows(xs_ref[0]) + _unpack_rows(xs_ref[1])]
        cols = (0, 2 * SC_ROW, SC_ROW, 3 * SC_ROW)
        hg = sum(_dot(p, wg_ref[c:c + SC_ROW, :]) for p, c in zip(parts, cols))
        hu = sum(_dot(p, wu_ref[c:c + SC_ROW, :]) for p, c in zip(parts, cols))
        row = lax.broadcasted_iota(jnp.int32, (ROW_TILE, 1), 0)
        hid = jnp.where(row < n_valid, hg * _sigmoid(hg) * hu, 0.0).astype(BF16)
        _split_planes(_pack_rows(_dot(hid, wd_ref[...])), ys_ref)


def _experts(xs, tile_expert, n_valid, block, w_gate, w_up, w_down):
    n_tiles = xs.shape[1] // ROW_TILE
    w_spec = pl.BlockSpec((None, D_MODEL, D_EXPERT), lambda r, te, nv, blk: (te[r], 0, 0))
    grid_spec = pltpu.PrefetchScalarGridSpec(
        num_scalar_prefetch=3,
        grid=(n_tiles,),
        in_specs=[
            pl.BlockSpec((2, ROW_TILE, SC_ROW), lambda r, te, nv, blk: (0, blk[r], 0)),
            w_spec, w_spec,
            pl.BlockSpec((None, D_EXPERT, D_MODEL), lambda r, te, nv, blk: (te[r], 0, 0)),
        ],
        out_specs=pl.BlockSpec((2, ROW_TILE, SC_ROW), lambda r, te, nv, blk: (0, blk[r], 0)),
    )
    return pl.pallas_call(
        _experts_kernel,
        grid_spec=grid_spec,
        out_shape=jax.ShapeDtypeStruct((2, n_tiles * ROW_TILE, SC_ROW), U32),
        compiler_params=_params(("arbitrary",)),
        name="experts",
    )(tile_expert, n_valid, block, xs, w_gate, w_up, w_down)


def _ple_kernel(h_ref, yg_ref, meta_t_ref, p_ref, wg_ref, bg_ref, wp_ref, gf_ref, *rest, final_norm):
    o_ref = rest[-1]
    meta = jnp.transpose(meta_t_ref[...])
    w1 = meta[:, 4:5]
    w2 = meta[:, 5:6]
    q0, q2 = (w1 * u + w2 * v for u, v in zip(_unpack_rows(yg_ref[0]), _unpack_rows(yg_ref[2])))
    q1, q3 = (w1 * u + w2 * v for u, v in zip(_unpack_rows(yg_ref[1]), _unpack_rows(yg_ref[3])))
    moe = jnp.concatenate([q0, q1, q2, q3], axis=1)
    h = h_ref[...] + moe
    gate = _sigmoid(_dot(_rms(h).astype(BF16), wg_ref[...]) + bg_ref[...])
    h = h + gate * _dot(p_ref[...].astype(BF16), wp_ref[...])
    o_ref[...] = _rms(h, gf_ref[...]) if final_norm else h


def _ple(h1, yg, meta_t, p2, w_gate, b_gate, w_proj, g_final, final_norm, tm, part, prev_out):
    t = h1.shape[0]
    steps = t // tm // GATHER_PARTS
    off = part * steps
    in_specs = [
        pl.BlockSpec((tm, D_MODEL), lambda i: (i + off, 0)),
        pl.BlockSpec((4, tm, SC_ROW), lambda i: (0, i, 0)),
        pl.BlockSpec((META_ROWS, tm), lambda i: (0, i + off)),
        pl.BlockSpec((tm, D_PLE), lambda i: (i + off, 0)),
        pl.BlockSpec((D_MODEL, D_MODEL), lambda i: (0, 0)),
        pl.BlockSpec((1, D_MODEL), lambda i: (0, 0)),
        pl.BlockSpec((D_PLE, D_MODEL), lambda i: (0, 0)),
        pl.BlockSpec((1, D_MODEL), lambda i: (0, 0)),
    ]
    args = [h1, yg, meta_t, p2, w_gate, b_gate, w_proj, g_final]
    aliases = {}
    if prev_out is not None:
        in_specs.append(pl.BlockSpec(memory_space=pl.ANY))
        args.append(prev_out)
        aliases = {len(args) - 1: 0}
    return pl.pallas_call(
        functools.partial(_ple_kernel, final_norm=final_norm),
        grid=(steps,),
        in_specs=in_specs,
        out_specs=pl.BlockSpec((tm, D_MODEL), lambda i: (i + off, 0)),
        out_shape=jax.ShapeDtypeStruct((t, D_MODEL), F32),
        input_output_aliases=aliases,
        compiler_params=_params(("parallel",)),
        name="ple",
    )(*args)


def kernel(x, p, g_mix, w_in, pool_w, pool_scale, ssm_a_re, ssm_a_im, ssm_log_dt, ssm_b_re,
           ssm_b_im, ssm_c_re, ssm_c_im, ssm_d, glu_w, glu_b, w_out, g_ffn, router_grp_w,
           router_grp_b, router_exp_w, router_exp_b, exp_w_gate, exp_w_up, exp_w_down, g_ple,
           ple_gate_w, ple_gate_b, ple_proj_w, g_final):
    bsz, seq, dm = x.shape
    depth = g_mix.shape[0]
    t = bsz * seq
    seq_chunks = seq // CHUNK
    nch = t // CHUNK
    tm = 1024
    n_sorted = (pl.cdiv(2 * t, ROW_TILE) + N_EXPERTS) * ROW_TILE
    w_gate_all = exp_w_gate.reshape(depth * N_EXPERTS, dm, D_EXPERT)
    w_up_all = exp_w_up.reshape(depth * N_EXPERTS, dm, D_EXPERT)
    w_down_all = exp_w_down.reshape(depth * N_EXPERTS, D_EXPERT, dm)

    h = x.reshape(t, dm)
    for i in range(depth):
        w_in_b = (g_mix[i][:, None] * w_in[i]).astype(BF16)
        zp, ut = _in_proj(h.reshape(nch, CHUNK, dm), w_in_b[:, :D_POOL],
                          jnp.transpose(w_in_b[:, D_POOL:]))
        a = _pool(zp, pool_w[i], pool_scale[i][None], bsz)
        yt, w_gate_b, w_up_b, w_down_b = _ssm(
            ut, ssm_a_re[i], ssm_a_im[i], ssm_log_dt[i], ssm_b_re[i], ssm_b_im[i], ssm_c_re[i],
            ssm_c_im[i], ssm_d[i], seq_chunks, w_gate_all, w_up_all, w_down_all, i)
        h = _mix_out(h.reshape(nch, CHUNK, dm), a, yt,
                     jnp.transpose(glu_w[i]).astype(BF16), glu_b[i][:, None],
                     w_out[i].astype(BF16)).reshape(t, dm)

        eg = EXPERTS_PER_GROUP
        w_router = jnp.concatenate(
            [router_grp_w[i], jnp.zeros((dm, eg - N_EXPERT_GROUPS), F32),
             jnp.transpose(router_exp_w[i], (1, 0, 2)).reshape(dm, N_EXPERTS),
             jnp.zeros((dm, ROUTER_W - eg - N_EXPERTS), F32)], axis=1)
        b_router = jnp.concatenate(
            [router_grp_b[i], jnp.zeros((eg - N_EXPERT_GROUPS,), F32),
             router_exp_b[i].reshape(N_EXPERTS),
             jnp.zeros((ROUTER_W - eg - N_EXPERTS,), F32)])[None]
        vp, meta_t, counts = _router(h, g_ffn[i][None], w_router, b_router, tm)
        idx, tile_expert, n_valid, block = _plan(meta_t, counts, n_sorted // ROW_TILE)
        xs = _sc_scatter_rows(vp.reshape(2 * t, SC_ROW), idx, 2 * n_sorted)
        ys = _experts(xs.reshape(2, n_sorted, SC_ROW), tile_expert, n_valid, block,
                      w_gate_b, w_up_b, w_down_b)
        ys2 = ys.reshape(2 * n_sorted, SC_ROW)
        idx4 = idx.reshape(4, t)
        tp = t // GATHER_PARTS
        ple_wg = (g_ple[i][:, None] * ple_gate_w[i]).astype(BF16)
        ple_wp = ple_proj_w[i].astype(BF16)
        out = None
        for q in range(GATHER_PARTS):
            yg_q = _sc_gather_rows(ys2, idx4[:, q * tp:(q + 1) * tp].reshape(1, 4 * tp))
            out = _ple(h, yg_q.reshape(4, tp, SC_ROW), meta_t, p[i].reshape(t, D_PLE), ple_wg,
                       ple_gate_b[i][None], ple_wp, g_final[None], i == depth - 1, tm, q, out)
        h = out
    return h.reshape(bsz, seq, dm)
```

```python
import functools
import math

import numpy as np
import jax
import jax.numpy as jnp
from jax import lax
from jax.experimental import pallas as pl
from jax.experimental.pallas import tpu as pltpu
from jax.experimental.pallas import tpu_sc as plsc

F32 = jnp.float32
BF16 = jnp.bfloat16
U32 = jnp.uint32

D_MODEL = 1024
D_POOL = 512
D_SSM = 512
POOL_WINDOWS = (2, 4, 8, 16)
POOL_GROUP = 128
SSM_GROUP = 16
N_SSM_GROUPS = 32
SSM_STATE = 64
N_EXPERT_GROUPS = 4
EXPERTS_PER_GROUP = 8
N_EXPERTS = N_EXPERT_GROUPS * EXPERTS_PER_GROUP
D_EXPERT = 256
D_PLE = 256
RMS_EPS = 1e-6

LANES = 128
CHUNK = 32
CHUNK_W = CHUNK * SSM_GROUP
T_SUB = 8
S_TILE = 8
C_TILE = 128
ROUTER_W = LANES
HALF = D_MODEL // 2
ROW_TILE = 1280
ROW_BLOCK = 256
SC_WINDOW = 128
SC_ROW = HALF // 2
META_ROWS = 8
GATHER_PARTS = 2
VMEM_LIMIT = 56 * 1024 * 1024


def _dot(a, b):
    return jnp.dot(a, b, preferred_element_type=F32)


def _dot_nt(a, b):
    return lax.dot_general(a, b, (((1,), (1,)), ((), ())), preferred_element_type=F32)


def _dot_tn(a, b):
    return lax.dot_general(a, b, (((0,), (0,)), ((), ())), preferred_element_type=F32)


def _rms(x, g=None):
    y = x * lax.rsqrt(jnp.mean(x * x, axis=-1, keepdims=True) + RMS_EPS)
    return y if g is None else y * g


def _sigmoid(x):
    return 1.0 / (1.0 + jnp.exp(-x))


def _params(sem):
    return pltpu.CompilerParams(dimension_semantics=sem, vmem_limit_bytes=VMEM_LIMIT)


def _tile_step():
    n_s = pl.num_programs(1)
    return pl.program_id(0) * n_s + pl.program_id(1), pl.num_programs(0) * n_s


def _row_copies(hbm, buf, sem, step_idx, slot_idx, to_hbm):
    n_s = pl.num_programs(1)
    c0 = (step_idx // n_s) * C_TILE
    s0 = (step_idx % n_s) * S_TILE
    out = []
    for j in range(S_TILE):
        far, near = hbm.at[pl.ds(c0, C_TILE), s0 + j, :], buf.at[slot_idx, j]
        src, dst = (near, far) if to_hbm else (far, near)
        out.append(pltpu.make_async_copy(src, dst, sem.at[slot_idx]))
    return out


def _load_time_major_rows(hbm, buf, sem):
    step, n_steps = _tile_step()
    slot = step % 2

    @pl.when(step == 0)
    def _():
        for cp in _row_copies(hbm, buf, sem, step, slot, False):
            cp.start()

    @pl.when(step + 1 < n_steps)
    def _():
        for cp in _row_copies(hbm, buf, sem, step + 1, 1 - slot, False):
            cp.start()

    for cp in _row_copies(hbm, buf, sem, step, slot, False):
        cp.wait()
    return buf[slot].reshape(S_TILE * C_TILE, buf.shape[-1])


def _store_time_major_rows(val, hbm, buf, sem):
    step, n_steps = _tile_step()
    slot = step % 2

    @pl.when(step >= 2)
    def _():
        for cp in _row_copies(hbm, buf, sem, step - 2, slot, True):
            cp.wait()

    buf[slot] = val.reshape(S_TILE, C_TILE, val.shape[-1])
    for cp in _row_copies(hbm, buf, sem, step, slot, True):
        cp.start()

    @pl.when(step == n_steps - 1)
    def _():
        @pl.when(step >= 1)
        def _():
            for cp in _row_copies(hbm, buf, sem, step - 1, 1 - slot, True):
                cp.wait()
        for cp in _row_copies(hbm, buf, sem, step, slot, True):
            cp.wait()


def _in_proj_kernel(x_hbm, wp_ref, wst_ref, zp_ref, ut_ref, xbuf, xsem):
    nc = C_TILE
    u = _rms(_load_time_major_rows(x_hbm, xbuf, xsem)).astype(BF16)
    zp_ref[...] = _dot(u, wp_ref[...]).reshape(zp_ref.shape).astype(zp_ref.dtype)
    zt = _dot_nt(wst_ref[...], u).astype(BF16)
    for j in range(S_TILE):
        ut_ref[:, j, :, :] = zt[:, j * nc:(j + 1) * nc].reshape(N_SSM_GROUPS, SSM_GROUP, nc)


def _in_proj(x3, w_pool, w_ssm_t):
    nch = x3.shape[0]
    return pl.pallas_call(
        _in_proj_kernel,
        grid=(nch // C_TILE, CHUNK // S_TILE),
        in_specs=[
            pl.BlockSpec(memory_space=pl.ANY),
            pl.BlockSpec((D_MODEL, D_POOL), lambda c, s: (0, 0)),
            pl.BlockSpec((D_SSM, D_MODEL), lambda c, s: (0, 0)),
        ],
        out_specs=[
            pl.BlockSpec((S_TILE, C_TILE, D_POOL), lambda c, s: (s, c, 0)),
            pl.BlockSpec((N_SSM_GROUPS, None, S_TILE, SSM_GROUP, C_TILE), lambda c, s: (0, c, s, 0, 0)),
        ],
        out_shape=[
            jax.ShapeDtypeStruct((CHUNK, nch, D_POOL), BF16),
            jax.ShapeDtypeStruct((N_SSM_GROUPS, nch // C_TILE, CHUNK, SSM_GROUP, C_TILE), BF16),
        ],
        scratch_shapes=[
            pltpu.VMEM((2, S_TILE, C_TILE, D_MODEL), F32),
            pltpu.SemaphoreType.DMA((2,)),
        ],
        compiler_params=_params(("arbitrary", "arbitrary")),
        name="in_proj",
    )(x3, w_pool, w_ssm_t)


def _pool_kernel(z_ref, w_ref, sc_ref, o_ref):
    n_c = z_ref.shape[1]
    gi = pl.program_id(1)
    chunk = lax.broadcasted_iota(jnp.int32, (n_c, 1), 0)

    def plane(p):
        if p < 0:
            return jnp.where(chunk >= 1, pltpu.roll(z_ref[p + CHUNK].astype(F32), 1, 0), 0.0)
        if p >= CHUNK:
            return jnp.where(chunk < n_c - 1, pltpu.roll(z_ref[p - CHUNK].astype(F32), n_c - 1, 0), 0.0)
        return z_ref[p].astype(F32)

    for k, w in enumerate(POOL_WINDOWS):
        @pl.when(gi == k)
        def _(w=w):
            half = w // 2
            total = plane(-half)
            for p in range(-half + 1, half):
                total = total + plane(p)
            diffs = []
            for s_t in range(CHUNK):
                t = chunk * CHUNK + s_t
                cnt = (jnp.minimum(t + half, n_c * CHUNK) - jnp.maximum(t - half, 0)).astype(F32)
                diffs.append((total / cnt - plane(s_t)).astype(BF16))
                if s_t + 1 < CHUNK:
                    total = total + plane(s_t + half) - plane(s_t - half)
            diff = jnp.concatenate(diffs, axis=0)
            out = _dot(diff, w_ref[...].astype(BF16)) * sc_ref[...]
            o_ref[...] = out.reshape(o_ref.shape).astype(o_ref.dtype)


def _pool(zp_t, pool_w, pool_scale, bsz):
    _, nch, _ = zp_t.shape
    n_c = nch // bsz
    return pl.pallas_call(
        _pool_kernel,
        grid=(bsz, len(POOL_WINDOWS)),
        in_specs=[
            pl.BlockSpec((CHUNK, n_c, POOL_GROUP), lambda i, g: (0, i, g)),
            pl.BlockSpec((None, POOL_GROUP, POOL_GROUP), lambda i, g: (g, 0, 0)),
            pl.BlockSpec((1, POOL_GROUP), lambda i, g: (0, g)),
        ],
        out_specs=pl.BlockSpec((CHUNK, n_c, POOL_GROUP), lambda i, g: (0, i, g)),
        out_shape=jax.ShapeDtypeStruct(zp_t.shape, BF16),
        compiler_params=_params(("parallel", "parallel")),
        name="pool",
    )(zp_t, pool_w, pool_scale)


def _expand_consts():
    time = np.arange(CHUNK_W) // SSM_GROUP
    def onehot(e):
        m = np.zeros((CHUNK_W, LANES), np.float32)
        m[np.arange(CHUNK_W), e] = 1.0
        return m
    return np.stack([
        onehot(CHUNK - 1 - time),
        onehot(time),
        onehot(time + 1),
        onehot(CHUNK - time),
    ])


def _ssm_kernel(u_ref, vec_ref, mat_ref, exp_ref, wg_ref, wu_ref, wd_ref, y_ref, og_ref, ou_ref,
                od_ref, *, seq_chunks):
    og_ref[...] = wg_ref[...].astype(BF16)
    ou_ref[...] = wu_ref[...].astype(BF16)
    od_ref[...] = wd_ref[...].astype(BF16)

    n_ct = u_ref.shape[0]
    nch = n_ct * C_TILE
    half = LANES // 2
    lane = lax.broadcasted_iota(jnp.int32, (1, LANES), 1)
    lo_half = lane < half

    def direction(di):
        a_re = vec_ref[di, 0:1]
        a_im = vec_ref[di, 1:2]
        dt = jnp.exp(vec_ref[di, 2:3])
        mag = jnp.exp(a_re * dt)
        ang = a_im * dt
        lb_re = mag * jnp.cos(ang)
        lb_im = mag * jnp.sin(ang)
        den = a_re * a_re + a_im * a_im
        f_re = ((lb_re - 1.0) * a_re + lb_im * a_im) / den
        f_im = (lb_im * a_re - (lb_re - 1.0) * a_im) / den
        return (lb_re, lb_im), f_re, f_im

    def power_table(lam):
        e = lax.broadcasted_iota(jnp.int32, (LANES, 1), 0)
        sq_re, sq_im = lam
        t_re = jnp.ones((LANES, LANES), F32)
        t_im = jnp.zeros((LANES, LANES), F32)
        for k in range(CHUNK.bit_length()):
            bit = ((e >> k) & 1) == 1
            t_re, t_im = (jnp.where(bit, t_re * sq_re - t_im * sq_im, t_re),
                          jnp.where(bit, t_re * sq_im + t_im * sq_re, t_im))
            sq_re, sq_im = sq_re * sq_re - sq_im * sq_im, 2.0 * sq_re * sq_im
        return jnp.where(lo_half, t_re, t_im), jnp.where(lo_half, t_im, t_re)

    def tile_rows(x16):
        return jnp.broadcast_to(x16[None], (CHUNK, SSM_GROUP, LANES)).reshape(CHUNK_W, LANES)

    def expanded(tab, which, v_re, v_im, conj_sign):
        lexp = _dot(exp_ref[which], jnp.concatenate(tab, axis=1).astype(BF16))
        if conj_sign > 0:
            p = jnp.where(lo_half, v_re, v_re)
            q = jnp.where(lo_half, -v_im, v_im)
        else:
            p = jnp.where(lo_half, v_re, -v_re)
            q = jnp.where(lo_half, -v_im, -v_im)
        return lexp[:, :LANES] * tile_rows(p) + lexp[:, LANES:] * tile_rows(q)

    lam_f, ff_re, ff_im = direction(0)
    lam_b, fb_re, fb_im = direction(1)
    tabs_f = power_table(lam_f)
    tabs_b = power_table(lam_b)
    tab_f, tab_b = tabs_f[0], tabs_b[0]

    def bbar(bt_re, bt_im, f_re, f_im):
        return bt_re * f_re - bt_im * f_im, bt_re * f_im + bt_im * f_re

    def mat(di, k):
        return mat_ref[di, k * SSM_GROUP:(k + 1) * SSM_GROUP, :]

    bf_re, bf_im = bbar(mat(0, 0), mat(0, 1), ff_re, ff_im)
    bb_re, bb_im = bbar(mat(1, 0), mat(1, 1), fb_re, fb_im)

    pb1 = expanded(tabs_f, 0, bf_re, bf_im, 1)
    pb2 = expanded(tabs_b, 1, bb_re, bb_im, 1)
    pb3 = expanded(tabs_b, 2, bb_re, bb_im, 1)
    ft_f = expanded(tabs_f, 2, mat(0, 3), mat(0, 4), -1)
    ft_b = expanded(tabs_b, 3, mat(1, 3), mat(1, 4), -1)

    row_w = lax.broadcasted_iota(jnp.int32, (CHUNK_W, 1), 0)
    last_blk = row_w >= CHUNK_W - SSM_GROUP
    pb2_lag0 = jnp.where(last_blk, pltpu.roll(pb2, CHUNK_W - SSM_GROUP, 0), 0.0)
    ccr_f = mat(0, 2).astype(BF16)
    ccr_b = mat(1, 2).astype(BF16)
    r_lo = _dot_nt(ccr_f, pb1.astype(BF16)) + _dot_nt(ccr_b, pb2_lag0.astype(BF16))
    co = lax.broadcasted_iota(jnp.int32, (SSM_GROUP, CHUNK_W), 0)
    col = lax.broadcasted_iota(jnp.int32, (SSM_GROUP, CHUNK_W), 1)
    r_lo = r_lo + jnp.where(col == CHUNK_W - SSM_GROUP + co, mat(0, 5)[:, 0:1], 0.0)
    r_hi = _dot_nt(ccr_b, pb3.astype(BF16))
    r_t = jnp.concatenate([r_lo, r_hi], axis=1)
    g_t = jnp.concatenate(
        [pltpu.roll(r_t, SSM_GROUP * (tl + 1), 1) for tl in range(T_SUB)], axis=0
    ).astype(BF16)

    u = jnp.concatenate([u_ref[ct].reshape(CHUNK_W, C_TILE) for ct in range(n_ct)],
                        axis=1)
    e_mat = jnp.concatenate([pb1, pb2], axis=1).astype(BF16)
    xend = _dot_tn(e_mat, u)
    lanec = lax.broadcasted_iota(jnp.int32, (1, nch), 1) % seq_chunks
    ns = SSM_STATE

    def scan(re, im, tab, forward):
        lam_col = jnp.transpose(tab[CHUNK:CHUNK + 8, :])[:, 0:1]
        a, b = lam_col[:ns], lam_col[ns:]
        n_steps = int(math.log2(seq_chunks))

        def rolled(v, d):
            if d % LANES == 0:
                return jnp.concatenate([v[:, nch - d:], v[:, :nch - d]], axis=1)
            return pltpu.roll(v, d, 1)

        def shifted(v, d):
            if forward:
                return jnp.where(lanec >= d, rolled(v, d), 0.0)
            return jnp.where(lanec < seq_chunks - d, rolled(v, nch - d), 0.0)

        for k in range(n_steps):
            sr, si = shifted(re, 1 << k), shifted(im, 1 << k)
            re, im = re + (sr * a - si * b), im + (sr * b + si * a)
            a, b = a * a - b * b, 2.0 * a * b
        return shifted(re, 1), shifted(im, 1)

    f_re, f_im = scan(xend[:ns], xend[ns:2 * ns], tab_f, True)
    b_re, b_im = scan(xend[2 * ns:3 * ns], xend[3 * ns:], tab_b, False)
    xin = jnp.concatenate([f_re, f_im, b_re, b_im], axis=0).astype(BF16)
    f_t = jnp.concatenate([ft_f, ft_b], axis=1).astype(BF16)

    toeplitz = jnp.concatenate(
        [g_t[:, CHUNK_W - LANES * th:2 * CHUNK_W - LANES * th] for th in range(CHUNK // T_SUB)],
        axis=0)
    y_t = _dot(toeplitz, u) + _dot(f_t, xin)
    for ct in range(n_ct):
        y_ref[ct] = y_t[:, ct * C_TILE:(ct + 1) * C_TILE].reshape(
            CHUNK, SSM_GROUP, C_TILE).astype(y_ref.dtype)


def _ssm(ut, a_re, a_im, log_dt, b_re, b_im, c_re, c_im, d, seq_chunks, w_gate, w_up, w_down, layer):
    g, n_ct = ut.shape[:2]
    assert N_EXPERTS % g == 0
    epg = N_EXPERTS // g
    base = layer * g
    n = SSM_STATE

    def per_group(a):
        return jnp.swapaxes(a, 0, 1)

    def dup(a):
        return jnp.concatenate([a, a], axis=-1)

    vecs = dup(jnp.stack([per_group(a_re), per_group(a_im),
                          jnp.broadcast_to(per_group(log_dt)[..., None], (g, 2, n))], axis=2))
    cr, ci = per_group(c_re), per_group(c_im)
    d_blk = jnp.broadcast_to(d.reshape(g, 1, SSM_GROUP, 1), (g, 2, SSM_GROUP, LANES))
    mats = jnp.concatenate(
        [dup(jnp.swapaxes(per_group(b_re), 2, 3)), dup(jnp.swapaxes(per_group(b_im), 2, 3)),
         jnp.concatenate([cr, -ci], axis=-1), dup(cr), dup(ci), d_blk], axis=2)
    exp_c = jnp.asarray(_expand_consts(), BF16)

    return pl.pallas_call(
        functools.partial(_ssm_kernel, seq_chunks=seq_chunks),
        grid=(g,),
        in_specs=[
            pl.BlockSpec((None, n_ct, CHUNK, SSM_GROUP, C_TILE), lambda i: (i, 0, 0, 0, 0)),
            pl.BlockSpec((None, 2, 3, LANES), lambda i: (i, 0, 0, 0)),
            pl.BlockSpec((None, 2, 6 * SSM_GROUP, LANES), lambda i: (i, 0, 0, 0)),
            pl.BlockSpec((4, CHUNK_W, LANES), lambda i: (0, 0, 0)),
            pl.BlockSpec((epg, D_MODEL, D_EXPERT), lambda i: (base + i, 0, 0)),
            pl.BlockSpec((epg, D_MODEL, D_EXPERT), lambda i: (base + i, 0, 0)),
            pl.BlockSpec((epg, D_EXPERT, D_MODEL), lambda i: (base + i, 0, 0)),
        ],
        out_specs=[
            pl.BlockSpec((None, n_ct, CHUNK, SSM_GROUP, C_TILE), lambda i: (i, 0, 0, 0, 0)),
            pl.BlockSpec((epg, D_MODEL, D_EXPERT), lambda i: (i, 0, 0)),
            pl.BlockSpec((epg, D_MODEL, D_EXPERT), lambda i: (i, 0, 0)),
            pl.BlockSpec((epg, D_EXPERT, D_MODEL), lambda i: (i, 0, 0)),
        ],
        out_shape=[
            jax.ShapeDtypeStruct(ut.shape, BF16),
            jax.ShapeDtypeStruct((N_EXPERTS, D_MODEL, D_EXPERT), BF16),
            jax.ShapeDtypeStruct((N_EXPERTS, D_MODEL, D_EXPERT), BF16),
            jax.ShapeDtypeStruct((N_EXPERTS, D_EXPERT, D_MODEL), BF16),
        ],
        compiler_params=_params(("parallel",)),
        name="ssm",
    )(ut, vecs, mats, exp_c, w_gate, w_up, w_down)


def _mix_out_kernel(x_hbm, a_ref, yt_ref, gwt_ref, gb_ref, wo_ref, h_hbm, xbuf, hbuf, xsem, hsem):
    nc = C_TILE
    x = _load_time_major_rows(x_hbm, xbuf, xsem)
    a = a_ref[...].reshape(S_TILE * nc, D_POOL)
    y = jnp.concatenate([yt_ref[:, j, :, :].reshape(D_SSM, nc) for j in range(S_TILE)],
                        axis=1).astype(F32)
    z = 0.5 * y * (1.0 + jnp.tanh(math.sqrt(2.0 / math.pi) * (y + 0.044715 * (y * y * y))))
    gate = _sigmoid(_dot(gwt_ref[...], z.astype(BF16)) + gb_ref[...])
    s = (z * gate).astype(BF16)
    h = x + _dot(a, wo_ref[:D_POOL, :]) + _dot_tn(s, wo_ref[D_POOL:, :])
    _store_time_major_rows(h, h_hbm, hbuf, hsem)


def _mix_out(x3, a3, yt, glu_w_t, glu_b_col, w_out):
    nch = x3.shape[0]
    return pl.pallas_call(
        _mix_out_kernel,
        grid=(nch // C_TILE, CHUNK // S_TILE),
        in_specs=[
            pl.BlockSpec(memory_space=pl.ANY),
            pl.BlockSpec((S_TILE, C_TILE, D_POOL), lambda c, t: (t, c, 0)),
            pl.BlockSpec((N_SSM_GROUPS, None, S_TILE, SSM_GROUP, C_TILE), lambda c, t: (0, c, t, 0, 0)),
            pl.BlockSpec((D_SSM, D_SSM), lambda c, t: (0, 0)),
            pl.BlockSpec((D_SSM, 1), lambda c, t: (0, 0)),
            pl.BlockSpec((D_MODEL, D_MODEL), lambda c, t: (0, 0)),
        ],
        out_specs=pl.BlockSpec(memory_space=pl.ANY),
        out_shape=jax.ShapeDtypeStruct((nch, CHUNK, D_MODEL), F32),
        scratch_shapes=[
            pltpu.VMEM((2, S_TILE, C_TILE, D_MODEL), F32),
            pltpu.VMEM((2, S_TILE, C_TILE, D_MODEL), F32),
            pltpu.SemaphoreType.DMA((2,)),
            pltpu.SemaphoreType.DMA((2,)),
        ],
        compiler_params=_params(("arbitrary", "arbitrary")),
        name="mix_out",
    )(x3, a3, yt, glu_w_t, glu_b_col, w_out)


def _pack_rows(x):
    b = lax.bitcast_convert_type(x.astype(BF16).astype(F32), U32)
    return (b[:, :HALF] & jnp.uint32(0xFFFF0000)) | (b[:, HALF:] >> 16)


def _unpack_rows(w):
    lo = lax.bitcast_convert_type(w & jnp.uint32(0xFFFF0000), F32)
    hi = lax.bitcast_convert_type(w << 16, F32)
    return lo, hi


def _split_bf16(x):
    hi = x.astype(BF16)
    return hi, (x - hi.astype(F32)).astype(BF16)


def _route(v32, wr_ref, br_ref):
    v_hi, v_lo = _split_bf16(v32)
    w_hi, w_lo = _split_bf16(wr_ref[...])
    both = _dot(v_hi, jnp.concatenate([w_hi, w_lo], axis=1))
    logits = both[:, :ROUTER_W] + (both[:, ROUTER_W:] + _dot(v_lo, w_hi)) + br_ref[...]
    return jnp.transpose(logits)


def _top1(x, valid=None):
    n = x.shape[0]
    row = lax.broadcasted_iota(jnp.int32, x.shape, 0).astype(F32)
    if valid is not None:
        x = jnp.where(valid, x, -jnp.inf)
    m = jnp.max(x, axis=0, keepdims=True)
    idx = jnp.min(jnp.where(x == m, row, float(n)), axis=0, keepdims=True)
    return m, idx, x, row


def _split_planes(packed, ref):
    ref[0] = packed[:, :SC_ROW]
    ref[1] = packed[:, SC_ROW:]


def _router_kernel(h_ref, g_ref, wr_ref, br_ref, before_ref, vp_ref, meta_t_ref, cnt_ref, carry_ref):
    @pl.when(pl.program_id(0) == 0)
    def _():
        carry_ref[...] = jnp.zeros_like(carry_ref)

    v32 = _rms(h_ref[...], g_ref[...])
    _split_planes(_pack_rows(v32), vp_ref)
    lt = _route(v32, wr_ref, br_ref)
    tm = lt.shape[1]
    eg = EXPERTS_PER_GROUP

    grp = lt[:eg]
    grp_row = lax.broadcasted_iota(jnp.int32, grp.shape, 0)
    mg, grp_idx, grp, _ = _top1(grp, grp_row < N_EXPERT_GROUPS)
    grp_p = 1.0 / jnp.sum(jnp.exp(grp - mg), axis=0, keepdims=True)
    le = jnp.zeros((eg, tm), F32)
    for g in range(N_EXPERT_GROUPS):
        le = jnp.where(grp_idx == float(g), lt[eg * (g + 1):eg * (g + 2)], le)
    m1, i1, le, row = _top1(le)
    z = jnp.sum(jnp.exp(le - m1), axis=0, keepdims=True)
    m2, i2, _, _ = _top1(jnp.where(row == i1, -jnp.inf, le))
    p1 = 1.0 / z
    p2 = jnp.exp(m2 - m1) / z
    tot = p1 + p2
    w1 = grp_p * (p1 / tot)
    w2 = grp_p * (p2 / tot)
    e1 = grp_idx * eg + i1
    e2 = grp_idx * eg + i2

    erow = lax.broadcasted_iota(jnp.int32, (N_EXPERTS, tm), 0).astype(F32)
    onehot = jnp.where(erow == e1, 1.0, jnp.where(erow == e2, 1.0, 0.0))
    before = _dot(onehot.astype(BF16), before_ref[...]) + carry_ref[...]
    rank1 = jnp.sum(jnp.where(erow == e1, before, 0.0), axis=0, keepdims=True)
    rank2 = jnp.sum(jnp.where(erow == e2, before, 0.0), axis=0, keepdims=True)
    carry = carry_ref[...] + jnp.sum(onehot, axis=1, keepdims=True)
    carry_ref[...] = carry
    cnt_ref[...] = carry

    mrow = lax.broadcasted_iota(jnp.int32, (META_ROWS, tm), 0)
    meta_t_ref[...] = jnp.where(mrow == 0, e1, jnp.where(mrow == 1, e2, jnp.where(
        mrow == 2, rank1, jnp.where(mrow == 3, rank2, jnp.where(
            mrow == 4, w1, jnp.where(mrow == 5, w2, 0.0))))))


def _earlier_matrix(tm):
    return np.triu(np.ones((tm, tm), np.float32), k=1)


def _router(h1, g_ffn, w_router, b_router, tm):
    t = h1.shape[0]
    return pl.pallas_call(
        _router_kernel,
        grid=(t // tm,),
        in_specs=[
            pl.BlockSpec((tm, D_MODEL), lambda i: (i, 0)),
            pl.BlockSpec((1, D_MODEL), lambda i: (0, 0)),
            pl.BlockSpec((D_MODEL, ROUTER_W), lambda i: (0, 0)),
            pl.BlockSpec((1, ROUTER_W), lambda i: (0, 0)),
            pl.BlockSpec((tm, tm), lambda i: (0, 0)),
        ],
        out_specs=[
            pl.BlockSpec((2, tm, SC_ROW), lambda i: (0, i, 0)),
            pl.BlockSpec((META_ROWS, tm), lambda i: (0, i)),
            pl.BlockSpec((N_EXPERTS, 1), lambda i: (0, 0)),
        ],
        out_shape=[
            jax.ShapeDtypeStruct((2, t, SC_ROW), U32),
            jax.ShapeDtypeStruct((META_ROWS, t), F32),
            jax.ShapeDtypeStruct((N_EXPERTS, 1), F32),
        ],
        scratch_shapes=[pltpu.VMEM((N_EXPERTS, 1), F32)],
        compiler_params=_params(("arbitrary",)),
        name="router",
    )(h1, g_ffn, w_router, b_router, jnp.asarray(_earlier_matrix(tm), BF16))


def _plan(meta_t, counts, n_tiles):
    e1 = meta_t[0].astype(jnp.int32)
    e2 = meta_t[1].astype(jnp.int32)
    rank1 = meta_t[2].astype(jnp.int32)
    rank2 = meta_t[3].astype(jnp.int32)
    cnt = counts[:, 0].astype(jnp.int32)
    padded = ((cnt + ROW_TILE - 1) // ROW_TILE) * ROW_TILE
    ends = jnp.cumsum(padded)
    starts = ends - padded
    experts = jnp.arange(N_EXPERTS, dtype=jnp.int32)
    pos1 = rank1 + jnp.sum(jnp.where(e1[None, :] == experts[:, None], starts[:, None], 0), axis=0)
    pos2 = rank2 + jnp.sum(jnp.where(e2[None, :] == experts[:, None], starts[:, None], 0), axis=0)
    tile_start = jnp.arange(n_tiles, dtype=jnp.int32) * ROW_TILE
    tile_expert = jnp.sum((tile_start[:, None] >= ends[None, :]).astype(jnp.int32), axis=1)
    tile_expert = jnp.minimum(tile_expert, N_EXPERTS - 1)
    rows_left = jnp.sum(jnp.where(tile_expert[:, None] == experts, cnt + starts, 0), axis=1) - tile_start
    n_valid = jnp.clip(rows_left, 0, ROW_TILE).astype(jnp.int32)
    last_used = jnp.maximum(ends[-1] // ROW_TILE - 1, 0)
    block = jnp.minimum(jnp.arange(n_tiles, dtype=jnp.int32), last_used)
    tile_expert = jnp.sum(jnp.where(block[:, None] == jnp.arange(n_tiles)[None, :],
                                    tile_expert[None, :], 0), axis=1)
    plane = n_tiles * ROW_TILE
    half_rows = jnp.concatenate([pos1, pos1 + plane, pos2, pos2 + plane])[None]
    return half_rows, tile_expert, n_valid, block


def _sc_mesh():
    return plsc.VectorSubcoreMesh(core_axis_name="c", subcore_axis_name="s")


def _sc_scatter_rows(rows, idx, n_out):
    t, width = rows.shape
    steps = t // SC_WINDOW
    half = steps // 2

    @pl.kernel(out_type=jax.ShapeDtypeStruct((n_out, width), rows.dtype), mesh=_sc_mesh(),
               scratch_types=[], name="moe_scatter")
    def scatter(rows_hbm, idx_hbm, out_hbm):
        def body(rows_vmem, idx0_vmem, idx1_vmem):
            pltpu.sync_copy(rows_vmem, out_hbm.at[idx0_vmem.at[0]])
            pltpu.sync_copy(rows_vmem, out_hbm.at[idx1_vmem.at[0]])

        pltpu.emit_pipeline(
            body,
            grid=(2, half),
            in_specs=[pl.BlockSpec((SC_WINDOW, width), lambda c, j: (c * half + j, 0)),
                      pl.BlockSpec((1, SC_WINDOW), lambda c, j: (0, c * half + j)),
                      pl.BlockSpec((1, SC_WINDOW), lambda c, j: (0, steps + c * half + j))],
            out_specs=[],
            core_axis_name=("c", "s"),
            dimension_semantics=(pltpu.PARALLEL, pltpu.PARALLEL),
        )(rows_hbm, idx_hbm, idx_hbm)

    return scatter(rows, idx)


def _sc_gather_rows(table, idx):
    m = idx.shape[1]
    width = table.shape[1]
    steps = m // (2 * SC_WINDOW)

    @pl.kernel(out_type=jax.ShapeDtypeStruct((m, width), table.dtype), mesh=_sc_mesh(),
               scratch_types=[], name="moe_gather")
    def gather(table_hbm, idx_hbm, out_hbm):
        def body(idx_vmem, out_vmem):
            pltpu.sync_copy(table_hbm.at[idx_vmem.at[0]], out_vmem)

        pltpu.emit_pipeline(
            body,
            grid=(2, steps),
            in_specs=[pl.BlockSpec((1, SC_WINDOW), lambda k, j: (0, k * steps + j))],
            out_specs=[pl.BlockSpec((SC_WINDOW, width), lambda k, j: (k * steps + j, 0))],
            core_axis_name=("c", "s"),
            dimension_semantics=(pltpu.PARALLEL, pltpu.PARALLEL),
        )(idx_hbm, out_hbm)

    return gather(table, idx)


def _experts_kernel(te_ref, nv_ref, blk_ref, xs_ref, wg_ref, wu_ref, wd_ref, ys_ref):
    r = pl.program_id(0)
    n_valid = nv_ref[r]
    row_blocks = (n_valid + ROW_BLOCK - 1) // ROW_BLOCK

    for k in range(1, ROW_TILE // ROW_BLOCK + 1):
        @pl.when(row_blocks == k)
        def _(m=k * ROW_BLOCK):
            parts = [p.astype(BF16) for p in _unpack_rows(xs_ref[0, :m]) + _unpack_rows(xs_ref[1, :m])]
            cols = (0, 2 * SC_ROW, SC_ROW, 3 * SC_ROW)
            hg = sum(_dot(p, wg_ref[c:c + SC_ROW, :]) for p, c in zip(parts, cols))
            hu = sum(_dot(p, wu_ref[c:c + SC_ROW, :]) for p, c in zip(parts, cols))
            row = lax.broadcasted_iota(jnp.int32, (m, 1), 0)
            hid = jnp.where(row < n_valid, hg * _sigmoid(hg) * hu, 0.0).astype(BF16)
            packed = _pack_rows(_dot(hid, wd_ref[...]))
            ys_ref[0, :m] = packed[:, :SC_ROW]
            ys_ref[1, :m] = packed[:, SC_ROW:]


def _experts(xs, tile_expert, n_valid, block, w_gate, w_up, w_down):
    n_tiles = xs.shape[1] // ROW_TILE
    w_spec = pl.BlockSpec((None, D_MODEL, D_EXPERT), lambda r, te, nv, blk: (te[r], 0, 0))
    grid_spec = pltpu.PrefetchScalarGridSpec(
        num_scalar_prefetch=3,
        grid=(n_tiles,),
        in_specs=[
            pl.BlockSpec((2, ROW_TILE, SC_ROW), lambda r, te, nv, blk: (0, blk[r], 0)),
            w_spec, w_spec,
            pl.BlockSpec((None, D_EXPERT, D_MODEL), lambda r, te, nv, blk: (te[r], 0, 0)),
        ],
        out_specs=pl.BlockSpec((2, ROW_TILE, SC_ROW), lambda r, te, nv, blk: (0, blk[r], 0)),
    )
    return pl.pallas_call(
        _experts_kernel,
        grid_spec=grid_spec,
        out_shape=jax.ShapeDtypeStruct((2, n_tiles * ROW_TILE, SC_ROW), U32),
        compiler_params=_params(("arbitrary",)),
        name="experts",
    )(tile_expert, n_valid, block, xs, w_gate, w_up, w_down)


def _ple_kernel(h_ref, yg_ref, meta_t_ref, p_ref, wg_ref, bg_ref, wp_ref, gf_ref, *rest, final_norm):
    o_ref = rest[-1]
    meta = jnp.transpose(meta_t_ref[...])
    w1 = meta[:, 4:5]
    w2 = meta[:, 5:6]
    q0, q2 = (w1 * u + w2 * v for u, v in zip(_unpack_rows(yg_ref[0]), _unpack_rows(yg_ref[2])))
    q1, q3 = (w1 * u + w2 * v for u, v in zip(_unpack_rows(yg_ref[1]), _unpack_rows(yg_ref[3])))
    moe = jnp.concatenate([q0, q1, q2, q3], axis=1)
    h = h_ref[...] + moe
    gate = _sigmoid(_dot(_rms(h).astype(BF16), wg_ref[...]) + bg_ref[...])
    h = h + gate * _dot(p_ref[...].astype(BF16), wp_ref[...])
    o_ref[...] = _rms(h, gf_ref[...]) if final_norm else h


def _ple(h1, yg, meta_t, p2, w_gate, b_gate, w_proj, g_final, final_norm, tm, part, prev_out):
    t = h1.shape[0]
    steps = t // tm // GATHER_PARTS
    off = part * steps
    in_specs = [
        pl.BlockSpec((tm, D_MODEL), lambda i: (i + off, 0)),
        pl.BlockSpec((4, tm, SC_ROW), lambda i: (0, i, 0)),
        pl.BlockSpec((META_ROWS, tm), lambda i: (0, i + off)),
        pl.BlockSpec((tm, D_PLE), lambda i: (i + off, 0)),
        pl.BlockSpec((D_MODEL, D_MODEL), lambda i: (0, 0)),
        pl.BlockSpec((1, D_MODEL), lambda i: (0, 0)),
        pl.BlockSpec((D_PLE, D_MODEL), lambda i: (0, 0)),
        pl.BlockSpec((1, D_MODEL), lambda i: (0, 0)),
    ]
    args = [h1, yg, meta_t, p2, w_gate, b_gate, w_proj, g_final]
    aliases = {}
    if prev_out is not None:
        in_specs.append(pl.BlockSpec(memory_space=pl.ANY))
        args.append(prev_out)
        aliases = {len(args) - 1: 0}
    return pl.pallas_call(
        functools.partial(_ple_kernel, final_norm=final_norm),
        grid=(steps,),
        in_specs=in_specs,
        out_specs=pl.BlockSpec((tm, D_MODEL), lambda i: (i + off, 0)),
        out_shape=jax.ShapeDtypeStruct((t, D_MODEL), F32),
        input_output_aliases=aliases,
        compiler_params=_params(("parallel",)),
        name="ple",
    )(*args)


def kernel(x, p, g_mix, w_in, pool_w, pool_scale, ssm_a_re, ssm_a_im, ssm_log_dt, ssm_b_re,
           ssm_b_im, ssm_c_re, ssm_c_im, ssm_d, glu_w, glu_b, w_out, g_ffn, router_grp_w,
           router_grp_b, router_exp_w, router_exp_b, exp_w_gate, exp_w_up, exp_w_down, g_ple,
           ple_gate_w, ple_gate_b, ple_proj_w, g_final):
    bsz, seq, dm = x.shape
    depth = g_mix.shape[0]
    t = bsz * seq
    seq_chunks = seq // CHUNK
    nch = t // CHUNK
    tm = 1024
    n_sorted = (pl.cdiv(2 * t, ROW_TILE) + N_EXPERTS) * ROW_TILE
    w_gate_all = exp_w_gate.reshape(depth * N_EXPERTS, dm, D_EXPERT)
    w_up_all = exp_w_up.reshape(depth * N_EXPERTS, dm, D_EXPERT)
    w_down_all = exp_w_down.reshape(depth * N_EXPERTS, D_EXPERT, dm)

    h = x.reshape(t, dm)
    for i in range(depth):
        w_in_b = (g_mix[i][:, None] * w_in[i]).astype(BF16)
        zp, ut = _in_proj(h.reshape(nch, CHUNK, dm), w_in_b[:, :D_POOL],
                          jnp.transpose(w_in_b[:, D_POOL:]))
        a = _pool(zp, pool_w[i], pool_scale[i][None], bsz)
        yt, w_gate_b, w_up_b, w_down_b = _ssm(
            ut, ssm_a_re[i], ssm_a_im[i], ssm_log_dt[i], ssm_b_re[i], ssm_b_im[i], ssm_c_re[i],
            ssm_c_im[i], ssm_d[i], seq_chunks, w_gate_all, w_up_all, w_down_all, i)
        h = _mix_out(h.reshape(nch, CHUNK, dm), a, yt,
                     jnp.transpose(glu_w[i]).astype(BF16), glu_b[i][:, None],
                     w_out[i].astype(BF16)).reshape(t, dm)

        eg = EXPERTS_PER_GROUP
        w_router = jnp.concatenate(
            [router_grp_w[i], jnp.zeros((dm, eg - N_EXPERT_GROUPS), F32),
             jnp.transpose(router_exp_w[i], (1, 0, 2)).reshape(dm, N_EXPERTS),
             jnp.zeros((dm, ROUTER_W - eg - N_EXPERTS), F32)], axis=1)
        b_router = jnp.concatenate(
            [router_grp_b[i], jnp.zeros((eg - N_EXPERT_GROUPS,), F32),
             router_exp_b[i].reshape(N_EXPERTS),
             jnp.zeros((ROUTER_W - eg - N_EXPERTS,), F32)])[None]
        vp, meta_t, counts = _router(h, g_ffn[i][None], w_router, b_router, tm)
        idx, tile_expert, n_valid, block = _plan(meta_t, counts, n_sorted // ROW_TILE)
        xs = _sc_scatter_rows(vp.reshape(2 * t, SC_ROW), idx, 2 * n_sorted)
        ys = _experts(xs.reshape(2, n_sorted, SC_ROW), tile_expert, n_valid, block,
                      w_gate_b, w_up_b, w_down_b)
        ys2 = ys.reshape(2 * n_sorted, SC_ROW)
        idx4 = idx.reshape(4, t)
        tp = t // GATHER_PARTS
        ple_wg = (g_ple[i][:, None] * ple_gate_w[i]).astype(BF16)
        ple_wp = ple_proj_w[i].astype(BF16)
        out = None
        for q in range(GATHER_PARTS):
            yg_q = _sc_gather_rows(ys2, idx4[:, q * tp:(q + 1) * tp].reshape(1, 4 * tp))
            out = _ple(h, yg_q.reshape(4, tp, SC_ROW), meta_t, p[i].reshape(t, D_PLE), ple_wg,
                       ple_gate_b[i][None], ple_wp, g_final[None], i == depth - 1, tm, q, out)
        h = out
    return h.reshape(bsz, seq, dm)
```

```python
import functools
import math

import numpy as np
import jax
import jax.numpy as jnp
from jax import lax
from jax.experimental import pallas as pl
from jax.experimental.pallas import tpu as pltpu
from jax.experimental.pallas import tpu_sc as plsc

F32 = jnp.float32
BF16 = jnp.bfloat16
U32 = jnp.uint32

D_MODEL = 1024
D_POOL = 512
D_SSM = 512
POOL_WINDOWS = (2, 4, 8, 16)
POOL_GROUP = 128
SSM_GROUP = 16
N_SSM_GROUPS = 32
SSM_STATE = 64
N_EXPERT_GROUPS = 4
EXPERTS_PER_GROUP = 8
N_EXPERTS = N_EXPERT_GROUPS * EXPERTS_PER_GROUP
D_EXPERT = 256
D_PLE = 256
RMS_EPS = 1e-6

LANES = 128
CHUNK = 32
CHUNK_W = CHUNK * SSM_GROUP
T_SUB = 8
S_TILE = 8
C_TILE = 128
ROUTER_W = LANES
HALF = D_MODEL // 2
ROW_TILE = 1280
SC_WINDOW = 128
SC_ROW = HALF // 2
META_ROWS = 8
FIRST_PART = 4
VMEM_LIMIT = 56 * 1024 * 1024


def _dot(a, b):
    return jnp.dot(a, b, preferred_element_type=F32)


def _dot_nt(a, b):
    return lax.dot_general(a, b, (((1,), (1,)), ((), ())), preferred_element_type=F32)


def _dot_tn(a, b):
    return lax.dot_general(a, b, (((0,), (0,)), ((), ())), preferred_element_type=F32)


def _rms(x, g=None):
    y = x * lax.rsqrt(jnp.mean(x * x, axis=-1, keepdims=True) + RMS_EPS)
    return y if g is None else y * g


def _sigmoid(x):
    return 1.0 / (1.0 + jnp.exp(-x))


def _params(sem):
    return pltpu.CompilerParams(dimension_semantics=sem, vmem_limit_bytes=VMEM_LIMIT)


def _tile_step():
    n_s = pl.num_programs(1)
    return pl.program_id(0) * n_s + pl.program_id(1), pl.num_programs(0) * n_s


def _row_copies(hbm, buf, sem, step_idx, slot_idx, to_hbm):
    n_s = pl.num_programs(1)
    c0 = (step_idx // n_s) * C_TILE
    s0 = (step_idx % n_s) * S_TILE
    out = []
    for j in range(S_TILE):
        far, near = hbm.at[pl.ds(c0, C_TILE), s0 + j, :], buf.at[slot_idx, j]
        src, dst = (near, far) if to_hbm else (far, near)
        out.append(pltpu.make_async_copy(src, dst, sem.at[slot_idx]))
    return out


def _load_time_major_rows(hbm, buf, sem):
    step, n_steps = _tile_step()
    slot = step % 2

    @pl.when(step == 0)
    def _():
        for cp in _row_copies(hbm, buf, sem, step, slot, False):
            cp.start()

    @pl.when(step + 1 < n_steps)
    def _():
        for cp in _row_copies(hbm, buf, sem, step + 1, 1 - slot, False):
            cp.start()

    for cp in _row_copies(hbm, buf, sem, step, slot, False):
        cp.wait()
    return buf[slot].reshape(S_TILE * C_TILE, buf.shape[-1])


def _store_time_major_rows(val, hbm, buf, sem):
    step, n_steps = _tile_step()
    slot = step % 2

    @pl.when(step >= 2)
    def _():
        for cp in _row_copies(hbm, buf, sem, step - 2, slot, True):
            cp.wait()

    buf[slot] = val.reshape(S_TILE, C_TILE, val.shape[-1])
    for cp in _row_copies(hbm, buf, sem, step, slot, True):
        cp.start()

    @pl.when(step == n_steps - 1)
    def _():
        @pl.when(step >= 1)
        def _():
            for cp in _row_copies(hbm, buf, sem, step - 1, 1 - slot, True):
                cp.wait()
        for cp in _row_copies(hbm, buf, sem, step, slot, True):
            cp.wait()


def _in_proj_kernel(x_hbm, wp_ref, wst_ref, zp_ref, ut_ref, xbuf, xsem):
    nc = C_TILE
    u = _rms(_load_time_major_rows(x_hbm, xbuf, xsem)).astype(BF16)
    zp_ref[...] = _dot(u, wp_ref[...]).reshape(zp_ref.shape).astype(zp_ref.dtype)
    zt = _dot_nt(wst_ref[...], u).astype(BF16)
    for j in range(S_TILE):
        ut_ref[:, j, :, :] = zt[:, j * nc:(j + 1) * nc].reshape(N_SSM_GROUPS, SSM_GROUP, nc)


def _in_proj(x3, w_pool, w_ssm_t):
    nch = x3.shape[0]
    return pl.pallas_call(
        _in_proj_kernel,
        grid=(nch // C_TILE, CHUNK // S_TILE),
        in_specs=[
            pl.BlockSpec(memory_space=pl.ANY),
            pl.BlockSpec((D_MODEL, D_POOL), lambda c, s: (0, 0)),
            pl.BlockSpec((D_SSM, D_MODEL), lambda c, s: (0, 0)),
        ],
        out_specs=[
            pl.BlockSpec((S_TILE, C_TILE, D_POOL), lambda c, s: (s, c, 0)),
            pl.BlockSpec((N_SSM_GROUPS, None, S_TILE, SSM_GROUP, C_TILE), lambda c, s: (0, c, s, 0, 0)),
        ],
        out_shape=[
            jax.ShapeDtypeStruct((CHUNK, nch, D_POOL), BF16),
            jax.ShapeDtypeStruct((N_SSM_GROUPS, nch // C_TILE, CHUNK, SSM_GROUP, C_TILE), BF16),
        ],
        scratch_shapes=[
            pltpu.VMEM((2, S_TILE, C_TILE, D_MODEL), F32),
            pltpu.SemaphoreType.DMA((2,)),
        ],
        compiler_params=_params(("arbitrary", "arbitrary")),
        name="in_proj",
    )(x3, w_pool, w_ssm_t)


def _pool_kernel(z_ref, w_ref, sc_ref, o_ref):
    n_c = z_ref.shape[1]
    gi = pl.program_id(1)
    chunk = lax.broadcasted_iota(jnp.int32, (n_c, 1), 0)

    def plane(p):
        if p < 0:
            return jnp.where(chunk >= 1, pltpu.roll(z_ref[p + CHUNK].astype(F32), 1, 0), 0.0)
        if p >= CHUNK:
            return jnp.where(chunk < n_c - 1, pltpu.roll(z_ref[p - CHUNK].astype(F32), n_c - 1, 0), 0.0)
        return z_ref[p].astype(F32)

    def inv_count(s_t, half):
        n = n_c * CHUNK
        first = min(s_t + half, n) - max(s_t - half, 0)
        t_last = n - CHUNK + s_t
        last = min(t_last + half, n) - max(t_last - half, 0)
        inv = 1.0 / (2 * half)
        if first != 2 * half:
            inv = jnp.where(chunk == 0, 1.0 / first, inv)
        if last != 2 * half:
            inv = jnp.where(chunk == n_c - 1, 1.0 / last, inv)
        return inv

    for k, w in enumerate(POOL_WINDOWS):
        @pl.when(gi == k)
        def _(w=w):
            half = w // 2
            total = plane(-half)
            for p in range(-half + 1, half):
                total = total + plane(p)
            diffs = []
            for s_t in range(CHUNK):
                diffs.append((total * inv_count(s_t, half) - plane(s_t)).astype(BF16))
                if s_t + 1 < CHUNK:
                    total = total + plane(s_t + half) - plane(s_t - half)
            diff = jnp.concatenate(diffs, axis=0)
            out = _dot(diff, w_ref[...].astype(BF16)) * sc_ref[...]
            o_ref[...] = out.reshape(o_ref.shape).astype(o_ref.dtype)


def _pool(zp_t, pool_w, pool_scale, bsz):
    _, nch, _ = zp_t.shape
    n_c = nch // bsz
    return pl.pallas_call(
        _pool_kernel,
        grid=(bsz, len(POOL_WINDOWS)),
        in_specs=[
            pl.BlockSpec((CHUNK, n_c, POOL_GROUP), lambda i, g: (0, i, g)),
            pl.BlockSpec((None, POOL_GROUP, POOL_GROUP), lambda i, g: (g, 0, 0)),
            pl.BlockSpec((1, POOL_GROUP), lambda i, g: (0, g)),
        ],
        out_specs=pl.BlockSpec((CHUNK, n_c, POOL_GROUP), lambda i, g: (0, i, g)),
        out_shape=jax.ShapeDtypeStruct(zp_t.shape, BF16),
        compiler_params=_params(("parallel", "parallel")),
        name="pool",
    )(zp_t, pool_w, pool_scale)


def _expand_consts():
    time = np.arange(CHUNK_W) // SSM_GROUP
    def onehot(e):
        m = np.zeros((CHUNK_W, LANES), np.float32)
        m[np.arange(CHUNK_W), e] = 1.0
        return m
    return np.stack([
        onehot(CHUNK - 1 - time),
        onehot(time),
        onehot(time + 1),
        onehot(CHUNK - time),
    ])


def _ssm_kernel(u_ref, vec_ref, mat_ref, exp_ref, wg_ref, wu_ref, wd_ref, y_ref, og_ref, ou_ref,
                od_ref, *, seq_chunks):
    og_ref[...] = wg_ref[...].astype(BF16)
    ou_ref[...] = wu_ref[...].astype(BF16)
    od_ref[...] = wd_ref[...].astype(BF16)

    n_ct = u_ref.shape[0]
    nch = n_ct * C_TILE
    half = LANES // 2
    lane = lax.broadcasted_iota(jnp.int32, (1, LANES), 1)
    lo_half = lane < half

    def direction(di):
        a_re = vec_ref[di, 0:1]
        a_im = vec_ref[di, 1:2]
        dt = jnp.exp(vec_ref[di, 2:3])
        mag = jnp.exp(a_re * dt)
        ang = a_im * dt
        lb_re = mag * jnp.cos(ang)
        lb_im = mag * jnp.sin(ang)
        den = a_re * a_re + a_im * a_im
        f_re = ((lb_re - 1.0) * a_re + lb_im * a_im) / den
        f_im = (lb_im * a_re - (lb_re - 1.0) * a_im) / den
        return (lb_re, lb_im), f_re, f_im

    def power_table(lam):
        e = lax.broadcasted_iota(jnp.int32, (LANES, 1), 0)
        sq_re, sq_im = lam
        t_re = jnp.ones((LANES, LANES), F32)
        t_im = jnp.zeros((LANES, LANES), F32)
        for k in range(CHUNK.bit_length()):
            bit = ((e >> k) & 1) == 1
            t_re, t_im = (jnp.where(bit, t_re * sq_re - t_im * sq_im, t_re),
                          jnp.where(bit, t_re * sq_im + t_im * sq_re, t_im))
            sq_re, sq_im = sq_re * sq_re - sq_im * sq_im, 2.0 * sq_re * sq_im
        return jnp.where(lo_half, t_re, t_im), jnp.where(lo_half, t_im, t_re)

    def tile_rows(x16):
        return jnp.broadcast_to(x16[None], (CHUNK, SSM_GROUP, LANES)).reshape(CHUNK_W, LANES)

    def expanded(tab, which, v_re, v_im, conj_sign):
        lexp = _dot(exp_ref[which], jnp.concatenate(tab, axis=1).astype(BF16))
        if conj_sign > 0:
            p = jnp.where(lo_half, v_re, v_re)
            q = jnp.where(lo_half, -v_im, v_im)
        else:
            p = jnp.where(lo_half, v_re, -v_re)
            q = jnp.where(lo_half, -v_im, -v_im)
        return lexp[:, :LANES] * tile_rows(p) + lexp[:, LANES:] * tile_rows(q)

    lam_f, ff_re, ff_im = direction(0)
    lam_b, fb_re, fb_im = direction(1)
    tabs_f = power_table(lam_f)
    tabs_b = power_table(lam_b)
    tab_f, tab_b = tabs_f[0], tabs_b[0]

    def bbar(bt_re, bt_im, f_re, f_im):
        return bt_re * f_re - bt_im * f_im, bt_re * f_im + bt_im * f_re

    def mat(di, k):
        return mat_ref[di, k * SSM_GROUP:(k + 1) * SSM_GROUP, :]

    bf_re, bf_im = bbar(mat(0, 0), mat(0, 1), ff_re, ff_im)
    bb_re, bb_im = bbar(mat(1, 0), mat(1, 1), fb_re, fb_im)

    pb1 = expanded(tabs_f, 0, bf_re, bf_im, 1)
    pb2 = expanded(tabs_b, 1, bb_re, bb_im, 1)
    pb3 = expanded(tabs_b, 2, bb_re, bb_im, 1)
    ft_f = expanded(tabs_f, 2, mat(0, 3), mat(0, 4), -1)
    ft_b = expanded(tabs_b, 3, mat(1, 3), mat(1, 4), -1)

    row_w = lax.broadcasted_iota(jnp.int32, (CHUNK_W, 1), 0)
    last_blk = row_w >= CHUNK_W - SSM_GROUP
    pb2_lag0 = jnp.where(last_blk, pltpu.roll(pb2, CHUNK_W - SSM_GROUP, 0), 0.0)
    ccr_f = mat(0, 2).astype(BF16)
    ccr_b = mat(1, 2).astype(BF16)
    r_lo = _dot_nt(ccr_f, pb1.astype(BF16)) + _dot_nt(ccr_b, pb2_lag0.astype(BF16))
    co = lax.broadcasted_iota(jnp.int32, (SSM_GROUP, CHUNK_W), 0)
    col = lax.broadcasted_iota(jnp.int32, (SSM_GROUP, CHUNK_W), 1)
    r_lo = r_lo + jnp.where(col == CHUNK_W - SSM_GROUP + co, mat(0, 5)[:, 0:1], 0.0)
    r_hi = _dot_nt(ccr_b, pb3.astype(BF16))
    r_t = jnp.concatenate([r_lo, r_hi], axis=1)
    g_t = jnp.concatenate(
        [pltpu.roll(r_t, SSM_GROUP * (tl + 1), 1) for tl in range(T_SUB)], axis=0
    ).astype(BF16)

    u = jnp.concatenate([u_ref[ct].reshape(CHUNK_W, C_TILE) for ct in range(n_ct)],
                        axis=1)
    e_mat = jnp.concatenate([pb1, pb2], axis=1).astype(BF16)
    xend = _dot_tn(e_mat, u)
    lanec = lax.broadcasted_iota(jnp.int32, (1, nch), 1) % seq_chunks
    ns = SSM_STATE

    def scan(re, im, tab, forward):
        lam_col = jnp.transpose(tab[CHUNK:CHUNK + 8, :])[:, 0:1]
        a, b = lam_col[:ns], lam_col[ns:]
        n_steps = int(math.log2(seq_chunks))

        def rolled(v, d):
            if d % LANES == 0:
                return jnp.concatenate([v[:, nch - d:], v[:, :nch - d]], axis=1)
            return pltpu.roll(v, d, 1)

        def shifted(v, d):
            if forward:
                return jnp.where(lanec >= d, rolled(v, d), 0.0)
            return jnp.where(lanec < seq_chunks - d, rolled(v, nch - d), 0.0)

        for k in range(n_steps):
            sr, si = shifted(re, 1 << k), shifted(im, 1 << k)
            re, im = re + (sr * a - si * b), im + (sr * b + si * a)
            a, b = a * a - b * b, 2.0 * a * b
        return shifted(re, 1), shifted(im, 1)

    f_re, f_im = scan(xend[:ns], xend[ns:2 * ns], tab_f, True)
    b_re, b_im = scan(xend[2 * ns:3 * ns], xend[3 * ns:], tab_b, False)
    xin = jnp.concatenate([f_re, f_im, b_re, b_im], axis=0).astype(BF16)
    f_t = jnp.concatenate([ft_f, ft_b], axis=1).astype(BF16)

    toeplitz = jnp.concatenate(
        [g_t[:, CHUNK_W - LANES * th:2 * CHUNK_W - LANES * th] for th in range(CHUNK // T_SUB)],
        axis=0)
    y_t = _dot(toeplitz, u) + _dot(f_t, xin)
    for ct in range(n_ct):
        y_ref[ct] = y_t[:, ct * C_TILE:(ct + 1) * C_TILE].reshape(
            CHUNK, SSM_GROUP, C_TILE).astype(y_ref.dtype)


def _ssm(ut, a_re, a_im, log_dt, b_re, b_im, c_re, c_im, d, seq_chunks, w_gate, w_up, w_down, layer):
    g, n_ct = ut.shape[:2]
    assert N_EXPERTS % g == 0
    epg = N_EXPERTS // g
    base = layer * g
    n = SSM_STATE

    def per_group(a):
        return jnp.swapaxes(a, 0, 1)

    def dup(a):
        return jnp.concatenate([a, a], axis=-1)

    vecs = dup(jnp.stack([per_group(a_re), per_group(a_im),
                          jnp.broadcast_to(per_group(log_dt)[..., None], (g, 2, n))], axis=2))
    cr, ci = per_group(c_re), per_group(c_im)
    d_blk = jnp.broadcast_to(d.reshape(g, 1, SSM_GROUP, 1), (g, 2, SSM_GROUP, LANES))
    mats = jnp.concatenate(
        [dup(jnp.swapaxes(per_group(b_re), 2, 3)), dup(jnp.swapaxes(per_group(b_im), 2, 3)),
         jnp.concatenate([cr, -ci], axis=-1), dup(cr), dup(ci), d_blk], axis=2)
    exp_c = jnp.asarray(_expand_consts(), BF16)

    return pl.pallas_call(
        functools.partial(_ssm_kernel, seq_chunks=seq_chunks),
        grid=(g,),
        in_specs=[
            pl.BlockSpec((None, n_ct, CHUNK, SSM_GROUP, C_TILE), lambda i: (i, 0, 0, 0, 0)),
            pl.BlockSpec((None, 2, 3, LANES), lambda i: (i, 0, 0, 0)),
            pl.BlockSpec((None, 2, 6 * SSM_GROUP, LANES), lambda i: (i, 0, 0, 0)),
            pl.BlockSpec((4, CHUNK_W, LANES), lambda i: (0, 0, 0)),
            pl.BlockSpec((epg, D_MODEL, D_EXPERT), lambda i: (base + i, 0, 0)),
            pl.BlockSpec((epg, D_MODEL, D_EXPERT), lambda i: (base + i, 0, 0)),
            pl.BlockSpec((epg, D_EXPERT, D_MODEL), lambda i: (base + i, 0, 0)),
        ],
        out_specs=[
            pl.BlockSpec((None, n_ct, CHUNK, SSM_GROUP, C_TILE), lambda i: (i, 0, 0, 0, 0)),
            pl.BlockSpec((epg, D_MODEL, D_EXPERT), lambda i: (i, 0, 0)),
            pl.BlockSpec((epg, D_MODEL, D_EXPERT), lambda i: (i, 0, 0)),
            pl.BlockSpec((epg, D_EXPERT, D_MODEL), lambda i: (i, 0, 0)),
        ],
        out_shape=[
            jax.ShapeDtypeStruct(ut.shape, BF16),
            jax.ShapeDtypeStruct((N_EXPERTS, D_MODEL, D_EXPERT), BF16),
            jax.ShapeDtypeStruct((N_EXPERTS, D_MODEL, D_EXPERT), BF16),
            jax.ShapeDtypeStruct((N_EXPERTS, D_EXPERT, D_MODEL), BF16),
        ],
        compiler_params=_params(("parallel",)),
        name="ssm",
    )(ut, vecs, mats, exp_c, w_gate, w_up, w_down)


def _mix_out_kernel(x_hbm, a_ref, yt_ref, gwt_ref, gb_ref, wo_ref, h_hbm, xbuf, hbuf, xsem, hsem):
    nc = C_TILE
    x = _load_time_major_rows(x_hbm, xbuf, xsem)
    a = a_ref[...].reshape(S_TILE * nc, D_POOL)
    y = jnp.concatenate([yt_ref[:, j, :, :].reshape(D_SSM, nc) for j in range(S_TILE)],
                        axis=1).astype(F32)
    z = 0.5 * y * (1.0 + jnp.tanh(math.sqrt(2.0 / math.pi) * (y + 0.044715 * (y * y * y))))
    gate = _sigmoid(_dot(gwt_ref[...], z.astype(BF16)) + gb_ref[...])
    s = (z * gate).astype(BF16)
    h = x + _dot(a, wo_ref[:D_POOL, :]) + _dot_tn(s, wo_ref[D_POOL:, :])
    _store_time_major_rows(h, h_hbm, hbuf, hsem)


def _mix_out(x3, a3, yt, glu_w_t, glu_b_col, w_out):
    nch = x3.shape[0]
    return pl.pallas_call(
        _mix_out_kernel,
        grid=(nch // C_TILE, CHUNK // S_TILE),
        in_specs=[
            pl.BlockSpec(memory_space=pl.ANY),
            pl.BlockSpec((S_TILE, C_TILE, D_POOL), lambda c, t: (t, c, 0)),
            pl.BlockSpec((N_SSM_GROUPS, None, S_TILE, SSM_GROUP, C_TILE), lambda c, t: (0, c, t, 0, 0)),
            pl.BlockSpec((D_SSM, D_SSM), lambda c, t: (0, 0)),
            pl.BlockSpec((D_SSM, 1), lambda c, t: (0, 0)),
            pl.BlockSpec((D_MODEL, D_MODEL), lambda c, t: (0, 0)),
        ],
        out_specs=pl.BlockSpec(memory_space=pl.ANY),
        out_shape=jax.ShapeDtypeStruct((nch, CHUNK, D_MODEL), F32),
        scratch_shapes=[
            pltpu.VMEM((2, S_TILE, C_TILE, D_MODEL), F32),
            pltpu.VMEM((2, S_TILE, C_TILE, D_MODEL), F32),
            pltpu.SemaphoreType.DMA((2,)),
            pltpu.SemaphoreType.DMA((2,)),
        ],
        compiler_params=_params(("arbitrary", "arbitrary")),
        name="mix_out",
    )(x3, a3, yt, glu_w_t, glu_b_col, w_out)


def _pack_rows(x):
    b = lax.bitcast_convert_type(x.astype(BF16).astype(F32), U32)
    return (b[:, :HALF] & jnp.uint32(0xFFFF0000)) | (b[:, HALF:] >> 16)


def _unpack_rows(w):
    lo = lax.bitcast_convert_type(w & jnp.uint32(0xFFFF0000), F32)
    hi = lax.bitcast_convert_type(w << 16, F32)
    return lo, hi


def _split_bf16(x):
    hi = x.astype(BF16)
    return hi, (x - hi.astype(F32)).astype(BF16)


def _route(v32, wr_ref, br_ref):
    v_hi, v_lo = _split_bf16(v32)
    w_hi, w_lo = _split_bf16(wr_ref[...])
    both = _dot(v_hi, jnp.concatenate([w_hi, w_lo], axis=1))
    logits = both[:, :ROUTER_W] + (both[:, ROUTER_W:] + _dot(v_lo, w_hi)) + br_ref[...]
    return jnp.transpose(logits)


def _top1(x, valid=None):
    n = x.shape[0]
    row = lax.broadcasted_iota(jnp.int32, x.shape, 0).astype(F32)
    if valid is not None:
        x = jnp.where(valid, x, -jnp.inf)
    m = jnp.max(x, axis=0, keepdims=True)
    idx = jnp.min(jnp.where(x == m, row, float(n)), axis=0, keepdims=True)
    return m, idx, x, row


def _split_planes(packed, ref):
    ref[0] = packed[:, :SC_ROW]
    ref[1] = packed[:, SC_ROW:]


def _router_kernel(h_ref, g_ref, wr_ref, br_ref, before_ref, vp_ref, meta_t_ref, cnt_ref, carry_ref):
    @pl.when(pl.program_id(0) == 0)
    def _():
        carry_ref[...] = jnp.zeros_like(carry_ref)

    v32 = _rms(h_ref[...], g_ref[...])
    _split_planes(_pack_rows(v32), vp_ref)
    lt = _route(v32, wr_ref, br_ref)
    tm = lt.shape[1]
    eg = EXPERTS_PER_GROUP

    grp = lt[:eg]
    grp_row = lax.broadcasted_iota(jnp.int32, grp.shape, 0)
    mg, grp_idx, grp, _ = _top1(grp, grp_row < N_EXPERT_GROUPS)
    grp_p = 1.0 / jnp.sum(jnp.exp(grp - mg), axis=0, keepdims=True)
    le = jnp.zeros((eg, tm), F32)
    for g in range(N_EXPERT_GROUPS):
        le = jnp.where(grp_idx == float(g), lt[eg * (g + 1):eg * (g + 2)], le)
    m1, i1, le, row = _top1(le)
    z = jnp.sum(jnp.exp(le - m1), axis=0, keepdims=True)
    m2, i2, _, _ = _top1(jnp.where(row == i1, -jnp.inf, le))
    p1 = 1.0 / z
    p2 = jnp.exp(m2 - m1) / z
    tot = p1 + p2
    w1 = grp_p * (p1 / tot)
    w2 = grp_p * (p2 / tot)
    e1 = grp_idx * eg + i1
    e2 = grp_idx * eg + i2

    erow = lax.broadcasted_iota(jnp.int32, (N_EXPERTS, tm), 0).astype(F32)
    onehot = jnp.where(erow == e1, 1.0, jnp.where(erow == e2, 1.0, 0.0))
    before = _dot(onehot.astype(BF16), before_ref[...]) + carry_ref[...]
    rank1 = jnp.sum(jnp.where(erow == e1, before, 0.0), axis=0, keepdims=True)
    rank2 = jnp.sum(jnp.where(erow == e2, before, 0.0), axis=0, keepdims=True)
    carry = carry_ref[...] + jnp.sum(onehot, axis=1, keepdims=True)
    carry_ref[...] = carry
    cnt_ref[...] = carry

    mrow = lax.broadcasted_iota(jnp.int32, (META_ROWS, tm), 0)
    meta_t_ref[...] = jnp.where(mrow == 0, e1, jnp.where(mrow == 1, e2, jnp.where(
        mrow == 2, rank1, jnp.where(mrow == 3, rank2, jnp.where(
            mrow == 4, w1, jnp.where(mrow == 5, w2, 0.0))))))


def _earlier_matrix(tm):
    return np.triu(np.ones((tm, tm), np.float32), k=1)


def _router(h1, g_ffn, w_router, b_router, tm):
    t = h1.shape[0]
    return pl.pallas_call(
        _router_kernel,
        grid=(t // tm,),
        in_specs=[
            pl.BlockSpec((tm, D_MODEL), lambda i: (i, 0)),
            pl.BlockSpec((1, D_MODEL), lambda i: (0, 0)),
            pl.BlockSpec((D_MODEL, ROUTER_W), lambda i: (0, 0)),
            pl.BlockSpec((1, ROUTER_W), lambda i: (0, 0)),
            pl.BlockSpec((tm, tm), lambda i: (0, 0)),
        ],
        out_specs=[
            pl.BlockSpec((2, tm, SC_ROW), lambda i: (0, i, 0)),
            pl.BlockSpec((META_ROWS, tm), lambda i: (0, i)),
            pl.BlockSpec((N_EXPERTS, 1), lambda i: (0, 0)),
        ],
        out_shape=[
            jax.ShapeDtypeStruct((2, t, SC_ROW), U32),
            jax.ShapeDtypeStruct((META_ROWS, t), F32),
            jax.ShapeDtypeStruct((N_EXPERTS, 1), F32),
        ],
        scratch_shapes=[pltpu.VMEM((N_EXPERTS, 1), F32)],
        compiler_params=_params(("arbitrary",)),
        name="router",
    )(h1, g_ffn, w_router, b_router, jnp.asarray(_earlier_matrix(tm), BF16))


def _plan(meta_t, counts, n_tiles):
    e1 = meta_t[0].astype(jnp.int32)
    e2 = meta_t[1].astype(jnp.int32)
    rank1 = meta_t[2].astype(jnp.int32)
    rank2 = meta_t[3].astype(jnp.int32)
    cnt = counts[:, 0].astype(jnp.int32)
    padded = ((cnt + ROW_TILE - 1) // ROW_TILE) * ROW_TILE
    ends = jnp.cumsum(padded)
    starts = ends - padded
    experts = jnp.arange(N_EXPERTS, dtype=jnp.int32)
    pos1 = rank1 + jnp.sum(jnp.where(e1[None, :] == experts[:, None], starts[:, None], 0), axis=0)
    pos2 = rank2 + jnp.sum(jnp.where(e2[None, :] == experts[:, None], starts[:, None], 0), axis=0)
    tile_start = jnp.arange(n_tiles, dtype=jnp.int32) * ROW_TILE
    tile_expert = jnp.sum((tile_start[:, None] >= ends[None, :]).astype(jnp.int32), axis=1)
    tile_expert = jnp.minimum(tile_expert, N_EXPERTS - 1)
    rows_left = jnp.sum(jnp.where(tile_expert[:, None] == experts, cnt + starts, 0), axis=1) - tile_start
    n_valid = jnp.clip(rows_left, 0, ROW_TILE).astype(jnp.int32)
    last_used = jnp.maximum(ends[-1] // ROW_TILE - 1, 0)
    block = jnp.minimum(jnp.arange(n_tiles, dtype=jnp.int32), last_used)
    tile_expert = jnp.sum(jnp.where(block[:, None] == jnp.arange(n_tiles)[None, :],
                                    tile_expert[None, :], 0), axis=1)
    plane = n_tiles * ROW_TILE
    half_rows = jnp.concatenate([pos1, pos1 + plane, pos2, pos2 + plane])[None]
    return half_rows, tile_expert, n_valid, block


def _sc_mesh():
    return plsc.VectorSubcoreMesh(core_axis_name="c", subcore_axis_name="s")


def _sc_scatter_rows(rows, idx, n_out):
    t, width = rows.shape
    steps = t // SC_WINDOW
    half = steps // 2

    @pl.kernel(out_type=jax.ShapeDtypeStruct((n_out, width), rows.dtype), mesh=_sc_mesh(),
               scratch_types=[], name="moe_scatter")
    def scatter(rows_hbm, idx_hbm, out_hbm):
        def body(rows_vmem, idx0_vmem, idx1_vmem):
            pltpu.sync_copy(rows_vmem, out_hbm.at[idx0_vmem.at[0]])
            pltpu.sync_copy(rows_vmem, out_hbm.at[idx1_vmem.at[0]])

        pltpu.emit_pipeline(
            body,
            grid=(2, half),
            in_specs=[pl.BlockSpec((SC_WINDOW, width), lambda c, j: (c * half + j, 0)),
                      pl.BlockSpec((1, SC_WINDOW), lambda c, j: (0, c * half + j)),
                      pl.BlockSpec((1, SC_WINDOW), lambda c, j: (0, steps + c * half + j))],
            out_specs=[],
            core_axis_name=("c", "s"),
            dimension_semantics=(pltpu.PARALLEL, pltpu.PARALLEL),
        )(rows_hbm, idx_hbm, idx_hbm)

    return scatter(rows, idx)


def _sc_gather_rows(table, idx):
    m = idx.shape[1]
    width = table.shape[1]
    steps = m // (2 * SC_WINDOW)

    @pl.kernel(out_type=jax.ShapeDtypeStruct((m, width), table.dtype), mesh=_sc_mesh(),
               scratch_types=[], name="moe_gather")
    def gather(table_hbm, idx_hbm, out_hbm):
        def body(idx_vmem, out_vmem):
            pltpu.sync_copy(table_hbm.at[idx_vmem.at[0]], out_vmem)

        pltpu.emit_pipeline(
            body,
            grid=(2, steps),
            in_specs=[pl.BlockSpec((1, SC_WINDOW), lambda k, j: (0, k * steps + j))],
            out_specs=[pl.BlockSpec((SC_WINDOW, width), lambda k, j: (k * steps + j, 0))],
            core_axis_name=("c", "s"),
            dimension_semantics=(pltpu.PARALLEL, pltpu.PARALLEL),
        )(idx_hbm, out_hbm)

    return gather(table, idx)


def _experts_kernel(te_ref, nv_ref, blk_ref, xs_ref, wg_ref, wu_ref, wd_ref, ys_ref):
    r = pl.program_id(0)
    n_valid = nv_ref[r]

    @pl.when(n_valid > 0)
    def _():
        parts = [p.astype(BF16) for p in _unpack_rows(xs_ref[0]) + _unpack_rows(xs_ref[1])]
        cols = (0, 2 * SC_ROW, SC_ROW, 3 * SC_ROW)
        hg = sum(_dot(p, wg_ref[c:c + SC_ROW, :]) for p, c in zip(parts, cols))
        hu = sum(_dot(p, wu_ref[c:c + SC_ROW, :]) for p, c in zip(parts, cols))
        row = lax.broadcasted_iota(jnp.int32, (ROW_TILE, 1), 0)
        hid = jnp.where(row < n_valid, hg * _sigmoid(hg) * hu, 0.0).astype(BF16)
        _split_planes(_pack_rows(_dot(hid, wd_ref[...])), ys_ref)


def _experts(xs, tile_expert, n_valid, block, w_gate, w_up, w_down):
    n_tiles = xs.shape[1] // ROW_TILE
    w_spec = pl.BlockSpec((None, D_MODEL, D_EXPERT), lambda r, te, nv, blk: (te[r], 0, 0))
    grid_spec = pltpu.PrefetchScalarGridSpec(
        num_scalar_prefetch=3,
        grid=(n_tiles,),
        in_specs=[
            pl.BlockSpec((2, ROW_TILE, SC_ROW), lambda r, te, nv, blk: (0, blk[r], 0)),
            w_spec, w_spec,
            pl.BlockSpec((None, D_EXPERT, D_MODEL), lambda r, te, nv, blk: (te[r], 0, 0)),
        ],
        out_specs=pl.BlockSpec((2, ROW_TILE, SC_ROW), lambda r, te, nv, blk: (0, blk[r], 0)),
    )
    return pl.pallas_call(
        _experts_kernel,
        grid_spec=grid_spec,
        out_shape=jax.ShapeDtypeStruct((2, n_tiles * ROW_TILE, SC_ROW), U32),
        compiler_params=_params(("arbitrary",)),
        name="experts",
    )(tile_expert, n_valid, block, xs, w_gate, w_up, w_down)


def _ple_kernel(h_ref, yg_ref, meta_t_ref, p_ref, wg_ref, bg_ref, wp_ref, gf_ref, *rest, final_norm):
    o_ref = rest[-1]
    meta = jnp.transpose(meta_t_ref[...])
    w1 = meta[:, 4:5]
    w2 = meta[:, 5:6]
    q0, q2 = (w1 * u + w2 * v for u, v in zip(_unpack_rows(yg_ref[0]), _unpack_rows(yg_ref[2])))
    q1, q3 = (w1 * u + w2 * v for u, v in zip(_unpack_rows(yg_ref[1]), _unpack_rows(yg_ref[3])))
    moe = jnp.concatenate([q0, q1, q2, q3], axis=1)
    h = h_ref[...] + moe
    gate = _sigmoid(_dot(_rms(h).astype(BF16), wg_ref[...]) + bg_ref[...])
    h = h + gate * _dot(p_ref[...].astype(BF16), wp_ref[...])
    o_ref[...] = _rms(h, gf_ref[...]) if final_norm else h


def _ple(h1, yg, meta_t, p2, w_gate, b_gate, w_proj, g_final, final_norm, tm, off, prev_out):
    t = h1.shape[0]
    steps = yg.shape[1] // tm
    in_specs = [
        pl.BlockSpec((tm, D_MODEL), lambda i: (i + off, 0)),
        pl.BlockSpec((4, tm, SC_ROW), lambda i: (0, i, 0)),
        pl.BlockSpec((META_ROWS, tm), lambda i: (0, i + off)),
        pl.BlockSpec((tm, D_PLE), lambda i: (i + off, 0)),
        pl.BlockSpec((D_MODEL, D_MODEL), lambda i: (0, 0)),
        pl.BlockSpec((1, D_MODEL), lambda i: (0, 0)),
        pl.BlockSpec((D_PLE, D_MODEL), lambda i: (0, 0)),
        pl.BlockSpec((1, D_MODEL), lambda i: (0, 0)),
    ]
    args = [h1, yg, meta_t, p2, w_gate, b_gate, w_proj, g_final]
    aliases = {}
    if prev_out is not None:
        in_specs.append(pl.BlockSpec(memory_space=pl.ANY))
        args.append(prev_out)
        aliases = {len(args) - 1: 0}
    return pl.pallas_call(
        functools.partial(_ple_kernel, final_norm=final_norm),
        grid=(steps,),
        in_specs=in_specs,
        out_specs=pl.BlockSpec((tm, D_MODEL), lambda i: (i + off, 0)),
        out_shape=jax.ShapeDtypeStruct((t, D_MODEL), F32),
        input_output_aliases=aliases,
        compiler_params=_params(("parallel",)),
        name="ple",
    )(*args)


def kernel(x, p, g_mix, w_in, pool_w, pool_scale, ssm_a_re, ssm_a_im, ssm_log_dt, ssm_b_re,
           ssm_b_im, ssm_c_re, ssm_c_im, ssm_d, glu_w, glu_b, w_out, g_ffn, router_grp_w,
           router_grp_b, router_exp_w, router_exp_b, exp_w_gate, exp_w_up, exp_w_down, g_ple,
           ple_gate_w, ple_gate_b, ple_proj_w, g_final):
    bsz, seq, dm = x.shape
    depth = g_mix.shape[0]
    t = bsz * seq
    seq_chunks = seq // CHUNK
    nch = t // CHUNK
    tm = 1024
    n_sorted = (pl.cdiv(2 * t, ROW_TILE) + N_EXPERTS) * ROW_TILE
    w_gate_all = exp_w_gate.reshape(depth * N_EXPERTS, dm, D_EXPERT)
    w_up_all = exp_w_up.reshape(depth * N_EXPERTS, dm, D_EXPERT)
    w_down_all = exp_w_down.reshape(depth * N_EXPERTS, D_EXPERT, dm)

    h = x.reshape(t, dm)
    for i in range(depth):
        w_in_b = (g_mix[i][:, None] * w_in[i]).astype(BF16)
        zp, ut = _in_proj(h.reshape(nch, CHUNK, dm), w_in_b[:, :D_POOL],
                          jnp.transpose(w_in_b[:, D_POOL:]))
        a = _pool(zp, pool_w[i], pool_scale[i][None], bsz)
        yt, w_gate_b, w_up_b, w_down_b = _ssm(
            ut, ssm_a_re[i], ssm_a_im[i], ssm_log_dt[i], ssm_b_re[i], ssm_b_im[i], ssm_c_re[i],
            ssm_c_im[i], ssm_d[i], seq_chunks, w_gate_all, w_up_all, w_down_all, i)
        h = _mix_out(h.reshape(nch, CHUNK, dm), a, yt,
                     jnp.transpose(glu_w[i]).astype(BF16), glu_b[i][:, None],
                     w_out[i].astype(BF16)).reshape(t, dm)

        eg = EXPERTS_PER_GROUP
        w_router = jnp.concatenate(
            [router_grp_w[i], jnp.zeros((dm, eg - N_EXPERT_GROUPS), F32),
             jnp.transpose(router_exp_w[i], (1, 0, 2)).reshape(dm, N_EXPERTS),
             jnp.zeros((dm, ROUTER_W - eg - N_EXPERTS), F32)], axis=1)
        b_router = jnp.concatenate(
            [router_grp_b[i], jnp.zeros((eg - N_EXPERT_GROUPS,), F32),
             router_exp_b[i].reshape(N_EXPERTS),
             jnp.zeros((ROUTER_W - eg - N_EXPERTS,), F32)])[None]
        vp, meta_t, counts = _router(h, g_ffn[i][None], w_router, b_router, tm)
        idx, tile_expert, n_valid, block = _plan(meta_t, counts, n_sorted // ROW_TILE)
        xs = _sc_scatter_rows(vp.reshape(2 * t, SC_ROW), idx, 2 * n_sorted)
        ys = _experts(xs.reshape(2, n_sorted, SC_ROW), tile_expert, n_valid, block,
                      w_gate_b, w_up_b, w_down_b)
        ys2 = ys.reshape(2 * n_sorted, SC_ROW)
        idx4 = idx.reshape(4, t)
        ple_wg = (g_ple[i][:, None] * ple_gate_w[i]).astype(BF16)
        ple_wp = ple_proj_w[i].astype(BF16)
        out = None
        cuts = (0, t // FIRST_PART, t)
        for lo, hi in zip(cuts[:-1], cuts[1:]):
            yg_q = _sc_gather_rows(ys2, idx4[:, lo:hi].reshape(1, 4 * (hi - lo)))
            out = _ple(h, yg_q.reshape(4, hi - lo, SC_ROW), meta_t, p[i].reshape(t, D_PLE), ple_wg,
                       ple_gate_b[i][None], ple_wp, g_final[None], i == depth - 1, tm, lo // tm, out)
        h = out
    return h.reshape(bsz, seq, dm)
```

```python
import functools
import math

import numpy as np
import jax
import jax.numpy as jnp
from jax import lax
from jax.experimental import pallas as pl
from jax.experimental.pallas import tpu as pltpu
from jax.experimental.pallas import tpu_sc as plsc

F32 = jnp.float32
BF16 = jnp.bfloat16
U32 = jnp.uint32

D_MODEL = 1024
D_POOL = 512
D_SSM = 512
POOL_WINDOWS = (2, 4, 8, 16)
POOL_GROUP = 128
SSM_GROUP = 16
N_SSM_GROUPS = 32
SSM_STATE = 64
N_EXPERT_GROUPS = 4
EXPERTS_PER_GROUP = 8
N_EXPERTS = N_EXPERT_GROUPS * EXPERTS_PER_GROUP
D_EXPERT = 256
D_PLE = 256
RMS_EPS = 1e-6

LANES = 128
CHUNK = 32
CHUNK_W = CHUNK * SSM_GROUP
T_SUB = 8
S_TILE = 8
C_TILE = 128
ROUTER_W = LANES
HALF = D_MODEL // 2
ROW_TILE = 1280
SC_WINDOW = 128
SC_ROW = HALF // 2
META_ROWS = 8
FIRST_PART = 4
VMEM_LIMIT = 56 * 1024 * 1024


def _dot(a, b):
    return jnp.dot(a, b, preferred_element_type=F32)


def _dot_nt(a, b):
    return lax.dot_general(a, b, (((1,), (1,)), ((), ())), preferred_element_type=F32)


def _dot_tn(a, b):
    return lax.dot_general(a, b, (((0,), (0,)), ((), ())), preferred_element_type=F32)


def _rms(x, g=None):
    y = x * lax.rsqrt(jnp.mean(x * x, axis=-1, keepdims=True) + RMS_EPS)
    return y if g is None else y * g


def _sigmoid(x):
    return 1.0 / (1.0 + jnp.exp(-x))


def _params(sem):
    return pltpu.CompilerParams(dimension_semantics=sem, vmem_limit_bytes=VMEM_LIMIT)


def _tile_step():
    n_s = pl.num_programs(1)
    return pl.program_id(0) * n_s + pl.program_id(1), pl.num_programs(0) * n_s


def _row_copies(hbm, buf, sem, step_idx, slot_idx, to_hbm):
    n_s = pl.num_programs(1)
    c0 = (step_idx // n_s) * C_TILE
    s0 = (step_idx % n_s) * S_TILE
    out = []
    for j in range(S_TILE):
        far, near = hbm.at[pl.ds(c0, C_TILE), s0 + j, :], buf.at[slot_idx, j]
        src, dst = (near, far) if to_hbm else (far, near)
        out.append(pltpu.make_async_copy(src, dst, sem.at[slot_idx]))
    return out


def _load_time_major_rows(hbm, buf, sem):
    step, n_steps = _tile_step()
    slot = step % 2

    @pl.when(step == 0)
    def _():
        for cp in _row_copies(hbm, buf, sem, step, slot, False):
            cp.start()

    @pl.when(step + 1 < n_steps)
    def _():
        for cp in _row_copies(hbm, buf, sem, step + 1, 1 - slot, False):
            cp.start()

    for cp in _row_copies(hbm, buf, sem, step, slot, False):
        cp.wait()
    return buf[slot].reshape(S_TILE * C_TILE, buf.shape[-1])


def _store_time_major_rows(val, hbm, buf, sem):
    step, n_steps = _tile_step()
    slot = step % 2

    @pl.when(step >= 2)
    def _():
        for cp in _row_copies(hbm, buf, sem, step - 2, slot, True):
            cp.wait()

    buf[slot] = val.reshape(S_TILE, C_TILE, val.shape[-1])
    for cp in _row_copies(hbm, buf, sem, step, slot, True):
        cp.start()

    @pl.when(step == n_steps - 1)
    def _():
        @pl.when(step >= 1)
        def _():
            for cp in _row_copies(hbm, buf, sem, step - 1, 1 - slot, True):
                cp.wait()
        for cp in _row_copies(hbm, buf, sem, step, slot, True):
            cp.wait()


def _in_proj_kernel(x_hbm, wp_ref, wst_ref, zp_ref, ut_ref, xbuf, xsem):
    nc = C_TILE
    u = _rms(_load_time_major_rows(x_hbm, xbuf, xsem)).astype(BF16)
    zp_ref[...] = _dot(u, wp_ref[...]).reshape(zp_ref.shape).astype(zp_ref.dtype)
    zt = _dot_nt(wst_ref[...], u).astype(BF16)
    for j in range(S_TILE):
        ut_ref[:, j, :, :] = zt[:, j * nc:(j + 1) * nc].reshape(N_SSM_GROUPS, SSM_GROUP, nc)


def _in_proj(x3, w_pool, w_ssm_t):
    nch = x3.shape[0]
    return pl.pallas_call(
        _in_proj_kernel,
        grid=(nch // C_TILE, CHUNK // S_TILE),
        in_specs=[
            pl.BlockSpec(memory_space=pl.ANY),
            pl.BlockSpec((D_MODEL, D_POOL), lambda c, s: (0, 0)),
            pl.BlockSpec((D_SSM, D_MODEL), lambda c, s: (0, 0)),
        ],
        out_specs=[
            pl.BlockSpec((S_TILE, C_TILE, D_POOL), lambda c, s: (s, c, 0)),
            pl.BlockSpec((N_SSM_GROUPS, None, S_TILE, SSM_GROUP, C_TILE), lambda c, s: (0, c, s, 0, 0)),
        ],
        out_shape=[
            jax.ShapeDtypeStruct((CHUNK, nch, D_POOL), BF16),
            jax.ShapeDtypeStruct((N_SSM_GROUPS, nch // C_TILE, CHUNK, SSM_GROUP, C_TILE), BF16),
        ],
        scratch_shapes=[
            pltpu.VMEM((2, S_TILE, C_TILE, D_MODEL), F32),
            pltpu.SemaphoreType.DMA((2,)),
        ],
        compiler_params=_params(("arbitrary", "arbitrary")),
        name="in_proj",
    )(x3, w_pool, w_ssm_t)


def _pool_kernel(z_ref, w_ref, sc_ref, o_ref):
    n_c = z_ref.shape[1]
    gi = pl.program_id(1)
    chunk = lax.broadcasted_iota(jnp.int32, (n_c, 1), 0)

    def plane(p):
        if p < 0:
            return jnp.where(chunk >= 1, pltpu.roll(z_ref[p + CHUNK].astype(F32), 1, 0), 0.0)
        if p >= CHUNK:
            return jnp.where(chunk < n_c - 1, pltpu.roll(z_ref[p - CHUNK].astype(F32), n_c - 1, 0), 0.0)
        return z_ref[p].astype(F32)

    def inv_count(s_t, half):
        n = n_c * CHUNK
        first = min(s_t + half, n) - max(s_t - half, 0)
        t_last = n - CHUNK + s_t
        last = min(t_last + half, n) - max(t_last - half, 0)
        inv = 1.0 / (2 * half)
        if first != 2 * half:
            inv = jnp.where(chunk == 0, 1.0 / first, inv)
        if last != 2 * half:
            inv = jnp.where(chunk == n_c - 1, 1.0 / last, inv)
        return inv

    for k, w in enumerate(POOL_WINDOWS):
        @pl.when(gi == k)
        def _(w=w):
            half = w // 2
            total = plane(-half)
            for p in range(-half + 1, half):
                total = total + plane(p)
            diffs = []
            for s_t in range(CHUNK):
                diffs.append((total * inv_count(s_t, half) - plane(s_t)).astype(BF16))
                if s_t + 1 < CHUNK:
                    total = total + plane(s_t + half) - plane(s_t - half)
            diff = jnp.concatenate(diffs, axis=0)
            out = _dot(diff, w_ref[...].astype(BF16)) * sc_ref[...]
            o_ref[...] = out.reshape(o_ref.shape).astype(o_ref.dtype)


def _pool(zp_t, pool_w, pool_scale, bsz):
    _, nch, _ = zp_t.shape
    n_c = nch // bsz
    return pl.pallas_call(
        _pool_kernel,
        grid=(bsz, len(POOL_WINDOWS)),
        in_specs=[
            pl.BlockSpec((CHUNK, n_c, POOL_GROUP), lambda i, g: (0, i, g)),
            pl.BlockSpec((None, POOL_GROUP, POOL_GROUP), lambda i, g: (g, 0, 0)),
            pl.BlockSpec((1, POOL_GROUP), lambda i, g: (0, g)),
        ],
        out_specs=pl.BlockSpec((CHUNK, n_c, POOL_GROUP), lambda i, g: (0, i, g)),
        out_shape=jax.ShapeDtypeStruct(zp_t.shape, BF16),
        compiler_params=_params(("parallel", "parallel")),
        name="pool",
    )(zp_t, pool_w, pool_scale)


def _expand_consts():
    time = np.arange(CHUNK_W) // SSM_GROUP
    def onehot(e):
        m = np.zeros((CHUNK_W, LANES), np.float32)
        m[np.arange(CHUNK_W), e] = 1.0
        return m
    return np.stack([
        onehot(CHUNK - 1 - time),
        onehot(time),
        onehot(time + 1),
        onehot(CHUNK - time),
    ])


def _ssm_kernel(u_ref, vec_ref, mat_ref, exp_ref, wg_ref, wu_ref, wd_ref, y_ref, og_ref, ou_ref,
                od_ref, *, seq_chunks):
    og_ref[...] = wg_ref[...].astype(BF16)
    ou_ref[...] = wu_ref[...].astype(BF16)
    od_ref[...] = wd_ref[...].astype(BF16)

    n_ct = u_ref.shape[0]
    nch = n_ct * C_TILE
    half = LANES // 2
    lane = lax.broadcasted_iota(jnp.int32, (1, LANES), 1)
    lo_half = lane < half

    def direction(di):
        a_re = vec_ref[di, 0:1]
        a_im = vec_ref[di, 1:2]
        dt = jnp.exp(vec_ref[di, 2:3])
        mag = jnp.exp(a_re * dt)
        ang = a_im * dt
        lb_re = mag * jnp.cos(ang)
        lb_im = mag * jnp.sin(ang)
        den = a_re * a_re + a_im * a_im
        f_re = ((lb_re - 1.0) * a_re + lb_im * a_im) / den
        f_im = (lb_im * a_re - (lb_re - 1.0) * a_im) / den
        return (lb_re, lb_im), f_re, f_im

    def power_table(lam):
        e = lax.broadcasted_iota(jnp.int32, (LANES, 1), 0)
        sq_re, sq_im = lam
        t_re = jnp.ones((LANES, LANES), F32)
        t_im = jnp.zeros((LANES, LANES), F32)
        for k in range(CHUNK.bit_length()):
            bit = ((e >> k) & 1) == 1
            t_re, t_im = (jnp.where(bit, t_re * sq_re - t_im * sq_im, t_re),
                          jnp.where(bit, t_re * sq_im + t_im * sq_re, t_im))
            sq_re, sq_im = sq_re * sq_re - sq_im * sq_im, 2.0 * sq_re * sq_im
        return jnp.where(lo_half, t_re, t_im), jnp.where(lo_half, t_im, t_re)

    def tile_rows(x16):
        return jnp.broadcast_to(x16[None], (CHUNK, SSM_GROUP, LANES)).reshape(CHUNK_W, LANES)

    def expanded(tab, which, v_re, v_im, conj_sign):
        lexp = _dot(exp_ref[which], jnp.concatenate(tab, axis=1).astype(BF16))
        if conj_sign > 0:
            p = jnp.where(lo_half, v_re, v_re)
            q = jnp.where(lo_half, -v_im, v_im)
        else:
            p = jnp.where(lo_half, v_re, -v_re)
            q = jnp.where(lo_half, -v_im, -v_im)
        return lexp[:, :LANES] * tile_rows(p) + lexp[:, LANES:] * tile_rows(q)

    lam_f, ff_re, ff_im = direction(0)
    lam_b, fb_re, fb_im = direction(1)
    tabs_f = power_table(lam_f)
    tabs_b = power_table(lam_b)
    tab_f, tab_b = tabs_f[0], tabs_b[0]

    def bbar(bt_re, bt_im, f_re, f_im):
        return bt_re * f_re - bt_im * f_im, bt_re * f_im + bt_im * f_re

    def mat(di, k):
        return mat_ref[di, k * SSM_GROUP:(k + 1) * SSM_GROUP, :]

    bf_re, bf_im = bbar(mat(0, 0), mat(0, 1), ff_re, ff_im)
    bb_re, bb_im = bbar(mat(1, 0), mat(1, 1), fb_re, fb_im)

    pb1 = expanded(tabs_f, 0, bf_re, bf_im, 1)
    pb2 = expanded(tabs_b, 1, bb_re, bb_im, 1)
    pb3 = expanded(tabs_b, 2, bb_re, bb_im, 1)
    ft_f = expanded(tabs_f, 2, mat(0, 3), mat(0, 4), -1)
    ft_b = expanded(tabs_b, 3, mat(1, 3), mat(1, 4), -1)

    row_w = lax.broadcasted_iota(jnp.int32, (CHUNK_W, 1), 0)
    last_blk = row_w >= CHUNK_W - SSM_GROUP
    pb2_lag0 = jnp.where(last_blk, pltpu.roll(pb2, CHUNK_W - SSM_GROUP, 0), 0.0)
    ccr_f = mat(0, 2).astype(BF16)
    ccr_b = mat(1, 2).astype(BF16)
    r_lo = _dot_nt(ccr_f, pb1.astype(BF16)) + _dot_nt(ccr_b, pb2_lag0.astype(BF16))
    co = lax.broadcasted_iota(jnp.int32, (SSM_GROUP, CHUNK_W), 0)
    col = lax.broadcasted_iota(jnp.int32, (SSM_GROUP, CHUNK_W), 1)
    r_lo = r_lo + jnp.where(col == CHUNK_W - SSM_GROUP + co, mat(0, 5)[:, 0:1], 0.0)
    r_hi = _dot_nt(ccr_b, pb3.astype(BF16))
    r_t = jnp.concatenate([r_lo, r_hi], axis=1)
    g_t = jnp.concatenate(
        [pltpu.roll(r_t, SSM_GROUP * (tl + 1), 1) for tl in range(T_SUB)], axis=0
    ).astype(BF16)

    u = jnp.concatenate([u_ref[ct].reshape(CHUNK_W, C_TILE) for ct in range(n_ct)],
                        axis=1)
    e_mat = jnp.concatenate([pb1, pb2], axis=1).astype(BF16)
    xend = _dot_tn(e_mat, u)
    lanec = lax.broadcasted_iota(jnp.int32, (1, nch), 1) % seq_chunks
    ns = SSM_STATE

    def scan(re, im, tab, forward):
        lam_col = jnp.transpose(tab[CHUNK:CHUNK + 8, :])[:, 0:1]
        a, b = lam_col[:ns], lam_col[ns:]
        n_steps = int(math.log2(seq_chunks))

        def rolled(v, d):
            if d % LANES == 0:
                return jnp.concatenate([v[:, nch - d:], v[:, :nch - d]], axis=1)
            return pltpu.roll(v, d, 1)

        def shifted(v, d):
            if forward:
                return jnp.where(lanec >= d, rolled(v, d), 0.0)
            return jnp.where(lanec < seq_chunks - d, rolled(v, nch - d), 0.0)

        for k in range(n_steps):
            sr, si = shifted(re, 1 << k), shifted(im, 1 << k)
            re, im = re + (sr * a - si * b), im + (sr * b + si * a)
            a, b = a * a - b * b, 2.0 * a * b
        return shifted(re, 1), shifted(im, 1)

    f_re, f_im = scan(xend[:ns], xend[ns:2 * ns], tab_f, True)
    b_re, b_im = scan(xend[2 * ns:3 * ns], xend[3 * ns:], tab_b, False)
    xin = jnp.concatenate([f_re, f_im, b_re, b_im], axis=0).astype(BF16)
    f_t = jnp.concatenate([ft_f, ft_b], axis=1).astype(BF16)

    toeplitz = jnp.concatenate(
        [g_t[:, CHUNK_W - LANES * th:2 * CHUNK_W - LANES * th] for th in range(CHUNK // T_SUB)],
        axis=0)
    y_t = _dot(toeplitz, u) + _dot(f_t, xin)
    for ct in range(n_ct):
        y_ref[ct] = y_t[:, ct * C_TILE:(ct + 1) * C_TILE].reshape(
            CHUNK, SSM_GROUP, C_TILE).astype(y_ref.dtype)


def _ssm(ut, a_re, a_im, log_dt, b_re, b_im, c_re, c_im, d, seq_chunks, w_gate, w_up, w_down, layer):
    g, n_ct = ut.shape[:2]
    assert N_EXPERTS % g == 0
    epg = N_EXPERTS // g
    base = layer * g
    n = SSM_STATE

    def per_group(a):
        return jnp.swapaxes(a, 0, 1)

    def dup(a):
        return jnp.concatenate([a, a], axis=-1)

    vecs = dup(jnp.stack([per_group(a_re), per_group(a_im),
                          jnp.broadcast_to(per_group(log_dt)[..., None], (g, 2, n))], axis=2))
    cr, ci = per_group(c_re), per_group(c_im)
    d_blk = jnp.broadcast_to(d.reshape(g, 1, SSM_GROUP, 1), (g, 2, SSM_GROUP, LANES))
    mats = jnp.concatenate(
        [dup(jnp.swapaxes(per_group(b_re), 2, 3)), dup(jnp.swapaxes(per_group(b_im), 2, 3)),
         jnp.concatenate([cr, -ci], axis=-1), dup(cr), dup(ci), d_blk], axis=2)
    exp_c = jnp.asarray(_expand_consts(), BF16)

    return pl.pallas_call(
        functools.partial(_ssm_kernel, seq_chunks=seq_chunks),
        grid=(g,),
        in_specs=[
            pl.BlockSpec((None, n_ct, CHUNK, SSM_GROUP, C_TILE), lambda i: (i, 0, 0, 0, 0)),
            pl.BlockSpec((None, 2, 3, LANES), lambda i: (i, 0, 0, 0)),
            pl.BlockSpec((None, 2, 6 * SSM_GROUP, LANES), lambda i: (i, 0, 0, 0)),
            pl.BlockSpec((4, CHUNK_W, LANES), lambda i: (0, 0, 0)),
            pl.BlockSpec((epg, D_MODEL, D_EXPERT), lambda i: (base + i, 0, 0)),
            pl.BlockSpec((epg, D_MODEL, D_EXPERT), lambda i: (base + i, 0, 0)),
            pl.BlockSpec((epg, D_EXPERT, D_MODEL), lambda i: (base + i, 0, 0)),
        ],
        out_specs=[
            pl.BlockSpec((None, n_ct, CHUNK, SSM_GROUP, C_TILE), lambda i: (i, 0, 0, 0, 0)),
            pl.BlockSpec((epg, D_MODEL, D_EXPERT), lambda i: (i, 0, 0)),
            pl.BlockSpec((epg, D_MODEL, D_EXPERT), lambda i: (i, 0, 0)),
            pl.BlockSpec((epg, D_EXPERT, D_MODEL), lambda i: (i, 0, 0)),
        ],
        out_shape=[
            jax.ShapeDtypeStruct(ut.shape, BF16),
            jax.ShapeDtypeStruct((N_EXPERTS, D_MODEL, D_EXPERT), BF16),
            jax.ShapeDtypeStruct((N_EXPERTS, D_MODEL, D_EXPERT), BF16),
            jax.ShapeDtypeStruct((N_EXPERTS, D_EXPERT, D_MODEL), BF16),
        ],
        compiler_params=_params(("parallel",)),
        name="ssm",
    )(ut, vecs, mats, exp_c, w_gate, w_up, w_down)


def _mix_out_kernel(x_hbm, a_ref, yt_ref, gwt_ref, gb_ref, wo_ref, h_hbm, xbuf, hbuf, xsem, hsem):
    nc = C_TILE
    x = _load_time_major_rows(x_hbm, xbuf, xsem)
    a = a_ref[...].reshape(S_TILE * nc, D_POOL)
    y = jnp.concatenate([yt_ref[:, j, :, :].reshape(D_SSM, nc) for j in range(S_TILE)],
                        axis=1).astype(F32)
    z = 0.5 * y * (1.0 + jnp.tanh(math.sqrt(2.0 / math.pi) * (y + 0.044715 * (y * y * y))))
    gate = _sigmoid(_dot(gwt_ref[...], z.astype(BF16)) + gb_ref[...])
    s = (z * gate).astype(BF16)
    h = x + _dot(a, wo_ref[:D_POOL, :]) + _dot_tn(s, wo_ref[D_POOL:, :])
    _store_time_major_rows(h, h_hbm, hbuf, hsem)


def _mix_out(x3, a3, yt, glu_w_t, glu_b_col, w_out):
    nch = x3.shape[0]
    return pl.pallas_call(
        _mix_out_kernel,
        grid=(nch // C_TILE, CHUNK // S_TILE),
        in_specs=[
            pl.BlockSpec(memory_space=pl.ANY),
            pl.BlockSpec((S_TILE, C_TILE, D_POOL), lambda c, t: (t, c, 0)),
            pl.BlockSpec((N_SSM_GROUPS, None, S_TILE, SSM_GROUP, C_TILE), lambda c, t: (0, c, t, 0, 0)),
            pl.BlockSpec((D_SSM, D_SSM), lambda c, t: (0, 0)),
            pl.BlockSpec((D_SSM, 1), lambda c, t: (0, 0)),
            pl.BlockSpec((D_MODEL, D_MODEL), lambda c, t: (0, 0)),
        ],
        out_specs=pl.BlockSpec(memory_space=pl.ANY),
        out_shape=jax.ShapeDtypeStruct((nch, CHUNK, D_MODEL), F32),
        scratch_shapes=[
            pltpu.VMEM((2, S_TILE, C_TILE, D_MODEL), F32),
            pltpu.VMEM((2, S_TILE, C_TILE, D_MODEL), F32),
            pltpu.SemaphoreType.DMA((2,)),
            pltpu.SemaphoreType.DMA((2,)),
        ],
        compiler_params=_params(("arbitrary", "arbitrary")),
        name="mix_out",
    )(x3, a3, yt, glu_w_t, glu_b_col, w_out)


def _pack_rows(x):
    b = lax.bitcast_convert_type(x.astype(BF16).astype(F32), U32)
    return (b[:, :HALF] & jnp.uint32(0xFFFF0000)) | (b[:, HALF:] >> 16)


def _unpack_rows(w):
    lo = lax.bitcast_convert_type(w & jnp.uint32(0xFFFF0000), F32)
    hi = lax.bitcast_convert_type(w << 16, F32)
    return lo, hi


def _split_bf16(x):
    hi = x.astype(BF16)
    return hi, (x - hi.astype(F32)).astype(BF16)


def _route(v32, wr_ref, br_ref):
    v_hi, v_lo = _split_bf16(v32)
    w_hi, w_lo = _split_bf16(wr_ref[...])
    both = _dot(v_hi, jnp.concatenate([w_hi, w_lo], axis=1))
    logits = both[:, :ROUTER_W] + (both[:, ROUTER_W:] + _dot(v_lo, w_hi)) + br_ref[...]
    return jnp.transpose(logits)


def _top1(x, valid=None):
    n = x.shape[0]
    row = lax.broadcasted_iota(jnp.int32, x.shape, 0).astype(F32)
    if valid is not None:
        x = jnp.where(valid, x, -jnp.inf)
    m = jnp.max(x, axis=0, keepdims=True)
    idx = jnp.min(jnp.where(x == m, row, float(n)), axis=0, keepdims=True)
    return m, idx, x, row


def _split_planes(packed, ref):
    ref[0] = packed[:, :SC_ROW]
    ref[1] = packed[:, SC_ROW:]


def _router_kernel(h_ref, g_ref, wr_ref, br_ref, before_ref, vp_ref, meta_t_ref, cnt_ref, carry_ref):
    @pl.when(pl.program_id(0) == 0)
    def _():
        carry_ref[...] = jnp.zeros_like(carry_ref)

    v32 = _rms(h_ref[...], g_ref[...])
    _split_planes(_pack_rows(v32), vp_ref)
    lt = _route(v32, wr_ref, br_ref)
    tm = lt.shape[1]
    eg = EXPERTS_PER_GROUP

    grp = lt[:eg]
    grp_row = lax.broadcasted_iota(jnp.int32, grp.shape, 0)
    mg, grp_idx, grp, _ = _top1(grp, grp_row < N_EXPERT_GROUPS)
    grp_p = 1.0 / jnp.sum(jnp.exp(grp - mg), axis=0, keepdims=True)
    le = jnp.zeros((eg, tm), F32)
    for g in range(N_EXPERT_GROUPS):
        le = jnp.where(grp_idx == float(g), lt[eg * (g + 1):eg * (g + 2)], le)
    m1, i1, le, row = _top1(le)
    z = jnp.sum(jnp.exp(le - m1), axis=0, keepdims=True)
    m2, i2, _, _ = _top1(jnp.where(row == i1, -jnp.inf, le))
    p1 = 1.0 / z
    p2 = jnp.exp(m2 - m1) / z
    tot = p1 + p2
    w1 = grp_p * (p1 / tot)
    w2 = grp_p * (p2 / tot)
    e1 = grp_idx * eg + i1
    e2 = grp_idx * eg + i2

    erow = lax.broadcasted_iota(jnp.int32, (N_EXPERTS, tm), 0).astype(F32)
    onehot = jnp.where(erow == e1, 1.0, jnp.where(erow == e2, 1.0, 0.0))
    before = _dot(onehot.astype(BF16), before_ref[...]) + carry_ref[...]
    rank1 = jnp.sum(jnp.where(erow == e1, before, 0.0), axis=0, keepdims=True)
    rank2 = jnp.sum(jnp.where(erow == e2, before, 0.0), axis=0, keepdims=True)
    carry = carry_ref[...] + jnp.sum(onehot, axis=1, keepdims=True)
    carry_ref[...] = carry
    cnt_ref[...] = carry

    mrow = lax.broadcasted_iota(jnp.int32, (META_ROWS, tm), 0)
    meta_t_ref[...] = jnp.where(mrow == 0, e1, jnp.where(mrow == 1, e2, jnp.where(
        mrow == 2, rank1, jnp.where(mrow == 3, rank2, jnp.where(
            mrow == 4, w1, jnp.where(mrow == 5, w2, 0.0))))))


def _earlier_matrix(tm):
    return np.triu(np.ones((tm, tm), np.float32), k=1)


def _router(h1, g_ffn, w_router, b_router, tm):
    t = h1.shape[0]
    return pl.pallas_call(
        _router_kernel,
        grid=(t // tm,),
        in_specs=[
            pl.BlockSpec((tm, D_MODEL), lambda i: (i, 0)),
            pl.BlockSpec((1, D_MODEL), lambda i: (0, 0)),
            pl.BlockSpec((D_MODEL, ROUTER_W), lambda i: (0, 0)),
            pl.BlockSpec((1, ROUTER_W), lambda i: (0, 0)),
            pl.BlockSpec((tm, tm), lambda i: (0, 0)),
        ],
        out_specs=[
            pl.BlockSpec((2, tm, SC_ROW), lambda i: (0, i, 0)),
            pl.BlockSpec((META_ROWS, tm), lambda i: (0, i)),
            pl.BlockSpec((N_EXPERTS, 1), lambda i: (0, 0)),
        ],
        out_shape=[
            jax.ShapeDtypeStruct((2, t, SC_ROW), U32),
            jax.ShapeDtypeStruct((META_ROWS, t), F32),
            jax.ShapeDtypeStruct((N_EXPERTS, 1), F32),
        ],
        scratch_shapes=[pltpu.VMEM((N_EXPERTS, 1), F32)],
        compiler_params=_params(("arbitrary",)),
        name="router",
    )(h1, g_ffn, w_router, b_router, jnp.asarray(_earlier_matrix(tm), BF16))


def _plan(meta_t, counts, n_tiles):
    e1 = meta_t[0].astype(jnp.int32)
    e2 = meta_t[1].astype(jnp.int32)
    rank1 = meta_t[2].astype(jnp.int32)
    rank2 = meta_t[3].astype(jnp.int32)
    cnt = counts[:, 0].astype(jnp.int32)
    padded = ((cnt + ROW_TILE - 1) // ROW_TILE) * ROW_TILE
    ends = jnp.cumsum(padded)
    starts = ends - padded
    experts = jnp.arange(N_EXPERTS, dtype=jnp.int32)
    pos1 = rank1 + jnp.sum(jnp.where(e1[None, :] == experts[:, None], starts[:, None], 0), axis=0)
    pos2 = rank2 + jnp.sum(jnp.where(e2[None, :] == experts[:, None], starts[:, None], 0), axis=0)
    tile_start = jnp.arange(n_tiles, dtype=jnp.int32) * ROW_TILE
    tile_expert = jnp.sum((tile_start[:, None] >= ends[None, :]).astype(jnp.int32), axis=1)
    tile_expert = jnp.minimum(tile_expert, N_EXPERTS - 1)
    rows_left = jnp.sum(jnp.where(tile_expert[:, None] == experts, cnt + starts, 0), axis=1) - tile_start
    n_valid = jnp.clip(rows_left, 0, ROW_TILE).astype(jnp.int32)
    last_used = jnp.maximum(ends[-1] // ROW_TILE - 1, 0)
    block = jnp.minimum(jnp.arange(n_tiles, dtype=jnp.int32), last_used)
    tile_expert = jnp.sum(jnp.where(block[:, None] == jnp.arange(n_tiles)[None, :],
                                    tile_expert[None, :], 0), axis=1)
    plane = n_tiles * ROW_TILE
    half_rows = jnp.concatenate([pos1, pos1 + plane, pos2, pos2 + plane])[None]
    return half_rows, tile_expert, n_valid, block


def _sc_mesh():
    return plsc.VectorSubcoreMesh(core_axis_name="c", subcore_axis_name="s")


def _sc_scatter_rows(rows, idx, n_out):
    t, width = rows.shape
    steps = t // SC_WINDOW
    half = steps // 2

    @pl.kernel(out_type=jax.ShapeDtypeStruct((n_out, width), rows.dtype), mesh=_sc_mesh(),
               scratch_types=[], name="moe_scatter")
    def scatter(rows_hbm, idx_hbm, out_hbm):
        def body(rows_vmem, idx0_vmem, idx1_vmem):
            pltpu.sync_copy(rows_vmem, out_hbm.at[idx0_vmem.at[0]])
            pltpu.sync_copy(rows_vmem, out_hbm.at[idx1_vmem.at[0]])

        pltpu.emit_pipeline(
            body,
            grid=(2, half),
            in_specs=[pl.BlockSpec((SC_WINDOW, width), lambda c, j: (c * half + j, 0)),
                      pl.BlockSpec((1, SC_WINDOW), lambda c, j: (0, c * half + j)),
                      pl.BlockSpec((1, SC_WINDOW), lambda c, j: (0, steps + c * half + j))],
            out_specs=[],
            core_axis_name=("c", "s"),
            dimension_semantics=(pltpu.PARALLEL, pltpu.PARALLEL),
        )(rows_hbm, idx_hbm, idx_hbm)

    return scatter(rows, idx)


def _sc_gather_rows(table, idx):
    m = idx.shape[1]
    width = table.shape[1]
    steps = m // (2 * SC_WINDOW)

    @pl.kernel(out_type=jax.ShapeDtypeStruct((m, width), table.dtype), mesh=_sc_mesh(),
               scratch_types=[], name="moe_gather")
    def gather(table_hbm, idx_hbm, out_hbm):
        def body(idx_vmem, out_vmem):
            pltpu.sync_copy(table_hbm.at[idx_vmem.at[0]], out_vmem)

        pltpu.emit_pipeline(
            body,
            grid=(2, steps),
            in_specs=[pl.BlockSpec((1, SC_WINDOW), lambda k, j: (0, k * steps + j))],
            out_specs=[pl.BlockSpec((SC_WINDOW, width), lambda k, j: (k * steps + j, 0))],
            core_axis_name=("c", "s"),
            dimension_semantics=(pltpu.PARALLEL, pltpu.PARALLEL),
        )(idx_hbm, out_hbm)

    return gather(table, idx)


def _experts_kernel(te_ref, nv_ref, blk_ref, xs_ref, wg_ref, wu_ref, wd_ref, ys_ref):
    r = pl.program_id(0)
    n_valid = nv_ref[r]

    @pl.when(n_valid > 0)
    def _():
        parts = [p.astype(BF16) for p in _unpack_rows(xs_ref[0]) + _unpack_rows(xs_ref[1])]
        cols = (0, 2 * SC_ROW, SC_ROW, 3 * SC_ROW)
        hg = sum(_dot(p, wg_ref[c:c + SC_ROW, :]) for p, c in zip(parts, cols))
        hu = sum(_dot(p, wu_ref[c:c + SC_ROW, :]) for p, c in zip(parts, cols))
        row = lax.broadcasted_iota(jnp.int32, (ROW_TILE, 1), 0)
        hid = jnp.where(row < n_valid, hg * _sigmoid(hg) * hu, 0.0).astype(BF16)
        _split_planes(_pack_rows(_dot(hid, wd_ref[...])), ys_ref)


def _experts(xs, tile_expert, n_valid, block, w_gate, w_up, w_down):
    n_tiles = xs.shape[1] // ROW_TILE
    w_spec = pl.BlockSpec((None, D_MODEL, D_EXPERT), lambda r, te, nv, blk: (te[r], 0, 0))
    grid_spec = pltpu.PrefetchScalarGridSpec(
        num_scalar_prefetch=3,
        grid=(n_tiles,),
        in_specs=[
            pl.BlockSpec((2, ROW_TILE, SC_ROW), lambda r, te, nv, blk: (0, blk[r], 0)),
            w_spec, w_spec,
            pl.BlockSpec((None, D_EXPERT, D_MODEL), lambda r, te, nv, blk: (te[r], 0, 0)),
        ],
        out_specs=pl.BlockSpec((2, ROW_TILE, SC_ROW), lambda r, te, nv, blk: (0, blk[r], 0)),
    )
    return pl.pallas_call(
        _experts_kernel,
        grid_spec=grid_spec,
        out_shape=jax.ShapeDtypeStruct((2, n_tiles * ROW_TILE, SC_ROW), U32),
        compiler_params=_params(("arbitrary",)),
        name="experts",
    )(tile_expert, n_valid, block, xs, w_gate, w_up, w_down)


def _ple_kernel(h_ref, yg_ref, meta_t_ref, p_ref, wg_ref, bg_ref, wp_ref, gf_ref, *rest, final_norm):
    o_ref = rest[-1]
    meta = jnp.transpose(meta_t_ref[...])
    w1 = meta[:, 4:5]
    w2 = meta[:, 5:6]
    q0, q2 = (w1 * u + w2 * v for u, v in zip(_unpack_rows(yg_ref[0]), _unpack_rows(yg_ref[2])))
    q1, q3 = (w1 * u + w2 * v for u, v in zip(_unpack_rows(yg_ref[1]), _unpack_rows(yg_ref[3])))
    moe = jnp.concatenate([q0, q1, q2, q3], axis=1)
    h = h_ref[...] + moe
    gate = _sigmoid(_dot(_rms(h).astype(BF16), wg_ref[...]) + bg_ref[...])
    h = h + gate * _dot(p_ref[...].astype(BF16), wp_ref[...])
    o_ref[...] = _rms(h, gf_ref[...]) if final_norm else h


def _ple(h1, yg, meta_t, p2, w_gate, b_gate, w_proj, g_final, final_norm, tm, off, prev_out):
    t = h1.shape[0]
    steps = yg.shape[1] // tm
    in_specs = [
        pl.BlockSpec((tm, D_MODEL), lambda i: (i + off, 0)),
        pl.BlockSpec((4, tm, SC_ROW), lambda i: (0, i, 0)),
        pl.BlockSpec((META_ROWS, tm), lambda i: (0, i + off)),
        pl.BlockSpec((tm, D_PLE), lambda i: (i + off, 0)),
        pl.BlockSpec((D_MODEL, D_MODEL), lambda i: (0, 0)),
        pl.BlockSpec((1, D_MODEL), lambda i: (0, 0)),
        pl.BlockSpec((D_PLE, D_MODEL), lambda i: (0, 0)),
        pl.BlockSpec((1, D_MODEL), lambda i: (0, 0)),
    ]
    args = [h1, yg, meta_t, p2, w_gate, b_gate, w_proj, g_final]
    aliases = {}
    if prev_out is not None:
        in_specs.append(pl.BlockSpec(memory_space=pl.ANY))
        args.append(prev_out)
        aliases = {len(args) - 1: 0}
    return pl.pallas_call(
        functools.partial(_ple_kernel, final_norm=final_norm),
        grid=(steps,),
        in_specs=in_specs,
        out_specs=pl.BlockSpec((tm, D_MODEL), lambda i: (i + off, 0)),
        out_shape=jax.ShapeDtypeStruct((t, D_MODEL), F32),
        input_output_aliases=aliases,
        compiler_params=_params(("parallel",)),
        name="ple",
    )(*args)


def kernel(x, p, g_mix, w_in, pool_w, pool_scale, ssm_a_re, ssm_a_im, ssm_log_dt, ssm_b_re,
           ssm_b_im, ssm_c_re, ssm_c_im, ssm_d, glu_w, glu_b, w_out, g_ffn, router_grp_w,
           router_grp_b, router_exp_w, router_exp_b, exp_w_gate, exp_w_up, exp_w_down, g_ple,
           ple_gate_w, ple_gate_b, ple_proj_w, g_final):
    bsz, seq, dm = x.shape
    depth = g_mix.shape[0]
    t = bsz * seq
    seq_chunks = seq // CHUNK
    nch = t // CHUNK
    tm = 1024
    n_sorted = (pl.cdiv(2 * t, ROW_TILE) + N_EXPERTS) * ROW_TILE
    w_gate_all = exp_w_gate.reshape(depth * N_EXPERTS, dm, D_EXPERT)
    w_up_all = exp_w_up.reshape(depth * N_EXPERTS, dm, D_EXPERT)
    w_down_all = exp_w_down.reshape(depth * N_EXPERTS, D_EXPERT, dm)

    h = x.reshape(t, dm)
    for i in range(depth):
        w_in_b = (g_mix[i][:, None] * w_in[i]).astype(BF16)
        zp, ut = _in_proj(h.reshape(nch, CHUNK, dm), w_in_b[:, :D_POOL],
                          jnp.transpose(w_in_b[:, D_POOL:]))
        a = _pool(zp, pool_w[i], pool_scale[i][None], bsz)
        yt, w_gate_b, w_up_b, w_down_b = _ssm(
            ut, ssm_a_re[i], ssm_a_im[i], ssm_log_dt[i], ssm_b_re[i], ssm_b_im[i], ssm_c_re[i],
            ssm_c_im[i], ssm_d[i], seq_chunks, w_gate_all, w_up_all, w_down_all, i)
        h = _mix_out(h.reshape(nch, CHUNK, dm), a, yt,
                     jnp.transpose(glu_w[i]).astype(BF16), glu_b[i][:, None],
                     w_out[i].astype(BF16)).reshape(t, dm)

        eg = EXPERTS_PER_GROUP
        w_router = jnp.concatenate(
            [router_grp_w[i], jnp.zeros((dm, eg - N_EXPERT_GROUPS), F32),
             jnp.transpose(router_exp_w[i], (1, 0, 2)).reshape(dm, N_EXPERTS),
             jnp.zeros((dm, ROUTER_W - eg - N_EXPERTS), F32)], axis=1)
        b_router = jnp.concatenate(
            [router_grp_b[i], jnp.zeros((eg - N_EXPERT_GROUPS,), F32),
             router_exp_b[i].reshape(N_EXPERTS),
             jnp.zeros((ROUTER_W - eg - N_EXPERTS,), F32)])[None]
        vp, meta_t, counts = _router(h, g_ffn[i][None], w_router, b_router, tm)
        idx, tile_expert, n_valid, block = _plan(meta_t, counts, n_sorted // ROW_TILE)
        xs = _sc_scatter_rows(vp.reshape(2 * t, SC_ROW), idx, 2 * n_sorted)
        idx4 = idx.reshape(4, t)
        cuts = (0, t // FIRST_PART, t)
        parts = list(zip(cuts[:-1], cuts[1:]))
        gather_idx = [idx4[:, lo:hi].reshape(1, 4 * (hi - lo)) for lo, hi in parts]
        ple_wg = (g_ple[i][:, None] * ple_gate_w[i]).astype(BF16)
        ple_wp = ple_proj_w[i].astype(BF16)
        xs, gather_idx, ple_wg, ple_wp = lax.optimization_barrier((xs, gather_idx, ple_wg, ple_wp))
        ys = _experts(xs.reshape(2, n_sorted, SC_ROW), tile_expert, n_valid, block,
                      w_gate_b, w_up_b, w_down_b)
        ys2 = ys.reshape(2 * n_sorted, SC_ROW)
        out = None
        for (lo, hi), idx_q in zip(parts, gather_idx):
            yg_q = _sc_gather_rows(ys2, idx_q)
            out = _ple(h, yg_q.reshape(4, hi - lo, SC_ROW), meta_t, p[i].reshape(t, D_PLE), ple_wg,
                       ple_gate_b[i][None], ple_wp, g_final[None], i == depth - 1, tm, lo // tm, out)
        h = out
    return h.reshape(bsz, seq, dm)
```

```python
import functools
import math

import numpy as np
import jax
import jax.numpy as jnp
from jax import lax
from jax.experimental import pallas as pl
from jax.experimental.pallas import tpu as pltpu
from jax.experimental.pallas import tpu_sc as plsc

F32 = jnp.float32
BF16 = jnp.bfloat16
U32 = jnp.uint32

D_MODEL = 1024
D_POOL = 512
D_SSM = 512
POOL_WINDOWS = (2, 4, 8, 16)
POOL_GROUP = 128
SSM_GROUP = 16
N_SSM_GROUPS = 32
SSM_STATE = 64
N_EXPERT_GROUPS = 4
EXPERTS_PER_GROUP = 8
N_EXPERTS = N_EXPERT_GROUPS * EXPERTS_PER_GROUP
D_EXPERT = 256
D_PLE = 256
RMS_EPS = 1e-6

LANES = 128
CHUNK = 32
CHUNK_W = CHUNK * SSM_GROUP
T_SUB = 8
S_TILE = 8
C_TILE = 128
ROUTER_W = LANES
HALF = D_MODEL // 2
ROW_TILE = 1280
SC_WINDOW = 128
SC_ROW = HALF // 2
META_ROWS = 8
TAIL_EIGHTHS = (1, 3, 4)
VMEM_LIMIT = 56 * 1024 * 1024


def _dot(a, b):
    return jnp.dot(a, b, preferred_element_type=F32)


def _dot_nt(a, b):
    return lax.dot_general(a, b, (((1,), (1,)), ((), ())), preferred_element_type=F32)


def _dot_tn(a, b):
    return lax.dot_general(a, b, (((0,), (0,)), ((), ())), preferred_element_type=F32)


def _rms(x, g=None):
    y = x * lax.rsqrt(jnp.mean(x * x, axis=-1, keepdims=True) + RMS_EPS)
    return y if g is None else y * g


def _sigmoid(x):
    return 1.0 / (1.0 + jnp.exp(-x))


def _params(sem):
    return pltpu.CompilerParams(dimension_semantics=sem, vmem_limit_bytes=VMEM_LIMIT)


def _tile_step():
    n_s = pl.num_programs(1)
    return pl.program_id(0) * n_s + pl.program_id(1), pl.num_programs(0) * n_s


def _row_copies(hbm, buf, sem, step_idx, slot_idx, to_hbm):
    n_s = pl.num_programs(1)
    c0 = (step_idx // n_s) * C_TILE
    s0 = (step_idx % n_s) * S_TILE
    out = []
    for j in range(S_TILE):
        far, near = hbm.at[pl.ds(c0, C_TILE), s0 + j, :], buf.at[slot_idx, j]
        src, dst = (near, far) if to_hbm else (far, near)
        out.append(pltpu.make_async_copy(src, dst, sem.at[slot_idx]))
    return out


def _load_time_major_rows(hbm, buf, sem):
    step, n_steps = _tile_step()
    slot = step % 2

    @pl.when(step == 0)
    def _():
        for cp in _row_copies(hbm, buf, sem, step, slot, False):
            cp.start()

    @pl.when(step + 1 < n_steps)
    def _():
        for cp in _row_copies(hbm, buf, sem, step + 1, 1 - slot, False):
            cp.start()

    for cp in _row_copies(hbm, buf, sem, step, slot, False):
        cp.wait()
    return buf[slot].reshape(S_TILE * C_TILE, buf.shape[-1])


def _store_time_major_rows(val, hbm, buf, sem):
    step, n_steps = _tile_step()
    slot = step % 2

    @pl.when(step >= 2)
    def _():
        for cp in _row_copies(hbm, buf, sem, step - 2, slot, True):
            cp.wait()

    buf[slot] = val.reshape(S_TILE, C_TILE, val.shape[-1])
    for cp in _row_copies(hbm, buf, sem, step, slot, True):
        cp.start()

    @pl.when(step == n_steps - 1)
    def _():
        @pl.when(step >= 1)
        def _():
            for cp in _row_copies(hbm, buf, sem, step - 1, 1 - slot, True):
                cp.wait()
        for cp in _row_copies(hbm, buf, sem, step, slot, True):
            cp.wait()


def _in_proj_kernel(x_hbm, wp_ref, wst_ref, zp_ref, ut_ref, xbuf, xsem):
    nc = C_TILE
    u = _rms(_load_time_major_rows(x_hbm, xbuf, xsem)).astype(BF16)
    zp_ref[...] = _dot(u, wp_ref[...]).reshape(zp_ref.shape).astype(zp_ref.dtype)
    zt = _dot_nt(wst_ref[...], u).astype(BF16)
    for j in range(S_TILE):
        ut_ref[:, j, :, :] = zt[:, j * nc:(j + 1) * nc].reshape(N_SSM_GROUPS, SSM_GROUP, nc)


def _in_proj(x3, w_pool, w_ssm_t):
    nch = x3.shape[0]
    return pl.pallas_call(
        _in_proj_kernel,
        grid=(nch // C_TILE, CHUNK // S_TILE),
        in_specs=[
            pl.BlockSpec(memory_space=pl.ANY),
            pl.BlockSpec((D_MODEL, D_POOL), lambda c, s: (0, 0)),
            pl.BlockSpec((D_SSM, D_MODEL), lambda c, s: (0, 0)),
        ],
        out_specs=[
            pl.BlockSpec((S_TILE, C_TILE, D_POOL), lambda c, s: (s, c, 0)),
            pl.BlockSpec((N_SSM_GROUPS, None, S_TILE, SSM_GROUP, C_TILE), lambda c, s: (0, c, s, 0, 0)),
        ],
        out_shape=[
            jax.ShapeDtypeStruct((CHUNK, nch, D_POOL), BF16),
            jax.ShapeDtypeStruct((N_SSM_GROUPS, nch // C_TILE, CHUNK, SSM_GROUP, C_TILE), BF16),
        ],
        scratch_shapes=[
            pltpu.VMEM((2, S_TILE, C_TILE, D_MODEL), F32),
            pltpu.SemaphoreType.DMA((2,)),
        ],
        compiler_params=_params(("arbitrary", "arbitrary")),
        name="in_proj",
    )(x3, w_pool, w_ssm_t)


def _pool_kernel(z_ref, w_ref, sc_ref, o_ref):
    n_c = z_ref.shape[1]
    gi = pl.program_id(1)
    chunk = lax.broadcasted_iota(jnp.int32, (n_c, 1), 0)

    def plane(p):
        if p < 0:
            return jnp.where(chunk >= 1, pltpu.roll(z_ref[p + CHUNK].astype(F32), 1, 0), 0.0)
        if p >= CHUNK:
            return jnp.where(chunk < n_c - 1, pltpu.roll(z_ref[p - CHUNK].astype(F32), n_c - 1, 0), 0.0)
        return z_ref[p].astype(F32)

    def inv_count(s_t, half):
        n = n_c * CHUNK
        first = min(s_t + half, n) - max(s_t - half, 0)
        t_last = n - CHUNK + s_t
        last = min(t_last + half, n) - max(t_last - half, 0)
        inv = 1.0 / (2 * half)
        if first != 2 * half:
            inv = jnp.where(chunk == 0, 1.0 / first, inv)
        if last != 2 * half:
            inv = jnp.where(chunk == n_c - 1, 1.0 / last, inv)
        return inv

    for k, w in enumerate(POOL_WINDOWS):
        @pl.when(gi == k)
        def _(w=w):
            half = w // 2
            total = plane(-half)
            for p in range(-half + 1, half):
                total = total + plane(p)
            diffs = []
            for s_t in range(CHUNK):
                diffs.append((total * inv_count(s_t, half) - plane(s_t)).astype(BF16))
                if s_t + 1 < CHUNK:
                    total = total + plane(s_t + half) - plane(s_t - half)
            diff = jnp.concatenate(diffs, axis=0)
            out = _dot(diff, w_ref[...].astype(BF16)) * sc_ref[...]
            o_ref[...] = out.reshape(o_ref.shape).astype(o_ref.dtype)


def _pool(zp_t, pool_w, pool_scale, bsz):
    _, nch, _ = zp_t.shape
    n_c = nch // bsz
    return pl.pallas_call(
        _pool_kernel,
        grid=(bsz, len(POOL_WINDOWS)),
        in_specs=[
            pl.BlockSpec((CHUNK, n_c, POOL_GROUP), lambda i, g: (0, i, g)),
            pl.BlockSpec((None, POOL_GROUP, POOL_GROUP), lambda i, g: (g, 0, 0)),
            pl.BlockSpec((1, POOL_GROUP), lambda i, g: (0, g)),
        ],
        out_specs=pl.BlockSpec((CHUNK, n_c, POOL_GROUP), lambda i, g: (0, i, g)),
        out_shape=jax.ShapeDtypeStruct(zp_t.shape, BF16),
        compiler_params=_params(("parallel", "parallel")),
        name="pool",
    )(zp_t, pool_w, pool_scale)


def _expand_consts():
    time = np.arange(CHUNK_W) // SSM_GROUP
    def onehot(e):
        m = np.zeros((CHUNK_W, LANES), np.float32)
        m[np.arange(CHUNK_W), e] = 1.0
        return m
    return np.stack([
        onehot(CHUNK - 1 - time),
        onehot(time),
        onehot(time + 1),
        onehot(CHUNK - time),
    ])


def _ssm_kernel(u_ref, vec_ref, mat_ref, exp_ref, wg_ref, wu_ref, wd_ref, y_ref, og_ref, ou_ref,
                od_ref, *, seq_chunks):
    og_ref[...] = wg_ref[...].astype(BF16)
    ou_ref[...] = wu_ref[...].astype(BF16)
    od_ref[...] = wd_ref[...].astype(BF16)

    n_ct = u_ref.shape[0]
    nch = n_ct * C_TILE
    half = LANES // 2
    lane = lax.broadcasted_iota(jnp.int32, (1, LANES), 1)
    lo_half = lane < half

    def direction(di):
        a_re = vec_ref[di, 0:1]
        a_im = vec_ref[di, 1:2]
        dt = jnp.exp(vec_ref[di, 2:3])
        mag = jnp.exp(a_re * dt)
        ang = a_im * dt
        lb_re = mag * jnp.cos(ang)
        lb_im = mag * jnp.sin(ang)
        den = a_re * a_re + a_im * a_im
        f_re = ((lb_re - 1.0) * a_re + lb_im * a_im) / den
        f_im = (lb_im * a_re - (lb_re - 1.0) * a_im) / den
        return (lb_re, lb_im), f_re, f_im

    def power_table(lam):
        e = lax.broadcasted_iota(jnp.int32, (LANES, 1), 0)
        sq_re, sq_im = lam
        t_re = jnp.ones((LANES, LANES), F32)
        t_im = jnp.zeros((LANES, LANES), F32)
        for k in range(CHUNK.bit_length()):
            bit = ((e >> k) & 1) == 1
            t_re, t_im = (jnp.where(bit, t_re * sq_re - t_im * sq_im, t_re),
                          jnp.where(bit, t_re * sq_im + t_im * sq_re, t_im))
            sq_re, sq_im = sq_re * sq_re - sq_im * sq_im, 2.0 * sq_re * sq_im
        return jnp.where(lo_half, t_re, t_im), jnp.where(lo_half, t_im, t_re)

    def tile_rows(x16):
        return jnp.broadcast_to(x16[None], (CHUNK, SSM_GROUP, LANES)).reshape(CHUNK_W, LANES)

    def expanded(tab, which, v_re, v_im, conj_sign):
        lexp = _dot(exp_ref[which], jnp.concatenate(tab, axis=1).astype(BF16))
        if conj_sign > 0:
            p = jnp.where(lo_half, v_re, v_re)
            q = jnp.where(lo_half, -v_im, v_im)
        else:
            p = jnp.where(lo_half, v_re, -v_re)
            q = jnp.where(lo_half, -v_im, -v_im)
        return lexp[:, :LANES] * tile_rows(p) + lexp[:, LANES:] * tile_rows(q)

    lam_f, ff_re, ff_im = direction(0)
    lam_b, fb_re, fb_im = direction(1)
    tabs_f = power_table(lam_f)
    tabs_b = power_table(lam_b)
    tab_f, tab_b = tabs_f[0], tabs_b[0]

    def bbar(bt_re, bt_im, f_re, f_im):
        return bt_re * f_re - bt_im * f_im, bt_re * f_im + bt_im * f_re

    def mat(di, k):
        return mat_ref[di, k * SSM_GROUP:(k + 1) * SSM_GROUP, :]

    bf_re, bf_im = bbar(mat(0, 0), mat(0, 1), ff_re, ff_im)
    bb_re, bb_im = bbar(mat(1, 0), mat(1, 1), fb_re, fb_im)

    pb1 = expanded(tabs_f, 0, bf_re, bf_im, 1)
    pb2 = expanded(tabs_b, 1, bb_re, bb_im, 1)
    pb3 = expanded(tabs_b, 2, bb_re, bb_im, 1)
    ft_f = expanded(tabs_f, 2, mat(0, 3), mat(0, 4), -1)
    ft_b = expanded(tabs_b, 3, mat(1, 3), mat(1, 4), -1)

    row_w = lax.broadcasted_iota(jnp.int32, (CHUNK_W, 1), 0)
    last_blk = row_w >= CHUNK_W - SSM_GROUP
    pb2_lag0 = jnp.where(last_blk, pltpu.roll(pb2, CHUNK_W - SSM_GROUP, 0), 0.0)
    ccr_f = mat(0, 2).astype(BF16)
    ccr_b = mat(1, 2).astype(BF16)
    r_lo = _dot_nt(ccr_f, pb1.astype(BF16)) + _dot_nt(ccr_b, pb2_lag0.astype(BF16))
    co = lax.broadcasted_iota(jnp.int32, (SSM_GROUP, CHUNK_W), 0)
    col = lax.broadcasted_iota(jnp.int32, (SSM_GROUP, CHUNK_W), 1)
    r_lo = r_lo + jnp.where(col == CHUNK_W - SSM_GROUP + co, mat(0, 5)[:, 0:1], 0.0)
    r_hi = _dot_nt(ccr_b, pb3.astype(BF16))
    r_t = jnp.concatenate([r_lo, r_hi], axis=1)
    g_t = jnp.concatenate(
        [pltpu.roll(r_t, SSM_GROUP * (tl + 1), 1) for tl in range(T_SUB)], axis=0
    ).astype(BF16)

    u = jnp.concatenate([u_ref[ct].reshape(CHUNK_W, C_TILE) for ct in range(n_ct)],
                        axis=1)
    e_mat = jnp.concatenate([pb1, pb2], axis=1).astype(BF16)
    xend = _dot_tn(e_mat, u)
    lanec = lax.broadcasted_iota(jnp.int32, (1, nch), 1) % seq_chunks
    ns = SSM_STATE

    def scan(re, im, tab, forward):
        lam_col = jnp.transpose(tab[CHUNK:CHUNK + 8, :])[:, 0:1]
        a, b = lam_col[:ns], lam_col[ns:]
        n_steps = int(math.log2(seq_chunks))

        def rolled(v, d):
            if d % LANES == 0:
                return jnp.concatenate([v[:, nch - d:], v[:, :nch - d]], axis=1)
            return pltpu.roll(v, d, 1)

        def shifted(v, d):
            if forward:
                return jnp.where(lanec >= d, rolled(v, d), 0.0)
            return jnp.where(lanec < seq_chunks - d, rolled(v, nch - d), 0.0)

        for k in range(n_steps):
            sr, si = shifted(re, 1 << k), shifted(im, 1 << k)
            re, im = re + (sr * a - si * b), im + (sr * b + si * a)
            a, b = a * a - b * b, 2.0 * a * b
        return shifted(re, 1), shifted(im, 1)

    f_re, f_im = scan(xend[:ns], xend[ns:2 * ns], tab_f, True)
    b_re, b_im = scan(xend[2 * ns:3 * ns], xend[3 * ns:], tab_b, False)
    xin = jnp.concatenate([f_re, f_im, b_re, b_im], axis=0).astype(BF16)
    f_t = jnp.concatenate([ft_f, ft_b], axis=1).astype(BF16)

    toeplitz = jnp.concatenate(
        [g_t[:, CHUNK_W - LANES * th:2 * CHUNK_W - LANES * th] for th in range(CHUNK // T_SUB)],
        axis=0)
    y_t = _dot(toeplitz, u) + _dot(f_t, xin)
    for ct in range(n_ct):
        y_ref[ct] = y_t[:, ct * C_TILE:(ct + 1) * C_TILE].reshape(
            CHUNK, SSM_GROUP, C_TILE).astype(y_ref.dtype)


def _ssm(ut, a_re, a_im, log_dt, b_re, b_im, c_re, c_im, d, seq_chunks, w_gate, w_up, w_down, layer):
    g, n_ct = ut.shape[:2]
    assert N_EXPERTS % g == 0
    epg = N_EXPERTS // g
    base = layer * g
    n = SSM_STATE

    def per_group(a):
        return jnp.swapaxes(a, 0, 1)

    def dup(a):
        return jnp.concatenate([a, a], axis=-1)

    vecs = dup(jnp.stack([per_group(a_re), per_group(a_im),
                          jnp.broadcast_to(per_group(log_dt)[..., None], (g, 2, n))], axis=2))
    cr, ci = per_group(c_re), per_group(c_im)
    d_blk = jnp.broadcast_to(d.reshape(g, 1, SSM_GROUP, 1), (g, 2, SSM_GROUP, LANES))
    mats = jnp.concatenate(
        [dup(jnp.swapaxes(per_group(b_re), 2, 3)), dup(jnp.swapaxes(per_group(b_im), 2, 3)),
         jnp.concatenate([cr, -ci], axis=-1), dup(cr), dup(ci), d_blk], axis=2)
    exp_c = jnp.asarray(_expand_consts(), BF16)

    return pl.pallas_call(
        functools.partial(_ssm_kernel, seq_chunks=seq_chunks),
        grid=(g,),
        in_specs=[
            pl.BlockSpec((None, n_ct, CHUNK, SSM_GROUP, C_TILE), lambda i: (i, 0, 0, 0, 0)),
            pl.BlockSpec((None, 2, 3, LANES), lambda i: (i, 0, 0, 0)),
            pl.BlockSpec((None, 2, 6 * SSM_GROUP, LANES), lambda i: (i, 0, 0, 0)),
            pl.BlockSpec((4, CHUNK_W, LANES), lambda i: (0, 0, 0)),
            pl.BlockSpec((epg, D_MODEL, D_EXPERT), lambda i: (base + i, 0, 0)),
            pl.BlockSpec((epg, D_MODEL, D_EXPERT), lambda i: (base + i, 0, 0)),
            pl.BlockSpec((epg, D_EXPERT, D_MODEL), lambda i: (base + i, 0, 0)),
        ],
        out_specs=[
            pl.BlockSpec((None, n_ct, CHUNK, SSM_GROUP, C_TILE), lambda i: (i, 0, 0, 0, 0)),
            pl.BlockSpec((epg, D_MODEL, D_EXPERT), lambda i: (i, 0, 0)),
            pl.BlockSpec((epg, D_MODEL, D_EXPERT), lambda i: (i, 0, 0)),
            pl.BlockSpec((epg, D_EXPERT, D_MODEL), lambda i: (i, 0, 0)),
        ],
        out_shape=[
            jax.ShapeDtypeStruct(ut.shape, BF16),
            jax.ShapeDtypeStruct((N_EXPERTS, D_MODEL, D_EXPERT), BF16),
            jax.ShapeDtypeStruct((N_EXPERTS, D_MODEL, D_EXPERT), BF16),
            jax.ShapeDtypeStruct((N_EXPERTS, D_EXPERT, D_MODEL), BF16),
        ],
        compiler_params=_params(("parallel",)),
        name="ssm",
    )(ut, vecs, mats, exp_c, w_gate, w_up, w_down)


def _mix_out_kernel(x_hbm, a_ref, yt_ref, gwt_ref, gb_ref, wo_ref, h_hbm, xbuf, hbuf, xsem, hsem):
    nc = C_TILE
    x = _load_time_major_rows(x_hbm, xbuf, xsem)
    a = a_ref[...].reshape(S_TILE * nc, D_POOL)
    y = jnp.concatenate([yt_ref[:, j, :, :].reshape(D_SSM, nc) for j in range(S_TILE)],
                        axis=1).astype(F32)
    z = 0.5 * y * (1.0 + jnp.tanh(math.sqrt(2.0 / math.pi) * (y + 0.044715 * (y * y * y))))
    gate = _sigmoid(_dot(gwt_ref[...], z.astype(BF16)) + gb_ref[...])
    s = (z * gate).astype(BF16)
    h = x + _dot(a, wo_ref[:D_POOL, :]) + _dot_tn(s, wo_ref[D_POOL:, :])
    _store_time_major_rows(h, h_hbm, hbuf, hsem)


def _mix_out(x3, a3, yt, glu_w_t, glu_b_col, w_out):
    nch = x3.shape[0]
    return pl.pallas_call(
        _mix_out_kernel,
        grid=(nch // C_TILE, CHUNK // S_TILE),
        in_specs=[
            pl.BlockSpec(memory_space=pl.ANY),
            pl.BlockSpec((S_TILE, C_TILE, D_POOL), lambda c, t: (t, c, 0)),
            pl.BlockSpec((N_SSM_GROUPS, None, S_TILE, SSM_GROUP, C_TILE), lambda c, t: (0, c, t, 0, 0)),
            pl.BlockSpec((D_SSM, D_SSM), lambda c, t: (0, 0)),
            pl.BlockSpec((D_SSM, 1), lambda c, t: (0, 0)),
            pl.BlockSpec((D_MODEL, D_MODEL), lambda c, t: (0, 0)),
        ],
        out_specs=pl.BlockSpec(memory_space=pl.ANY),
        out_shape=jax.ShapeDtypeStruct((nch, CHUNK, D_MODEL), F32),
        scratch_shapes=[
            pltpu.VMEM((2, S_TILE, C_TILE, D_MODEL), F32),
            pltpu.VMEM((2, S_TILE, C_TILE, D_MODEL), F32),
            pltpu.SemaphoreType.DMA((2,)),
            pltpu.SemaphoreType.DMA((2,)),
        ],
        compiler_params=_params(("arbitrary", "arbitrary")),
        name="mix_out",
    )(x3, a3, yt, glu_w_t, glu_b_col, w_out)


def _pack_rows(x):
    b = lax.bitcast_convert_type(x.astype(BF16).astype(F32), U32)
    return (b[:, :HALF] & jnp.uint32(0xFFFF0000)) | (b[:, HALF:] >> 16)


def _unpack_rows(w):
    lo = lax.bitcast_convert_type(w & jnp.uint32(0xFFFF0000), F32)
    hi = lax.bitcast_convert_type(w << 16, F32)
    return lo, hi


def _split_bf16(x):
    hi = x.astype(BF16)
    return hi, (x - hi.astype(F32)).astype(BF16)


def _route(v32, wr_ref, br_ref):
    v_hi, v_lo = _split_bf16(v32)
    w_hi, w_lo = _split_bf16(wr_ref[...])
    both = _dot(v_hi, jnp.concatenate([w_hi, w_lo], axis=1))
    logits = both[:, :ROUTER_W] + (both[:, ROUTER_W:] + _dot(v_lo, w_hi)) + br_ref[...]
    return jnp.transpose(logits)


def _top1(x, valid=None):
    n = x.shape[0]
    row = lax.broadcasted_iota(jnp.int32, x.shape, 0).astype(F32)
    if valid is not None:
        x = jnp.where(valid, x, -jnp.inf)
    m = jnp.max(x, axis=0, keepdims=True)
    idx = jnp.min(jnp.where(x == m, row, float(n)), axis=0, keepdims=True)
    return m, idx, x, row


def _split_planes(packed, ref):
    ref[0] = packed[:, :SC_ROW]
    ref[1] = packed[:, SC_ROW:]


def _router_kernel(h_ref, g_ref, wr_ref, br_ref, before_ref, vp_ref, meta_t_ref, cnt_ref, carry_ref):
    @pl.when(pl.program_id(0) == 0)
    def _():
        carry_ref[...] = jnp.zeros_like(carry_ref)

    v32 = _rms(h_ref[...], g_ref[...])
    _split_planes(_pack_rows(v32), vp_ref)
    lt = _route(v32, wr_ref, br_ref)
    tm = lt.shape[1]
    eg = EXPERTS_PER_GROUP

    grp = lt[:eg]
    grp_row = lax.broadcasted_iota(jnp.int32, grp.shape, 0)
    mg, grp_idx, grp, _ = _top1(grp, grp_row < N_EXPERT_GROUPS)
    grp_p = 1.0 / jnp.sum(jnp.exp(grp - mg), axis=0, keepdims=True)
    le = jnp.zeros((eg, tm), F32)
    for g in range(N_EXPERT_GROUPS):
        le = jnp.where(grp_idx == float(g), lt[eg * (g + 1):eg * (g + 2)], le)
    m1, i1, le, row = _top1(le)
    z = jnp.sum(jnp.exp(le - m1), axis=0, keepdims=True)
    m2, i2, _, _ = _top1(jnp.where(row == i1, -jnp.inf, le))
    p1 = 1.0 / z
    p2 = jnp.exp(m2 - m1) / z
    tot = p1 + p2
    w1 = grp_p * (p1 / tot)
    w2 = grp_p * (p2 / tot)
    e1 = grp_idx * eg + i1
    e2 = grp_idx * eg + i2

    erow = lax.broadcasted_iota(jnp.int32, (N_EXPERTS, tm), 0).astype(F32)
    onehot = jnp.where(erow == e1, 1.0, jnp.where(erow == e2, 1.0, 0.0))
    before = _dot(onehot.astype(BF16), before_ref[...]) + carry_ref[...]
    rank1 = jnp.sum(jnp.where(erow == e1, before, 0.0), axis=0, keepdims=True)
    rank2 = jnp.sum(jnp.where(erow == e2, before, 0.0), axis=0, keepdims=True)
    carry = carry_ref[...] + jnp.sum(onehot, axis=1, keepdims=True)
    carry_ref[...] = carry
    cnt_ref[...] = carry

    mrow = lax.broadcasted_iota(jnp.int32, (META_ROWS, tm), 0)
    meta_t_ref[...] = jnp.where(mrow == 0, e1, jnp.where(mrow == 1, e2, jnp.where(
        mrow == 2, rank1, jnp.where(mrow == 3, rank2, jnp.where(
            mrow == 4, w1, jnp.where(mrow == 5, w2, 0.0))))))


def _earlier_matrix(tm):
    return np.triu(np.ones((tm, tm), np.float32), k=1)


def _router(h1, g_ffn, w_router, b_router, tm):
    t = h1.shape[0]
    return pl.pallas_call(
        _router_kernel,
        grid=(t // tm,),
        in_specs=[
            pl.BlockSpec((tm, D_MODEL), lambda i: (i, 0)),
            pl.BlockSpec((1, D_MODEL), lambda i: (0, 0)),
            pl.BlockSpec((D_MODEL, ROUTER_W), lambda i: (0, 0)),
            pl.BlockSpec((1, ROUTER_W), lambda i: (0, 0)),
            pl.BlockSpec((tm, tm), lambda i: (0, 0)),
        ],
        out_specs=[
            pl.BlockSpec((2, tm, SC_ROW), lambda i: (0, i, 0)),
            pl.BlockSpec((META_ROWS, tm), lambda i: (0, i)),
            pl.BlockSpec((N_EXPERTS, 1), lambda i: (0, 0)),
        ],
        out_shape=[
            jax.ShapeDtypeStruct((2, t, SC_ROW), U32),
            jax.ShapeDtypeStruct((META_ROWS, t), F32),
            jax.ShapeDtypeStruct((N_EXPERTS, 1), F32),
        ],
        scratch_shapes=[pltpu.VMEM((N_EXPERTS, 1), F32)],
        compiler_params=_params(("arbitrary",)),
        name="router",
    )(h1, g_ffn, w_router, b_router, jnp.asarray(_earlier_matrix(tm), BF16))


def _plan(meta_t, counts, n_tiles):
    e1 = meta_t[0].astype(jnp.int32)
    e2 = meta_t[1].astype(jnp.int32)
    rank1 = meta_t[2].astype(jnp.int32)
    rank2 = meta_t[3].astype(jnp.int32)
    cnt = counts[:, 0].astype(jnp.int32)
    padded = ((cnt + ROW_TILE - 1) // ROW_TILE) * ROW_TILE
    ends = jnp.cumsum(padded)
    starts = ends - padded
    experts = jnp.arange(N_EXPERTS, dtype=jnp.int32)
    pos1 = rank1 + jnp.sum(jnp.where(e1[None, :] == experts[:, None], starts[:, None], 0), axis=0)
    pos2 = rank2 + jnp.sum(jnp.where(e2[None, :] == experts[:, None], starts[:, None], 0), axis=0)
    tile_start = jnp.arange(n_tiles, dtype=jnp.int32) * ROW_TILE
    tile_expert = jnp.sum((tile_start[:, None] >= ends[None, :]).astype(jnp.int32), axis=1)
    tile_expert = jnp.minimum(tile_expert, N_EXPERTS - 1)
    rows_left = jnp.sum(jnp.where(tile_expert[:, None] == experts, cnt + starts, 0), axis=1) - tile_start
    n_valid = jnp.clip(rows_left, 0, ROW_TILE).astype(jnp.int32)
    last_used = jnp.maximum(ends[-1] // ROW_TILE - 1, 0)
    block = jnp.minimum(jnp.arange(n_tiles, dtype=jnp.int32), last_used)
    tile_expert = jnp.sum(jnp.where(block[:, None] == jnp.arange(n_tiles)[None, :],
                                    tile_expert[None, :], 0), axis=1)
    plane = n_tiles * ROW_TILE
    half_rows = jnp.concatenate([pos1, pos1 + plane, pos2, pos2 + plane])[None]
    return half_rows, tile_expert, n_valid, block


def _sc_mesh():
    return plsc.VectorSubcoreMesh(core_axis_name="c", subcore_axis_name="s")


def _sc_scatter_rows(rows, idx, n_out):
    t, width = rows.shape
    steps = t // SC_WINDOW
    half = steps // 2

    @pl.kernel(out_type=jax.ShapeDtypeStruct((n_out, width), rows.dtype), mesh=_sc_mesh(),
               scratch_types=[], name="moe_scatter")
    def scatter(rows_hbm, idx_hbm, out_hbm):
        def body(rows_vmem, idx0_vmem, idx1_vmem):
            pltpu.sync_copy(rows_vmem, out_hbm.at[idx0_vmem.at[0]])
            pltpu.sync_copy(rows_vmem, out_hbm.at[idx1_vmem.at[0]])

        pltpu.emit_pipeline(
            body,
            grid=(2, half),
            in_specs=[pl.BlockSpec((SC_WINDOW, width), lambda c, j: (c * half + j, 0)),
                      pl.BlockSpec((1, SC_WINDOW), lambda c, j: (0, c * half + j)),
                      pl.BlockSpec((1, SC_WINDOW), lambda c, j: (0, steps + c * half + j))],
            out_specs=[],
            core_axis_name=("c", "s"),
            dimension_semantics=(pltpu.PARALLEL, pltpu.PARALLEL),
        )(rows_hbm, idx_hbm, idx_hbm)

    return scatter(rows, idx)


def _sc_gather_rows(table, idx):
    m = idx.shape[1]
    width = table.shape[1]
    steps = m // (2 * SC_WINDOW)

    @pl.kernel(out_type=jax.ShapeDtypeStruct((m, width), table.dtype), mesh=_sc_mesh(),
               scratch_types=[], name="moe_gather")
    def gather(table_hbm, idx_hbm, out_hbm):
        def body(idx_vmem, out_vmem):
            pltpu.sync_copy(table_hbm.at[idx_vmem.at[0]], out_vmem)

        pltpu.emit_pipeline(
            body,
            grid=(2, steps),
            in_specs=[pl.BlockSpec((1, SC_WINDOW), lambda k, j: (0, k * steps + j))],
            out_specs=[pl.BlockSpec((SC_WINDOW, width), lambda k, j: (k * steps + j, 0))],
            core_axis_name=("c", "s"),
            dimension_semantics=(pltpu.PARALLEL, pltpu.PARALLEL),
        )(idx_hbm, out_hbm)

    return gather(table, idx)


def _experts_kernel(te_ref, nv_ref, blk_ref, xs_ref, wg_ref, wu_ref, wd_ref, ys_ref):
    r = pl.program_id(0)
    n_valid = nv_ref[r]

    @pl.when(n_valid > 0)
    def _():
        parts = [p.astype(BF16) for p in _unpack_rows(xs_ref[0]) + _unpack_rows(xs_ref[1])]
        cols = (0, 2 * SC_ROW, SC_ROW, 3 * SC_ROW)
        hg = sum(_dot(p, wg_ref[c:c + SC_ROW, :]) for p, c in zip(parts, cols))
        hu = sum(_dot(p, wu_ref[c:c + SC_ROW, :]) for p, c in zip(parts, cols))
        row = lax.broadcasted_iota(jnp.int32, (ROW_TILE, 1), 0)
        hid = jnp.where(row < n_valid, hg * _sigmoid(hg) * hu, 0.0).astype(BF16)
        _split_planes(_pack_rows(_dot(hid, wd_ref[...])), ys_ref)


def _experts(xs, tile_expert, n_valid, block, w_gate, w_up, w_down):
    n_tiles = xs.shape[1] // ROW_TILE
    w_spec = pl.BlockSpec((None, D_MODEL, D_EXPERT), lambda r, te, nv, blk: (te[r], 0, 0))
    grid_spec = pltpu.PrefetchScalarGridSpec(
        num_scalar_prefetch=3,
        grid=(n_tiles,),
        in_specs=[
            pl.BlockSpec((2, ROW_TILE, SC_ROW), lambda r, te, nv, blk: (0, blk[r], 0)),
            w_spec, w_spec,
            pl.BlockSpec((None, D_EXPERT, D_MODEL), lambda r, te, nv, blk: (te[r], 0, 0)),
        ],
        out_specs=pl.BlockSpec((2, ROW_TILE, SC_ROW), lambda r, te, nv, blk: (0, blk[r], 0)),
    )
    return pl.pallas_call(
        _experts_kernel,
        grid_spec=grid_spec,
        out_shape=jax.ShapeDtypeStruct((2, n_tiles * ROW_TILE, SC_ROW), U32),
        compiler_params=_params(("arbitrary",)),
        name="experts",
    )(tile_expert, n_valid, block, xs, w_gate, w_up, w_down)


def _ple_kernel(h_ref, yg_ref, meta_t_ref, p_ref, wg_ref, bg_ref, wp_ref, gf_ref, *rest, final_norm):
    o_ref = rest[-1]
    meta = jnp.transpose(meta_t_ref[...])
    w1 = meta[:, 4:5]
    w2 = meta[:, 5:6]
    q0, q2 = (w1 * u + w2 * v for u, v in zip(_unpack_rows(yg_ref[0]), _unpack_rows(yg_ref[2])))
    q1, q3 = (w1 * u + w2 * v for u, v in zip(_unpack_rows(yg_ref[1]), _unpack_rows(yg_ref[3])))
    moe = jnp.concatenate([q0, q1, q2, q3], axis=1)
    h = h_ref[...] + moe
    gate = _sigmoid(_dot(_rms(h).astype(BF16), wg_ref[...]) + bg_ref[...])
    h = h + gate * _dot(p_ref[...].astype(BF16), wp_ref[...])
    o_ref[...] = _rms(h, gf_ref[...]) if final_norm else h


def _ple(h1, yg, meta_t, p2, w_gate, b_gate, w_proj, g_final, final_norm, tm, off, prev_out):
    t = h1.shape[0]
    steps = yg.shape[1] // tm
    in_specs = [
        pl.BlockSpec((tm, D_MODEL), lambda i: (i + off, 0)),
        pl.BlockSpec((4, tm, SC_ROW), lambda i: (0, i, 0)),
        pl.BlockSpec((META_ROWS, tm), lambda i: (0, i + off)),
        pl.BlockSpec((tm, D_PLE), lambda i: (i + off, 0)),
        pl.BlockSpec((D_MODEL, D_MODEL), lambda i: (0, 0)),
        pl.BlockSpec((1, D_MODEL), lambda i: (0, 0)),
        pl.BlockSpec((D_PLE, D_MODEL), lambda i: (0, 0)),
        pl.BlockSpec((1, D_MODEL), lambda i: (0, 0)),
    ]
    args = [h1, yg, meta_t, p2, w_gate, b_gate, w_proj, g_final]
    aliases = {}
    if prev_out is not None:
        in_specs.append(pl.BlockSpec(memory_space=pl.ANY))
        args.append(prev_out)
        aliases = {len(args) - 1: 0}
    return pl.pallas_call(
        functools.partial(_ple_kernel, final_norm=final_norm),
        grid=(steps,),
        in_specs=in_specs,
        out_specs=pl.BlockSpec((tm, D_MODEL), lambda i: (i + off, 0)),
        out_shape=jax.ShapeDtypeStruct((t, D_MODEL), F32),
        input_output_aliases=aliases,
        compiler_params=_params(("parallel",)),
        name="ple",
    )(*args)


def kernel(x, p, g_mix, w_in, pool_w, pool_scale, ssm_a_re, ssm_a_im, ssm_log_dt, ssm_b_re,
           ssm_b_im, ssm_c_re, ssm_c_im, ssm_d, glu_w, glu_b, w_out, g_ffn, router_grp_w,
           router_grp_b, router_exp_w, router_exp_b, exp_w_gate, exp_w_up, exp_w_down, g_ple,
           ple_gate_w, ple_gate_b, ple_proj_w, g_final):
    bsz, seq, dm = x.shape
    depth = g_mix.shape[0]
    t = bsz * seq
    seq_chunks = seq // CHUNK
    nch = t // CHUNK
    tm = 1024
    n_sorted = (pl.cdiv(2 * t, ROW_TILE) + N_EXPERTS) * ROW_TILE
    w_gate_all = exp_w_gate.reshape(depth * N_EXPERTS, dm, D_EXPERT)
    w_up_all = exp_w_up.reshape(depth * N_EXPERTS, dm, D_EXPERT)
    w_down_all = exp_w_down.reshape(depth * N_EXPERTS, D_EXPERT, dm)

    h = x.reshape(t, dm)
    for i in range(depth):
        w_in_b = (g_mix[i][:, None] * w_in[i]).astype(BF16)
        zp, ut = _in_proj(h.reshape(nch, CHUNK, dm), w_in_b[:, :D_POOL],
                          jnp.transpose(w_in_b[:, D_POOL:]))
        a = _pool(zp, pool_w[i], pool_scale[i][None], bsz)
        yt, w_gate_b, w_up_b, w_down_b = _ssm(
            ut, ssm_a_re[i], ssm_a_im[i], ssm_log_dt[i], ssm_b_re[i], ssm_b_im[i], ssm_c_re[i],
            ssm_c_im[i], ssm_d[i], seq_chunks, w_gate_all, w_up_all, w_down_all, i)
        h = _mix_out(h.reshape(nch, CHUNK, dm), a, yt,
                     jnp.transpose(glu_w[i]).astype(BF16), glu_b[i][:, None],
                     w_out[i].astype(BF16)).reshape(t, dm)

        eg = EXPERTS_PER_GROUP
        w_router = jnp.concatenate(
            [router_grp_w[i], jnp.zeros((dm, eg - N_EXPERT_GROUPS), F32),
             jnp.transpose(router_exp_w[i], (1, 0, 2)).reshape(dm, N_EXPERTS),
             jnp.zeros((dm, ROUTER_W - eg - N_EXPERTS), F32)], axis=1)
        b_router = jnp.concatenate(
            [router_grp_b[i], jnp.zeros((eg - N_EXPERT_GROUPS,), F32),
             router_exp_b[i].reshape(N_EXPERTS),
             jnp.zeros((ROUTER_W - eg - N_EXPERTS,), F32)])[None]
        vp, meta_t, counts = _router(h, g_ffn[i][None], w_router, b_router, tm)
        idx, tile_expert, n_valid, block = _plan(meta_t, counts, n_sorted // ROW_TILE)
        xs = _sc_scatter_rows(vp.reshape(2 * t, SC_ROW), idx, 2 * n_sorted)
        ys = _experts(xs.reshape(2, n_sorted, SC_ROW), tile_expert, n_valid, block,
                      w_gate_b, w_up_b, w_down_b)
        ys2 = ys.reshape(2 * n_sorted, SC_ROW)
        idx4 = idx.reshape(4, t)
        ple_wg = (g_ple[i][:, None] * ple_gate_w[i]).astype(BF16)
        ple_wp = ple_proj_w[i].astype(BF16)
        out = None
        cuts = [0]
        for share in TAIL_EIGHTHS:
            cuts.append(cuts[-1] + share * t // 8)
        for lo, hi in zip(cuts[:-1], cuts[1:]):
            yg_q = _sc_gather_rows(ys2, idx4[:, lo:hi].reshape(1, 4 * (hi - lo)))
            out = _ple(h, yg_q.reshape(4, hi - lo, SC_ROW), meta_t, p[i].reshape(t, D_PLE), ple_wg,
                       ple_gate_b[i][None], ple_wp, g_final[None], i == depth - 1, tm, lo // tm, out)
        h = out
    return h.reshape(bsz, seq, dm)
```

```python
import functools
import math

import numpy as np
import jax
import jax.numpy as jnp
from jax import lax
from jax.experimental import pallas as pl
from jax.experimental.pallas import tpu as pltpu
from jax.experimental.pallas import tpu_sc as plsc

F32 = jnp.float32
BF16 = jnp.bfloat16
U32 = jnp.uint32

D_MODEL = 1024
D_POOL = 512
D_SSM = 512
POOL_WINDOWS = (2, 4, 8, 16)
POOL_GROUP = 128
SSM_GROUP = 16
N_SSM_GROUPS = 32
SSM_STATE = 64
N_EXPERT_GROUPS = 4
EXPERTS_PER_GROUP = 8
N_EXPERTS = N_EXPERT_GROUPS * EXPERTS_PER_GROUP
D_EXPERT = 256
D_PLE = 256
RMS_EPS = 1e-6

LANES = 128
CHUNK = 32
CHUNK_W = CHUNK * SSM_GROUP
T_SUB = 8
S_TILE = 8
C_TILE = 128
ROUTER_W = LANES
HALF = D_MODEL // 2
ROW_TILE = 1280
SC_WINDOW = 128
SC_ROW = HALF // 2
META_ROWS = 8
FIRST_PART = 4
VMEM_LIMIT = 56 * 1024 * 1024


def _dot(a, b):
    return jnp.dot(a, b, preferred_element_type=F32)


def _dot_nt(a, b):
    return lax.dot_general(a, b, (((1,), (1,)), ((), ())), preferred_element_type=F32)


def _dot_tn(a, b):
    return lax.dot_general(a, b, (((0,), (0,)), ((), ())), preferred_element_type=F32)


def _rms(x, g=None):
    y = x * lax.rsqrt(jnp.mean(x * x, axis=-1, keepdims=True) + RMS_EPS)
    return y if g is None else y * g


def _sigmoid(x):
    return 1.0 / (1.0 + jnp.exp(-x))


def _params(sem):
    return pltpu.CompilerParams(dimension_semantics=sem, vmem_limit_bytes=VMEM_LIMIT)


def _tile_step():
    n_s = pl.num_programs(1)
    return pl.program_id(0) * n_s + pl.program_id(1), pl.num_programs(0) * n_s


def _row_copies(hbm, buf, sem, step_idx, slot_idx, to_hbm):
    n_s = pl.num_programs(1)
    c0 = (step_idx // n_s) * C_TILE
    s0 = (step_idx % n_s) * S_TILE
    out = []
    for j in range(S_TILE):
        far, near = hbm.at[pl.ds(c0, C_TILE), s0 + j, :], buf.at[slot_idx, j]
        src, dst = (near, far) if to_hbm else (far, near)
        out.append(pltpu.make_async_copy(src, dst, sem.at[slot_idx]))
    return out


def _load_time_major_rows(hbm, buf, sem):
    step, n_steps = _tile_step()
    slot = step % 2

    @pl.when(step == 0)
    def _():
        for cp in _row_copies(hbm, buf, sem, step, slot, False):
            cp.start()

    @pl.when(step + 1 < n_steps)
    def _():
        for cp in _row_copies(hbm, buf, sem, step + 1, 1 - slot, False):
            cp.start()

    for cp in _row_copies(hbm, buf, sem, step, slot, False):
        cp.wait()
    return buf[slot].reshape(S_TILE * C_TILE, buf.shape[-1])


def _store_time_major_rows(val, hbm, buf, sem):
    step, n_steps = _tile_step()
    slot = step % 2

    @pl.when(step >= 2)
    def _():
        for cp in _row_copies(hbm, buf, sem, step - 2, slot, True):
            cp.wait()

    buf[slot] = val.reshape(S_TILE, C_TILE, val.shape[-1])
    for cp in _row_copies(hbm, buf, sem, step, slot, True):
        cp.start()

    @pl.when(step == n_steps - 1)
    def _():
        @pl.when(step >= 1)
        def _():
            for cp in _row_copies(hbm, buf, sem, step - 1, 1 - slot, True):
                cp.wait()
        for cp in _row_copies(hbm, buf, sem, step, slot, True):
            cp.wait()


def _in_proj_kernel(x_hbm, wp_ref, wst_ref, zp_ref, ut_ref, xbuf, xsem):
    nc = C_TILE
    u = _rms(_load_time_major_rows(x_hbm, xbuf, xsem)).astype(BF16)
    zp_ref[...] = _dot(u, wp_ref[...]).reshape(zp_ref.shape).astype(zp_ref.dtype)
    zt = _dot_nt(wst_ref[...], u).astype(BF16)
    for j in range(S_TILE):
        ut_ref[:, j, :, :] = zt[:, j * nc:(j + 1) * nc].reshape(N_SSM_GROUPS, SSM_GROUP, nc)


def _in_proj(x3, w_pool, w_ssm_t):
    nch = x3.shape[0]
    return pl.pallas_call(
        _in_proj_kernel,
        grid=(nch // C_TILE, CHUNK // S_TILE),
        in_specs=[
            pl.BlockSpec(memory_space=pl.ANY),
            pl.BlockSpec((D_MODEL, D_POOL), lambda c, s: (0, 0)),
            pl.BlockSpec((D_SSM, D_MODEL), lambda c, s: (0, 0)),
        ],
        out_specs=[
            pl.BlockSpec((S_TILE, C_TILE, D_POOL), lambda c, s: (s, c, 0)),
            pl.BlockSpec((N_SSM_GROUPS, None, S_TILE, SSM_GROUP, C_TILE), lambda c, s: (0, c, s, 0, 0)),
        ],
        out_shape=[
            jax.ShapeDtypeStruct((CHUNK, nch, D_POOL), BF16),
            jax.ShapeDtypeStruct((N_SSM_GROUPS, nch // C_TILE, CHUNK, SSM_GROUP, C_TILE), BF16),
        ],
        scratch_shapes=[
            pltpu.VMEM((2, S_TILE, C_TILE, D_MODEL), F32),
            pltpu.SemaphoreType.DMA((2,)),
        ],
        compiler_params=_params(("arbitrary", "arbitrary")),
        name="in_proj",
    )(x3, w_pool, w_ssm_t)


def _pool_kernel(z_ref, w_ref, sc_ref, o_ref):
    n_c = z_ref.shape[1]
    gi = pl.program_id(1)
    chunk = lax.broadcasted_iota(jnp.int32, (n_c, 1), 0)

    def plane(p):
        if p < 0:
            return jnp.where(chunk >= 1, pltpu.roll(z_ref[p + CHUNK].astype(F32), 1, 0), 0.0)
        if p >= CHUNK:
            return jnp.where(chunk < n_c - 1, pltpu.roll(z_ref[p - CHUNK].astype(F32), n_c - 1, 0), 0.0)
        return z_ref[p].astype(F32)

    def inv_count(s_t, half):
        n = n_c * CHUNK
        first = min(s_t + half, n) - max(s_t - half, 0)
        t_last = n - CHUNK + s_t
        last = min(t_last + half, n) - max(t_last - half, 0)
        inv = 1.0 / (2 * half)
        if first != 2 * half:
            inv = jnp.where(chunk == 0, 1.0 / first, inv)
        if last != 2 * half:
            inv = jnp.where(chunk == n_c - 1, 1.0 / last, inv)
        return inv

    for k, w in enumerate(POOL_WINDOWS):
        @pl.when(gi == k)
        def _(w=w):
            half = w // 2
            total = plane(-half)
            for p in range(-half + 1, half):
                total = total + plane(p)
            diffs = []
            for s_t in range(CHUNK):
                diffs.append((total * inv_count(s_t, half) - plane(s_t)).astype(BF16))
                if s_t + 1 < CHUNK:
                    total = total + plane(s_t + half) - plane(s_t - half)
            diff = jnp.concatenate(diffs, axis=0)
            out = _dot(diff, w_ref[...].astype(BF16)) * sc_ref[...]
            o_ref[...] = out.reshape(o_ref.shape).astype(o_ref.dtype)


def _pool(zp_t, pool_w, pool_scale, bsz):
    _, nch, _ = zp_t.shape
    n_c = nch // bsz
    return pl.pallas_call(
        _pool_kernel,
        grid=(bsz, len(POOL_WINDOWS)),
        in_specs=[
            pl.BlockSpec((CHUNK, n_c, POOL_GROUP), lambda i, g: (0, i, g)),
            pl.BlockSpec((None, POOL_GROUP, POOL_GROUP), lambda i, g: (g, 0, 0)),
            pl.BlockSpec((1, POOL_GROUP), lambda i, g: (0, g)),
        ],
        out_specs=pl.BlockSpec((CHUNK, n_c, POOL_GROUP), lambda i, g: (0, i, g)),
        out_shape=jax.ShapeDtypeStruct(zp_t.shape, BF16),
        compiler_params=_params(("parallel", "parallel")),
        name="pool",
    )(zp_t, pool_w, pool_scale)


def _expand_consts():
    time = np.arange(CHUNK_W) // SSM_GROUP
    def onehot(e):
        m = np.zeros((CHUNK_W, LANES), np.float32)
        m[np.arange(CHUNK_W), e] = 1.0
        return m
    return np.stack([
        onehot(CHUNK - 1 - time),
        onehot(time),
        onehot(time + 1),
        onehot(CHUNK - time),
    ])


def _ssm_kernel(u_ref, vec_ref, mat_ref, exp_ref, wg_ref, wu_ref, wd_ref, y_ref, og_ref, ou_ref,
                od_ref, *, seq_chunks):
    og_ref[...] = wg_ref[...].astype(BF16)
    ou_ref[...] = wu_ref[...].astype(BF16)
    od_ref[...] = wd_ref[...].astype(BF16)

    n_ct = u_ref.shape[0]
    nch = n_ct * C_TILE
    half = LANES // 2
    lane = lax.broadcasted_iota(jnp.int32, (1, LANES), 1)
    lo_half = lane < half

    def direction(di):
        a_re = vec_ref[di, 0:1]
        a_im = vec_ref[di, 1:2]
        dt = jnp.exp(vec_ref[di, 2:3])
        mag = jnp.exp(a_re * dt)
        ang = a_im * dt
        lb_re = mag * jnp.cos(ang)
        lb_im = mag * jnp.sin(ang)
        den = a_re * a_re + a_im * a_im
        f_re = ((lb_re - 1.0) * a_re + lb_im * a_im) / den
        f_im = (lb_im * a_re - (lb_re - 1.0) * a_im) / den
        return (lb_re, lb_im), f_re, f_im

    def power_table(lam):
        e = lax.broadcasted_iota(jnp.int32, (LANES, 1), 0)
        sq_re, sq_im = lam
        t_re = jnp.ones((LANES, LANES), F32)
        t_im = jnp.zeros((LANES, LANES), F32)
        for k in range(CHUNK.bit_length()):
            bit = ((e >> k) & 1) == 1
            t_re, t_im = (jnp.where(bit, t_re * sq_re - t_im * sq_im, t_re),
                          jnp.where(bit, t_re * sq_im + t_im * sq_re, t_im))
            sq_re, sq_im = sq_re * sq_re - sq_im * sq_im, 2.0 * sq_re * sq_im
        return jnp.where(lo_half, t_re, t_im), jnp.where(lo_half, t_im, t_re)

    def tile_rows(x16):
        return jnp.broadcast_to(x16[None], (CHUNK, SSM_GROUP, LANES)).reshape(CHUNK_W, LANES)

    def expanded(tab, which, v_re, v_im, conj_sign):
        lexp = _dot(exp_ref[which], jnp.concatenate(tab, axis=1).astype(BF16))
        if conj_sign > 0:
            p = jnp.where(lo_half, v_re, v_re)
            q = jnp.where(lo_half, -v_im, v_im)
        else:
            p = jnp.where(lo_half, v_re, -v_re)
            q = jnp.where(lo_half, -v_im, -v_im)
        return lexp[:, :LANES] * tile_rows(p) + lexp[:, LANES:] * tile_rows(q)

    lam_f, ff_re, ff_im = direction(0)
    lam_b, fb_re, fb_im = direction(1)
    tabs_f = power_table(lam_f)
    tabs_b = power_table(lam_b)
    tab_f, tab_b = tabs_f[0], tabs_b[0]

    def bbar(bt_re, bt_im, f_re, f_im):
        return bt_re * f_re - bt_im * f_im, bt_re * f_im + bt_im * f_re

    def mat(di, k):
        return mat_ref[di, k * SSM_GROUP:(k + 1) * SSM_GROUP, :]

    bf_re, bf_im = bbar(mat(0, 0), mat(0, 1), ff_re, ff_im)
    bb_re, bb_im = bbar(mat(1, 0), mat(1, 1), fb_re, fb_im)

    pb1 = expanded(tabs_f, 0, bf_re, bf_im, 1)
    pb2 = expanded(tabs_b, 1, bb_re, bb_im, 1)
    pb3 = expanded(tabs_b, 2, bb_re, bb_im, 1)
    ft_f = expanded(tabs_f, 2, mat(0, 3), mat(0, 4), -1)
    ft_b = expanded(tabs_b, 3, mat(1, 3), mat(1, 4), -1)

    row_w = lax.broadcasted_iota(jnp.int32, (CHUNK_W, 1), 0)
    last_blk = row_w >= CHUNK_W - SSM_GROUP
    pb2_lag0 = jnp.where(last_blk, pltpu.roll(pb2, CHUNK_W - SSM_GROUP, 0), 0.0)
    ccr_f = mat(0, 2).astype(BF16)
    ccr_b = mat(1, 2).astype(BF16)
    r_lo = _dot_nt(ccr_f, pb1.astype(BF16)) + _dot_nt(ccr_b, pb2_lag0.astype(BF16))
    co = lax.broadcasted_iota(jnp.int32, (SSM_GROUP, CHUNK_W), 0)
    col = lax.broadcasted_iota(jnp.int32, (SSM_GROUP, CHUNK_W), 1)
    r_lo = r_lo + jnp.where(col == CHUNK_W - SSM_GROUP + co, mat(0, 5)[:, 0:1], 0.0)
    r_hi = _dot_nt(ccr_b, pb3.astype(BF16))
    r_t = jnp.concatenate([r_lo, r_hi], axis=1)
    g_t = jnp.concatenate(
        [pltpu.roll(r_t, SSM_GROUP * (tl + 1), 1) for tl in range(T_SUB)], axis=0
    ).astype(BF16)

    u = jnp.concatenate([u_ref[ct].reshape(CHUNK_W, C_TILE) for ct in range(n_ct)],
                        axis=1)
    e_mat = jnp.concatenate([pb1, pb2], axis=1).astype(BF16)
    xend = _dot_tn(e_mat, u)
    lanec = lax.broadcasted_iota(jnp.int32, (1, nch), 1) % seq_chunks
    ns = SSM_STATE

    def scan(re, im, tab, forward):
        lam_col = jnp.transpose(tab[CHUNK:CHUNK + 8, :])[:, 0:1]
        a, b = lam_col[:ns], lam_col[ns:]
        n_steps = int(math.log2(seq_chunks))

        def rolled(v, d):
            if d % LANES == 0:
                return jnp.concatenate([v[:, nch - d:], v[:, :nch - d]], axis=1)
            return pltpu.roll(v, d, 1)

        def shifted(v, d):
            if forward:
                return jnp.where(lanec >= d, rolled(v, d), 0.0)
            return jnp.where(lanec < seq_chunks - d, rolled(v, nch - d), 0.0)

        for k in range(n_steps):
            sr, si = shifted(re, 1 << k), shifted(im, 1 << k)
            re, im = re + (sr * a - si * b), im + (sr * b + si * a)
            a, b = a * a - b * b, 2.0 * a * b
        return shifted(re, 1), shifted(im, 1)

    f_re, f_im = scan(xend[:ns], xend[ns:2 * ns], tab_f, True)
    b_re, b_im = scan(xend[2 * ns:3 * ns], xend[3 * ns:], tab_b, False)
    xin = jnp.concatenate([f_re, f_im, b_re, b_im], axis=0).astype(BF16)
    f_t = jnp.concatenate([ft_f, ft_b], axis=1).astype(BF16)

    toeplitz = jnp.concatenate(
        [g_t[:, CHUNK_W - LANES * th:2 * CHUNK_W - LANES * th] for th in range(CHUNK // T_SUB)],
        axis=0)
    y_t = _dot(toeplitz, u) + _dot(f_t, xin)
    for ct in range(n_ct):
        y_ref[ct] = y_t[:, ct * C_TILE:(ct + 1) * C_TILE].reshape(
            CHUNK, SSM_GROUP, C_TILE).astype(y_ref.dtype)


def _ssm(ut, a_re, a_im, log_dt, b_re, b_im, c_re, c_im, d, seq_chunks, w_gate, w_up, w_down, layer):
    g, n_ct = ut.shape[:2]
    assert N_EXPERTS % g == 0
    epg = N_EXPERTS // g
    base = layer * g
    n = SSM_STATE

    def per_group(a):
        return jnp.swapaxes(a, 0, 1)

    def dup(a):
        return jnp.concatenate([a, a], axis=-1)

    vecs = dup(jnp.stack([per_group(a_re), per_group(a_im),
                          jnp.broadcast_to(per_group(log_dt)[..., None], (g, 2, n))], axis=2))
    cr, ci = per_group(c_re), per_group(c_im)
    d_blk = jnp.broadcast_to(d.reshape(g, 1, SSM_GROUP, 1), (g, 2, SSM_GROUP, LANES))
    mats = jnp.concatenate(
        [dup(jnp.swapaxes(per_group(b_re), 2, 3)), dup(jnp.swapaxes(per_group(b_im), 2, 3)),
         jnp.concatenate([cr, -ci], axis=-1), dup(cr), dup(ci), d_blk], axis=2)
    exp_c = jnp.asarray(_expand_consts(), BF16)

    return pl.pallas_call(
        functools.partial(_ssm_kernel, seq_chunks=seq_chunks),
        grid=(g,),
        in_specs=[
            pl.BlockSpec((None, n_ct, CHUNK, SSM_GROUP, C_TILE), lambda i: (i, 0, 0, 0, 0)),
            pl.BlockSpec((None, 2, 3, LANES), lambda i: (i, 0, 0, 0)),
            pl.BlockSpec((None, 2, 6 * SSM_GROUP, LANES), lambda i: (i, 0, 0, 0)),
            pl.BlockSpec((4, CHUNK_W, LANES), lambda i: (0, 0, 0)),
            pl.BlockSpec((epg, D_MODEL, D_EXPERT), lambda i: (base + i, 0, 0)),
            pl.BlockSpec((epg, D_MODEL, D_EXPERT), lambda i: (base + i, 0, 0)),
            pl.BlockSpec((epg, D_EXPERT, D_MODEL), lambda i: (base + i, 0, 0)),
        ],
        out_specs=[
            pl.BlockSpec((None, n_ct, CHUNK, SSM_GROUP, C_TILE), lambda i: (i, 0, 0, 0, 0)),
            pl.BlockSpec((epg, D_MODEL, D_EXPERT), lambda i: (i, 0, 0)),
            pl.BlockSpec((epg, D_MODEL, D_EXPERT), lambda i: (i, 0, 0)),
            pl.BlockSpec((epg, D_EXPERT, D_MODEL), lambda i: (i, 0, 0)),
        ],
        out_shape=[
            jax.ShapeDtypeStruct(ut.shape, BF16),
            jax.ShapeDtypeStruct((N_EXPERTS, D_MODEL, D_EXPERT), BF16),
            jax.ShapeDtypeStruct((N_EXPERTS, D_MODEL, D_EXPERT), BF16),
            jax.ShapeDtypeStruct((N_EXPERTS, D_EXPERT, D_MODEL), BF16),
        ],
        compiler_params=_params(("parallel",)),
        name="ssm",
    )(ut, vecs, mats, exp_c, w_gate, w_up, w_down)


def _mix_out_kernel(x_hbm, a_ref, yt_ref, gwt_ref, gb_ref, wo_ref, h_hbm, xbuf, hbuf, xsem, hsem):
    nc = C_TILE
    x = _load_time_major_rows(x_hbm, xbuf, xsem)
    a = a_ref[...].reshape(S_TILE * nc, D_POOL)
    y = jnp.concatenate([yt_ref[:, j, :, :].reshape(D_SSM, nc) for j in range(S_TILE)],
                        axis=1).astype(F32)
    z = 0.5 * y * (1.0 + jnp.tanh(math.sqrt(2.0 / math.pi) * (y + 0.044715 * (y * y * y))))
    gate = _sigmoid(_dot(gwt_ref[...], z.astype(BF16)) + gb_ref[...])
    s = (z * gate).astype(BF16)
    h = x + _dot(a, wo_ref[:D_POOL, :]) + _dot_tn(s, wo_ref[D_POOL:, :])
    _store_time_major_rows(h, h_hbm, hbuf, hsem)


def _mix_out(x3, a3, yt, glu_w_t, glu_b_col, w_out):
    nch = x3.shape[0]
    return pl.pallas_call(
        _mix_out_kernel,
        grid=(nch // C_TILE, CHUNK // S_TILE),
        in_specs=[
            pl.BlockSpec(memory_space=pl.ANY),
            pl.BlockSpec((S_TILE, C_TILE, D_POOL), lambda c, t: (t, c, 0)),
            pl.BlockSpec((N_SSM_GROUPS, None, S_TILE, SSM_GROUP, C_TILE), lambda c, t: (0, c, t, 0, 0)),
            pl.BlockSpec((D_SSM, D_SSM), lambda c, t: (0, 0)),
            pl.BlockSpec((D_SSM, 1), lambda c, t: (0, 0)),
            pl.BlockSpec((D_MODEL, D_MODEL), lambda c, t: (0, 0)),
        ],
        out_specs=pl.BlockSpec(memory_space=pl.ANY),
        out_shape=jax.ShapeDtypeStruct((nch, CHUNK, D_MODEL), F32),
        scratch_shapes=[
            pltpu.VMEM((2, S_TILE, C_TILE, D_MODEL), F32),
            pltpu.VMEM((2, S_TILE, C_TILE, D_MODEL), F32),
            pltpu.SemaphoreType.DMA((2,)),
            pltpu.SemaphoreType.DMA((2,)),
        ],
        compiler_params=_params(("arbitrary", "arbitrary")),
        name="mix_out",
    )(x3, a3, yt, glu_w_t, glu_b_col, w_out)


def _pack_rows(x):
    b = lax.bitcast_convert_type(x.astype(BF16).astype(F32), U32)
    return (b[:, :HALF] & jnp.uint32(0xFFFF0000)) | (b[:, HALF:] >> 16)


def _unpack_rows(w):
    lo = lax.bitcast_convert_type(w & jnp.uint32(0xFFFF0000), F32)
    hi = lax.bitcast_convert_type(w << 16, F32)
    return lo, hi


def _split_bf16(x):
    hi = x.astype(BF16)
    return hi, (x - hi.astype(F32)).astype(BF16)


def _route(v32, wr_ref, br_ref):
    v_hi, v_lo = _split_bf16(v32)
    w_hi, w_lo = _split_bf16(wr_ref[...])
    both = _dot(v_hi, jnp.concatenate([w_hi, w_lo], axis=1))
    logits = both[:, :ROUTER_W] + (both[:, ROUTER_W:] + _dot(v_lo, w_hi)) + br_ref[...]
    return jnp.transpose(logits)


def _top1(x, valid=None):
    n = x.shape[0]
    row = lax.broadcasted_iota(jnp.int32, x.shape, 0).astype(F32)
    if valid is not None:
        x = jnp.where(valid, x, -jnp.inf)
    m = jnp.max(x, axis=0, keepdims=True)
    idx = jnp.min(jnp.where(x == m, row, float(n)), axis=0, keepdims=True)
    return m, idx, x, row


def _split_planes(packed, ref):
    ref[0] = packed[:, :SC_ROW]
    ref[1] = packed[:, SC_ROW:]


def _router_kernel(h_ref, g_ref, wr_ref, br_ref, before_ref, vp_ref, meta_t_ref, cnt_ref, carry_ref):
    @pl.when(pl.program_id(0) == 0)
    def _():
        carry_ref[...] = jnp.zeros_like(carry_ref)

    v32 = _rms(h_ref[...], g_ref[...])
    _split_planes(_pack_rows(v32), vp_ref)
    lt = _route(v32, wr_ref, br_ref)
    tm = lt.shape[1]
    eg = EXPERTS_PER_GROUP

    grp = lt[:eg]
    grp_row = lax.broadcasted_iota(jnp.int32, grp.shape, 0)
    mg, grp_idx, grp, _ = _top1(grp, grp_row < N_EXPERT_GROUPS)
    grp_p = 1.0 / jnp.sum(jnp.exp(grp - mg), axis=0, keepdims=True)
    le = jnp.zeros((eg, tm), F32)
    for g in range(N_EXPERT_GROUPS):
        le = jnp.where(grp_idx == float(g), lt[eg * (g + 1):eg * (g + 2)], le)
    m1, i1, le, row = _top1(le)
    z = jnp.sum(jnp.exp(le - m1), axis=0, keepdims=True)
    m2, i2, _, _ = _top1(jnp.where(row == i1, -jnp.inf, le))
    p1 = 1.0 / z
    p2 = jnp.exp(m2 - m1) / z
    tot = p1 + p2
    w1 = grp_p * (p1 / tot)
    w2 = grp_p * (p2 / tot)
    e1 = grp_idx * eg + i1
    e2 = grp_idx * eg + i2

    erow = lax.broadcasted_iota(jnp.int32, (N_EXPERTS, tm), 0).astype(F32)
    onehot = jnp.where(erow == e1, 1.0, jnp.where(erow == e2, 1.0, 0.0))
    before = _dot(onehot.astype(BF16), before_ref[...]) + carry_ref[...]
    rank1 = jnp.sum(jnp.where(erow == e1, before, 0.0), axis=0, keepdims=True)
    rank2 = jnp.sum(jnp.where(erow == e2, before, 0.0), axis=0, keepdims=True)
    carry = carry_ref[...] + jnp.sum(onehot, axis=1, keepdims=True)
    carry_ref[...] = carry
    cnt_ref[...] = carry

    mrow = lax.broadcasted_iota(jnp.int32, (META_ROWS, tm), 0)
    meta_t_ref[...] = jnp.where(mrow == 0, e1, jnp.where(mrow == 1, e2, jnp.where(
        mrow == 2, rank1, jnp.where(mrow == 3, rank2, jnp.where(
            mrow == 4, w1, jnp.where(mrow == 5, w2, 0.0))))))


def _earlier_matrix(tm):
    return np.triu(np.ones((tm, tm), np.float32), k=1)


def _router(h1, g_ffn, w_router, b_router, tm):
    t = h1.shape[0]
    return pl.pallas_call(
        _router_kernel,
        grid=(t // tm,),
        in_specs=[
            pl.BlockSpec((tm, D_MODEL), lambda i: (i, 0)),
            pl.BlockSpec((1, D_MODEL), lambda i: (0, 0)),
            pl.BlockSpec((D_MODEL, ROUTER_W), lambda i: (0, 0)),
            pl.BlockSpec((1, ROUTER_W), lambda i: (0, 0)),
            pl.BlockSpec((tm, tm), lambda i: (0, 0)),
        ],
        out_specs=[
            pl.BlockSpec((2, tm, SC_ROW), lambda i: (0, i, 0)),
            pl.BlockSpec((META_ROWS, tm), lambda i: (0, i)),
            pl.BlockSpec((N_EXPERTS, 1), lambda i: (0, 0)),
        ],
        out_shape=[
            jax.ShapeDtypeStruct((2, t, SC_ROW), U32),
            jax.ShapeDtypeStruct((META_ROWS, t), F32),
            jax.ShapeDtypeStruct((N_EXPERTS, 1), F32),
        ],
        scratch_shapes=[pltpu.VMEM((N_EXPERTS, 1), F32)],
        compiler_params=_params(("arbitrary",)),
        name="router",
    )(h1, g_ffn, w_router, b_router, jnp.asarray(_earlier_matrix(tm), BF16))


def _plan(meta_t, counts, n_tiles):
    e1 = meta_t[0].astype(jnp.int32)
    e2 = meta_t[1].astype(jnp.int32)
    rank1 = meta_t[2].astype(jnp.int32)
    rank2 = meta_t[3].astype(jnp.int32)
    cnt = counts[:, 0].astype(jnp.int32)
    padded = ((cnt + ROW_TILE - 1) // ROW_TILE) * ROW_TILE
    ends = jnp.cumsum(padded)
    starts = ends - padded
    experts = jnp.arange(N_EXPERTS, dtype=jnp.int32)
    pos1 = rank1 + jnp.sum(jnp.where(e1[None, :] == experts[:, None], starts[:, None], 0), axis=0)
    pos2 = rank2 + jnp.sum(jnp.where(e2[None, :] == experts[:, None], starts[:, None], 0), axis=0)
    tile_start = jnp.arange(n_tiles, dtype=jnp.int32) * ROW_TILE
    tile_expert = jnp.sum((tile_start[:, None] >= ends[None, :]).astype(jnp.int32), axis=1)
    tile_expert = jnp.minimum(tile_expert, N_EXPERTS - 1)
    rows_left = jnp.sum(jnp.where(tile_expert[:, None] == experts, cnt + starts, 0), axis=1) - tile_start
    n_valid = jnp.clip(rows_left, 0, ROW_TILE).astype(jnp.int32)
    last_used = jnp.maximum(ends[-1] // ROW_TILE - 1, 0)
    block = jnp.minimum(jnp.arange(n_tiles, dtype=jnp.int32), last_used)
    tile_expert = jnp.sum(jnp.where(block[:, None] == jnp.arange(n_tiles)[None, :],
                                    tile_expert[None, :], 0), axis=1)
    plane = n_tiles * ROW_TILE
    half_rows = jnp.concatenate([pos1, pos1 + plane, pos2, pos2 + plane])[None]
    return half_rows, tile_expert, n_valid, block


def _sc_mesh():
    return plsc.VectorSubcoreMesh(core_axis_name="c", subcore_axis_name="s")


def _sc_scatter_rows(rows, idx, n_out):
    t, width = rows.shape
    steps = t // SC_WINDOW
    half = steps // 2

    @pl.kernel(out_type=jax.ShapeDtypeStruct((n_out, width), rows.dtype), mesh=_sc_mesh(),
               scratch_types=[], name="moe_scatter")
    def scatter(rows_hbm, idx_hbm, out_hbm):
        def body(rows_vmem, idx0_vmem, idx1_vmem):
            pltpu.sync_copy(rows_vmem, out_hbm.at[idx0_vmem.at[0]])
            pltpu.sync_copy(rows_vmem, out_hbm.at[idx1_vmem.at[0]])

        pltpu.emit_pipeline(
            body,
            grid=(2, half),
            in_specs=[pl.BlockSpec((SC_WINDOW, width), lambda c, j: (c * half + j, 0)),
                      pl.BlockSpec((1, SC_WINDOW), lambda c, j: (0, c * half + j)),
                      pl.BlockSpec((1, SC_WINDOW), lambda c, j: (0, steps + c * half + j))],
            out_specs=[],
            core_axis_name=("c", "s"),
            dimension_semantics=(pltpu.PARALLEL, pltpu.PARALLEL),
        )(rows_hbm, idx_hbm, idx_hbm)

    return scatter(rows, idx)


def _sc_gather_rows(table, idx):
    m = idx.shape[1]
    width = table.shape[1]
    steps = m // (2 * SC_WINDOW)

    @pl.kernel(out_type=jax.ShapeDtypeStruct((m, width), table.dtype), mesh=_sc_mesh(),
               scratch_types=[], name="moe_gather")
    def gather(table_hbm, idx_hbm, out_hbm):
        def body(idx_vmem, out_vmem):
            pltpu.sync_copy(table_hbm.at[idx_vmem.at[0]], out_vmem)

        pltpu.emit_pipeline(
            body,
            grid=(2, steps),
            in_specs=[pl.BlockSpec((1, SC_WINDOW), lambda k, j: (0, k * steps + j))],
            out_specs=[pl.BlockSpec((SC_WINDOW, width), lambda k, j: (k * steps + j, 0))],
            core_axis_name=("c", "s"),
            dimension_semantics=(pltpu.PARALLEL, pltpu.PARALLEL),
        )(idx_hbm, out_hbm)

    return gather(table, idx)


def _experts_kernel(te_ref, nv_ref, blk_ref, xs_ref, wg_ref, wu_ref, wd_ref, ys_ref):
    r = pl.program_id(0)
    n_valid = nv_ref[r]

    @pl.when(n_valid > 0)
    def _():
        parts = [p.astype(BF16) for p in _unpack_rows(xs_ref[0]) + _unpack_rows(xs_ref[1])]
        cols = (0, 2 * SC_ROW, SC_ROW, 3 * SC_ROW)
        hg = sum(_dot(p, wg_ref[c:c + SC_ROW, :]) for p, c in zip(parts, cols))
        hu = sum(_dot(p, wu_ref[c:c + SC_ROW, :]) for p, c in zip(parts, cols))
        row = lax.broadcasted_iota(jnp.int32, (ROW_TILE, 1), 0)
        hid = jnp.where(row < n_valid, hg * _sigmoid(hg) * hu, 0.0).astype(BF16)
        _split_planes(_pack_rows(_dot(hid, wd_ref[...])), ys_ref)


def _experts(xs, tile_expert, n_valid, block, w_gate, w_up, w_down):
    n_tiles = xs.shape[1] // ROW_TILE
    w_spec = pl.BlockSpec((None, D_MODEL, D_EXPERT), lambda r, te, nv, blk: (te[r], 0, 0))
    grid_spec = pltpu.PrefetchScalarGridSpec(
        num_scalar_prefetch=3,
        grid=(n_tiles,),
        in_specs=[
            pl.BlockSpec((2, ROW_TILE, SC_ROW), lambda r, te, nv, blk: (0, blk[r], 0)),
            w_spec, w_spec,
            pl.BlockSpec((None, D_EXPERT, D_MODEL), lambda r, te, nv, blk: (te[r], 0, 0)),
        ],
        out_specs=pl.BlockSpec((2, ROW_TILE, SC_ROW), lambda r, te, nv, blk: (0, blk[r], 0)),
    )
    return pl.pallas_call(
        _experts_kernel,
        grid_spec=grid_spec,
        out_shape=jax.ShapeDtypeStruct((2, n_tiles * ROW_TILE, SC_ROW), U32),
        compiler_params=_params(("arbitrary",)),
        name="experts",
    )(tile_expert, n_valid, block, xs, w_gate, w_up, w_down)


def _ple_kernel(h_ref, yg_ref, meta_t_ref, p_ref, gp_ref, wg_ref, bg_ref, wp_ref, gf_ref, *rest, final_norm):
    o_ref = rest[-1]
    meta = jnp.transpose(meta_t_ref[...])
    w1 = meta[:, 4:5]
    w2 = meta[:, 5:6]
    q0, q2 = (w1 * u + w2 * v for u, v in zip(_unpack_rows(yg_ref[0]), _unpack_rows(yg_ref[2])))
    q1, q3 = (w1 * u + w2 * v for u, v in zip(_unpack_rows(yg_ref[1]), _unpack_rows(yg_ref[3])))
    moe = jnp.concatenate([q0, q1, q2, q3], axis=1)
    h = h_ref[...] + moe
    gate = _sigmoid(_dot(_rms(h, gp_ref[...]).astype(BF16), wg_ref[...]) + bg_ref[...])
    h = h + gate * _dot(p_ref[...].astype(BF16), wp_ref[...])
    o_ref[...] = _rms(h, gf_ref[...]) if final_norm else h


def _ple(h1, yg, meta_t, p2, g_ple, w_gate, b_gate, w_proj, g_final, final_norm, tm, off, prev_out):
    t = h1.shape[0]
    steps = yg.shape[1] // tm
    in_specs = [
        pl.BlockSpec((tm, D_MODEL), lambda i: (i + off, 0)),
        pl.BlockSpec((4, tm, SC_ROW), lambda i: (0, i, 0)),
        pl.BlockSpec((META_ROWS, tm), lambda i: (0, i + off)),
        pl.BlockSpec((tm, D_PLE), lambda i: (i + off, 0)),
        pl.BlockSpec((1, D_MODEL), lambda i: (0, 0)),
        pl.BlockSpec((D_MODEL, D_MODEL), lambda i: (0, 0)),
        pl.BlockSpec((1, D_MODEL), lambda i: (0, 0)),
        pl.BlockSpec((D_PLE, D_MODEL), lambda i: (0, 0)),
        pl.BlockSpec((1, D_MODEL), lambda i: (0, 0)),
    ]
    args = [h1, yg, meta_t, p2, g_ple, w_gate, b_gate, w_proj, g_final]
    aliases = {}
    if prev_out is not None:
        in_specs.append(pl.BlockSpec(memory_space=pl.ANY))
        args.append(prev_out)
        aliases = {len(args) - 1: 0}
    return pl.pallas_call(
        functools.partial(_ple_kernel, final_norm=final_norm),
        grid=(steps,),
        in_specs=in_specs,
        out_specs=pl.BlockSpec((tm, D_MODEL), lambda i: (i + off, 0)),
        out_shape=jax.ShapeDtypeStruct((t, D_MODEL), F32),
        input_output_aliases=aliases,
        compiler_params=_params(("parallel",)),
        name="ple",
    )(*args)


def kernel(x, p, g_mix, w_in, pool_w, pool_scale, ssm_a_re, ssm_a_im, ssm_log_dt, ssm_b_re,
           ssm_b_im, ssm_c_re, ssm_c_im, ssm_d, glu_w, glu_b, w_out, g_ffn, router_grp_w,
           router_grp_b, router_exp_w, router_exp_b, exp_w_gate, exp_w_up, exp_w_down, g_ple,
           ple_gate_w, ple_gate_b, ple_proj_w, g_final):
    bsz, seq, dm = x.shape
    depth = g_mix.shape[0]
    t = bsz * seq
    seq_chunks = seq // CHUNK
    nch = t // CHUNK
    tm = 1024
    n_sorted = (pl.cdiv(2 * t, ROW_TILE) + N_EXPERTS) * ROW_TILE
    w_gate_all = exp_w_gate.reshape(depth * N_EXPERTS, dm, D_EXPERT)
    w_up_all = exp_w_up.reshape(depth * N_EXPERTS, dm, D_EXPERT)
    w_down_all = exp_w_down.reshape(depth * N_EXPERTS, D_EXPERT, dm)

    h = x.reshape(t, dm)
    for i in range(depth):
        w_in_b = (g_mix[i][:, None] * w_in[i]).astype(BF16)
        zp, ut = _in_proj(h.reshape(nch, CHUNK, dm), w_in_b[:, :D_POOL],
                          jnp.transpose(w_in_b[:, D_POOL:]))
        a = _pool(zp, pool_w[i], pool_scale[i][None], bsz)
        yt, w_gate_b, w_up_b, w_down_b = _ssm(
            ut, ssm_a_re[i], ssm_a_im[i], ssm_log_dt[i], ssm_b_re[i], ssm_b_im[i], ssm_c_re[i],
            ssm_c_im[i], ssm_d[i], seq_chunks, w_gate_all, w_up_all, w_down_all, i)
        h = _mix_out(h.reshape(nch, CHUNK, dm), a, yt,
                     jnp.transpose(glu_w[i]).astype(BF16), glu_b[i][:, None],
                     w_out[i].astype(BF16)).reshape(t, dm)

        eg = EXPERTS_PER_GROUP
        w_router = jnp.concatenate(
            [router_grp_w[i], jnp.zeros((dm, eg - N_EXPERT_GROUPS), F32),
             jnp.transpose(router_exp_w[i], (1, 0, 2)).reshape(dm, N_EXPERTS),
             jnp.zeros((dm, ROUTER_W - eg - N_EXPERTS), F32)], axis=1)
        b_router = jnp.concatenate(
            [router_grp_b[i], jnp.zeros((eg - N_EXPERT_GROUPS,), F32),
             router_exp_b[i].reshape(N_EXPERTS),
             jnp.zeros((ROUTER_W - eg - N_EXPERTS,), F32)])[None]
        vp, meta_t, counts = _router(h, g_ffn[i][None], w_router, b_router, tm)
        idx, tile_expert, n_valid, block = _plan(meta_t, counts, n_sorted // ROW_TILE)
        xs = _sc_scatter_rows(vp.reshape(2 * t, SC_ROW), idx, 2 * n_sorted)
        ys = _experts(xs.reshape(2, n_sorted, SC_ROW), tile_expert, n_valid, block,
                      w_gate_b, w_up_b, w_down_b)
        ys2 = ys.reshape(2 * n_sorted, SC_ROW)
        idx4 = idx.reshape(4, t)
        ple_wg = ple_gate_w[i].astype(BF16)
        ple_wp = ple_proj_w[i].astype(BF16)
        out = None
        cuts = (0, t // FIRST_PART, t)
        for lo, hi in zip(cuts[:-1], cuts[1:]):
            yg_q = _sc_gather_rows(ys2, idx4[:, lo:hi].reshape(1, 4 * (hi - lo)))
            out = _ple(h, yg_q.reshape(4, hi - lo, SC_ROW), meta_t, p[i].reshape(t, D_PLE),
                       g_ple[i][None], ple_wg,
                       ple_gate_b[i][None], ple_wp, g_final[None], i == depth - 1, tm, lo // tm, out)
        h = out
    return h.reshape(bsz, seq, dm)
```

```python
import functools
import math

import numpy as np
import jax
import jax.numpy as jnp
from jax import lax
from jax.experimental import pallas as pl
from jax.experimental.pallas import tpu as pltpu
from jax.experimental.pallas import tpu_sc as plsc

F32 = jnp.float32
BF16 = jnp.bfloat16
U32 = jnp.uint32

D_MODEL = 1024
D_POOL = 512
D_SSM = 512
POOL_WINDOWS = (2, 4, 8, 16)
POOL_GROUP = 128
SSM_GROUP = 16
N_SSM_GROUPS = 32
SSM_STATE = 64
N_EXPERT_GROUPS = 4
EXPERTS_PER_GROUP = 8
N_EXPERTS = N_EXPERT_GROUPS * EXPERTS_PER_GROUP
D_EXPERT = 256
D_PLE = 256
RMS_EPS = 1e-6

LANES = 128
CHUNK = 32
CHUNK_W = CHUNK * SSM_GROUP
T_SUB = 8
S_TILE = 8
C_TILE = 128
ROUTER_W = LANES
HALF = D_MODEL // 2
ROW_TILE = 1280
ROW_BLOCK = 256
SC_WINDOW = 128
SC_ROW = HALF // 2
META_ROWS = 8
FIRST_PART = 4
VMEM_LIMIT = 56 * 1024 * 1024


def _dot(a, b):
    return jnp.dot(a, b, preferred_element_type=F32)


def _dot_nt(a, b):
    return lax.dot_general(a, b, (((1,), (1,)), ((), ())), preferred_element_type=F32)


def _dot_tn(a, b):
    return lax.dot_general(a, b, (((0,), (0,)), ((), ())), preferred_element_type=F32)


def _rms(x, g=None):
    y = x * lax.rsqrt(jnp.mean(x * x, axis=-1, keepdims=True) + RMS_EPS)
    return y if g is None else y * g


def _sigmoid(x):
    return 1.0 / (1.0 + jnp.exp(-x))


def _params(sem):
    return pltpu.CompilerParams(dimension_semantics=sem, vmem_limit_bytes=VMEM_LIMIT)


def _tile_step():
    n_s = pl.num_programs(1)
    return pl.program_id(0) * n_s + pl.program_id(1), pl.num_programs(0) * n_s


def _row_copies(hbm, buf, sem, step_idx, slot_idx, to_hbm):
    n_s = pl.num_programs(1)
    c0 = (step_idx // n_s) * C_TILE
    s0 = (step_idx % n_s) * S_TILE
    out = []
    for j in range(S_TILE):
        far, near = hbm.at[pl.ds(c0, C_TILE), s0 + j, :], buf.at[slot_idx, j]
        src, dst = (near, far) if to_hbm else (far, near)
        out.append(pltpu.make_async_copy(src, dst, sem.at[slot_idx]))
    return out


def _load_time_major_rows(hbm, buf, sem):
    step, n_steps = _tile_step()
    slot = step % 2

    @pl.when(step == 0)
    def _():
        for cp in _row_copies(hbm, buf, sem, step, slot, False):
            cp.start()

    @pl.when(step + 1 < n_steps)
    def _():
        for cp in _row_copies(hbm, buf, sem, step + 1, 1 - slot, False):
            cp.start()

    for cp in _row_copies(hbm, buf, sem, step, slot, False):
        cp.wait()
    return buf[slot].reshape(S_TILE * C_TILE, buf.shape[-1])


def _store_time_major_rows(val, hbm, buf, sem):
    step, n_steps = _tile_step()
    slot = step % 2

    @pl.when(step >= 2)
    def _():
        for cp in _row_copies(hbm, buf, sem, step - 2, slot, True):
            cp.wait()

    buf[slot] = val.reshape(S_TILE, C_TILE, val.shape[-1])
    for cp in _row_copies(hbm, buf, sem, step, slot, True):
        cp.start()

    @pl.when(step == n_steps - 1)
    def _():
        @pl.when(step >= 1)
        def _():
            for cp in _row_copies(hbm, buf, sem, step - 1, 1 - slot, True):
                cp.wait()
        for cp in _row_copies(hbm, buf, sem, step, slot, True):
            cp.wait()


def _in_proj_kernel(x_hbm, wp_ref, wst_ref, zp_ref, ut_ref, xbuf, xsem):
    nc = C_TILE
    u = _rms(_load_time_major_rows(x_hbm, xbuf, xsem)).astype(BF16)
    zp_ref[...] = _dot(u, wp_ref[...]).reshape(zp_ref.shape).astype(zp_ref.dtype)
    zt = _dot_nt(wst_ref[...], u).astype(BF16)
    for j in range(S_TILE):
        ut_ref[:, j, :, :] = zt[:, j * nc:(j + 1) * nc].reshape(N_SSM_GROUPS, SSM_GROUP, nc)


def _in_proj(x3, w_pool, w_ssm_t):
    nch = x3.shape[0]
    return pl.pallas_call(
        _in_proj_kernel,
        grid=(nch // C_TILE, CHUNK // S_TILE),
        in_specs=[
            pl.BlockSpec(memory_space=pl.ANY),
            pl.BlockSpec((D_MODEL, D_POOL), lambda c, s: (0, 0)),
            pl.BlockSpec((D_SSM, D_MODEL), lambda c, s: (0, 0)),
        ],
        out_specs=[
            pl.BlockSpec((S_TILE, C_TILE, D_POOL), lambda c, s: (s, c, 0)),
            pl.BlockSpec((N_SSM_GROUPS, None, S_TILE, SSM_GROUP, C_TILE), lambda c, s: (0, c, s, 0, 0)),
        ],
        out_shape=[
            jax.ShapeDtypeStruct((CHUNK, nch, D_POOL), BF16),
            jax.ShapeDtypeStruct((N_SSM_GROUPS, nch // C_TILE, CHUNK, SSM_GROUP, C_TILE), BF16),
        ],
        scratch_shapes=[
            pltpu.VMEM((2, S_TILE, C_TILE, D_MODEL), F32),
            pltpu.SemaphoreType.DMA((2,)),
        ],
        compiler_params=_params(("arbitrary", "arbitrary")),
        name="in_proj",
    )(x3, w_pool, w_ssm_t)


def _pool_kernel(z_ref, w_ref, sc_ref, o_ref):
    n_c = z_ref.shape[1]
    gi = pl.program_id(1)
    chunk = lax.broadcasted_iota(jnp.int32, (n_c, 1), 0)

    def plane(p):
        if p < 0:
            return jnp.where(chunk >= 1, pltpu.roll(z_ref[p + CHUNK].astype(F32), 1, 0), 0.0)
        if p >= CHUNK:
            return jnp.where(chunk < n_c - 1, pltpu.roll(z_ref[p - CHUNK].astype(F32), n_c - 1, 0), 0.0)
        return z_ref[p].astype(F32)

    def inv_count(s_t, half):
        n = n_c * CHUNK
        first = min(s_t + half, n) - max(s_t - half, 0)
        t_last = n - CHUNK + s_t
        last = min(t_last + half, n) - max(t_last - half, 0)
        inv = 1.0 / (2 * half)
        if first != 2 * half:
            inv = jnp.where(chunk == 0, 1.0 / first, inv)
        if last != 2 * half:
            inv = jnp.where(chunk == n_c - 1, 1.0 / last, inv)
        return inv

    for k, w in enumerate(POOL_WINDOWS):
        @pl.when(gi == k)
        def _(w=w):
            half = w // 2
            total = plane(-half)
            for p in range(-half + 1, half):
                total = total + plane(p)
            diffs = []
            for s_t in range(CHUNK):
                diffs.append((total * inv_count(s_t, half) - plane(s_t)).astype(BF16))
                if s_t + 1 < CHUNK:
                    total = total + plane(s_t + half) - plane(s_t - half)
            diff = jnp.concatenate(diffs, axis=0)
            out = _dot(diff, w_ref[...].astype(BF16)) * sc_ref[...]
            o_ref[...] = out.reshape(o_ref.shape).astype(o_ref.dtype)


def _pool(zp_t, pool_w, pool_scale, bsz):
    _, nch, _ = zp_t.shape
    n_c = nch // bsz
    return pl.pallas_call(
        _pool_kernel,
        grid=(bsz, len(POOL_WINDOWS)),
        in_specs=[
            pl.BlockSpec((CHUNK, n_c, POOL_GROUP), lambda i, g: (0, i, g)),
            pl.BlockSpec((None, POOL_GROUP, POOL_GROUP), lambda i, g: (g, 0, 0)),
            pl.BlockSpec((1, POOL_GROUP), lambda i, g: (0, g)),
        ],
        out_specs=pl.BlockSpec((CHUNK, n_c, POOL_GROUP), lambda i, g: (0, i, g)),
        out_shape=jax.ShapeDtypeStruct(zp_t.shape, BF16),
        compiler_params=_params(("parallel", "parallel")),
        name="pool",
    )(zp_t, pool_w, pool_scale)


def _expand_consts():
    time = np.arange(CHUNK_W) // SSM_GROUP
    def onehot(e):
        m = np.zeros((CHUNK_W, LANES), np.float32)
        m[np.arange(CHUNK_W), e] = 1.0
        return m
    return np.stack([
        onehot(CHUNK - 1 - time),
        onehot(time),
        onehot(time + 1),
        onehot(CHUNK - time),
    ])


def _ssm_kernel(u_ref, vec_ref, mat_ref, exp_ref, wg_ref, wu_ref, wd_ref, y_ref, og_ref, ou_ref,
                od_ref, *, seq_chunks):
    og_ref[...] = wg_ref[...].astype(BF16)
    ou_ref[...] = wu_ref[...].astype(BF16)
    od_ref[...] = wd_ref[...].astype(BF16)

    n_ct = u_ref.shape[0]
    nch = n_ct * C_TILE
    half = LANES // 2
    lane = lax.broadcasted_iota(jnp.int32, (1, LANES), 1)
    lo_half = lane < half

    def direction(di):
        a_re = vec_ref[di, 0:1]
        a_im = vec_ref[di, 1:2]
        dt = jnp.exp(vec_ref[di, 2:3])
        mag = jnp.exp(a_re * dt)
        ang = a_im * dt
        lb_re = mag * jnp.cos(ang)
        lb_im = mag * jnp.sin(ang)
        den = a_re * a_re + a_im * a_im
        f_re = ((lb_re - 1.0) * a_re + lb_im * a_im) / den
        f_im = (lb_im * a_re - (lb_re - 1.0) * a_im) / den
        return (lb_re, lb_im), f_re, f_im

    def power_table(lam):
        e = lax.broadcasted_iota(jnp.int32, (LANES, 1), 0)
        sq_re, sq_im = lam
        t_re = jnp.ones((LANES, LANES), F32)
        t_im = jnp.zeros((LANES, LANES), F32)
        for k in range(CHUNK.bit_length()):
            bit = ((e >> k) & 1) == 1
            t_re, t_im = (jnp.where(bit, t_re * sq_re - t_im * sq_im, t_re),
                          jnp.where(bit, t_re * sq_im + t_im * sq_re, t_im))
            sq_re, sq_im = sq_re * sq_re - sq_im * sq_im, 2.0 * sq_re * sq_im
        return jnp.where(lo_half, t_re, t_im), jnp.where(lo_half, t_im, t_re)

    def tile_rows(x16):
        return jnp.broadcast_to(x16[None], (CHUNK, SSM_GROUP, LANES)).reshape(CHUNK_W, LANES)

    def expanded(tab, which, v_re, v_im, conj_sign):
        lexp = _dot(exp_ref[which], jnp.concatenate(tab, axis=1).astype(BF16))
        if conj_sign > 0:
            p = jnp.where(lo_half, v_re, v_re)
            q = jnp.where(lo_half, -v_im, v_im)
        else:
            p = jnp.where(lo_half, v_re, -v_re)
            q = jnp.where(lo_half, -v_im, -v_im)
        return lexp[:, :LANES] * tile_rows(p) + lexp[:, LANES:] * tile_rows(q)

    lam_f, ff_re, ff_im = direction(0)
    lam_b, fb_re, fb_im = direction(1)
    tabs_f = power_table(lam_f)
    tabs_b = power_table(lam_b)
    tab_f, tab_b = tabs_f[0], tabs_b[0]

    def bbar(bt_re, bt_im, f_re, f_im):
        return bt_re * f_re - bt_im * f_im, bt_re * f_im + bt_im * f_re

    def mat(di, k):
        return mat_ref[di, k * SSM_GROUP:(k + 1) * SSM_GROUP, :]

    bf_re, bf_im = bbar(mat(0, 0), mat(0, 1), ff_re, ff_im)
    bb_re, bb_im = bbar(mat(1, 0), mat(1, 1), fb_re, fb_im)

    pb1 = expanded(tabs_f, 0, bf_re, bf_im, 1)
    pb2 = expanded(tabs_b, 1, bb_re, bb_im, 1)
    pb3 = expanded(tabs_b, 2, bb_re, bb_im, 1)
    ft_f = expanded(tabs_f, 2, mat(0, 3), mat(0, 4), -1)
    ft_b = expanded(tabs_b, 3, mat(1, 3), mat(1, 4), -1)

    row_w = lax.broadcasted_iota(jnp.int32, (CHUNK_W, 1), 0)
    last_blk = row_w >= CHUNK_W - SSM_GROUP
    pb2_lag0 = jnp.where(last_blk, pltpu.roll(pb2, CHUNK_W - SSM_GROUP, 0), 0.0)
    ccr_f = mat(0, 2).astype(BF16)
    ccr_b = mat(1, 2).astype(BF16)
    r_lo = _dot_nt(ccr_f, pb1.astype(BF16)) + _dot_nt(ccr_b, pb2_lag0.astype(BF16))
    co = lax.broadcasted_iota(jnp.int32, (SSM_GROUP, CHUNK_W), 0)
    col = lax.broadcasted_iota(jnp.int32, (SSM_GROUP, CHUNK_W), 1)
    r_lo = r_lo + jnp.where(col == CHUNK_W - SSM_GROUP + co, mat(0, 5)[:, 0:1], 0.0)
    r_hi = _dot_nt(ccr_b, pb3.astype(BF16))
    r_t = jnp.concatenate([r_lo, r_hi], axis=1)
    g_t = jnp.concatenate(
        [pltpu.roll(r_t, SSM_GROUP * (tl + 1), 1) for tl in range(T_SUB)], axis=0
    ).astype(BF16)

    u = jnp.concatenate([u_ref[ct].reshape(CHUNK_W, C_TILE) for ct in range(n_ct)],
                        axis=1)
    e_mat = jnp.concatenate([pb1, pb2], axis=1).astype(BF16)
    xend = _dot_tn(e_mat, u)
    lanec = lax.broadcasted_iota(jnp.int32, (1, nch), 1) % seq_chunks
    ns = SSM_STATE

    def scan(re, im, tab, forward):
        lam_col = jnp.transpose(tab[CHUNK:CHUNK + 8, :])[:, 0:1]
        a, b = lam_col[:ns], lam_col[ns:]
        n_steps = int(math.log2(seq_chunks))

        def rolled(v, d):
            if d % LANES == 0:
                return jnp.concatenate([v[:, nch - d:], v[:, :nch - d]], axis=1)
            return pltpu.roll(v, d, 1)

        def shifted(v, d):
            if forward:
                return jnp.where(lanec >= d, rolled(v, d), 0.0)
            return jnp.where(lanec < seq_chunks - d, rolled(v, nch - d), 0.0)

        for k in range(n_steps):
            sr, si = shifted(re, 1 << k), shifted(im, 1 << k)
            re, im = re + (sr * a - si * b), im + (sr * b + si * a)
            a, b = a * a - b * b, 2.0 * a * b
        return shifted(re, 1), shifted(im, 1)

    f_re, f_im = scan(xend[:ns], xend[ns:2 * ns], tab_f, True)
    b_re, b_im = scan(xend[2 * ns:3 * ns], xend[3 * ns:], tab_b, False)
    xin = jnp.concatenate([f_re, f_im, b_re, b_im], axis=0).astype(BF16)
    f_t = jnp.concatenate([ft_f, ft_b], axis=1).astype(BF16)

    toeplitz = jnp.concatenate(
        [g_t[:, CHUNK_W - LANES * th:2 * CHUNK_W - LANES * th] for th in range(CHUNK // T_SUB)],
        axis=0)
    y_t = _dot(toeplitz, u) + _dot(f_t, xin)
    for ct in range(n_ct):
        y_ref[ct] = y_t[:, ct * C_TILE:(ct + 1) * C_TILE].reshape(
            CHUNK, SSM_GROUP, C_TILE).astype(y_ref.dtype)


def _ssm(ut, a_re, a_im, log_dt, b_re, b_im, c_re, c_im, d, seq_chunks, w_gate, w_up, w_down, layer):
    g, n_ct = ut.shape[:2]
    assert N_EXPERTS % g == 0
    epg = N_EXPERTS // g
    base = layer * g
    n = SSM_STATE

    def per_group(a):
        return jnp.swapaxes(a, 0, 1)

    def dup(a):
        return jnp.concatenate([a, a], axis=-1)

    vecs = dup(jnp.stack([per_group(a_re), per_group(a_im),
                          jnp.broadcast_to(per_group(log_dt)[..., None], (g, 2, n))], axis=2))
    cr, ci = per_group(c_re), per_group(c_im)
    d_blk = jnp.broadcast_to(d.reshape(g, 1, SSM_GROUP, 1), (g, 2, SSM_GROUP, LANES))
    mats = jnp.concatenate(
        [dup(jnp.swapaxes(per_group(b_re), 2, 3)), dup(jnp.swapaxes(per_group(b_im), 2, 3)),
         jnp.concatenate([cr, -ci], axis=-1), dup(cr), dup(ci), d_blk], axis=2)
    exp_c = jnp.asarray(_expand_consts(), BF16)

    return pl.pallas_call(
        functools.partial(_ssm_kernel, seq_chunks=seq_chunks),
        grid=(g,),
        in_specs=[
            pl.BlockSpec((None, n_ct, CHUNK, SSM_GROUP, C_TILE), lambda i: (i, 0, 0, 0, 0)),
            pl.BlockSpec((None, 2, 3, LANES), lambda i: (i, 0, 0, 0)),
            pl.BlockSpec((None, 2, 6 * SSM_GROUP, LANES), lambda i: (i, 0, 0, 0)),
            pl.BlockSpec((4, CHUNK_W, LANES), lambda i: (0, 0, 0)),
            pl.BlockSpec((epg, D_MODEL, D_EXPERT), lambda i: (base + i, 0, 0)),
            pl.BlockSpec((epg, D_MODEL, D_EXPERT), lambda i: (base + i, 0, 0)),
            pl.BlockSpec((epg, D_EXPERT, D_MODEL), lambda i: (base + i, 0, 0)),
        ],
        out_specs=[
            pl.BlockSpec((None, n_ct, CHUNK, SSM_GROUP, C_TILE), lambda i: (i, 0, 0, 0, 0)),
            pl.BlockSpec((epg, D_MODEL, D_EXPERT), lambda i: (i, 0, 0)),
            pl.BlockSpec((epg, D_MODEL, D_EXPERT), lambda i: (i, 0, 0)),
            pl.BlockSpec((epg, D_EXPERT, D_MODEL), lambda i: (i, 0, 0)),
        ],
        out_shape=[
            jax.ShapeDtypeStruct(ut.shape, BF16),
            jax.ShapeDtypeStruct((N_EXPERTS, D_MODEL, D_EXPERT), BF16),
            jax.ShapeDtypeStruct((N_EXPERTS, D_MODEL, D_EXPERT), BF16),
            jax.ShapeDtypeStruct((N_EXPERTS, D_EXPERT, D_MODEL), BF16),
        ],
        compiler_params=_params(("parallel",)),
        name="ssm",
    )(ut, vecs, mats, exp_c, w_gate, w_up, w_down)


def _mix_out_kernel(x_hbm, a_ref, yt_ref, gwt_ref, gb_ref, wo_ref, h_hbm, xbuf, hbuf, xsem, hsem):
    nc = C_TILE
    x = _load_time_major_rows(x_hbm, xbuf, xsem)
    a = a_ref[...].reshape(S_TILE * nc, D_POOL)
    y = jnp.concatenate([yt_ref[:, j, :, :].reshape(D_SSM, nc) for j in range(S_TILE)],
                        axis=1).astype(F32)
    z = 0.5 * y * (1.0 + jnp.tanh(math.sqrt(2.0 / math.pi) * (y + 0.044715 * (y * y * y))))
    gate = _sigmoid(_dot(gwt_ref[...], z.astype(BF16)) + gb_ref[...])
    s = (z * gate).astype(BF16)
    h = x + _dot(a, wo_ref[:D_POOL, :]) + _dot_tn(s, wo_ref[D_POOL:, :])
    _store_time_major_rows(h, h_hbm, hbuf, hsem)


def _mix_out(x3, a3, yt, glu_w_t, glu_b_col, w_out):
    nch = x3.shape[0]
    return pl.pallas_call(
        _mix_out_kernel,
        grid=(nch // C_TILE, CHUNK // S_TILE),
        in_specs=[
            pl.BlockSpec(memory_space=pl.ANY),
            pl.BlockSpec((S_TILE, C_TILE, D_POOL), lambda c, t: (t, c, 0)),
            pl.BlockSpec((N_SSM_GROUPS, None, S_TILE, SSM_GROUP, C_TILE), lambda c, t: (0, c, t, 0, 0)),
            pl.BlockSpec((D_SSM, D_SSM), lambda c, t: (0, 0)),
            pl.BlockSpec((D_SSM, 1), lambda c, t: (0, 0)),
            pl.BlockSpec((D_MODEL, D_MODEL), lambda c, t: (0, 0)),
        ],
        out_specs=pl.BlockSpec(memory_space=pl.ANY),
        out_shape=jax.ShapeDtypeStruct((nch, CHUNK, D_MODEL), F32),
        scratch_shapes=[
            pltpu.VMEM((2, S_TILE, C_TILE, D_MODEL), F32),
            pltpu.VMEM((2, S_TILE, C_TILE, D_MODEL), F32),
            pltpu.SemaphoreType.DMA((2,)),
            pltpu.SemaphoreType.DMA((2,)),
        ],
        compiler_params=_params(("arbitrary", "arbitrary")),
        name="mix_out",
    )(x3, a3, yt, glu_w_t, glu_b_col, w_out)


def _pack_rows(x):
    b = lax.bitcast_convert_type(x.astype(BF16).astype(F32), U32)
    return (b[:, :HALF] & jnp.uint32(0xFFFF0000)) | (b[:, HALF:] >> 16)


def _unpack_rows(w):
    lo = lax.bitcast_convert_type(w & jnp.uint32(0xFFFF0000), F32)
    hi = lax.bitcast_convert_type(w << 16, F32)
    return lo, hi


def _split_bf16(x):
    hi = x.astype(BF16)
    return hi, (x - hi.astype(F32)).astype(BF16)


def _route(v32, wr_ref, br_ref):
    v_hi, v_lo = _split_bf16(v32)
    w_hi, w_lo = _split_bf16(wr_ref[...])
    both = _dot(v_hi, jnp.concatenate([w_hi, w_lo], axis=1))
    logits = both[:, :ROUTER_W] + (both[:, ROUTER_W:] + _dot(v_lo, w_hi)) + br_ref[...]
    return jnp.transpose(logits)


def _top1(x, valid=None):
    n = x.shape[0]
    row = lax.broadcasted_iota(jnp.int32, x.shape, 0).astype(F32)
    if valid is not None:
        x = jnp.where(valid, x, -jnp.inf)
    m = jnp.max(x, axis=0, keepdims=True)
    idx = jnp.min(jnp.where(x == m, row, float(n)), axis=0, keepdims=True)
    return m, idx, x, row


def _split_planes(packed, ref):
    ref[0] = packed[:, :SC_ROW]
    ref[1] = packed[:, SC_ROW:]


def _router_kernel(h_ref, g_ref, wr_ref, br_ref, before_ref, vp_ref, meta_t_ref, cnt_ref, carry_ref):
    @pl.when(pl.program_id(0) == 0)
    def _():
        carry_ref[...] = jnp.zeros_like(carry_ref)

    v32 = _rms(h_ref[...], g_ref[...])
    _split_planes(_pack_rows(v32), vp_ref)
    lt = _route(v32, wr_ref, br_ref)
    tm = lt.shape[1]
    eg = EXPERTS_PER_GROUP

    grp = lt[:eg]
    grp_row = lax.broadcasted_iota(jnp.int32, grp.shape, 0)
    mg, grp_idx, grp, _ = _top1(grp, grp_row < N_EXPERT_GROUPS)
    grp_p = 1.0 / jnp.sum(jnp.exp(grp - mg), axis=0, keepdims=True)
    le = jnp.zeros((eg, tm), F32)
    for g in range(N_EXPERT_GROUPS):
        le = jnp.where(grp_idx == float(g), lt[eg * (g + 1):eg * (g + 2)], le)
    m1, i1, le, row = _top1(le)
    z = jnp.sum(jnp.exp(le - m1), axis=0, keepdims=True)
    m2, i2, _, _ = _top1(jnp.where(row == i1, -jnp.inf, le))
    p1 = 1.0 / z
    p2 = jnp.exp(m2 - m1) / z
    tot = p1 + p2
    w1 = grp_p * (p1 / tot)
    w2 = grp_p * (p2 / tot)
    e1 = grp_idx * eg + i1
    e2 = grp_idx * eg + i2

    erow = lax.broadcasted_iota(jnp.int32, (N_EXPERTS, tm), 0).astype(F32)
    onehot = jnp.where(erow == e1, 1.0, jnp.where(erow == e2, 1.0, 0.0))
    before = _dot(onehot.astype(BF16), before_ref[...]) + carry_ref[...]
    rank1 = jnp.sum(jnp.where(erow == e1, before, 0.0), axis=0, keepdims=True)
    rank2 = jnp.sum(jnp.where(erow == e2, before, 0.0), axis=0, keepdims=True)
    carry = carry_ref[...] + jnp.sum(onehot, axis=1, keepdims=True)
    carry_ref[...] = carry
    cnt_ref[...] = carry

    mrow = lax.broadcasted_iota(jnp.int32, (META_ROWS, tm), 0)
    meta_t_ref[...] = jnp.where(mrow == 0, e1, jnp.where(mrow == 1, e2, jnp.where(
        mrow == 2, rank1, jnp.where(mrow == 3, rank2, jnp.where(
            mrow == 4, w1, jnp.where(mrow == 5, w2, 0.0))))))


def _earlier_matrix(tm):
    return np.triu(np.ones((tm, tm), np.float32), k=1)


def _router(h1, g_ffn, w_router, b_router, tm):
    t = h1.shape[0]
    return pl.pallas_call(
        _router_kernel,
        grid=(t // tm,),
        in_specs=[
            pl.BlockSpec((tm, D_MODEL), lambda i: (i, 0)),
            pl.BlockSpec((1, D_MODEL), lambda i: (0, 0)),
            pl.BlockSpec((D_MODEL, ROUTER_W), lambda i: (0, 0)),
            pl.BlockSpec((1, ROUTER_W), lambda i: (0, 0)),
            pl.BlockSpec((tm, tm), lambda i: (0, 0)),
        ],
        out_specs=[
            pl.BlockSpec((2, tm, SC_ROW), lambda i: (0, i, 0)),
            pl.BlockSpec((META_ROWS, tm), lambda i: (0, i)),
            pl.BlockSpec((N_EXPERTS, 1), lambda i: (0, 0)),
        ],
        out_shape=[
            jax.ShapeDtypeStruct((2, t, SC_ROW), U32),
            jax.ShapeDtypeStruct((META_ROWS, t), F32),
            jax.ShapeDtypeStruct((N_EXPERTS, 1), F32),
        ],
        scratch_shapes=[pltpu.VMEM((N_EXPERTS, 1), F32)],
        compiler_params=_params(("arbitrary",)),
        name="router",
    )(h1, g_ffn, w_router, b_router, jnp.asarray(_earlier_matrix(tm), BF16))


def _plan(meta_t, counts, n_tiles):
    e1 = meta_t[0].astype(jnp.int32)
    e2 = meta_t[1].astype(jnp.int32)
    rank1 = meta_t[2].astype(jnp.int32)
    rank2 = meta_t[3].astype(jnp.int32)
    cnt = counts[:, 0].astype(jnp.int32)
    padded = ((cnt + ROW_TILE - 1) // ROW_TILE) * ROW_TILE
    ends = jnp.cumsum(padded)
    starts = ends - padded
    experts = jnp.arange(N_EXPERTS, dtype=jnp.int32)
    pos1 = rank1 + jnp.sum(jnp.where(e1[None, :] == experts[:, None], starts[:, None], 0), axis=0)
    pos2 = rank2 + jnp.sum(jnp.where(e2[None, :] == experts[:, None], starts[:, None], 0), axis=0)
    tile_start = jnp.arange(n_tiles, dtype=jnp.int32) * ROW_TILE
    tile_expert = jnp.sum((tile_start[:, None] >= ends[None, :]).astype(jnp.int32), axis=1)
    tile_expert = jnp.minimum(tile_expert, N_EXPERTS - 1)
    rows_left = jnp.sum(jnp.where(tile_expert[:, None] == experts, cnt + starts, 0), axis=1) - tile_start
    n_valid = jnp.clip(rows_left, 0, ROW_TILE).astype(jnp.int32)
    last_used = jnp.maximum(ends[-1] // ROW_TILE - 1, 0)
    block = jnp.minimum(jnp.arange(n_tiles, dtype=jnp.int32), last_used)
    tile_expert = jnp.sum(jnp.where(block[:, None] == jnp.arange(n_tiles)[None, :],
                                    tile_expert[None, :], 0), axis=1)
    plane = n_tiles * ROW_TILE
    half_rows = jnp.concatenate([pos1, pos1 + plane, pos2, pos2 + plane])[None]
    return half_rows, tile_expert, n_valid


def _sc_mesh():
    return plsc.VectorSubcoreMesh(core_axis_name="c", subcore_axis_name="s")


def _sc_scatter_rows(rows, idx, n_out):
    t, width = rows.shape
    steps = t // SC_WINDOW
    half = steps // 2

    @pl.kernel(out_type=jax.ShapeDtypeStruct((n_out, width), rows.dtype), mesh=_sc_mesh(),
               scratch_types=[], name="moe_scatter")
    def scatter(rows_hbm, idx_hbm, out_hbm):
        def body(rows_vmem, idx0_vmem, idx1_vmem):
            pltpu.sync_copy(rows_vmem, out_hbm.at[idx0_vmem.at[0]])
            pltpu.sync_copy(rows_vmem, out_hbm.at[idx1_vmem.at[0]])

        pltpu.emit_pipeline(
            body,
            grid=(2, half),
            in_specs=[pl.BlockSpec((SC_WINDOW, width), lambda c, j: (c * half + j, 0)),
                      pl.BlockSpec((1, SC_WINDOW), lambda c, j: (0, c * half + j)),
                      pl.BlockSpec((1, SC_WINDOW), lambda c, j: (0, steps + c * half + j))],
            out_specs=[],
            core_axis_name=("c", "s"),
            dimension_semantics=(pltpu.PARALLEL, pltpu.PARALLEL),
        )(rows_hbm, idx_hbm, idx_hbm)

    return scatter(rows, idx)


def _sc_gather_rows(table, idx):
    m = idx.shape[1]
    width = table.shape[1]
    steps = m // (2 * SC_WINDOW)

    @pl.kernel(out_type=jax.ShapeDtypeStruct((m, width), table.dtype), mesh=_sc_mesh(),
               scratch_types=[], name="moe_gather")
    def gather(table_hbm, idx_hbm, out_hbm):
        def body(idx_vmem, out_vmem):
            pltpu.sync_copy(table_hbm.at[idx_vmem.at[0]], out_vmem)

        pltpu.emit_pipeline(
            body,
            grid=(2, steps),
            in_specs=[pl.BlockSpec((1, SC_WINDOW), lambda k, j: (0, k * steps + j))],
            out_specs=[pl.BlockSpec((SC_WINDOW, width), lambda k, j: (k * steps + j, 0))],
            core_axis_name=("c", "s"),
            dimension_semantics=(pltpu.PARALLEL, pltpu.PARALLEL),
        )(idx_hbm, out_hbm)

    return gather(table, idx)


def _tile_row_copies(nv_ref, hbm, buf, sem, tile, slot, to_hbm):
    n_blocks = (nv_ref[tile] + ROW_BLOCK - 1) // ROW_BLOCK
    out = []
    for j in range(ROW_TILE // ROW_BLOCK):
        far = hbm.at[:, pl.ds(tile * ROW_TILE + j * ROW_BLOCK, ROW_BLOCK), :]
        near = buf.at[slot, :, pl.ds(j * ROW_BLOCK, ROW_BLOCK), :]
        src, dst = (near, far) if to_hbm else (far, near)
        out.append((pltpu.make_async_copy(src, dst, sem.at[slot]), j < n_blocks))
    return out


def _start_tile_rows(*args):
    for cp, wanted in _tile_row_copies(*args):
        @pl.when(wanted)
        def _(cp=cp):
            cp.start()


def _wait_tile_rows(*args):
    for cp, wanted in _tile_row_copies(*args):
        @pl.when(wanted)
        def _(cp=cp):
            cp.wait()


def _experts_kernel(te_ref, nv_ref, xs_hbm, wg_ref, wu_ref, wd_ref, ys_hbm, xbuf, ybuf, xsem, ysem):
    r = pl.program_id(0)
    n_tiles = pl.num_programs(0)
    slot = r % 2
    n_valid = nv_ref[r]

    @pl.when(r == 0)
    def _():
        _start_tile_rows(nv_ref, xs_hbm, xbuf, xsem, 0, 0, False)

    @pl.when(r + 1 < n_tiles)
    def _():
        _start_tile_rows(nv_ref, xs_hbm, xbuf, xsem, r + 1, 1 - slot, False)

    _wait_tile_rows(nv_ref, xs_hbm, xbuf, xsem, r, slot, False)

    @pl.when(r >= 2)
    def _():
        _wait_tile_rows(nv_ref, ys_hbm, ybuf, ysem, r - 2, slot, True)

    @pl.when(n_valid > 0)
    def _():
        parts = [p.astype(BF16) for p in _unpack_rows(xbuf[slot, 0]) + _unpack_rows(xbuf[slot, 1])]
        cols = (0, 2 * SC_ROW, SC_ROW, 3 * SC_ROW)
        hg = sum(_dot(p, wg_ref[c:c + SC_ROW, :]) for p, c in zip(parts, cols))
        hu = sum(_dot(p, wu_ref[c:c + SC_ROW, :]) for p, c in zip(parts, cols))
        row = lax.broadcasted_iota(jnp.int32, (ROW_TILE, 1), 0)
        hid = jnp.where(row < n_valid, hg * _sigmoid(hg) * hu, 0.0).astype(BF16)
        _split_planes(_pack_rows(_dot(hid, wd_ref[...])), ybuf.at[slot])

    _start_tile_rows(nv_ref, ys_hbm, ybuf, ysem, r, slot, True)

    @pl.when(r == n_tiles - 1)
    def _():
        _wait_tile_rows(nv_ref, ys_hbm, ybuf, ysem, r, slot, True)

        @pl.when(r >= 1)
        def _():
            _wait_tile_rows(nv_ref, ys_hbm, ybuf, ysem, r - 1, 1 - slot, True)


def _experts(xs, tile_expert, n_valid, w_gate, w_up, w_down):
    n_tiles = xs.shape[1] // ROW_TILE
    w_spec = pl.BlockSpec((None, D_MODEL, D_EXPERT), lambda r, te, nv: (te[r], 0, 0))
    grid_spec = pltpu.PrefetchScalarGridSpec(
        num_scalar_prefetch=2,
        grid=(n_tiles,),
        in_specs=[
            pl.BlockSpec(memory_space=pl.ANY),
            w_spec, w_spec,
            pl.BlockSpec((None, D_EXPERT, D_MODEL), lambda r, te, nv: (te[r], 0, 0)),
        ],
        out_specs=pl.BlockSpec(memory_space=pl.ANY),
        scratch_shapes=[
            pltpu.VMEM((2, 2, ROW_TILE, SC_ROW), U32),
            pltpu.VMEM((2, 2, ROW_TILE, SC_ROW), U32),
            pltpu.SemaphoreType.DMA((2,)),
            pltpu.SemaphoreType.DMA((2,)),
        ],
    )
    return pl.pallas_call(
        _experts_kernel,
        grid_spec=grid_spec,
        out_shape=jax.ShapeDtypeStruct((2, n_tiles * ROW_TILE, SC_ROW), U32),
        compiler_params=_params(("arbitrary",)),
        name="experts",
    )(tile_expert, n_valid, xs, w_gate, w_up, w_down)


def _ple_kernel(h_ref, yg_ref, meta_t_ref, p_ref, wg_ref, bg_ref, wp_ref, gf_ref, *rest, final_norm):
    o_ref = rest[-1]
    meta = jnp.transpose(meta_t_ref[...])
    w1 = meta[:, 4:5]
    w2 = meta[:, 5:6]
    q0, q2 = (w1 * u + w2 * v for u, v in zip(_unpack_rows(yg_ref[0]), _unpack_rows(yg_ref[2])))
    q1, q3 = (w1 * u + w2 * v for u, v in zip(_unpack_rows(yg_ref[1]), _unpack_rows(yg_ref[3])))
    moe = jnp.concatenate([q0, q1, q2, q3], axis=1)
    h = h_ref[...] + moe
    gate = _sigmoid(_dot(_rms(h).astype(BF16), wg_ref[...]) + bg_ref[...])
    h = h + gate * _dot(p_ref[...].astype(BF16), wp_ref[...])
    o_ref[...] = _rms(h, gf_ref[...]) if final_norm else h


def _ple(h1, yg, meta_t, p2, w_gate, b_gate, w_proj, g_final, final_norm, tm, off, prev_out):
    t = h1.shape[0]
    steps = yg.shape[1] // tm
    in_specs = [
        pl.BlockSpec((tm, D_MODEL), lambda i: (i + off, 0)),
        pl.BlockSpec((4, tm, SC_ROW), lambda i: (0, i, 0)),
        pl.BlockSpec((META_ROWS, tm), lambda i: (0, i + off)),
        pl.BlockSpec((tm, D_PLE), lambda i: (i + off, 0)),
        pl.BlockSpec((D_MODEL, D_MODEL), lambda i: (0, 0)),
        pl.BlockSpec((1, D_MODEL), lambda i: (0, 0)),
        pl.BlockSpec((D_PLE, D_MODEL), lambda i: (0, 0)),
        pl.BlockSpec((1, D_MODEL), lambda i: (0, 0)),
    ]
    args = [h1, yg, meta_t, p2, w_gate, b_gate, w_proj, g_final]
    aliases = {}
    if prev_out is not None:
        in_specs.append(pl.BlockSpec(memory_space=pl.ANY))
        args.append(prev_out)
        aliases = {len(args) - 1: 0}
    return pl.pallas_call(
        functools.partial(_ple_kernel, final_norm=final_norm),
        grid=(steps,),
        in_specs=in_specs,
        out_specs=pl.BlockSpec((tm, D_MODEL), lambda i: (i + off, 0)),
        out_shape=jax.ShapeDtypeStruct((t, D_MODEL), F32),
        input_output_aliases=aliases,
        compiler_params=_params(("parallel",)),
        name="ple",
    )(*args)


def kernel(x, p, g_mix, w_in, pool_w, pool_scale, ssm_a_re, ssm_a_im, ssm_log_dt, ssm_b_re,
           ssm_b_im, ssm_c_re, ssm_c_im, ssm_d, glu_w, glu_b, w_out, g_ffn, router_grp_w,
           router_grp_b, router_exp_w, router_exp_b, exp_w_gate, exp_w_up, exp_w_down, g_ple,
           ple_gate_w, ple_gate_b, ple_proj_w, g_final):
    bsz, seq, dm = x.shape
    depth = g_mix.shape[0]
    t = bsz * seq
    seq_chunks = seq // CHUNK
    nch = t // CHUNK
    tm = 1024
    n_sorted = (pl.cdiv(2 * t, ROW_TILE) + N_EXPERTS) * ROW_TILE
    w_gate_all = exp_w_gate.reshape(depth * N_EXPERTS, dm, D_EXPERT)
    w_up_all = exp_w_up.reshape(depth * N_EXPERTS, dm, D_EXPERT)
    w_down_all = exp_w_down.reshape(depth * N_EXPERTS, D_EXPERT, dm)

    h = x.reshape(t, dm)
    for i in range(depth):
        w_in_b = (g_mix[i][:, None] * w_in[i]).astype(BF16)
        zp, ut = _in_proj(h.reshape(nch, CHUNK, dm), w_in_b[:, :D_POOL],
                          jnp.transpose(w_in_b[:, D_POOL:]))
        a = _pool(zp, pool_w[i], pool_scale[i][None], bsz)
        yt, w_gate_b, w_up_b, w_down_b = _ssm(
            ut, ssm_a_re[i], ssm_a_im[i], ssm_log_dt[i], ssm_b_re[i], ssm_b_im[i], ssm_c_re[i],
            ssm_c_im[i], ssm_d[i], seq_chunks, w_gate_all, w_up_all, w_down_all, i)
        h = _mix_out(h.reshape(nch, CHUNK, dm), a, yt,
                     jnp.transpose(glu_w[i]).astype(BF16), glu_b[i][:, None],
                     w_out[i].astype(BF16)).reshape(t, dm)

        eg = EXPERTS_PER_GROUP
        w_router = jnp.concatenate(
            [router_grp_w[i], jnp.zeros((dm, eg - N_EXPERT_GROUPS), F32),
             jnp.transpose(router_exp_w[i], (1, 0, 2)).reshape(dm, N_EXPERTS),
             jnp.zeros((dm, ROUTER_W - eg - N_EXPERTS), F32)], axis=1)
        b_router = jnp.concatenate(
            [router_grp_b[i], jnp.zeros((eg - N_EXPERT_GROUPS,), F32),
             router_exp_b[i].reshape(N_EXPERTS),
             jnp.zeros((ROUTER_W - eg - N_EXPERTS,), F32)])[None]
        vp, meta_t, counts = _router(h, g_ffn[i][None], w_router, b_router, tm)
        idx, tile_expert, n_valid = _plan(meta_t, counts, n_sorted // ROW_TILE)
        xs = _sc_scatter_rows(vp.reshape(2 * t, SC_ROW), idx, 2 * n_sorted)
        ys = _experts(xs.reshape(2, n_sorted, SC_ROW), tile_expert, n_valid,
                      w_gate_b, w_up_b, w_down_b)
        ys2 = ys.reshape(2 * n_sorted, SC_ROW)
        idx4 = idx.reshape(4, t)
        ple_wg = (g_ple[i][:, None] * ple_gate_w[i]).astype(BF16)
        ple_wp = ple_proj_w[i].astype(BF16)
        out = None
        cuts = (0, t // FIRST_PART, t)
        for lo, hi in zip(cuts[:-1], cuts[1:]):
            yg_q = _sc_gather_rows(ys2, idx4[:, lo:hi].reshape(1, 4 * (hi - lo)))
            out = _ple(h, yg_q.reshape(4, hi - lo, SC_ROW), meta_t, p[i].reshape(t, D_PLE), ple_wg,
                       ple_gate_b[i][None], ple_wp, g_final[None], i == depth - 1, tm, lo // tm, out)
        h = out
    return h.reshape(bsz, seq, dm)
```

```python
import functools
import math

import numpy as np
import jax
import jax.numpy as jnp
from jax import lax
from jax.experimental import pallas as pl
from jax.experimental.pallas import tpu as pltpu
from jax.experimental.pallas import tpu_sc as plsc

F32 = jnp.float32
BF16 = jnp.bfloat16
U32 = jnp.uint32

D_MODEL = 1024
D_POOL = 512
D_SSM = 512
POOL_WINDOWS = (2, 4, 8, 16)
POOL_GROUP = 128
SSM_GROUP = 16
N_SSM_GROUPS = 32
SSM_STATE = 64
N_EXPERT_GROUPS = 4
EXPERTS_PER_GROUP = 8
N_EXPERTS = N_EXPERT_GROUPS * EXPERTS_PER_GROUP
D_EXPERT = 256
D_PLE = 256
RMS_EPS = 1e-6

LANES = 128
CHUNK = 32
CHUNK_W = CHUNK * SSM_GROUP
T_SUB = 8
S_TILE = 8
C_TILE = 128
ROUTER_W = LANES
HALF = D_MODEL // 2
ROW_TILE = 1280
ROW_BLOCK = 256
SC_WINDOW = 128
SC_ROW = HALF // 2
META_ROWS = 8
FIRST_PART = 4
VMEM_LIMIT = 56 * 1024 * 1024


def _dot(a, b):
    return jnp.dot(a, b, preferred_element_type=F32)


def _dot_nt(a, b):
    return lax.dot_general(a, b, (((1,), (1,)), ((), ())), preferred_element_type=F32)


def _dot_tn(a, b):
    return lax.dot_general(a, b, (((0,), (0,)), ((), ())), preferred_element_type=F32)


def _rms(x, g=None):
    y = x * lax.rsqrt(jnp.mean(x * x, axis=-1, keepdims=True) + RMS_EPS)
    return y if g is None else y * g


def _sigmoid(x):
    return 1.0 / (1.0 + jnp.exp(-x))


def _params(sem):
    return pltpu.CompilerParams(dimension_semantics=sem, vmem_limit_bytes=VMEM_LIMIT)


def _tile_step():
    n_s = pl.num_programs(1)
    return pl.program_id(0) * n_s + pl.program_id(1), pl.num_programs(0) * n_s


def _row_copies(hbm, buf, sem, step_idx, slot_idx, to_hbm):
    n_s = pl.num_programs(1)
    c0 = (step_idx // n_s) * C_TILE
    s0 = (step_idx % n_s) * S_TILE
    out = []
    for j in range(S_TILE):
        far, near = hbm.at[pl.ds(c0, C_TILE), s0 + j, :], buf.at[slot_idx, j]
        src, dst = (near, far) if to_hbm else (far, near)
        out.append(pltpu.make_async_copy(src, dst, sem.at[slot_idx]))
    return out


def _load_time_major_rows(hbm, buf, sem):
    step, n_steps = _tile_step()
    slot = step % 2

    @pl.when(step == 0)
    def _():
        for cp in _row_copies(hbm, buf, sem, step, slot, False):
            cp.start()

    @pl.when(step + 1 < n_steps)
    def _():
        for cp in _row_copies(hbm, buf, sem, step + 1, 1 - slot, False):
            cp.start()

    for cp in _row_copies(hbm, buf, sem, step, slot, False):
        cp.wait()
    return buf[slot].reshape(S_TILE * C_TILE, buf.shape[-1])


def _store_time_major_rows(val, hbm, buf, sem):
    step, n_steps = _tile_step()
    slot = step % 2

    @pl.when(step >= 2)
    def _():
        for cp in _row_copies(hbm, buf, sem, step - 2, slot, True):
            cp.wait()

    buf[slot] = val.reshape(S_TILE, C_TILE, val.shape[-1])
    for cp in _row_copies(hbm, buf, sem, step, slot, True):
        cp.start()

    @pl.when(step == n_steps - 1)
    def _():
        @pl.when(step >= 1)
        def _():
            for cp in _row_copies(hbm, buf, sem, step - 1, 1 - slot, True):
                cp.wait()
        for cp in _row_copies(hbm, buf, sem, step, slot, True):
            cp.wait()


def _in_proj_kernel(x_hbm, wp_ref, wst_ref, zp_ref, ut_ref, xbuf, xsem):
    nc = C_TILE
    u = _rms(_load_time_major_rows(x_hbm, xbuf, xsem)).astype(BF16)
    zp_ref[...] = _dot(u, wp_ref[...]).reshape(zp_ref.shape).astype(zp_ref.dtype)
    zt = _dot_nt(wst_ref[...], u).astype(BF16)
    for j in range(S_TILE):
        ut_ref[:, j, :, :] = zt[:, j * nc:(j + 1) * nc].reshape(N_SSM_GROUPS, SSM_GROUP, nc)


def _in_proj(x3, w_pool, w_ssm_t):
    nch = x3.shape[0]
    return pl.pallas_call(
        _in_proj_kernel,
        grid=(nch // C_TILE, CHUNK // S_TILE),
        in_specs=[
            pl.BlockSpec(memory_space=pl.ANY),
            pl.BlockSpec((D_MODEL, D_POOL), lambda c, s: (0, 0)),
            pl.BlockSpec((D_SSM, D_MODEL), lambda c, s: (0, 0)),
        ],
        out_specs=[
            pl.BlockSpec((S_TILE, C_TILE, D_POOL), lambda c, s: (s, c, 0)),
            pl.BlockSpec((N_SSM_GROUPS, None, S_TILE, SSM_GROUP, C_TILE), lambda c, s: (0, c, s, 0, 0)),
        ],
        out_shape=[
            jax.ShapeDtypeStruct((CHUNK, nch, D_POOL), BF16),
            jax.ShapeDtypeStruct((N_SSM_GROUPS, nch // C_TILE, CHUNK, SSM_GROUP, C_TILE), BF16),
        ],
        scratch_shapes=[
            pltpu.VMEM((2, S_TILE, C_TILE, D_MODEL), F32),
            pltpu.SemaphoreType.DMA((2,)),
        ],
        compiler_params=_params(("arbitrary", "arbitrary")),
        name="in_proj",
    )(x3, w_pool, w_ssm_t)


def _pool_kernel(z_ref, w_ref, sc_ref, o_ref):
    n_c = z_ref.shape[1]
    gi = pl.program_id(1)
    chunk = lax.broadcasted_iota(jnp.int32, (n_c, 1), 0)

    def plane(p):
        if p < 0:
            return jnp.where(chunk >= 1, pltpu.roll(z_ref[p + CHUNK].astype(F32), 1, 0), 0.0)
        if p >= CHUNK:
            return jnp.where(chunk < n_c - 1, pltpu.roll(z_ref[p - CHUNK].astype(F32), n_c - 1, 0), 0.0)
        return z_ref[p].astype(F32)

    def inv_count(s_t, half):
        n = n_c * CHUNK
        first = min(s_t + half, n) - max(s_t - half, 0)
        t_last = n - CHUNK + s_t
        last = min(t_last + half, n) - max(t_last - half, 0)
        inv = 1.0 / (2 * half)
        if first != 2 * half:
            inv = jnp.where(chunk == 0, 1.0 / first, inv)
        if last != 2 * half:
            inv = jnp.where(chunk == n_c - 1, 1.0 / last, inv)
        return inv

    for k, w in enumerate(POOL_WINDOWS):
        @pl.when(gi == k)
        def _(w=w):
            half = w // 2
            total = plane(-half)
            for p in range(-half + 1, half):
                total = total + plane(p)
            diffs = []
            for s_t in range(CHUNK):
                diffs.append((total * inv_count(s_t, half) - plane(s_t)).astype(BF16))
                if s_t + 1 < CHUNK:
                    total = total + plane(s_t + half) - plane(s_t - half)
            diff = jnp.concatenate(diffs, axis=0)
            out = _dot(diff, w_ref[...].astype(BF16)) * sc_ref[...]
            o_ref[...] = out.reshape(o_ref.shape).astype(o_ref.dtype)


def _pool(zp_t, pool_w, pool_scale, bsz):
    _, nch, _ = zp_t.shape
    n_c = nch // bsz
    return pl.pallas_call(
        _pool_kernel,
        grid=(bsz, len(POOL_WINDOWS)),
        in_specs=[
            pl.BlockSpec((CHUNK, n_c, POOL_GROUP), lambda i, g: (0, i, g)),
            pl.BlockSpec((None, POOL_GROUP, POOL_GROUP), lambda i, g: (g, 0, 0)),
            pl.BlockSpec((1, POOL_GROUP), lambda i, g: (0, g)),
        ],
        out_specs=pl.BlockSpec((CHUNK, n_c, POOL_GROUP), lambda i, g: (0, i, g)),
        out_shape=jax.ShapeDtypeStruct(zp_t.shape, BF16),
        compiler_params=_params(("parallel", "parallel")),
        name="pool",
    )(zp_t, pool_w, pool_scale)


def _expand_consts():
    time = np.arange(CHUNK_W) // SSM_GROUP
    def onehot(e):
        m = np.zeros((CHUNK_W, LANES), np.float32)
        m[np.arange(CHUNK_W), e] = 1.0
        return m
    return np.stack([
        onehot(CHUNK - 1 - time),
        onehot(time),
        onehot(time + 1),
        onehot(CHUNK - time),
    ])


def _ssm_kernel(u_ref, vec_ref, mat_ref, exp_ref, wg_ref, wu_ref, wd_ref, y_ref, og_ref, ou_ref,
                od_ref, *, seq_chunks):
    og_ref[...] = wg_ref[...].astype(BF16)
    ou_ref[...] = wu_ref[...].astype(BF16)
    od_ref[...] = wd_ref[...].astype(BF16)

    n_ct = u_ref.shape[0]
    nch = n_ct * C_TILE
    half = LANES // 2
    lane = lax.broadcasted_iota(jnp.int32, (1, LANES), 1)
    lo_half = lane < half

    def direction(di):
        a_re = vec_ref[di, 0:1]
        a_im = vec_ref[di, 1:2]
        dt = jnp.exp(vec_ref[di, 2:3])
        mag = jnp.exp(a_re * dt)
        ang = a_im * dt
        lb_re = mag * jnp.cos(ang)
        lb_im = mag * jnp.sin(ang)
        den = a_re * a_re + a_im * a_im
        f_re = ((lb_re - 1.0) * a_re + lb_im * a_im) / den
        f_im = (lb_im * a_re - (lb_re - 1.0) * a_im) / den
        return (lb_re, lb_im), f_re, f_im

    def power_table(lam):
        e = lax.broadcasted_iota(jnp.int32, (LANES, 1), 0)
        sq_re, sq_im = lam
        t_re = jnp.ones((LANES, LANES), F32)
        t_im = jnp.zeros((LANES, LANES), F32)
        for k in range(CHUNK.bit_length()):
            bit = ((e >> k) & 1) == 1
            t_re, t_im = (jnp.where(bit, t_re * sq_re - t_im * sq_im, t_re),
                          jnp.where(bit, t_re * sq_im + t_im * sq_re, t_im))
            sq_re, sq_im = sq_re * sq_re - sq_im * sq_im, 2.0 * sq_re * sq_im
        return jnp.where(lo_half, t_re, t_im), jnp.where(lo_half, t_im, t_re)

    def tile_rows(x16):
        return jnp.broadcast_to(x16[None], (CHUNK, SSM_GROUP, LANES)).reshape(CHUNK_W, LANES)

    def expanded(tab, which, v_re, v_im, conj_sign):
        lexp = _dot(exp_ref[which], jnp.concatenate(tab, axis=1).astype(BF16))
        if conj_sign > 0:
            p = jnp.where(lo_half, v_re, v_re)
            q = jnp.where(lo_half, -v_im, v_im)
        else:
            p = jnp.where(lo_half, v_re, -v_re)
            q = jnp.where(lo_half, -v_im, -v_im)
        return lexp[:, :LANES] * tile_rows(p) + lexp[:, LANES:] * tile_rows(q)

    lam_f, ff_re, ff_im = direction(0)
    lam_b, fb_re, fb_im = direction(1)
    tabs_f = power_table(lam_f)
    tabs_b = power_table(lam_b)
    tab_f, tab_b = tabs_f[0], tabs_b[0]

    def bbar(bt_re, bt_im, f_re, f_im):
        return bt_re * f_re - bt_im * f_im, bt_re * f_im + bt_im * f_re

    def mat(di, k):
        return mat_ref[di, k * SSM_GROUP:(k + 1) * SSM_GROUP, :]

    bf_re, bf_im = bbar(mat(0, 0), mat(0, 1), ff_re, ff_im)
    bb_re, bb_im = bbar(mat(1, 0), mat(1, 1), fb_re, fb_im)

    pb1 = expanded(tabs_f, 0, bf_re, bf_im, 1)
    pb2 = expanded(tabs_b, 1, bb_re, bb_im, 1)
    pb3 = expanded(tabs_b, 2, bb_re, bb_im, 1)
    ft_f = expanded(tabs_f, 2, mat(0, 3), mat(0, 4), -1)
    ft_b = expanded(tabs_b, 3, mat(1, 3), mat(1, 4), -1)

    row_w = lax.broadcasted_iota(jnp.int32, (CHUNK_W, 1), 0)
    last_blk = row_w >= CHUNK_W - SSM_GROUP
    pb2_lag0 = jnp.where(last_blk, pltpu.roll(pb2, CHUNK_W - SSM_GROUP, 0), 0.0)
    ccr_f = mat(0, 2).astype(BF16)
    ccr_b = mat(1, 2).astype(BF16)
    r_lo = _dot_nt(ccr_f, pb1.astype(BF16)) + _dot_nt(ccr_b, pb2_lag0.astype(BF16))
    co = lax.broadcasted_iota(jnp.int32, (SSM_GROUP, CHUNK_W), 0)
    col = lax.broadcasted_iota(jnp.int32, (SSM_GROUP, CHUNK_W), 1)
    r_lo = r_lo + jnp.where(col == CHUNK_W - SSM_GROUP + co, mat(0, 5)[:, 0:1], 0.0)
    r_hi = _dot_nt(ccr_b, pb3.astype(BF16))
    r_t = jnp.concatenate([r_lo, r_hi], axis=1)
    g_t = jnp.concatenate(
        [pltpu.roll(r_t, SSM_GROUP * (tl + 1), 1) for tl in range(T_SUB)], axis=0
    ).astype(BF16)

    u = jnp.concatenate([u_ref[ct].reshape(CHUNK_W, C_TILE) for ct in range(n_ct)],
                        axis=1)
    e_mat = jnp.concatenate([pb1, pb2], axis=1).astype(BF16)
    xend = _dot_tn(e_mat, u)
    lanec = lax.broadcasted_iota(jnp.int32, (1, nch), 1) % seq_chunks
    ns = SSM_STATE

    def scan(re, im, tab, forward):
        lam_col = jnp.transpose(tab[CHUNK:CHUNK + 8, :])[:, 0:1]
        a, b = lam_col[:ns], lam_col[ns:]
        n_steps = int(math.log2(seq_chunks))

        def rolled(v, d):
            if d % LANES == 0:
                return jnp.concatenate([v[:, nch - d:], v[:, :nch - d]], axis=1)
            return pltpu.roll(v, d, 1)

        def shifted(v, d):
            if forward:
                return jnp.where(lanec >= d, rolled(v, d), 0.0)
            return jnp.where(lanec < seq_chunks - d, rolled(v, nch - d), 0.0)

        for k in range(n_steps):
            sr, si = shifted(re, 1 << k), shifted(im, 1 << k)
            re, im = re + (sr * a - si * b), im + (sr * b + si * a)
            a, b = a * a - b * b, 2.0 * a * b
        return shifted(re, 1), shifted(im, 1)

    f_re, f_im = scan(xend[:ns], xend[ns:2 * ns], tab_f, True)
    b_re, b_im = scan(xend[2 * ns:3 * ns], xend[3 * ns:], tab_b, False)
    xin = jnp.concatenate([f_re, f_im, b_re, b_im], axis=0).astype(BF16)
    f_t = jnp.concatenate([ft_f, ft_b], axis=1).astype(BF16)

    toeplitz = jnp.concatenate(
        [g_t[:, CHUNK_W - LANES * th:2 * CHUNK_W - LANES * th] for th in range(CHUNK // T_SUB)],
        axis=0)
    y_t = _dot(toeplitz, u) + _dot(f_t, xin)
    for ct in range(n_ct):
        y_ref[ct] = y_t[:, ct * C_TILE:(ct + 1) * C_TILE].reshape(
            CHUNK, SSM_GROUP, C_TILE).astype(y_ref.dtype)


def _ssm(ut, a_re, a_im, log_dt, b_re, b_im, c_re, c_im, d, seq_chunks, w_gate, w_up, w_down, layer):
    g, n_ct = ut.shape[:2]
    assert N_EXPERTS % g == 0
    epg = N_EXPERTS // g
    base = layer * g
    n = SSM_STATE

    def per_group(a):
        return jnp.swapaxes(a, 0, 1)

    def dup(a):
        return jnp.concatenate([a, a], axis=-1)

    vecs = dup(jnp.stack([per_group(a_re), per_group(a_im),
                          jnp.broadcast_to(per_group(log_dt)[..., None], (g, 2, n))], axis=2))
    cr, ci = per_group(c_re), per_group(c_im)
    d_blk = jnp.broadcast_to(d.reshape(g, 1, SSM_GROUP, 1), (g, 2, SSM_GROUP, LANES))
    mats = jnp.concatenate(
        [dup(jnp.swapaxes(per_group(b_re), 2, 3)), dup(jnp.swapaxes(per_group(b_im), 2, 3)),
         jnp.concatenate([cr, -ci], axis=-1), dup(cr), dup(ci), d_blk], axis=2)
    exp_c = jnp.asarray(_expand_consts(), BF16)

    return pl.pallas_call(
        functools.partial(_ssm_kernel, seq_chunks=seq_chunks),
        grid=(g,),
        in_specs=[
            pl.BlockSpec((None, n_ct, CHUNK, SSM_GROUP, C_TILE), lambda i: (i, 0, 0, 0, 0)),
            pl.BlockSpec((None, 2, 3, LANES), lambda i: (i, 0, 0, 0)),
            pl.BlockSpec((None, 2, 6 * SSM_GROUP, LANES), lambda i: (i, 0, 0, 0)),
            pl.BlockSpec((4, CHUNK_W, LANES), lambda i: (0, 0, 0)),
            pl.BlockSpec((epg, D_MODEL, D_EXPERT), lambda i: (base + i, 0, 0)),
            pl.BlockSpec((epg, D_MODEL, D_EXPERT), lambda i: (base + i, 0, 0)),
            pl.BlockSpec((epg, D_EXPERT, D_MODEL), lambda i: (base + i, 0, 0)),
        ],
        out_specs=[
            pl.BlockSpec((None, n_ct, CHUNK, SSM_GROUP, C_TILE), lambda i: (i, 0, 0, 0, 0)),
            pl.BlockSpec((epg, D_MODEL, D_EXPERT), lambda i: (i, 0, 0)),
            pl.BlockSpec((epg, D_MODEL, D_EXPERT), lambda i: (i, 0, 0)),
            pl.BlockSpec((epg, D_EXPERT, D_MODEL), lambda i: (i, 0, 0)),
        ],
        out_shape=[
            jax.ShapeDtypeStruct(ut.shape, BF16),
            jax.ShapeDtypeStruct((N_EXPERTS, D_MODEL, D_EXPERT), BF16),
            jax.ShapeDtypeStruct((N_EXPERTS, D_MODEL, D_EXPERT), BF16),
            jax.ShapeDtypeStruct((N_EXPERTS, D_EXPERT, D_MODEL), BF16),
        ],
        compiler_params=_params(("parallel",)),
        name="ssm",
    )(ut, vecs, mats, exp_c, w_gate, w_up, w_down)


def _mix_out_kernel(x_hbm, a_ref, yt_ref, gwt_ref, gb_ref, wo_ref, h_hbm, xbuf, hbuf, xsem, hsem):
    nc = C_TILE
    x = _load_time_major_rows(x_hbm, xbuf, xsem)
    a = a_ref[...].reshape(S_TILE * nc, D_POOL)
    y = jnp.concatenate([yt_ref[:, j, :, :].reshape(D_SSM, nc) for j in range(S_TILE)],
                        axis=1).astype(F32)
    z = 0.5 * y * (1.0 + jnp.tanh(math.sqrt(2.0 / math.pi) * (y + 0.044715 * (y * y * y))))
    gate = _sigmoid(_dot(gwt_ref[...], z.astype(BF16)) + gb_ref[...])
    s = (z * gate).astype(BF16)
    h = x + _dot(a, wo_ref[:D_POOL, :]) + _dot_tn(s, wo_ref[D_POOL:, :])
    _store_time_major_rows(h, h_hbm, hbuf, hsem)


def _mix_out(x3, a3, yt, glu_w_t, glu_b_col, w_out):
    nch = x3.shape[0]
    return pl.pallas_call(
        _mix_out_kernel,
        grid=(nch // C_TILE, CHUNK // S_TILE),
        in_specs=[
            pl.BlockSpec(memory_space=pl.ANY),
            pl.BlockSpec((S_TILE, C_TILE, D_POOL), lambda c, t: (t, c, 0)),
            pl.BlockSpec((N_SSM_GROUPS, None, S_TILE, SSM_GROUP, C_TILE), lambda c, t: (0, c, t, 0, 0)),
            pl.BlockSpec((D_SSM, D_SSM), lambda c, t: (0, 0)),
            pl.BlockSpec((D_SSM, 1), lambda c, t: (0, 0)),
            pl.BlockSpec((D_MODEL, D_MODEL), lambda c, t: (0, 0)),
        ],
        out_specs=pl.BlockSpec(memory_space=pl.ANY),
        out_shape=jax.ShapeDtypeStruct((nch, CHUNK, D_MODEL), F32),
        scratch_shapes=[
            pltpu.VMEM((2, S_TILE, C_TILE, D_MODEL), F32),
            pltpu.VMEM((2, S_TILE, C_TILE, D_MODEL), F32),
            pltpu.SemaphoreType.DMA((2,)),
            pltpu.SemaphoreType.DMA((2,)),
        ],
        compiler_params=_params(("arbitrary", "arbitrary")),
        name="mix_out",
    )(x3, a3, yt, glu_w_t, glu_b_col, w_out)


def _pack_rows(x):
    b = lax.bitcast_convert_type(x.astype(BF16).astype(F32), U32)
    return (b[:, :HALF] & jnp.uint32(0xFFFF0000)) | (b[:, HALF:] >> 16)


def _unpack_rows(w):
    lo = lax.bitcast_convert_type(w & jnp.uint32(0xFFFF0000), F32)
    hi = lax.bitcast_convert_type(w << 16, F32)
    return lo, hi


def _split_bf16(x):
    hi = x.astype(BF16)
    return hi, (x - hi.astype(F32)).astype(BF16)


def _route(v32, wr_ref, br_ref):
    v_hi, v_lo = _split_bf16(v32)
    w_hi, w_lo = _split_bf16(wr_ref[...])
    both = _dot(v_hi, jnp.concatenate([w_hi, w_lo], axis=1))
    logits = both[:, :ROUTER_W] + (both[:, ROUTER_W:] + _dot(v_lo, w_hi)) + br_ref[...]
    return jnp.transpose(logits)


def _top1(x, valid=None):
    n = x.shape[0]
    row = lax.broadcasted_iota(jnp.int32, x.shape, 0).astype(F32)
    if valid is not None:
        x = jnp.where(valid, x, -jnp.inf)
    m = jnp.max(x, axis=0, keepdims=True)
    idx = jnp.min(jnp.where(x == m, row, float(n)), axis=0, keepdims=True)
    return m, idx, x, row


def _split_planes(packed, ref):
    ref[0] = packed[:, :SC_ROW]
    ref[1] = packed[:, SC_ROW:]


def _router_kernel(h_ref, g_ref, wr_ref, br_ref, before_ref, vp_ref, meta_t_ref, cnt_ref, carry_ref):
    @pl.when(pl.program_id(0) == 0)
    def _():
        carry_ref[...] = jnp.zeros_like(carry_ref)

    v32 = _rms(h_ref[...], g_ref[...])
    _split_planes(_pack_rows(v32), vp_ref)
    lt = _route(v32, wr_ref, br_ref)
    tm = lt.shape[1]
    eg = EXPERTS_PER_GROUP

    grp = lt[:eg]
    grp_row = lax.broadcasted_iota(jnp.int32, grp.shape, 0)
    mg, grp_idx, grp, _ = _top1(grp, grp_row < N_EXPERT_GROUPS)
    grp_p = 1.0 / jnp.sum(jnp.exp(grp - mg), axis=0, keepdims=True)
    le = jnp.zeros((eg, tm), F32)
    for g in range(N_EXPERT_GROUPS):
        le = jnp.where(grp_idx == float(g), lt[eg * (g + 1):eg * (g + 2)], le)
    m1, i1, le, row = _top1(le)
    z = jnp.sum(jnp.exp(le - m1), axis=0, keepdims=True)
    m2, i2, _, _ = _top1(jnp.where(row == i1, -jnp.inf, le))
    p1 = 1.0 / z
    p2 = jnp.exp(m2 - m1) / z
    tot = p1 + p2
    w1 = grp_p * (p1 / tot)
    w2 = grp_p * (p2 / tot)
    e1 = grp_idx * eg + i1
    e2 = grp_idx * eg + i2

    erow = lax.broadcasted_iota(jnp.int32, (N_EXPERTS, tm), 0).astype(F32)
    onehot = jnp.where(erow == e1, 1.0, jnp.where(erow == e2, 1.0, 0.0))
    before = _dot(onehot.astype(BF16), before_ref[...]) + carry_ref[...]
    rank1 = jnp.sum(jnp.where(erow == e1, before, 0.0), axis=0, keepdims=True)
    rank2 = jnp.sum(jnp.where(erow == e2, before, 0.0), axis=0, keepdims=True)
    carry = carry_ref[...] + jnp.sum(onehot, axis=1, keepdims=True)
    carry_ref[...] = carry
    cnt_ref[...] = carry

    mrow = lax.broadcasted_iota(jnp.int32, (META_ROWS, tm), 0)
    meta_t_ref[...] = jnp.where(mrow == 0, e1, jnp.where(mrow == 1, e2, jnp.where(
        mrow == 2, rank1, jnp.where(mrow == 3, rank2, jnp.where(
            mrow == 4, w1, jnp.where(mrow == 5, w2, 0.0))))))


def _earlier_matrix(tm):
    return np.triu(np.ones((tm, tm), np.float32), k=1)


def _router(h1, g_ffn, w_router, b_router, tm):
    t = h1.shape[0]
    return pl.pallas_call(
        _router_kernel,
        grid=(t // tm,),
        in_specs=[
            pl.BlockSpec((tm, D_MODEL), lambda i: (i, 0)),
            pl.BlockSpec((1, D_MODEL), lambda i: (0, 0)),
            pl.BlockSpec((D_MODEL, ROUTER_W), lambda i: (0, 0)),
            pl.BlockSpec((1, ROUTER_W), lambda i: (0, 0)),
            pl.BlockSpec((tm, tm), lambda i: (0, 0)),
        ],
        out_specs=[
            pl.BlockSpec((2, tm, SC_ROW), lambda i: (0, i, 0)),
            pl.BlockSpec((META_ROWS, tm), lambda i: (0, i)),
            pl.BlockSpec((N_EXPERTS, 1), lambda i: (0, 0)),
        ],
        out_shape=[
            jax.ShapeDtypeStruct((2, t, SC_ROW), U32),
            jax.ShapeDtypeStruct((META_ROWS, t), F32),
            jax.ShapeDtypeStruct((N_EXPERTS, 1), F32),
        ],
        scratch_shapes=[pltpu.VMEM((N_EXPERTS, 1), F32)],
        compiler_params=_params(("arbitrary",)),
        name="router",
    )(h1, g_ffn, w_router, b_router, jnp.asarray(_earlier_matrix(tm), BF16))


def _plan(meta_t, counts, n_tiles):
    e1 = meta_t[0].astype(jnp.int32)
    e2 = meta_t[1].astype(jnp.int32)
    rank1 = meta_t[2].astype(jnp.int32)
    rank2 = meta_t[3].astype(jnp.int32)
    cnt = counts[:, 0].astype(jnp.int32)
    padded = ((cnt + ROW_TILE - 1) // ROW_TILE) * ROW_TILE
    ends = jnp.cumsum(padded)
    starts = ends - padded
    experts = jnp.arange(N_EXPERTS, dtype=jnp.int32)
    pos1 = rank1 + jnp.sum(jnp.where(e1[None, :] == experts[:, None], starts[:, None], 0), axis=0)
    pos2 = rank2 + jnp.sum(jnp.where(e2[None, :] == experts[:, None], starts[:, None], 0), axis=0)
    tile_start = jnp.arange(n_tiles, dtype=jnp.int32) * ROW_TILE
    tile_expert = jnp.sum((tile_start[:, None] >= ends[None, :]).astype(jnp.int32), axis=1)
    tile_expert = jnp.minimum(tile_expert, N_EXPERTS - 1)
    rows_left = jnp.sum(jnp.where(tile_expert[:, None] == experts, cnt + starts, 0), axis=1) - tile_start
    n_valid = jnp.clip(rows_left, 0, ROW_TILE).astype(jnp.int32)
    last_used = jnp.maximum(ends[-1] // ROW_TILE - 1, 0)
    block = jnp.minimum(jnp.arange(n_tiles, dtype=jnp.int32), last_used)
    tile_expert = jnp.sum(jnp.where(block[:, None] == jnp.arange(n_tiles)[None, :],
                                    tile_expert[None, :], 0), axis=1)
    plane = n_tiles * ROW_TILE
    half_rows = jnp.concatenate([pos1, pos1 + plane, pos2, pos2 + plane])[None]
    return half_rows, tile_expert, n_valid


def _sc_mesh():
    return plsc.VectorSubcoreMesh(core_axis_name="c", subcore_axis_name="s")


def _sc_scatter_rows(rows, idx, n_out):
    t, width = rows.shape
    steps = t // SC_WINDOW
    half = steps // 2

    @pl.kernel(out_type=jax.ShapeDtypeStruct((n_out, width), rows.dtype), mesh=_sc_mesh(),
               scratch_types=[], name="moe_scatter")
    def scatter(rows_hbm, idx_hbm, out_hbm):
        def body(rows_vmem, idx0_vmem, idx1_vmem):
            pltpu.sync_copy(rows_vmem, out_hbm.at[idx0_vmem.at[0]])
            pltpu.sync_copy(rows_vmem, out_hbm.at[idx1_vmem.at[0]])

        pltpu.emit_pipeline(
            body,
            grid=(2, half),
            in_specs=[pl.BlockSpec((SC_WINDOW, width), lambda c, j: (c * half + j, 0)),
                      pl.BlockSpec((1, SC_WINDOW), lambda c, j: (0, c * half + j)),
                      pl.BlockSpec((1, SC_WINDOW), lambda c, j: (0, steps + c * half + j))],
            out_specs=[],
            core_axis_name=("c", "s"),
            dimension_semantics=(pltpu.PARALLEL, pltpu.PARALLEL),
        )(rows_hbm, idx_hbm, idx_hbm)

    return scatter(rows, idx)


def _sc_gather_rows(table, idx):
    m = idx.shape[1]
    width = table.shape[1]
    steps = m // (2 * SC_WINDOW)

    @pl.kernel(out_type=jax.ShapeDtypeStruct((m, width), table.dtype), mesh=_sc_mesh(),
               scratch_types=[], name="moe_gather")
    def gather(table_hbm, idx_hbm, out_hbm):
        def body(idx_vmem, out_vmem):
            pltpu.sync_copy(table_hbm.at[idx_vmem.at[0]], out_vmem)

        pltpu.emit_pipeline(
            body,
            grid=(2, steps),
            in_specs=[pl.BlockSpec((1, SC_WINDOW), lambda k, j: (0, k * steps + j))],
            out_specs=[pl.BlockSpec((SC_WINDOW, width), lambda k, j: (k * steps + j, 0))],
            core_axis_name=("c", "s"),
            dimension_semantics=(pltpu.PARALLEL, pltpu.PARALLEL),
        )(idx_hbm, out_hbm)

    return gather(table, idx)


def _tile_row_copies(nv_ref, hbm, buf, sem, tile, slot, to_hbm):
    n_blocks = (nv_ref[tile] + ROW_BLOCK - 1) // ROW_BLOCK
    out = []
    for j in range(ROW_TILE // ROW_BLOCK):
        far = hbm.at[:, pl.ds(tile * ROW_TILE + j * ROW_BLOCK, ROW_BLOCK), :]
        near = buf.at[slot, :, pl.ds(j * ROW_BLOCK, ROW_BLOCK), :]
        src, dst = (near, far) if to_hbm else (far, near)
        out.append((pltpu.make_async_copy(src, dst, sem.at[slot]), j < n_blocks))
    return out


def _start_tile_rows(*args):
    for cp, wanted in _tile_row_copies(*args):
        @pl.when(wanted)
        def _(cp=cp):
            cp.start()


def _wait_tile_rows(*args):
    for cp, wanted in _tile_row_copies(*args):
        @pl.when(wanted)
        def _(cp=cp):
            cp.wait()


def _experts_kernel(te_ref, nv_ref, xs_hbm, wg_ref, wu_ref, wd_ref, ys_hbm, xbuf, ybuf, xsem, ysem):
    r = pl.program_id(0)
    n_tiles = pl.num_programs(0)
    slot = r % 2
    n_valid = nv_ref[r]

    @pl.when(r == 0)
    def _():
        _start_tile_rows(nv_ref, xs_hbm, xbuf, xsem, 0, 0, False)

    @pl.when(r + 1 < n_tiles)
    def _():
        _start_tile_rows(nv_ref, xs_hbm, xbuf, xsem, r + 1, 1 - slot, False)

    _wait_tile_rows(nv_ref, xs_hbm, xbuf, xsem, r, slot, False)

    @pl.when(r >= 2)
    def _():
        _wait_tile_rows(nv_ref, ys_hbm, ybuf, ysem, r - 2, slot, True)

    n_blocks = (n_valid + ROW_BLOCK - 1) // ROW_BLOCK
    for k in range(1, ROW_TILE // ROW_BLOCK + 1):
        @pl.when(n_blocks == k)
        def _(m=k * ROW_BLOCK):
            parts = [p.astype(BF16)
                     for p in _unpack_rows(xbuf[slot, 0, :m]) + _unpack_rows(xbuf[slot, 1, :m])]
            cols = (0, 2 * SC_ROW, SC_ROW, 3 * SC_ROW)
            hg = sum(_dot(p, wg_ref[c:c + SC_ROW, :]) for p, c in zip(parts, cols))
            hu = sum(_dot(p, wu_ref[c:c + SC_ROW, :]) for p, c in zip(parts, cols))
            row = lax.broadcasted_iota(jnp.int32, (m, 1), 0)
            hid = jnp.where(row < n_valid, hg * _sigmoid(hg) * hu, 0.0).astype(BF16)
            packed = _pack_rows(_dot(hid, wd_ref[...]))
            ybuf[slot, 0, :m] = packed[:, :SC_ROW]
            ybuf[slot, 1, :m] = packed[:, SC_ROW:]

    _start_tile_rows(nv_ref, ys_hbm, ybuf, ysem, r, slot, True)

    @pl.when(r == n_tiles - 1)
    def _():
        _wait_tile_rows(nv_ref, ys_hbm, ybuf, ysem, r, slot, True)

        @pl.when(r >= 1)
        def _():
            _wait_tile_rows(nv_ref, ys_hbm, ybuf, ysem, r - 1, 1 - slot, True)


def _experts(xs, tile_expert, n_valid, w_gate, w_up, w_down):
    n_tiles = xs.shape[1] // ROW_TILE
    w_spec = pl.BlockSpec((None, D_MODEL, D_EXPERT), lambda r, te, nv: (te[r], 0, 0))
    grid_spec = pltpu.PrefetchScalarGridSpec(
        num_scalar_prefetch=2,
        grid=(n_tiles,),
        in_specs=[
            pl.BlockSpec(memory_space=pl.ANY),
            w_spec, w_spec,
            pl.BlockSpec((None, D_EXPERT, D_MODEL), lambda r, te, nv: (te[r], 0, 0)),
        ],
        out_specs=pl.BlockSpec(memory_space=pl.ANY),
        scratch_shapes=[
            pltpu.VMEM((2, 2, ROW_TILE, SC_ROW), U32),
            pltpu.VMEM((2, 2, ROW_TILE, SC_ROW), U32),
            pltpu.SemaphoreType.DMA((2,)),
            pltpu.SemaphoreType.DMA((2,)),
        ],
    )
    return pl.pallas_call(
        _experts_kernel,
        grid_spec=grid_spec,
        out_shape=jax.ShapeDtypeStruct((2, n_tiles * ROW_TILE, SC_ROW), U32),
        compiler_params=_params(("arbitrary",)),
        name="experts",
    )(tile_expert, n_valid, xs, w_gate, w_up, w_down)


def _ple_kernel(h_ref, yg_ref, meta_t_ref, p_ref, wg_ref, bg_ref, wp_ref, gf_ref, *rest, final_norm):
    o_ref = rest[-1]
    meta = jnp.transpose(meta_t_ref[...])
    w1 = meta[:, 4:5]
    w2 = meta[:, 5:6]
    q0, q2 = (w1 * u + w2 * v for u, v in zip(_unpack_rows(yg_ref[0]), _unpack_rows(yg_ref[2])))
    q1, q3 = (w1 * u + w2 * v for u, v in zip(_unpack_rows(yg_ref[1]), _unpack_rows(yg_ref[3])))
    moe = jnp.concatenate([q0, q1, q2, q3], axis=1)
    h = h_ref[...] + moe
    gate = _sigmoid(_dot(_rms(h).astype(BF16), wg_ref[...]) + bg_ref[...])
    h = h + gate * _dot(p_ref[...].astype(BF16), wp_ref[...])
    o_ref[...] = _rms(h, gf_ref[...]) if final_norm else h


def _ple(h1, yg, meta_t, p2, w_gate, b_gate, w_proj, g_final, final_norm, tm, off, prev_out):
    t = h1.shape[0]
    steps = yg.shape[1] // tm
    in_specs = [
        pl.BlockSpec((tm, D_MODEL), lambda i: (i + off, 0)),
        pl.BlockSpec((4, tm, SC_ROW), lambda i: (0, i, 0)),
        pl.BlockSpec((META_ROWS, tm), lambda i: (0, i + off)),
        pl.BlockSpec((tm, D_PLE), lambda i: (i + off, 0)),
        pl.BlockSpec((D_MODEL, D_MODEL), lambda i: (0, 0)),
        pl.BlockSpec((1, D_MODEL), lambda i: (0, 0)),
        pl.BlockSpec((D_PLE, D_MODEL), lambda i: (0, 0)),
        pl.BlockSpec((1, D_MODEL), lambda i: (0, 0)),
    ]
    args = [h1, yg, meta_t, p2, w_gate, b_gate, w_proj, g_final]
    aliases = {}
    if prev_out is not None:
        in_specs.append(pl.BlockSpec(memory_space=pl.ANY))
        args.append(prev_out)
        aliases = {len(args) - 1: 0}
    return pl.pallas_call(
        functools.partial(_ple_kernel, final_norm=final_norm),
        grid=(steps,),
        in_specs=in_specs,
        out_specs=pl.BlockSpec((tm, D_MODEL), lambda i: (i + off, 0)),
        out_shape=jax.ShapeDtypeStruct((t, D_MODEL), F32),
        input_output_aliases=aliases,
        compiler_params=_params(("parallel",)),
        name="ple",
    )(*args)


def kernel(x, p, g_mix, w_in, pool_w, pool_scale, ssm_a_re, ssm_a_im, ssm_log_dt, ssm_b_re,
           ssm_b_im, ssm_c_re, ssm_c_im, ssm_d, glu_w, glu_b, w_out, g_ffn, router_grp_w,
           router_grp_b, router_exp_w, router_exp_b, exp_w_gate, exp_w_up, exp_w_down, g_ple,
           ple_gate_w, ple_gate_b, ple_proj_w, g_final):
    bsz, seq, dm = x.shape
    depth = g_mix.shape[0]
    t = bsz * seq
    seq_chunks = seq // CHUNK
    nch = t // CHUNK
    tm = 1024
    n_sorted = (pl.cdiv(2 * t, ROW_TILE) + N_EXPERTS) * ROW_TILE
    w_gate_all = exp_w_gate.reshape(depth * N_EXPERTS, dm, D_EXPERT)
    w_up_all = exp_w_up.reshape(depth * N_EXPERTS, dm, D_EXPERT)
    w_down_all = exp_w_down.reshape(depth * N_EXPERTS, D_EXPERT, dm)

    h = x.reshape(t, dm)
    for i in range(depth):
        w_in_b = (g_mix[i][:, None] * w_in[i]).astype(BF16)
        zp, ut = _in_proj(h.reshape(nch, CHUNK, dm), w_in_b[:, :D_POOL],
                          jnp.transpose(w_in_b[:, D_POOL:]))
        a = _pool(zp, pool_w[i], pool_scale[i][None], bsz)
        yt, w_gate_b, w_up_b, w_down_b = _ssm(
            ut, ssm_a_re[i], ssm_a_im[i], ssm_log_dt[i], ssm_b_re[i], ssm_b_im[i], ssm_c_re[i],
            ssm_c_im[i], ssm_d[i], seq_chunks, w_gate_all, w_up_all, w_down_all, i)
        h = _mix_out(h.reshape(nch, CHUNK, dm), a, yt,
                     jnp.transpose(glu_w[i]).astype(BF16), glu_b[i][:, None],
                     w_out[i].astype(BF16)).reshape(t, dm)

        eg = EXPERTS_PER_GROUP
        w_router = jnp.concatenate(
            [router_grp_w[i], jnp.zeros((dm, eg - N_EXPERT_GROUPS), F32),
             jnp.transpose(router_exp_w[i], (1, 0, 2)).reshape(dm, N_EXPERTS),
             jnp.zeros((dm, ROUTER_W - eg - N_EXPERTS), F32)], axis=1)
        b_router = jnp.concatenate(
            [router_grp_b[i], jnp.zeros((eg - N_EXPERT_GROUPS,), F32),
             router_exp_b[i].reshape(N_EXPERTS),
             jnp.zeros((ROUTER_W - eg - N_EXPERTS,), F32)])[None]
        vp, meta_t, counts = _router(h, g_ffn[i][None], w_router, b_router, tm)
        idx, tile_expert, n_valid = _plan(meta_t, counts, n_sorted // ROW_TILE)
        xs = _sc_scatter_rows(vp.reshape(2 * t, SC_ROW), idx, 2 * n_sorted)
        ys = _experts(xs.reshape(2, n_sorted, SC_ROW), tile_expert, n_valid,
                      w_gate_b, w_up_b, w_down_b)
        ys2 = ys.reshape(2 * n_sorted, SC_ROW)
        idx4 = idx.reshape(4, t)
        ple_wg = (g_ple[i][:, None] * ple_gate_w[i]).astype(BF16)
        ple_wp = ple_proj_w[i].astype(BF16)
        out = None
        cuts = (0, t // FIRST_PART, t)
        for lo, hi in zip(cuts[:-1], cuts[1:]):
            yg_q = _sc_gather_rows(ys2, idx4[:, lo:hi].reshape(1, 4 * (hi - lo)))
            out = _ple(h, yg_q.reshape(4, hi - lo, SC_ROW), meta_t, p[i].reshape(t, D_PLE), ple_wg,
                       ple_gate_b[i][None], ple_wp, g_final[None], i == depth - 1, tm, lo // tm, out)
        h = out
    return h.reshape(bsz, seq, dm)
```

```python
import functools
import math

import numpy as np
import jax
import jax.numpy as jnp
from jax import lax
from jax.experimental import pallas as pl
from jax.experimental.pallas import tpu as pltpu
from jax.experimental.pallas import tpu_sc as plsc

F32 = jnp.float32
BF16 = jnp.bfloat16
U32 = jnp.uint32

D_MODEL = 1024
D_POOL = 512
D_SSM = 512
POOL_WINDOWS = (2, 4, 8, 16)
POOL_GROUP = 128
SSM_GROUP = 16
N_SSM_GROUPS = 32
SSM_STATE = 64
N_EXPERT_GROUPS = 4
EXPERTS_PER_GROUP = 8
N_EXPERTS = N_EXPERT_GROUPS * EXPERTS_PER_GROUP
D_EXPERT = 256
D_PLE = 256
RMS_EPS = 1e-6

LANES = 128
CHUNK = 32
CHUNK_W = CHUNK * SSM_GROUP
T_SUB = 8
S_TILE = 8
C_TILE = 128
READ_RING = 3
ROUTER_W = LANES
HALF = D_MODEL // 2
ROW_TILE = 1280
SC_WINDOW = 128
SC_ROW = HALF // 2
META_ROWS = 8
FIRST_PART = 4
VMEM_LIMIT = 56 * 1024 * 1024


def _dot(a, b):
    return jnp.dot(a, b, preferred_element_type=F32)


def _dot_nt(a, b):
    return lax.dot_general(a, b, (((1,), (1,)), ((), ())), preferred_element_type=F32)


def _dot_tn(a, b):
    return lax.dot_general(a, b, (((0,), (0,)), ((), ())), preferred_element_type=F32)


def _rms(x, g=None):
    y = x * lax.rsqrt(jnp.mean(x * x, axis=-1, keepdims=True) + RMS_EPS)
    return y if g is None else y * g


def _sigmoid(x):
    return 1.0 / (1.0 + jnp.exp(-x))


def _params(sem):
    return pltpu.CompilerParams(dimension_semantics=sem, vmem_limit_bytes=VMEM_LIMIT)


def _tile_step():
    n_s = pl.num_programs(1)
    return pl.program_id(0) * n_s + pl.program_id(1), pl.num_programs(0) * n_s


def _row_copies(hbm, buf, sem, step_idx, slot_idx, to_hbm):
    n_s = pl.num_programs(1)
    c0 = (step_idx // n_s) * C_TILE
    s0 = (step_idx % n_s) * S_TILE
    out = []
    for j in range(S_TILE):
        far, near = hbm.at[pl.ds(c0, C_TILE), s0 + j, :], buf.at[slot_idx, j]
        src, dst = (near, far) if to_hbm else (far, near)
        out.append(pltpu.make_async_copy(src, dst, sem.at[slot_idx]))
    return out


def _load_time_major_rows(hbm, buf, sem):
    step, n_steps = _tile_step()
    n_buf = buf.shape[0]
    ahead = n_buf - 1
    slot = step % n_buf

    @pl.when(step == 0)
    def _():
        for k in range(ahead):
            @pl.when(k < n_steps)
            def _(k=k):
                for cp in _row_copies(hbm, buf, sem, k, k, False):
                    cp.start()

    @pl.when(step + ahead < n_steps)
    def _():
        for cp in _row_copies(hbm, buf, sem, step + ahead, (step + ahead) % n_buf, False):
            cp.start()

    for cp in _row_copies(hbm, buf, sem, step, slot, False):
        cp.wait()
    return buf[slot].reshape(S_TILE * C_TILE, buf.shape[-1])


def _store_time_major_rows(val, hbm, buf, sem):
    step, n_steps = _tile_step()
    slot = step % 2

    @pl.when(step >= 2)
    def _():
        for cp in _row_copies(hbm, buf, sem, step - 2, slot, True):
            cp.wait()

    buf[slot] = val.reshape(S_TILE, C_TILE, val.shape[-1])
    for cp in _row_copies(hbm, buf, sem, step, slot, True):
        cp.start()

    @pl.when(step == n_steps - 1)
    def _():
        @pl.when(step >= 1)
        def _():
            for cp in _row_copies(hbm, buf, sem, step - 1, 1 - slot, True):
                cp.wait()
        for cp in _row_copies(hbm, buf, sem, step, slot, True):
            cp.wait()


def _in_proj_kernel(x_hbm, wp_ref, wst_ref, zp_ref, ut_ref, xbuf, xsem):
    nc = C_TILE
    u = _rms(_load_time_major_rows(x_hbm, xbuf, xsem)).astype(BF16)
    zp_ref[...] = _dot(u, wp_ref[...]).reshape(zp_ref.shape).astype(zp_ref.dtype)
    zt = _dot_nt(wst_ref[...], u).astype(BF16)
    for j in range(S_TILE):
        ut_ref[:, j, :, :] = zt[:, j * nc:(j + 1) * nc].reshape(N_SSM_GROUPS, SSM_GROUP, nc)


def _in_proj(x3, w_pool, w_ssm_t):
    nch = x3.shape[0]
    return pl.pallas_call(
        _in_proj_kernel,
        grid=(nch // C_TILE, CHUNK // S_TILE),
        in_specs=[
            pl.BlockSpec(memory_space=pl.ANY),
            pl.BlockSpec((D_MODEL, D_POOL), lambda c, s: (0, 0)),
            pl.BlockSpec((D_SSM, D_MODEL), lambda c, s: (0, 0)),
        ],
        out_specs=[
            pl.BlockSpec((S_TILE, C_TILE, D_POOL), lambda c, s: (s, c, 0)),
            pl.BlockSpec((N_SSM_GROUPS, None, S_TILE, SSM_GROUP, C_TILE), lambda c, s: (0, c, s, 0, 0)),
        ],
        out_shape=[
            jax.ShapeDtypeStruct((CHUNK, nch, D_POOL), BF16),
            jax.ShapeDtypeStruct((N_SSM_GROUPS, nch // C_TILE, CHUNK, SSM_GROUP, C_TILE), BF16),
        ],
        scratch_shapes=[
            pltpu.VMEM((READ_RING, S_TILE, C_TILE, D_MODEL), F32),
            pltpu.SemaphoreType.DMA((READ_RING,)),
        ],
        compiler_params=_params(("arbitrary", "arbitrary")),
        name="in_proj",
    )(x3, w_pool, w_ssm_t)


def _pool_kernel(z_ref, w_ref, sc_ref, o_ref):
    n_c = z_ref.shape[1]
    gi = pl.program_id(1)
    chunk = lax.broadcasted_iota(jnp.int32, (n_c, 1), 0)

    def plane(p):
        if p < 0:
            return jnp.where(chunk >= 1, pltpu.roll(z_ref[p + CHUNK].astype(F32), 1, 0), 0.0)
        if p >= CHUNK:
            return jnp.where(chunk < n_c - 1, pltpu.roll(z_ref[p - CHUNK].astype(F32), n_c - 1, 0), 0.0)
        return z_ref[p].astype(F32)

    def inv_count(s_t, half):
        n = n_c * CHUNK
        first = min(s_t + half, n) - max(s_t - half, 0)
        t_last = n - CHUNK + s_t
        last = min(t_last + half, n) - max(t_last - half, 0)
        inv = 1.0 / (2 * half)
        if first != 2 * half:
            inv = jnp.where(chunk == 0, 1.0 / first, inv)
        if last != 2 * half:
            inv = jnp.where(chunk == n_c - 1, 1.0 / last, inv)
        return inv

    for k, w in enumerate(POOL_WINDOWS):
        @pl.when(gi == k)
        def _(w=w):
            half = w // 2
            total = plane(-half)
            for p in range(-half + 1, half):
                total = total + plane(p)
            diffs = []
            for s_t in range(CHUNK):
                diffs.append((total * inv_count(s_t, half) - plane(s_t)).astype(BF16))
                if s_t + 1 < CHUNK:
                    total = total + plane(s_t + half) - plane(s_t - half)
            diff = jnp.concatenate(diffs, axis=0)
            out = _dot(diff, w_ref[...].astype(BF16)) * sc_ref[...]
            o_ref[...] = out.reshape(o_ref.shape).astype(o_ref.dtype)


def _pool(zp_t, pool_w, pool_scale, bsz):
    _, nch, _ = zp_t.shape
    n_c = nch // bsz
    return pl.pallas_call(
        _pool_kernel,
        grid=(bsz, len(POOL_WINDOWS)),
        in_specs=[
            pl.BlockSpec((CHUNK, n_c, POOL_GROUP), lambda i, g: (0, i, g)),
            pl.BlockSpec((None, POOL_GROUP, POOL_GROUP), lambda i, g: (g, 0, 0)),
            pl.BlockSpec((1, POOL_GROUP), lambda i, g: (0, g)),
        ],
        out_specs=pl.BlockSpec((CHUNK, n_c, POOL_GROUP), lambda i, g: (0, i, g)),
        out_shape=jax.ShapeDtypeStruct(zp_t.shape, BF16),
        compiler_params=_params(("parallel", "parallel")),
        name="pool",
    )(zp_t, pool_w, pool_scale)


def _expand_consts():
    time = np.arange(CHUNK_W) // SSM_GROUP
    def onehot(e):
        m = np.zeros((CHUNK_W, LANES), np.float32)
        m[np.arange(CHUNK_W), e] = 1.0
        return m
    return np.stack([
        onehot(CHUNK - 1 - time),
        onehot(time),
        onehot(time + 1),
        onehot(CHUNK - time),
    ])


def _ssm_kernel(u_ref, vec_ref, mat_ref, exp_ref, wg_ref, wu_ref, wd_ref, y_ref, og_ref, ou_ref,
                od_ref, *, seq_chunks):
    og_ref[...] = wg_ref[...].astype(BF16)
    ou_ref[...] = wu_ref[...].astype(BF16)
    od_ref[...] = wd_ref[...].astype(BF16)

    n_ct = u_ref.shape[0]
    nch = n_ct * C_TILE
    half = LANES // 2
    lane = lax.broadcasted_iota(jnp.int32, (1, LANES), 1)
    lo_half = lane < half

    def direction(di):
        a_re = vec_ref[di, 0:1]
        a_im = vec_ref[di, 1:2]
        dt = jnp.exp(vec_ref[di, 2:3])
        mag = jnp.exp(a_re * dt)
        ang = a_im * dt
        lb_re = mag * jnp.cos(ang)
        lb_im = mag * jnp.sin(ang)
        den = a_re * a_re + a_im * a_im
        f_re = ((lb_re - 1.0) * a_re + lb_im * a_im) / den
        f_im = (lb_im * a_re - (lb_re - 1.0) * a_im) / den
        return (lb_re, lb_im), f_re, f_im

    def power_table(lam):
        e = lax.broadcasted_iota(jnp.int32, (LANES, 1), 0)
        sq_re, sq_im = lam
        t_re = jnp.ones((LANES, LANES), F32)
        t_im = jnp.zeros((LANES, LANES), F32)
        for k in range(CHUNK.bit_length()):
            bit = ((e >> k) & 1) == 1
            t_re, t_im = (jnp.where(bit, t_re * sq_re - t_im * sq_im, t_re),
                          jnp.where(bit, t_re * sq_im + t_im * sq_re, t_im))
            sq_re, sq_im = sq_re * sq_re - sq_im * sq_im, 2.0 * sq_re * sq_im
        return jnp.where(lo_half, t_re, t_im), jnp.where(lo_half, t_im, t_re)

    def tile_rows(x16):
        return jnp.broadcast_to(x16[None], (CHUNK, SSM_GROUP, LANES)).reshape(CHUNK_W, LANES)

    def expanded(tab, which, v_re, v_im, conj_sign):
        lexp = _dot(exp_ref[which], jnp.concatenate(tab, axis=1).astype(BF16))
        if conj_sign > 0:
            p = jnp.where(lo_half, v_re, v_re)
            q = jnp.where(lo_half, -v_im, v_im)
        else:
            p = jnp.where(lo_half, v_re, -v_re)
            q = jnp.where(lo_half, -v_im, -v_im)
        return lexp[:, :LANES] * tile_rows(p) + lexp[:, LANES:] * tile_rows(q)

    lam_f, ff_re, ff_im = direction(0)
    lam_b, fb_re, fb_im = direction(1)
    tabs_f = power_table(lam_f)
    tabs_b = power_table(lam_b)
    tab_f, tab_b = tabs_f[0], tabs_b[0]

    def bbar(bt_re, bt_im, f_re, f_im):
        return bt_re * f_re - bt_im * f_im, bt_re * f_im + bt_im * f_re

    def mat(di, k):
        return mat_ref[di, k * SSM_GROUP:(k + 1) * SSM_GROUP, :]

    bf_re, bf_im = bbar(mat(0, 0), mat(0, 1), ff_re, ff_im)
    bb_re, bb_im = bbar(mat(1, 0), mat(1, 1), fb_re, fb_im)

    pb1 = expanded(tabs_f, 0, bf_re, bf_im, 1)
    pb2 = expanded(tabs_b, 1, bb_re, bb_im, 1)
    pb3 = expanded(tabs_b, 2, bb_re, bb_im, 1)
    ft_f = expanded(tabs_f, 2, mat(0, 3), mat(0, 4), -1)
    ft_b = expanded(tabs_b, 3, mat(1, 3), mat(1, 4), -1)

    row_w = lax.broadcasted_iota(jnp.int32, (CHUNK_W, 1), 0)
    last_blk = row_w >= CHUNK_W - SSM_GROUP
    pb2_lag0 = jnp.where(last_blk, pltpu.roll(pb2, CHUNK_W - SSM_GROUP, 0), 0.0)
    ccr_f = mat(0, 2).astype(BF16)
    ccr_b = mat(1, 2).astype(BF16)
    r_lo = _dot_nt(ccr_f, pb1.astype(BF16)) + _dot_nt(ccr_b, pb2_lag0.astype(BF16))
    co = lax.broadcasted_iota(jnp.int32, (SSM_GROUP, CHUNK_W), 0)
    col = lax.broadcasted_iota(jnp.int32, (SSM_GROUP, CHUNK_W), 1)
    r_lo = r_lo + jnp.where(col == CHUNK_W - SSM_GROUP + co, mat(0, 5)[:, 0:1], 0.0)
    r_hi = _dot_nt(ccr_b, pb3.astype(BF16))
    r_t = jnp.concatenate([r_lo, r_hi], axis=1)
    g_t = jnp.concatenate(
        [pltpu.roll(r_t, SSM_GROUP * (tl + 1), 1) for tl in range(T_SUB)], axis=0
    ).astype(BF16)

    u = jnp.concatenate([u_ref[ct].reshape(CHUNK_W, C_TILE) for ct in range(n_ct)],
                        axis=1)
    e_mat = jnp.concatenate([pb1, pb2], axis=1).astype(BF16)
    xend = _dot_tn(e_mat, u)
    lanec = lax.broadcasted_iota(jnp.int32, (1, nch), 1) % seq_chunks
    ns = SSM_STATE

    def scan(re, im, tab, forward):
        lam_col = jnp.transpose(tab[CHUNK:CHUNK + 8, :])[:, 0:1]
        a, b = lam_col[:ns], lam_col[ns:]
        n_steps = int(math.log2(seq_chunks))

        def rolled(v, d):
            if d % LANES == 0:
                return jnp.concatenate([v[:, nch - d:], v[:, :nch - d]], axis=1)
            return pltpu.roll(v, d, 1)

        def shifted(v, d):
            if forward:
                return jnp.where(lanec >= d, rolled(v, d), 0.0)
            return jnp.where(lanec < seq_chunks - d, rolled(v, nch - d), 0.0)

        for k in range(n_steps):
            sr, si = shifted(re, 1 << k), shifted(im, 1 << k)
            re, im = re + (sr * a - si * b), im + (sr * b + si * a)
            a, b = a * a - b * b, 2.0 * a * b
        return shifted(re, 1), shifted(im, 1)

    f_re, f_im = scan(xend[:ns], xend[ns:2 * ns], tab_f, True)
    b_re, b_im = scan(xend[2 * ns:3 * ns], xend[3 * ns:], tab_b, False)
    xin = jnp.concatenate([f_re, f_im, b_re, b_im], axis=0).astype(BF16)
    f_t = jnp.concatenate([ft_f, ft_b], axis=1).astype(BF16)

    toeplitz = jnp.concatenate(
        [g_t[:, CHUNK_W - LANES * th:2 * CHUNK_W - LANES * th] for th in range(CHUNK // T_SUB)],
        axis=0)
    y_t = _dot(toeplitz, u) + _dot(f_t, xin)
    for ct in range(n_ct):
        y_ref[ct] = y_t[:, ct * C_TILE:(ct + 1) * C_TILE].reshape(
            CHUNK, SSM_GROUP, C_TILE).astype(y_ref.dtype)


def _ssm(ut, a_re, a_im, log_dt, b_re, b_im, c_re, c_im, d, seq_chunks, w_gate, w_up, w_down, layer):
    g, n_ct = ut.shape[:2]
    assert N_EXPERTS % g == 0
    epg = N_EXPERTS // g
    base = layer * g
    n = SSM_STATE

    def per_group(a):
        return jnp.swapaxes(a, 0, 1)

    def dup(a):
        return jnp.concatenate([a, a], axis=-1)

    vecs = dup(jnp.stack([per_group(a_re), per_group(a_im),
                          jnp.broadcast_to(per_group(log_dt)[..., None], (g, 2, n))], axis=2))
    cr, ci = per_group(c_re), per_group(c_im)
    d_blk = jnp.broadcast_to(d.reshape(g, 1, SSM_GROUP, 1), (g, 2, SSM_GROUP, LANES))
    mats = jnp.concatenate(
        [dup(jnp.swapaxes(per_group(b_re), 2, 3)), dup(jnp.swapaxes(per_group(b_im), 2, 3)),
         jnp.concatenate([cr, -ci], axis=-1), dup(cr), dup(ci), d_blk], axis=2)
    exp_c = jnp.asarray(_expand_consts(), BF16)

    return pl.pallas_call(
        functools.partial(_ssm_kernel, seq_chunks=seq_chunks),
        grid=(g,),
        in_specs=[
            pl.BlockSpec((None, n_ct, CHUNK, SSM_GROUP, C_TILE), lambda i: (i, 0, 0, 0, 0)),
            pl.BlockSpec((None, 2, 3, LANES), lambda i: (i, 0, 0, 0)),
            pl.BlockSpec((None, 2, 6 * SSM_GROUP, LANES), lambda i: (i, 0, 0, 0)),
            pl.BlockSpec((4, CHUNK_W, LANES), lambda i: (0, 0, 0)),
            pl.BlockSpec((epg, D_MODEL, D_EXPERT), lambda i: (base + i, 0, 0)),
            pl.BlockSpec((epg, D_MODEL, D_EXPERT), lambda i: (base + i, 0, 0)),
            pl.BlockSpec((epg, D_EXPERT, D_MODEL), lambda i: (base + i, 0, 0)),
        ],
        out_specs=[
            pl.BlockSpec((None, n_ct, CHUNK, SSM_GROUP, C_TILE), lambda i: (i, 0, 0, 0, 0)),
            pl.BlockSpec((epg, D_MODEL, D_EXPERT), lambda i: (i, 0, 0)),
            pl.BlockSpec((epg, D_MODEL, D_EXPERT), lambda i: (i, 0, 0)),
            pl.BlockSpec((epg, D_EXPERT, D_MODEL), lambda i: (i, 0, 0)),
        ],
        out_shape=[
            jax.ShapeDtypeStruct(ut.shape, BF16),
            jax.ShapeDtypeStruct((N_EXPERTS, D_MODEL, D_EXPERT), BF16),
            jax.ShapeDtypeStruct((N_EXPERTS, D_MODEL, D_EXPERT), BF16),
            jax.ShapeDtypeStruct((N_EXPERTS, D_EXPERT, D_MODEL), BF16),
        ],
        compiler_params=_params(("parallel",)),
        name="ssm",
    )(ut, vecs, mats, exp_c, w_gate, w_up, w_down)


def _mix_out_kernel(x_hbm, a_ref, yt_ref, gwt_ref, gb_ref, wo_ref, h_hbm, xbuf, hbuf, xsem, hsem):
    nc = C_TILE
    x = _load_time_major_rows(x_hbm, xbuf, xsem)
    a = a_ref[...].reshape(S_TILE * nc, D_POOL)
    y = jnp.concatenate([yt_ref[:, j, :, :].reshape(D_SSM, nc) for j in range(S_TILE)],
                        axis=1).astype(F32)
    z = 0.5 * y * (1.0 + jnp.tanh(math.sqrt(2.0 / math.pi) * (y + 0.044715 * (y * y * y))))
    gate = _sigmoid(_dot(gwt_ref[...], z.astype(BF16)) + gb_ref[...])
    s = (z * gate).astype(BF16)
    h = x + _dot(a, wo_ref[:D_POOL, :]) + _dot_tn(s, wo_ref[D_POOL:, :])
    _store_time_major_rows(h, h_hbm, hbuf, hsem)


def _mix_out(x3, a3, yt, glu_w_t, glu_b_col, w_out):
    nch = x3.shape[0]
    return pl.pallas_call(
        _mix_out_kernel,
        grid=(nch // C_TILE, CHUNK // S_TILE),
        in_specs=[
            pl.BlockSpec(memory_space=pl.ANY),
            pl.BlockSpec((S_TILE, C_TILE, D_POOL), lambda c, t: (t, c, 0)),
            pl.BlockSpec((N_SSM_GROUPS, None, S_TILE, SSM_GROUP, C_TILE), lambda c, t: (0, c, t, 0, 0)),
            pl.BlockSpec((D_SSM, D_SSM), lambda c, t: (0, 0)),
            pl.BlockSpec((D_SSM, 1), lambda c, t: (0, 0)),
            pl.BlockSpec((D_MODEL, D_MODEL), lambda c, t: (0, 0)),
        ],
        out_specs=pl.BlockSpec(memory_space=pl.ANY),
        out_shape=jax.ShapeDtypeStruct((nch, CHUNK, D_MODEL), F32),
        scratch_shapes=[
            pltpu.VMEM((READ_RING, S_TILE, C_TILE, D_MODEL), F32),
            pltpu.VMEM((2, S_TILE, C_TILE, D_MODEL), F32),
            pltpu.SemaphoreType.DMA((READ_RING,)),
            pltpu.SemaphoreType.DMA((2,)),
        ],
        compiler_params=_params(("arbitrary", "arbitrary")),
        name="mix_out",
    )(x3, a3, yt, glu_w_t, glu_b_col, w_out)


def _pack_rows(x):
    b = lax.bitcast_convert_type(x.astype(BF16).astype(F32), U32)
    return (b[:, :HALF] & jnp.uint32(0xFFFF0000)) | (b[:, HALF:] >> 16)


def _unpack_rows(w):
    lo = lax.bitcast_convert_type(w & jnp.uint32(0xFFFF0000), F32)
    hi = lax.bitcast_convert_type(w << 16, F32)
    return lo, hi


def _split_bf16(x):
    hi = x.astype(BF16)
    return hi, (x - hi.astype(F32)).astype(BF16)


def _route(v32, wr_ref, br_ref):
    v_hi, v_lo = _split_bf16(v32)
    w_hi, w_lo = _split_bf16(wr_ref[...])
    both = _dot(v_hi, jnp.concatenate([w_hi, w_lo], axis=1))
    logits = both[:, :ROUTER_W] + (both[:, ROUTER_W:] + _dot(v_lo, w_hi)) + br_ref[...]
    return jnp.transpose(logits)


def _top1(x, valid=None):
    n = x.shape[0]
    row = lax.broadcasted_iota(jnp.int32, x.shape, 0).astype(F32)
    if valid is not None:
        x = jnp.where(valid, x, -jnp.inf)
    m = jnp.max(x, axis=0, keepdims=True)
    idx = jnp.min(jnp.where(x == m, row, float(n)), axis=0, keepdims=True)
    return m, idx, x, row


def _split_planes(packed, ref):
    ref[0] = packed[:, :SC_ROW]
    ref[1] = packed[:, SC_ROW:]


def _router_kernel(h_ref, g_ref, wr_ref, br_ref, before_ref, vp_ref, meta_t_ref, cnt_ref, carry_ref):
    @pl.when(pl.program_id(0) == 0)
    def _():
        carry_ref[...] = jnp.zeros_like(carry_ref)

    v32 = _rms(h_ref[...], g_ref[...])
    _split_planes(_pack_rows(v32), vp_ref)
    lt = _route(v32, wr_ref, br_ref)
    tm = lt.shape[1]
    eg = EXPERTS_PER_GROUP

    grp = lt[:eg]
    grp_row = lax.broadcasted_iota(jnp.int32, grp.shape, 0)
    mg, grp_idx, grp, _ = _top1(grp, grp_row < N_EXPERT_GROUPS)
    grp_p = 1.0 / jnp.sum(jnp.exp(grp - mg), axis=0, keepdims=True)
    le = jnp.zeros((eg, tm), F32)
    for g in range(N_EXPERT_GROUPS):
        le = jnp.where(grp_idx == float(g), lt[eg * (g + 1):eg * (g + 2)], le)
    m1, i1, le, row = _top1(le)
    z = jnp.sum(jnp.exp(le - m1), axis=0, keepdims=True)
    m2, i2, _, _ = _top1(jnp.where(row == i1, -jnp.inf, le))
    p1 = 1.0 / z
    p2 = jnp.exp(m2 - m1) / z
    tot = p1 + p2
    w1 = grp_p * (p1 / tot)
    w2 = grp_p * (p2 / tot)
    e1 = grp_idx * eg + i1
    e2 = grp_idx * eg + i2

    erow = lax.broadcasted_iota(jnp.int32, (N_EXPERTS, tm), 0).astype(F32)
    onehot = jnp.where(erow == e1, 1.0, jnp.where(erow == e2, 1.0, 0.0))
    before = _dot(onehot.astype(BF16), before_ref[...]) + carry_ref[...]
    rank1 = jnp.sum(jnp.where(erow == e1, before, 0.0), axis=0, keepdims=True)
    rank2 = jnp.sum(jnp.where(erow == e2, before, 0.0), axis=0, keepdims=True)
    carry = carry_ref[...] + jnp.sum(onehot, axis=1, keepdims=True)
    carry_ref[...] = carry
    cnt_ref[...] = carry

    mrow = lax.broadcasted_iota(jnp.int32, (META_ROWS, tm), 0)
    meta_t_ref[...] = jnp.where(mrow == 0, e1, jnp.where(mrow == 1, e2, jnp.where(
        mrow == 2, rank1, jnp.where(mrow == 3, rank2, jnp.where(
            mrow == 4, w1, jnp.where(mrow == 5, w2, 0.0))))))


def _earlier_matrix(tm):
    return np.triu(np.ones((tm, tm), np.float32), k=1)


def _router(h1, g_ffn, w_router, b_router, tm):
    t = h1.shape[0]
    return pl.pallas_call(
        _router_kernel,
        grid=(t // tm,),
        in_specs=[
            pl.BlockSpec((tm, D_MODEL), lambda i: (i, 0)),
            pl.BlockSpec((1, D_MODEL), lambda i: (0, 0)),
            pl.BlockSpec((D_MODEL, ROUTER_W), lambda i: (0, 0)),
            pl.BlockSpec((1, ROUTER_W), lambda i: (0, 0)),
            pl.BlockSpec((tm, tm), lambda i: (0, 0)),
        ],
        out_specs=[
            pl.BlockSpec((2, tm, SC_ROW), lambda i: (0, i, 0)),
            pl.BlockSpec((META_ROWS, tm), lambda i: (0, i)),
            pl.BlockSpec((N_EXPERTS, 1), lambda i: (0, 0)),
        ],
        out_shape=[
            jax.ShapeDtypeStruct((2, t, SC_ROW), U32),
            jax.ShapeDtypeStruct((META_ROWS, t), F32),
            jax.ShapeDtypeStruct((N_EXPERTS, 1), F32),
        ],
        scratch_shapes=[pltpu.VMEM((N_EXPERTS, 1), F32)],
        compiler_params=_params(("arbitrary",)),
        name="router",
    )(h1, g_ffn, w_router, b_router, jnp.asarray(_earlier_matrix(tm), BF16))


def _plan(meta_t, counts, n_tiles):
    e1 = meta_t[0].astype(jnp.int32)
    e2 = meta_t[1].astype(jnp.int32)
    rank1 = meta_t[2].astype(jnp.int32)
    rank2 = meta_t[3].astype(jnp.int32)
    cnt = counts[:, 0].astype(jnp.int32)
    padded = ((cnt + ROW_TILE - 1) // ROW_TILE) * ROW_TILE
    ends = jnp.cumsum(padded)
    starts = ends - padded
    experts = jnp.arange(N_EXPERTS, dtype=jnp.int32)
    pos1 = rank1 + jnp.sum(jnp.where(e1[None, :] == experts[:, None], starts[:, None], 0), axis=0)
    pos2 = rank2 + jnp.sum(jnp.where(e2[None, :] == experts[:, None], starts[:, None], 0), axis=0)
    tile_start = jnp.arange(n_tiles, dtype=jnp.int32) * ROW_TILE
    tile_expert = jnp.sum((tile_start[:, None] >= ends[None, :]).astype(jnp.int32), axis=1)
    tile_expert = jnp.minimum(tile_expert, N_EXPERTS - 1)
    rows_left = jnp.sum(jnp.where(tile_expert[:, None] == experts, cnt + starts, 0), axis=1) - tile_start
    n_valid = jnp.clip(rows_left, 0, ROW_TILE).astype(jnp.int32)
    last_used = jnp.maximum(ends[-1] // ROW_TILE - 1, 0)
    block = jnp.minimum(jnp.arange(n_tiles, dtype=jnp.int32), last_used)
    tile_expert = jnp.sum(jnp.where(block[:, None] == jnp.arange(n_tiles)[None, :],
                                    tile_expert[None, :], 0), axis=1)
    plane = n_tiles * ROW_TILE
    half_rows = jnp.concatenate([pos1, pos1 + plane, pos2, pos2 + plane])[None]
    return half_rows, tile_expert, n_valid, block


def _sc_mesh():
    return plsc.VectorSubcoreMesh(core_axis_name="c", subcore_axis_name="s")


def _sc_scatter_rows(rows, idx, n_out):
    t, width = rows.shape
    steps = t // SC_WINDOW
    half = steps // 2

    @pl.kernel(out_type=jax.ShapeDtypeStruct((n_out, width), rows.dtype), mesh=_sc_mesh(),
               scratch_types=[], name="moe_scatter")
    def scatter(rows_hbm, idx_hbm, out_hbm):
        def body(rows_vmem, idx0_vmem, idx1_vmem):
            pltpu.sync_copy(rows_vmem, out_hbm.at[idx0_vmem.at[0]])
            pltpu.sync_copy(rows_vmem, out_hbm.at[idx1_vmem.at[0]])

        pltpu.emit_pipeline(
            body,
            grid=(2, half),
            in_specs=[pl.BlockSpec((SC_WINDOW, width), lambda c, j: (c * half + j, 0)),
                      pl.BlockSpec((1, SC_WINDOW), lambda c, j: (0, c * half + j)),
                      pl.BlockSpec((1, SC_WINDOW), lambda c, j: (0, steps + c * half + j))],
            out_specs=[],
            core_axis_name=("c", "s"),
            dimension_semantics=(pltpu.PARALLEL, pltpu.PARALLEL),
        )(rows_hbm, idx_hbm, idx_hbm)

    return scatter(rows, idx)


def _sc_gather_rows(table, idx):
    m = idx.shape[1]
    width = table.shape[1]
    steps = m // (2 * SC_WINDOW)

    @pl.kernel(out_type=jax.ShapeDtypeStruct((m, width), table.dtype), mesh=_sc_mesh(),
               scratch_types=[], name="moe_gather")
    def gather(table_hbm, idx_hbm, out_hbm):
        def body(idx_vmem, out_vmem):
            pltpu.sync_copy(table_hbm.at[idx_vmem.at[0]], out_vmem)

        pltpu.emit_pipeline(
            body,
            grid=(2, steps),
            in_specs=[pl.BlockSpec((1, SC_WINDOW), lambda k, j: (0, k * steps + j))],
            out_specs=[pl.BlockSpec((SC_WINDOW, width), lambda k, j: (k * steps + j, 0))],
            core_axis_name=("c", "s"),
            dimension_semantics=(pltpu.PARALLEL, pltpu.PARALLEL),
        )(idx_hbm, out_hbm)

    return gather(table, idx)


def _experts_kernel(te_ref, nv_ref, blk_ref, xs_ref, wg_ref, wu_ref, wd_ref, ys_ref):
    r = pl.program_id(0)
    n_valid = nv_ref[r]

    @pl.when(n_valid > 0)
    def _():
        parts = [p.astype(BF16) for p in _unpack_rows(xs_ref[0]) + _unpack_rows(xs_ref[1])]
        cols = (0, 2 * SC_ROW, SC_ROW, 3 * SC_ROW)
        hg = sum(_dot(p, wg_ref[c:c + SC_ROW, :]) for p, c in zip(parts, cols))
        hu = sum(_dot(p, wu_ref[c:c + SC_ROW, :]) for p, c in zip(parts, cols))
        row = lax.broadcasted_iota(jnp.int32, (ROW_TILE, 1), 0)
        hid = jnp.where(row < n_valid, hg * _sigmoid(hg) * hu, 0.0).astype(BF16)
        _split_planes(_pack_rows(_dot(hid, wd_ref[...])), ys_ref)


def _experts(xs, tile_expert, n_valid, block, w_gate, w_up, w_down):
    n_tiles = xs.shape[1] // ROW_TILE
    w_spec = pl.BlockSpec((None, D_MODEL, D_EXPERT), lambda r, te, nv, blk: (te[r], 0, 0))
    grid_spec = pltpu.PrefetchScalarGridSpec(
        num_scalar_prefetch=3,
        grid=(n_tiles,),
        in_specs=[
            pl.BlockSpec((2, ROW_TILE, SC_ROW), lambda r, te, nv, blk: (0, blk[r], 0)),
            w_spec, w_spec,
            pl.BlockSpec((None, D_EXPERT, D_MODEL), lambda r, te, nv, blk: (te[r], 0, 0)),
        ],
        out_specs=pl.BlockSpec((2, ROW_TILE, SC_ROW), lambda r, te, nv, blk: (0, blk[r], 0)),
    )
    return pl.pallas_call(
        _experts_kernel,
        grid_spec=grid_spec,
        out_shape=jax.ShapeDtypeStruct((2, n_tiles * ROW_TILE, SC_ROW), U32),
        compiler_params=_params(("arbitrary",)),
        name="experts",
    )(tile_expert, n_valid, block, xs, w_gate, w_up, w_down)


def _ple_kernel(h_ref, yg_ref, meta_t_ref, p_ref, wg_ref, bg_ref, wp_ref, gf_ref, *rest, final_norm):
    o_ref = rest[-1]
    meta = jnp.transpose(meta_t_ref[...])
    w1 = meta[:, 4:5]
    w2 = meta[:, 5:6]
    q0, q2 = (w1 * u + w2 * v for u, v in zip(_unpack_rows(yg_ref[0]), _unpack_rows(yg_ref[2])))
    q1, q3 = (w1 * u + w2 * v for u, v in zip(_unpack_rows(yg_ref[1]), _unpack_rows(yg_ref[3])))
    moe = jnp.concatenate([q0, q1, q2, q3], axis=1)
    h = h_ref[...] + moe
    gate = _sigmoid(_dot(_rms(h).astype(BF16), wg_ref[...]) + bg_ref[...])
    h = h + gate * _dot(p_ref[...].astype(BF16), wp_ref[...])
    o_ref[...] = _rms(h, gf_ref[...]) if final_norm else h


def _ple(h1, yg, meta_t, p2, w_gate, b_gate, w_proj, g_final, final_norm, tm, off, prev_out):
    t = h1.shape[0]
    steps = yg.shape[1] // tm
    in_specs = [
        pl.BlockSpec((tm, D_MODEL), lambda i: (i + off, 0)),
        pl.BlockSpec((4, tm, SC_ROW), lambda i: (0, i, 0)),
        pl.BlockSpec((META_ROWS, tm), lambda i: (0, i + off)),
        pl.BlockSpec((tm, D_PLE), lambda i: (i + off, 0)),
        pl.BlockSpec((D_MODEL, D_MODEL), lambda i: (0, 0)),
        pl.BlockSpec((1, D_MODEL), lambda i: (0, 0)),
        pl.BlockSpec((D_PLE, D_MODEL), lambda i: (0, 0)),
        pl.BlockSpec((1, D_MODEL), lambda i: (0, 0)),
    ]
    args = [h1, yg, meta_t, p2, w_gate, b_gate, w_proj, g_final]
    aliases = {}
    if prev_out is not None:
        in_specs.append(pl.BlockSpec(memory_space=pl.ANY))
        args.append(prev_out)
        aliases = {len(args) - 1: 0}
    return pl.pallas_call(
        functools.partial(_ple_kernel, final_norm=final_norm),
        grid=(steps,),
        in_specs=in_specs,
        out_specs=pl.BlockSpec((tm, D_MODEL), lambda i: (i + off, 0)),
        out_shape=jax.ShapeDtypeStruct((t, D_MODEL), F32),
        input_output_aliases=aliases,
        compiler_params=_params(("parallel",)),
        name="ple",
    )(*args)


def kernel(x, p, g_mix, w_in, pool_w, pool_scale, ssm_a_re, ssm_a_im, ssm_log_dt, ssm_b_re,
           ssm_b_im, ssm_c_re, ssm_c_im, ssm_d, glu_w, glu_b, w_out, g_ffn, router_grp_w,
           router_grp_b, router_exp_w, router_exp_b, exp_w_gate, exp_w_up, exp_w_down, g_ple,
           ple_gate_w, ple_gate_b, ple_proj_w, g_final):
    bsz, seq, dm = x.shape
    depth = g_mix.shape[0]
    t = bsz * seq
    seq_chunks = seq // CHUNK
    nch = t // CHUNK
    tm = 1024
    n_sorted = (pl.cdiv(2 * t, ROW_TILE) + N_EXPERTS) * ROW_TILE
    w_gate_all = exp_w_gate.reshape(depth * N_EXPERTS, dm, D_EXPERT)
    w_up_all = exp_w_up.reshape(depth * N_EXPERTS, dm, D_EXPERT)
    w_down_all = exp_w_down.reshape(depth * N_EXPERTS, D_EXPERT, dm)

    h = x.reshape(t, dm)
    for i in range(depth):
        w_in_b = (g_mix[i][:, None] * w_in[i]).astype(BF16)
        zp, ut = _in_proj(h.reshape(nch, CHUNK, dm), w_in_b[:, :D_POOL],
                          jnp.transpose(w_in_b[:, D_POOL:]))
        a = _pool(zp, pool_w[i], pool_scale[i][None], bsz)
        yt, w_gate_b, w_up_b, w_down_b = _ssm(
            ut, ssm_a_re[i], ssm_a_im[i], ssm_log_dt[i], ssm_b_re[i], ssm_b_im[i], ssm_c_re[i],
            ssm_c_im[i], ssm_d[i], seq_chunks, w_gate_all, w_up_all, w_down_all, i)
        h = _mix_out(h.reshape(nch, CHUNK, dm), a, yt,
                     jnp.transpose(glu_w[i]).astype(BF16), glu_b[i][:, None],
                     w_out[i].astype(BF16)).reshape(t, dm)

        eg = EXPERTS_PER_GROUP
        w_router = jnp.concatenate(
            [router_grp_w[i], jnp.zeros((dm, eg - N_EXPERT_GROUPS), F32),
             jnp.transpose(router_exp_w[i], (1, 0, 2)).reshape(dm, N_EXPERTS),
             jnp.zeros((dm, ROUTER_W - eg - N_EXPERTS), F32)], axis=1)
        b_router = jnp.concatenate(
            [router_grp_b[i], jnp.zeros((eg - N_EXPERT_GROUPS,), F32),
             router_exp_b[i].reshape(N_EXPERTS),
             jnp.zeros((ROUTER_W - eg - N_EXPERTS,), F32)])[None]
        vp, meta_t, counts = _router(h, g_ffn[i][None], w_router, b_router, tm)
        idx, tile_expert, n_valid, block = _plan(meta_t, counts, n_sorted // ROW_TILE)
        xs = _sc_scatter_rows(vp.reshape(2 * t, SC_ROW), idx, 2 * n_sorted)
        ys = _experts(xs.reshape(2, n_sorted, SC_ROW), tile_expert, n_valid, block,
                      w_gate_b, w_up_b, w_down_b)
        ys2 = ys.reshape(2 * n_sorted, SC_ROW)
        idx4 = idx.reshape(4, t)
        ple_wg = (g_ple[i][:, None] * ple_gate_w[i]).astype(BF16)
        ple_wp = ple_proj_w[i].astype(BF16)
        out = None
        cuts = (0, t // FIRST_PART, t)
        for lo, hi in zip(cuts[:-1], cuts[1:]):
            yg_q = _sc_gather_rows(ys2, idx4[:, lo:hi].reshape(1, 4 * (hi - lo)))
            out = _ple(h, yg_q.reshape(4, hi - lo, SC_ROW), meta_t, p[i].reshape(t, D_PLE), ple_wg,
                       ple_gate_b[i][None], ple_wp, g_final[None], i == depth - 1, tm, lo // tm, out)
        h = out
    return h.reshape(bsz, seq, dm)
```
